```python
import math
import jax, jax.numpy as jnp
from jax import lax
import numpy as np

D_MODEL = 1024
BATCH = 8
SEQ = 8192
DEPTH = 2

D_FF = 2816
N_EVEN = (DEPTH + 1) // 2
N_ODD = DEPTH // 2
S5_WIDTH = D_MODEL // 2
S5_GROUP = 16
S5_GROUPS = S5_WIDTH // S5_GROUP
S5_STATE = 64
GLA_HEADS = 4
GLA_DK = D_MODEL // 4 // GLA_HEADS
GLA_DV = D_MODEL // 2 // GLA_HEADS
GLA_RANK = 16
GLA_GATE_NORM = 16.0
RET_HEADS = 8
RET_DK = D_MODEL // RET_HEADS
RET_DV = 2 * D_MODEL // RET_HEADS
ROPE_BASE = 10000.0
CHUNK = 64
EPS = 1e-6

AB_IN = S5_WIDTH + 2 * GLA_HEADS * GLA_DK + 2 * GLA_HEADS * GLA_DV + 2 * GLA_RANK
AB_OUT = S5_WIDTH + GLA_HEADS * GLA_DV
RET_IN = 2 * RET_HEADS * RET_DK + 2 * RET_HEADS * RET_DV
RET_OUT = RET_HEADS * RET_DV

kernel_name = 'hybrid_s5_gla_retention_macaron_encoder'


def rmsnorm(x, g):
    xf = x.astype(jnp.float32)
    y = xf * lax.rsqrt(jnp.mean(xf * xf, axis=-1, keepdims=True) + EPS)
    return (y * g.astype(jnp.float32)).astype(x.dtype)


def head_rmsnorm(o, g):
    y = o * lax.rsqrt(jnp.mean(o * o, axis=-1, keepdims=True) + EPS)
    return y * g.astype(jnp.float32).reshape(o.shape[-2], o.shape[-1])


def swiglu_ffn(h, w1, w2):
    gate, up = jnp.split(h @ w1, 2, axis=-1)
    return (jax.nn.silu(gate) * up) @ w2


def to_heads(t, n_heads):
    b, l, _ = t.shape
    return t.reshape(b, l, n_heads, -1).transpose(0, 2, 1, 3)


def flip_seq(t):
    return jnp.flip(t, axis=2)


def _complex_affine_combine(e1, e2):
    a1r, a1i, b1r, b1i = e1
    a2r, a2i, b2r, b2i = e2
    return (a2r * a1r - a2i * a1i,
            a2r * a1i + a2i * a1r,
            a2r * b1r - a2i * b1i + b2r,
            a2r * b1i + a2i * b1r + b2i)


def s5_bidirectional(u, lam_re, lam_im, b_re, b_im, c_re, c_im, log_dt, d_skip):
    f32 = jnp.float32
    uf = u.astype(f32)
    seq_len = u.shape[1]
    y = uf * d_skip.astype(f32).reshape(S5_GROUPS, S5_GROUP)
    for direction in range(2):
        lr = jnp.minimum(lam_re[direction].astype(f32), -1e-4)
        li = lam_im[direction].astype(f32)
        dt = jnp.exp(log_dt[direction].astype(f32))[:, None]
        mag = jnp.exp(lr * dt)
        ar = mag * jnp.cos(li * dt)
        ai = mag * jnp.sin(li * dt)
        den = lr * lr + li * li
        cr = ((ar - 1.0) * lr + ai * li) / den
        ci = (ai * lr - (ar - 1.0) * li) / den
        br = b_re[direction].astype(f32)
        bi = b_im[direction].astype(f32)
        bbr = cr[..., None] * br - ci[..., None] * bi
        bbi = cr[..., None] * bi + ci[..., None] * br
        bu_r = jnp.einsum('blgp,gnp->blgn', uf, bbr)
        bu_i = jnp.einsum('blgp,gnp->blgn', uf, bbi)
        a_r = jnp.broadcast_to(ar[None, None], (1, seq_len) + ar.shape)
        a_i = jnp.broadcast_to(ai[None, None], (1, seq_len) + ai.shape)
        _, _, xr, xi = lax.associative_scan(_complex_affine_combine, (a_r, a_i, bu_r, bu_i),
                                            reverse=(direction == 1), axis=1)
        y = y + jnp.einsum('blgn,gpn->blgp', xr, c_re[direction].astype(f32)) \
              - jnp.einsum('blgn,gpn->blgp', xi, c_im[direction].astype(f32))
    return y


def gla_chunked(q, k, v, g, strict):
    b_, h_, l_, dk = q.shape
    dv = v.shape[-1]
    n = l_ // CHUNK
    q = q.reshape(b_, h_, n, CHUNK, dk)
    k = k.reshape(b_, h_, n, CHUNK, dk)
    v = v.reshape(b_, h_, n, CHUNK, dv)
    g = g.reshape(b_, h_, n, CHUNK, dk)
    cum = jnp.cumsum(g, axis=3)
    q_dec = q * jnp.exp(cum)
    k_inv = k * jnp.exp(-cum)
    mask = jnp.tril(jnp.ones((CHUNK, CHUNK), dtype=bool), k=-1 if strict else 0)
    scores = jnp.where(mask, jnp.einsum('bhnid,bhnjd->bhnij', q_dec, k_inv), 0.0)
    o_intra = jnp.einsum('bhnij,bhnjv->bhniv', scores, v)
    last = cum[:, :, :, -1:, :]
    kv = jnp.einsum('bhnjd,bhnjv->bhndv', k * jnp.exp(last - cum), v)
    chunk_decay = jnp.exp(last[:, :, :, 0, :])

    def step(state, inp):
        dec, kv_c = inp
        return dec[..., None] * state + kv_c, state

    _, s_before = lax.scan(step, jnp.zeros((b_, h_, dk, dv), jnp.float32),
                           (jnp.moveaxis(chunk_decay, 2, 0), jnp.moveaxis(kv, 2, 0)))
    s_before = jnp.moveaxis(s_before, 0, 2)
    o_inter = jnp.einsum('bhnid,bhndv->bhniv', q_dec, s_before)
    return (o_intra + o_inter).reshape(b_, h_, l_, dv)


def retention_chunkwise(q, k, v, log_gamma, strict):
    b_, h_, l_, dk = q.shape
    dv = v.shape[-1]
    n = l_ // CHUNK
    q = q.reshape(b_, h_, n, CHUNK, dk)
    k = k.reshape(b_, h_, n, CHUNK, dk)
    v = v.reshape(b_, h_, n, CHUNK, dv)
    idx_i = jnp.arange(CHUNK)
    diff_i = idx_i[:, None] - idx_i[None, :]
    mask = diff_i >= (1 if strict else 0)
    idx = idx_i.astype(jnp.float32)
    diff = jnp.maximum(diff_i, 0).astype(jnp.float32)
    decay_mat = jnp.where(mask[None], jnp.exp(diff[None] * log_gamma[:, None, None]), 0.0)
    scores = jnp.einsum('bhnid,bhnjd->bhnij', q, k) * decay_mat[None, :, None]
    o_intra = jnp.einsum('bhnij,bhnjv->bhniv', scores, v)
    zeta = jnp.exp((CHUNK - 1.0 - idx)[None, :] * log_gamma[:, None])
    xi = jnp.exp((idx + 1.0)[None, :] * log_gamma[:, None])
    kv = jnp.einsum('bhnjd,bhnjv,hj->bhndv', k, v, zeta)
    chunk_decay = jnp.exp(CHUNK * log_gamma)

    def step(state, kv_c):
        return chunk_decay[None, :, None, None] * state + kv_c, state

    _, r_before = lax.scan(step, jnp.zeros((b_, h_, dk, dv), jnp.float32), jnp.moveaxis(kv, 2, 0))
    r_before = jnp.moveaxis(r_before, 0, 2)
    o_inter = jnp.einsum('bhnid,bhndv,hi->bhniv', q, r_before, xi)
    return (o_intra + o_inter).reshape(b_, h_, l_, dv)


def rotary(t):
    dk = t.shape[-1]
    half = dk // 2
    pos = jnp.arange(t.shape[2], dtype=jnp.float32)
    inv = jnp.exp(-math.log(ROPE_BASE) * jnp.arange(half, dtype=jnp.float32) / half)
    ang = pos[:, None] * inv[None, :]
    cos, sin = jnp.cos(ang), jnp.sin(ang)
    t1, t2 = t[..., :half], t[..., half:]
    return jnp.concatenate([t1 * cos - t2 * sin, t1 * sin + t2 * cos], axis=-1)


def s5_gla_mixer(h, w_in, lam_re, lam_im, b_re, b_im, c_re, c_im, log_dt, d_skip, w_glu,
                 w_gk, b_gk, gla_norm, w_out):
    b_, l_, _ = h.shape
    hk = GLA_HEADS * GLA_DK
    hv = GLA_HEADS * GLA_DV
    proj = h @ w_in
    cuts = [S5_WIDTH, S5_WIDTH + hk, S5_WIDTH + 2 * hk, S5_WIDTH + 2 * hk + hv, S5_WIDTH + 2 * hk + 2 * hv]
    u, q, k, v, og, glo = jnp.split(proj, cuts, axis=-1)
    y = s5_bidirectional(u.reshape(b_, l_, S5_GROUPS, S5_GROUP), lam_re, lam_im, b_re, b_im,
                         c_re, c_im, log_dt, d_skip).reshape(b_, l_, S5_WIDTH)
    gy = jax.nn.gelu(y).astype(h.dtype)
    s5_out = gy * jax.nn.sigmoid(gy @ w_glu)
    glo = glo.reshape(b_, l_, 2, GLA_RANK)
    gk = jnp.einsum('blsr,srk->blsk', glo, w_gk) + b_gk
    gk = jax.nn.log_sigmoid(gk.astype(jnp.float32)) / GLA_GATE_NORM
    qh = to_heads(q, GLA_HEADS).astype(jnp.float32) * GLA_DK ** -0.5
    kh = to_heads(k, GLA_HEADS).astype(jnp.float32)
    vh = to_heads(v, GLA_HEADS).astype(jnp.float32)
    gf = to_heads(gk[:, :, 0], GLA_HEADS)
    gb = to_heads(gk[:, :, 1], GLA_HEADS)
    o_f = gla_chunked(qh, kh, vh, gf, strict=False)
    o_b = flip_seq(gla_chunked(flip_seq(qh), flip_seq(kh), flip_seq(vh), flip_seq(gb), strict=True))
    o = head_rmsnorm((o_f + o_b).transpose(0, 2, 1, 3), gla_norm).reshape(b_, l_, hv)
    gla_out = o.astype(h.dtype) * jax.nn.silu(og)
    return jnp.concatenate([s5_out, gla_out], axis=-1) @ w_out


def retention_mixer(h, w_in, ret_norm, w_out):
    b_, l_, _ = h.shape
    hk = RET_HEADS * RET_DK
    hv = RET_HEADS * RET_DV
    q, k, v, og = jnp.split(h @ w_in, [hk, 2 * hk, 2 * hk + hv], axis=-1)
    qh = rotary(to_heads(q, RET_HEADS).astype(jnp.float32))
    kh = rotary(to_heads(k, RET_HEADS).astype(jnp.float32)) * RET_DK ** -0.5
    vh = to_heads(v, RET_HEADS).astype(jnp.float32)
    lg_f = jnp.log1p(-jnp.exp2(-5.0 - jnp.arange(RET_HEADS, dtype=jnp.float32)))
    lg_b = lg_f[::-1]
    o_f = retention_chunkwise(qh, kh, vh, lg_f, strict=False)
    o_b = flip_seq(retention_chunkwise(flip_seq(qh), flip_seq(kh), flip_seq(vh), lg_b, strict=True))
    o = head_rmsnorm((o_f + o_b).transpose(0, 2, 1, 3), ret_norm).reshape(b_, l_, hv)
    return (o.astype(h.dtype) * jax.nn.silu(og)) @ w_out


def _fwd_setup_inputs(seed: int = 0) -> dict:
    key = jax.random.key(seed)
    ks = jax.random.split(key, 32)
    f32 = jnp.float32

    def nrm(k, shape, scale):
        return jax.random.normal(k, shape, f32) * scale

    def gain(k, shape):
        return 1.0 + 0.02 * jax.random.normal(k, shape, f32)

    n_idx = jnp.arange(S5_STATE, dtype=f32)
    return {
        'x': nrm(ks[0], (BATCH, SEQ, D_MODEL), 1.0),
        'ffn1_norm': gain(ks[1], (DEPTH, D_MODEL)),
        'ffn1_w1': nrm(ks[2], (DEPTH, D_MODEL, 2 * D_FF), D_MODEL ** -0.5),
        'ffn1_w2': nrm(ks[3], (DEPTH, D_FF, D_MODEL), D_FF ** -0.5),
        'mix_norm': gain(ks[4], (DEPTH, D_MODEL)),
        'ffn2_norm': gain(ks[5], (DEPTH, D_MODEL)),
        'ffn2_w1': nrm(ks[6], (DEPTH, D_MODEL, 2 * D_FF), D_MODEL ** -0.5),
        'ffn2_w2': nrm(ks[7], (DEPTH, D_FF, D_MODEL), D_FF ** -0.5),
        'ab_w_in': nrm(ks[8], (N_EVEN, D_MODEL, AB_IN), D_MODEL ** -0.5),
        's5_lambda_re': -0.5 + 0.01 * jax.random.normal(ks[9], (N_EVEN, 2, S5_GROUPS, S5_STATE), f32),
        's5_lambda_im': math.pi * n_idx + 0.01 * jax.random.normal(ks[10], (N_EVEN, 2, S5_GROUPS, S5_STATE), f32),
        's5_b_re': nrm(ks[11], (N_EVEN, 2, S5_GROUPS, S5_STATE, S5_GROUP), (2 * S5_GROUP) ** -0.5),
        's5_b_im': nrm(ks[12], (N_EVEN, 2, S5_GROUPS, S5_STATE, S5_GROUP), (2 * S5_GROUP) ** -0.5),
        's5_c_re': nrm(ks[13], (N_EVEN, 2, S5_GROUPS, S5_GROUP, S5_STATE), (2 * S5_STATE) ** -0.5),
        's5_c_im': nrm(ks[14], (N_EVEN, 2, S5_GROUPS, S5_GROUP, S5_STATE), (2 * S5_STATE) ** -0.5),
        's5_log_dt': jax.random.uniform(ks[15], (N_EVEN, 2, S5_GROUPS), f32, math.log(1e-3), math.log(1e-1)),
        's5_d': nrm(ks[16], (N_EVEN, S5_WIDTH), 1.0),
        's5_w_glu': nrm(ks[17], (N_EVEN, S5_WIDTH, S5_WIDTH), S5_WIDTH ** -0.5),
        'gla_w_gk': nrm(ks[18], (N_EVEN, 2, GLA_RANK, GLA_HEADS * GLA_DK), GLA_RANK ** -0.5),
        'gla_b_gk': nrm(ks[19], (N_EVEN, 2, GLA_HEADS * GLA_DK), 0.1),
        'gla_norm': gain(ks[20], (N_EVEN, GLA_HEADS * GLA_DV)),
        'ab_w_out': nrm(ks[21], (N_EVEN, AB_OUT, D_MODEL), AB_OUT ** -0.5),
        'ret_w_in': nrm(ks[22], (N_ODD, D_MODEL, RET_IN), D_MODEL ** -0.5),
        'ret_norm': gain(ks[23], (N_ODD, RET_OUT)),
        'ret_w_out': nrm(ks[24], (N_ODD, RET_OUT, D_MODEL), RET_OUT ** -0.5),
        'final_norm': gain(ks[25], (D_MODEL,)),
    }


def _fwd_reference(x, ffn1_norm, ffn1_w1, ffn1_w2, mix_norm, ffn2_norm, ffn2_w1, ffn2_w2,
              ab_w_in, s5_lambda_re, s5_lambda_im, s5_b_re, s5_b_im, s5_c_re, s5_c_im,
              s5_log_dt, s5_d, s5_w_glu, gla_w_gk, gla_b_gk, gla_norm, ab_w_out,
              ret_w_in, ret_norm, ret_w_out, final_norm):
    for i in range(DEPTH):
        j = i // 2
        x = x + 0.5 * swiglu_ffn(rmsnorm(x, ffn1_norm[i]), ffn1_w1[i], ffn1_w2[i])
        h = rmsnorm(x, mix_norm[i])
        if i % 2 == 0:
            x = x + s5_gla_mixer(h, ab_w_in[j], s5_lambda_re[j], s5_lambda_im[j], s5_b_re[j], s5_b_im[j],
                                 s5_c_re[j], s5_c_im[j], s5_log_dt[j], s5_d[j], s5_w_glu[j],
                                 gla_w_gk[j], gla_b_gk[j], gla_norm[j], ab_w_out[j])
        else:
            x = x + retention_mixer(h, ret_w_in[j], ret_norm[j], ret_w_out[j])
        x = x + 0.5 * swiglu_ffn(rmsnorm(x, ffn2_norm[i]), ffn2_w1[i], ffn2_w2[i])
    return rmsnorm(x, final_norm)


import jax as _jax
import jax.numpy as _jnp

TWIN_FORMAT = 'train_step'
FWD_PARAMS = ['x', 'ffn1_norm', 'ffn1_w1', 'ffn1_w2', 'mix_norm', 'ffn2_norm', 'ffn2_w1', 'ffn2_w2', 'ab_w_in', 's5_lambda_re', 's5_lambda_im', 's5_b_re', 's5_b_im', 's5_c_re', 's5_c_im', 's5_log_dt', 's5_d', 's5_w_glu', 'gla_w_gk', 'gla_b_gk', 'gla_norm', 'ab_w_out', 'ret_w_in', 'ret_norm', 'ret_w_out', 'final_norm']
TWIN_WEIGHTS = ['ffn1_norm', 'ffn1_w1', 'ffn1_w2', 'mix_norm', 'ffn2_norm', 'ffn2_w1', 'ffn2_w2', 'ab_w_in', 's5_lambda_re', 's5_lambda_im', 's5_b_re', 's5_b_im', 's5_c_re', 's5_c_im', 's5_log_dt', 's5_d', 's5_w_glu', 'gla_w_gk', 'gla_b_gk', 'gla_norm', 'ab_w_out', 'ret_w_in', 'ret_norm', 'ret_w_out', 'final_norm']
TWIN_DIFF_INPUT = 'x'
TWIN_INPUTS = ['x', 'ffn1_norm', 'ffn1_w1', 'ffn1_w2', 'mix_norm', 'ffn2_norm', 'ffn2_w1', 'ffn2_w2', 'ab_w_in', 's5_lambda_re', 's5_lambda_im', 's5_b_re', 's5_b_im', 's5_c_re', 's5_c_im', 's5_log_dt', 's5_d', 's5_w_glu', 'gla_w_gk', 'gla_b_gk', 'gla_norm', 'ab_w_out', 'ret_w_in', 'ret_norm', 'ret_w_out', 'final_norm', 'loss_target', 'm_ffn1_norm', 'm_ffn1_w1', 'm_ffn1_w2', 'm_mix_norm', 'm_ffn2_norm', 'm_ffn2_w1', 'm_ffn2_w2', 'm_ab_w_in', 'm_s5_lambda_re', 'm_s5_lambda_im', 'm_s5_b_re', 'm_s5_b_im', 'm_s5_c_re', 'm_s5_c_im', 'm_s5_log_dt', 'm_s5_d', 'm_s5_w_glu', 'm_gla_w_gk', 'm_gla_b_gk', 'm_gla_norm', 'm_ab_w_out', 'm_ret_w_in', 'm_ret_norm', 'm_ret_w_out', 'm_final_norm', 'v_ffn1_norm', 'v_ffn1_w1', 'v_ffn1_w2', 'v_mix_norm', 'v_ffn2_norm', 'v_ffn2_w1', 'v_ffn2_w2', 'v_ab_w_in', 'v_s5_lambda_re', 'v_s5_lambda_im', 'v_s5_b_re', 'v_s5_b_im', 'v_s5_c_re', 'v_s5_c_im', 'v_s5_log_dt', 'v_s5_d', 'v_s5_w_glu', 'v_gla_w_gk', 'v_gla_b_gk', 'v_gla_norm', 'v_ab_w_out', 'v_ret_w_in', 'v_ret_norm', 'v_ret_w_out', 'v_final_norm']
TWIN_OUTPUTS = ['loss', 'grad_x', 'grad_ffn1_norm', 'grad_ffn1_w1', 'grad_ffn1_w2', 'grad_mix_norm', 'grad_ffn2_norm', 'grad_ffn2_w1', 'grad_ffn2_w2', 'grad_ab_w_in', 'grad_s5_lambda_re', 'grad_s5_lambda_im', 'grad_s5_b_re', 'grad_s5_b_im', 'grad_s5_c_re', 'grad_s5_c_im', 'grad_s5_log_dt', 'grad_s5_d', 'grad_s5_w_glu', 'grad_gla_w_gk', 'grad_gla_b_gk', 'grad_gla_norm', 'grad_ab_w_out', 'grad_ret_w_in', 'grad_ret_norm', 'grad_ret_w_out', 'grad_final_norm', 'delta_ffn1_norm', 'delta_ffn1_w1', 'delta_ffn1_w2', 'delta_mix_norm', 'delta_ffn2_norm', 'delta_ffn2_w1', 'delta_ffn2_w2', 'delta_ab_w_in', 'delta_s5_lambda_re', 'delta_s5_lambda_im', 'delta_s5_b_re', 'delta_s5_b_im', 'delta_s5_c_re', 'delta_s5_c_im', 'delta_s5_log_dt', 'delta_s5_d', 'delta_s5_w_glu', 'delta_gla_w_gk', 'delta_gla_b_gk', 'delta_gla_norm', 'delta_ab_w_out', 'delta_ret_w_in', 'delta_ret_norm', 'delta_ret_w_out', 'delta_final_norm', 'new_m_ffn1_norm', 'new_m_ffn1_w1', 'new_m_ffn1_w2', 'new_m_mix_norm', 'new_m_ffn2_norm', 'new_m_ffn2_w1', 'new_m_ffn2_w2', 'new_m_ab_w_in', 'new_m_s5_lambda_re', 'new_m_s5_lambda_im', 'new_m_s5_b_re', 'new_m_s5_b_im', 'new_m_s5_c_re', 'new_m_s5_c_im', 'new_m_s5_log_dt', 'new_m_s5_d', 'new_m_s5_w_glu', 'new_m_gla_w_gk', 'new_m_gla_b_gk', 'new_m_gla_norm', 'new_m_ab_w_out', 'new_m_ret_w_in', 'new_m_ret_norm', 'new_m_ret_w_out', 'new_m_final_norm', 'new_v_ffn1_norm', 'new_v_ffn1_w1', 'new_v_ffn1_w2', 'new_v_mix_norm', 'new_v_ffn2_norm', 'new_v_ffn2_w1', 'new_v_ffn2_w2', 'new_v_ab_w_in', 'new_v_s5_lambda_re', 'new_v_s5_lambda_im', 'new_v_s5_b_re', 'new_v_s5_b_im', 'new_v_s5_c_re', 'new_v_s5_c_im', 'new_v_s5_log_dt', 'new_v_s5_d', 'new_v_s5_w_glu', 'new_v_gla_w_gk', 'new_v_gla_b_gk', 'new_v_gla_norm', 'new_v_ab_w_out', 'new_v_ret_w_in', 'new_v_ret_norm', 'new_v_ret_w_out', 'new_v_final_norm']
TWIN_LEAF_KINDS = {'loss': 'loss', 'grad_x': 'grad_x', 'grad_ffn1_norm': 'grad_w', 'grad_ffn1_w1': 'grad_w', 'grad_ffn1_w2': 'grad_w', 'grad_mix_norm': 'grad_w', 'grad_ffn2_norm': 'grad_w', 'grad_ffn2_w1': 'grad_w', 'grad_ffn2_w2': 'grad_w', 'grad_ab_w_in': 'grad_w', 'grad_s5_lambda_re': 'grad_w', 'grad_s5_lambda_im': 'grad_w', 'grad_s5_b_re': 'grad_w', 'grad_s5_b_im': 'grad_w', 'grad_s5_c_re': 'grad_w', 'grad_s5_c_im': 'grad_w', 'grad_s5_log_dt': 'grad_w', 'grad_s5_d': 'grad_w', 'grad_s5_w_glu': 'grad_w', 'grad_gla_w_gk': 'grad_w', 'grad_gla_b_gk': 'grad_w', 'grad_gla_norm': 'grad_w', 'grad_ab_w_out': 'grad_w', 'grad_ret_w_in': 'grad_w', 'grad_ret_norm': 'grad_w', 'grad_ret_w_out': 'grad_w', 'grad_final_norm': 'grad_w', 'delta_ffn1_norm': 'delta_w', 'delta_ffn1_w1': 'delta_w', 'delta_ffn1_w2': 'delta_w', 'delta_mix_norm': 'delta_w', 'delta_ffn2_norm': 'delta_w', 'delta_ffn2_w1': 'delta_w', 'delta_ffn2_w2': 'delta_w', 'delta_ab_w_in': 'delta_w', 'delta_s5_lambda_re': 'delta_w', 'delta_s5_lambda_im': 'delta_w', 'delta_s5_b_re': 'delta_w', 'delta_s5_b_im': 'delta_w', 'delta_s5_c_re': 'delta_w', 'delta_s5_c_im': 'delta_w', 'delta_s5_log_dt': 'delta_w', 'delta_s5_d': 'delta_w', 'delta_s5_w_glu': 'delta_w', 'delta_gla_w_gk': 'delta_w', 'delta_gla_b_gk': 'delta_w', 'delta_gla_norm': 'delta_w', 'delta_ab_w_out': 'delta_w', 'delta_ret_w_in': 'delta_w', 'delta_ret_norm': 'delta_w', 'delta_ret_w_out': 'delta_w', 'delta_final_norm': 'delta_w', 'new_m_ffn1_norm': 'new_m', 'new_m_ffn1_w1': 'new_m', 'new_m_ffn1_w2': 'new_m', 'new_m_mix_norm': 'new_m', 'new_m_ffn2_norm': 'new_m', 'new_m_ffn2_w1': 'new_m', 'new_m_ffn2_w2': 'new_m', 'new_m_ab_w_in': 'new_m', 'new_m_s5_lambda_re': 'new_m', 'new_m_s5_lambda_im': 'new_m', 'new_m_s5_b_re': 'new_m', 'new_m_s5_b_im': 'new_m', 'new_m_s5_c_re': 'new_m', 'new_m_s5_c_im': 'new_m', 'new_m_s5_log_dt': 'new_m', 'new_m_s5_d': 'new_m', 'new_m_s5_w_glu': 'new_m', 'new_m_gla_w_gk': 'new_m', 'new_m_gla_b_gk': 'new_m', 'new_m_gla_norm': 'new_m', 'new_m_ab_w_out': 'new_m', 'new_m_ret_w_in': 'new_m', 'new_m_ret_norm': 'new_m', 'new_m_ret_w_out': 'new_m', 'new_m_final_norm': 'new_m', 'new_v_ffn1_norm': 'new_v', 'new_v_ffn1_w1': 'new_v', 'new_v_ffn1_w2': 'new_v', 'new_v_mix_norm': 'new_v', 'new_v_ffn2_norm': 'new_v', 'new_v_ffn2_w1': 'new_v', 'new_v_ffn2_w2': 'new_v', 'new_v_ab_w_in': 'new_v', 'new_v_s5_lambda_re': 'new_v', 'new_v_s5_lambda_im': 'new_v', 'new_v_s5_b_re': 'new_v', 'new_v_s5_b_im': 'new_v', 'new_v_s5_c_re': 'new_v', 'new_v_s5_c_im': 'new_v', 'new_v_s5_log_dt': 'new_v', 'new_v_s5_d': 'new_v', 'new_v_s5_w_glu': 'new_v', 'new_v_gla_w_gk': 'new_v', 'new_v_gla_b_gk': 'new_v', 'new_v_gla_norm': 'new_v', 'new_v_ab_w_out': 'new_v', 'new_v_ret_w_in': 'new_v', 'new_v_ret_norm': 'new_v', 'new_v_ret_w_out': 'new_v', 'new_v_final_norm': 'new_v'}


def _forward(args):
    return _fwd_reference(*[args[k] for k in FWD_PARAMS])


def _output_shape():
    out = _jax.eval_shape(lambda: _forward(_fwd_setup_inputs(0)))
    return out.shape, out.dtype

N_MICROBATCH = 1
ADAM_LR = 0.001
ADAM_B1 = 0.9
ADAM_B2 = 0.999
ADAM_EPS = 1e-08
ADAM_WD = 0.01
ADAM_STEP = 10
PER_EXAMPLE_BATCH_AXIS = {'x': 0, 'loss_target': 0}
SHARED_INPUTS = []
_WEIGHT_DTYPES = {'ffn1_norm': _jnp.float32, 'ffn1_w1': _jnp.float32, 'ffn1_w2': _jnp.float32, 'mix_norm': _jnp.float32, 'ffn2_norm': _jnp.float32, 'ffn2_w1': _jnp.float32, 'ffn2_w2': _jnp.float32, 'ab_w_in': _jnp.float32, 's5_lambda_re': _jnp.float32, 's5_lambda_im': _jnp.float32, 's5_b_re': _jnp.float32, 's5_b_im': _jnp.float32, 's5_c_re': _jnp.float32, 's5_c_im': _jnp.float32, 's5_log_dt': _jnp.float32, 's5_d': _jnp.float32, 's5_w_glu': _jnp.float32, 'gla_w_gk': _jnp.float32, 'gla_b_gk': _jnp.float32, 'gla_norm': _jnp.float32, 'ab_w_out': _jnp.float32, 'ret_w_in': _jnp.float32, 'ret_norm': _jnp.float32, 'ret_w_out': _jnp.float32, 'final_norm': _jnp.float32}
MOMENT_SCALE = {'ffn1_norm': 1.419385e-01, 'ffn1_w1': 6.079920e-02, 'ffn1_w2': 9.909481e-02, 'mix_norm': 2.555186e-01, 'ffn2_norm': 1.021635e-01, 'ffn2_w1': 4.268433e-02, 'ffn2_w2': 6.954737e-02, 'ab_w_in': 1.887187e-01, 's5_lambda_re': 6.968901e-03, 's5_lambda_im': 7.157251e-03, 's5_b_re': 3.797313e-03, 's5_b_im': 3.848508e-03, 's5_c_re': 7.301580e-03, 's5_c_im': 7.298528e-03, 's5_log_dt': 4.577898e+00, 's5_d': 1.265912e-01, 's5_w_glu': 2.952034e-02, 'gla_w_gk': 1.928769e-02, 'gla_b_gk': 7.746937e-02, 'gla_norm': 1.822270e-01, 'ab_w_out': 1.444802e-01, 'ret_w_in': 9.300916e-02, 'ret_norm': 8.269570e-02, 'ret_w_out': 1.125292e-01, 'final_norm': 6.387952e+01}


def _to_microbatches(a, axis):
    t = _jnp.moveaxis(a, axis, 0)
    t = t.reshape((N_MICROBATCH, t.shape[0] // N_MICROBATCH) + t.shape[1:])
    return _jnp.moveaxis(t, 1, axis + 1)


def setup_inputs(seed: int = 0) -> dict:
    inp = _fwd_setup_inputs(seed)
    key = _jax.random.fold_in(_jax.random.key(seed), 7919)
    shape, _ = _output_shape()
    out = dict(inp)
    out["loss_target"] = _jax.random.normal(_jax.random.fold_in(key, 0), shape, _jnp.float32)
    for i, name in enumerate(TWIN_WEIGHTS):
        w = inp[name].astype(_jnp.float32)
        if MOMENT_SCALE is None:
            s = _jnp.sqrt(_jnp.mean(_jnp.square(w)) + 1e-30)
        else:
            s = MOMENT_SCALE[name]
        km, kv = _jax.random.split(_jax.random.fold_in(key, i + 1))
        out[name] = w
        out["m_" + name] = s * _jax.random.normal(km, w.shape, _jnp.float32)
        out["v_" + name] = (s * s) * _jax.random.uniform(kv, w.shape, _jnp.float32, 0.5, 1.5)
    if N_MICROBATCH > 1:
        for name, axis in PER_EXAMPLE_BATCH_AXIS.items():
            out[name] = _to_microbatches(out[name], axis)
    return {'x': out['x'], 'ffn1_norm': out['ffn1_norm'], 'ffn1_w1': out['ffn1_w1'], 'ffn1_w2': out['ffn1_w2'], 'mix_norm': out['mix_norm'], 'ffn2_norm': out['ffn2_norm'], 'ffn2_w1': out['ffn2_w1'], 'ffn2_w2': out['ffn2_w2'], 'ab_w_in': out['ab_w_in'], 's5_lambda_re': out['s5_lambda_re'], 's5_lambda_im': out['s5_lambda_im'], 's5_b_re': out['s5_b_re'], 's5_b_im': out['s5_b_im'], 's5_c_re': out['s5_c_re'], 's5_c_im': out['s5_c_im'], 's5_log_dt': out['s5_log_dt'], 's5_d': out['s5_d'], 's5_w_glu': out['s5_w_glu'], 'gla_w_gk': out['gla_w_gk'], 'gla_b_gk': out['gla_b_gk'], 'gla_norm': out['gla_norm'], 'ab_w_out': out['ab_w_out'], 'ret_w_in': out['ret_w_in'], 'ret_norm': out['ret_norm'], 'ret_w_out': out['ret_w_out'], 'final_norm': out['final_norm'], 'loss_target': out['loss_target'], 'm_ffn1_norm': out['m_ffn1_norm'], 'm_ffn1_w1': out['m_ffn1_w1'], 'm_ffn1_w2': out['m_ffn1_w2'], 'm_mix_norm': out['m_mix_norm'], 'm_ffn2_norm': out['m_ffn2_norm'], 'm_ffn2_w1': out['m_ffn2_w1'], 'm_ffn2_w2': out['m_ffn2_w2'], 'm_ab_w_in': out['m_ab_w_in'], 'm_s5_lambda_re': out['m_s5_lambda_re'], 'm_s5_lambda_im': out['m_s5_lambda_im'], 'm_s5_b_re': out['m_s5_b_re'], 'm_s5_b_im': out['m_s5_b_im'], 'm_s5_c_re': out['m_s5_c_re'], 'm_s5_c_im': out['m_s5_c_im'], 'm_s5_log_dt': out['m_s5_log_dt'], 'm_s5_d': out['m_s5_d'], 'm_s5_w_glu': out['m_s5_w_glu'], 'm_gla_w_gk': out['m_gla_w_gk'], 'm_gla_b_gk': out['m_gla_b_gk'], 'm_gla_norm': out['m_gla_norm'], 'm_ab_w_out': out['m_ab_w_out'], 'm_ret_w_in': out['m_ret_w_in'], 'm_ret_norm': out['m_ret_norm'], 'm_ret_w_out': out['m_ret_w_out'], 'm_final_norm': out['m_final_norm'], 'v_ffn1_norm': out['v_ffn1_norm'], 'v_ffn1_w1': out['v_ffn1_w1'], 'v_ffn1_w2': out['v_ffn1_w2'], 'v_mix_norm': out['v_mix_norm'], 'v_ffn2_norm': out['v_ffn2_norm'], 'v_ffn2_w1': out['v_ffn2_w1'], 'v_ffn2_w2': out['v_ffn2_w2'], 'v_ab_w_in': out['v_ab_w_in'], 'v_s5_lambda_re': out['v_s5_lambda_re'], 'v_s5_lambda_im': out['v_s5_lambda_im'], 'v_s5_b_re': out['v_s5_b_re'], 'v_s5_b_im': out['v_s5_b_im'], 'v_s5_c_re': out['v_s5_c_re'], 'v_s5_c_im': out['v_s5_c_im'], 'v_s5_log_dt': out['v_s5_log_dt'], 'v_s5_d': out['v_s5_d'], 'v_s5_w_glu': out['v_s5_w_glu'], 'v_gla_w_gk': out['v_gla_w_gk'], 'v_gla_b_gk': out['v_gla_b_gk'], 'v_gla_norm': out['v_gla_norm'], 'v_ab_w_out': out['v_ab_w_out'], 'v_ret_w_in': out['v_ret_w_in'], 'v_ret_norm': out['v_ret_norm'], 'v_ret_w_out': out['v_ret_w_out'], 'v_final_norm': out['v_final_norm']}


def _loss(weights, diff, rest, loss_target):
    with _jax.named_scope("forward"):
        args = {**rest, TWIN_DIFF_INPUT: diff, **{k: w.astype(_WEIGHT_DTYPES[k]) for k, w in weights.items()}}
        y = _forward(args)
    with _jax.named_scope("loss_head"):
        err = _jnp.square(y.astype(_jnp.float32) - loss_target)
        return 0.5 * _jnp.sum(_jnp.mean(err, axis=-1)) if err.ndim else 0.5 * err


def _adamw(w, g, m, v):
    m = ADAM_B1 * m + (1.0 - ADAM_B1) * g
    v = ADAM_B2 * v + (1.0 - ADAM_B2) * _jnp.square(g)
    m_hat = m / (1.0 - ADAM_B1 ** ADAM_STEP)
    v_hat = v / (1.0 - ADAM_B2 ** ADAM_STEP)
    delta = -ADAM_LR * (m_hat / (_jnp.sqrt(v_hat) + ADAM_EPS) + ADAM_WD * w)
    return delta, m, v


def reference(x, ffn1_norm, ffn1_w1, ffn1_w2, mix_norm, ffn2_norm, ffn2_w1, ffn2_w2, ab_w_in, s5_lambda_re, s5_lambda_im, s5_b_re, s5_b_im, s5_c_re, s5_c_im, s5_log_dt, s5_d, s5_w_glu, gla_w_gk, gla_b_gk, gla_norm, ab_w_out, ret_w_in, ret_norm, ret_w_out, final_norm, loss_target, m_ffn1_norm, m_ffn1_w1, m_ffn1_w2, m_mix_norm, m_ffn2_norm, m_ffn2_w1, m_ffn2_w2, m_ab_w_in, m_s5_lambda_re, m_s5_lambda_im, m_s5_b_re, m_s5_b_im, m_s5_c_re, m_s5_c_im, m_s5_log_dt, m_s5_d, m_s5_w_glu, m_gla_w_gk, m_gla_b_gk, m_gla_norm, m_ab_w_out, m_ret_w_in, m_ret_norm, m_ret_w_out, m_final_norm, v_ffn1_norm, v_ffn1_w1, v_ffn1_w2, v_mix_norm, v_ffn2_norm, v_ffn2_w1, v_ffn2_w2, v_ab_w_in, v_s5_lambda_re, v_s5_lambda_im, v_s5_b_re, v_s5_b_im, v_s5_c_re, v_s5_c_im, v_s5_log_dt, v_s5_d, v_s5_w_glu, v_gla_w_gk, v_gla_b_gk, v_gla_norm, v_ab_w_out, v_ret_w_in, v_ret_norm, v_ret_w_out, v_final_norm):
    given = dict(x=x, ffn1_norm=ffn1_norm, ffn1_w1=ffn1_w1, ffn1_w2=ffn1_w2, mix_norm=mix_norm, ffn2_norm=ffn2_norm, ffn2_w1=ffn2_w1, ffn2_w2=ffn2_w2, ab_w_in=ab_w_in, s5_lambda_re=s5_lambda_re, s5_lambda_im=s5_lambda_im, s5_b_re=s5_b_re, s5_b_im=s5_b_im, s5_c_re=s5_c_re, s5_c_im=s5_c_im, s5_log_dt=s5_log_dt, s5_d=s5_d, s5_w_glu=s5_w_glu, gla_w_gk=gla_w_gk, gla_b_gk=gla_b_gk, gla_norm=gla_norm, ab_w_out=ab_w_out, ret_w_in=ret_w_in, ret_norm=ret_norm, ret_w_out=ret_w_out, final_norm=final_norm, loss_target=loss_target, m_ffn1_norm=m_ffn1_norm, m_ffn1_w1=m_ffn1_w1, m_ffn1_w2=m_ffn1_w2, m_mix_norm=m_mix_norm, m_ffn2_norm=m_ffn2_norm, m_ffn2_w1=m_ffn2_w1, m_ffn2_w2=m_ffn2_w2, m_ab_w_in=m_ab_w_in, m_s5_lambda_re=m_s5_lambda_re, m_s5_lambda_im=m_s5_lambda_im, m_s5_b_re=m_s5_b_re, m_s5_b_im=m_s5_b_im, m_s5_c_re=m_s5_c_re, m_s5_c_im=m_s5_c_im, m_s5_log_dt=m_s5_log_dt, m_s5_d=m_s5_d, m_s5_w_glu=m_s5_w_glu, m_gla_w_gk=m_gla_w_gk, m_gla_b_gk=m_gla_b_gk, m_gla_norm=m_gla_norm, m_ab_w_out=m_ab_w_out, m_ret_w_in=m_ret_w_in, m_ret_norm=m_ret_norm, m_ret_w_out=m_ret_w_out, m_final_norm=m_final_norm, v_ffn1_norm=v_ffn1_norm, v_ffn1_w1=v_ffn1_w1, v_ffn1_w2=v_ffn1_w2, v_mix_norm=v_mix_norm, v_ffn2_norm=v_ffn2_norm, v_ffn2_w1=v_ffn2_w1, v_ffn2_w2=v_ffn2_w2, v_ab_w_in=v_ab_w_in, v_s5_lambda_re=v_s5_lambda_re, v_s5_lambda_im=v_s5_lambda_im, v_s5_b_re=v_s5_b_re, v_s5_b_im=v_s5_b_im, v_s5_c_re=v_s5_c_re, v_s5_c_im=v_s5_c_im, v_s5_log_dt=v_s5_log_dt, v_s5_d=v_s5_d, v_s5_w_glu=v_s5_w_glu, v_gla_w_gk=v_gla_w_gk, v_gla_b_gk=v_gla_b_gk, v_gla_norm=v_gla_norm, v_ab_w_out=v_ab_w_out, v_ret_w_in=v_ret_w_in, v_ret_norm=v_ret_norm, v_ret_w_out=v_ret_w_out, v_final_norm=v_final_norm)
    weights = {n: given[n] for n in TWIN_WEIGHTS}
    shared = {n: given[n] for n in SHARED_INPUTS}
    per_example = {n: given[n] for n in ['x']}
    grad_fn = _jax.value_and_grad(_loss, argnums=(0, 1))

    def one_microbatch(ex, loss_target):
        ex = dict(ex)
        diff = ex.pop(TWIN_DIFF_INPUT)
        return grad_fn(weights, diff, {**shared, **ex}, loss_target)

    if N_MICROBATCH == 1:
        loss, (grad_w, grad_x) = one_microbatch(per_example, given["loss_target"])
    else:
        def body(carry, xs):
            loss_sum, grad_sum = carry
            l_k, (gw_k, gx_k) = one_microbatch(xs[0], xs[1])
            with _jax.named_scope("update"):
                return (loss_sum + l_k, _jax.tree.map(_jnp.add, grad_sum, gw_k)), gx_k

        init = (_jnp.zeros((), _jnp.float32), _jax.tree.map(_jnp.zeros_like, weights))
        (loss, grad_w), grad_x = _jax.lax.scan(body, init, (per_example, given["loss_target"]))
    with _jax.named_scope("update"):
        delta_w, new_m, new_v = {}, {}, {}
        for n in TWIN_WEIGHTS:
            delta_w[n], new_m[n], new_v[n] = _adamw(weights[n], grad_w[n], given["m_" + n], given["v_" + n])
    return (loss, grad_x, *[grad_w[n] for n in TWIN_WEIGHTS], *[delta_w[n] for n in TWIN_WEIGHTS],
            *[new_m[n] for n in TWIN_WEIGHTS], *[new_v[n] for n in TWIN_WEIGHTS])
```

```python
import functools
import math

import jax
import jax.numpy as jnp
from jax import lax
from jax.experimental import pallas as pl
from jax.experimental.pallas import tpu as pltpu

F32 = jnp.float32
BF = jnp.bfloat16
N_DEV = 8
EPS = 1e-6
S5_GROUP = 16
GLA_HEADS = 4
GLA_RANK = 16
GLA_GATE_NORM = 16.0
RET_HEADS = 8
ROPE_BASE = 10000.0
GLA_CHUNK = 64
RET_CHUNK = 256
ADAM_LR, ADAM_B1, ADAM_B2, ADAM_EPS, ADAM_WD, ADAM_STEP = 0.001, 0.9, 0.999, 1e-08, 0.01, 10
VMEM_LIMIT_BYTES = 56 * 1024 * 1024
LANE = 128
SUBLANE = 8

NN = (((1,), (0,)), ((), ()))
NT = (((1,), (1,)), ((), ()))
TN = (((0,), (0,)), ((), ()))


def _tile(n, pref, align):
    if n <= pref:
        return n
    t = (pref // align) * align
    while t >= align:
        if n % t == 0:
            return t
        t -= align
    return n


def _params(sem):
    return pltpu.CompilerParams(dimension_semantics=sem, vmem_limit_bytes=VMEM_LIMIT_BYTES)


def _dot(a, b, dims=NN):
    return lax.dot_general(a.astype(BF), b.astype(BF), dims, preferred_element_type=F32)


def _dot3(m01, g, dims=NN):
    g1 = g.astype(BF)
    r1 = g - g1.astype(F32)
    g2 = r1.astype(BF)
    g3 = (r1 - g2.astype(F32)).astype(BF)
    m = m01.astype(BF)
    return (lax.dot_general(m, g1, dims, preferred_element_type=F32)
            + lax.dot_general(m, g2, dims, preferred_element_type=F32)
            + lax.dot_general(m, g3, dims, preferred_element_type=F32))


def _sigmoid(x):
    return 1.0 / (1.0 + jnp.exp(-x))


def _mm(name, pairs, dims, grid, outs, out_specs, acc_shape, epi=None, eins=(), especs=()):
    n_p, n_e, n_o = len(pairs), len(eins), len(outs)
    nk = grid[2]

    def body(*refs):
        a_refs = refs[0:2 * n_p:2]
        b_refs = refs[1:2 * n_p:2]
        e_refs = refs[2 * n_p:2 * n_p + n_e]
        o_refs = refs[2 * n_p + n_e:2 * n_p + n_e + n_o]
        acc = refs[-1]
        ids = (pl.program_id(0), pl.program_id(1), pl.program_id(2))

        part = _dot(a_refs[0][...], b_refs[0][...], dims)
        for p in range(1, n_p):
            part = part + _dot(a_refs[p][...], b_refs[p][...], dims)

        def finish(total):
            if epi is None:
                o_refs[0][...] = total.astype(o_refs[0].dtype)
            else:
                epi(total, e_refs, o_refs, ids)

        if nk == 1:
            finish(part)
        else:
            @pl.when(ids[2] == 0)
            def _():
                acc[...] = part

            @pl.when(ids[2] > 0)
            def _():
                acc[...] += part

            @pl.when(ids[2] == nk - 1)
            def _():
                finish(acc[...])

    in_specs, args = [], []
    for a, a_spec, b, b_spec in pairs:
        in_specs += [a_spec, b_spec]
        args += [a, b]
    in_specs += list(especs)
    args += list(eins)
    res = pl.pallas_call(
        body, name=name, grid=grid, in_specs=in_specs, out_specs=list(out_specs), out_shape=list(outs),
        scratch_shapes=[pltpu.VMEM(acc_shape, F32)],
        compiler_params=_params(("arbitrary", "arbitrary", "arbitrary")),
    )(*args)
    return res


def _sds(shape, dtype):
    return jax.ShapeDtypeStruct(shape, dtype)


def _mm_plain(name, a, b, dims, out_dtype, tm=512, tn=1024, tk=1024, epi=None, eins=(), especs=None, extra_outs=(), extra_specs=()):
    if dims == NN:
        (m, k), n = a.shape, b.shape[1]
    elif dims == NT:
        (m, k), n = a.shape, b.shape[0]
    else:
        (k, m), n = a.shape, b.shape[1]
    tm, tn = _tile(m, tm, LANE if dims == TN else SUBLANE), _tile(n, tn, LANE)
    tk = _tile(k, tk, SUBLANE if dims == TN else LANE)
    grid = (m // tm, n // tn, k // tk)
    if dims == NN:
        a_spec = pl.BlockSpec((tm, tk), lambda i, j, kk: (i, kk))
        b_spec = pl.BlockSpec((tk, tn), lambda i, j, kk: (kk, j))
    elif dims == NT:
        a_spec = pl.BlockSpec((tm, tk), lambda i, j, kk: (i, kk))
        b_spec = pl.BlockSpec((tn, tk), lambda i, j, kk: (j, kk))
    else:
        a_spec = pl.BlockSpec((tk, tm), lambda i, j, kk: (kk, i))
        b_spec = pl.BlockSpec((tk, tn), lambda i, j, kk: (kk, j))
    o_spec = pl.BlockSpec((tm, tn), lambda i, j, kk: (i, j))
    if especs is None:
        especs = [o_spec] * len(eins)
    else:
        especs = [o_spec if s is None else s for s in especs]
    res = _mm(name, [(a, a_spec, b, b_spec)], dims, grid, [_sds((m, n), out_dtype)] + list(extra_outs),
              [o_spec] + list(extra_specs), (tm, tn), epi, eins, especs)
    return res if extra_outs else res[0]


def _rms_fwd(name, x, g):
    l, d = x.shape
    tm = _tile(l, 1024, SUBLANE)

    def body(x_ref, g_ref, o_ref):
        xv = x_ref[...]
        r = lax.rsqrt(jnp.mean(xv * xv, axis=-1, keepdims=True) + EPS)
        o_ref[...] = (xv * r * g_ref[...]).astype(o_ref.dtype)

    return pl.pallas_call(
        body, name=name, grid=(l // tm,),
        in_specs=[pl.BlockSpec((tm, d), lambda i: (i, 0)), pl.BlockSpec((1, d), lambda i: (0, 0))],
        out_specs=pl.BlockSpec((tm, d), lambda i: (i, 0)), out_shape=_sds((l, d), BF),
        compiler_params=_params(("arbitrary",)),
    )(x, g)


def _rms_bwd_epi(first_axis):
    def epi(acc, e_refs, o_refs, ids):
        x_ref, g_ref, dr_ref = e_refs
        dx_ref, dg_ref = o_refs
        xv = x_ref[...]
        r = lax.rsqrt(jnp.mean(xv * xv, axis=-1, keepdims=True) + EPS)
        xh = xv * r
        dxh = acc * g_ref[...]
        dx_ref[...] = dr_ref[...] + r * (dxh - xh * jnp.mean(dxh * xh, axis=-1, keepdims=True))
        part = jnp.sum(acc * xh, axis=0, keepdims=True)

        @pl.when(ids[first_axis] == 0)
        def _():
            dg_ref[...] = part

        @pl.when(ids[first_axis] > 0)
        def _():
            dg_ref[...] += part

    return epi


def _loss_head(name, x, g, target):
    l, d = x.shape
    tm = _tile(l, 512, SUBLANE)
    n = l // tm

    def body(x_ref, g_ref, t_ref, dx_ref, dg_ref, loss_ref, lacc):
        i = pl.program_id(0)
        xv = x_ref[...]
        r = lax.rsqrt(jnp.mean(xv * xv, axis=-1, keepdims=True) + EPS)
        xh = xv * r
        e = xh * g_ref[...] - t_ref[...]
        dy = e * (1.0 / d)
        dxh = dy * g_ref[...]
        dx_ref[...] = r * (dxh - xh * jnp.mean(dxh * xh, axis=-1, keepdims=True))
        dg_part = jnp.sum(dy * xh, axis=0, keepdims=True)
        l_part = jnp.sum(e * e, axis=0, keepdims=True)

        @pl.when(i == 0)
        def _():
            dg_ref[...] = dg_part
            lacc[...] = l_part

        @pl.when(i > 0)
        def _():
            dg_ref[...] += dg_part
            lacc[...] += l_part

        @pl.when(i == n - 1)
        def _():
            loss_ref[...] = jnp.zeros_like(loss_ref) + jnp.sum(lacc[...]) * (0.5 / d)

    return pl.pallas_call(
        body, name=name, grid=(n,),
        in_specs=[pl.BlockSpec((tm, d), lambda i: (i, 0)), pl.BlockSpec((1, d), lambda i: (0, 0)),
                  pl.BlockSpec((tm, d), lambda i: (i, 0))],
        out_specs=[pl.BlockSpec((tm, d), lambda i: (i, 0)), pl.BlockSpec((1, d), lambda i: (0, 0)),
                   pl.BlockSpec((1, LANE), lambda i: (0, 0))],
        out_shape=[_sds((l, d), F32), _sds((1, d), F32), _sds((1, LANE), F32)],
        scratch_shapes=[pltpu.VMEM((1, d), F32)],
        compiler_params=_params(("arbitrary",)),
    )(x, g, target)


def _ffn_up(name, hn, w1):
    l, d = hn.shape
    f = w1.shape[1] // 2
    tm, tn = _tile(l, 512, SUBLANE), _tile(f, 1408, LANE)
    nj = f // tn

    def body(h_ref, wg_ref, wu_ref, gu_ref, a_ref):
        h = h_ref[...]
        g = jnp.dot(h, wg_ref[...], preferred_element_type=F32)
        u = jnp.dot(h, wu_ref[...], preferred_element_type=F32)
        gu_ref[0] = g.astype(BF)
        gu_ref[1] = u.astype(BF)
        a_ref[...] = (g * _sigmoid(g) * u).astype(BF)

    return pl.pallas_call(
        body, name=name, grid=(nj, l // tm),
        in_specs=[pl.BlockSpec((tm, d), lambda j, i: (i, 0)), pl.BlockSpec((d, tn), lambda j, i: (0, j)),
                  pl.BlockSpec((d, tn), lambda j, i: (0, j + nj))],
        out_specs=[pl.BlockSpec((2, tm, tn), lambda j, i: (0, i, j)), pl.BlockSpec((tm, tn), lambda j, i: (i, j))],
        out_shape=[_sds((2, l, f), BF), _sds((l, f), BF)],
        compiler_params=_params(("arbitrary", "arbitrary")),
    )(hn, w1, w1)


def _ffn_fwd(tag, x, gnorm, w1, w2):
    hn = _rms_fwd(tag + "_norm", x, gnorm)
    gu, a = _ffn_up(tag + "_up", hn, w1)

    def epi(acc, e_refs, o_refs, ids):
        o_refs[0][...] = e_refs[0][...] + 0.5 * acc

    l, d = x.shape
    x_new = _mm_plain(tag + "_down", a, w2, NN, F32, tm=512, tn=d, tk=1408, epi=epi, eins=[x])
    return x_new, (hn, gu, a)


def _ffn_bwd(tag, dres, x, gnorm, w1, w2, saved):
    hn, gu, a = saved
    l, d = x.shape
    f = w2.shape[0]
    tm, tn = _tile(l, 512, SUBLANE), _tile(f, 1408, LANE)
    nj = f // tn

    def epi_gu(acc, e_refs, o_refs, ids):
        g = e_refs[0][0].astype(F32)
        u = e_refs[0][1].astype(F32)
        s = _sigmoid(g)
        da = 0.5 * acc
        o_refs[0][0] = (da * u * s * (1.0 + g * (1.0 - s))).astype(BF)
        o_refs[0][1] = (da * g * s).astype(BF)

    gu_spec = pl.BlockSpec((2, tm, tn), lambda i, j, kk: (0, i, j))
    dgu = _mm(tag + "_dgu",
              [(dres, pl.BlockSpec((tm, d), lambda i, j, kk: (i, 0)), w2, pl.BlockSpec((tn, d), lambda i, j, kk: (j, 0)))],
              NT, (l // tm, nj, 1), [_sds((2, l, f), BF)], [gu_spec], (tm, tn), epi_gu, [gu], [gu_spec])[0]

    def epi_half(acc, e_refs, o_refs, ids):
        o_refs[0][...] = (0.5 * acc).astype(BF)

    dw2 = _mm_plain(tag + "_dw2", a, dres, TN, BF, tm=1408, tn=d, tk=512, epi=epi_half)

    tk = _tile(l, 512, SUBLANE)
    dw1 = _mm(tag + "_dw1",
              [(hn, pl.BlockSpec((tk, d), lambda i, j, kk: (kk, 0)),
                dgu, pl.BlockSpec((None, tk, tn), lambda i, j, kk: (j // nj, kk, j % nj)))],
              TN, (1, 2 * nj, l // tk), [_sds((d, 2 * f), BF)], [pl.BlockSpec((d, tn), lambda i, j, kk: (0, j))], (d, tn))[0]

    row = pl.BlockSpec((tm, d), lambda i, j, kk: (i, 0))
    vec = pl.BlockSpec((1, d), lambda i, j, kk: (0, 0))
    dx, dg = _mm(tag + "_dhn",
                 [(dgu, pl.BlockSpec((None, tm, tn), lambda i, j, kk: (kk // nj, i, kk % nj)),
                   w1, pl.BlockSpec((d, tn), lambda i, j, kk: (0, kk)))],
                 NT, (l // tm, 1, 2 * nj), [_sds((l, d), F32), _sds((1, d), F32)], [row, vec], (tm, d),
                 _rms_bwd_epi(0), [x, gnorm, dres], [row, vec, row])
    return dx, dg, dw1, dw2


def _scan(name, bu, chunk0, nch, cw, a_tab, reverse, xs=None):
    l = bu.shape[0]
    hw = cw // 2
    tb = _tile(l, 512, SUBLANE)
    nb = l // tb
    with_acc = xs is not None

    def body(*refs):
        if with_acc:
            bu_ref, a_ref, xs_ref, x_ref, acc_ref, st_ref = refs
        else:
            bu_ref, a_ref, x_ref, st_ref = refs
        r = pl.program_id(1)

        @pl.when(r == 0)
        def _():
            st_ref[...] = jnp.zeros_like(st_ref)
            if with_acc:
                acc_ref[...] = jnp.zeros_like(acc_ref)

        ar = a_ref[:, :hw]
        ai = a_ref[:, hw:]

        def step(t, carry):
            tt = (tb - 1 - t) if reverse else t
            sr, si = carry[0], carry[1]
            row = bu_ref[pl.ds(tt, 1), :]
            out = list(carry)
            if with_acc:
                xrow = xs_ref[pl.ds(tt, 1), :]
                xr, xi = xrow[:, :hw], xrow[:, hw:]
                out[2] = carry[2] + sr * xr + si * xi
                out[3] = carry[3] + si * xr - sr * xi
            nr = ar * sr - ai * si + row[:, :hw]
            ni = ar * si + ai * sr + row[:, hw:]
            x_ref[pl.ds(tt, 1), pl.ds(0, hw)] = nr
            x_ref[pl.ds(tt, 1), pl.ds(hw, hw)] = ni
            out[0], out[1] = nr, ni
            return tuple(out)

        init = [st_ref[:, :hw], st_ref[:, hw:]]
        if with_acc:
            init += [jnp.zeros((1, hw), F32), jnp.zeros((1, hw), F32)]
        fin = lax.fori_loop(0, tb, step, tuple(init), unroll=8)
        st_ref[:, pl.ds(0, hw)] = fin[0]
        st_ref[:, pl.ds(hw, hw)] = fin[1]
        if with_acc:
            acc_ref[:, pl.ds(0, hw)] += fin[2]
            acc_ref[:, pl.ds(hw, hw)] += fin[3]

    def rows(c, r):
        return ((nb - 1 - r) if reverse else r)

    in_specs = [pl.BlockSpec((tb, cw), lambda c, r: (rows(c, r), chunk0 + c)), pl.BlockSpec((1, cw), lambda c, r: (0, c))]
    args = [bu, a_tab]
    out_specs = [pl.BlockSpec((tb, cw), lambda c, r: (rows(c, r), c))]
    outs = [_sds((l, nch * cw), F32)]
    if with_acc:
        in_specs.append(pl.BlockSpec((tb, cw), lambda c, r: (rows(c, r), c)))
        args.append(xs)
        out_specs.append(pl.BlockSpec((1, cw), lambda c, r: (0, c)))
        outs.append(_sds((1, nch * cw), F32))
    res = pl.pallas_call(
        body, name=name, grid=(nch, nb), in_specs=in_specs, out_specs=out_specs, out_shape=outs,
        scratch_shapes=[pltpu.VMEM((1, cw), F32)],
        compiler_params=_params(("arbitrary", "arbitrary")),
    )(*args)
    return res if with_acc else res[0]


def _s5_tables(lam_re, lam_im, b_re, b_im, c_re, c_im, log_dt, hs):
    f32 = F32
    g, n = lam_re.shape[1], lam_re.shape[2]
    p = b_re.shape[-1]
    nch = (g * n) // hs
    lr = jnp.minimum(lam_re.astype(f32), -1e-4)
    li = lam_im.astype(f32)
    dt = jnp.exp(log_dt.astype(f32))[..., None]
    mag = jnp.exp(lr * dt)
    ar = mag * jnp.cos(li * dt)
    ai = mag * jnp.sin(li * dt)
    den = lr * lr + li * li
    cr = ((ar - 1.0) * lr + ai * li) / den
    ci = (ai * lr - (ar - 1.0) * li) / den
    bbr = cr[..., None] * b_re - ci[..., None] * b_im
    bbi = cr[..., None] * b_im + ci[..., None] * b_re
    eye = jnp.eye(g, dtype=f32)
    a_tab = jnp.stack([ar, ai], axis=1).reshape(2, 2, nch, hs).transpose(0, 2, 1, 3).reshape(1, -1)
    bb = jnp.stack([bbr, bbi], axis=1)
    bd = (bb[:, :, :, :, :, None] * eye[None, None, :, None, None, :])
    bd = bd.transpose(5, 4, 0, 1, 2, 3).reshape(g * p, 2, 2, nch, hs).transpose(0, 1, 3, 2, 4).reshape(g * p, -1)
    cc = jnp.stack([c_re, -c_im], axis=1)
    cd = (cc[:, :, :, :, :, None] * eye[None, None, :, None, None, :])
    cd = cd.transpose(0, 1, 2, 4, 5, 3).reshape(2, 2, nch, hs, g * p).transpose(0, 2, 1, 3, 4).reshape(-1, g * p)
    return a_tab, bd, cd


def _conj_tab(a_tab, hs):
    t = a_tab.reshape(-1, 2, hs)
    return jnp.stack([t[:, 0], -t[:, 1]], axis=1).reshape(1, -1)


def _gelu(y):
    c = math.sqrt(2.0 / math.pi)
    return 0.5 * y * (1.0 + jnp.tanh(c * (y + 0.044715 * y * y * y)))


def _gelu_grad(y):
    c = math.sqrt(2.0 / math.pi)
    th = jnp.tanh(c * (y + 0.044715 * y * y * y))
    return 0.5 * (1.0 + th) + 0.5 * y * (1.0 - th * th) * c * (1.0 + 3.0 * 0.044715 * y * y)


def _glu_fwd(name, y, w):
    l, wd = y.shape
    tm = _tile(l, 512, SUBLANE)

    def body(y_ref, w_ref, o_ref):
        gy = _gelu(y_ref[...])
        z = _dot(gy, w_ref[...])
        o_ref[...] = (gy * _sigmoid(z)).astype(o_ref.dtype)

    return pl.pallas_call(
        body, name=name, grid=(l // tm,),
        in_specs=[pl.BlockSpec((tm, wd), lambda i: (i, 0)), pl.BlockSpec((wd, wd), lambda i: (0, 0))],
        out_specs=pl.BlockSpec((tm, wd), lambda i: (i, 0)), out_shape=_sds((l, wd), BF),
        compiler_params=_params(("arbitrary",)),
    )(y, w)


def _glu_bwd(name, y, w, dout, dcol):
    l, wd = y.shape
    tm = _tile(l, 512, SUBLANE)

    def body(y_ref, w_ref, d_ref, dy_ref, dw_ref):
        i = pl.program_id(0)
        yv = y_ref[...]
        gy = _gelu(yv)
        s = _sigmoid(_dot(gy, w_ref[...]))
        d = d_ref[...].astype(F32)
        t = d * gy * s * (1.0 - s)
        dgy = d * s + _dot(t, w_ref[...], NT)
        dy_ref[...] = dgy * _gelu_grad(yv)
        part = _dot(gy, t, TN)

        @pl.when(i == 0)
        def _():
            dw_ref[...] = part

        @pl.when(i > 0)
        def _():
            dw_ref[...] += part

    return pl.pallas_call(
        body, name=name, grid=(l // tm,),
        in_specs=[pl.BlockSpec((tm, wd), lambda i: (i, 0)), pl.BlockSpec((wd, wd), lambda i: (0, 0)),
                  pl.BlockSpec((tm, wd), lambda i: (i, dcol))],
        out_specs=[pl.BlockSpec((tm, wd), lambda i: (i, 0)), pl.BlockSpec((wd, wd), lambda i: (0, 0))],
        out_shape=[_sds((l, wd), F32), _sds((wd, wd), F32)],
        compiler_params=_params(("arbitrary",)),
    )(y, w, dout)


def _log_sigmoid(x):
    return jnp.minimum(x, 0.0) - jnp.log(1.0 + jnp.exp(-jnp.abs(x)))


def _gate_fwd(name, glo, wf, wb, bf, bb):
    l, r2 = glo.shape
    hk = wf.shape[1]
    tm = _tile(l, 1024, SUBLANE)

    def body(x_ref, wf_ref, wb_ref, bf_ref, bb_ref, gf_ref, gb_ref):
        xv = x_ref[...]
        gf_ref[...] = _log_sigmoid(_dot(xv, wf_ref[...]) + bf_ref[...]) * (1.0 / GLA_GATE_NORM)
        gb_ref[...] = _log_sigmoid(_dot(xv, wb_ref[...]) + bb_ref[...]) * (1.0 / GLA_GATE_NORM)

    w_spec = pl.BlockSpec((r2, hk), lambda i: (0, 0))
    b_spec = pl.BlockSpec((1, hk), lambda i: (0, 0))
    o_spec = pl.BlockSpec((tm, hk), lambda i: (i, 0))
    return pl.pallas_call(
        body, name=name, grid=(l // tm,),
        in_specs=[pl.BlockSpec((tm, r2), lambda i: (i, 0)), w_spec, w_spec, b_spec, b_spec],
        out_specs=[o_spec, o_spec], out_shape=[_sds((l, hk), F32), _sds((l, hk), F32)],
        compiler_params=_params(("arbitrary",)),
    )(glo, wf, wb, bf, bb)


def _gate_bwd(name, glo, wf, wb, bf, bb, dgf, dgb):
    l, r2 = glo.shape
    hk = wf.shape[1]
    tm = _tile(l, 1024, SUBLANE)

    def body(x_ref, wf_ref, wb_ref, bf_ref, bb_ref, dgf_ref, dgb_ref, dx_ref, dwf_ref, dwb_ref, dbf_ref, dbb_ref):
        i = pl.program_id(0)
        xv = x_ref[...]
        kf = _dot(xv, wf_ref[...]) + bf_ref[...]
        kb = _dot(xv, wb_ref[...]) + bb_ref[...]
        dkf = dgf_ref[...] * (1.0 / GLA_GATE_NORM) * _sigmoid(-kf)
        dkb = dgb_ref[...] * (1.0 / GLA_GATE_NORM) * _sigmoid(-kb)
        dx_ref[...] = _dot(dkf, wf_ref[...], NT) + _dot(dkb, wb_ref[...], NT)
        parts = (_dot(xv, dkf, TN), _dot(xv, dkb, TN), jnp.sum(dkf, axis=0, keepdims=True), jnp.sum(dkb, axis=0, keepdims=True))
        accs = (dwf_ref, dwb_ref, dbf_ref, dbb_ref)

        @pl.when(i == 0)
        def _():
            for a_, p_ in zip(accs, parts):
                a_[...] = p_

        @pl.when(i > 0)
        def _():
            for a_, p_ in zip(accs, parts):
                a_[...] += p_

    w_spec = pl.BlockSpec((r2, hk), lambda i: (0, 0))
    b_spec = pl.BlockSpec((1, hk), lambda i: (0, 0))
    g_spec = pl.BlockSpec((tm, hk), lambda i: (i, 0))
    x_spec = pl.BlockSpec((tm, r2), lambda i: (i, 0))
    return pl.pallas_call(
        body, name=name, grid=(l // tm,),
        in_specs=[x_spec, w_spec, w_spec, b_spec, b_spec, g_spec, g_spec],
        out_specs=[x_spec, w_spec, w_spec, b_spec, b_spec],
        out_shape=[_sds((l, r2), F32), _sds((r2, hk), F32), _sds((r2, hk), F32), _sds((1, hk), F32), _sds((1, hk), F32)],
        compiler_params=_params(("arbitrary",)),
    )(glo, wf, wb, bf, bb, dgf, dgb)


def _chunk_terms(qc, kc, gc, lg, chunk, reverse):
    ri = lax.broadcasted_iota(jnp.int32, (chunk, chunk), 0)
    ci = lax.broadcasted_iota(jnp.int32, (chunk, chunk), 1)
    if reverse:
        tri = ci >= ri
        mask = ci > ri
    else:
        tri = ci <= ri
        mask = ci <= ri
    if gc is not None:
        cum = _dot3(tri.astype(F32), gc)
        last = cum[0:1, :] if reverse else cum[chunk - 1:chunk, :]
    else:
        pos = lax.broadcasted_iota(jnp.int32, (chunk, 1), 0).astype(F32)
        cum = ((chunk - pos) if reverse else (pos + 1.0)) * lg
        last = chunk * lg
    e = jnp.exp(cum)
    einv = jnp.exp(-cum)
    dec = jnp.exp(last - cum)
    return e, einv, dec, qc * e, kc * einv, kc * dec, jnp.exp(last), mask, tri


def _lin_specs(arr, width, col, tb, nb, reverse, per_head):
    if per_head:
        return pl.BlockSpec((tb, width), lambda h, r: ((nb - 1 - r) if reverse else r, col + h))
    return pl.BlockSpec((tb, width), lambda h, r: ((nb - 1 - r) if reverse else r, col))


def _lin_fwd(name, q, k, v, g, lgtab, *, heads, hb, dk, dv, chunk, tb, qcol, kcol, vcol, qscale, reverse):
    l = q.shape[0]
    nb = l // tb
    ncb = tb // chunk
    ng = heads // hb
    gated = g is not None
    per_head = ng > 1

    def body(*refs):
        if gated:
            q_ref, k_ref, v_ref, g_ref, o_ref, sp_ref, st = refs
        else:
            q_ref, k_ref, v_ref, lg_ref, o_ref, sp_ref, st = refs
        r = pl.program_id(1)

        @pl.when(r == 0)
        def _():
            st[...] = jnp.zeros_like(st)

        lg = None if gated else lg_ref[0:1, 0:1]

        def one_chunk(c, carry):
            cc = (ncb - 1 - c) if reverse else c
            rows = pl.ds(pl.multiple_of(cc * chunk, chunk), chunk)
            for h in range(hb):
                qc = q_ref[rows, h * dk:(h + 1) * dk].astype(F32) * qscale
                kc = k_ref[rows, h * dk:(h + 1) * dk].astype(F32)
                vc = v_ref[rows, h * dv:(h + 1) * dv]
                gc = g_ref[rows, h * dk:(h + 1) * dk] if gated else None
                _, _, _, qd, ki, kdec, e_last, mask, _ = _chunk_terms(qc, kc, gc, lg, chunk, reverse)
                a = jnp.where(mask, _dot(qd, ki, NT), 0.0)
                s_t = st[h]
                o_ref[rows, h * dv:(h + 1) * dv] = _dot(a, vc) + _dot(qd, s_t, NT)
                sp_ref[cc, h] = s_t
                st[h] = s_t * e_last + _dot(vc, kdec, TN)
            return carry

        lax.fori_loop(0, ncb, one_chunk, 0)

    in_specs = [_lin_specs(q, hb * dk, qcol, tb, nb, reverse, per_head), _lin_specs(k, hb * dk, kcol, tb, nb, reverse, per_head),
                _lin_specs(v, hb * dv, vcol, tb, nb, reverse, per_head)]
    args = [q, k, v]
    if gated:
        in_specs.append(_lin_specs(g, hb * dk, 0, tb, nb, reverse, per_head))
        args.append(g)
    else:
        in_specs.append(pl.BlockSpec((None, 1, LANE), lambda h, r: (h, 0, 0)))
        args.append(lgtab)
    out_specs = [_lin_specs(None, hb * dv, 0, tb, nb, reverse, per_head),
                 pl.BlockSpec((ncb, hb, dv, dk), lambda h, r: ((nb - 1 - r) if reverse else r, h, 0, 0))]
    outs = [_sds((l, heads * dv), F32), _sds((l // chunk, heads, dv, dk), F32)]
    return pl.pallas_call(
        body, name=name, grid=(ng, nb), in_specs=in_specs, out_specs=out_specs, out_shape=outs,
        scratch_shapes=[pltpu.VMEM((hb, dv, dk), F32)],
        compiler_params=_params(("arbitrary", "arbitrary")),
    )(*args)


def _lin_bwd(name, q, k, v, g, lgtab, sprev, do, prev, *, heads, hb, dk, dv, chunk, tb, qcol, kcol, vcol, qscale, reverse):
    l = q.shape[0]
    nb = l // tb
    ncb = tb // chunk
    ng = heads // hb
    gated = g is not None
    per_head = ng > 1
    brev = not reverse
    n_prev = 0 if prev is None else len(prev)

    def body(*refs):
        q_ref, k_ref, v_ref, x_ref, sp_ref, do_ref = refs[:6]
        p_refs = refs[6:6 + n_prev]
        o_refs = refs[6 + n_prev:-1]
        dst = refs[-1]
        dq_ref, dk_ref, dv_ref = o_refs[:3]
        r = pl.program_id(1)

        @pl.when(r == 0)
        def _():
            dst[...] = jnp.zeros_like(dst)

        lg = None if gated else x_ref[0:1, 0:1]

        def one_chunk(c, carry):
            cc = (ncb - 1 - c) if brev else c
            rows = pl.ds(pl.multiple_of(cc * chunk, chunk), chunk)
            for h in range(hb):
                ks = slice(h * dk, (h + 1) * dk)
                vs = slice(h * dv, (h + 1) * dv)
                qc = q_ref[rows, ks].astype(F32) * qscale
                kc = k_ref[rows, ks].astype(F32)
                vc = v_ref[rows, vs]
                gc = x_ref[rows, ks] if gated else None
                e, einv, dec, qd, ki, kdec, e_last, mask, tri = _chunk_terms(qc, kc, gc, lg, chunk, reverse)
                a = jnp.where(mask, _dot(qd, ki, NT), 0.0)
                s_t = sp_ref[cc, h]
                ds_t = dst[h]
                doc = do_ref[rows, vs]
                dvc = _dot(a, doc, TN) + _dot(kdec, ds_t, NT)
                da = jnp.where(mask, _dot(doc, vc, NT), 0.0)
                dqd = _dot(da, ki) + _dot(doc, s_t)
                dki = _dot(da, qd, TN)
                dkdec = _dot(vc, ds_t)
                dst[h] = ds_t * e_last + _dot(doc, qd, TN)
                dqc = dqd * e * qscale
                dkc = dki * einv + dkdec * dec
                if n_prev:
                    dqc = dqc + p_refs[0][rows, ks]
                    dkc = dkc + p_refs[1][rows, ks]
                    dvc = dvc + p_refs[2][rows, vs]
                dq_ref[rows, ks] = dqc
                dk_ref[rows, ks] = dkc
                dv_ref[rows, vs] = dvc
                if gated:
                    dcum = dqd * qd - dki * ki - dkdec * kdec
                    dlast = jnp.sum(dkdec * kdec, axis=0, keepdims=True) + e_last * jnp.sum(s_t * ds_t, axis=0, keepdims=True)
                    rid = lax.broadcasted_iota(jnp.int32, (chunk, 1), 0)
                    dcum = dcum + jnp.where(rid == (0 if reverse else chunk - 1), dlast, 0.0)
                    dgc = _dot3(tri.astype(F32), dcum, TN)
                    o_refs[3][rows, ks] = dgc
            return carry

        lax.fori_loop(0, ncb, one_chunk, 0)

    def spec(width, col):
        return _lin_specs(None, width, col, tb, nb, brev, per_head)

    in_specs = [spec(hb * dk, qcol), spec(hb * dk, kcol), spec(hb * dv, vcol)]
    args = [q, k, v]
    if gated:
        in_specs.append(spec(hb * dk, 0))
        args.append(g)
    else:
        in_specs.append(pl.BlockSpec((None, 1, LANE), lambda h, r: (h, 0, 0)))
        args.append(lgtab)
    in_specs.append(pl.BlockSpec((ncb, hb, dv, dk), lambda h, r: ((nb - 1 - r) if brev else r, h, 0, 0)))
    args.append(sprev)
    in_specs.append(spec(hb * dv, 0))
    args.append(do)
    out_specs = [spec(hb * dk, 0), spec(hb * dk, 0), spec(hb * dv, 0)]
    outs = [_sds((l, heads * dk), F32), _sds((l, heads * dk), F32), _sds((l, heads * dv), F32)]
    if gated:
        out_specs.append(spec(hb * dk, 0))
        outs.append(_sds((l, heads * dk), F32))
    if n_prev:
        in_specs += out_specs[:3]
        args += list(prev)
    return pl.pallas_call(
        body, name=name, grid=(ng, nb), in_specs=in_specs, out_specs=out_specs, out_shape=outs,
        scratch_shapes=[pltpu.VMEM((hb, dv, dk), F32)],
        compiler_params=_params(("arbitrary", "arbitrary")),
    )(*args)


def _headgate_fwd(name, o_f, o_b, og_arr, og_col, gn, dv):
    l, w = o_f.shape
    tm = _tile(l, 512, SUBLANE)
    nh = w // dv

    def body(of_ref, ob_ref, og_ref, gn_ref, out_ref):
        for h in range(nh):
            cs = slice(h * dv, (h + 1) * dv)
            o = of_ref[:, cs] + ob_ref[:, cs]
            r = lax.rsqrt(jnp.mean(o * o, axis=-1, keepdims=True) + EPS)
            og = og_ref[:, cs].astype(F32)
            out_ref[:, cs] = (o * r * gn_ref[:, cs] * (og * _sigmoid(og))).astype(out_ref.dtype)

    row = pl.BlockSpec((tm, w), lambda i: (i, 0))
    return pl.pallas_call(
        body, name=name, grid=(l // tm,),
        in_specs=[row, row, pl.BlockSpec((tm, w), lambda i: (i, og_col)), pl.BlockSpec((1, w), lambda i: (0, 0))],
        out_specs=row, out_shape=_sds((l, w), BF),
        compiler_params=_params(("arbitrary",)),
    )(o_f, o_b, og_arr, gn)


def _headgate_bwd(name, o_f, o_b, og_arr, og_col, gn, dout, dcol, dv):
    l, w = o_f.shape
    tm = _tile(l, 512, SUBLANE)
    nh = w // dv

    def body(of_ref, ob_ref, og_ref, gn_ref, d_ref, do_ref, dog_ref, dgn_ref):
        i = pl.program_id(0)
        for h in range(nh):
            cs = slice(h * dv, (h + 1) * dv)
            o = of_ref[:, cs] + ob_ref[:, cs]
            r = lax.rsqrt(jnp.mean(o * o, axis=-1, keepdims=True) + EPS)
            oh = o * r
            og = og_ref[:, cs].astype(F32)
            s = _sigmoid(og)
            d = d_ref[:, cs].astype(F32)
            gnv = gn_ref[:, cs]
            d_on = d * (og * s)
            dog_ref[:, cs] = (d * (oh * gnv) * s * (1.0 + og * (1.0 - s))).astype(dog_ref.dtype)
            doh = d_on * gnv
            do_ref[:, cs] = r * (doh - oh * jnp.mean(doh * oh, axis=-1, keepdims=True))
            part = jnp.sum(d_on * oh, axis=0, keepdims=True)

            @pl.when(i == 0)
            def _():
                dgn_ref[:, cs] = part

            @pl.when(i > 0)
            def _():
                dgn_ref[:, cs] += part

    row = pl.BlockSpec((tm, w), lambda i: (i, 0))
    vec = pl.BlockSpec((1, w), lambda i: (0, 0))
    return pl.pallas_call(
        body, name=name, grid=(l // tm,),
        in_specs=[row, row, pl.BlockSpec((tm, w), lambda i: (i, og_col)), vec, pl.BlockSpec((tm, w), lambda i: (i, dcol))],
        out_specs=[row, row, vec], out_shape=[_sds((l, w), F32), _sds((l, w), BF), _sds((1, w), F32)],
        compiler_params=_params(("arbitrary",)),
    )(o_f, o_b, og_arr, gn, dout)


def _rot_tables(l, dk):
    half = dk // 2
    pos = jnp.arange(l, dtype=F32)
    inv = jnp.exp(-math.log(ROPE_BASE) * jnp.arange(half, dtype=F32) / half)
    ang = pos[:, None] * inv[None, :]
    cos, sin = jnp.cos(ang), jnp.sin(ang)
    return jnp.concatenate([cos, cos], axis=-1), jnp.concatenate([-sin, sin], axis=-1)


def _rot_apply(name, src_q, qcol, src_k, kcol, cos_t, sin_t, heads, dk, kscale, out_dtype, transpose):
    l = src_q.shape[0]
    w = heads * dk
    tm = _tile(l, 512, SUBLANE)

    def rot(t, cos_v, sin_v):
        if transpose:
            return t * cos_v + pltpu.roll(t * sin_v, dk // 2, 1)
        return t * cos_v + pltpu.roll(t, dk // 2, 1) * sin_v

    def body(q_ref, k_ref, c_ref, s_ref, qo_ref, ko_ref):
        cos_v, sin_v = c_ref[...], s_ref[...]
        for h in range(heads):
            cs = slice(h * dk, (h + 1) * dk)
            qo_ref[:, cs] = rot(q_ref[:, cs].astype(F32), cos_v, sin_v).astype(out_dtype)
            ko_ref[:, cs] = (rot(k_ref[:, cs].astype(F32), cos_v, sin_v) * kscale).astype(out_dtype)

    tab = pl.BlockSpec((tm, dk), lambda i: (i, 0))
    row = pl.BlockSpec((tm, w), lambda i: (i, 0))
    return pl.pallas_call(
        body, name=name, grid=(l // tm,),
        in_specs=[pl.BlockSpec((tm, w), lambda i: (i, qcol)), pl.BlockSpec((tm, w), lambda i: (i, kcol)), tab, tab],
        out_specs=[row, row], out_shape=[_sds((l, w), out_dtype), _sds((l, w), out_dtype)],
        compiler_params=_params(("arbitrary",)),
    )(src_q, src_k, cos_t, sin_t)


def _exchange(name, src, gather):
    shape = src.shape if gather else src.shape[1:]

    def body(src_ref, out_ref, send_sems, recv_sems, local_sem):
        x, y, c = lax.axis_index("x"), lax.axis_index("y"), lax.axis_index("c")
        me = 4 * x + 2 * y + c
        own = pltpu.make_async_copy(src_ref if gather else src_ref.at[me], out_ref.at[me], local_sem)
        own.start()
        copies = []
        for kk in range(1, N_DEV):
            px = (1 - x) if kk & 4 else x
            py = (1 - y) if kk & 2 else y
            pc = (1 - c) if kk & 1 else c
            peer = 4 * px + 2 * py + pc
            copies.append(pltpu.make_async_remote_copy(
                src_ref=src_ref if gather else src_ref.at[peer], dst_ref=out_ref.at[me],
                send_sem=send_sems.at[kk - 1], recv_sem=recv_sems.at[kk - 1],
                device_id=(px, py, pc), device_id_type=pl.DeviceIdType.MESH))
        for cp in copies:
            cp.start()
        for cp in copies:
            cp.wait_recv()
        for cp in copies:
            cp.wait_send()
        own.wait()

    return pl.pallas_call(
        body, name=name,
        in_specs=[pl.BlockSpec(memory_space=pl.ANY)], out_specs=pl.BlockSpec(memory_space=pl.ANY),
        out_shape=_sds((N_DEV,) + tuple(shape), src.dtype),
        scratch_shapes=[pltpu.SemaphoreType.DMA((N_DEV - 1,)), pltpu.SemaphoreType.DMA((N_DEV - 1,)), pltpu.SemaphoreType.DMA],
        )(src)


def _adam_math(w, gsum, m, v):
    m2 = ADAM_B1 * m + (1.0 - ADAM_B1) * gsum
    v2 = ADAM_B2 * v + (1.0 - ADAM_B2) * (gsum * gsum)
    m_hat = m2 / (1.0 - ADAM_B1 ** ADAM_STEP)
    v_hat = v2 / (1.0 - ADAM_B2 ** ADAM_STEP)
    delta = -ADAM_LR * (m_hat / (jnp.sqrt(v_hat) + ADAM_EPS) + ADAM_WD * w)
    return delta, m2, v2


def _reduce_adam(name, parts, w, m, v):
    r, c = w.shape
    tr = _tile(r, 256, 16)

    def body(p_ref, w_ref, m_ref, v_ref, g_ref, d_ref, m2_ref, v2_ref):
        gsum = p_ref[0].astype(F32)
        for s in range(1, N_DEV):
            gsum = gsum + p_ref[s].astype(F32)
        g_ref[...] = gsum
        delta, m2, v2 = _adam_math(w_ref[...], gsum, m_ref[...], v_ref[...])
        d_ref[...] = delta
        m2_ref[...] = m2
        v2_ref[...] = v2

    row = pl.BlockSpec((tr, c), lambda i: (i, 0))
    return pl.pallas_call(
        body, name=name, grid=(r // tr,),
        in_specs=[pl.BlockSpec((N_DEV, tr, c), lambda i: (0, i, 0)), row, row, row],
        out_specs=[row, row, row, row], out_shape=[_sds((r, c), F32)] * 4,
        compiler_params=_params(("arbitrary",)),
    )(parts, w, m, v)


def _reduce8(name, parts):
    _, r, c = parts.shape

    def body(p_ref, g_ref):
        gsum = p_ref[0]
        for s in range(1, N_DEV):
            gsum = gsum + p_ref[s]
        g_ref[...] = gsum

    return pl.pallas_call(
        body, name=name, grid=(1,),
        in_specs=[pl.BlockSpec((N_DEV, r, c), lambda i: (0, 0, 0))],
        out_specs=pl.BlockSpec((r, c), lambda i: (0, 0)), out_shape=_sds((r, c), F32),
        compiler_params=_params(("arbitrary",)),
    )(parts)


def _adam_packed(name, w, g, m, v):
    r, c = w.shape

    def body(w_ref, g_ref, m_ref, v_ref, d_ref, m2_ref, v2_ref):
        delta, m2, v2 = _adam_math(w_ref[...], g_ref[...], m_ref[...], v_ref[...])
        d_ref[...] = delta
        m2_ref[...] = m2
        v2_ref[...] = v2

    spec = pl.BlockSpec((r, c), lambda i: (0, 0))
    return pl.pallas_call(
        body, name=name, grid=(1,), in_specs=[spec] * 4, out_specs=[spec] * 3, out_shape=[_sds((r, c), F32)] * 3,
        compiler_params=_params(("arbitrary",)),
    )(w, g, m, v)


def _pack(arrs):
    flat = jnp.concatenate([a.reshape(-1).astype(F32) for a in arrs])
    n = flat.shape[0]
    pad = (-n) % (SUBLANE * LANE)
    return jnp.pad(flat, (0, pad)).reshape(-1, LANE)


def _unpack(packed, like):
    flat = packed.reshape(-1)
    out, off = [], 0
    for a in like:
        n = math.prod(a.shape)
        out.append(flat[off:off + n].reshape(a.shape))
        off += n
    return out


def _row_blocks(full):
    return full.reshape(N_DEV, full.shape[0] // N_DEV, full.shape[1])


def _col_blocks(full):
    r, c = full.shape
    return full.reshape(r, N_DEV, c // N_DEV).transpose(1, 0, 2)


def _cols_natural(blocks):
    n, r, c = blocks.shape
    return blocks.transpose(1, 0, 2).reshape(r, n * c)


def kernel(x, ffn1_norm, ffn1_w1, ffn1_w2, mix_norm, ffn2_norm, ffn2_w1, ffn2_w2, ab_w_in, s5_lambda_re, s5_lambda_im, s5_b_re, s5_b_im, s5_c_re, s5_c_im, s5_log_dt, s5_d, s5_w_glu, gla_w_gk, gla_b_gk, gla_norm, ab_w_out, ret_w_in, ret_norm, ret_w_out, final_norm, loss_target, m_ffn1_norm, m_ffn1_w1, m_ffn1_w2, m_mix_norm, m_ffn2_norm, m_ffn2_w1, m_ffn2_w2, m_ab_w_in, m_s5_lambda_re, m_s5_lambda_im, m_s5_b_re, m_s5_b_im, m_s5_c_re, m_s5_c_im, m_s5_log_dt, m_s5_d, m_s5_w_glu, m_gla_w_gk, m_gla_b_gk, m_gla_norm, m_ab_w_out, m_ret_w_in, m_ret_norm, m_ret_w_out, m_final_norm, v_ffn1_norm, v_ffn1_w1, v_ffn1_w2, v_mix_norm, v_ffn2_norm, v_ffn2_w1, v_ffn2_w2, v_ab_w_in, v_s5_lambda_re, v_s5_lambda_im, v_s5_b_re, v_s5_b_im, v_s5_c_re, v_s5_c_im, v_s5_log_dt, v_s5_d, v_s5_w_glu, v_gla_w_gk, v_gla_b_gk, v_gla_norm, v_ab_w_out, v_ret_w_in, v_ret_norm, v_ret_w_out, v_final_norm):
    names = ['ffn1_norm', 'ffn1_w1', 'ffn1_w2', 'mix_norm', 'ffn2_norm', 'ffn2_w1', 'ffn2_w2', 'ab_w_in', 's5_lambda_re', 's5_lambda_im', 's5_b_re', 's5_b_im', 's5_c_re', 's5_c_im', 's5_log_dt', 's5_d', 's5_w_glu', 'gla_w_gk', 'gla_b_gk', 'gla_norm', 'ab_w_out', 'ret_w_in', 'ret_norm', 'ret_w_out', 'final_norm']
    loc = locals()
    W = {n: loc[n] for n in names}
    M = {n: loc["m_" + n] for n in names}
    V = {n: loc["v_" + n] for n in names}

    me = 4 * lax.axis_index("x") + 2 * lax.axis_index("y") + lax.axis_index("c")
    xs = x[0]
    tgt = loss_target[0]
    l, d = xs.shape
    depth = ffn1_norm.shape[0]

    def gather_cols(tag, w):
        g = _exchange("ag_" + tag, w.astype(BF), True)
        g = jnp.moveaxis(g, 0, -2)
        return g.reshape(g.shape[:-2] + (g.shape[-2] * g.shape[-1],))

    def gather_rows(tag, w):
        g = _exchange("ag_" + tag, w.astype(BF), True)
        g = jnp.moveaxis(g, 0, -3)
        return g.reshape(g.shape[:-3] + (g.shape[-3] * g.shape[-2], g.shape[-1]))

    full = {}
    for n in ['ffn1_w1', 'ffn2_w1', 'ab_w_in', 'ret_w_in']:
        full[n] = gather_cols(n, W[n])
    for n in ['ffn1_w2', 'ffn2_w2', 's5_w_glu', 'ab_w_out', 'ret_w_out']:
        full[n] = gather_rows(n, W[n])
    small_sharded = [gla_w_gk, gla_b_gk, ret_norm]
    got = _exchange("ag_small_weights", _pack(small_sharded), True)
    per_dev = [_unpack(got[p], small_sharded) for p in range(N_DEV)]
    full['gla_w_gk'] = jnp.concatenate([t[0] for t in per_dev], axis=-1).astype(BF)
    full['gla_b_gk'] = jnp.concatenate([t[1] for t in per_dev], axis=-1)
    ret_norm_full = jnp.concatenate([t[2] for t in per_dev], axis=-1)

    s5w = s5_d.shape[1]
    g_s5, n_s5 = s5_lambda_re.shape[2], s5_lambda_re.shape[3]
    hs = min(512, g_s5 * n_s5)
    nch = (g_s5 * n_s5) // hs
    cw = 2 * hs
    gla_hk = gla_w_gk.shape[-1] * N_DEV
    gla_dk = gla_hk // GLA_HEADS
    gla_hv = gla_norm.shape[1]
    gla_dv = gla_hv // GLA_HEADS
    ret_hv = ret_norm.shape[1] * N_DEV
    ret_dv = ret_hv // RET_HEADS
    ret_hk = (ret_w_in.shape[2] * N_DEV - 2 * ret_hv) // 2
    ret_dk = ret_hk // RET_HEADS
    assert s5w == gla_hv and 2 * gla_hk == s5w, "column blocks of the mixer projection assume these widths"
    assert ret_hv == 2 * ret_hk
    main_w = s5w + 2 * gla_hk + 2 * gla_hv
    gla_tb = _tile(l, 256, GLA_CHUNK)
    ret_chunk = min(RET_CHUNK, l)

    cos_t, sin_t = _rot_tables(l, ret_dk)
    lg_f = jnp.log1p(-jnp.exp2(-5.0 - jnp.arange(RET_HEADS, dtype=F32)))
    lgtab_f = jnp.broadcast_to(lg_f[:, None, None], (RET_HEADS, 1, LANE))
    lgtab_b = jnp.broadcast_to(lg_f[::-1][:, None, None], (RET_HEADS, 1, LANE))

    saved = []
    cur = xs
    for i in range(depth):
        j = i // 2
        s = {}
        s['x0'] = cur
        cur, s['ffn1'] = _ffn_fwd(f"l{i}_ffn1", cur, ffn1_norm[i:i + 1], full['ffn1_w1'][i], full['ffn1_w2'][i])
        s['x1'] = cur
        h = _rms_fwd(f"l{i}_mixnorm", cur, mix_norm[i:i + 1])
        s['h'] = h
        if i % 2 == 0:
            w_in = full['ab_w_in'][j]
            w_main, w_glo = w_in[:, :main_w], w_in[:, main_w:]
            proj = _mm_plain(f"l{i}_proj", h, w_main, NN, BF, tm=512, tn=1024, tk=d)
            glo = _mm_plain(f"l{i}_glo", h, w_glo, NN, F32, tm=1024, tn=2 * GLA_RANK, tk=d)
            s5_args = (s5_lambda_re[j], s5_lambda_im[j], s5_b_re[j], s5_b_im[j], s5_c_re[j], s5_c_im[j], s5_log_dt[j])
            (a_tab, bd, cd), s5_vjp = jax.vjp(lambda *a: _s5_tables(*a, hs), *s5_args)
            bd16, cd16 = bd.astype(BF), cd.astype(BF)
            tm = _tile(l, 512, SUBLANE)
            half = nch * cw
            tn_bu = _tile(2 * half, 1024, LANE)
            bu = _mm(f"l{i}_s5_bu",
                     [(proj, pl.BlockSpec((tm, s5w), lambda ii, jj, kk: (ii, 0)), bd16, pl.BlockSpec((s5w, tn_bu), lambda ii, jj, kk: (0, jj)))],
                     NN, (l // tm, (2 * half) // tn_bu, 1), [_sds((l, 2 * half), F32)],
                     [pl.BlockSpec((tm, tn_bu), lambda ii, jj, kk: (ii, jj))], (tm, tn_bu))[0]
            x_f = _scan(f"l{i}_s5_scan_f", bu, 0, nch, cw, a_tab[:, :half], False)
            x_b = _scan(f"l{i}_s5_scan_b", bu, nch, nch, cw, a_tab[:, half:], True)
            tk = _tile(half, 1024, LANE)
            nk = half // tk
            d_row = s5_d[j:j + 1]

            def epi_y(acc, e_refs, o_refs, ids):
                o_refs[0][...] = acc + e_refs[0][...].astype(F32) * e_refs[1][...]

            y_spec = pl.BlockSpec((tm, s5w), lambda ii, jj, kk: (ii, 0))
            y = _mm(f"l{i}_s5_y",
                    [(x_f, pl.BlockSpec((tm, tk), lambda ii, jj, kk: (ii, kk)), cd16, pl.BlockSpec((tk, s5w), lambda ii, jj, kk: (kk, 0))),
                     (x_b, pl.BlockSpec((tm, tk), lambda ii, jj, kk: (ii, kk)), cd16, pl.BlockSpec((tk, s5w), lambda ii, jj, kk: (kk + nk, 0)))],
                    NN, (l // tm, 1, nk), [_sds((l, s5w), F32)], [y_spec], (tm, s5w), epi_y,
                    [proj, d_row], [y_spec, pl.BlockSpec((1, s5w), lambda ii, jj, kk: (0, 0))])[0]
            s5_out = _glu_fwd(f"l{i}_s5_glu", y, full['s5_w_glu'][j])
            zeros_r = jnp.zeros((GLA_RANK, gla_hk), BF)
            w_gk = full['gla_w_gk'][j]
            wgk_f = jnp.concatenate([w_gk[0], zeros_r], axis=0)
            wgk_b = jnp.concatenate([zeros_r, w_gk[1]], axis=0)
            b_gk = full['gla_b_gk'][j]
            g_f, g_b = _gate_fwd(f"l{i}_gla_gate", glo, wgk_f, wgk_b, b_gk[0:1], b_gk[1:2])
            qcol, kcol, vcol, ogcol = s5w // gla_hk, s5w // gla_hk + 1, (s5w + 2 * gla_hk) // gla_hv, (s5w + 2 * gla_hk) // gla_hv + 1
            lin_kw = dict(heads=GLA_HEADS, hb=GLA_HEADS, dk=gla_dk, dv=gla_dv, chunk=GLA_CHUNK, tb=gla_tb, qcol=qcol, kcol=kcol, vcol=vcol,
                          qscale=gla_dk ** -0.5)
            o_f, sp_f = _lin_fwd(f"l{i}_gla_fwd_f", proj, proj, proj, g_f, None, reverse=False, **lin_kw)
            o_b, sp_b = _lin_fwd(f"l{i}_gla_fwd_b", proj, proj, proj, g_b, None, reverse=True, **lin_kw)
            gla_out = _headgate_fwd(f"l{i}_gla_out", o_f, o_b, proj, ogcol, gla_norm[j:j + 1], gla_dv)
            w_out = full['ab_w_out'][j]

            def epi_res(acc, e_refs, o_refs, ids):
                o_refs[0][...] = e_refs[0][...] + acc

            row = pl.BlockSpec((tm, d), lambda ii, jj, kk: (ii, 0))
            cur = _mm(f"l{i}_mix_out",
                      [(s5_out, pl.BlockSpec((tm, s5w), lambda ii, jj, kk: (ii, 0)), w_out, pl.BlockSpec((s5w, d), lambda ii, jj, kk: (0, 0))),
                       (gla_out, pl.BlockSpec((tm, gla_hv), lambda ii, jj, kk: (ii, 0)), w_out, pl.BlockSpec((gla_hv, d), lambda ii, jj, kk: (1, 0)))],
                      NN, (l // tm, 1, 1), [_sds((l, d), F32)], [row], (tm, d), epi_res, [cur], [row])[0]
            s.update(proj=proj, glo=glo, s5_vjp=s5_vjp, a_tab=a_tab, bd16=bd16, cd16=cd16, x_f=x_f, x_b=x_b, y=y, s5_out=s5_out,
                     wgk_f=wgk_f, wgk_b=wgk_b, b_gk=b_gk, g_f=g_f, g_b=g_b, o_f=o_f, o_b=o_b, sp_f=sp_f, sp_b=sp_b, gla_out=gla_out,
                     w_main=w_main, w_glo=w_glo, lin_kw=lin_kw, ogcol=ogcol)
        else:
            w_in = full['ret_w_in'][j]
            proj = _mm_plain(f"l{i}_proj", h, w_in, NN, BF, tm=512, tn=1024, tk=d)
            qr, kr = _rot_apply(f"l{i}_rot", proj, 0, proj, 1, cos_t, sin_t, RET_HEADS, ret_dk, ret_dk ** -0.5, BF, False)
            lin_kw = dict(heads=RET_HEADS, hb=1, dk=ret_dk, dv=ret_dv, chunk=ret_chunk, tb=ret_chunk, qcol=0, kcol=0, vcol=(2 * ret_hk) // ret_dv,
                          qscale=1.0)
            o_f, sp_f = _lin_fwd(f"l{i}_ret_fwd_f", qr, kr, proj, None, lgtab_f, reverse=False, **lin_kw)
            o_b, sp_b = _lin_fwd(f"l{i}_ret_fwd_b", qr, kr, proj, None, lgtab_b, reverse=True, **lin_kw)
            ogcol = (2 * ret_hk + ret_hv) // ret_hv
            r_out = _headgate_fwd(f"l{i}_ret_out", o_f, o_b, proj, ogcol, ret_norm_full, ret_dv)

            def epi_res(acc, e_refs, o_refs, ids):
                o_refs[0][...] = e_refs[0][...] + acc

            cur = _mm_plain(f"l{i}_mix_out", r_out, full['ret_w_out'][j], NN, F32, tm=512, tn=d, tk=1024, epi=epi_res, eins=[cur])
            s.update(proj=proj, qr=qr, kr=kr, o_f=o_f, o_b=o_b, sp_f=sp_f, sp_b=sp_b, r_out=r_out, lin_kw=lin_kw, ogcol=ogcol)
        s['x2'] = cur
        cur, s['ffn2'] = _ffn_fwd(f"l{i}_ffn2", cur, ffn2_norm[i:i + 1], full['ffn2_w1'][i], full['ffn2_w2'][i])
        saved.append(s)

    dx, d_final_norm, loss_row = _loss_head("loss_head", cur, final_norm.reshape(1, -1), tgt)
    loss = lax.psum(loss_row[0, 0], ("x", "y", "c"))

    G = {}
    big = {}
    G['final_norm'] = d_final_norm.reshape(-1)
    per_layer = {n: [None] * depth for n in ['ffn1_norm', 'mix_norm', 'ffn2_norm']}
    for n in ['ffn1_w1', 'ffn1_w2', 'ffn2_w1', 'ffn2_w2']:
        big[n] = [None] * depth
    for i in reversed(range(depth)):
        j = i // 2
        s = saved[i]
        dx, dg, dw1, dw2 = _ffn_bwd(f"l{i}_ffn2b", dx, s['x2'], ffn2_norm[i:i + 1], full['ffn2_w1'][i], full['ffn2_w2'][i], s['ffn2'])
        per_layer['ffn2_norm'][i] = dg[0]
        big['ffn2_w1'][i] = _col_blocks(dw1)
        big['ffn2_w2'][i] = _row_blocks(dw2)
        tm = _tile(l, 512, SUBLANE)
        row = pl.BlockSpec((tm, d), lambda ii, jj, kk: (ii, 0))
        vec = pl.BlockSpec((1, d), lambda ii, jj, kk: (0, 0))
        if i % 2 == 0:
            proj, lin_kw = s['proj'], s['lin_kw']
            w_out = full['ab_w_out'][j]
            d_cat = _mm_plain(f"l{i}_dcat", dx, w_out, NT, BF, tm=512, tn=1024, tk=d)
            dwo_a = _mm_plain(f"l{i}_dwout_a", s['s5_out'], dx, TN, BF, tm=s5w, tn=d, tk=512)
            dwo_b = _mm_plain(f"l{i}_dwout_b", s['gla_out'], dx, TN, BF, tm=gla_hv, tn=d, tk=512)
            big['ab_w_out'] = _row_blocks(jnp.concatenate([dwo_a, dwo_b], axis=0))
            do, dog, dgn = _headgate_bwd(f"l{i}_gla_outb", s['o_f'], s['o_b'], proj, s['ogcol'], gla_norm[j:j + 1], d_cat, 1, gla_dv)
            G['gla_norm'] = dgn
            r1 = _lin_bwd(f"l{i}_gla_bwd_f", proj, proj, proj, s['g_f'], None, s['sp_f'], do, None, reverse=False, **lin_kw)
            dq, dk_, dv_, dgf = r1
            r2 = _lin_bwd(f"l{i}_gla_bwd_b", proj, proj, proj, s['g_b'], None, s['sp_b'], do, (dq, dk_, dv_), reverse=True, **lin_kw)
            dq, dk_, dv_, dgb = r2
            dglo, dwf, dwb, dbf, dbb = _gate_bwd(f"l{i}_gla_gateb", s['glo'], s['wgk_f'], s['wgk_b'], s['b_gk'][0:1], s['b_gk'][1:2], dgf, dgb)
            G['gla_w_gk'] = jnp.stack([dwf[:GLA_RANK], dwb[GLA_RANK:]], axis=0)[None]
            G['gla_b_gk'] = jnp.concatenate([dbf, dbb], axis=0)[None]
            dy, dwglu = _glu_bwd(f"l{i}_s5_glub", s['y'], full['s5_w_glu'][j], d_cat, 0)
            big['s5_w_glu'] = _row_blocks(dwglu.astype(BF))
            half = nch * cw
            cd16, bd16 = s['cd16'], s['bd16']
            dxs = _mm_plain(f"l{i}_s5_dx", dy, cd16, NT, F32, tm=512, tn=1024, tk=s5w)
            dcd_f = _mm_plain(f"l{i}_s5_dcd_f", s['x_f'], dy, TN, F32, tm=1024, tn=s5w, tk=512)
            dcd_b = _mm_plain(f"l{i}_s5_dcd_b", s['x_b'], dy, TN, F32, tm=1024, tn=s5w, tk=512)
            a_conj = _conj_tab(s['a_tab'], hs)
            lam_f, da_f = _scan(f"l{i}_s5_adj_f", dxs, 0, nch, cw, a_conj[:, :half], True, xs=s['x_f'])
            lam_b, da_b = _scan(f"l{i}_s5_adj_b", dxs, nch, nch, cw, a_conj[:, half:], False, xs=s['x_b'])
            tk = _tile(l, 512, SUBLANE)
            u_spec = pl.BlockSpec((tk, s5w), lambda ii, jj, kk: (kk, 0))
            tn = _tile(half, 1024, LANE)
            dbd_f = _mm(f"l{i}_s5_dbd_f", [(proj, u_spec, lam_f, pl.BlockSpec((tk, tn), lambda ii, jj, kk: (kk, jj)))],
                        TN, (1, half // tn, l // tk), [_sds((s5w, half), F32)], [pl.BlockSpec((s5w, tn), lambda ii, jj, kk: (0, jj))], (s5w, tn))[0]
            dbd_b = _mm(f"l{i}_s5_dbd_b", [(proj, u_spec, lam_b, pl.BlockSpec((tk, tn), lambda ii, jj, kk: (kk, jj)))],
                        TN, (1, half // tn, l // tk), [_sds((s5w, half), F32)], [pl.BlockSpec((s5w, tn), lambda ii, jj, kk: (0, jj))], (s5w, tn))[0]
            d_row = s5_d[j:j + 1]

            def epi_du(acc, e_refs, o_refs, ids):
                dyv = e_refs[0][...]
                o_refs[0][...] = (acc + dyv * e_refs[1][...]).astype(BF)
                part = jnp.sum(dyv * e_refs[2][...].astype(F32), axis=0, keepdims=True)

                @pl.when(ids[0] == 0)
                def _():
                    o_refs[1][...] = part

                @pl.when(ids[0] > 0)
                def _():
                    o_refs[1][...] += part

            tkk = _tile(half, 1024, LANE)
            nkk = half // tkk
            u_row = pl.BlockSpec((tm, s5w), lambda ii, jj, kk: (ii, 0))
            u_vec = pl.BlockSpec((1, s5w), lambda ii, jj, kk: (0, 0))
            du, dd = _mm(f"l{i}_s5_du",
                         [(lam_f, pl.BlockSpec((tm, tkk), lambda ii, jj, kk: (ii, kk)), bd16, pl.BlockSpec((s5w, tkk), lambda ii, jj, kk: (0, kk))),
                          (lam_b, pl.BlockSpec((tm, tkk), lambda ii, jj, kk: (ii, kk)), bd16, pl.BlockSpec((s5w, tkk), lambda ii, jj, kk: (0, kk + nkk)))],
                         NT, (l // tm, 1, nkk), [_sds((l, s5w), BF), _sds((1, s5w), F32)], [u_row, u_vec], (tm, s5w), epi_du,
                         [dy, d_row, proj], [u_row, u_vec, u_row])
            G['s5_d'] = dd
            d_atab = jnp.concatenate([da_f, da_b], axis=1)
            d_bd = jnp.concatenate([dbd_f, dbd_b], axis=1)
            d_cd = jnp.concatenate([dcd_f, dcd_b], axis=0)
            g_lre, g_lim, g_bre, g_bim, g_cre, g_cim, g_ldt = s['s5_vjp']((d_atab, d_bd, d_cd))
            G['s5_lambda_re'], G['s5_lambda_im'], G['s5_b_re'], G['s5_b_im'] = g_lre[None], g_lim[None], g_bre[None], g_bim[None]
            G['s5_c_re'], G['s5_c_im'], G['s5_log_dt'] = g_cre[None], g_cim[None], g_ldt[None]
            dproj = jnp.concatenate([du, dq.astype(BF), dk_.astype(BF), dv_.astype(BF), dog], axis=1)
            r2w = 2 * GLA_RANK
            pairs = [(dproj, pl.BlockSpec((tm, main_w), lambda ii, jj, kk: (ii, 0)), s['w_main'], pl.BlockSpec((d, main_w), lambda ii, jj, kk: (0, 0))),
                     (dglo, pl.BlockSpec((tm, r2w), lambda ii, jj, kk: (ii, 0)), s['w_glo'], pl.BlockSpec((d, r2w), lambda ii, jj, kk: (0, 0)))]
            dx, dg = _mm(f"l{i}_dh", pairs, NT, (l // tm, 1, 1), [_sds((l, d), F32), _sds((1, d), F32)], [row, vec], (tm, d),
                         _rms_bwd_epi(0), [s['x1'], mix_norm[i:i + 1], dx], [row, vec, row])
            dw_main = _mm_plain(f"l{i}_dwin_main", s['h'], dproj, TN, BF, tm=d, tn=1024, tk=512)
            dw_glo = _mm_plain(f"l{i}_dwin_glo", s['h'], dglo, TN, BF, tm=d, tn=r2w, tk=512)
            big['ab_w_in'] = _col_blocks(jnp.concatenate([dw_main, dw_glo], axis=1))
        else:
            proj, lin_kw = s['proj'], s['lin_kw']
            w_out = full['ret_w_out'][j]
            d_ro = _mm_plain(f"l{i}_dro", dx, w_out, NT, BF, tm=512, tn=1024, tk=d)
            dwo = _mm_plain(f"l{i}_dwout", s['r_out'], dx, TN, BF, tm=1024, tn=d, tk=512)
            big['ret_w_out'] = _row_blocks(dwo)
            do, dog, dgn = _headgate_bwd(f"l{i}_ret_outb", s['o_f'], s['o_b'], proj, s['ogcol'], ret_norm_full, d_ro, 0, ret_dv)
            G['ret_norm'] = dgn
            r1 = _lin_bwd(f"l{i}_ret_bwd_f", s['qr'], s['kr'], proj, None, lgtab_f, s['sp_f'], do, None, reverse=False, **lin_kw)
            r2 = _lin_bwd(f"l{i}_ret_bwd_b", s['qr'], s['kr'], proj, None, lgtab_b, s['sp_b'], do, r1, reverse=True, **lin_kw)
            dqr, dkr, dv_ = r2
            dq, dk_ = _rot_apply(f"l{i}_rotb", dqr, 0, dkr, 0, cos_t, sin_t, RET_HEADS, ret_dk, ret_dk ** -0.5, BF, True)
            dproj = jnp.concatenate([dq, dk_, dv_.astype(BF), dog], axis=1)
            dx, dg = _mm_plain(f"l{i}_dh", dproj, full['ret_w_in'][j], NT, F32, tm=512, tn=d, tk=1024, epi=_rms_bwd_epi(0),
                               eins=[s['x1'], mix_norm[i:i + 1], dx], especs=[None, vec, None],
                               extra_outs=[_sds((1, d), F32)], extra_specs=[vec])
            dw_in = _mm_plain(f"l{i}_dwin", s['h'], dproj, TN, BF, tm=d, tn=1024, tk=512)
            big['ret_w_in'] = _col_blocks(dw_in)
        per_layer['mix_norm'][i] = dg[0]
        dx, dg, dw1, dw2 = _ffn_bwd(f"l{i}_ffn1b", dx, s['x0'], ffn1_norm[i:i + 1], full['ffn1_w1'][i], full['ffn1_w2'][i], s['ffn1'])
        per_layer['ffn1_norm'][i] = dg[0]
        big['ffn1_w1'][i] = _col_blocks(dw1)
        big['ffn1_w2'][i] = _row_blocks(dw2)
    for n in per_layer:
        G[n] = jnp.stack(per_layer[n], axis=0)
    grad_x = dx[None]

    out_g, out_d, out_m, out_v = {}, {}, {}, {}

    def big_update(tag, blocks, w, m, v):
        recv = _exchange("a2a_" + tag, blocks, False)
        return _reduce_adam("upd_" + tag, recv, w, m, v)

    for n in ['ffn1_w1', 'ffn1_w2', 'ffn2_w1', 'ffn2_w2']:
        res = [big_update(f"{n}_{i}", big[n][i], W[n][i], M[n][i], V[n][i]) for i in range(depth)]
        out_g[n], out_d[n], out_m[n], out_v[n] = [jnp.stack([r[t] for r in res], axis=0) for t in range(4)]
    for n in ['ab_w_in', 's5_w_glu', 'ab_w_out', 'ret_w_in', 'ret_w_out']:
        res = big_update(n, big[n], W[n][0], M[n][0], V[n][0])
        out_g[n], out_d[n], out_m[n], out_v[n] = [r[None] for r in res]

    small = ['ffn1_norm', 'mix_norm', 'ffn2_norm', 's5_lambda_re', 's5_lambda_im', 's5_b_re', 's5_b_im', 's5_c_re', 's5_c_im',
             's5_log_dt', 's5_d', 'gla_w_gk', 'gla_b_gk', 'gla_norm', 'ret_norm', 'final_norm']
    packed = _pack([G[n] for n in small])
    gathered = _exchange("ag_small_grads", packed, True)
    summed = _reduce8("sum_small_grads", gathered)
    g_full = dict(zip(small, _unpack(summed, [G[n] for n in small])))
    g_small = {}
    for n in small:
        gf = g_full[n]
        if n in ('gla_w_gk', 'gla_b_gk', 'ret_norm'):
            width = W[n].shape[-1]
            gf = lax.dynamic_slice_in_dim(gf, me * width, width, axis=gf.ndim - 1)
        g_small[n] = gf.reshape(W[n].shape)
    pw, pg, pm, pv = (_pack([src[n] for n in small]) for src in (W, g_small, M, V))
    pd, pm2, pv2 = _adam_packed("upd_small", pw, pg, pm, pv)
    like = [W[n] for n in small]
    for n, dd_, mm_, vv_ in zip(small, _unpack(pd, like), _unpack(pm2, like), _unpack(pv2, like)):
        out_g[n], out_d[n], out_m[n], out_v[n] = g_small[n], dd_, mm_, vv_

    return (loss, grad_x, *[out_g[n] for n in names], *[out_d[n] for n in names], *[out_m[n] for n in names], *[out_v[n] for n in names])
```

```python
import functools
import math

import jax
import jax.numpy as jnp
from jax import lax
from jax.experimental import pallas as pl
from jax.experimental.pallas import tpu as pltpu

F32 = jnp.float32
BF = jnp.bfloat16
N_DEV = 8
EPS = 1e-6
S5_GROUP = 16
GLA_HEADS = 4
GLA_RANK = 16
GLA_GATE_NORM = 16.0
RET_HEADS = 8
ROPE_BASE = 10000.0
GLA_CHUNK = 64
RET_CHUNK = 256
ADAM_LR, ADAM_B1, ADAM_B2, ADAM_EPS, ADAM_WD, ADAM_STEP = 0.001, 0.9, 0.999, 1e-08, 0.01, 10
VMEM_LIMIT_BYTES = 56 * 1024 * 1024
LANE = 128
SUBLANE = 8

NN = (((1,), (0,)), ((), ()))
NT = (((1,), (1,)), ((), ()))
TN = (((0,), (0,)), ((), ()))


def _tile(n, pref, align):
    if n <= pref:
        return n
    t = (pref // align) * align
    while t >= align:
        if n % t == 0:
            return t
        t -= align
    return n


def _params(sem):
    return pltpu.CompilerParams(dimension_semantics=sem, vmem_limit_bytes=VMEM_LIMIT_BYTES)


def _dot(a, b, dims=NN):
    return lax.dot_general(a.astype(BF), b.astype(BF), dims, preferred_element_type=F32)


def _dot3(m01, g, dims=NN):
    g1 = g.astype(BF)
    r1 = g - g1.astype(F32)
    g2 = r1.astype(BF)
    g3 = (r1 - g2.astype(F32)).astype(BF)
    m = m01.astype(BF)
    return (lax.dot_general(m, g1, dims, preferred_element_type=F32)
            + lax.dot_general(m, g2, dims, preferred_element_type=F32)
            + lax.dot_general(m, g3, dims, preferred_element_type=F32))


def _sigmoid(x):
    return 1.0 / (1.0 + jnp.exp(-x))


def _mm(name, pairs, dims, grid, outs, out_specs, acc_shape, epi=None, eins=(), especs=()):
    n_p, n_e, n_o = len(pairs), len(eins), len(outs)
    nk = grid[2]

    def body(*refs):
        a_refs = refs[0:2 * n_p:2]
        b_refs = refs[1:2 * n_p:2]
        e_refs = refs[2 * n_p:2 * n_p + n_e]
        o_refs = refs[2 * n_p + n_e:2 * n_p + n_e + n_o]
        acc = refs[-1]
        ids = (pl.program_id(0), pl.program_id(1), pl.program_id(2))

        part = _dot(a_refs[0][...], b_refs[0][...], dims)
        for p in range(1, n_p):
            part = part + _dot(a_refs[p][...], b_refs[p][...], dims)

        def finish(total):
            if epi is None:
                o_refs[0][...] = total.astype(o_refs[0].dtype)
            else:
                epi(total, e_refs, o_refs, ids)

        if nk == 1:
            finish(part)
        else:
            @pl.when(ids[2] == 0)
            def _():
                acc[...] = part

            @pl.when(ids[2] > 0)
            def _():
                acc[...] += part

            @pl.when(ids[2] == nk - 1)
            def _():
                finish(acc[...])

    in_specs, args = [], []
    for a, a_spec, b, b_spec in pairs:
        in_specs += [a_spec, b_spec]
        args += [a, b]
    in_specs += list(especs)
    args += list(eins)
    res = pl.pallas_call(
        body, name=name, grid=grid, in_specs=in_specs, out_specs=list(out_specs), out_shape=list(outs),
        scratch_shapes=[pltpu.VMEM(acc_shape, F32)],
        compiler_params=_params(("arbitrary", "arbitrary", "arbitrary")),
    )(*args)
    return res


def _sds(shape, dtype):
    return jax.ShapeDtypeStruct(shape, dtype)


def _mm_plain(name, a, b, dims, out_dtype, tm=512, tn=1024, tk=1024, epi=None, eins=(), especs=None, extra_outs=(), extra_specs=()):
    if dims == NN:
        (m, k), n = a.shape, b.shape[1]
    elif dims == NT:
        (m, k), n = a.shape, b.shape[0]
    else:
        (k, m), n = a.shape, b.shape[1]
    tm, tn = _tile(m, tm, LANE if dims == TN else SUBLANE), _tile(n, tn, LANE)
    tk = _tile(k, tk, SUBLANE if dims == TN else LANE)
    grid = (m // tm, n // tn, k // tk)
    if dims == NN:
        a_spec = pl.BlockSpec((tm, tk), lambda i, j, kk: (i, kk))
        b_spec = pl.BlockSpec((tk, tn), lambda i, j, kk: (kk, j))
    elif dims == NT:
        a_spec = pl.BlockSpec((tm, tk), lambda i, j, kk: (i, kk))
        b_spec = pl.BlockSpec((tn, tk), lambda i, j, kk: (j, kk))
    else:
        a_spec = pl.BlockSpec((tk, tm), lambda i, j, kk: (kk, i))
        b_spec = pl.BlockSpec((tk, tn), lambda i, j, kk: (kk, j))
    o_spec = pl.BlockSpec((tm, tn), lambda i, j, kk: (i, j))
    if especs is None:
        especs = [o_spec] * len(eins)
    else:
        especs = [o_spec if s is None else s for s in especs]
    res = _mm(name, [(a, a_spec, b, b_spec)], dims, grid, [_sds((m, n), out_dtype)] + list(extra_outs),
              [o_spec] + list(extra_specs), (tm, tn), epi, eins, especs)
    return res if extra_outs else res[0]


def _rms_fwd(name, x, g):
    l, d = x.shape
    tm = _tile(l, 1024, SUBLANE)

    def body(x_ref, g_ref, o_ref):
        xv = x_ref[...]
        r = lax.rsqrt(jnp.mean(xv * xv, axis=-1, keepdims=True) + EPS)
        o_ref[...] = (xv * r * g_ref[...]).astype(o_ref.dtype)

    return pl.pallas_call(
        body, name=name, grid=(l // tm,),
        in_specs=[pl.BlockSpec((tm, d), lambda i: (i, 0)), pl.BlockSpec((1, d), lambda i: (0, 0))],
        out_specs=pl.BlockSpec((tm, d), lambda i: (i, 0)), out_shape=_sds((l, d), BF),
        compiler_params=_params(("arbitrary",)),
    )(x, g)


def _rms_bwd_epi(first_axis):
    def epi(acc, e_refs, o_refs, ids):
        x_ref, g_ref, dr_ref = e_refs
        dx_ref, dg_ref = o_refs
        xv = x_ref[...]
        r = lax.rsqrt(jnp.mean(xv * xv, axis=-1, keepdims=True) + EPS)
        xh = xv * r
        dxh = acc * g_ref[...]
        dx_ref[...] = dr_ref[...] + r * (dxh - xh * jnp.mean(dxh * xh, axis=-1, keepdims=True))
        part = jnp.sum(acc * xh, axis=0, keepdims=True)

        @pl.when(ids[first_axis] == 0)
        def _():
            dg_ref[...] = part

        @pl.when(ids[first_axis] > 0)
        def _():
            dg_ref[...] += part

    return epi


def _loss_head(name, x, g, target):
    l, d = x.shape
    tm = _tile(l, 512, SUBLANE)
    n = l // tm

    def body(x_ref, g_ref, t_ref, dx_ref, dg_ref, loss_ref, lacc):
        i = pl.program_id(0)
        xv = x_ref[...]
        r = lax.rsqrt(jnp.mean(xv * xv, axis=-1, keepdims=True) + EPS)
        xh = xv * r
        e = xh * g_ref[...] - t_ref[...]
        dy = e * (1.0 / d)
        dxh = dy * g_ref[...]
        dx_ref[...] = r * (dxh - xh * jnp.mean(dxh * xh, axis=-1, keepdims=True))
        dg_part = jnp.sum(dy * xh, axis=0, keepdims=True)
        l_part = jnp.sum(e * e, axis=0, keepdims=True)

        @pl.when(i == 0)
        def _():
            dg_ref[...] = dg_part
            lacc[...] = l_part

        @pl.when(i > 0)
        def _():
            dg_ref[...] += dg_part
            lacc[...] += l_part

        @pl.when(i == n - 1)
        def _():
            loss_ref[...] = jnp.zeros_like(loss_ref) + jnp.sum(lacc[...]) * (0.5 / d)

    return pl.pallas_call(
        body, name=name, grid=(n,),
        in_specs=[pl.BlockSpec((tm, d), lambda i: (i, 0)), pl.BlockSpec((1, d), lambda i: (0, 0)),
                  pl.BlockSpec((tm, d), lambda i: (i, 0))],
        out_specs=[pl.BlockSpec((tm, d), lambda i: (i, 0)), pl.BlockSpec((1, d), lambda i: (0, 0)),
                   pl.BlockSpec((1, LANE), lambda i: (0, 0))],
        out_shape=[_sds((l, d), F32), _sds((1, d), F32), _sds((1, LANE), F32)],
        scratch_shapes=[pltpu.VMEM((1, d), F32)],
        compiler_params=_params(("arbitrary",)),
    )(x, g, target)


def _ffn_up(name, hn, w1):
    l, d = hn.shape
    f = w1.shape[1] // 2
    tm, tn = _tile(l, 512, SUBLANE), _tile(f, 1408, LANE)
    nj = f // tn

    def body(h_ref, wg_ref, wu_ref, gu_ref, a_ref):
        h = h_ref[...]
        g = jnp.dot(h, wg_ref[...], preferred_element_type=F32)
        u = jnp.dot(h, wu_ref[...], preferred_element_type=F32)
        gu_ref[0] = g.astype(BF)
        gu_ref[1] = u.astype(BF)
        a_ref[...] = (g * _sigmoid(g) * u).astype(BF)

    return pl.pallas_call(
        body, name=name, grid=(nj, l // tm),
        in_specs=[pl.BlockSpec((tm, d), lambda j, i: (i, 0)), pl.BlockSpec((d, tn), lambda j, i: (0, j)),
                  pl.BlockSpec((d, tn), lambda j, i: (0, j + nj))],
        out_specs=[pl.BlockSpec((2, tm, tn), lambda j, i: (0, i, j)), pl.BlockSpec((tm, tn), lambda j, i: (i, j))],
        out_shape=[_sds((2, l, f), BF), _sds((l, f), BF)],
        compiler_params=_params(("arbitrary", "arbitrary")),
    )(hn, w1, w1)


def _ffn_fwd(tag, x, gnorm, w1, w2):
    hn = _rms_fwd(tag + "_norm", x, gnorm)
    gu, a = _ffn_up(tag + "_up", hn, w1)

    def epi(acc, e_refs, o_refs, ids):
        o_refs[0][...] = e_refs[0][...] + 0.5 * acc

    l, d = x.shape
    x_new = _mm_plain(tag + "_down", a, w2, NN, F32, tm=512, tn=d, tk=1408, epi=epi, eins=[x])
    return x_new, (hn, gu, a)


def _ffn_bwd(tag, dres, x, gnorm, w1, w2, saved):
    hn, gu, a = saved
    l, d = x.shape
    f = w2.shape[0]
    tm, tn = _tile(l, 512, SUBLANE), _tile(f, 1408, LANE)
    nj = f // tn

    def epi_gu(acc, e_refs, o_refs, ids):
        g = e_refs[0][0].astype(F32)
        u = e_refs[0][1].astype(F32)
        s = _sigmoid(g)
        da = 0.5 * acc
        o_refs[0][0] = (da * u * s * (1.0 + g * (1.0 - s))).astype(BF)
        o_refs[0][1] = (da * g * s).astype(BF)

    gu_spec = pl.BlockSpec((2, tm, tn), lambda i, j, kk: (0, i, j))
    dgu = _mm(tag + "_dgu",
              [(dres, pl.BlockSpec((tm, d), lambda i, j, kk: (i, 0)), w2, pl.BlockSpec((tn, d), lambda i, j, kk: (j, 0)))],
              NT, (l // tm, nj, 1), [_sds((2, l, f), BF)], [gu_spec], (tm, tn), epi_gu, [gu], [gu_spec])[0]

    def epi_half(acc, e_refs, o_refs, ids):
        o_refs[0][...] = (0.5 * acc).astype(BF)

    dw2 = _mm_plain(tag + "_dw2", a, dres, TN, BF, tm=1408, tn=d, tk=512, epi=epi_half)

    tk = _tile(l, 512, SUBLANE)
    dw1 = _mm(tag + "_dw1",
              [(hn, pl.BlockSpec((tk, d), lambda i, j, kk: (kk, 0)),
                dgu, pl.BlockSpec((None, tk, tn), lambda i, j, kk: (j // nj, kk, j % nj)))],
              TN, (1, 2 * nj, l // tk), [_sds((d, 2 * f), BF)], [pl.BlockSpec((d, tn), lambda i, j, kk: (0, j))], (d, tn))[0]

    row = pl.BlockSpec((tm, d), lambda i, j, kk: (i, 0))
    vec = pl.BlockSpec((1, d), lambda i, j, kk: (0, 0))
    dx, dg = _mm(tag + "_dhn",
                 [(dgu, pl.BlockSpec((None, tm, tn), lambda i, j, kk: (kk // nj, i, kk % nj)),
                   w1, pl.BlockSpec((d, tn), lambda i, j, kk: (0, kk)))],
                 NT, (l // tm, 1, 2 * nj), [_sds((l, d), F32), _sds((1, d), F32)], [row, vec], (tm, d),
                 _rms_bwd_epi(0), [x, gnorm, dres], [row, vec, row])
    return dx, dg, dw1, dw2


def _scan(name, bu, chunk0, nch, cw, a_tab, reverse, xs=None):
    l = bu.shape[0]
    hw = cw // 2
    tb = _tile(l, 512, SUBLANE)
    nb = l // tb
    with_acc = xs is not None

    def body(*refs):
        if with_acc:
            bu_ref, a_ref, xs_ref, x_ref, acc_ref, st_ref = refs
        else:
            bu_ref, a_ref, x_ref, st_ref = refs
        r = pl.program_id(1)

        @pl.when(r == 0)
        def _():
            st_ref[...] = jnp.zeros_like(st_ref)
            if with_acc:
                acc_ref[...] = jnp.zeros_like(acc_ref)

        ar = a_ref[:, :hw]
        ai = a_ref[:, hw:]

        def step(t, carry):
            tt = (tb - 1 - t) if reverse else t
            sr, si = carry[0], carry[1]
            row = bu_ref[pl.ds(tt, 1), :]
            out = list(carry)
            if with_acc:
                xrow = xs_ref[pl.ds(tt, 1), :]
                xr, xi = xrow[:, :hw], xrow[:, hw:]
                out[2] = carry[2] + sr * xr + si * xi
                out[3] = carry[3] + si * xr - sr * xi
            nr = ar * sr - ai * si + row[:, :hw]
            ni = ar * si + ai * sr + row[:, hw:]
            x_ref[pl.ds(tt, 1), pl.ds(0, hw)] = nr
            x_ref[pl.ds(tt, 1), pl.ds(hw, hw)] = ni
            out[0], out[1] = nr, ni
            return tuple(out)

        init = [st_ref[:, :hw], st_ref[:, hw:]]
        if with_acc:
            init += [jnp.zeros((1, hw), F32), jnp.zeros((1, hw), F32)]
        fin = lax.fori_loop(0, tb, step, tuple(init), unroll=8)
        st_ref[:, pl.ds(0, hw)] = fin[0]
        st_ref[:, pl.ds(hw, hw)] = fin[1]
        if with_acc:
            acc_ref[:, pl.ds(0, hw)] += fin[2]
            acc_ref[:, pl.ds(hw, hw)] += fin[3]

    def rows(c, r):
        return ((nb - 1 - r) if reverse else r)

    in_specs = [pl.BlockSpec((tb, cw), lambda c, r: (rows(c, r), chunk0 + c)), pl.BlockSpec((1, cw), lambda c, r: (0, c))]
    args = [bu, a_tab]
    out_specs = [pl.BlockSpec((tb, cw), lambda c, r: (rows(c, r), c))]
    outs = [_sds((l, nch * cw), F32)]
    if with_acc:
        in_specs.append(pl.BlockSpec((tb, cw), lambda c, r: (rows(c, r), c)))
        args.append(xs)
        out_specs.append(pl.BlockSpec((1, cw), lambda c, r: (0, c)))
        outs.append(_sds((1, nch * cw), F32))
    res = pl.pallas_call(
        body, name=name, grid=(nch, nb), in_specs=in_specs, out_specs=out_specs, out_shape=outs,
        scratch_shapes=[pltpu.VMEM((1, cw), F32)],
        compiler_params=_params(("arbitrary", "arbitrary")),
    )(*args)
    return res if with_acc else res[0]


def _s5_tables(lam_re, lam_im, b_re, b_im, c_re, c_im, log_dt, hs):
    f32 = F32
    g, n = lam_re.shape[1], lam_re.shape[2]
    p = b_re.shape[-1]
    nch = (g * n) // hs
    lr = jnp.minimum(lam_re.astype(f32), -1e-4)
    li = lam_im.astype(f32)
    dt = jnp.exp(log_dt.astype(f32))[..., None]
    mag = jnp.exp(lr * dt)
    ar = mag * jnp.cos(li * dt)
    ai = mag * jnp.sin(li * dt)
    den = lr * lr + li * li
    cr = ((ar - 1.0) * lr + ai * li) / den
    ci = (ai * lr - (ar - 1.0) * li) / den
    bbr = cr[..., None] * b_re - ci[..., None] * b_im
    bbi = cr[..., None] * b_im + ci[..., None] * b_re
    eye = jnp.eye(g, dtype=f32)
    a_tab = jnp.stack([ar, ai], axis=1).reshape(2, 2, nch, hs).transpose(0, 2, 1, 3).reshape(1, -1)
    bb = jnp.stack([bbr, bbi], axis=1)
    bd = (bb[:, :, :, :, :, None] * eye[None, None, :, None, None, :])
    bd = bd.transpose(5, 4, 0, 1, 2, 3).reshape(g * p, 2, 2, nch, hs).transpose(0, 1, 3, 2, 4).reshape(g * p, -1)
    cc = jnp.stack([c_re, -c_im], axis=1)
    cd = (cc[:, :, :, :, :, None] * eye[None, None, :, None, None, :])
    cd = cd.transpose(0, 1, 2, 4, 5, 3).reshape(2, 2, nch, hs, g * p).transpose(0, 2, 1, 3, 4).reshape(-1, g * p)
    return a_tab, bd, cd


def _conj_tab(a_tab, hs):
    t = a_tab.reshape(-1, 2, hs)
    return jnp.stack([t[:, 0], -t[:, 1]], axis=1).reshape(1, -1)


def _gelu(y):
    c = math.sqrt(2.0 / math.pi)
    return 0.5 * y * (1.0 + jnp.tanh(c * (y + 0.044715 * y * y * y)))


def _gelu_grad(y):
    c = math.sqrt(2.0 / math.pi)
    th = jnp.tanh(c * (y + 0.044715 * y * y * y))
    return 0.5 * (1.0 + th) + 0.5 * y * (1.0 - th * th) * c * (1.0 + 3.0 * 0.044715 * y * y)


def _glu_fwd(name, y, w):
    l, wd = y.shape
    tm = _tile(l, 512, SUBLANE)

    def body(y_ref, w_ref, o_ref):
        gy = _gelu(y_ref[...])
        z = _dot(gy, w_ref[...])
        o_ref[...] = (gy * _sigmoid(z)).astype(o_ref.dtype)

    return pl.pallas_call(
        body, name=name, grid=(l // tm,),
        in_specs=[pl.BlockSpec((tm, wd), lambda i: (i, 0)), pl.BlockSpec((wd, wd), lambda i: (0, 0))],
        out_specs=pl.BlockSpec((tm, wd), lambda i: (i, 0)), out_shape=_sds((l, wd), BF),
        compiler_params=_params(("arbitrary",)),
    )(y, w)


def _glu_bwd(name, y, w, dout, dcol):
    l, wd = y.shape
    tm = _tile(l, 512, SUBLANE)

    def body(y_ref, w_ref, d_ref, dy_ref, dw_ref):
        i = pl.program_id(0)
        yv = y_ref[...]
        gy = _gelu(yv)
        s = _sigmoid(_dot(gy, w_ref[...]))
        d = d_ref[...].astype(F32)
        t = d * gy * s * (1.0 - s)
        dgy = d * s + _dot(t, w_ref[...], NT)
        dy_ref[...] = dgy * _gelu_grad(yv)
        part = _dot(gy, t, TN)

        @pl.when(i == 0)
        def _():
            dw_ref[...] = part

        @pl.when(i > 0)
        def _():
            dw_ref[...] += part

    return pl.pallas_call(
        body, name=name, grid=(l // tm,),
        in_specs=[pl.BlockSpec((tm, wd), lambda i: (i, 0)), pl.BlockSpec((wd, wd), lambda i: (0, 0)),
                  pl.BlockSpec((tm, wd), lambda i: (i, dcol))],
        out_specs=[pl.BlockSpec((tm, wd), lambda i: (i, 0)), pl.BlockSpec((wd, wd), lambda i: (0, 0))],
        out_shape=[_sds((l, wd), F32), _sds((wd, wd), F32)],
        compiler_params=_params(("arbitrary",)),
    )(y, w, dout)


def _log_sigmoid(x):
    return jnp.minimum(x, 0.0) - jnp.log(1.0 + jnp.exp(-jnp.abs(x)))


def _gate_fwd(name, glo, wf, wb, bf, bb):
    l, r2 = glo.shape
    hk = wf.shape[1]
    tm = _tile(l, 1024, SUBLANE)

    def body(x_ref, wf_ref, wb_ref, bf_ref, bb_ref, gf_ref, gb_ref):
        xv = x_ref[...]
        gf_ref[...] = _log_sigmoid(_dot(xv, wf_ref[...]) + bf_ref[...]) * (1.0 / GLA_GATE_NORM)
        gb_ref[...] = _log_sigmoid(_dot(xv, wb_ref[...]) + bb_ref[...]) * (1.0 / GLA_GATE_NORM)

    w_spec = pl.BlockSpec((r2, hk), lambda i: (0, 0))
    b_spec = pl.BlockSpec((1, hk), lambda i: (0, 0))
    o_spec = pl.BlockSpec((tm, hk), lambda i: (i, 0))
    return pl.pallas_call(
        body, name=name, grid=(l // tm,),
        in_specs=[pl.BlockSpec((tm, r2), lambda i: (i, 0)), w_spec, w_spec, b_spec, b_spec],
        out_specs=[o_spec, o_spec], out_shape=[_sds((l, hk), F32), _sds((l, hk), F32)],
        compiler_params=_params(("arbitrary",)),
    )(glo, wf, wb, bf, bb)


def _gate_bwd(name, glo, wf, wb, bf, bb, dgf, dgb):
    l, r2 = glo.shape
    hk = wf.shape[1]
    tm = _tile(l, 1024, SUBLANE)

    def body(x_ref, wf_ref, wb_ref, bf_ref, bb_ref, dgf_ref, dgb_ref, dx_ref, dwf_ref, dwb_ref, dbf_ref, dbb_ref):
        i = pl.program_id(0)
        xv = x_ref[...]
        kf = _dot(xv, wf_ref[...]) + bf_ref[...]
        kb = _dot(xv, wb_ref[...]) + bb_ref[...]
        dkf = dgf_ref[...] * (1.0 / GLA_GATE_NORM) * _sigmoid(-kf)
        dkb = dgb_ref[...] * (1.0 / GLA_GATE_NORM) * _sigmoid(-kb)
        dx_ref[...] = _dot(dkf, wf_ref[...], NT) + _dot(dkb, wb_ref[...], NT)
        parts = (_dot(xv, dkf, TN), _dot(xv, dkb, TN), jnp.sum(dkf, axis=0, keepdims=True), jnp.sum(dkb, axis=0, keepdims=True))
        accs = (dwf_ref, dwb_ref, dbf_ref, dbb_ref)

        @pl.when(i == 0)
        def _():
            for a_, p_ in zip(accs, parts):
                a_[...] = p_

        @pl.when(i > 0)
        def _():
            for a_, p_ in zip(accs, parts):
                a_[...] += p_

    w_spec = pl.BlockSpec((r2, hk), lambda i: (0, 0))
    b_spec = pl.BlockSpec((1, hk), lambda i: (0, 0))
    g_spec = pl.BlockSpec((tm, hk), lambda i: (i, 0))
    x_spec = pl.BlockSpec((tm, r2), lambda i: (i, 0))
    return pl.pallas_call(
        body, name=name, grid=(l // tm,),
        in_specs=[x_spec, w_spec, w_spec, b_spec, b_spec, g_spec, g_spec],
        out_specs=[x_spec, w_spec, w_spec, b_spec, b_spec],
        out_shape=[_sds((l, r2), F32), _sds((r2, hk), F32), _sds((r2, hk), F32), _sds((1, hk), F32), _sds((1, hk), F32)],
        compiler_params=_params(("arbitrary",)),
    )(glo, wf, wb, bf, bb, dgf, dgb)


def _chunk_terms(qc, kc, gc, lg, chunk, reverse):
    ri = lax.broadcasted_iota(jnp.int32, (chunk, chunk), 0)
    ci = lax.broadcasted_iota(jnp.int32, (chunk, chunk), 1)
    if reverse:
        tri = ci >= ri
        mask = ci > ri
    else:
        tri = ci <= ri
        mask = ci <= ri
    if gc is not None:
        cum = _dot3(tri.astype(F32), gc)
        last = cum[0:1, :] if reverse else cum[chunk - 1:chunk, :]
    else:
        pos = lax.broadcasted_iota(jnp.int32, (chunk, 1), 0).astype(F32)
        cum = ((chunk - pos) if reverse else (pos + 1.0)) * lg
        last = chunk * lg
    e = jnp.exp(cum)
    einv = jnp.exp(-cum)
    dec = jnp.exp(last - cum)
    return e, einv, dec, qc * e, kc * einv, kc * dec, jnp.exp(last), mask, tri


def _lin_specs(arr, width, col, tb, nb, reverse, per_head):
    if per_head:
        return pl.BlockSpec((tb, width), lambda h, r: ((nb - 1 - r) if reverse else r, col + h))
    return pl.BlockSpec((tb, width), lambda h, r: ((nb - 1 - r) if reverse else r, col))


def _lin_fwd(name, q, k, v, g, lgtab, *, heads, hb, dk, dv, chunk, tb, qcol, kcol, vcol, qscale, reverse):
    l = q.shape[0]
    nb = l // tb
    ncb = tb // chunk
    ng = heads // hb
    gated = g is not None
    per_head = ng > 1

    def body(*refs):
        if gated:
            q_ref, k_ref, v_ref, g_ref, o_ref, sp_ref, st = refs
        else:
            q_ref, k_ref, v_ref, lg_ref, o_ref, sp_ref, st = refs
        r = pl.program_id(1)

        @pl.when(r == 0)
        def _():
            st[...] = jnp.zeros_like(st)

        lg = None if gated else lg_ref[0:1, 0:1]

        def one_chunk(c, carry):
            cc = (ncb - 1 - c) if reverse else c
            rows = pl.ds(pl.multiple_of(cc * chunk, chunk), chunk)
            for h in range(hb):
                qc = q_ref[rows, h * dk:(h + 1) * dk].astype(F32) * qscale
                kc = k_ref[rows, h * dk:(h + 1) * dk].astype(F32)
                vc = v_ref[rows, h * dv:(h + 1) * dv]
                gc = g_ref[rows, h * dk:(h + 1) * dk] if gated else None
                _, _, _, qd, ki, kdec, e_last, mask, _ = _chunk_terms(qc, kc, gc, lg, chunk, reverse)
                a = jnp.where(mask, _dot(qd, ki, NT), 0.0)
                s_t = st[h]
                o_ref[rows, h * dv:(h + 1) * dv] = _dot(a, vc) + _dot(qd, s_t, NT)
                sp_ref[cc, h] = s_t
                st[h] = s_t * e_last + _dot(vc, kdec, TN)
            return carry

        lax.fori_loop(0, ncb, one_chunk, 0)

    in_specs = [_lin_specs(q, hb * dk, qcol, tb, nb, reverse, per_head), _lin_specs(k, hb * dk, kcol, tb, nb, reverse, per_head),
                _lin_specs(v, hb * dv, vcol, tb, nb, reverse, per_head)]
    args = [q, k, v]
    if gated:
        in_specs.append(_lin_specs(g, hb * dk, 0, tb, nb, reverse, per_head))
        args.append(g)
    else:
        in_specs.append(pl.BlockSpec((None, 1, LANE), lambda h, r: (h, 0, 0)))
        args.append(lgtab)
    out_specs = [_lin_specs(None, hb * dv, 0, tb, nb, reverse, per_head),
                 pl.BlockSpec((ncb, hb, dv, dk), lambda h, r: ((nb - 1 - r) if reverse else r, h, 0, 0))]
    outs = [_sds((l, heads * dv), F32), _sds((l // chunk, heads, dv, dk), F32)]
    return pl.pallas_call(
        body, name=name, grid=(ng, nb), in_specs=in_specs, out_specs=out_specs, out_shape=outs,
        scratch_shapes=[pltpu.VMEM((hb, dv, dk), F32)],
        compiler_params=_params(("arbitrary", "arbitrary")),
    )(*args)


def _lin_bwd(name, q, k, v, g, lgtab, sprev, do, prev, *, heads, hb, dk, dv, chunk, tb, qcol, kcol, vcol, qscale, reverse):
    l = q.shape[0]
    nb = l // tb
    ncb = tb // chunk
    ng = heads // hb
    gated = g is not None
    per_head = ng > 1
    brev = not reverse
    n_prev = 0 if prev is None else len(prev)

    def body(*refs):
        q_ref, k_ref, v_ref, x_ref, sp_ref, do_ref = refs[:6]
        p_refs = refs[6:6 + n_prev]
        o_refs = refs[6 + n_prev:-1]
        dst = refs[-1]
        dq_ref, dk_ref, dv_ref = o_refs[:3]
        r = pl.program_id(1)

        @pl.when(r == 0)
        def _():
            dst[...] = jnp.zeros_like(dst)

        lg = None if gated else x_ref[0:1, 0:1]

        def one_chunk(c, carry):
            cc = (ncb - 1 - c) if brev else c
            rows = pl.ds(pl.multiple_of(cc * chunk, chunk), chunk)
            for h in range(hb):
                ks = slice(h * dk, (h + 1) * dk)
                vs = slice(h * dv, (h + 1) * dv)
                qc = q_ref[rows, ks].astype(F32) * qscale
                kc = k_ref[rows, ks].astype(F32)
                vc = v_ref[rows, vs]
                gc = x_ref[rows, ks] if gated else None
                e, einv, dec, qd, ki, kdec, e_last, mask, tri = _chunk_terms(qc, kc, gc, lg, chunk, reverse)
                a = jnp.where(mask, _dot(qd, ki, NT), 0.0)
                s_t = sp_ref[cc, h]
                ds_t = dst[h]
                doc = do_ref[rows, vs]
                dvc = _dot(a, doc, TN) + _dot(kdec, ds_t, NT)
                da = jnp.where(mask, _dot(doc, vc, NT), 0.0)
                dqd = _dot(da, ki) + _dot(doc, s_t)
                dki = _dot(da, qd, TN)
                dkdec = _dot(vc, ds_t)
                dst[h] = ds_t * e_last + _dot(doc, qd, TN)
                dqc = dqd * e * qscale
                dkc = dki * einv + dkdec * dec
                if n_prev:
                    dqc = dqc + p_refs[0][rows, ks]
                    dkc = dkc + p_refs[1][rows, ks]
                    dvc = dvc + p_refs[2][rows, vs]
                dq_ref[rows, ks] = dqc
                dk_ref[rows, ks] = dkc
                dv_ref[rows, vs] = dvc
                if gated:
                    dcum = dqd * qd - dki * ki - dkdec * kdec
                    dlast = jnp.sum(dkdec * kdec, axis=0, keepdims=True) + e_last * jnp.sum(s_t * ds_t, axis=0, keepdims=True)
                    rid = lax.broadcasted_iota(jnp.int32, (chunk, 1), 0)
                    dcum = dcum + jnp.where(rid == (0 if reverse else chunk - 1), dlast, 0.0)
                    dgc = _dot3(tri.astype(F32), dcum, TN)
                    o_refs[3][rows, ks] = dgc
            return carry

        lax.fori_loop(0, ncb, one_chunk, 0)

    def spec(width, col):
        return _lin_specs(None, width, col, tb, nb, brev, per_head)

    in_specs = [spec(hb * dk, qcol), spec(hb * dk, kcol), spec(hb * dv, vcol)]
    args = [q, k, v]
    if gated:
        in_specs.append(spec(hb * dk, 0))
        args.append(g)
    else:
        in_specs.append(pl.BlockSpec((None, 1, LANE), lambda h, r: (h, 0, 0)))
        args.append(lgtab)
    in_specs.append(pl.BlockSpec((ncb, hb, dv, dk), lambda h, r: ((nb - 1 - r) if brev else r, h, 0, 0)))
    args.append(sprev)
    in_specs.append(spec(hb * dv, 0))
    args.append(do)
    out_specs = [spec(hb * dk, 0), spec(hb * dk, 0), spec(hb * dv, 0)]
    outs = [_sds((l, heads * dk), F32), _sds((l, heads * dk), F32), _sds((l, heads * dv), F32)]
    if gated:
        out_specs.append(spec(hb * dk, 0))
        outs.append(_sds((l, heads * dk), F32))
    if n_prev:
        in_specs += out_specs[:3]
        args += list(prev)
    return pl.pallas_call(
        body, name=name, grid=(ng, nb), in_specs=in_specs, out_specs=out_specs, out_shape=outs,
        scratch_shapes=[pltpu.VMEM((hb, dv, dk), F32)],
        compiler_params=_params(("arbitrary", "arbitrary")),
    )(*args)


def _headgate_fwd(name, o_f, o_b, og_arr, og_col, gn, dv):
    l, w = o_f.shape
    tm = _tile(l, 512, SUBLANE)
    nh = w // dv

    def body(of_ref, ob_ref, og_ref, gn_ref, out_ref):
        for h in range(nh):
            cs = slice(h * dv, (h + 1) * dv)
            o = of_ref[:, cs] + ob_ref[:, cs]
            r = lax.rsqrt(jnp.mean(o * o, axis=-1, keepdims=True) + EPS)
            og = og_ref[:, cs].astype(F32)
            out_ref[:, cs] = (o * r * gn_ref[:, cs] * (og * _sigmoid(og))).astype(out_ref.dtype)

    row = pl.BlockSpec((tm, w), lambda i: (i, 0))
    return pl.pallas_call(
        body, name=name, grid=(l // tm,),
        in_specs=[row, row, pl.BlockSpec((tm, w), lambda i: (i, og_col)), pl.BlockSpec((1, w), lambda i: (0, 0))],
        out_specs=row, out_shape=_sds((l, w), BF),
        compiler_params=_params(("arbitrary",)),
    )(o_f, o_b, og_arr, gn)


def _headgate_bwd(name, o_f, o_b, og_arr, og_col, gn, dout, dcol, dv):
    l, w = o_f.shape
    tm = _tile(l, 512, SUBLANE)
    nh = w // dv

    def body(of_ref, ob_ref, og_ref, gn_ref, d_ref, do_ref, dog_ref, dgn_ref):
        i = pl.program_id(0)
        for h in range(nh):
            cs = slice(h * dv, (h + 1) * dv)
            o = of_ref[:, cs] + ob_ref[:, cs]
            r = lax.rsqrt(jnp.mean(o * o, axis=-1, keepdims=True) + EPS)
            oh = o * r
            og = og_ref[:, cs].astype(F32)
            s = _sigmoid(og)
            d = d_ref[:, cs].astype(F32)
            gnv = gn_ref[:, cs]
            d_on = d * (og * s)
            dog_ref[:, cs] = (d * (oh * gnv) * s * (1.0 + og * (1.0 - s))).astype(dog_ref.dtype)
            doh = d_on * gnv
            do_ref[:, cs] = r * (doh - oh * jnp.mean(doh * oh, axis=-1, keepdims=True))
            part = jnp.sum(d_on * oh, axis=0, keepdims=True)

            @pl.when(i == 0)
            def _():
                dgn_ref[:, cs] = part

            @pl.when(i > 0)
            def _():
                dgn_ref[:, cs] += part

    row = pl.BlockSpec((tm, w), lambda i: (i, 0))
    vec = pl.BlockSpec((1, w), lambda i: (0, 0))
    return pl.pallas_call(
        body, name=name, grid=(l // tm,),
        in_specs=[row, row, pl.BlockSpec((tm, w), lambda i: (i, og_col)), vec, pl.BlockSpec((tm, w), lambda i: (i, dcol))],
        out_specs=[row, row, vec], out_shape=[_sds((l, w), F32), _sds((l, w), BF), _sds((1, w), F32)],
        compiler_params=_params(("arbitrary",)),
    )(o_f, o_b, og_arr, gn, dout)


def _rot_tables(l, dk):
    half = dk // 2
    pos = jnp.arange(l, dtype=F32)
    inv = jnp.exp(-math.log(ROPE_BASE) * jnp.arange(half, dtype=F32) / half)
    ang = pos[:, None] * inv[None, :]
    cos, sin = jnp.cos(ang), jnp.sin(ang)
    return jnp.concatenate([cos, cos], axis=-1), jnp.concatenate([-sin, sin], axis=-1)


def _rot_apply(name, src_q, qcol, src_k, kcol, cos_t, sin_t, heads, dk, kscale, out_dtype, transpose):
    l = src_q.shape[0]
    w = heads * dk
    tm = _tile(l, 512, SUBLANE)

    def rot(t, cos_v, sin_v):
        if transpose:
            return t * cos_v + pltpu.roll(t * sin_v, dk // 2, 1)
        return t * cos_v + pltpu.roll(t, dk // 2, 1) * sin_v

    def body(q_ref, k_ref, c_ref, s_ref, qo_ref, ko_ref):
        cos_v, sin_v = c_ref[...], s_ref[...]
        for h in range(heads):
            cs = slice(h * dk, (h + 1) * dk)
            qo_ref[:, cs] = rot(q_ref[:, cs].astype(F32), cos_v, sin_v).astype(out_dtype)
            ko_ref[:, cs] = (rot(k_ref[:, cs].astype(F32), cos_v, sin_v) * kscale).astype(out_dtype)

    tab = pl.BlockSpec((tm, dk), lambda i: (i, 0))
    row = pl.BlockSpec((tm, w), lambda i: (i, 0))
    return pl.pallas_call(
        body, name=name, grid=(l // tm,),
        in_specs=[pl.BlockSpec((tm, w), lambda i: (i, qcol)), pl.BlockSpec((tm, w), lambda i: (i, kcol)), tab, tab],
        out_specs=[row, row], out_shape=[_sds((l, w), out_dtype), _sds((l, w), out_dtype)],
        compiler_params=_params(("arbitrary",)),
    )(src_q, src_k, cos_t, sin_t)


def _exchange(name, src, gather):
    shape = src.shape if gather else src.shape[1:]

    def body(src_ref, out_ref, send_sems, recv_sems, local_sem):
        me = _my_index()
        own = pltpu.make_async_copy(src_ref if gather else src_ref.at[me], out_ref.at[me], local_sem)
        own.start()
        copies = _peer_copies(src_ref, out_ref, send_sems, recv_sems, gather)
        for cp in copies:
            cp.start()
        for cp in copies:
            cp.wait_recv()
        for cp in copies:
            cp.wait_send()
        own.wait()

    return pl.pallas_call(
        body, name=name,
        in_specs=[pl.BlockSpec(memory_space=pl.ANY)], out_specs=pl.BlockSpec(memory_space=pl.ANY),
        out_shape=_sds((N_DEV,) + tuple(shape), src.dtype),
        scratch_shapes=[pltpu.SemaphoreType.DMA((N_DEV - 1,)), pltpu.SemaphoreType.DMA((N_DEV - 1,)), pltpu.SemaphoreType.DMA],
        )(src)


def _my_index():
    return 4 * lax.axis_index("x") + 2 * lax.axis_index("y") + lax.axis_index("c")


def _peer_copies(src_ref, out_ref, send_sems, recv_sems, gather):
    x, y, c = lax.axis_index("x"), lax.axis_index("y"), lax.axis_index("c")
    me = 4 * x + 2 * y + c
    copies = []
    for kk in range(1, N_DEV):
        px = (1 - x) if kk & 4 else x
        py = (1 - y) if kk & 2 else y
        pc = (1 - c) if kk & 1 else c
        peer = 4 * px + 2 * py + pc
        copies.append(pltpu.make_async_remote_copy(
            src_ref=src_ref if gather else src_ref.at[peer], dst_ref=out_ref.at[me],
            send_sem=send_sems.at[kk - 1], recv_sem=recv_sems.at[kk - 1],
            device_id=(px, py, pc), device_id_type=pl.DeviceIdType.MESH))
    return copies


_HBM = pl.BlockSpec(memory_space=pltpu.HBM)
_SEM = pl.BlockSpec(memory_space=pltpu.SEMAPHORE)
_EFFECT = pltpu.SideEffectType.DATAFLOW_SIDE_EFFECTING


def _exchange_start(name, src, gather):
    shape = src.shape if gather else src.shape[1:]
    land = lax.empty((N_DEV,) + tuple(shape), src.dtype)

    def body(src_ref, land_ref, send_sems, recv_sems, src_thru, land_thru, token):
        for cp in _peer_copies(src_ref, land_ref, send_sems, recv_sems, gather):
            cp.start()
        token[...] = jnp.zeros_like(token)

    return pl.pallas_call(
        body, name=name,
        out_shape=(pltpu.SemaphoreType.DMA((N_DEV - 1,)), pltpu.SemaphoreType.DMA((N_DEV - 1,)),
                   pltpu.HBM(src.shape, src.dtype), pltpu.HBM(land.shape, land.dtype), _sds((SUBLANE, LANE), F32)),
        in_specs=(_HBM, _HBM), out_specs=(_SEM, _SEM, _HBM, _HBM, pl.BlockSpec(memory_space=pltpu.VMEM)),
        input_output_aliases={0: 2, 1: 3},
        compiler_params=pltpu.CompilerParams(has_side_effects=_EFFECT),
    )(pltpu.with_memory_space_constraint(src, pltpu.HBM), pltpu.with_memory_space_constraint(land, pltpu.HBM))


def _exchange_wait(name, started, gather, after):
    send_sems, recv_sems, src_thru, land_thru, _ = started

    def body(src_ref, land_ref, send_sems, recv_sems, after_ref, src_out, land_out):
        copies = _peer_copies(src_ref, land_ref, send_sems, recv_sems, gather)
        for cp in copies:
            cp.wait_send()
        for cp in copies:
            cp.wait_recv()

    return pl.pallas_call(
        body, name=name,
        out_shape=(pltpu.HBM(src_thru.shape, src_thru.dtype), pltpu.HBM(land_thru.shape, land_thru.dtype)),
        in_specs=(_HBM, _HBM, _SEM, _SEM, pl.BlockSpec(memory_space=pl.ANY)), out_specs=(_HBM, _HBM),
        input_output_aliases={0: 0, 1: 1},
        compiler_params=pltpu.CompilerParams(has_side_effects=_EFFECT),
    )(src_thru, land_thru, send_sems, recv_sems, after)


def _adam_math(w, gsum, m, v):
    m2 = ADAM_B1 * m + (1.0 - ADAM_B1) * gsum
    v2 = ADAM_B2 * v + (1.0 - ADAM_B2) * (gsum * gsum)
    m_hat = m2 / (1.0 - ADAM_B1 ** ADAM_STEP)
    v_hat = v2 / (1.0 - ADAM_B2 ** ADAM_STEP)
    delta = -ADAM_LR * (m_hat / (jnp.sqrt(v_hat) + ADAM_EPS) + ADAM_WD * w)
    return delta, m2, v2


def _reduce_adam(name, parts, w, m, v):
    r, c = w.shape
    tr = _tile(r, 256, 16)

    def body(p_ref, w_ref, m_ref, v_ref, g_ref, d_ref, m2_ref, v2_ref):
        gsum = p_ref[0].astype(F32)
        for s in range(1, N_DEV):
            gsum = gsum + p_ref[s].astype(F32)
        g_ref[...] = gsum
        delta, m2, v2 = _adam_math(w_ref[...], gsum, m_ref[...], v_ref[...])
        d_ref[...] = delta
        m2_ref[...] = m2
        v2_ref[...] = v2

    row = pl.BlockSpec((tr, c), lambda i: (i, 0))
    return pl.pallas_call(
        body, name=name, grid=(r // tr,),
        in_specs=[pl.BlockSpec((N_DEV, tr, c), lambda i: (0, i, 0)), row, row, row],
        out_specs=[row, row, row, row], out_shape=[_sds((r, c), F32)] * 4,
        compiler_params=_params(("arbitrary",)),
    )(parts, w, m, v)


def _reduce8(name, parts):
    _, r, c = parts.shape

    def body(p_ref, g_ref):
        gsum = p_ref[0]
        for s in range(1, N_DEV):
            gsum = gsum + p_ref[s]
        g_ref[...] = gsum

    return pl.pallas_call(
        body, name=name, grid=(1,),
        in_specs=[pl.BlockSpec((N_DEV, r, c), lambda i: (0, 0, 0))],
        out_specs=pl.BlockSpec((r, c), lambda i: (0, 0)), out_shape=_sds((r, c), F32),
        compiler_params=_params(("arbitrary",)),
    )(parts)


def _adam_packed(name, w, g, m, v):
    r, c = w.shape

    def body(w_ref, g_ref, m_ref, v_ref, d_ref, m2_ref, v2_ref):
        delta, m2, v2 = _adam_math(w_ref[...], g_ref[...], m_ref[...], v_ref[...])
        d_ref[...] = delta
        m2_ref[...] = m2
        v2_ref[...] = v2

    spec = pl.BlockSpec((r, c), lambda i: (0, 0))
    return pl.pallas_call(
        body, name=name, grid=(1,), in_specs=[spec] * 4, out_specs=[spec] * 3, out_shape=[_sds((r, c), F32)] * 3,
        compiler_params=_params(("arbitrary",)),
    )(w, g, m, v)


def _pack(arrs):
    flat = jnp.concatenate([a.reshape(-1).astype(F32) for a in arrs])
    n = flat.shape[0]
    pad = (-n) % (SUBLANE * LANE)
    return jnp.pad(flat, (0, pad)).reshape(-1, LANE)


def _unpack(packed, like):
    flat = packed.reshape(-1)
    out, off = [], 0
    for a in like:
        n = math.prod(a.shape)
        out.append(flat[off:off + n].reshape(a.shape))
        off += n
    return out


def _row_blocks(full):
    return full.reshape(N_DEV, full.shape[0] // N_DEV, full.shape[1])


def _col_blocks(full):
    r, c = full.shape
    return full.reshape(r, N_DEV, c // N_DEV).transpose(1, 0, 2)


def _cols_natural(blocks):
    n, r, c = blocks.shape
    return blocks.transpose(1, 0, 2).reshape(r, n * c)


def kernel(x, ffn1_norm, ffn1_w1, ffn1_w2, mix_norm, ffn2_norm, ffn2_w1, ffn2_w2, ab_w_in, s5_lambda_re, s5_lambda_im, s5_b_re, s5_b_im, s5_c_re, s5_c_im, s5_log_dt, s5_d, s5_w_glu, gla_w_gk, gla_b_gk, gla_norm, ab_w_out, ret_w_in, ret_norm, ret_w_out, final_norm, loss_target, m_ffn1_norm, m_ffn1_w1, m_ffn1_w2, m_mix_norm, m_ffn2_norm, m_ffn2_w1, m_ffn2_w2, m_ab_w_in, m_s5_lambda_re, m_s5_lambda_im, m_s5_b_re, m_s5_b_im, m_s5_c_re, m_s5_c_im, m_s5_log_dt, m_s5_d, m_s5_w_glu, m_gla_w_gk, m_gla_b_gk, m_gla_norm, m_ab_w_out, m_ret_w_in, m_ret_norm, m_ret_w_out, m_final_norm, v_ffn1_norm, v_ffn1_w1, v_ffn1_w2, v_mix_norm, v_ffn2_norm, v_ffn2_w1, v_ffn2_w2, v_ab_w_in, v_s5_lambda_re, v_s5_lambda_im, v_s5_b_re, v_s5_b_im, v_s5_c_re, v_s5_c_im, v_s5_log_dt, v_s5_d, v_s5_w_glu, v_gla_w_gk, v_gla_b_gk, v_gla_norm, v_ab_w_out, v_ret_w_in, v_ret_norm, v_ret_w_out, v_final_norm):
    names = ['ffn1_norm', 'ffn1_w1', 'ffn1_w2', 'mix_norm', 'ffn2_norm', 'ffn2_w1', 'ffn2_w2', 'ab_w_in', 's5_lambda_re', 's5_lambda_im', 's5_b_re', 's5_b_im', 's5_c_re', 's5_c_im', 's5_log_dt', 's5_d', 's5_w_glu', 'gla_w_gk', 'gla_b_gk', 'gla_norm', 'ab_w_out', 'ret_w_in', 'ret_norm', 'ret_w_out', 'final_norm']
    loc = locals()
    W = {n: loc[n] for n in names}
    M = {n: loc["m_" + n] for n in names}
    V = {n: loc["v_" + n] for n in names}

    me = 4 * lax.axis_index("x") + 2 * lax.axis_index("y") + lax.axis_index("c")
    xs = x[0]
    tgt = loss_target[0]
    l, d = xs.shape
    depth = ffn1_norm.shape[0]

    pending, tokens = {}, []

    def start_gather(tag, shard):
        started = _exchange_start("ags_" + tag, shard, True)
        pending[tag] = (started, shard)
        tokens.append(started[4][0, 0])

    def finish_gather(tag, after):
        started, shard = pending.pop(tag)
        _, got = _exchange_wait("agw_" + tag, started, True, after)
        return lax.dynamic_update_index_in_dim(got, shard, me, 0)

    def finish_cols(tag, after):
        g = finish_gather(tag, after)
        return g.transpose(1, 0, 2).reshape(g.shape[1], -1)

    def finish_rows(tag, after):
        g = finish_gather(tag, after)
        return g.reshape(-1, g.shape[2])

    small_sharded = [gla_w_gk, gla_b_gk, ret_norm]
    for i in range(depth):
        j = i // 2
        start_gather(f"ffn1_w1_{i}", ffn1_w1[i].astype(BF))
        start_gather(f"ffn1_w2_{i}", ffn1_w2[i].astype(BF))
        if i % 2 == 0:
            start_gather(f"ab_w_in_{j}", ab_w_in[j].astype(BF))
            if i == 0:
                start_gather("small", _pack(small_sharded))
            start_gather(f"s5_w_glu_{j}", s5_w_glu[j].astype(BF))
            start_gather(f"ab_w_out_{j}", ab_w_out[j].astype(BF))
        else:
            start_gather(f"ret_w_in_{j}", ret_w_in[j].astype(BF))
            start_gather(f"ret_w_out_{j}", ret_w_out[j].astype(BF))
        start_gather(f"ffn2_w1_{i}", ffn2_w1[i].astype(BF))
        start_gather(f"ffn2_w2_{i}", ffn2_w2[i].astype(BF))
    started_all = functools.reduce(lambda a, b: a + b, tokens)
    full = {}

    s5w = s5_d.shape[1]
    g_s5, n_s5 = s5_lambda_re.shape[2], s5_lambda_re.shape[3]
    hs = min(512, g_s5 * n_s5)
    nch = (g_s5 * n_s5) // hs
    cw = 2 * hs
    gla_hk = gla_w_gk.shape[-1] * N_DEV
    gla_dk = gla_hk // GLA_HEADS
    gla_hv = gla_norm.shape[1]
    gla_dv = gla_hv // GLA_HEADS
    ret_hv = ret_norm.shape[1] * N_DEV
    ret_dv = ret_hv // RET_HEADS
    ret_hk = (ret_w_in.shape[2] * N_DEV - 2 * ret_hv) // 2
    ret_dk = ret_hk // RET_HEADS
    assert s5w == gla_hv and 2 * gla_hk == s5w, "column blocks of the mixer projection assume these widths"
    assert ret_hv == 2 * ret_hk
    main_w = s5w + 2 * gla_hk + 2 * gla_hv
    gla_tb = _tile(l, 256, GLA_CHUNK)
    ret_chunk = min(RET_CHUNK, l)

    cos_t, sin_t = _rot_tables(l, ret_dk)
    lg_f = jnp.log1p(-jnp.exp2(-5.0 - jnp.arange(RET_HEADS, dtype=F32)))
    lgtab_f = jnp.broadcast_to(lg_f[:, None, None], (RET_HEADS, 1, LANE))
    lgtab_b = jnp.broadcast_to(lg_f[::-1][:, None, None], (RET_HEADS, 1, LANE))

    saved = []
    cur = xs
    for i in range(depth):
        j = i // 2
        s = {}
        s['x0'] = cur
        s['f1w1'], s['f1w2'] = finish_cols(f"ffn1_w1_{i}", cur), finish_rows(f"ffn1_w2_{i}", cur)
        g1 = ffn1_norm[i:i + 1] + started_all if i == 0 else ffn1_norm[i:i + 1]
        cur, s['ffn1'] = _ffn_fwd(f"l{i}_ffn1", cur, g1, s['f1w1'], s['f1w2'])
        s['x1'] = cur
        h = _rms_fwd(f"l{i}_mixnorm", cur, mix_norm[i:i + 1])
        s['h'] = h
        if i % 2 == 0:
            w_in = finish_cols(f"ab_w_in_{j}", cur)
            if i == 0:
                got = finish_gather("small", cur)
                per_dev = [_unpack(got[p], small_sharded) for p in range(N_DEV)]
                full['gla_w_gk'] = jnp.concatenate([t[0] for t in per_dev], axis=-1).astype(BF)
                full['gla_b_gk'] = jnp.concatenate([t[1] for t in per_dev], axis=-1)
                ret_norm_full = jnp.concatenate([t[2] for t in per_dev], axis=-1)
            s['w_glu'], s['w_out'] = finish_rows(f"s5_w_glu_{j}", cur), finish_rows(f"ab_w_out_{j}", cur)
            w_main, w_glo = w_in[:, :main_w], w_in[:, main_w:]
            proj = _mm_plain(f"l{i}_proj", h, w_main, NN, BF, tm=512, tn=1024, tk=d)
            glo = _mm_plain(f"l{i}_glo", h, w_glo, NN, F32, tm=1024, tn=2 * GLA_RANK, tk=d)
            s5_args = (s5_lambda_re[j], s5_lambda_im[j], s5_b_re[j], s5_b_im[j], s5_c_re[j], s5_c_im[j], s5_log_dt[j])
            (a_tab, bd, cd), s5_vjp = jax.vjp(lambda *a: _s5_tables(*a, hs), *s5_args)
            bd16, cd16 = bd.astype(BF), cd.astype(BF)
            tm = _tile(l, 512, SUBLANE)
            half = nch * cw
            tn_bu = _tile(2 * half, 1024, LANE)
            bu = _mm(f"l{i}_s5_bu",
                     [(proj, pl.BlockSpec((tm, s5w), lambda ii, jj, kk: (ii, 0)), bd16, pl.BlockSpec((s5w, tn_bu), lambda ii, jj, kk: (0, jj)))],
                     NN, (l // tm, (2 * half) // tn_bu, 1), [_sds((l, 2 * half), F32)],
                     [pl.BlockSpec((tm, tn_bu), lambda ii, jj, kk: (ii, jj))], (tm, tn_bu))[0]
            x_f = _scan(f"l{i}_s5_scan_f", bu, 0, nch, cw, a_tab[:, :half], False)
            x_b = _scan(f"l{i}_s5_scan_b", bu, nch, nch, cw, a_tab[:, half:], True)
            tk = _tile(half, 1024, LANE)
            nk = half // tk
            d_row = s5_d[j:j + 1]

            def epi_y(acc, e_refs, o_refs, ids):
                o_refs[0][...] = acc + e_refs[0][...].astype(F32) * e_refs[1][...]

            y_spec = pl.BlockSpec((tm, s5w), lambda ii, jj, kk: (ii, 0))
            y = _mm(f"l{i}_s5_y",
                    [(x_f, pl.BlockSpec((tm, tk), lambda ii, jj, kk: (ii, kk)), cd16, pl.BlockSpec((tk, s5w), lambda ii, jj, kk: (kk, 0))),
                     (x_b, pl.BlockSpec((tm, tk), lambda ii, jj, kk: (ii, kk)), cd16, pl.BlockSpec((tk, s5w), lambda ii, jj, kk: (kk + nk, 0)))],
                    NN, (l // tm, 1, nk), [_sds((l, s5w), F32)], [y_spec], (tm, s5w), epi_y,
                    [proj, d_row], [y_spec, pl.BlockSpec((1, s5w), lambda ii, jj, kk: (0, 0))])[0]
            s5_out = _glu_fwd(f"l{i}_s5_glu", y, s['w_glu'])
            zeros_r = jnp.zeros((GLA_RANK, gla_hk), BF)
            w_gk = full['gla_w_gk'][j]
            wgk_f = jnp.concatenate([w_gk[0], zeros_r], axis=0)
            wgk_b = jnp.concatenate([zeros_r, w_gk[1]], axis=0)
            b_gk = full['gla_b_gk'][j]
            g_f, g_b = _gate_fwd(f"l{i}_gla_gate", glo, wgk_f, wgk_b, b_gk[0:1], b_gk[1:2])
            qcol, kcol, vcol, ogcol = s5w // gla_hk, s5w // gla_hk + 1, (s5w + 2 * gla_hk) // gla_hv, (s5w + 2 * gla_hk) // gla_hv + 1
            lin_kw = dict(heads=GLA_HEADS, hb=GLA_HEADS, dk=gla_dk, dv=gla_dv, chunk=GLA_CHUNK, tb=gla_tb, qcol=qcol, kcol=kcol, vcol=vcol,
                          qscale=gla_dk ** -0.5)
            o_f, sp_f = _lin_fwd(f"l{i}_gla_fwd_f", proj, proj, proj, g_f, None, reverse=False, **lin_kw)
            o_b, sp_b = _lin_fwd(f"l{i}_gla_fwd_b", proj, proj, proj, g_b, None, reverse=True, **lin_kw)
            gla_out = _headgate_fwd(f"l{i}_gla_out", o_f, o_b, proj, ogcol, gla_norm[j:j + 1], gla_dv)
            w_out = s['w_out']

            def epi_res(acc, e_refs, o_refs, ids):
                o_refs[0][...] = e_refs[0][...] + acc

            row = pl.BlockSpec((tm, d), lambda ii, jj, kk: (ii, 0))
            cur = _mm(f"l{i}_mix_out",
                      [(s5_out, pl.BlockSpec((tm, s5w), lambda ii, jj, kk: (ii, 0)), w_out, pl.BlockSpec((s5w, d), lambda ii, jj, kk: (0, 0))),
                       (gla_out, pl.BlockSpec((tm, gla_hv), lambda ii, jj, kk: (ii, 0)), w_out, pl.BlockSpec((gla_hv, d), lambda ii, jj, kk: (1, 0)))],
                      NN, (l // tm, 1, 1), [_sds((l, d), F32)], [row], (tm, d), epi_res, [cur], [row])[0]
            s.update(proj=proj, glo=glo, s5_vjp=s5_vjp, a_tab=a_tab, bd16=bd16, cd16=cd16, x_f=x_f, x_b=x_b, y=y, s5_out=s5_out,
                     wgk_f=wgk_f, wgk_b=wgk_b, b_gk=b_gk, g_f=g_f, g_b=g_b, o_f=o_f, o_b=o_b, sp_f=sp_f, sp_b=sp_b, gla_out=gla_out,
                     w_main=w_main, w_glo=w_glo, lin_kw=lin_kw, ogcol=ogcol)
        else:
            w_in = finish_cols(f"ret_w_in_{j}", cur)
            s['w_in'], s['w_out'] = w_in, finish_rows(f"ret_w_out_{j}", cur)
            proj = _mm_plain(f"l{i}_proj", h, w_in, NN, BF, tm=512, tn=1024, tk=d)
            qr, kr = _rot_apply(f"l{i}_rot", proj, 0, proj, 1, cos_t, sin_t, RET_HEADS, ret_dk, ret_dk ** -0.5, BF, False)
            lin_kw = dict(heads=RET_HEADS, hb=1, dk=ret_dk, dv=ret_dv, chunk=ret_chunk, tb=ret_chunk, qcol=0, kcol=0, vcol=(2 * ret_hk) // ret_dv,
                          qscale=1.0)
            o_f, sp_f = _lin_fwd(f"l{i}_ret_fwd_f", qr, kr, proj, None, lgtab_f, reverse=False, **lin_kw)
            o_b, sp_b = _lin_fwd(f"l{i}_ret_fwd_b", qr, kr, proj, None, lgtab_b, reverse=True, **lin_kw)
            ogcol = (2 * ret_hk + ret_hv) // ret_hv
            r_out = _headgate_fwd(f"l{i}_ret_out", o_f, o_b, proj, ogcol, ret_norm_full, ret_dv)

            def epi_res(acc, e_refs, o_refs, ids):
                o_refs[0][...] = e_refs[0][...] + acc

            cur = _mm_plain(f"l{i}_mix_out", r_out, s['w_out'], NN, F32, tm=512, tn=d, tk=1024, epi=epi_res, eins=[cur])
            s.update(proj=proj, qr=qr, kr=kr, o_f=o_f, o_b=o_b, sp_f=sp_f, sp_b=sp_b, r_out=r_out, lin_kw=lin_kw, ogcol=ogcol)
        s['x2'] = cur
        s['f2w1'], s['f2w2'] = finish_cols(f"ffn2_w1_{i}", cur), finish_rows(f"ffn2_w2_{i}", cur)
        cur, s['ffn2'] = _ffn_fwd(f"l{i}_ffn2", cur, ffn2_norm[i:i + 1], s['f2w1'], s['f2w2'])
        saved.append(s)

    dx, d_final_norm, loss_row = _loss_head("loss_head", cur, final_norm.reshape(1, -1), tgt)
    loss = lax.psum(loss_row[0, 0], ("x", "y", "c"))

    G = {}
    big = {}
    G['final_norm'] = d_final_norm.reshape(-1)
    per_layer = {n: [None] * depth for n in ['ffn1_norm', 'mix_norm', 'ffn2_norm']}
    a2a, tok = {}, [jnp.zeros((), F32)]

    def start_a2a(tag, blocks):
        started = _exchange_start("a2as_" + tag, blocks, False)
        a2a[tag] = started
        tok[0] = tok[0] + started[4][0, 0]

    def dep(vec):
        return vec + tok[0]

    for i in reversed(range(depth)):
        j = i // 2
        s = saved[i]
        dx, dg, dw1, dw2 = _ffn_bwd(f"l{i}_ffn2b", dx, s['x2'], dep(ffn2_norm[i:i + 1]), s['f2w1'], s['f2w2'], s['ffn2'])
        per_layer['ffn2_norm'][i] = dg[0]
        start_a2a(f"ffn2_w1_{i}", _col_blocks(dw1))
        start_a2a(f"ffn2_w2_{i}", _row_blocks(dw2))
        tm = _tile(l, 512, SUBLANE)
        row = pl.BlockSpec((tm, d), lambda ii, jj, kk: (ii, 0))
        vec = pl.BlockSpec((1, d), lambda ii, jj, kk: (0, 0))
        if i % 2 == 0:
            proj, lin_kw = s['proj'], s['lin_kw']
            w_out = s['w_out']
            d_cat = _mm_plain(f"l{i}_dcat", dx, w_out, NT, BF, tm=512, tn=1024, tk=d)
            dwo_a = _mm_plain(f"l{i}_dwout_a", s['s5_out'], dx, TN, BF, tm=s5w, tn=d, tk=512)
            dwo_b = _mm_plain(f"l{i}_dwout_b", s['gla_out'], dx, TN, BF, tm=gla_hv, tn=d, tk=512)
            start_a2a(f"ab_w_out_{j}", _row_blocks(jnp.concatenate([dwo_a, dwo_b], axis=0)))
            do, dog, dgn = _headgate_bwd(f"l{i}_gla_outb", s['o_f'], s['o_b'], proj, s['ogcol'], dep(gla_norm[j:j + 1]), d_cat, 1, gla_dv)
            G['gla_norm'] = dgn
            r1 = _lin_bwd(f"l{i}_gla_bwd_f", proj, proj, proj, s['g_f'], None, s['sp_f'], do, None, reverse=False, **lin_kw)
            dq, dk_, dv_, dgf = r1
            r2 = _lin_bwd(f"l{i}_gla_bwd_b", proj, proj, proj, s['g_b'], None, s['sp_b'], do, (dq, dk_, dv_), reverse=True, **lin_kw)
            dq, dk_, dv_, dgb = r2
            dglo, dwf, dwb, dbf, dbb = _gate_bwd(f"l{i}_gla_gateb", s['glo'], s['wgk_f'], s['wgk_b'], s['b_gk'][0:1], s['b_gk'][1:2], dgf, dgb)
            G['gla_w_gk'] = jnp.stack([dwf[:GLA_RANK], dwb[GLA_RANK:]], axis=0)[None]
            G['gla_b_gk'] = jnp.concatenate([dbf, dbb], axis=0)[None]
            dy, dwglu = _glu_bwd(f"l{i}_s5_glub", s['y'], s['w_glu'], d_cat, 0)
            start_a2a(f"s5_w_glu_{j}", _row_blocks(dwglu.astype(BF)))
            half = nch * cw
            cd16, bd16 = s['cd16'], s['bd16']
            dxs = _mm_plain(f"l{i}_s5_dx", dy, cd16, NT, F32, tm=512, tn=1024, tk=s5w)
            dcd_f = _mm_plain(f"l{i}_s5_dcd_f", s['x_f'], dy, TN, F32, tm=1024, tn=s5w, tk=512)
            dcd_b = _mm_plain(f"l{i}_s5_dcd_b", s['x_b'], dy, TN, F32, tm=1024, tn=s5w, tk=512)
            a_conj = _conj_tab(s['a_tab'], hs)
            lam_f, da_f = _scan(f"l{i}_s5_adj_f", dxs, 0, nch, cw, a_conj[:, :half], True, xs=s['x_f'])
            lam_b, da_b = _scan(f"l{i}_s5_adj_b", dxs, nch, nch, cw, a_conj[:, half:], False, xs=s['x_b'])
            tk = _tile(l, 512, SUBLANE)
            u_spec = pl.BlockSpec((tk, s5w), lambda ii, jj, kk: (kk, 0))
            tn = _tile(half, 1024, LANE)
            dbd_f = _mm(f"l{i}_s5_dbd_f", [(proj, u_spec, lam_f, pl.BlockSpec((tk, tn), lambda ii, jj, kk: (kk, jj)))],
                        TN, (1, half // tn, l // tk), [_sds((s5w, half), F32)], [pl.BlockSpec((s5w, tn), lambda ii, jj, kk: (0, jj))], (s5w, tn))[0]
            dbd_b = _mm(f"l{i}_s5_dbd_b", [(proj, u_spec, lam_b, pl.BlockSpec((tk, tn), lambda ii, jj, kk: (kk, jj)))],
                        TN, (1, half // tn, l // tk), [_sds((s5w, half), F32)], [pl.BlockSpec((s5w, tn), lambda ii, jj, kk: (0, jj))], (s5w, tn))[0]
            d_row = s5_d[j:j + 1]

            def epi_du(acc, e_refs, o_refs, ids):
                dyv = e_refs[0][...]
                o_refs[0][...] = (acc + dyv * e_refs[1][...]).astype(BF)
                part = jnp.sum(dyv * e_refs[2][...].astype(F32), axis=0, keepdims=True)

                @pl.when(ids[0] == 0)
                def _():
                    o_refs[1][...] = part

                @pl.when(ids[0] > 0)
                def _():
                    o_refs[1][...] += part

            tkk = _tile(half, 1024, LANE)
            nkk = half // tkk
            u_row = pl.BlockSpec((tm, s5w), lambda ii, jj, kk: (ii, 0))
            u_vec = pl.BlockSpec((1, s5w), lambda ii, jj, kk: (0, 0))
            du, dd = _mm(f"l{i}_s5_du",
                         [(lam_f, pl.BlockSpec((tm, tkk), lambda ii, jj, kk: (ii, kk)), bd16, pl.BlockSpec((s5w, tkk), lambda ii, jj, kk: (0, kk))),
                          (lam_b, pl.BlockSpec((tm, tkk), lambda ii, jj, kk: (ii, kk)), bd16, pl.BlockSpec((s5w, tkk), lambda ii, jj, kk: (0, kk + nkk)))],
                         NT, (l // tm, 1, nkk), [_sds((l, s5w), BF), _sds((1, s5w), F32)], [u_row, u_vec], (tm, s5w), epi_du,
                         [dy, d_row, proj], [u_row, u_vec, u_row])
            G['s5_d'] = dd
            d_atab = jnp.concatenate([da_f, da_b], axis=1)
            d_bd = jnp.concatenate([dbd_f, dbd_b], axis=1)
            d_cd = jnp.concatenate([dcd_f, dcd_b], axis=0)
            g_lre, g_lim, g_bre, g_bim, g_cre, g_cim, g_ldt = s['s5_vjp']((d_atab, d_bd, d_cd))
            G['s5_lambda_re'], G['s5_lambda_im'], G['s5_b_re'], G['s5_b_im'] = g_lre[None], g_lim[None], g_bre[None], g_bim[None]
            G['s5_c_re'], G['s5_c_im'], G['s5_log_dt'] = g_cre[None], g_cim[None], g_ldt[None]
            dproj = jnp.concatenate([du, dq.astype(BF), dk_.astype(BF), dv_.astype(BF), dog], axis=1)
            r2w = 2 * GLA_RANK
            pairs = [(dproj, pl.BlockSpec((tm, main_w), lambda ii, jj, kk: (ii, 0)), s['w_main'], pl.BlockSpec((d, main_w), lambda ii, jj, kk: (0, 0))),
                     (dglo, pl.BlockSpec((tm, r2w), lambda ii, jj, kk: (ii, 0)), s['w_glo'], pl.BlockSpec((d, r2w), lambda ii, jj, kk: (0, 0)))]
            dx, dg = _mm(f"l{i}_dh", pairs, NT, (l // tm, 1, 1), [_sds((l, d), F32), _sds((1, d), F32)], [row, vec], (tm, d),
                         _rms_bwd_epi(0), [s['x1'], dep(mix_norm[i:i + 1]), dx], [row, vec, row])
            dw_main = _mm_plain(f"l{i}_dwin_main", s['h'], dproj, TN, BF, tm=d, tn=1024, tk=512)
            dw_glo = _mm_plain(f"l{i}_dwin_glo", s['h'], dglo, TN, BF, tm=d, tn=r2w, tk=512)
            start_a2a(f"ab_w_in_{j}", _col_blocks(jnp.concatenate([dw_main, dw_glo], axis=1)))
        else:
            proj, lin_kw = s['proj'], s['lin_kw']
            w_out = s['w_out']
            d_ro = _mm_plain(f"l{i}_dro", dx, w_out, NT, BF, tm=512, tn=1024, tk=d)
            dwo = _mm_plain(f"l{i}_dwout", s['r_out'], dx, TN, BF, tm=1024, tn=d, tk=512)
            start_a2a(f"ret_w_out_{j}", _row_blocks(dwo))
            do, dog, dgn = _headgate_bwd(f"l{i}_ret_outb", s['o_f'], s['o_b'], proj, s['ogcol'], dep(ret_norm_full), d_ro, 0, ret_dv)
            G['ret_norm'] = dgn
            r1 = _lin_bwd(f"l{i}_ret_bwd_f", s['qr'], s['kr'], proj, None, lgtab_f, s['sp_f'], do, None, reverse=False, **lin_kw)
            r2 = _lin_bwd(f"l{i}_ret_bwd_b", s['qr'], s['kr'], proj, None, lgtab_b, s['sp_b'], do, r1, reverse=True, **lin_kw)
            dqr, dkr, dv_ = r2
            dq, dk_ = _rot_apply(f"l{i}_rotb", dqr, 0, dkr, 0, cos_t, sin_t, RET_HEADS, ret_dk, ret_dk ** -0.5, BF, True)
            dproj = jnp.concatenate([dq, dk_, dv_.astype(BF), dog], axis=1)
            dx, dg = _mm_plain(f"l{i}_dh", dproj, s['w_in'], NT, F32, tm=512, tn=d, tk=1024, epi=_rms_bwd_epi(0),
                               eins=[s['x1'], dep(mix_norm[i:i + 1]), dx], especs=[None, vec, None],
                               extra_outs=[_sds((1, d), F32)], extra_specs=[vec])
            dw_in = _mm_plain(f"l{i}_dwin", s['h'], dproj, TN, BF, tm=d, tn=1024, tk=512)
            start_a2a(f"ret_w_in_{j}", _col_blocks(dw_in))
        per_layer['mix_norm'][i] = dg[0]
        dx, dg, dw1, dw2 = _ffn_bwd(f"l{i}_ffn1b", dx, s['x0'], dep(ffn1_norm[i:i + 1]), s['f1w1'], s['f1w2'], s['ffn1'])
        per_layer['ffn1_norm'][i] = dg[0]
        start_a2a(f"ffn1_w1_{i}", _col_blocks(dw1))
        start_a2a(f"ffn1_w2_{i}", _row_blocks(dw2))
    for n in per_layer:
        G[n] = jnp.stack(per_layer[n], axis=0)
    grad_x = dx[None]

    out_g, out_d, out_m, out_v = {}, {}, {}, {}

    def big_update(tag, w, m, v):
        blocks, got = _exchange_wait("a2aw_" + tag, a2a.pop(tag), False, dx)
        parts = lax.dynamic_update_index_in_dim(got, lax.dynamic_index_in_dim(blocks, me, 0, keepdims=False), me, 0)
        return _reduce_adam("upd_" + tag, parts, w, m, v)

    for n in ['ffn2_w1', 'ffn2_w2', 'ffn1_w1', 'ffn1_w2']:
        res = [big_update(f"{n}_{i}", W[n][i], M[n][i], V[n][i]) for i in range(depth)]
        out_g[n], out_d[n], out_m[n], out_v[n] = [jnp.stack([r[t] for r in res], axis=0) for t in range(4)]
    for n in ['ab_w_in', 's5_w_glu', 'ab_w_out', 'ret_w_in', 'ret_w_out']:
        res = big_update(f"{n}_0", W[n][0], M[n][0], V[n][0])
        out_g[n], out_d[n], out_m[n], out_v[n] = [r[None] for r in res]
    assert not a2a and not pending

    small = ['ffn1_norm', 'mix_norm', 'ffn2_norm', 's5_lambda_re', 's5_lambda_im', 's5_b_re', 's5_b_im', 's5_c_re', 's5_c_im',
             's5_log_dt', 's5_d', 'gla_w_gk', 'gla_b_gk', 'gla_norm', 'ret_norm', 'final_norm']
    packed = _pack([G[n] for n in small])
    gathered = _exchange("ag_small_grads", packed, True)
    summed = _reduce8("sum_small_grads", gathered)
    g_full = dict(zip(small, _unpack(summed, [G[n] for n in small])))
    g_small = {}
    for n in small:
        gf = g_full[n]
        if n in ('gla_w_gk', 'gla_b_gk', 'ret_norm'):
            width = W[n].shape[-1]
            gf = lax.dynamic_slice_in_dim(gf, me * width, width, axis=gf.ndim - 1)
        g_small[n] = gf.reshape(W[n].shape)
    pw, pg, pm, pv = (_pack([src[n] for n in small]) for src in (W, g_small, M, V))
    pd, pm2, pv2 = _adam_packed("upd_small", pw, pg, pm, pv)
    like = [W[n] for n in small]
    for n, dd_, mm_, vv_ in zip(small, _unpack(pd, like), _unpack(pm2, like), _unpack(pv2, like)):
        out_g[n], out_d[n], out_m[n], out_v[n] = g_small[n], dd_, mm_, vv_

    return (loss, grad_x, *[out_g[n] for n in names], *[out_d[n] for n in names], *[out_m[n] for n in names], *[out_v[n] for n in names])
```

```python
import functools
import math

import jax
import jax.numpy as jnp
from jax import lax
from jax.experimental import pallas as pl
from jax.experimental.pallas import tpu as pltpu

F32 = jnp.float32
BF = jnp.bfloat16
N_DEV = 8
EPS = 1e-6
S5_GROUP = 16
GLA_HEADS = 4
GLA_RANK = 16
GLA_GATE_NORM = 16.0
RET_HEADS = 8
ROPE_BASE = 10000.0
GLA_CHUNK = 64
RET_CHUNK = 256
ADAM_LR, ADAM_B1, ADAM_B2, ADAM_EPS, ADAM_WD, ADAM_STEP = 0.001, 0.9, 0.999, 1e-08, 0.01, 10
VMEM_LIMIT_BYTES = 56 * 1024 * 1024
LANE = 128
SUBLANE = 8

NN = (((1,), (0,)), ((), ()))
NT = (((1,), (1,)), ((), ()))
TN = (((0,), (0,)), ((), ()))


def _tile(n, pref, align):
    if n <= pref:
        return n
    t = (pref // align) * align
    while t >= align:
        if n % t == 0:
            return t
        t -= align
    return n


def _params(sem):
    return pltpu.CompilerParams(dimension_semantics=sem, vmem_limit_bytes=VMEM_LIMIT_BYTES)


def _dot(a, b, dims=NN):
    return lax.dot_general(a.astype(BF), b.astype(BF), dims, preferred_element_type=F32)


def _dot3(m01, g, dims=NN):
    g1 = g.astype(BF)
    r1 = g - g1.astype(F32)
    g2 = r1.astype(BF)
    g3 = (r1 - g2.astype(F32)).astype(BF)
    m = m01.astype(BF)
    return (lax.dot_general(m, g1, dims, preferred_element_type=F32)
            + lax.dot_general(m, g2, dims, preferred_element_type=F32)
            + lax.dot_general(m, g3, dims, preferred_element_type=F32))


def _sigmoid(x):
    return 1.0 / (1.0 + jnp.exp(-x))


def _mm(name, pairs, dims, grid, outs, out_specs, acc_shape, epi=None, eins=(), especs=()):
    n_p, n_e, n_o = len(pairs), len(eins), len(outs)
    nk = grid[2]

    def body(*refs):
        a_refs = refs[0:2 * n_p:2]
        b_refs = refs[1:2 * n_p:2]
        e_refs = refs[2 * n_p:2 * n_p + n_e]
        o_refs = refs[2 * n_p + n_e:2 * n_p + n_e + n_o]
        acc = refs[-1]
        ids = (pl.program_id(0), pl.program_id(1), pl.program_id(2))

        part = _dot(a_refs[0][...], b_refs[0][...], dims)
        for p in range(1, n_p):
            part = part + _dot(a_refs[p][...], b_refs[p][...], dims)

        def finish(total):
            if epi is None:
                o_refs[0][...] = total.astype(o_refs[0].dtype)
            else:
                epi(total, e_refs, o_refs, ids)

        if nk == 1:
            finish(part)
        else:
            @pl.when(ids[2] == 0)
            def _():
                acc[...] = part

            @pl.when(ids[2] > 0)
            def _():
                acc[...] += part

            @pl.when(ids[2] == nk - 1)
            def _():
                finish(acc[...])

    in_specs, args = [], []
    for a, a_spec, b, b_spec in pairs:
        in_specs += [a_spec, b_spec]
        args += [a, b]
    in_specs += list(especs)
    args += list(eins)
    res = pl.pallas_call(
        body, name=name, grid=grid, in_specs=in_specs, out_specs=list(out_specs), out_shape=list(outs),
        scratch_shapes=[pltpu.VMEM(acc_shape, F32)],
        compiler_params=_params(("arbitrary", "arbitrary", "arbitrary")),
    )(*args)
    return res


def _sds(shape, dtype):
    return jax.ShapeDtypeStruct(shape, dtype)


def _mm_plain(name, a, b, dims, out_dtype, tm=512, tn=1024, tk=1024, epi=None, eins=(), especs=None, extra_outs=(), extra_specs=()):
    if dims == NN:
        (m, k), n = a.shape, b.shape[1]
    elif dims == NT:
        (m, k), n = a.shape, b.shape[0]
    else:
        (k, m), n = a.shape, b.shape[1]
    tm, tn = _tile(m, tm, LANE if dims == TN else SUBLANE), _tile(n, tn, LANE)
    tk = _tile(k, tk, SUBLANE if dims == TN else LANE)
    grid = (m // tm, n // tn, k // tk)
    if dims == NN:
        a_spec = pl.BlockSpec((tm, tk), lambda i, j, kk: (i, kk))
        b_spec = pl.BlockSpec((tk, tn), lambda i, j, kk: (kk, j))
    elif dims == NT:
        a_spec = pl.BlockSpec((tm, tk), lambda i, j, kk: (i, kk))
        b_spec = pl.BlockSpec((tn, tk), lambda i, j, kk: (j, kk))
    else:
        a_spec = pl.BlockSpec((tk, tm), lambda i, j, kk: (kk, i))
        b_spec = pl.BlockSpec((tk, tn), lambda i, j, kk: (kk, j))
    o_spec = pl.BlockSpec((tm, tn), lambda i, j, kk: (i, j))
    if especs is None:
        especs = [o_spec] * len(eins)
    else:
        especs = [o_spec if s is None else s for s in especs]
    res = _mm(name, [(a, a_spec, b, b_spec)], dims, grid, [_sds((m, n), out_dtype)] + list(extra_outs),
              [o_spec] + list(extra_specs), (tm, tn), epi, eins, especs)
    return res if extra_outs else res[0]


def _rms_fwd(name, x, g):
    l, d = x.shape
    tm = _tile(l, 1024, SUBLANE)

    def body(x_ref, g_ref, o_ref):
        xv = x_ref[...]
        r = lax.rsqrt(jnp.mean(xv * xv, axis=-1, keepdims=True) + EPS)
        o_ref[...] = (xv * r * g_ref[...]).astype(o_ref.dtype)

    return pl.pallas_call(
        body, name=name, grid=(l // tm,),
        in_specs=[pl.BlockSpec((tm, d), lambda i: (i, 0)), pl.BlockSpec((1, d), lambda i: (0, 0))],
        out_specs=pl.BlockSpec((tm, d), lambda i: (i, 0)), out_shape=_sds((l, d), BF),
        compiler_params=_params(("arbitrary",)),
    )(x, g)


def _rms_bwd_epi(first_axis):
    def epi(acc, e_refs, o_refs, ids):
        x_ref, g_ref, dr_ref = e_refs
        dx_ref, dg_ref = o_refs
        xv = x_ref[...]
        r = lax.rsqrt(jnp.mean(xv * xv, axis=-1, keepdims=True) + EPS)
        xh = xv * r
        dxh = acc * g_ref[...]
        dx_ref[...] = dr_ref[...] + r * (dxh - xh * jnp.mean(dxh * xh, axis=-1, keepdims=True))
        part = jnp.sum(acc * xh, axis=0, keepdims=True)

        @pl.when(ids[first_axis] == 0)
        def _():
            dg_ref[...] = part

        @pl.when(ids[first_axis] > 0)
        def _():
            dg_ref[...] += part

    return epi


def _loss_head(name, x, g, target):
    l, d = x.shape
    tm = _tile(l, 512, SUBLANE)
    n = l // tm

    def body(x_ref, g_ref, t_ref, dx_ref, dg_ref, loss_ref, lacc):
        i = pl.program_id(0)
        xv = x_ref[...]
        r = lax.rsqrt(jnp.mean(xv * xv, axis=-1, keepdims=True) + EPS)
        xh = xv * r
        e = xh * g_ref[...] - t_ref[...]
        dy = e * (1.0 / d)
        dxh = dy * g_ref[...]
        dx_ref[...] = r * (dxh - xh * jnp.mean(dxh * xh, axis=-1, keepdims=True))
        dg_part = jnp.sum(dy * xh, axis=0, keepdims=True)
        l_part = jnp.sum(e * e, axis=0, keepdims=True)

        @pl.when(i == 0)
        def _():
            dg_ref[...] = dg_part
            lacc[...] = l_part

        @pl.when(i > 0)
        def _():
            dg_ref[...] += dg_part
            lacc[...] += l_part

        @pl.when(i == n - 1)
        def _():
            loss_ref[...] = jnp.zeros_like(loss_ref) + jnp.sum(lacc[...]) * (0.5 / d)

    return pl.pallas_call(
        body, name=name, grid=(n,),
        in_specs=[pl.BlockSpec((tm, d), lambda i: (i, 0)), pl.BlockSpec((1, d), lambda i: (0, 0)),
                  pl.BlockSpec((tm, d), lambda i: (i, 0))],
        out_specs=[pl.BlockSpec((tm, d), lambda i: (i, 0)), pl.BlockSpec((1, d), lambda i: (0, 0)),
                   pl.BlockSpec((1, LANE), lambda i: (0, 0))],
        out_shape=[_sds((l, d), F32), _sds((1, d), F32), _sds((1, LANE), F32)],
        scratch_shapes=[pltpu.VMEM((1, d), F32)],
        compiler_params=_params(("arbitrary",)),
    )(x, g, target)


def _ffn_up(name, hn, w1):
    l, d = hn.shape
    f = w1.shape[1] // 2
    tm, tn = _tile(l, 512, SUBLANE), _tile(f, 1408, LANE)
    nj = f // tn

    def body(h_ref, wg_ref, wu_ref, gu_ref, a_ref):
        h = h_ref[...]
        g = jnp.dot(h, wg_ref[...], preferred_element_type=F32)
        u = jnp.dot(h, wu_ref[...], preferred_element_type=F32)
        gu_ref[0] = g.astype(BF)
        gu_ref[1] = u.astype(BF)
        a_ref[...] = (g * _sigmoid(g) * u).astype(BF)

    return pl.pallas_call(
        body, name=name, grid=(nj, l // tm),
        in_specs=[pl.BlockSpec((tm, d), lambda j, i: (i, 0)), pl.BlockSpec((d, tn), lambda j, i: (0, j)),
                  pl.BlockSpec((d, tn), lambda j, i: (0, j + nj))],
        out_specs=[pl.BlockSpec((2, tm, tn), lambda j, i: (0, i, j)), pl.BlockSpec((tm, tn), lambda j, i: (i, j))],
        out_shape=[_sds((2, l, f), BF), _sds((l, f), BF)],
        compiler_params=_params(("arbitrary", "arbitrary")),
    )(hn, w1, w1)


def _ffn_fwd(tag, x, gnorm, w1, w2):
    hn = _rms_fwd(tag + "_norm", x, gnorm)
    gu, a = _ffn_up(tag + "_up", hn, w1)

    def epi(acc, e_refs, o_refs, ids):
        o_refs[0][...] = e_refs[0][...] + 0.5 * acc

    l, d = x.shape
    x_new = _mm_plain(tag + "_down", a, w2, NN, F32, tm=512, tn=d, tk=1408, epi=epi, eins=[x])
    return x_new, (hn, gu, a)


def _ffn_bwd(tag, dres, x, gnorm, w1, w2, saved, on_grads):
    hn, gu, a = saved
    l, d = x.shape
    f = w2.shape[0]
    tm, tn = _tile(l, 512, SUBLANE), _tile(f, 1408, LANE)
    nj = f // tn

    def epi_gu(acc, e_refs, o_refs, ids):
        g = e_refs[0][0].astype(F32)
        u = e_refs[0][1].astype(F32)
        s = _sigmoid(g)
        da = 0.5 * acc
        o_refs[0][0] = (da * u * s * (1.0 + g * (1.0 - s))).astype(BF)
        o_refs[0][1] = (da * g * s).astype(BF)

    gu_spec = pl.BlockSpec((2, tm, tn), lambda i, j, kk: (0, i, j))
    dgu = _mm(tag + "_dgu",
              [(dres, pl.BlockSpec((tm, d), lambda i, j, kk: (i, 0)), w2, pl.BlockSpec((tn, d), lambda i, j, kk: (j, 0)))],
              NT, (l // tm, nj, 1), [_sds((2, l, f), BF)], [gu_spec], (tm, tn), epi_gu, [gu], [gu_spec])[0]

    def epi_half(acc, e_refs, o_refs, ids):
        o_refs[0][...] = (0.5 * acc).astype(BF)

    dw2 = _mm_plain(tag + "_dw2", a, dres, TN, BF, tm=1408, tn=d, tk=512, epi=epi_half)

    tk = _tile(l, 512, SUBLANE)
    dw1 = _mm(tag + "_dw1",
              [(hn, pl.BlockSpec((tk, d), lambda i, j, kk: (kk, 0)),
                dgu, pl.BlockSpec((None, tk, tn), lambda i, j, kk: (j // nj, kk, j % nj)))],
              TN, (1, 2 * nj, l // tk), [_sds((d, 2 * f), BF)], [pl.BlockSpec((d, tn), lambda i, j, kk: (0, j))], (d, tn))[0]

    gnorm = on_grads(dw1, dw2, gnorm)
    row = pl.BlockSpec((tm, d), lambda i, j, kk: (i, 0))
    vec = pl.BlockSpec((1, d), lambda i, j, kk: (0, 0))
    dx, dg = _mm(tag + "_dhn",
                 [(dgu, pl.BlockSpec((None, tm, tn), lambda i, j, kk: (kk // nj, i, kk % nj)),
                   w1, pl.BlockSpec((d, tn), lambda i, j, kk: (0, kk)))],
                 NT, (l // tm, 1, 2 * nj), [_sds((l, d), F32), _sds((1, d), F32)], [row, vec], (tm, d),
                 _rms_bwd_epi(0), [x, gnorm, dres], [row, vec, row])
    return dx, dg


def _s5_chunk_tables(lam_re, lam_im, b_re, b_im, c_re, c_im, log_dt, hs):
    f32 = F32
    g, n = lam_re.shape[1], lam_re.shape[2]
    p = b_re.shape[-1]
    nch, gpc, nt = (g * n) // hs, hs // n, hs // LANE
    lr = jnp.minimum(lam_re.astype(f32), -1e-4)
    li = lam_im.astype(f32)
    dt = jnp.exp(log_dt.astype(f32))[..., None]
    mag = jnp.exp(lr * dt)
    ar = mag * jnp.cos(li * dt)
    ai = mag * jnp.sin(li * dt)
    den = lr * lr + li * li
    cr = ((ar - 1.0) * lr + ai * li) / den
    ci = (ai * lr - (ar - 1.0) * li) / den
    bbr = cr[..., None] * b_re - ci[..., None] * b_im
    bbi = cr[..., None] * b_im + ci[..., None] * b_re
    eye = jnp.eye(gpc, dtype=f32)[None, None, None, :, None, None, :]
    a_f = jnp.stack([ar, ai], axis=1).reshape(2, 2, nch, nt, LANE).transpose(0, 2, 1, 3, 4).reshape(2, nch, 2 * nt, LANE)
    bb = jnp.stack([bbr, bbi], axis=1).reshape(2, 2, nch, gpc, n, p)
    bd = (bb[..., None] * eye).transpose(0, 2, 6, 5, 1, 3, 4).reshape(2, nch, gpc * p, 2 * hs)
    cc = jnp.stack([c_re, -c_im], axis=1).reshape(2, 2, nch, gpc, p, n)
    cd = (cc[..., None] * eye).transpose(0, 2, 1, 3, 5, 6, 4).reshape(2, nch, 2 * hs, gpc * p)
    return a_f, bd, cd


def _fold_store(ref, val, tb, ntiles):
    for s in range(ntiles):
        ref[:, s * SUBLANE:(s + 1) * SUBLANE, :] = val[:, s * LANE:(s + 1) * LANE].reshape(tb // SUBLANE, SUBLANE, LANE)


def _unfold(ref, tb, ntiles):
    return jnp.concatenate([ref[:, s * SUBLANE:(s + 1) * SUBLANE, :].reshape(tb, LANE) for s in range(ntiles)], axis=1)


def _s5_fwd(name, proj, bd, cd, a_f, reverse):
    l = proj.shape[0]
    nch, cu, hs2 = bd.shape
    nt = hs2 // (2 * LANE)
    frows = 2 * nt * SUBLANE
    tb = _tile(l, 512, SUBLANE)
    nb = l // tb

    def body(u_ref, bd_ref, cd_ref, a_ref, xf_ref, y_ref, st):
        r = pl.program_id(1)

        @pl.when(r == 0)
        def _():
            st[...] = jnp.zeros_like(st)

        _fold_store(xf_ref, _dot(u_ref[...], bd_ref[...]), tb, 2 * nt)
        ar, ai = a_ref[0:nt, :], a_ref[nt:2 * nt, :]

        def group(gi, carry):
            rr = (tb // SUBLANE - 1 - gi) if reverse else gi
            sr, si = carry
            for qq in range(SUBLANE):
                q = (SUBLANE - 1 - qq) if reverse else qq
                re_rows, im_rows = pl.ds(q, nt, stride=SUBLANE), pl.ds(nt * SUBLANE + q, nt, stride=SUBLANE)
                nr = ar * sr - ai * si + xf_ref[rr, re_rows, :]
                ni = ar * si + ai * sr + xf_ref[rr, im_rows, :]
                xf_ref[rr, re_rows, :] = nr
                xf_ref[rr, im_rows, :] = ni
                sr, si = nr, ni
            return sr, si

        fin = lax.fori_loop(0, tb // SUBLANE, group, (st[0:nt, :], st[nt:2 * nt, :]))
        st[0:nt, :] = fin[0]
        st[nt:2 * nt, :] = fin[1]
        y_ref[...] = _dot(_unfold(xf_ref, tb, 2 * nt), cd_ref[...])

    def rows(r):
        return (nb - 1 - r) if reverse else r

    return pl.pallas_call(
        body, name=name, grid=(nch, nb),
        in_specs=[pl.BlockSpec((tb, cu), lambda c, r: (rows(r), c)), pl.BlockSpec((None, cu, hs2), lambda c, r: (c, 0, 0)),
                  pl.BlockSpec((None, hs2, cu), lambda c, r: (c, 0, 0)), pl.BlockSpec((None, 2 * nt, LANE), lambda c, r: (c, 0, 0))],
        out_specs=[pl.BlockSpec((tb // SUBLANE, frows, LANE), lambda c, r: (rows(r), c, 0)), pl.BlockSpec((tb, cu), lambda c, r: (rows(r), c))],
        out_shape=[_sds((l // SUBLANE, nch * frows, LANE), F32), _sds((l, nch * cu), F32)],
        scratch_shapes=[pltpu.VMEM((2 * nt, LANE), F32)],
        compiler_params=_params(("arbitrary", "arbitrary")),
    )(proj, bd, cd, a_f)


def _s5_bwd(name, proj, dy, xf, bd, cd, a_conj, reverse):
    l = proj.shape[0]
    nch, cu, hs2 = bd.shape
    nt = hs2 // (2 * LANE)
    frows = 2 * nt * SUBLANE
    tb = _tile(l, 512, SUBLANE)
    nb = l // tb

    def body(u_ref, dy_ref, xs_ref, bd_ref, cd_ref, a_ref, du_ref, dbd_ref, dcd_ref, da_ref, lam, st):
        r = pl.program_id(1)

        @pl.when(r == 0)
        def _():
            st[...] = jnp.zeros_like(st)
            dbd_ref[...] = jnp.zeros_like(dbd_ref)
            dcd_ref[...] = jnp.zeros_like(dcd_ref)
            da_ref[...] = jnp.zeros_like(da_ref)

        dyv = dy_ref[...]
        _fold_store(lam, _dot(dyv, cd_ref[...], NT), tb, 2 * nt)
        ar, ai = a_ref[0:nt, :], a_ref[nt:2 * nt, :]

        def group(gi, carry):
            rr = (tb // SUBLANE - 1 - gi) if reverse else gi
            sr, si, cr, ci = carry
            for qq in range(SUBLANE):
                q = (SUBLANE - 1 - qq) if reverse else qq
                re_rows, im_rows = pl.ds(q, nt, stride=SUBLANE), pl.ds(nt * SUBLANE + q, nt, stride=SUBLANE)
                xr, xi = xs_ref[rr, re_rows, :], xs_ref[rr, im_rows, :]
                cr = cr + sr * xr + si * xi
                ci = ci + si * xr - sr * xi
                nr = ar * sr - ai * si + lam[rr, re_rows, :]
                ni = ar * si + ai * sr + lam[rr, im_rows, :]
                lam[rr, re_rows, :] = nr
                lam[rr, im_rows, :] = ni
                sr, si = nr, ni
            return sr, si, cr, ci

        zero = jnp.zeros((nt, LANE), F32)
        fin = lax.fori_loop(0, tb // SUBLANE, group, (st[0:nt, :], st[nt:2 * nt, :], zero, zero))
        st[0:nt, :] = fin[0]
        st[nt:2 * nt, :] = fin[1]
        da_ref[0:nt, :] += fin[2]
        da_ref[nt:2 * nt, :] += fin[3]
        lam_u = _unfold(lam, tb, 2 * nt)
        du_ref[...] = _dot(lam_u, bd_ref[...], NT)
        dbd_ref[...] += _dot(u_ref[...], lam_u, TN)
        dcd_ref[...] += _dot(_unfold(xs_ref, tb, 2 * nt), dyv, TN)

    def rows(r):
        return (nb - 1 - r) if reverse else r

    chunk_rows = pl.BlockSpec((tb, cu), lambda c, r: (rows(r), c))
    bd_spec = pl.BlockSpec((None, cu, hs2), lambda c, r: (c, 0, 0))
    cd_spec = pl.BlockSpec((None, hs2, cu), lambda c, r: (c, 0, 0))
    a_spec = pl.BlockSpec((None, 2 * nt, LANE), lambda c, r: (c, 0, 0))
    return pl.pallas_call(
        body, name=name, grid=(nch, nb),
        in_specs=[chunk_rows, chunk_rows, pl.BlockSpec((tb // SUBLANE, frows, LANE), lambda c, r: (rows(r), c, 0)), bd_spec, cd_spec, a_spec],
        out_specs=[chunk_rows, bd_spec, cd_spec, a_spec],
        out_shape=[_sds((l, nch * cu), F32), _sds((nch, cu, hs2), F32), _sds((nch, hs2, cu), F32), _sds((nch, 2 * nt, LANE), F32)],
        scratch_shapes=[pltpu.VMEM((tb // SUBLANE, frows, LANE), F32), pltpu.VMEM((2 * nt, LANE), F32)],
        compiler_params=_params(("arbitrary", "arbitrary")),
    )(proj, dy, xf, bd, cd, a_conj)


def _s5_du(name, du_f, du_b, dy, proj, d_row):
    l, w = dy.shape
    tm = _tile(l, 1024, SUBLANE)

    def body(f_ref, b_ref, dy_ref, u_ref, d_ref, du_ref, dd_ref):
        i = pl.program_id(0)
        dyv = dy_ref[...]
        du_ref[...] = (f_ref[...] + b_ref[...] + dyv * d_ref[...]).astype(du_ref.dtype)
        part = jnp.sum(dyv * u_ref[...].astype(F32), axis=0, keepdims=True)

        @pl.when(i == 0)
        def _():
            dd_ref[...] = part

        @pl.when(i > 0)
        def _():
            dd_ref[...] += part

    row = pl.BlockSpec((tm, w), lambda i: (i, 0))
    vec = pl.BlockSpec((1, w), lambda i: (0, 0))
    return pl.pallas_call(
        body, name=name, grid=(l // tm,), in_specs=[row, row, row, row, vec], out_specs=[row, vec],
        out_shape=[_sds((l, w), BF), _sds((1, w), F32)],
        compiler_params=_params(("arbitrary",)),
    )(du_f, du_b, dy, proj, d_row)


def _gelu(y):
    c = math.sqrt(2.0 / math.pi)
    return 0.5 * y * (1.0 + jnp.tanh(c * (y + 0.044715 * y * y * y)))


def _gelu_grad(y):
    c = math.sqrt(2.0 / math.pi)
    th = jnp.tanh(c * (y + 0.044715 * y * y * y))
    return 0.5 * (1.0 + th) + 0.5 * y * (1.0 - th * th) * c * (1.0 + 3.0 * 0.044715 * y * y)


def _glu_fwd(name, y_f, y_b, proj, d_row, w):
    l, wd = y_f.shape
    tm = _tile(l, 512, SUBLANE)

    def body(yf_ref, yb_ref, u_ref, d_ref, w_ref, y_ref, o_ref):
        y = yf_ref[...] + yb_ref[...] + u_ref[...].astype(F32) * d_ref[...]
        y_ref[...] = y
        gy = _gelu(y)
        z = _dot(gy, w_ref[...])
        o_ref[...] = (gy * _sigmoid(z)).astype(o_ref.dtype)

    row = pl.BlockSpec((tm, wd), lambda i: (i, 0))
    return pl.pallas_call(
        body, name=name, grid=(l // tm,),
        in_specs=[row, row, row, pl.BlockSpec((1, wd), lambda i: (0, 0)), pl.BlockSpec((wd, wd), lambda i: (0, 0))],
        out_specs=[row, row], out_shape=[_sds((l, wd), F32), _sds((l, wd), BF)],
        compiler_params=_params(("arbitrary",)),
    )(y_f, y_b, proj, d_row, w)


def _glu_bwd(name, y, w, dout, dcol):
    l, wd = y.shape
    tm = _tile(l, 512, SUBLANE)

    def body(y_ref, w_ref, d_ref, dy_ref, dw_ref):
        i = pl.program_id(0)
        yv = y_ref[...]
        gy = _gelu(yv)
        s = _sigmoid(_dot(gy, w_ref[...]))
        d = d_ref[...].astype(F32)
        t = d * gy * s * (1.0 - s)
        dgy = d * s + _dot(t, w_ref[...], NT)
        dy_ref[...] = dgy * _gelu_grad(yv)
        part = _dot(gy, t, TN)

        @pl.when(i == 0)
        def _():
            dw_ref[...] = part

        @pl.when(i > 0)
        def _():
            dw_ref[...] += part

    return pl.pallas_call(
        body, name=name, grid=(l // tm,),
        in_specs=[pl.BlockSpec((tm, wd), lambda i: (i, 0)), pl.BlockSpec((wd, wd), lambda i: (0, 0)),
                  pl.BlockSpec((tm, wd), lambda i: (i, dcol))],
        out_specs=[pl.BlockSpec((tm, wd), lambda i: (i, 0)), pl.BlockSpec((wd, wd), lambda i: (0, 0))],
        out_shape=[_sds((l, wd), F32), _sds((wd, wd), F32)],
        compiler_params=_params(("arbitrary",)),
    )(y, w, dout)


def _log_sigmoid(x):
    return jnp.minimum(x, 0.0) - jnp.log(1.0 + jnp.exp(-jnp.abs(x)))


def _gate_fwd(name, glo, wf, wb, bf, bb):
    l, r2 = glo.shape
    hk = wf.shape[1]
    tm = _tile(l, 1024, SUBLANE)

    def body(x_ref, wf_ref, wb_ref, bf_ref, bb_ref, gf_ref, gb_ref):
        xv = x_ref[...]
        gf_ref[...] = _log_sigmoid(_dot(xv, wf_ref[...]) + bf_ref[...]) * (1.0 / GLA_GATE_NORM)
        gb_ref[...] = _log_sigmoid(_dot(xv, wb_ref[...]) + bb_ref[...]) * (1.0 / GLA_GATE_NORM)

    w_spec = pl.BlockSpec((r2, hk), lambda i: (0, 0))
    b_spec = pl.BlockSpec((1, hk), lambda i: (0, 0))
    o_spec = pl.BlockSpec((tm, hk), lambda i: (i, 0))
    return pl.pallas_call(
        body, name=name, grid=(l // tm,),
        in_specs=[pl.BlockSpec((tm, r2), lambda i: (i, 0)), w_spec, w_spec, b_spec, b_spec],
        out_specs=[o_spec, o_spec], out_shape=[_sds((l, hk), F32), _sds((l, hk), F32)],
        compiler_params=_params(("arbitrary",)),
    )(glo, wf, wb, bf, bb)


def _gate_bwd(name, glo, wf, wb, bf, bb, dgf, dgb):
    l, r2 = glo.shape
    hk = wf.shape[1]
    tm = _tile(l, 1024, SUBLANE)

    def body(x_ref, wf_ref, wb_ref, bf_ref, bb_ref, dgf_ref, dgb_ref, dx_ref, dwf_ref, dwb_ref, dbf_ref, dbb_ref):
        i = pl.program_id(0)
        xv = x_ref[...]
        kf = _dot(xv, wf_ref[...]) + bf_ref[...]
        kb = _dot(xv, wb_ref[...]) + bb_ref[...]
        dkf = dgf_ref[...] * (1.0 / GLA_GATE_NORM) * _sigmoid(-kf)
        dkb = dgb_ref[...] * (1.0 / GLA_GATE_NORM) * _sigmoid(-kb)
        dx_ref[...] = _dot(dkf, wf_ref[...], NT) + _dot(dkb, wb_ref[...], NT)
        parts = (_dot(xv, dkf, TN), _dot(xv, dkb, TN), jnp.sum(dkf, axis=0, keepdims=True), jnp.sum(dkb, axis=0, keepdims=True))
        accs = (dwf_ref, dwb_ref, dbf_ref, dbb_ref)

        @pl.when(i == 0)
        def _():
            for a_, p_ in zip(accs, parts):
                a_[...] = p_

        @pl.when(i > 0)
        def _():
            for a_, p_ in zip(accs, parts):
                a_[...] += p_

    w_spec = pl.BlockSpec((r2, hk), lambda i: (0, 0))
    b_spec = pl.BlockSpec((1, hk), lambda i: (0, 0))
    g_spec = pl.BlockSpec((tm, hk), lambda i: (i, 0))
    x_spec = pl.BlockSpec((tm, r2), lambda i: (i, 0))
    return pl.pallas_call(
        body, name=name, grid=(l // tm,),
        in_specs=[x_spec, w_spec, w_spec, b_spec, b_spec, g_spec, g_spec],
        out_specs=[x_spec, w_spec, w_spec, b_spec, b_spec],
        out_shape=[_sds((l, r2), F32), _sds((r2, hk), F32), _sds((r2, hk), F32), _sds((1, hk), F32), _sds((1, hk), F32)],
        compiler_params=_params(("arbitrary",)),
    )(glo, wf, wb, bf, bb, dgf, dgb)


def _chunk_terms(qc, kc, gc, lg, chunk, reverse):
    ri = lax.broadcasted_iota(jnp.int32, (chunk, chunk), 0)
    ci = lax.broadcasted_iota(jnp.int32, (chunk, chunk), 1)
    if reverse:
        tri = ci >= ri
        mask = ci > ri
    else:
        tri = ci <= ri
        mask = ci <= ri
    if gc is not None:
        cum = _dot3(tri.astype(F32), gc)
        last = cum[0:1, :] if reverse else cum[chunk - 1:chunk, :]
    else:
        pos = lax.broadcasted_iota(jnp.int32, (chunk, 1), 0).astype(F32)
        cum = ((chunk - pos) if reverse else (pos + 1.0)) * lg
        last = chunk * lg
    e = jnp.exp(cum)
    einv = jnp.exp(-cum)
    dec = jnp.exp(last - cum)
    return e, einv, dec, qc * e, kc * einv, kc * dec, jnp.exp(last), mask, tri


def _lin_specs(arr, width, col, tb, nb, reverse, per_head):
    if per_head:
        return pl.BlockSpec((tb, width), lambda h, r: ((nb - 1 - r) if reverse else r, col + h))
    return pl.BlockSpec((tb, width), lambda h, r: ((nb - 1 - r) if reverse else r, col))


def _lin_fwd(name, q, k, v, g, lgtab, *, heads, hb, dk, dv, chunk, tb, qcol, kcol, vcol, qscale, reverse):
    l = q.shape[0]
    nb = l // tb
    ncb = tb // chunk
    ng = heads // hb
    gated = g is not None
    per_head = ng > 1

    def body(*refs):
        if gated:
            q_ref, k_ref, v_ref, g_ref, o_ref, sp_ref, st = refs
        else:
            q_ref, k_ref, v_ref, lg_ref, o_ref, sp_ref, st = refs
        r = pl.program_id(1)

        @pl.when(r == 0)
        def _():
            st[...] = jnp.zeros_like(st)

        lg = None if gated else lg_ref[0:1, 0:1]

        def one_chunk(c, carry):
            cc = (ncb - 1 - c) if reverse else c
            rows = pl.ds(pl.multiple_of(cc * chunk, chunk), chunk)
            for h in range(hb):
                qc = q_ref[rows, h * dk:(h + 1) * dk].astype(F32) * qscale
                kc = k_ref[rows, h * dk:(h + 1) * dk].astype(F32)
                vc = v_ref[rows, h * dv:(h + 1) * dv]
                gc = g_ref[rows, h * dk:(h + 1) * dk] if gated else None
                _, _, _, qd, ki, kdec, e_last, mask, _ = _chunk_terms(qc, kc, gc, lg, chunk, reverse)
                a = jnp.where(mask, _dot(qd, ki, NT), 0.0)
                s_t = st[h]
                o_ref[rows, h * dv:(h + 1) * dv] = _dot(a, vc) + _dot(qd, s_t, NT)
                sp_ref[cc, h] = s_t
                st[h] = s_t * e_last + _dot(vc, kdec, TN)
            return carry

        lax.fori_loop(0, ncb, one_chunk, 0)

    in_specs = [_lin_specs(q, hb * dk, qcol, tb, nb, reverse, per_head), _lin_specs(k, hb * dk, kcol, tb, nb, reverse, per_head),
                _lin_specs(v, hb * dv, vcol, tb, nb, reverse, per_head)]
    args = [q, k, v]
    if gated:
        in_specs.append(_lin_specs(g, hb * dk, 0, tb, nb, reverse, per_head))
        args.append(g)
    else:
        in_specs.append(pl.BlockSpec((None, 1, LANE), lambda h, r: (h, 0, 0)))
        args.append(lgtab)
    out_specs = [_lin_specs(None, hb * dv, 0, tb, nb, reverse, per_head),
                 pl.BlockSpec((ncb, hb, dv, dk), lambda h, r: ((nb - 1 - r) if reverse else r, h, 0, 0))]
    outs = [_sds((l, heads * dv), F32), _sds((l // chunk, heads, dv, dk), F32)]
    return pl.pallas_call(
        body, name=name, grid=(ng, nb), in_specs=in_specs, out_specs=out_specs, out_shape=outs,
        scratch_shapes=[pltpu.VMEM((hb, dv, dk), F32)],
        compiler_params=_params(("arbitrary", "arbitrary")),
    )(*args)


def _lin_bwd(name, q, k, v, g, lgtab, sprev, do, prev, *, heads, hb, dk, dv, chunk, tb, qcol, kcol, vcol, qscale, reverse):
    l = q.shape[0]
    nb = l // tb
    ncb = tb // chunk
    ng = heads // hb
    gated = g is not None
    per_head = ng > 1
    brev = not reverse
    n_prev = 0 if prev is None else len(prev)

    def body(*refs):
        q_ref, k_ref, v_ref, x_ref, sp_ref, do_ref = refs[:6]
        p_refs = refs[6:6 + n_prev]
        o_refs = refs[6 + n_prev:-1]
        dst = refs[-1]
        dq_ref, dk_ref, dv_ref = o_refs[:3]
        r = pl.program_id(1)

        @pl.when(r == 0)
        def _():
            dst[...] = jnp.zeros_like(dst)

        lg = None if gated else x_ref[0:1, 0:1]

        def one_chunk(c, carry):
            cc = (ncb - 1 - c) if brev else c
            rows = pl.ds(pl.multiple_of(cc * chunk, chunk), chunk)
            for h in range(hb):
                ks = slice(h * dk, (h + 1) * dk)
                vs = slice(h * dv, (h + 1) * dv)
                qc = q_ref[rows, ks].astype(F32) * qscale
                kc = k_ref[rows, ks].astype(F32)
                vc = v_ref[rows, vs]
                gc = x_ref[rows, ks] if gated else None
                e, einv, dec, qd, ki, kdec, e_last, mask, tri = _chunk_terms(qc, kc, gc, lg, chunk, reverse)
                a = jnp.where(mask, _dot(qd, ki, NT), 0.0)
                s_t = sp_ref[cc, h]
                ds_t = dst[h]
                doc = do_ref[rows, vs]
                dvc = _dot(a, doc, TN) + _dot(kdec, ds_t, NT)
                da = jnp.where(mask, _dot(doc, vc, NT), 0.0)
                dqd = _dot(da, ki) + _dot(doc, s_t)
                dki = _dot(da, qd, TN)
                dkdec = _dot(vc, ds_t)
                dst[h] = ds_t * e_last + _dot(doc, qd, TN)
                dqc = dqd * e * qscale
                dkc = dki * einv + dkdec * dec
                if n_prev:
                    dqc = dqc + p_refs[0][rows, ks]
                    dkc = dkc + p_refs[1][rows, ks]
                    dvc = dvc + p_refs[2][rows, vs]
                dq_ref[rows, ks] = dqc
                dk_ref[rows, ks] = dkc
                dv_ref[rows, vs] = dvc
                if gated:
                    dcum = dqd * qd - dki * ki - dkdec * kdec
                    dlast = jnp.sum(dkdec * kdec, axis=0, keepdims=True) + e_last * jnp.sum(s_t * ds_t, axis=0, keepdims=True)
                    rid = lax.broadcasted_iota(jnp.int32, (chunk, 1), 0)
                    dcum = dcum + jnp.where(rid == (0 if reverse else chunk - 1), dlast, 0.0)
                    dgc = _dot3(tri.astype(F32), dcum, TN)
                    o_refs[3][rows, ks] = dgc
            return carry

        lax.fori_loop(0, ncb, one_chunk, 0)

    def spec(width, col):
        return _lin_specs(None, width, col, tb, nb, brev, per_head)

    in_specs = [spec(hb * dk, qcol), spec(hb * dk, kcol), spec(hb * dv, vcol)]
    args = [q, k, v]
    if gated:
        in_specs.append(spec(hb * dk, 0))
        args.append(g)
    else:
        in_specs.append(pl.BlockSpec((None, 1, LANE), lambda h, r: (h, 0, 0)))
        args.append(lgtab)
    in_specs.append(pl.BlockSpec((ncb, hb, dv, dk), lambda h, r: ((nb - 1 - r) if brev else r, h, 0, 0)))
    args.append(sprev)
    in_specs.append(spec(hb * dv, 0))
    args.append(do)
    out_specs = [spec(hb * dk, 0), spec(hb * dk, 0), spec(hb * dv, 0)]
    outs = [_sds((l, heads * dk), F32), _sds((l, heads * dk), F32), _sds((l, heads * dv), F32)]
    if gated:
        out_specs.append(spec(hb * dk, 0))
        outs.append(_sds((l, heads * dk), F32))
    if n_prev:
        in_specs += out_specs[:3]
        args += list(prev)
    return pl.pallas_call(
        body, name=name, grid=(ng, nb), in_specs=in_specs, out_specs=out_specs, out_shape=outs,
        scratch_shapes=[pltpu.VMEM((hb, dv, dk), F32)],
        compiler_params=_params(("arbitrary", "arbitrary")),
    )(*args)


def _headgate_fwd(name, o_f, o_b, og_arr, og_col, gn, dv):
    l, w = o_f.shape
    tm = _tile(l, 512, SUBLANE)
    nh = w // dv

    def body(of_ref, ob_ref, og_ref, gn_ref, out_ref):
        for h in range(nh):
            cs = slice(h * dv, (h + 1) * dv)
            o = of_ref[:, cs] + ob_ref[:, cs]
            r = lax.rsqrt(jnp.mean(o * o, axis=-1, keepdims=True) + EPS)
            og = og_ref[:, cs].astype(F32)
            out_ref[:, cs] = (o * r * gn_ref[:, cs] * (og * _sigmoid(og))).astype(out_ref.dtype)

    row = pl.BlockSpec((tm, w), lambda i: (i, 0))
    return pl.pallas_call(
        body, name=name, grid=(l // tm,),
        in_specs=[row, row, pl.BlockSpec((tm, w), lambda i: (i, og_col)), pl.BlockSpec((1, w), lambda i: (0, 0))],
        out_specs=row, out_shape=_sds((l, w), BF),
        compiler_params=_params(("arbitrary",)),
    )(o_f, o_b, og_arr, gn)


def _headgate_bwd(name, o_f, o_b, og_arr, og_col, gn, dout, dcol, dv):
    l, w = o_f.shape
    tm = _tile(l, 512, SUBLANE)
    nh = w // dv

    def body(of_ref, ob_ref, og_ref, gn_ref, d_ref, do_ref, dog_ref, dgn_ref):
        i = pl.program_id(0)
        for h in range(nh):
            cs = slice(h * dv, (h + 1) * dv)
            o = of_ref[:, cs] + ob_ref[:, cs]
            r = lax.rsqrt(jnp.mean(o * o, axis=-1, keepdims=True) + EPS)
            oh = o * r
            og = og_ref[:, cs].astype(F32)
            s = _sigmoid(og)
            d = d_ref[:, cs].astype(F32)
            gnv = gn_ref[:, cs]
            d_on = d * (og * s)
            dog_ref[:, cs] = (d * (oh * gnv) * s * (1.0 + og * (1.0 - s))).astype(dog_ref.dtype)
            doh = d_on * gnv
            do_ref[:, cs] = r * (doh - oh * jnp.mean(doh * oh, axis=-1, keepdims=True))
            part = jnp.sum(d_on * oh, axis=0, keepdims=True)

            @pl.when(i == 0)
            def _():
                dgn_ref[:, cs] = part

            @pl.when(i > 0)
            def _():
                dgn_ref[:, cs] += part

    row = pl.BlockSpec((tm, w), lambda i: (i, 0))
    vec = pl.BlockSpec((1, w), lambda i: (0, 0))
    return pl.pallas_call(
        body, name=name, grid=(l // tm,),
        in_specs=[row, row, pl.BlockSpec((tm, w), lambda i: (i, og_col)), vec, pl.BlockSpec((tm, w), lambda i: (i, dcol))],
        out_specs=[row, row, vec], out_shape=[_sds((l, w), F32), _sds((l, w), BF), _sds((1, w), F32)],
        compiler_params=_params(("arbitrary",)),
    )(o_f, o_b, og_arr, gn, dout)


def _rot_tables(l, dk):
    half = dk // 2
    pos = jnp.arange(l, dtype=F32)
    inv = jnp.exp(-math.log(ROPE_BASE) * jnp.arange(half, dtype=F32) / half)
    ang = pos[:, None] * inv[None, :]
    cos, sin = jnp.cos(ang), jnp.sin(ang)
    return jnp.concatenate([cos, cos], axis=-1), jnp.concatenate([-sin, sin], axis=-1)


def _rot_apply(name, src_q, qcol, src_k, kcol, cos_t, sin_t, heads, dk, kscale, out_dtype, transpose):
    l = src_q.shape[0]
    w = heads * dk
    tm = _tile(l, 512, SUBLANE)

    def rot(t, cos_v, sin_v):
        if transpose:
            return t * cos_v + pltpu.roll(t * sin_v, dk // 2, 1)
        return t * cos_v + pltpu.roll(t, dk // 2, 1) * sin_v

    def body(q_ref, k_ref, c_ref, s_ref, qo_ref, ko_ref):
        cos_v, sin_v = c_ref[...], s_ref[...]
        for h in range(heads):
            cs = slice(h * dk, (h + 1) * dk)
            qo_ref[:, cs] = rot(q_ref[:, cs].astype(F32), cos_v, sin_v).astype(out_dtype)
            ko_ref[:, cs] = (rot(k_ref[:, cs].astype(F32), cos_v, sin_v) * kscale).astype(out_dtype)

    tab = pl.BlockSpec((tm, dk), lambda i: (i, 0))
    row = pl.BlockSpec((tm, w), lambda i: (i, 0))
    return pl.pallas_call(
        body, name=name, grid=(l // tm,),
        in_specs=[pl.BlockSpec((tm, w), lambda i: (i, qcol)), pl.BlockSpec((tm, w), lambda i: (i, kcol)), tab, tab],
        out_specs=[row, row], out_shape=[_sds((l, w), out_dtype), _sds((l, w), out_dtype)],
        compiler_params=_params(("arbitrary",)),
    )(src_q, src_k, cos_t, sin_t)


def _exchange(name, src, gather):
    shape = src.shape if gather else src.shape[1:]

    def body(src_ref, out_ref, send_sems, recv_sems, local_sem):
        me = _my_index()
        own = pltpu.make_async_copy(src_ref if gather else src_ref.at[me], out_ref.at[me], local_sem)
        own.start()
        copies = _peer_copies(src_ref, out_ref, send_sems, recv_sems, gather)
        for cp in copies:
            cp.start()
        for cp in copies:
            cp.wait_recv()
        for cp in copies:
            cp.wait_send()
        own.wait()

    return pl.pallas_call(
        body, name=name,
        in_specs=[pl.BlockSpec(memory_space=pl.ANY)], out_specs=pl.BlockSpec(memory_space=pl.ANY),
        out_shape=_sds((N_DEV,) + tuple(shape), src.dtype),
        scratch_shapes=[pltpu.SemaphoreType.DMA((N_DEV - 1,)), pltpu.SemaphoreType.DMA((N_DEV - 1,)), pltpu.SemaphoreType.DMA],
        )(src)


def _my_index():
    return 4 * lax.axis_index("x") + 2 * lax.axis_index("y") + lax.axis_index("c")


def _peer_copies(src_ref, out_ref, send_sems, recv_sems, gather):
    x, y, c = lax.axis_index("x"), lax.axis_index("y"), lax.axis_index("c")
    me = 4 * x + 2 * y + c
    copies = []
    for kk in range(1, N_DEV):
        px = (1 - x) if kk & 4 else x
        py = (1 - y) if kk & 2 else y
        pc = (1 - c) if kk & 1 else c
        peer = 4 * px + 2 * py + pc
        copies.append(pltpu.make_async_remote_copy(
            src_ref=src_ref if gather else src_ref.at[peer], dst_ref=out_ref.at[me],
            send_sem=send_sems.at[kk - 1], recv_sem=recv_sems.at[kk - 1],
            device_id=(px, py, pc), device_id_type=pl.DeviceIdType.MESH))
    return copies


_HBM = pl.BlockSpec(memory_space=pltpu.HBM)
_SEM = pl.BlockSpec(memory_space=pltpu.SEMAPHORE)
_EFFECT = pltpu.SideEffectType.DATAFLOW_SIDE_EFFECTING


def _exchange_start(name, src, gather):
    shape = src.shape if gather else src.shape[1:]
    land = lax.empty((N_DEV,) + tuple(shape), src.dtype)

    def body(src_ref, land_ref, send_sems, recv_sems, src_thru, land_thru, token):
        for cp in _peer_copies(src_ref, land_ref, send_sems, recv_sems, gather):
            cp.start()
        token[...] = jnp.zeros_like(token)

    return pl.pallas_call(
        body, name=name,
        out_shape=(pltpu.SemaphoreType.DMA((N_DEV - 1,)), pltpu.SemaphoreType.DMA((N_DEV - 1,)),
                   pltpu.HBM(src.shape, src.dtype), pltpu.HBM(land.shape, land.dtype), _sds((SUBLANE, LANE), F32)),
        in_specs=(_HBM, _HBM), out_specs=(_SEM, _SEM, _HBM, _HBM, pl.BlockSpec(memory_space=pltpu.VMEM)),
        input_output_aliases={0: 2, 1: 3},
        compiler_params=pltpu.CompilerParams(has_side_effects=_EFFECT),
    )(pltpu.with_memory_space_constraint(src, pltpu.HBM), pltpu.with_memory_space_constraint(land, pltpu.HBM))


def _exchange_wait(name, started, gather, after):
    send_sems, recv_sems, src_thru, land_thru, _ = started

    def body(src_ref, land_ref, send_sems, recv_sems, after_ref, src_out, land_out):
        copies = _peer_copies(src_ref, land_ref, send_sems, recv_sems, gather)
        for cp in copies:
            cp.wait_send()
        for cp in copies:
            cp.wait_recv()

    return pl.pallas_call(
        body, name=name,
        out_shape=(pltpu.HBM(src_thru.shape, src_thru.dtype), pltpu.HBM(land_thru.shape, land_thru.dtype)),
        in_specs=(_HBM, _HBM, _SEM, _SEM, pl.BlockSpec(memory_space=pl.ANY)), out_specs=(_HBM, _HBM),
        input_output_aliases={0: 0, 1: 1},
        compiler_params=pltpu.CompilerParams(has_side_effects=_EFFECT),
    )(src_thru, land_thru, send_sems, recv_sems, after)


def _adam_math(w, gsum, m, v):
    m2 = ADAM_B1 * m + (1.0 - ADAM_B1) * gsum
    v2 = ADAM_B2 * v + (1.0 - ADAM_B2) * (gsum * gsum)
    m_hat = m2 / (1.0 - ADAM_B1 ** ADAM_STEP)
    v_hat = v2 / (1.0 - ADAM_B2 ** ADAM_STEP)
    delta = -ADAM_LR * (m_hat / (jnp.sqrt(v_hat) + ADAM_EPS) + ADAM_WD * w)
    return delta, m2, v2


def _reduce_adam(name, parts, w, m, v):
    r, c = w.shape
    tr = _tile(r, 256, 16)

    def body(p_ref, w_ref, m_ref, v_ref, g_ref, d_ref, m2_ref, v2_ref):
        gsum = p_ref[0].astype(F32)
        for s in range(1, N_DEV):
            gsum = gsum + p_ref[s].astype(F32)
        g_ref[...] = gsum
        delta, m2, v2 = _adam_math(w_ref[...], gsum, m_ref[...], v_ref[...])
        d_ref[...] = delta
        m2_ref[...] = m2
        v2_ref[...] = v2

    row = pl.BlockSpec((tr, c), lambda i: (i, 0))
    return pl.pallas_call(
        body, name=name, grid=(r // tr,),
        in_specs=[pl.BlockSpec((N_DEV, tr, c), lambda i: (0, i, 0)), row, row, row],
        out_specs=[row, row, row, row], out_shape=[_sds((r, c), F32)] * 4,
        compiler_params=_params(("arbitrary",)),
    )(parts, w, m, v)


def _reduce8(name, parts):
    _, r, c = parts.shape

    def body(p_ref, g_ref):
        gsum = p_ref[0]
        for s in range(1, N_DEV):
            gsum = gsum + p_ref[s]
        g_ref[...] = gsum

    return pl.pallas_call(
        body, name=name, grid=(1,),
        in_specs=[pl.BlockSpec((N_DEV, r, c), lambda i: (0, 0, 0))],
        out_specs=pl.BlockSpec((r, c), lambda i: (0, 0)), out_shape=_sds((r, c), F32),
        compiler_params=_params(("arbitrary",)),
    )(parts)


def _adam_packed(name, w, g, m, v):
    r, c = w.shape

    def body(w_ref, g_ref, m_ref, v_ref, d_ref, m2_ref, v2_ref):
        delta, m2, v2 = _adam_math(w_ref[...], g_ref[...], m_ref[...], v_ref[...])
        d_ref[...] = delta
        m2_ref[...] = m2
        v2_ref[...] = v2

    spec = pl.BlockSpec((r, c), lambda i: (0, 0))
    return pl.pallas_call(
        body, name=name, grid=(1,), in_specs=[spec] * 4, out_specs=[spec] * 3, out_shape=[_sds((r, c), F32)] * 3,
        compiler_params=_params(("arbitrary",)),
    )(w, g, m, v)


def _pack(arrs):
    flat = jnp.concatenate([a.reshape(-1).astype(F32) for a in arrs])
    n = flat.shape[0]
    pad = (-n) % (SUBLANE * LANE)
    return jnp.pad(flat, (0, pad)).reshape(-1, LANE)


def _unpack(packed, like):
    flat = packed.reshape(-1)
    out, off = [], 0
    for a in like:
        n = math.prod(a.shape)
        out.append(flat[off:off + n].reshape(a.shape))
        off += n
    return out


def _row_blocks(full):
    return full.reshape(N_DEV, full.shape[0] // N_DEV, full.shape[1])


def _col_blocks(full):
    r, c = full.shape
    return full.reshape(r, N_DEV, c // N_DEV).transpose(1, 0, 2)


def _cols_natural(blocks):
    n, r, c = blocks.shape
    return blocks.transpose(1, 0, 2).reshape(r, n * c)


def kernel(x, ffn1_norm, ffn1_w1, ffn1_w2, mix_norm, ffn2_norm, ffn2_w1, ffn2_w2, ab_w_in, s5_lambda_re, s5_lambda_im, s5_b_re, s5_b_im, s5_c_re, s5_c_im, s5_log_dt, s5_d, s5_w_glu, gla_w_gk, gla_b_gk, gla_norm, ab_w_out, ret_w_in, ret_norm, ret_w_out, final_norm, loss_target, m_ffn1_norm, m_ffn1_w1, m_ffn1_w2, m_mix_norm, m_ffn2_norm, m_ffn2_w1, m_ffn2_w2, m_ab_w_in, m_s5_lambda_re, m_s5_lambda_im, m_s5_b_re, m_s5_b_im, m_s5_c_re, m_s5_c_im, m_s5_log_dt, m_s5_d, m_s5_w_glu, m_gla_w_gk, m_gla_b_gk, m_gla_norm, m_ab_w_out, m_ret_w_in, m_ret_norm, m_ret_w_out, m_final_norm, v_ffn1_norm, v_ffn1_w1, v_ffn1_w2, v_mix_norm, v_ffn2_norm, v_ffn2_w1, v_ffn2_w2, v_ab_w_in, v_s5_lambda_re, v_s5_lambda_im, v_s5_b_re, v_s5_b_im, v_s5_c_re, v_s5_c_im, v_s5_log_dt, v_s5_d, v_s5_w_glu, v_gla_w_gk, v_gla_b_gk, v_gla_norm, v_ab_w_out, v_ret_w_in, v_ret_norm, v_ret_w_out, v_final_norm):
    names = ['ffn1_norm', 'ffn1_w1', 'ffn1_w2', 'mix_norm', 'ffn2_norm', 'ffn2_w1', 'ffn2_w2', 'ab_w_in', 's5_lambda_re', 's5_lambda_im', 's5_b_re', 's5_b_im', 's5_c_re', 's5_c_im', 's5_log_dt', 's5_d', 's5_w_glu', 'gla_w_gk', 'gla_b_gk', 'gla_norm', 'ab_w_out', 'ret_w_in', 'ret_norm', 'ret_w_out', 'final_norm']
    loc = locals()
    W = {n: loc[n] for n in names}
    M = {n: loc["m_" + n] for n in names}
    V = {n: loc["v_" + n] for n in names}

    me = 4 * lax.axis_index("x") + 2 * lax.axis_index("y") + lax.axis_index("c")
    xs = x[0]
    tgt = loss_target[0]
    l, d = xs.shape
    depth = ffn1_norm.shape[0]

    pending, tokens = {}, []

    def start_gather(tag, shard):
        started = _exchange_start("ags_" + tag, shard, True)
        pending[tag] = (started, shard)
        tokens.append(started[4][0, 0])

    def finish_gather(tag, after):
        started, shard = pending.pop(tag)
        _, got = _exchange_wait("agw_" + tag, started, True, after)
        return lax.dynamic_update_index_in_dim(got, shard, me, 0)

    def finish_cols(tag, after):
        g = finish_gather(tag, after)
        return g.transpose(1, 0, 2).reshape(g.shape[1], -1)

    def finish_rows(tag, after):
        g = finish_gather(tag, after)
        return g.reshape(-1, g.shape[2])

    small_sharded = [gla_w_gk, gla_b_gk, ret_norm]
    for i in range(depth):
        j = i // 2
        start_gather(f"ffn1_w1_{i}", ffn1_w1[i].astype(BF))
        start_gather(f"ffn1_w2_{i}", ffn1_w2[i].astype(BF))
        if i % 2 == 0:
            start_gather(f"ab_w_in_{j}", ab_w_in[j].astype(BF))
            if i == 0:
                start_gather("small", _pack(small_sharded))
            start_gather(f"s5_w_glu_{j}", s5_w_glu[j].astype(BF))
            start_gather(f"ab_w_out_{j}", ab_w_out[j].astype(BF))
        else:
            start_gather(f"ret_w_in_{j}", ret_w_in[j].astype(BF))
            start_gather(f"ret_w_out_{j}", ret_w_out[j].astype(BF))
        start_gather(f"ffn2_w1_{i}", ffn2_w1[i].astype(BF))
        start_gather(f"ffn2_w2_{i}", ffn2_w2[i].astype(BF))
    started_all = functools.reduce(lambda a, b: a + b, tokens)
    full = {}

    s5w = s5_d.shape[1]
    g_s5, n_s5 = s5_lambda_re.shape[2], s5_lambda_re.shape[3]
    hs = min(SUBLANE * LANE, g_s5 * n_s5)
    gla_hk = gla_w_gk.shape[-1] * N_DEV
    gla_dk = gla_hk // GLA_HEADS
    gla_hv = gla_norm.shape[1]
    gla_dv = gla_hv // GLA_HEADS
    ret_hv = ret_norm.shape[1] * N_DEV
    ret_dv = ret_hv // RET_HEADS
    ret_hk = (ret_w_in.shape[2] * N_DEV - 2 * ret_hv) // 2
    ret_dk = ret_hk // RET_HEADS
    assert s5w == gla_hv and 2 * gla_hk == s5w, "column blocks of the mixer projection assume these widths"
    assert ret_hv == 2 * ret_hk
    main_w = s5w + 2 * gla_hk + 2 * gla_hv
    gla_tb = _tile(l, 256, GLA_CHUNK)
    ret_chunk = min(RET_CHUNK, l)

    cos_t, sin_t = _rot_tables(l, ret_dk)
    lg_f = jnp.log1p(-jnp.exp2(-5.0 - jnp.arange(RET_HEADS, dtype=F32)))
    lgtab_f = jnp.broadcast_to(lg_f[:, None, None], (RET_HEADS, 1, LANE))
    lgtab_b = jnp.broadcast_to(lg_f[::-1][:, None, None], (RET_HEADS, 1, LANE))

    saved = []
    cur = xs
    for i in range(depth):
        j = i // 2
        s = {}
        s['x0'] = cur
        s['f1w1'], s['f1w2'] = finish_cols(f"ffn1_w1_{i}", cur), finish_rows(f"ffn1_w2_{i}", cur)
        g1 = ffn1_norm[i:i + 1] + started_all if i == 0 else ffn1_norm[i:i + 1]
        cur, s['ffn1'] = _ffn_fwd(f"l{i}_ffn1", cur, g1, s['f1w1'], s['f1w2'])
        s['x1'] = cur
        h = _rms_fwd(f"l{i}_mixnorm", cur, mix_norm[i:i + 1])
        s['h'] = h
        if i % 2 == 0:
            w_in = finish_cols(f"ab_w_in_{j}", cur)
            if i == 0:
                got = finish_gather("small", cur)
                per_dev = [_unpack(got[p], small_sharded) for p in range(N_DEV)]
                full['gla_w_gk'] = jnp.concatenate([t[0] for t in per_dev], axis=-1).astype(BF)
                full['gla_b_gk'] = jnp.concatenate([t[1] for t in per_dev], axis=-1)
                ret_norm_full = jnp.concatenate([t[2] for t in per_dev], axis=-1)
            s['w_glu'], s['w_out'] = finish_rows(f"s5_w_glu_{j}", cur), finish_rows(f"ab_w_out_{j}", cur)
            w_main, w_glo = w_in[:, :main_w], w_in[:, main_w:]
            proj = _mm_plain(f"l{i}_proj", h, w_main, NN, BF, tm=512, tn=1024, tk=d)
            glo = _mm_plain(f"l{i}_glo", h, w_glo, NN, F32, tm=1024, tn=2 * GLA_RANK, tk=d)
            s5_args = (s5_lambda_re[j], s5_lambda_im[j], s5_b_re[j], s5_b_im[j], s5_c_re[j], s5_c_im[j], s5_log_dt[j])
            (a_tab, bd, cd), s5_vjp = jax.vjp(lambda *a: _s5_chunk_tables(*a, hs), *s5_args)
            bd16, cd16 = bd.astype(BF), cd.astype(BF)
            tm = _tile(l, 512, SUBLANE)
            x_f, y_f = _s5_fwd(f"l{i}_s5_fwd_f", proj, bd16[0], cd16[0], a_tab[0], False)
            x_b, y_b = _s5_fwd(f"l{i}_s5_fwd_b", proj, bd16[1], cd16[1], a_tab[1], True)
            d_row = s5_d[j:j + 1]
            y, s5_out = _glu_fwd(f"l{i}_s5_glu", y_f, y_b, proj, d_row, s['w_glu'])
            zeros_r = jnp.zeros((GLA_RANK, gla_hk), BF)
            w_gk = full['gla_w_gk'][j]
            wgk_f = jnp.concatenate([w_gk[0], zeros_r], axis=0)
            wgk_b = jnp.concatenate([zeros_r, w_gk[1]], axis=0)
            b_gk = full['gla_b_gk'][j]
            g_f, g_b = _gate_fwd(f"l{i}_gla_gate", glo, wgk_f, wgk_b, b_gk[0:1], b_gk[1:2])
            qcol, kcol, vcol, ogcol = s5w // gla_hk, s5w // gla_hk + 1, (s5w + 2 * gla_hk) // gla_hv, (s5w + 2 * gla_hk) // gla_hv + 1
            lin_kw = dict(heads=GLA_HEADS, hb=GLA_HEADS, dk=gla_dk, dv=gla_dv, chunk=GLA_CHUNK, tb=gla_tb, qcol=qcol, kcol=kcol, vcol=vcol,
                          qscale=gla_dk ** -0.5)
            o_f, sp_f = _lin_fwd(f"l{i}_gla_fwd_f", proj, proj, proj, g_f, None, reverse=False, **lin_kw)
            o_b, sp_b = _lin_fwd(f"l{i}_gla_fwd_b", proj, proj, proj, g_b, None, reverse=True, **lin_kw)
            gla_out = _headgate_fwd(f"l{i}_gla_out", o_f, o_b, proj, ogcol, gla_norm[j:j + 1], gla_dv)
            w_out = s['w_out']

            def epi_res(acc, e_refs, o_refs, ids):
                o_refs[0][...] = e_refs[0][...] + acc

            row = pl.BlockSpec((tm, d), lambda ii, jj, kk: (ii, 0))
            cur = _mm(f"l{i}_mix_out",
                      [(s5_out, pl.BlockSpec((tm, s5w), lambda ii, jj, kk: (ii, 0)), w_out, pl.BlockSpec((s5w, d), lambda ii, jj, kk: (0, 0))),
                       (gla_out, pl.BlockSpec((tm, gla_hv), lambda ii, jj, kk: (ii, 0)), w_out, pl.BlockSpec((gla_hv, d), lambda ii, jj, kk: (1, 0)))],
                      NN, (l // tm, 1, 1), [_sds((l, d), F32)], [row], (tm, d), epi_res, [cur], [row])[0]
            s.update(proj=proj, glo=glo, s5_vjp=s5_vjp, a_tab=a_tab, bd16=bd16, cd16=cd16, x_f=x_f, x_b=x_b, y=y, s5_out=s5_out,
                     wgk_f=wgk_f, wgk_b=wgk_b, b_gk=b_gk, g_f=g_f, g_b=g_b, o_f=o_f, o_b=o_b, sp_f=sp_f, sp_b=sp_b, gla_out=gla_out,
                     w_main=w_main, w_glo=w_glo, lin_kw=lin_kw, ogcol=ogcol)
        else:
            w_in = finish_cols(f"ret_w_in_{j}", cur)
            s['w_in'], s['w_out'] = w_in, finish_rows(f"ret_w_out_{j}", cur)
            proj = _mm_plain(f"l{i}_proj", h, w_in, NN, BF, tm=512, tn=1024, tk=d)
            qr, kr = _rot_apply(f"l{i}_rot", proj, 0, proj, 1, cos_t, sin_t, RET_HEADS, ret_dk, ret_dk ** -0.5, BF, False)
            lin_kw = dict(heads=RET_HEADS, hb=1, dk=ret_dk, dv=ret_dv, chunk=ret_chunk, tb=ret_chunk, qcol=0, kcol=0, vcol=(2 * ret_hk) // ret_dv,
                          qscale=1.0)
            o_f, sp_f = _lin_fwd(f"l{i}_ret_fwd_f", qr, kr, proj, None, lgtab_f, reverse=False, **lin_kw)
            o_b, sp_b = _lin_fwd(f"l{i}_ret_fwd_b", qr, kr, proj, None, lgtab_b, reverse=True, **lin_kw)
            ogcol = (2 * ret_hk + ret_hv) // ret_hv
            r_out = _headgate_fwd(f"l{i}_ret_out", o_f, o_b, proj, ogcol, ret_norm_full, ret_dv)

            def epi_res(acc, e_refs, o_refs, ids):
                o_refs[0][...] = e_refs[0][...] + acc

            cur = _mm_plain(f"l{i}_mix_out", r_out, s['w_out'], NN, F32, tm=512, tn=d, tk=1024, epi=epi_res, eins=[cur])
            s.update(proj=proj, qr=qr, kr=kr, o_f=o_f, o_b=o_b, sp_f=sp_f, sp_b=sp_b, r_out=r_out, lin_kw=lin_kw, ogcol=ogcol)
        s['x2'] = cur
        s['f2w1'], s['f2w2'] = finish_cols(f"ffn2_w1_{i}", cur), finish_rows(f"ffn2_w2_{i}", cur)
        cur, s['ffn2'] = _ffn_fwd(f"l{i}_ffn2", cur, ffn2_norm[i:i + 1], s['f2w1'], s['f2w2'])
        saved.append(s)

    dx, d_final_norm, loss_row = _loss_head("loss_head", cur, final_norm.reshape(1, -1), tgt)
    loss = lax.psum(loss_row[0, 0], ("x", "y", "c"))

    G = {}
    big = {}
    G['final_norm'] = d_final_norm.reshape(-1)
    per_layer = {n: [None] * depth for n in ['ffn1_norm', 'mix_norm', 'ffn2_norm']}
    a2a, tok = {}, [jnp.zeros((), F32)]

    def start_a2a(tag, blocks):
        started = _exchange_start("a2as_" + tag, blocks, False)
        a2a[tag] = started
        tok[0] = tok[0] + started[4][0, 0]

    def dep(vec):
        return vec + tok[0]

    for i in reversed(range(depth)):
        j = i // 2
        s = saved[i]
        def ffn_grads(which):
            def on_grads(dw1, dw2, gnorm):
                start_a2a(f"{which}_w1_{i}", _col_blocks(dw1))
                start_a2a(f"{which}_w2_{i}", _row_blocks(dw2))
                return dep(gnorm)
            return on_grads

        dx, dg = _ffn_bwd(f"l{i}_ffn2b", dx, s['x2'], ffn2_norm[i:i + 1], s['f2w1'], s['f2w2'], s['ffn2'], ffn_grads("ffn2"))
        per_layer['ffn2_norm'][i] = dg[0]
        tm = _tile(l, 512, SUBLANE)
        row = pl.BlockSpec((tm, d), lambda ii, jj, kk: (ii, 0))
        vec = pl.BlockSpec((1, d), lambda ii, jj, kk: (0, 0))
        if i % 2 == 0:
            proj, lin_kw = s['proj'], s['lin_kw']
            w_out = s['w_out']
            d_cat = _mm_plain(f"l{i}_dcat", dx, w_out, NT, BF, tm=512, tn=1024, tk=d)
            dwo_a = _mm_plain(f"l{i}_dwout_a", s['s5_out'], dx, TN, BF, tm=s5w, tn=d, tk=512)
            dwo_b = _mm_plain(f"l{i}_dwout_b", s['gla_out'], dx, TN, BF, tm=gla_hv, tn=d, tk=512)
            start_a2a(f"ab_w_out_{j}", _row_blocks(jnp.concatenate([dwo_a, dwo_b], axis=0)))
            do, dog, dgn = _headgate_bwd(f"l{i}_gla_outb", s['o_f'], s['o_b'], proj, s['ogcol'], dep(gla_norm[j:j + 1]), d_cat, 1, gla_dv)
            G['gla_norm'] = dgn
            r1 = _lin_bwd(f"l{i}_gla_bwd_f", proj, proj, proj, s['g_f'], None, s['sp_f'], do, None, reverse=False, **lin_kw)
            dq, dk_, dv_, dgf = r1
            r2 = _lin_bwd(f"l{i}_gla_bwd_b", proj, proj, proj, s['g_b'], None, s['sp_b'], do, (dq, dk_, dv_), reverse=True, **lin_kw)
            dq, dk_, dv_, dgb = r2
            dglo, dwf, dwb, dbf, dbb = _gate_bwd(f"l{i}_gla_gateb", s['glo'], s['wgk_f'], s['wgk_b'], s['b_gk'][0:1], s['b_gk'][1:2], dgf, dgb)
            G['gla_w_gk'] = jnp.stack([dwf[:GLA_RANK], dwb[GLA_RANK:]], axis=0)[None]
            G['gla_b_gk'] = jnp.concatenate([dbf, dbb], axis=0)[None]
            dy, dwglu = _glu_bwd(f"l{i}_s5_glub", s['y'], s['w_glu'], d_cat, 0)
            start_a2a(f"s5_w_glu_{j}", _row_blocks(dwglu.astype(BF)))
            cd16, bd16, a_tab = s['cd16'], s['bd16'], s['a_tab']
            nt2 = a_tab.shape[2]
            a_conj = a_tab * jnp.where(jnp.arange(nt2) < nt2 // 2, 1.0, -1.0)[None, None, :, None]
            du_f, dbd_f, dcd_f, da_f = _s5_bwd(f"l{i}_s5_bwd_f", proj, dy, s['x_f'], bd16[0], cd16[0], a_conj[0], True)
            du_b, dbd_b, dcd_b, da_b = _s5_bwd(f"l{i}_s5_bwd_b", proj, dy, s['x_b'], bd16[1], cd16[1], a_conj[1], False)
            du, dd = _s5_du(f"l{i}_s5_du", du_f, du_b, dy, proj, s5_d[j:j + 1])
            G['s5_d'] = dd
            cot = (jnp.stack([da_f, da_b]), jnp.stack([dbd_f, dbd_b]), jnp.stack([dcd_f, dcd_b]))
            g_lre, g_lim, g_bre, g_bim, g_cre, g_cim, g_ldt = s['s5_vjp'](cot)
            G['s5_lambda_re'], G['s5_lambda_im'], G['s5_b_re'], G['s5_b_im'] = g_lre[None], g_lim[None], g_bre[None], g_bim[None]
            G['s5_c_re'], G['s5_c_im'], G['s5_log_dt'] = g_cre[None], g_cim[None], g_ldt[None]
            dproj = jnp.concatenate([du, dq.astype(BF), dk_.astype(BF), dv_.astype(BF), dog], axis=1)
            r2w = 2 * GLA_RANK
            pairs = [(dproj, pl.BlockSpec((tm, main_w), lambda ii, jj, kk: (ii, 0)), s['w_main'], pl.BlockSpec((d, main_w), lambda ii, jj, kk: (0, 0))),
                     (dglo, pl.BlockSpec((tm, r2w), lambda ii, jj, kk: (ii, 0)), s['w_glo'], pl.BlockSpec((d, r2w), lambda ii, jj, kk: (0, 0)))]
            dx, dg = _mm(f"l{i}_dh", pairs, NT, (l // tm, 1, 1), [_sds((l, d), F32), _sds((1, d), F32)], [row, vec], (tm, d),
                         _rms_bwd_epi(0), [s['x1'], dep(mix_norm[i:i + 1]), dx], [row, vec, row])
            dw_main = _mm_plain(f"l{i}_dwin_main", s['h'], dproj, TN, BF, tm=d, tn=1024, tk=512)
            dw_glo = _mm_plain(f"l{i}_dwin_glo", s['h'], dglo, TN, BF, tm=d, tn=r2w, tk=512)
            start_a2a(f"ab_w_in_{j}", _col_blocks(jnp.concatenate([dw_main, dw_glo], axis=1)))
        else:
            proj, lin_kw = s['proj'], s['lin_kw']
            w_out = s['w_out']
            d_ro = _mm_plain(f"l{i}_dro", dx, w_out, NT, BF, tm=512, tn=1024, tk=d)
            dwo = _mm_plain(f"l{i}_dwout", s['r_out'], dx, TN, BF, tm=1024, tn=d, tk=512)
            start_a2a(f"ret_w_out_{j}", _row_blocks(dwo))
            do, dog, dgn = _headgate_bwd(f"l{i}_ret_outb", s['o_f'], s['o_b'], proj, s['ogcol'], dep(ret_norm_full), d_ro, 0, ret_dv)
            G['ret_norm'] = dgn
            r1 = _lin_bwd(f"l{i}_ret_bwd_f", s['qr'], s['kr'], proj, None, lgtab_f, s['sp_f'], do, None, reverse=False, **lin_kw)
            r2 = _lin_bwd(f"l{i}_ret_bwd_b", s['qr'], s['kr'], proj, None, lgtab_b, s['sp_b'], do, r1, reverse=True, **lin_kw)
            dqr, dkr, dv_ = r2
            dq, dk_ = _rot_apply(f"l{i}_rotb", dqr, 0, dkr, 0, cos_t, sin_t, RET_HEADS, ret_dk, ret_dk ** -0.5, BF, True)
            dproj = jnp.concatenate([dq, dk_, dv_.astype(BF), dog], axis=1)
            dx, dg = _mm_plain(f"l{i}_dh", dproj, s['w_in'], NT, F32, tm=512, tn=d, tk=1024, epi=_rms_bwd_epi(0),
                               eins=[s['x1'], dep(mix_norm[i:i + 1]), dx], especs=[None, vec, None],
                               extra_outs=[_sds((1, d), F32)], extra_specs=[vec])
            dw_in = _mm_plain(f"l{i}_dwin", s['h'], dproj, TN, BF, tm=d, tn=1024, tk=512)
            start_a2a(f"ret_w_in_{j}", _col_blocks(dw_in))
        per_layer['mix_norm'][i] = dg[0]
        dx, dg = _ffn_bwd(f"l{i}_ffn1b", dx, s['x0'], ffn1_norm[i:i + 1], s['f1w1'], s['f1w2'], s['ffn1'], ffn_grads("ffn1"))
        per_layer['ffn1_norm'][i] = dg[0]
    for n in per_layer:
        G[n] = jnp.stack(per_layer[n], axis=0)
    grad_x = dx[None]

    out_g, out_d, out_m, out_v = {}, {}, {}, {}

    def big_update(tag, w, m, v):
        blocks, got = _exchange_wait("a2aw_" + tag, a2a.pop(tag), False, dx)
        parts = lax.dynamic_update_index_in_dim(got, lax.dynamic_index_in_dim(blocks, me, 0, keepdims=False), me, 0)
        return _reduce_adam("upd_" + tag, parts, w, m, v)

    for n in ['ffn2_w1', 'ffn2_w2', 'ffn1_w1', 'ffn1_w2']:
        res = [big_update(f"{n}_{i}", W[n][i], M[n][i], V[n][i]) for i in range(depth)]
        out_g[n], out_d[n], out_m[n], out_v[n] = [jnp.stack([r[t] for r in res], axis=0) for t in range(4)]
    for n in ['ab_w_in', 's5_w_glu', 'ab_w_out', 'ret_w_in', 'ret_w_out']:
        res = big_update(f"{n}_0", W[n][0], M[n][0], V[n][0])
        out_g[n], out_d[n], out_m[n], out_v[n] = [r[None] for r in res]
    assert not a2a and not pending

    small = ['ffn1_norm', 'mix_norm', 'ffn2_norm', 's5_lambda_re', 's5_lambda_im', 's5_b_re', 's5_b_im', 's5_c_re', 's5_c_im',
             's5_log_dt', 's5_d', 'gla_w_gk', 'gla_b_gk', 'gla_norm', 'ret_norm', 'final_norm']
    packed = _pack([G[n] for n in small])
    gathered = _exchange("ag_small_grads", packed, True)
    summed = _reduce8("sum_small_grads", gathered)
    g_full = dict(zip(small, _unpack(summed, [G[n] for n in small])))
    g_small = {}
    for n in small:
        gf = g_full[n]
        if n in ('gla_w_gk', 'gla_b_gk', 'ret_norm'):
            width = W[n].shape[-1]
            gf = lax.dynamic_slice_in_dim(gf, me * width, width, axis=gf.ndim - 1)
        g_small[n] = gf.reshape(W[n].shape)
    pw, pg, pm, pv = (_pack([src[n] for n in small]) for src in (W, g_small, M, V))
    pd, pm2, pv2 = _adam_packed("upd_small", pw, pg, pm, pv)
    like = [W[n] for n in small]
    for n, dd_, mm_, vv_ in zip(small, _unpack(pd, like), _unpack(pm2, like), _unpack(pv2, like)):
        out_g[n], out_d[n], out_m[n], out_v[n] = g_small[n], dd_, mm_, vv_

    return (loss, grad_x, *[out_g[n] for n in names], *[out_d[n] for n in names], *[out_m[n] for n in names], *[out_v[n] for n in names])
```

```python
import functools
import math

import jax
import jax.numpy as jnp
from jax import lax
from jax.experimental import pallas as pl
from jax.experimental.pallas import tpu as pltpu

F32 = jnp.float32
BF = jnp.bfloat16
N_DEV = 8
EPS = 1e-6
S5_GROUP = 16
GLA_HEADS = 4
GLA_RANK = 16
GLA_GATE_NORM = 16.0
RET_HEADS = 8
ROPE_BASE = 10000.0
GLA_CHUNK = 64
RET_CHUNK = 256
ADAM_LR, ADAM_B1, ADAM_B2, ADAM_EPS, ADAM_WD, ADAM_STEP = 0.001, 0.9, 0.999, 1e-08, 0.01, 10
VMEM_LIMIT_BYTES = 56 * 1024 * 1024
LANE = 128
SUBLANE = 8

NN = (((1,), (0,)), ((), ()))
NT = (((1,), (1,)), ((), ()))
TN = (((0,), (0,)), ((), ()))


def _tile(n, pref, align):
    if n <= pref:
        return n
    t = (pref // align) * align
    while t >= align:
        if n % t == 0:
            return t
        t -= align
    return n


def _params(sem):
    return pltpu.CompilerParams(dimension_semantics=sem, vmem_limit_bytes=VMEM_LIMIT_BYTES)


def _dot(a, b, dims=NN):
    return lax.dot_general(a.astype(BF), b.astype(BF), dims, preferred_element_type=F32)


def _dot3(m01, g, dims=NN):
    g1 = g.astype(BF)
    r1 = g - g1.astype(F32)
    g2 = r1.astype(BF)
    g3 = (r1 - g2.astype(F32)).astype(BF)
    m = m01.astype(BF)
    return (lax.dot_general(m, g1, dims, preferred_element_type=F32)
            + lax.dot_general(m, g2, dims, preferred_element_type=F32)
            + lax.dot_general(m, g3, dims, preferred_element_type=F32))


def _sigmoid(x):
    return 1.0 / (1.0 + jnp.exp(-x))


def _mm(name, pairs, dims, grid, outs, out_specs, acc_shape, epi=None, eins=(), especs=()):
    n_p, n_e, n_o = len(pairs), len(eins), len(outs)
    nk = grid[2]

    def body(*refs):
        a_refs = refs[0:2 * n_p:2]
        b_refs = refs[1:2 * n_p:2]
        e_refs = refs[2 * n_p:2 * n_p + n_e]
        o_refs = refs[2 * n_p + n_e:2 * n_p + n_e + n_o]
        acc = refs[-1]
        ids = (pl.program_id(0), pl.program_id(1), pl.program_id(2))

        part = _dot(a_refs[0][...], b_refs[0][...], dims)
        for p in range(1, n_p):
            part = part + _dot(a_refs[p][...], b_refs[p][...], dims)

        def finish(total):
            if epi is None:
                o_refs[0][...] = total.astype(o_refs[0].dtype)
            else:
                epi(total, e_refs, o_refs, ids)

        if nk == 1:
            finish(part)
        else:
            @pl.when(ids[2] == 0)
            def _():
                acc[...] = part

            @pl.when(ids[2] > 0)
            def _():
                acc[...] += part

            @pl.when(ids[2] == nk - 1)
            def _():
                finish(acc[...])

    in_specs, args = [], []
    for a, a_spec, b, b_spec in pairs:
        in_specs += [a_spec, b_spec]
        args += [a, b]
    in_specs += list(especs)
    args += list(eins)
    res = pl.pallas_call(
        body, name=name, grid=grid, in_specs=in_specs, out_specs=list(out_specs), out_shape=list(outs),
        scratch_shapes=[pltpu.VMEM(acc_shape, F32)],
        compiler_params=_params(("arbitrary", "arbitrary", "arbitrary")),
    )(*args)
    return res


def _sds(shape, dtype):
    return jax.ShapeDtypeStruct(shape, dtype)


def _mm_plain(name, a, b, dims, out_dtype, tm=512, tn=1024, tk=1024, epi=None, eins=(), especs=None, extra_outs=(), extra_specs=(),
              b_outer=False):
    if dims == NN:
        (m, k), n = a.shape, b.shape[1]
    elif dims == NT:
        (m, k), n = a.shape, b.shape[0]
    else:
        (k, m), n = a.shape, b.shape[1]
    tm, tn = _tile(m, tm, LANE if dims == TN else SUBLANE), _tile(n, tn, LANE)
    tk = _tile(k, tk, SUBLANE if dims == TN else LANE)
    grid = (n // tn, m // tm, k // tk) if b_outer else (m // tm, n // tn, k // tk)

    def spec(block, index):
        if b_outer:
            return pl.BlockSpec(block, lambda j, i, kk: index(i, j, kk))
        return pl.BlockSpec(block, index)

    if dims == NN:
        a_spec = spec((tm, tk), lambda i, j, kk: (i, kk))
        b_spec = spec((tk, tn), lambda i, j, kk: (kk, j))
    elif dims == NT:
        a_spec = spec((tm, tk), lambda i, j, kk: (i, kk))
        b_spec = spec((tn, tk), lambda i, j, kk: (j, kk))
    else:
        a_spec = spec((tk, tm), lambda i, j, kk: (kk, i))
        b_spec = spec((tk, tn), lambda i, j, kk: (kk, j))
    o_spec = spec((tm, tn), lambda i, j, kk: (i, j))
    if especs is None:
        especs = [o_spec] * len(eins)
    else:
        especs = [o_spec if s is None else s for s in especs]
    res = _mm(name, [(a, a_spec, b, b_spec)], dims, grid, [_sds((m, n), out_dtype)] + list(extra_outs),
              [o_spec] + list(extra_specs), (tm, tn), epi, eins, especs)
    return res if extra_outs else res[0]


def _rms_fwd(name, x, g):
    l, d = x.shape
    tm = _tile(l, 1024, SUBLANE)

    def body(x_ref, g_ref, o_ref):
        xv = x_ref[...]
        r = lax.rsqrt(jnp.mean(xv * xv, axis=-1, keepdims=True) + EPS)
        o_ref[...] = (xv * r * g_ref[...]).astype(o_ref.dtype)

    return pl.pallas_call(
        body, name=name, grid=(l // tm,),
        in_specs=[pl.BlockSpec((tm, d), lambda i: (i, 0)), pl.BlockSpec((1, d), lambda i: (0, 0))],
        out_specs=pl.BlockSpec((tm, d), lambda i: (i, 0)), out_shape=_sds((l, d), BF),
        compiler_params=_params(("arbitrary",)),
    )(x, g)


def _rms_bwd_epi(first_axis):
    def epi(acc, e_refs, o_refs, ids):
        x_ref, g_ref, dr_ref = e_refs
        dx_ref, dg_ref = o_refs
        xv = x_ref[...]
        r = lax.rsqrt(jnp.mean(xv * xv, axis=-1, keepdims=True) + EPS)
        xh = xv * r
        dxh = acc * g_ref[...]
        dx_ref[...] = dr_ref[...] + r * (dxh - xh * jnp.mean(dxh * xh, axis=-1, keepdims=True))
        part = jnp.sum(acc * xh, axis=0, keepdims=True)

        @pl.when(ids[first_axis] == 0)
        def _():
            dg_ref[...] = part

        @pl.when(ids[first_axis] > 0)
        def _():
            dg_ref[...] += part

    return epi


def _loss_head(name, x, g, target):
    l, d = x.shape
    tm = _tile(l, 512, SUBLANE)
    n = l // tm

    def body(x_ref, g_ref, t_ref, dx_ref, dg_ref, loss_ref, lacc):
        i = pl.program_id(0)
        xv = x_ref[...]
        r = lax.rsqrt(jnp.mean(xv * xv, axis=-1, keepdims=True) + EPS)
        xh = xv * r
        e = xh * g_ref[...] - t_ref[...]
        dy = e * (1.0 / d)
        dxh = dy * g_ref[...]
        dx_ref[...] = r * (dxh - xh * jnp.mean(dxh * xh, axis=-1, keepdims=True))
        dg_part = jnp.sum(dy * xh, axis=0, keepdims=True)
        l_part = jnp.sum(e * e, axis=0, keepdims=True)

        @pl.when(i == 0)
        def _():
            dg_ref[...] = dg_part
            lacc[...] = l_part

        @pl.when(i > 0)
        def _():
            dg_ref[...] += dg_part
            lacc[...] += l_part

        @pl.when(i == n - 1)
        def _():
            loss_ref[...] = jnp.zeros_like(loss_ref) + jnp.sum(lacc[...]) * (0.5 / d)

    return pl.pallas_call(
        body, name=name, grid=(n,),
        in_specs=[pl.BlockSpec((tm, d), lambda i: (i, 0)), pl.BlockSpec((1, d), lambda i: (0, 0)),
                  pl.BlockSpec((tm, d), lambda i: (i, 0))],
        out_specs=[pl.BlockSpec((tm, d), lambda i: (i, 0)), pl.BlockSpec((1, d), lambda i: (0, 0)),
                   pl.BlockSpec((1, LANE), lambda i: (0, 0))],
        out_shape=[_sds((l, d), F32), _sds((1, d), F32), _sds((1, LANE), F32)],
        scratch_shapes=[pltpu.VMEM((1, d), F32)],
        compiler_params=_params(("arbitrary",)),
    )(x, g, target)


def _ffn_up(name, hn, w1):
    l, d = hn.shape
    f = w1.shape[1] // 2
    tm, tn = _tile(l, 512, SUBLANE), _tile(f, 1408, LANE)
    nj = f // tn

    def body(h_ref, wg_ref, wu_ref, gu_ref, a_ref):
        h = h_ref[...]
        g = jnp.dot(h, wg_ref[...], preferred_element_type=F32)
        u = jnp.dot(h, wu_ref[...], preferred_element_type=F32)
        s = _sigmoid(g)
        gs = g * s
        gu_ref[0] = (u * (s + gs * (1.0 - s))).astype(BF)
        gu_ref[1] = gs.astype(BF)
        a_ref[...] = (gs * u).astype(BF)

    return pl.pallas_call(
        body, name=name, grid=(nj, l // tm),
        in_specs=[pl.BlockSpec((tm, d), lambda j, i: (i, 0)), pl.BlockSpec((d, tn), lambda j, i: (0, j)),
                  pl.BlockSpec((d, tn), lambda j, i: (0, j + nj))],
        out_specs=[pl.BlockSpec((2, tm, tn), lambda j, i: (0, i, j)), pl.BlockSpec((tm, tn), lambda j, i: (i, j))],
        out_shape=[_sds((2, l, f), BF), _sds((l, f), BF)],
        compiler_params=_params(("arbitrary", "arbitrary")),
    )(hn, w1, w1)


def _ffn_fwd(tag, x, gnorm, w1, w2):
    hn = _rms_fwd(tag + "_norm", x, gnorm)
    gu, a = _ffn_up(tag + "_up", hn, w1)

    def epi(acc, e_refs, o_refs, ids):
        o_refs[0][...] = e_refs[0][...] + 0.5 * acc

    l, d = x.shape
    x_new = _mm_plain(tag + "_down", a, w2, NN, F32, tm=512, tn=d, tk=w2.shape[0], epi=epi, eins=[x])
    return x_new, (hn, gu, a)


def _ffn_bwd(tag, dres, x, gnorm, w1, w2, saved, on_grads):
    hn, gu, a = saved
    l, d = x.shape
    f = w2.shape[0]
    tm, tn = _tile(l, 512, SUBLANE), _tile(f, 1408, LANE)
    nj = f // tn

    def epi_gu(acc, e_refs, o_refs, ids):
        da = 0.5 * acc
        o_refs[0][0] = (da * e_refs[0][0].astype(F32)).astype(BF)
        o_refs[0][1] = (da * e_refs[0][1].astype(F32)).astype(BF)

    gu_spec = pl.BlockSpec((2, tm, tn), lambda j, i, kk: (0, i, j))
    dgu = _mm(tag + "_dgu",
              [(dres, pl.BlockSpec((tm, d), lambda j, i, kk: (i, 0)), w2, pl.BlockSpec((tn, d), lambda j, i, kk: (j, 0)))],
              NT, (nj, l // tm, 1), [_sds((2, l, f), BF)], [gu_spec], (tm, tn), epi_gu, [gu], [gu_spec])[0]

    def epi_half(acc, e_refs, o_refs, ids):
        o_refs[0][...] = (0.5 * acc).astype(BF)

    dw2 = _mm_plain(tag + "_dw2", a, dres, TN, BF, tm=1408, tn=d, tk=512, epi=epi_half)

    tk = _tile(l, 512, SUBLANE)
    dw1 = _mm(tag + "_dw1",
              [(hn, pl.BlockSpec((tk, d), lambda i, j, kk: (kk, 0)), dgu, pl.BlockSpec((None, tk, f), lambda i, j, kk: (j, kk, 0)))],
              TN, (1, 2, l // tk), [_sds((d, 2 * f), BF)], [pl.BlockSpec((d, f), lambda i, j, kk: (0, j))], (d, f))[0]

    gnorm = on_grads(dw1, dw2, gnorm)
    row = pl.BlockSpec((tm, d), lambda i, j, kk: (i, 0))
    vec = pl.BlockSpec((1, d), lambda i, j, kk: (0, 0))
    once = pl.Buffered(1)
    dx, dg = _mm(tag + "_dhn",
                 [(dgu, pl.BlockSpec((None, tm, f), lambda i, j, kk: (0, i, 0)), w1, pl.BlockSpec((d, f), lambda i, j, kk: (0, 0), pipeline_mode=once)),
                  (dgu, pl.BlockSpec((None, tm, f), lambda i, j, kk: (1, i, 0)), w1, pl.BlockSpec((d, f), lambda i, j, kk: (0, 1), pipeline_mode=once))],
                 NT, (l // tm, 1, 1), [_sds((l, d), F32), _sds((1, d), F32)], [row, vec], (tm, d),
                 _rms_bwd_epi(0), [x, gnorm, dres], [row, vec, row])
    return dx, dg


def _s5_chunk_tables(lam_re, lam_im, b_re, b_im, c_re, c_im, log_dt, hs):
    f32 = F32
    g, n = lam_re.shape[1], lam_re.shape[2]
    p = b_re.shape[-1]
    nch, gpc, nt = (g * n) // hs, hs // n, hs // LANE
    lr = jnp.minimum(lam_re.astype(f32), -1e-4)
    li = lam_im.astype(f32)
    dt = jnp.exp(log_dt.astype(f32))[..., None]
    mag = jnp.exp(lr * dt)
    ar = mag * jnp.cos(li * dt)
    ai = mag * jnp.sin(li * dt)
    den = lr * lr + li * li
    cr = ((ar - 1.0) * lr + ai * li) / den
    ci = (ai * lr - (ar - 1.0) * li) / den
    bbr = cr[..., None] * b_re - ci[..., None] * b_im
    bbi = cr[..., None] * b_im + ci[..., None] * b_re
    eye = jnp.eye(gpc, dtype=f32)[None, None, None, :, None, None, :]
    a_f = jnp.stack([ar, ai], axis=1).reshape(2, 2, nch, nt, LANE).transpose(0, 2, 1, 3, 4).reshape(2, nch, 2 * nt, LANE)
    bb = jnp.stack([bbr, bbi], axis=1).reshape(2, 2, nch, gpc, n, p)
    bd = (bb[..., None] * eye).transpose(0, 2, 6, 5, 1, 3, 4).reshape(2, nch, gpc * p, 2 * hs)
    cc = jnp.stack([c_re, -c_im], axis=1).reshape(2, 2, nch, gpc, p, n)
    cd = (cc[..., None] * eye).transpose(0, 2, 1, 3, 5, 6, 4).reshape(2, nch, 2 * hs, gpc * p)
    return a_f, bd, cd


def _fold_store(ref, val, tb, ntiles):
    for s in range(ntiles):
        ref[:, s * SUBLANE:(s + 1) * SUBLANE, :] = val[:, s * LANE:(s + 1) * LANE].reshape(tb // SUBLANE, SUBLANE, LANE)


def _unfold(ref, tb, ntiles):
    return jnp.concatenate([ref[:, s * SUBLANE:(s + 1) * SUBLANE, :].reshape(tb, LANE) for s in range(ntiles)], axis=1)


def _s5_fwd(name, proj, bd, cd, a_f, reverse):
    l = proj.shape[0]
    nch, cu, hs2 = bd.shape
    nt = hs2 // (2 * LANE)
    frows = 2 * nt * SUBLANE
    tb = _tile(l, 512, SUBLANE)
    nb = l // tb

    def body(u_ref, bd_ref, cd_ref, a_ref, xf_ref, y_ref, st):
        r = pl.program_id(1)

        @pl.when(r == 0)
        def _():
            st[...] = jnp.zeros_like(st)

        _fold_store(xf_ref, _dot(u_ref[...], bd_ref[...]), tb, 2 * nt)
        ar, ai = a_ref[0:nt, :], a_ref[nt:2 * nt, :]

        def group(gi, carry):
            rr = (tb // SUBLANE - 1 - gi) if reverse else gi
            sr, si = carry
            for qq in range(SUBLANE):
                q = (SUBLANE - 1 - qq) if reverse else qq
                re_rows, im_rows = pl.ds(q, nt, stride=SUBLANE), pl.ds(nt * SUBLANE + q, nt, stride=SUBLANE)
                nr = ar * sr - ai * si + xf_ref[rr, re_rows, :]
                ni = ar * si + ai * sr + xf_ref[rr, im_rows, :]
                xf_ref[rr, re_rows, :] = nr
                xf_ref[rr, im_rows, :] = ni
                sr, si = nr, ni
            return sr, si

        fin = lax.fori_loop(0, tb // SUBLANE, group, (st[0:nt, :], st[nt:2 * nt, :]))
        st[0:nt, :] = fin[0]
        st[nt:2 * nt, :] = fin[1]
        y_ref[...] = _dot(_unfold(xf_ref, tb, 2 * nt), cd_ref[...])

    def rows(r):
        return (nb - 1 - r) if reverse else r

    return pl.pallas_call(
        body, name=name, grid=(nch, nb),
        in_specs=[pl.BlockSpec((tb, cu), lambda c, r: (rows(r), c)), pl.BlockSpec((None, cu, hs2), lambda c, r: (c, 0, 0)),
                  pl.BlockSpec((None, hs2, cu), lambda c, r: (c, 0, 0)), pl.BlockSpec((None, 2 * nt, LANE), lambda c, r: (c, 0, 0))],
        out_specs=[pl.BlockSpec((tb // SUBLANE, frows, LANE), lambda c, r: (rows(r), c, 0)), pl.BlockSpec((tb, cu), lambda c, r: (rows(r), c))],
        out_shape=[_sds((l // SUBLANE, nch * frows, LANE), F32), _sds((l, nch * cu), F32)],
        scratch_shapes=[pltpu.VMEM((2 * nt, LANE), F32)],
        compiler_params=_params(("arbitrary", "arbitrary")),
    )(proj, bd, cd, a_f)


def _s5_bwd(name, proj, dy, xf, bd, cd, a_conj, reverse):
    l = proj.shape[0]
    nch, cu, hs2 = bd.shape
    nt = hs2 // (2 * LANE)
    frows = 2 * nt * SUBLANE
    tb = _tile(l, 512, SUBLANE)
    nb = l // tb

    def body(u_ref, dy_ref, xs_ref, bd_ref, cd_ref, a_ref, du_ref, dbd_ref, dcd_ref, da_ref, lam, st):
        r = pl.program_id(1)

        @pl.when(r == 0)
        def _():
            st[...] = jnp.zeros_like(st)
            dbd_ref[...] = jnp.zeros_like(dbd_ref)
            dcd_ref[...] = jnp.zeros_like(dcd_ref)
            da_ref[...] = jnp.zeros_like(da_ref)

        dyv = dy_ref[...]
        _fold_store(lam, _dot(dyv, cd_ref[...], NT), tb, 2 * nt)
        ar, ai = a_ref[0:nt, :], a_ref[nt:2 * nt, :]

        def group(gi, carry):
            rr = (tb // SUBLANE - 1 - gi) if reverse else gi
            sr, si, cr, ci = carry
            for qq in range(SUBLANE):
                q = (SUBLANE - 1 - qq) if reverse else qq
                re_rows, im_rows = pl.ds(q, nt, stride=SUBLANE), pl.ds(nt * SUBLANE + q, nt, stride=SUBLANE)
                xr, xi = xs_ref[rr, re_rows, :], xs_ref[rr, im_rows, :]
                cr = cr + sr * xr + si * xi
                ci = ci + si * xr - sr * xi
                nr = ar * sr - ai * si + lam[rr, re_rows, :]
                ni = ar * si + ai * sr + lam[rr, im_rows, :]
                lam[rr, re_rows, :] = nr
                lam[rr, im_rows, :] = ni
                sr, si = nr, ni
            return sr, si, cr, ci

        zero = jnp.zeros((nt, LANE), F32)
        fin = lax.fori_loop(0, tb // SUBLANE, group, (st[0:nt, :], st[nt:2 * nt, :], zero, zero))
        st[0:nt, :] = fin[0]
        st[nt:2 * nt, :] = fin[1]
        da_ref[0:nt, :] += fin[2]
        da_ref[nt:2 * nt, :] += fin[3]
        lam_u = _unfold(lam, tb, 2 * nt)
        du_ref[...] = _dot(lam_u, bd_ref[...], NT)
        dbd_ref[...] += _dot(u_ref[...], lam_u, TN)
        dcd_ref[...] += _dot(_unfold(xs_ref, tb, 2 * nt), dyv, TN)

    def rows(r):
        return (nb - 1 - r) if reverse else r

    chunk_rows = pl.BlockSpec((tb, cu), lambda c, r: (rows(r), c))
    bd_spec = pl.BlockSpec((None, cu, hs2), lambda c, r: (c, 0, 0))
    cd_spec = pl.BlockSpec((None, hs2, cu), lambda c, r: (c, 0, 0))
    a_spec = pl.BlockSpec((None, 2 * nt, LANE), lambda c, r: (c, 0, 0))
    return pl.pallas_call(
        body, name=name, grid=(nch, nb),
        in_specs=[chunk_rows, chunk_rows, pl.BlockSpec((tb // SUBLANE, frows, LANE), lambda c, r: (rows(r), c, 0)), bd_spec, cd_spec, a_spec],
        out_specs=[chunk_rows, bd_spec, cd_spec, a_spec],
        out_shape=[_sds((l, nch * cu), F32), _sds((nch, cu, hs2), F32), _sds((nch, hs2, cu), F32), _sds((nch, 2 * nt, LANE), F32)],
        scratch_shapes=[pltpu.VMEM((tb // SUBLANE, frows, LANE), F32), pltpu.VMEM((2 * nt, LANE), F32)],
        compiler_params=_params(("arbitrary", "arbitrary")),
    )(proj, dy, xf, bd, cd, a_conj)


def _s5_du(name, du_f, du_b, dy, proj, d_row):
    l, w = dy.shape
    tm = _tile(l, 1024, SUBLANE)

    def body(f_ref, b_ref, dy_ref, u_ref, d_ref, du_ref, dd_ref):
        i = pl.program_id(0)
        dyv = dy_ref[...]
        du_ref[...] = (f_ref[...] + b_ref[...] + dyv * d_ref[...]).astype(du_ref.dtype)
        part = jnp.sum(dyv * u_ref[...].astype(F32), axis=0, keepdims=True)

        @pl.when(i == 0)
        def _():
            dd_ref[...] = part

        @pl.when(i > 0)
        def _():
            dd_ref[...] += part

    row = pl.BlockSpec((tm, w), lambda i: (i, 0))
    vec = pl.BlockSpec((1, w), lambda i: (0, 0))
    return pl.pallas_call(
        body, name=name, grid=(l // tm,), in_specs=[row, row, row, row, vec], out_specs=[row, vec],
        out_shape=[_sds((l, w), BF), _sds((1, w), F32)],
        compiler_params=_params(("arbitrary",)),
    )(du_f, du_b, dy, proj, d_row)


def _gelu(y):
    c = math.sqrt(2.0 / math.pi)
    return 0.5 * y * (1.0 + jnp.tanh(c * (y + 0.044715 * y * y * y)))


def _gelu_grad(y):
    c = math.sqrt(2.0 / math.pi)
    th = jnp.tanh(c * (y + 0.044715 * y * y * y))
    return 0.5 * (1.0 + th) + 0.5 * y * (1.0 - th * th) * c * (1.0 + 3.0 * 0.044715 * y * y)


def _glu_fwd(name, y_f, y_b, proj, d_row, w):
    l, wd = y_f.shape
    tm = _tile(l, 512, SUBLANE)

    def body(yf_ref, yb_ref, u_ref, d_ref, w_ref, y_ref, o_ref):
        y = yf_ref[...] + yb_ref[...] + u_ref[...].astype(F32) * d_ref[...]
        y_ref[...] = y
        gy = _gelu(y)
        z = _dot(gy, w_ref[...])
        o_ref[...] = (gy * _sigmoid(z)).astype(o_ref.dtype)

    row = pl.BlockSpec((tm, wd), lambda i: (i, 0))
    return pl.pallas_call(
        body, name=name, grid=(l // tm,),
        in_specs=[row, row, row, pl.BlockSpec((1, wd), lambda i: (0, 0)), pl.BlockSpec((wd, wd), lambda i: (0, 0))],
        out_specs=[row, row], out_shape=[_sds((l, wd), F32), _sds((l, wd), BF)],
        compiler_params=_params(("arbitrary",)),
    )(y_f, y_b, proj, d_row, w)


def _glu_bwd(name, y, w, dout, dcol):
    l, wd = y.shape
    tm = _tile(l, 512, SUBLANE)

    def body(y_ref, w_ref, d_ref, dy_ref, dw_ref):
        i = pl.program_id(0)
        yv = y_ref[...]
        gy = _gelu(yv)
        s = _sigmoid(_dot(gy, w_ref[...]))
        d = d_ref[...].astype(F32)
        t = d * gy * s * (1.0 - s)
        dgy = d * s + _dot(t, w_ref[...], NT)
        dy_ref[...] = dgy * _gelu_grad(yv)
        part = _dot(gy, t, TN)

        @pl.when(i == 0)
        def _():
            dw_ref[...] = part

        @pl.when(i > 0)
        def _():
            dw_ref[...] += part

    return pl.pallas_call(
        body, name=name, grid=(l // tm,),
        in_specs=[pl.BlockSpec((tm, wd), lambda i: (i, 0)), pl.BlockSpec((wd, wd), lambda i: (0, 0)),
                  pl.BlockSpec((tm, wd), lambda i: (i, dcol))],
        out_specs=[pl.BlockSpec((tm, wd), lambda i: (i, 0)), pl.BlockSpec((wd, wd), lambda i: (0, 0))],
        out_shape=[_sds((l, wd), F32), _sds((wd, wd), F32)],
        compiler_params=_params(("arbitrary",)),
    )(y, w, dout)


def _log_sigmoid(x):
    return jnp.minimum(x, 0.0) - jnp.log(1.0 + jnp.exp(-jnp.abs(x)))


def _gate_fwd(name, glo, wf, wb, bf, bb):
    l, r2 = glo.shape
    hk = wf.shape[1]
    tm = _tile(l, 1024, SUBLANE)

    def body(x_ref, wf_ref, wb_ref, bf_ref, bb_ref, gf_ref, gb_ref):
        xv = x_ref[...]
        gf_ref[...] = _log_sigmoid(_dot(xv, wf_ref[...]) + bf_ref[...]) * (1.0 / GLA_GATE_NORM)
        gb_ref[...] = _log_sigmoid(_dot(xv, wb_ref[...]) + bb_ref[...]) * (1.0 / GLA_GATE_NORM)

    w_spec = pl.BlockSpec((r2, hk), lambda i: (0, 0))
    b_spec = pl.BlockSpec((1, hk), lambda i: (0, 0))
    o_spec = pl.BlockSpec((tm, hk), lambda i: (i, 0))
    return pl.pallas_call(
        body, name=name, grid=(l // tm,),
        in_specs=[pl.BlockSpec((tm, r2), lambda i: (i, 0)), w_spec, w_spec, b_spec, b_spec],
        out_specs=[o_spec, o_spec], out_shape=[_sds((l, hk), F32), _sds((l, hk), F32)],
        compiler_params=_params(("arbitrary",)),
    )(glo, wf, wb, bf, bb)


def _gate_bwd(name, glo, wf, wb, bf, bb, dgf, dgb):
    l, r2 = glo.shape
    hk = wf.shape[1]
    tm = _tile(l, 1024, SUBLANE)

    def body(x_ref, wf_ref, wb_ref, bf_ref, bb_ref, dgf_ref, dgb_ref, dx_ref, dwf_ref, dwb_ref, dbf_ref, dbb_ref):
        i = pl.program_id(0)
        xv = x_ref[...]
        kf = _dot(xv, wf_ref[...]) + bf_ref[...]
        kb = _dot(xv, wb_ref[...]) + bb_ref[...]
        dkf = dgf_ref[...] * (1.0 / GLA_GATE_NORM) * _sigmoid(-kf)
        dkb = dgb_ref[...] * (1.0 / GLA_GATE_NORM) * _sigmoid(-kb)
        dx_ref[...] = _dot(dkf, wf_ref[...], NT) + _dot(dkb, wb_ref[...], NT)
        parts = (_dot(xv, dkf, TN), _dot(xv, dkb, TN), jnp.sum(dkf, axis=0, keepdims=True), jnp.sum(dkb, axis=0, keepdims=True))
        accs = (dwf_ref, dwb_ref, dbf_ref, dbb_ref)

        @pl.when(i == 0)
        def _():
            for a_, p_ in zip(accs, parts):
                a_[...] = p_

        @pl.when(i > 0)
        def _():
            for a_, p_ in zip(accs, parts):
                a_[...] += p_

    w_spec = pl.BlockSpec((r2, hk), lambda i: (0, 0))
    b_spec = pl.BlockSpec((1, hk), lambda i: (0, 0))
    g_spec = pl.BlockSpec((tm, hk), lambda i: (i, 0))
    x_spec = pl.BlockSpec((tm, r2), lambda i: (i, 0))
    return pl.pallas_call(
        body, name=name, grid=(l // tm,),
        in_specs=[x_spec, w_spec, w_spec, b_spec, b_spec, g_spec, g_spec],
        out_specs=[x_spec, w_spec, w_spec, b_spec, b_spec],
        out_shape=[_sds((l, r2), F32), _sds((r2, hk), F32), _sds((r2, hk), F32), _sds((1, hk), F32), _sds((1, hk), F32)],
        compiler_params=_params(("arbitrary",)),
    )(glo, wf, wb, bf, bb, dgf, dgb)


def _chunk_terms(qc, kc, gc, lg, chunk, reverse):
    ri = lax.broadcasted_iota(jnp.int32, (chunk, chunk), 0)
    ci = lax.broadcasted_iota(jnp.int32, (chunk, chunk), 1)
    if reverse:
        tri = ci >= ri
        mask = ci > ri
    else:
        tri = ci <= ri
        mask = ci <= ri
    if gc is not None:
        cum = _dot3(tri.astype(F32), gc)
        last = cum[0:1, :] if reverse else cum[chunk - 1:chunk, :]
    else:
        pos = lax.broadcasted_iota(jnp.int32, (chunk, 1), 0).astype(F32)
        cum = ((chunk - pos) if reverse else (pos + 1.0)) * lg
        last = chunk * lg
    e = jnp.exp(cum)
    einv = jnp.exp(-cum)
    dec = jnp.exp(last - cum)
    return e, einv, dec, qc * e, kc * einv, kc * dec, jnp.exp(last), mask, tri


def _lin_specs(arr, width, col, tb, nb, reverse, per_head):
    if per_head:
        return pl.BlockSpec((tb, width), lambda h, r: ((nb - 1 - r) if reverse else r, col + h))
    return pl.BlockSpec((tb, width), lambda h, r: ((nb - 1 - r) if reverse else r, col))


def _lin_fwd(name, q, k, v, g, lgtab, *, heads, hb, dk, dv, chunk, tb, qcol, kcol, vcol, qscale, reverse):
    l = q.shape[0]
    nb = l // tb
    ncb = tb // chunk
    ng = heads // hb
    gated = g is not None
    per_head = ng > 1

    def body(*refs):
        if gated:
            q_ref, k_ref, v_ref, g_ref, o_ref, sp_ref, st = refs
        else:
            q_ref, k_ref, v_ref, lg_ref, o_ref, sp_ref, st = refs
        r = pl.program_id(1)

        @pl.when(r == 0)
        def _():
            st[...] = jnp.zeros_like(st)

        for c in range(ncb):
            cc = (ncb - 1 - c) if reverse else c
            rows = pl.ds(cc * chunk, chunk)
            for h in range(hb):
                lg = None if gated else lg_ref[h, :, 0:1]
                qc = q_ref[rows, h * dk:(h + 1) * dk].astype(F32) * qscale
                kc = k_ref[rows, h * dk:(h + 1) * dk].astype(F32)
                vc = v_ref[rows, h * dv:(h + 1) * dv]
                gc = g_ref[rows, h * dk:(h + 1) * dk] if gated else None
                _, _, _, qd, ki, kdec, e_last, mask, _ = _chunk_terms(qc, kc, gc, lg, chunk, reverse)
                a = jnp.where(mask, _dot(qd, ki, NT), 0.0)
                s_t = st[h]
                o_ref[rows, h * dv:(h + 1) * dv] = _dot(a, vc) + _dot(qd, s_t, NT)
                sp_ref[cc, h] = s_t
                st[h] = s_t * e_last + _dot(vc, kdec, TN)

    in_specs = [_lin_specs(q, hb * dk, qcol, tb, nb, reverse, per_head), _lin_specs(k, hb * dk, kcol, tb, nb, reverse, per_head),
                _lin_specs(v, hb * dv, vcol, tb, nb, reverse, per_head)]
    args = [q, k, v]
    if gated:
        in_specs.append(_lin_specs(g, hb * dk, 0, tb, nb, reverse, per_head))
        args.append(g)
    else:
        in_specs.append(pl.BlockSpec((hb, 1, LANE), lambda h, r: (h, 0, 0)))
        args.append(lgtab)
    out_specs = [_lin_specs(None, hb * dv, 0, tb, nb, reverse, per_head),
                 pl.BlockSpec((ncb, hb, dv, dk), lambda h, r: ((nb - 1 - r) if reverse else r, h, 0, 0))]
    outs = [_sds((l, heads * dv), F32), _sds((l // chunk, heads, dv, dk), F32)]
    return pl.pallas_call(
        body, name=name, grid=(ng, nb), in_specs=in_specs, out_specs=out_specs, out_shape=outs,
        scratch_shapes=[pltpu.VMEM((hb, dv, dk), F32)],
        compiler_params=_params(("arbitrary", "arbitrary")),
    )(*args)


def _lin_bwd(name, q, k, v, g, lgtab, sprev, do, prev, *, heads, hb, dk, dv, chunk, tb, qcol, kcol, vcol, qscale, reverse):
    l = q.shape[0]
    nb = l // tb
    ncb = tb // chunk
    ng = heads // hb
    gated = g is not None
    per_head = ng > 1
    brev = not reverse
    n_prev = 0 if prev is None else len(prev)

    def body(*refs):
        q_ref, k_ref, v_ref, x_ref, sp_ref, do_ref = refs[:6]
        p_refs = refs[6:6 + n_prev]
        o_refs = refs[6 + n_prev:-1]
        dst = refs[-1]
        dq_ref, dk_ref, dv_ref = o_refs[:3]
        r = pl.program_id(1)

        @pl.when(r == 0)
        def _():
            dst[...] = jnp.zeros_like(dst)

        for c in range(ncb):
            cc = (ncb - 1 - c) if brev else c
            rows = pl.ds(cc * chunk, chunk)
            for h in range(hb):
                lg = None if gated else x_ref[h, :, 0:1]
                ks = slice(h * dk, (h + 1) * dk)
                vs = slice(h * dv, (h + 1) * dv)
                qc = q_ref[rows, ks].astype(F32) * qscale
                kc = k_ref[rows, ks].astype(F32)
                vc = v_ref[rows, vs]
                gc = x_ref[rows, ks] if gated else None
                e, einv, dec, qd, ki, kdec, e_last, mask, tri = _chunk_terms(qc, kc, gc, lg, chunk, reverse)
                a = jnp.where(mask, _dot(qd, ki, NT), 0.0)
                s_t = sp_ref[cc, h]
                ds_t = dst[h]
                doc = do_ref[rows, vs]
                dvc = _dot(a, doc, TN) + _dot(kdec, ds_t, NT)
                da = jnp.where(mask, _dot(doc, vc, NT), 0.0)
                dqd = _dot(da, ki) + _dot(doc, s_t)
                dki = _dot(da, qd, TN)
                dkdec = _dot(vc, ds_t)
                dst[h] = ds_t * e_last + _dot(doc, qd, TN)
                dqc = dqd * e * qscale
                dkc = dki * einv + dkdec * dec
                if n_prev:
                    dqc = dqc + p_refs[0][rows, ks]
                    dkc = dkc + p_refs[1][rows, ks]
                    dvc = dvc + p_refs[2][rows, vs]
                dq_ref[rows, ks] = dqc
                dk_ref[rows, ks] = dkc
                dv_ref[rows, vs] = dvc
                if gated:
                    dcum = dqd * qd - dki * ki - dkdec * kdec
                    dlast = jnp.sum(dkdec * kdec, axis=0, keepdims=True) + e_last * jnp.sum(s_t * ds_t, axis=0, keepdims=True)
                    rid = lax.broadcasted_iota(jnp.int32, (chunk, 1), 0)
                    dcum = dcum + jnp.where(rid == (0 if reverse else chunk - 1), dlast, 0.0)
                    dgc = _dot3(tri.astype(F32), dcum, TN)
                    o_refs[3][rows, ks] = dgc

    def spec(width, col):
        return _lin_specs(None, width, col, tb, nb, brev, per_head)

    in_specs = [spec(hb * dk, qcol), spec(hb * dk, kcol), spec(hb * dv, vcol)]
    args = [q, k, v]
    if gated:
        in_specs.append(spec(hb * dk, 0))
        args.append(g)
    else:
        in_specs.append(pl.BlockSpec((hb, 1, LANE), lambda h, r: (h, 0, 0)))
        args.append(lgtab)
    in_specs.append(pl.BlockSpec((ncb, hb, dv, dk), lambda h, r: ((nb - 1 - r) if brev else r, h, 0, 0)))
    args.append(sprev)
    in_specs.append(spec(hb * dv, 0))
    args.append(do)
    out_specs = [spec(hb * dk, 0), spec(hb * dk, 0), spec(hb * dv, 0)]
    outs = [_sds((l, heads * dk), F32), _sds((l, heads * dk), F32), _sds((l, heads * dv), F32)]
    if gated:
        out_specs.append(spec(hb * dk, 0))
        outs.append(_sds((l, heads * dk), F32))
    if n_prev:
        in_specs += out_specs[:3]
        args += list(prev)
    return pl.pallas_call(
        body, name=name, grid=(ng, nb), in_specs=in_specs, out_specs=out_specs, out_shape=outs,
        scratch_shapes=[pltpu.VMEM((hb, dv, dk), F32)],
        compiler_params=_params(("arbitrary", "arbitrary")),
    )(*args)


def _headgate_fwd(name, o_f, o_b, og_arr, og_col, gn, dv):
    l, w = o_f.shape
    tm = _tile(l, 512, SUBLANE)
    nh = w // dv

    def body(of_ref, ob_ref, og_ref, gn_ref, out_ref):
        for h in range(nh):
            cs = slice(h * dv, (h + 1) * dv)
            o = of_ref[:, cs] + ob_ref[:, cs]
            r = lax.rsqrt(jnp.mean(o * o, axis=-1, keepdims=True) + EPS)
            og = og_ref[:, cs].astype(F32)
            out_ref[:, cs] = (o * r * gn_ref[:, cs] * (og * _sigmoid(og))).astype(out_ref.dtype)

    row = pl.BlockSpec((tm, w), lambda i: (i, 0))
    return pl.pallas_call(
        body, name=name, grid=(l // tm,),
        in_specs=[row, row, pl.BlockSpec((tm, w), lambda i: (i, og_col)), pl.BlockSpec((1, w), lambda i: (0, 0))],
        out_specs=row, out_shape=_sds((l, w), BF),
        compiler_params=_params(("arbitrary",)),
    )(o_f, o_b, og_arr, gn)


def _headgate_bwd(name, o_f, o_b, og_arr, og_col, gn, dout, dcol, dv):
    l, w = o_f.shape
    tm = _tile(l, 512, SUBLANE)
    nh = w // dv

    def body(of_ref, ob_ref, og_ref, gn_ref, d_ref, do_ref, dog_ref, dgn_ref):
        i = pl.program_id(0)
        for h in range(nh):
            cs = slice(h * dv, (h + 1) * dv)
            o = of_ref[:, cs] + ob_ref[:, cs]
            r = lax.rsqrt(jnp.mean(o * o, axis=-1, keepdims=True) + EPS)
            oh = o * r
            og = og_ref[:, cs].astype(F32)
            s = _sigmoid(og)
            d = d_ref[:, cs].astype(F32)
            gnv = gn_ref[:, cs]
            d_on = d * (og * s)
            dog_ref[:, cs] = (d * (oh * gnv) * s * (1.0 + og * (1.0 - s))).astype(dog_ref.dtype)
            doh = d_on * gnv
            do_ref[:, cs] = r * (doh - oh * jnp.mean(doh * oh, axis=-1, keepdims=True))
            part = jnp.sum(d_on * oh, axis=0, keepdims=True)

            @pl.when(i == 0)
            def _():
                dgn_ref[:, cs] = part

            @pl.when(i > 0)
            def _():
                dgn_ref[:, cs] += part

    row = pl.BlockSpec((tm, w), lambda i: (i, 0))
    vec = pl.BlockSpec((1, w), lambda i: (0, 0))
    return pl.pallas_call(
        body, name=name, grid=(l // tm,),
        in_specs=[row, row, pl.BlockSpec((tm, w), lambda i: (i, og_col)), vec, pl.BlockSpec((tm, w), lambda i: (i, dcol))],
        out_specs=[row, row, vec], out_shape=[_sds((l, w), F32), _sds((l, w), BF), _sds((1, w), F32)],
        compiler_params=_params(("arbitrary",)),
    )(o_f, o_b, og_arr, gn, dout)


def _rot_tables(l, dk):
    half = dk // 2
    pos = jnp.arange(l, dtype=F32)
    inv = jnp.exp(-math.log(ROPE_BASE) * jnp.arange(half, dtype=F32) / half)
    ang = pos[:, None] * inv[None, :]
    cos, sin = jnp.cos(ang), jnp.sin(ang)
    return jnp.concatenate([cos, cos], axis=-1), jnp.concatenate([-sin, sin], axis=-1)


def _rot_apply(name, src_q, qcol, src_k, kcol, cos_t, sin_t, heads, dk, kscale, out_dtype, transpose):
    l = src_q.shape[0]
    w = heads * dk
    tm = _tile(l, 512, SUBLANE)

    def rot(t, cos_v, sin_v):
        if transpose:
            return t * cos_v + pltpu.roll(t * sin_v, dk // 2, 1)
        return t * cos_v + pltpu.roll(t, dk // 2, 1) * sin_v

    def body(q_ref, k_ref, c_ref, s_ref, qo_ref, ko_ref):
        cos_v, sin_v = c_ref[...], s_ref[...]
        for h in range(heads):
            cs = slice(h * dk, (h + 1) * dk)
            qo_ref[:, cs] = rot(q_ref[:, cs].astype(F32), cos_v, sin_v).astype(out_dtype)
            ko_ref[:, cs] = (rot(k_ref[:, cs].astype(F32), cos_v, sin_v) * kscale).astype(out_dtype)

    tab = pl.BlockSpec((tm, dk), lambda i: (i, 0))
    row = pl.BlockSpec((tm, w), lambda i: (i, 0))
    return pl.pallas_call(
        body, name=name, grid=(l // tm,),
        in_specs=[pl.BlockSpec((tm, w), lambda i: (i, qcol)), pl.BlockSpec((tm, w), lambda i: (i, kcol)), tab, tab],
        out_specs=[row, row], out_shape=[_sds((l, w), out_dtype), _sds((l, w), out_dtype)],
        compiler_params=_params(("arbitrary",)),
    )(src_q, src_k, cos_t, sin_t)


def _exchange(name, src, gather):
    shape = src.shape if gather else src.shape[1:]

    def body(src_ref, out_ref, send_sems, recv_sems, local_sem):
        me = _my_index()
        own = pltpu.make_async_copy(src_ref if gather else src_ref.at[me], out_ref.at[me], local_sem)
        own.start()
        copies = _peer_copies(src_ref, out_ref, send_sems, recv_sems, gather)
        for cp in copies:
            cp.start()
        for cp in copies:
            cp.wait_recv()
        for cp in copies:
            cp.wait_send()
        own.wait()

    return pl.pallas_call(
        body, name=name,
        in_specs=[pl.BlockSpec(memory_space=pl.ANY)], out_specs=pl.BlockSpec(memory_space=pl.ANY),
        out_shape=_sds((N_DEV,) + tuple(shape), src.dtype),
        scratch_shapes=[pltpu.SemaphoreType.DMA((N_DEV - 1,)), pltpu.SemaphoreType.DMA((N_DEV - 1,)), pltpu.SemaphoreType.DMA],
        )(src)


def _my_index():
    return 4 * lax.axis_index("x") + 2 * lax.axis_index("y") + lax.axis_index("c")


def _peer_copies(src_ref, out_ref, send_sems, recv_sems, gather):
    x, y, c = lax.axis_index("x"), lax.axis_index("y"), lax.axis_index("c")
    me = 4 * x + 2 * y + c
    copies = []
    for kk in range(1, N_DEV):
        px = (1 - x) if kk & 4 else x
        py = (1 - y) if kk & 2 else y
        pc = (1 - c) if kk & 1 else c
        peer = 4 * px + 2 * py + pc
        copies.append(pltpu.make_async_remote_copy(
            src_ref=src_ref if gather else src_ref.at[peer], dst_ref=out_ref.at[me],
            send_sem=send_sems.at[kk - 1], recv_sem=recv_sems.at[kk - 1],
            device_id=(px, py, pc), device_id_type=pl.DeviceIdType.MESH))
    return copies


_HBM = pl.BlockSpec(memory_space=pltpu.HBM)
_SEM = pl.BlockSpec(memory_space=pltpu.SEMAPHORE)
_EFFECT = pltpu.SideEffectType.DATAFLOW_SIDE_EFFECTING


def _exchange_start(name, src, gather):
    shape = src.shape if gather else src.shape[1:]
    land = lax.empty((N_DEV,) + tuple(shape), src.dtype)

    def body(src_ref, land_ref, send_sems, recv_sems, src_thru, land_thru, token):
        for cp in _peer_copies(src_ref, land_ref, send_sems, recv_sems, gather):
            cp.start()
        token[...] = jnp.zeros_like(token)

    return pl.pallas_call(
        body, name=name,
        out_shape=(pltpu.SemaphoreType.DMA((N_DEV - 1,)), pltpu.SemaphoreType.DMA((N_DEV - 1,)),
                   pltpu.HBM(src.shape, src.dtype), pltpu.HBM(land.shape, land.dtype), _sds((SUBLANE, LANE), F32)),
        in_specs=(_HBM, _HBM), out_specs=(_SEM, _SEM, _HBM, _HBM, pl.BlockSpec(memory_space=pltpu.VMEM)),
        input_output_aliases={0: 2, 1: 3},
        compiler_params=pltpu.CompilerParams(has_side_effects=_EFFECT),
    )(pltpu.with_memory_space_constraint(src, pltpu.HBM), pltpu.with_memory_space_constraint(land, pltpu.HBM))


def _exchange_wait(name, started, gather, after):
    send_sems, recv_sems, src_thru, land_thru, _ = started

    def body(src_ref, land_ref, send_sems, recv_sems, after_ref, src_out, land_out):
        copies = _peer_copies(src_ref, land_ref, send_sems, recv_sems, gather)
        for cp in copies:
            cp.wait_send()
        for cp in copies:
            cp.wait_recv()

    return pl.pallas_call(
        body, name=name,
        out_shape=(pltpu.HBM(src_thru.shape, src_thru.dtype), pltpu.HBM(land_thru.shape, land_thru.dtype)),
        in_specs=(_HBM, _HBM, _SEM, _SEM, pl.BlockSpec(memory_space=pl.ANY)), out_specs=(_HBM, _HBM),
        input_output_aliases={0: 0, 1: 1},
        compiler_params=pltpu.CompilerParams(has_side_effects=_EFFECT),
    )(src_thru, land_thru, send_sems, recv_sems, after)


def _adam_math(w, gsum, m, v):
    m2 = ADAM_B1 * m + (1.0 - ADAM_B1) * gsum
    v2 = ADAM_B2 * v + (1.0 - ADAM_B2) * (gsum * gsum)
    m_hat = m2 / (1.0 - ADAM_B1 ** ADAM_STEP)
    v_hat = v2 / (1.0 - ADAM_B2 ** ADAM_STEP)
    delta = -ADAM_LR * (m_hat / (jnp.sqrt(v_hat) + ADAM_EPS) + ADAM_WD * w)
    return delta, m2, v2


def _reduce_adam(name, parts, w, m, v):
    r, c = w.shape
    tr = _tile(r, 256, 16)

    def body(p_ref, w_ref, m_ref, v_ref, g_ref, d_ref, m2_ref, v2_ref):
        gsum = p_ref[0].astype(F32)
        for s in range(1, N_DEV):
            gsum = gsum + p_ref[s].astype(F32)
        g_ref[...] = gsum
        delta, m2, v2 = _adam_math(w_ref[...], gsum, m_ref[...], v_ref[...])
        d_ref[...] = delta
        m2_ref[...] = m2
        v2_ref[...] = v2

    row = pl.BlockSpec((tr, c), lambda i: (i, 0))
    return pl.pallas_call(
        body, name=name, grid=(r // tr,),
        in_specs=[pl.BlockSpec((N_DEV, tr, c), lambda i: (0, i, 0)), row, row, row],
        out_specs=[row, row, row, row], out_shape=[_sds((r, c), F32)] * 4,
        compiler_params=_params(("arbitrary",)),
    )(parts, w, m, v)


def _reduce8(name, parts):
    _, r, c = parts.shape

    def body(p_ref, g_ref):
        gsum = p_ref[0]
        for s in range(1, N_DEV):
            gsum = gsum + p_ref[s]
        g_ref[...] = gsum

    return pl.pallas_call(
        body, name=name, grid=(1,),
        in_specs=[pl.BlockSpec((N_DEV, r, c), lambda i: (0, 0, 0))],
        out_specs=pl.BlockSpec((r, c), lambda i: (0, 0)), out_shape=_sds((r, c), F32),
        compiler_params=_params(("arbitrary",)),
    )(parts)


def _adam_packed(name, w, g, m, v):
    r, c = w.shape

    def body(w_ref, g_ref, m_ref, v_ref, d_ref, m2_ref, v2_ref):
        delta, m2, v2 = _adam_math(w_ref[...], g_ref[...], m_ref[...], v_ref[...])
        d_ref[...] = delta
        m2_ref[...] = m2
        v2_ref[...] = v2

    spec = pl.BlockSpec((r, c), lambda i: (0, 0))
    return pl.pallas_call(
        body, name=name, grid=(1,), in_specs=[spec] * 4, out_specs=[spec] * 3, out_shape=[_sds((r, c), F32)] * 3,
        compiler_params=_params(("arbitrary",)),
    )(w, g, m, v)


def _pack(arrs):
    flat = jnp.concatenate([a.reshape(-1).astype(F32) for a in arrs])
    n = flat.shape[0]
    pad = (-n) % (SUBLANE * LANE)
    return jnp.pad(flat, (0, pad)).reshape(-1, LANE)


def _unpack(packed, like):
    flat = packed.reshape(-1)
    out, off = [], 0
    for a in like:
        n = math.prod(a.shape)
        out.append(flat[off:off + n].reshape(a.shape))
        off += n
    return out


def _row_blocks(full):
    return full.reshape(N_DEV, full.shape[0] // N_DEV, full.shape[1])


def _col_blocks(full):
    r, c = full.shape
    return full.reshape(r, N_DEV, c // N_DEV).transpose(1, 0, 2)


def _cols_natural(blocks):
    n, r, c = blocks.shape
    return blocks.transpose(1, 0, 2).reshape(r, n * c)


def kernel(x, ffn1_norm, ffn1_w1, ffn1_w2, mix_norm, ffn2_norm, ffn2_w1, ffn2_w2, ab_w_in, s5_lambda_re, s5_lambda_im, s5_b_re, s5_b_im, s5_c_re, s5_c_im, s5_log_dt, s5_d, s5_w_glu, gla_w_gk, gla_b_gk, gla_norm, ab_w_out, ret_w_in, ret_norm, ret_w_out, final_norm, loss_target, m_ffn1_norm, m_ffn1_w1, m_ffn1_w2, m_mix_norm, m_ffn2_norm, m_ffn2_w1, m_ffn2_w2, m_ab_w_in, m_s5_lambda_re, m_s5_lambda_im, m_s5_b_re, m_s5_b_im, m_s5_c_re, m_s5_c_im, m_s5_log_dt, m_s5_d, m_s5_w_glu, m_gla_w_gk, m_gla_b_gk, m_gla_norm, m_ab_w_out, m_ret_w_in, m_ret_norm, m_ret_w_out, m_final_norm, v_ffn1_norm, v_ffn1_w1, v_ffn1_w2, v_mix_norm, v_ffn2_norm, v_ffn2_w1, v_ffn2_w2, v_ab_w_in, v_s5_lambda_re, v_s5_lambda_im, v_s5_b_re, v_s5_b_im, v_s5_c_re, v_s5_c_im, v_s5_log_dt, v_s5_d, v_s5_w_glu, v_gla_w_gk, v_gla_b_gk, v_gla_norm, v_ab_w_out, v_ret_w_in, v_ret_norm, v_ret_w_out, v_final_norm):
    names = ['ffn1_norm', 'ffn1_w1', 'ffn1_w2', 'mix_norm', 'ffn2_norm', 'ffn2_w1', 'ffn2_w2', 'ab_w_in', 's5_lambda_re', 's5_lambda_im', 's5_b_re', 's5_b_im', 's5_c_re', 's5_c_im', 's5_log_dt', 's5_d', 's5_w_glu', 'gla_w_gk', 'gla_b_gk', 'gla_norm', 'ab_w_out', 'ret_w_in', 'ret_norm', 'ret_w_out', 'final_norm']
    loc = locals()
    W = {n: loc[n] for n in names}
    M = {n: loc["m_" + n] for n in names}
    V = {n: loc["v_" + n] for n in names}

    me = 4 * lax.axis_index("x") + 2 * lax.axis_index("y") + lax.axis_index("c")
    xs = x[0]
    tgt = loss_target[0]
    l, d = xs.shape
    depth = ffn1_norm.shape[0]

    pending, tokens = {}, []

    def start_gather(tag, shard):
        started = _exchange_start("ags_" + tag, shard, True)
        pending[tag] = (started, shard)
        tokens.append(started[4][0, 0])

    def finish_gather(tag, after):
        started, shard = pending.pop(tag)
        _, got = _exchange_wait("agw_" + tag, started, True, after)
        return lax.dynamic_update_index_in_dim(got, shard, me, 0)

    def finish_cols(tag, after):
        g = finish_gather(tag, after)
        return g.transpose(1, 0, 2).reshape(g.shape[1], -1)

    def finish_rows(tag, after):
        g = finish_gather(tag, after)
        return g.reshape(-1, g.shape[2])

    small_sharded = [gla_w_gk, gla_b_gk, ret_norm]
    for i in range(depth):
        j = i // 2
        start_gather(f"ffn1_w1_{i}", ffn1_w1[i].astype(BF))
        start_gather(f"ffn1_w2_{i}", ffn1_w2[i].astype(BF))
        if i % 2 == 0:
            start_gather(f"ab_w_in_{j}", ab_w_in[j].astype(BF))
            if i == 0:
                start_gather("small", _pack(small_sharded))
            start_gather(f"s5_w_glu_{j}", s5_w_glu[j].astype(BF))
            start_gather(f"ab_w_out_{j}", ab_w_out[j].astype(BF))
        else:
            start_gather(f"ret_w_in_{j}", ret_w_in[j].astype(BF))
            start_gather(f"ret_w_out_{j}", ret_w_out[j].astype(BF))
        start_gather(f"ffn2_w1_{i}", ffn2_w1[i].astype(BF))
        start_gather(f"ffn2_w2_{i}", ffn2_w2[i].astype(BF))
    started_all = functools.reduce(lambda a, b: a + b, tokens)
    full = {}

    s5w = s5_d.shape[1]
    g_s5, n_s5 = s5_lambda_re.shape[2], s5_lambda_re.shape[3]
    hs = min(SUBLANE * LANE, g_s5 * n_s5)
    gla_hk = gla_w_gk.shape[-1] * N_DEV
    gla_dk = gla_hk // GLA_HEADS
    gla_hv = gla_norm.shape[1]
    gla_dv = gla_hv // GLA_HEADS
    ret_hv = ret_norm.shape[1] * N_DEV
    ret_dv = ret_hv // RET_HEADS
    ret_hk = (ret_w_in.shape[2] * N_DEV - 2 * ret_hv) // 2
    ret_dk = ret_hk // RET_HEADS
    assert s5w == gla_hv and 2 * gla_hk == s5w, "column blocks of the mixer projection assume these widths"
    assert ret_hv == 2 * ret_hk
    main_w = s5w + 2 * gla_hk + 2 * gla_hv
    gla_tb = _tile(l, 256, GLA_CHUNK)
    ret_chunk = min(RET_CHUNK, l)

    cos_t, sin_t = _rot_tables(l, ret_dk)
    lg_f = jnp.log1p(-jnp.exp2(-5.0 - jnp.arange(RET_HEADS, dtype=F32)))
    lgtab_f = jnp.broadcast_to(lg_f[:, None, None], (RET_HEADS, 1, LANE))
    lgtab_b = jnp.broadcast_to(lg_f[::-1][:, None, None], (RET_HEADS, 1, LANE))

    saved = []
    cur = xs
    for i in range(depth):
        j = i // 2
        s = {}
        s['x0'] = cur
        s['f1w1'], s['f1w2'] = finish_cols(f"ffn1_w1_{i}", cur), finish_rows(f"ffn1_w2_{i}", cur)
        g1 = ffn1_norm[i:i + 1] + started_all if i == 0 else ffn1_norm[i:i + 1]
        cur, s['ffn1'] = _ffn_fwd(f"l{i}_ffn1", cur, g1, s['f1w1'], s['f1w2'])
        s['x1'] = cur
        h = _rms_fwd(f"l{i}_mixnorm", cur, mix_norm[i:i + 1])
        s['h'] = h
        if i % 2 == 0:
            w_in = finish_cols(f"ab_w_in_{j}", cur)
            if i == 0:
                got = finish_gather("small", cur)
                per_dev = [_unpack(got[p], small_sharded) for p in range(N_DEV)]
                full['gla_w_gk'] = jnp.concatenate([t[0] for t in per_dev], axis=-1).astype(BF)
                full['gla_b_gk'] = jnp.concatenate([t[1] for t in per_dev], axis=-1)
                ret_norm_full = jnp.concatenate([t[2] for t in per_dev], axis=-1)
            s['w_glu'], s['w_out'] = finish_rows(f"s5_w_glu_{j}", cur), finish_rows(f"ab_w_out_{j}", cur)
            w_main, w_glo = w_in[:, :main_w], w_in[:, main_w:]
            proj = _mm_plain(f"l{i}_proj", h, w_main, NN, BF, tm=512, tn=1024, tk=d, b_outer=True)
            glo = _mm_plain(f"l{i}_glo", h, w_glo, NN, F32, tm=1024, tn=2 * GLA_RANK, tk=d)
            s5_args = (s5_lambda_re[j], s5_lambda_im[j], s5_b_re[j], s5_b_im[j], s5_c_re[j], s5_c_im[j], s5_log_dt[j])
            (a_tab, bd, cd), s5_vjp = jax.vjp(lambda *a: _s5_chunk_tables(*a, hs), *s5_args)
            bd16, cd16 = bd.astype(BF), cd.astype(BF)
            tm = _tile(l, 512, SUBLANE)
            x_f, y_f = _s5_fwd(f"l{i}_s5_fwd_f", proj, bd16[0], cd16[0], a_tab[0], False)
            x_b, y_b = _s5_fwd(f"l{i}_s5_fwd_b", proj, bd16[1], cd16[1], a_tab[1], True)
            d_row = s5_d[j:j + 1]
            y, s5_out = _glu_fwd(f"l{i}_s5_glu", y_f, y_b, proj, d_row, s['w_glu'])
            zeros_r = jnp.zeros((GLA_RANK, gla_hk), BF)
            w_gk = full['gla_w_gk'][j]
            wgk_f = jnp.concatenate([w_gk[0], zeros_r], axis=0)
            wgk_b = jnp.concatenate([zeros_r, w_gk[1]], axis=0)
            b_gk = full['gla_b_gk'][j]
            g_f, g_b = _gate_fwd(f"l{i}_gla_gate", glo, wgk_f, wgk_b, b_gk[0:1], b_gk[1:2])
            qcol, kcol, vcol, ogcol = s5w // gla_hk, s5w // gla_hk + 1, (s5w + 2 * gla_hk) // gla_hv, (s5w + 2 * gla_hk) // gla_hv + 1
            lin_kw = dict(heads=GLA_HEADS, hb=GLA_HEADS, dk=gla_dk, dv=gla_dv, chunk=GLA_CHUNK, tb=gla_tb, qcol=qcol, kcol=kcol, vcol=vcol,
                          qscale=gla_dk ** -0.5)
            o_f, sp_f = _lin_fwd(f"l{i}_gla_fwd_f", proj, proj, proj, g_f, None, reverse=False, **lin_kw)
            o_b, sp_b = _lin_fwd(f"l{i}_gla_fwd_b", proj, proj, proj, g_b, None, reverse=True, **lin_kw)
            gla_out = _headgate_fwd(f"l{i}_gla_out", o_f, o_b, proj, ogcol, gla_norm[j:j + 1], gla_dv)
            w_out = s['w_out']

            def epi_res(acc, e_refs, o_refs, ids):
                o_refs[0][...] = e_refs[0][...] + acc

            row = pl.BlockSpec((tm, d), lambda ii, jj, kk: (ii, 0))
            cur = _mm(f"l{i}_mix_out",
                      [(s5_out, pl.BlockSpec((tm, s5w), lambda ii, jj, kk: (ii, 0)), w_out, pl.BlockSpec((s5w, d), lambda ii, jj, kk: (0, 0))),
                       (gla_out, pl.BlockSpec((tm, gla_hv), lambda ii, jj, kk: (ii, 0)), w_out, pl.BlockSpec((gla_hv, d), lambda ii, jj, kk: (1, 0)))],
                      NN, (l // tm, 1, 1), [_sds((l, d), F32)], [row], (tm, d), epi_res, [cur], [row])[0]
            s.update(proj=proj, glo=glo, s5_vjp=s5_vjp, a_tab=a_tab, bd16=bd16, cd16=cd16, x_f=x_f, x_b=x_b, y=y, s5_out=s5_out,
                     wgk_f=wgk_f, wgk_b=wgk_b, b_gk=b_gk, g_f=g_f, g_b=g_b, o_f=o_f, o_b=o_b, sp_f=sp_f, sp_b=sp_b, gla_out=gla_out,
                     w_main=w_main, w_glo=w_glo, lin_kw=lin_kw, ogcol=ogcol)
        else:
            w_in = finish_cols(f"ret_w_in_{j}", cur)
            s['w_in'], s['w_out'] = w_in, finish_rows(f"ret_w_out_{j}", cur)
            proj = _mm_plain(f"l{i}_proj", h, w_in, NN, BF, tm=512, tn=1024, tk=d, b_outer=True)
            qr, kr = _rot_apply(f"l{i}_rot", proj, 0, proj, 1, cos_t, sin_t, RET_HEADS, ret_dk, ret_dk ** -0.5, BF, False)
            ret_hb = 4
            lin_kw = dict(heads=RET_HEADS, hb=ret_hb, dk=ret_dk, dv=ret_dv, chunk=ret_chunk, tb=ret_chunk, qcol=0, kcol=0,
                          vcol=(2 * ret_hk) // (ret_hb * ret_dv),
                          qscale=1.0)
            o_f, sp_f = _lin_fwd(f"l{i}_ret_fwd_f", qr, kr, proj, None, lgtab_f, reverse=False, **lin_kw)
            o_b, sp_b = _lin_fwd(f"l{i}_ret_fwd_b", qr, kr, proj, None, lgtab_b, reverse=True, **lin_kw)
            ogcol = (2 * ret_hk + ret_hv) // ret_hv
            r_out = _headgate_fwd(f"l{i}_ret_out", o_f, o_b, proj, ogcol, ret_norm_full, ret_dv)

            def epi_res(acc, e_refs, o_refs, ids):
                o_refs[0][...] = e_refs[0][...] + acc

            cur = _mm_plain(f"l{i}_mix_out", r_out, s['w_out'], NN, F32, tm=512, tn=d, tk=ret_hv, epi=epi_res, eins=[cur])
            s.update(proj=proj, qr=qr, kr=kr, o_f=o_f, o_b=o_b, sp_f=sp_f, sp_b=sp_b, r_out=r_out, lin_kw=lin_kw, ogcol=ogcol)
        s['x2'] = cur
        s['f2w1'], s['f2w2'] = finish_cols(f"ffn2_w1_{i}", cur), finish_rows(f"ffn2_w2_{i}", cur)
        cur, s['ffn2'] = _ffn_fwd(f"l{i}_ffn2", cur, ffn2_norm[i:i + 1], s['f2w1'], s['f2w2'])
        saved.append(s)

    dx, d_final_norm, loss_row = _loss_head("loss_head", cur, final_norm.reshape(1, -1), tgt)
    loss = lax.psum(loss_row[0, 0], ("x", "y", "c"))

    G = {}
    big = {}
    G['final_norm'] = d_final_norm.reshape(-1)
    per_layer = {n: [None] * depth for n in ['ffn1_norm', 'mix_norm', 'ffn2_norm']}
    a2a, tok = {}, [jnp.zeros((), F32)]

    def start_a2a(tag, blocks):
        started = _exchange_start("a2as_" + tag, blocks, False)
        a2a[tag] = started
        tok[0] = tok[0] + started[4][0, 0]

    def dep(vec):
        return vec + tok[0]

    for i in reversed(range(depth)):
        j = i // 2
        s = saved[i]
        def ffn_grads(which):
            def on_grads(dw1, dw2, gnorm):
                start_a2a(f"{which}_w1_{i}", _col_blocks(dw1))
                start_a2a(f"{which}_w2_{i}", _row_blocks(dw2))
                return dep(gnorm)
            return on_grads

        dx, dg = _ffn_bwd(f"l{i}_ffn2b", dx, s['x2'], ffn2_norm[i:i + 1], s['f2w1'], s['f2w2'], s['ffn2'], ffn_grads("ffn2"))
        per_layer['ffn2_norm'][i] = dg[0]
        tm = _tile(l, 512, SUBLANE)
        row = pl.BlockSpec((tm, d), lambda ii, jj, kk: (ii, 0))
        vec = pl.BlockSpec((1, d), lambda ii, jj, kk: (0, 0))
        if i % 2 == 0:
            proj, lin_kw = s['proj'], s['lin_kw']
            w_out = s['w_out']
            d_cat = _mm_plain(f"l{i}_dcat", dx, w_out, NT, BF, tm=512, tn=1024, tk=d)
            dwo_a = _mm_plain(f"l{i}_dwout_a", s['s5_out'], dx, TN, BF, tm=s5w, tn=d, tk=512)
            dwo_b = _mm_plain(f"l{i}_dwout_b", s['gla_out'], dx, TN, BF, tm=gla_hv, tn=d, tk=512)
            start_a2a(f"ab_w_out_{j}", _row_blocks(jnp.concatenate([dwo_a, dwo_b], axis=0)))
            do, dog, dgn = _headgate_bwd(f"l{i}_gla_outb", s['o_f'], s['o_b'], proj, s['ogcol'], dep(gla_norm[j:j + 1]), d_cat, 1, gla_dv)
            G['gla_norm'] = dgn
            r1 = _lin_bwd(f"l{i}_gla_bwd_f", proj, proj, proj, s['g_f'], None, s['sp_f'], do, None, reverse=False, **lin_kw)
            dq, dk_, dv_, dgf = r1
            r2 = _lin_bwd(f"l{i}_gla_bwd_b", proj, proj, proj, s['g_b'], None, s['sp_b'], do, (dq, dk_, dv_), reverse=True, **lin_kw)
            dq, dk_, dv_, dgb = r2
            dglo, dwf, dwb, dbf, dbb = _gate_bwd(f"l{i}_gla_gateb", s['glo'], s['wgk_f'], s['wgk_b'], s['b_gk'][0:1], s['b_gk'][1:2], dgf, dgb)
            G['gla_w_gk'] = jnp.stack([dwf[:GLA_RANK], dwb[GLA_RANK:]], axis=0)[None]
            G['gla_b_gk'] = jnp.concatenate([dbf, dbb], axis=0)[None]
            dy, dwglu = _glu_bwd(f"l{i}_s5_glub", s['y'], s['w_glu'], d_cat, 0)
            start_a2a(f"s5_w_glu_{j}", _row_blocks(dwglu.astype(BF)))
            cd16, bd16, a_tab = s['cd16'], s['bd16'], s['a_tab']
            nt2 = a_tab.shape[2]
            a_conj = a_tab * jnp.where(jnp.arange(nt2) < nt2 // 2, 1.0, -1.0)[None, None, :, None]
            du_f, dbd_f, dcd_f, da_f = _s5_bwd(f"l{i}_s5_bwd_f", proj, dy, s['x_f'], bd16[0], cd16[0], a_conj[0], True)
            du_b, dbd_b, dcd_b, da_b = _s5_bwd(f"l{i}_s5_bwd_b", proj, dy, s['x_b'], bd16[1], cd16[1], a_conj[1], False)
            du, dd = _s5_du(f"l{i}_s5_du", du_f, du_b, dy, proj, s5_d[j:j + 1])
            G['s5_d'] = dd
            cot = (jnp.stack([da_f, da_b]), jnp.stack([dbd_f, dbd_b]), jnp.stack([dcd_f, dcd_b]))
            g_lre, g_lim, g_bre, g_bim, g_cre, g_cim, g_ldt = s['s5_vjp'](cot)
            G['s5_lambda_re'], G['s5_lambda_im'], G['s5_b_re'], G['s5_b_im'] = g_lre[None], g_lim[None], g_bre[None], g_bim[None]
            G['s5_c_re'], G['s5_c_im'], G['s5_log_dt'] = g_cre[None], g_cim[None], g_ldt[None]
            dproj = jnp.concatenate([du, dq.astype(BF), dk_.astype(BF), dv_.astype(BF), dog], axis=1)
            r2w = 2 * GLA_RANK
            pairs = [(dproj, pl.BlockSpec((tm, main_w), lambda ii, jj, kk: (ii, 0)), s['w_main'], pl.BlockSpec((d, main_w), lambda ii, jj, kk: (0, 0))),
                     (dglo, pl.BlockSpec((tm, r2w), lambda ii, jj, kk: (ii, 0)), s['w_glo'], pl.BlockSpec((d, r2w), lambda ii, jj, kk: (0, 0)))]
            dx, dg = _mm(f"l{i}_dh", pairs, NT, (l // tm, 1, 1), [_sds((l, d), F32), _sds((1, d), F32)], [row, vec], (tm, d),
                         _rms_bwd_epi(0), [s['x1'], dep(mix_norm[i:i + 1]), dx], [row, vec, row])
            dw_main = _mm_plain(f"l{i}_dwin_main", s['h'], dproj, TN, BF, tm=d, tn=1024, tk=512)
            dw_glo = _mm_plain(f"l{i}_dwin_glo", s['h'], dglo, TN, BF, tm=d, tn=r2w, tk=512)
            start_a2a(f"ab_w_in_{j}", _col_blocks(jnp.concatenate([dw_main, dw_glo], axis=1)))
        else:
            proj, lin_kw = s['proj'], s['lin_kw']
            w_out = s['w_out']
            d_ro = _mm_plain(f"l{i}_dro", dx, w_out, NT, BF, tm=512, tn=1024, tk=d, b_outer=True)
            dwo = _mm_plain(f"l{i}_dwout", s['r_out'], dx, TN, BF, tm=1024, tn=d, tk=512)
            start_a2a(f"ret_w_out_{j}", _row_blocks(dwo))
            do, dog, dgn = _headgate_bwd(f"l{i}_ret_outb", s['o_f'], s['o_b'], proj, s['ogcol'], dep(ret_norm_full), d_ro, 0, ret_dv)
            G['ret_norm'] = dgn
            r1 = _lin_bwd(f"l{i}_ret_bwd_f", s['qr'], s['kr'], proj, None, lgtab_f, s['sp_f'], do, None, reverse=False, **lin_kw)
            r2 = _lin_bwd(f"l{i}_ret_bwd_b", s['qr'], s['kr'], proj, None, lgtab_b, s['sp_b'], do, r1, reverse=True, **lin_kw)
            dqr, dkr, dv_ = r2
            dq, dk_ = _rot_apply(f"l{i}_rotb", dqr, 0, dkr, 0, cos_t, sin_t, RET_HEADS, ret_dk, ret_dk ** -0.5, BF, True)
            dproj = jnp.concatenate([dq, dk_, dv_.astype(BF), dog], axis=1)
            dw_in = _mm_plain(f"l{i}_dwin", s['h'], dproj, TN, BF, tm=d, tn=2048, tk=512)
            start_a2a(f"ret_w_in_{j}", _col_blocks(dw_in))
            ret_in = dproj.shape[1]
            dx, dg = _mm(f"l{i}_dh",
                         [(dproj, pl.BlockSpec((tm, ret_in), lambda ii, jj, kk: (ii, 0)),
                           s['w_in'], pl.BlockSpec((d, ret_in), lambda ii, jj, kk: (0, 0), pipeline_mode=pl.Buffered(1)))],
                         NT, (l // tm, 1, 1), [_sds((l, d), F32), _sds((1, d), F32)], [row, vec], (tm, d),
                         _rms_bwd_epi(0), [s['x1'], dep(mix_norm[i:i + 1]), dx], [row, vec, row])
        per_layer['mix_norm'][i] = dg[0]
        dx, dg = _ffn_bwd(f"l{i}_ffn1b", dx, s['x0'], ffn1_norm[i:i + 1], s['f1w1'], s['f1w2'], s['ffn1'], ffn_grads("ffn1"))
        per_layer['ffn1_norm'][i] = dg[0]
    for n in per_layer:
        G[n] = jnp.stack(per_layer[n], axis=0)
    grad_x = dx[None]

    out_g, out_d, out_m, out_v = {}, {}, {}, {}

    def big_update(tag, w, m, v):
        blocks, got = _exchange_wait("a2aw_" + tag, a2a.pop(tag), False, dx)
        parts = lax.dynamic_update_index_in_dim(got, lax.dynamic_index_in_dim(blocks, me, 0, keepdims=False), me, 0)
        return _reduce_adam("upd_" + tag, parts, w, m, v)

    for n in ['ffn2_w1', 'ffn2_w2', 'ffn1_w1', 'ffn1_w2']:
        res = [big_update(f"{n}_{i}", W[n][i], M[n][i], V[n][i]) for i in range(depth)]
        out_g[n], out_d[n], out_m[n], out_v[n] = [jnp.stack([r[t] for r in res], axis=0) for t in range(4)]
    for n in ['ab_w_in', 's5_w_glu', 'ab_w_out', 'ret_w_in', 'ret_w_out']:
        res = big_update(f"{n}_0", W[n][0], M[n][0], V[n][0])
        out_g[n], out_d[n], out_m[n], out_v[n] = [r[None] for r in res]
    assert not a2a and not pending

    small = ['ffn1_norm', 'mix_norm', 'ffn2_norm', 's5_lambda_re', 's5_lambda_im', 's5_b_re', 's5_b_im', 's5_c_re', 's5_c_im',
             's5_log_dt', 's5_d', 'gla_w_gk', 'gla_b_gk', 'gla_norm', 'ret_norm', 'final_norm']
    packed = _pack([G[n] for n in small])
    gathered = _exchange("ag_small_grads", packed, True)
    summed = _reduce8("sum_small_grads", gathered)
    g_full = dict(zip(small, _unpack(summed, [G[n] for n in small])))
    g_small = {}
    for n in small:
        gf = g_full[n]
        if n in ('gla_w_gk', 'gla_b_gk', 'ret_norm'):
            width = W[n].shape[-1]
            gf = lax.dynamic_slice_in_dim(gf, me * width, width, axis=gf.ndim - 1)
        g_small[n] = gf.reshape(W[n].shape)
    pw, pg, pm, pv = (_pack([src[n] for n in small]) for src in (W, g_small, M, V))
    pd, pm2, pv2 = _adam_packed("upd_small", pw, pg, pm, pv)
    like = [W[n] for n in small]
    for n, dd_, mm_, vv_ in zip(small, _unpack(pd, like), _unpack(pm2, like), _unpack(pv2, like)):
        out_g[n], out_d[n], out_m[n], out_v[n] = g_small[n], dd_, mm_, vv_

    return (loss, grad_x, *[out_g[n] for n in names], *[out_d[n] for n in names], *[out_m[n] for n in names], *[out_v[n] for n in names])
```

```python
import functools
import math

import jax
import jax.numpy as jnp
from jax import lax
from jax.experimental import pallas as pl
from jax.experimental.pallas import tpu as pltpu

F32 = jnp.float32
BF = jnp.bfloat16
N_DEV = 8
EPS = 1e-6
S5_GROUP = 16
GLA_HEADS = 4
GLA_RANK = 16
GLA_GATE_NORM = 16.0
RET_HEADS = 8
ROPE_BASE = 10000.0
GLA_CHUNK = 64
RET_CHUNK = 256
ADAM_LR, ADAM_B1, ADAM_B2, ADAM_EPS, ADAM_WD, ADAM_STEP = 0.001, 0.9, 0.999, 1e-08, 0.01, 10
VMEM_LIMIT_BYTES = 56 * 1024 * 1024
LANE = 128
SUBLANE = 8

NN = (((1,), (0,)), ((), ()))
NT = (((1,), (1,)), ((), ()))
TN = (((0,), (0,)), ((), ()))


def _tile(n, pref, align):
    if n <= pref:
        return n
    t = (pref // align) * align
    while t >= align:
        if n % t == 0:
            return t
        t -= align
    return n


def _params(sem):
    return pltpu.CompilerParams(dimension_semantics=sem, vmem_limit_bytes=VMEM_LIMIT_BYTES)


def _dot(a, b, dims=NN):
    return lax.dot_general(a.astype(BF), b.astype(BF), dims, preferred_element_type=F32)


def _dot3(m01, g, dims=NN):
    g1 = g.astype(BF)
    r1 = g - g1.astype(F32)
    g2 = r1.astype(BF)
    g3 = (r1 - g2.astype(F32)).astype(BF)
    m = m01.astype(BF)
    return (lax.dot_general(m, g1, dims, preferred_element_type=F32)
            + lax.dot_general(m, g2, dims, preferred_element_type=F32)
            + lax.dot_general(m, g3, dims, preferred_element_type=F32))


def _sigmoid(x):
    return 1.0 / (1.0 + jnp.exp(-x))


def _mm(name, pairs, dims, grid, outs, out_specs, acc_shape, epi=None, eins=(), especs=()):
    n_p, n_e, n_o = len(pairs), len(eins), len(outs)
    nk = grid[2]

    def body(*refs):
        a_refs = refs[0:2 * n_p:2]
        b_refs = refs[1:2 * n_p:2]
        e_refs = refs[2 * n_p:2 * n_p + n_e]
        o_refs = refs[2 * n_p + n_e:2 * n_p + n_e + n_o]
        acc = refs[-1]
        ids = (pl.program_id(0), pl.program_id(1), pl.program_id(2))

        part = _dot(a_refs[0][...], b_refs[0][...], dims)
        for p in range(1, n_p):
            part = part + _dot(a_refs[p][...], b_refs[p][...], dims)

        def finish(total):
            if epi is None:
                o_refs[0][...] = total.astype(o_refs[0].dtype)
            else:
                epi(total, e_refs, o_refs, ids)

        if nk == 1:
            finish(part)
        else:
            @pl.when(ids[2] == 0)
            def _():
                acc[...] = part

            @pl.when(ids[2] > 0)
            def _():
                acc[...] += part

            @pl.when(ids[2] == nk - 1)
            def _():
                finish(acc[...])

    in_specs, args = [], []
    for a, a_spec, b, b_spec in pairs:
        in_specs += [a_spec, b_spec]
        args += [a, b]
    in_specs += list(especs)
    args += list(eins)
    res = pl.pallas_call(
        body, name=name, grid=grid, in_specs=in_specs, out_specs=list(out_specs), out_shape=list(outs),
        scratch_shapes=[pltpu.VMEM(acc_shape, F32)],
        compiler_params=_params(("arbitrary", "arbitrary", "arbitrary")),
    )(*args)
    return res


def _sds(shape, dtype):
    return jax.ShapeDtypeStruct(shape, dtype)


def _mm_plain(name, a, b, dims, out_dtype, tm=512, tn=1024, tk=1024, epi=None, eins=(), especs=None, extra_outs=(), extra_specs=(),
              b_outer=False):
    if dims == NN:
        (m, k), n = a.shape, b.shape[1]
    elif dims == NT:
        (m, k), n = a.shape, b.shape[0]
    else:
        (k, m), n = a.shape, b.shape[1]
    tm, tn = _tile(m, tm, LANE if dims == TN else SUBLANE), _tile(n, tn, LANE)
    tk = _tile(k, tk, SUBLANE if dims == TN else LANE)
    grid = (n // tn, m // tm, k // tk) if b_outer else (m // tm, n // tn, k // tk)

    def spec(block, index):
        if b_outer:
            return pl.BlockSpec(block, lambda j, i, kk: index(i, j, kk))
        return pl.BlockSpec(block, index)

    if dims == NN:
        a_spec = spec((tm, tk), lambda i, j, kk: (i, kk))
        b_spec = spec((tk, tn), lambda i, j, kk: (kk, j))
    elif dims == NT:
        a_spec = spec((tm, tk), lambda i, j, kk: (i, kk))
        b_spec = spec((tn, tk), lambda i, j, kk: (j, kk))
    else:
        a_spec = spec((tk, tm), lambda i, j, kk: (kk, i))
        b_spec = spec((tk, tn), lambda i, j, kk: (kk, j))
    o_spec = spec((tm, tn), lambda i, j, kk: (i, j))
    if especs is None:
        especs = [o_spec] * len(eins)
    else:
        especs = [o_spec if s is None else s for s in especs]
    res = _mm(name, [(a, a_spec, b, b_spec)], dims, grid, [_sds((m, n), out_dtype)] + list(extra_outs),
              [o_spec] + list(extra_specs), (tm, tn), epi, eins, especs)
    return res if extra_outs else res[0]


def _rms_fwd(name, x, g):
    l, d = x.shape
    tm = _tile(l, 1024, SUBLANE)

    def body(x_ref, g_ref, o_ref):
        xv = x_ref[...]
        r = lax.rsqrt(jnp.mean(xv * xv, axis=-1, keepdims=True) + EPS)
        o_ref[...] = (xv * r * g_ref[...]).astype(o_ref.dtype)

    return pl.pallas_call(
        body, name=name, grid=(l // tm,),
        in_specs=[pl.BlockSpec((tm, d), lambda i: (i, 0)), pl.BlockSpec((1, d), lambda i: (0, 0))],
        out_specs=pl.BlockSpec((tm, d), lambda i: (i, 0)), out_shape=_sds((l, d), BF),
        compiler_params=_params(("arbitrary",)),
    )(x, g)


def _rms_bwd_epi(first_axis):
    def epi(acc, e_refs, o_refs, ids):
        x_ref, g_ref, dr_ref = e_refs
        dx_ref, dg_ref = o_refs
        xv = x_ref[...]
        r = lax.rsqrt(jnp.mean(xv * xv, axis=-1, keepdims=True) + EPS)
        xh = xv * r
        dxh = acc * g_ref[...]
        dx_ref[...] = dr_ref[...] + r * (dxh - xh * jnp.mean(dxh * xh, axis=-1, keepdims=True))
        part = jnp.sum(acc * xh, axis=0, keepdims=True)

        @pl.when(ids[first_axis] == 0)
        def _():
            dg_ref[...] = part

        @pl.when(ids[first_axis] > 0)
        def _():
            dg_ref[...] += part

    return epi


def _loss_head(name, x, g, target):
    l, d = x.shape
    tm = _tile(l, 512, SUBLANE)
    n = l // tm

    def body(x_ref, g_ref, t_ref, dx_ref, dg_ref, loss_ref, lacc):
        i = pl.program_id(0)
        xv = x_ref[...]
        r = lax.rsqrt(jnp.mean(xv * xv, axis=-1, keepdims=True) + EPS)
        xh = xv * r
        e = xh * g_ref[...] - t_ref[...]
        dy = e * (1.0 / d)
        dxh = dy * g_ref[...]
        dx_ref[...] = r * (dxh - xh * jnp.mean(dxh * xh, axis=-1, keepdims=True))
        dg_part = jnp.sum(dy * xh, axis=0, keepdims=True)
        l_part = jnp.sum(e * e, axis=0, keepdims=True)

        @pl.when(i == 0)
        def _():
            dg_ref[...] = dg_part
            lacc[...] = l_part

        @pl.when(i > 0)
        def _():
            dg_ref[...] += dg_part
            lacc[...] += l_part

        @pl.when(i == n - 1)
        def _():
            loss_ref[...] = jnp.zeros_like(loss_ref) + jnp.sum(lacc[...]) * (0.5 / d)

    return pl.pallas_call(
        body, name=name, grid=(n,),
        in_specs=[pl.BlockSpec((tm, d), lambda i: (i, 0)), pl.BlockSpec((1, d), lambda i: (0, 0)),
                  pl.BlockSpec((tm, d), lambda i: (i, 0))],
        out_specs=[pl.BlockSpec((tm, d), lambda i: (i, 0)), pl.BlockSpec((1, d), lambda i: (0, 0)),
                   pl.BlockSpec((1, LANE), lambda i: (0, 0))],
        out_shape=[_sds((l, d), F32), _sds((1, d), F32), _sds((1, LANE), F32)],
        scratch_shapes=[pltpu.VMEM((1, d), F32)],
        compiler_params=_params(("arbitrary",)),
    )(x, g, target)


def _ffn_up(name, hn, w1):
    l, d = hn.shape
    f = w1.shape[1] // 2
    tm, tn = _tile(l, 512, SUBLANE), _tile(f, 1408, LANE)
    nj = f // tn

    def body(h_ref, wg_ref, wu_ref, gu_ref, a_ref):
        h = h_ref[...]
        g = jnp.dot(h, wg_ref[...], preferred_element_type=F32)
        u = jnp.dot(h, wu_ref[...], preferred_element_type=F32)
        s = _sigmoid(g)
        gs = g * s
        gu_ref[0] = (u * (s + gs * (1.0 - s))).astype(BF)
        gu_ref[1] = gs.astype(BF)
        a_ref[...] = (gs * u).astype(BF)

    return pl.pallas_call(
        body, name=name, grid=(nj, l // tm),
        in_specs=[pl.BlockSpec((tm, d), lambda j, i: (i, 0)), pl.BlockSpec((d, tn), lambda j, i: (0, j)),
                  pl.BlockSpec((d, tn), lambda j, i: (0, j + nj))],
        out_specs=[pl.BlockSpec((2, tm, tn), lambda j, i: (0, i, j)), pl.BlockSpec((tm, tn), lambda j, i: (i, j))],
        out_shape=[_sds((2, l, f), BF), _sds((l, f), BF)],
        compiler_params=_params(("arbitrary", "arbitrary")),
    )(hn, w1, w1)


def _ffn_fwd(tag, x, gnorm, w1, get_w2):
    hn = _rms_fwd(tag + "_norm", x, gnorm)
    gu, a = _ffn_up(tag + "_up", hn, w1)
    w2 = get_w2(a)

    def epi(acc, e_refs, o_refs, ids):
        o_refs[0][...] = e_refs[0][...] + 0.5 * acc

    l, d = x.shape
    x_new = _mm_plain(tag + "_down", a, w2, NN, F32, tm=512, tn=d, tk=w2.shape[0], epi=epi, eins=[x])
    return x_new, (hn, gu, a), w2


def _ffn_bwd(tag, dres, x, gnorm, w1, w2, saved, on_grads):
    hn, gu, a = saved
    l, d = x.shape
    f = w2.shape[0]
    tm, tn = _tile(l, 512, SUBLANE), _tile(f, 1408, LANE)
    nj = f // tn

    def epi_gu(acc, e_refs, o_refs, ids):
        da = 0.5 * acc
        o_refs[0][0] = (da * e_refs[0][0].astype(F32)).astype(BF)
        o_refs[0][1] = (da * e_refs[0][1].astype(F32)).astype(BF)

    gu_spec = pl.BlockSpec((2, tm, tn), lambda j, i, kk: (0, i, j))
    dgu = _mm(tag + "_dgu",
              [(dres, pl.BlockSpec((tm, d), lambda j, i, kk: (i, 0)), w2, pl.BlockSpec((tn, d), lambda j, i, kk: (j, 0)))],
              NT, (nj, l // tm, 1), [_sds((2, l, f), BF)], [gu_spec], (tm, tn), epi_gu, [gu], [gu_spec])[0]

    def epi_half(acc, e_refs, o_refs, ids):
        o_refs[0][...] = (0.5 * acc).astype(BF)

    dw2 = _mm_plain(tag + "_dw2", a, dres, TN, BF, tm=1408, tn=d, tk=1024, epi=epi_half)

    tk = _tile(l, 1024, SUBLANE)
    dw1 = _mm(tag + "_dw1",
              [(hn, pl.BlockSpec((tk, d), lambda i, j, kk: (kk, 0)), dgu, pl.BlockSpec((None, tk, f), lambda i, j, kk: (j, kk, 0)))],
              TN, (1, 2, l // tk), [_sds((d, 2 * f), BF)], [pl.BlockSpec((d, f), lambda i, j, kk: (0, j))], (d, f))[0]

    gnorm = on_grads(dw1, dw2, gnorm)
    row = pl.BlockSpec((tm, d), lambda i, j, kk: (i, 0))
    vec = pl.BlockSpec((1, d), lambda i, j, kk: (0, 0))
    once = pl.Buffered(1)
    dx, dg = _mm(tag + "_dhn",
                 [(dgu, pl.BlockSpec((None, tm, f), lambda i, j, kk: (0, i, 0)), w1, pl.BlockSpec((d, f), lambda i, j, kk: (0, 0), pipeline_mode=once)),
                  (dgu, pl.BlockSpec((None, tm, f), lambda i, j, kk: (1, i, 0)), w1, pl.BlockSpec((d, f), lambda i, j, kk: (0, 1), pipeline_mode=once))],
                 NT, (l // tm, 1, 1), [_sds((l, d), F32), _sds((1, d), F32)], [row, vec], (tm, d),
                 _rms_bwd_epi(0), [x, gnorm, dres], [row, vec, row])
    return dx, dg


def _s5_chunk_tables(lam_re, lam_im, b_re, b_im, c_re, c_im, log_dt, hs):
    f32 = F32
    g, n = lam_re.shape[1], lam_re.shape[2]
    p = b_re.shape[-1]
    nch, gpc, nt = (g * n) // hs, hs // n, hs // LANE
    lr = jnp.minimum(lam_re.astype(f32), -1e-4)
    li = lam_im.astype(f32)
    dt = jnp.exp(log_dt.astype(f32))[..., None]
    mag = jnp.exp(lr * dt)
    ar = mag * jnp.cos(li * dt)
    ai = mag * jnp.sin(li * dt)
    den = lr * lr + li * li
    cr = ((ar - 1.0) * lr + ai * li) / den
    ci = (ai * lr - (ar - 1.0) * li) / den
    bbr = cr[..., None] * b_re - ci[..., None] * b_im
    bbi = cr[..., None] * b_im + ci[..., None] * b_re
    eye = jnp.eye(gpc, dtype=f32)[None, None, None, :, None, None, :]
    a_f = jnp.stack([ar, ai], axis=1).reshape(2, 2, nch, nt, LANE).transpose(0, 2, 1, 3, 4).reshape(2, nch, 2 * nt, LANE)
    bb = jnp.stack([bbr, bbi], axis=1).reshape(2, 2, nch, gpc, n, p)
    bd = (bb[..., None] * eye).transpose(0, 2, 6, 5, 1, 3, 4).reshape(2, nch, gpc * p, 2 * hs)
    cc = jnp.stack([c_re, -c_im], axis=1).reshape(2, 2, nch, gpc, p, n)
    cd = (cc[..., None] * eye).transpose(0, 2, 1, 3, 5, 6, 4).reshape(2, nch, 2 * hs, gpc * p)
    return a_f, bd, cd


def _fold_store(ref, val, tb, ntiles):
    for s in range(ntiles):
        ref[:, s * SUBLANE:(s + 1) * SUBLANE, :] = val[:, s * LANE:(s + 1) * LANE].reshape(tb // SUBLANE, SUBLANE, LANE)


def _unfold(ref, tb, ntiles):
    return jnp.concatenate([ref[:, s * SUBLANE:(s + 1) * SUBLANE, :].reshape(tb, LANE) for s in range(ntiles)], axis=1)


def _s5_fwd(name, proj, bd, cd, a_f, reverse):
    l = proj.shape[0]
    nch, cu, hs2 = bd.shape
    nt = hs2 // (2 * LANE)
    frows = 2 * nt * SUBLANE
    tb = _tile(l, 512, SUBLANE)
    nb = l // tb

    def body(u_ref, bd_ref, cd_ref, a_ref, xf_ref, y_ref, st):
        r = pl.program_id(1)

        @pl.when(r == 0)
        def _():
            st[...] = jnp.zeros_like(st)

        _fold_store(xf_ref, _dot(u_ref[...], bd_ref[...]), tb, 2 * nt)
        ar, ai = a_ref[0:nt, :], a_ref[nt:2 * nt, :]

        def group(gi, carry):
            rr = (tb // SUBLANE - 1 - gi) if reverse else gi
            sr, si = carry
            for qq in range(SUBLANE):
                q = (SUBLANE - 1 - qq) if reverse else qq
                re_rows, im_rows = pl.ds(q, nt, stride=SUBLANE), pl.ds(nt * SUBLANE + q, nt, stride=SUBLANE)
                nr = ar * sr - ai * si + xf_ref[rr, re_rows, :]
                ni = ar * si + ai * sr + xf_ref[rr, im_rows, :]
                xf_ref[rr, re_rows, :] = nr
                xf_ref[rr, im_rows, :] = ni
                sr, si = nr, ni
            return sr, si

        fin = lax.fori_loop(0, tb // SUBLANE, group, (st[0:nt, :], st[nt:2 * nt, :]))
        st[0:nt, :] = fin[0]
        st[nt:2 * nt, :] = fin[1]
        y_ref[...] = _dot(_unfold(xf_ref, tb, 2 * nt), cd_ref[...])

    def rows(r):
        return (nb - 1 - r) if reverse else r

    return pl.pallas_call(
        body, name=name, grid=(nch, nb),
        in_specs=[pl.BlockSpec((tb, cu), lambda c, r: (rows(r), c)), pl.BlockSpec((None, cu, hs2), lambda c, r: (c, 0, 0)),
                  pl.BlockSpec((None, hs2, cu), lambda c, r: (c, 0, 0)), pl.BlockSpec((None, 2 * nt, LANE), lambda c, r: (c, 0, 0))],
        out_specs=[pl.BlockSpec((tb // SUBLANE, frows, LANE), lambda c, r: (rows(r), c, 0)), pl.BlockSpec((tb, cu), lambda c, r: (rows(r), c))],
        out_shape=[_sds((l // SUBLANE, nch * frows, LANE), F32), _sds((l, nch * cu), F32)],
        scratch_shapes=[pltpu.VMEM((2 * nt, LANE), F32)],
        compiler_params=_params(("arbitrary", "arbitrary")),
    )(proj, bd, cd, a_f)


def _s5_bwd(name, proj, dy, xf, bd, cd, a_conj, reverse):
    l = proj.shape[0]
    nch, cu, hs2 = bd.shape
    nt = hs2 // (2 * LANE)
    frows = 2 * nt * SUBLANE
    tb = _tile(l, 512, SUBLANE)
    nb = l // tb

    def body(u_ref, dy_ref, xs_ref, bd_ref, cd_ref, a_ref, du_ref, dbd_ref, dcd_ref, da_ref, lam, st):
        r = pl.program_id(1)

        @pl.when(r == 0)
        def _():
            st[...] = jnp.zeros_like(st)
            dbd_ref[...] = jnp.zeros_like(dbd_ref)
            dcd_ref[...] = jnp.zeros_like(dcd_ref)
            da_ref[...] = jnp.zeros_like(da_ref)

        dyv = dy_ref[...]
        _fold_store(lam, _dot(dyv, cd_ref[...], NT), tb, 2 * nt)
        ar, ai = a_ref[0:nt, :], a_ref[nt:2 * nt, :]

        def group(gi, carry):
            rr = (tb // SUBLANE - 1 - gi) if reverse else gi
            sr, si, cr, ci = carry
            for qq in range(SUBLANE):
                q = (SUBLANE - 1 - qq) if reverse else qq
                re_rows, im_rows = pl.ds(q, nt, stride=SUBLANE), pl.ds(nt * SUBLANE + q, nt, stride=SUBLANE)
                xr, xi = xs_ref[rr, re_rows, :], xs_ref[rr, im_rows, :]
                cr = cr + sr * xr + si * xi
                ci = ci + si * xr - sr * xi
                nr = ar * sr - ai * si + lam[rr, re_rows, :]
                ni = ar * si + ai * sr + lam[rr, im_rows, :]
                lam[rr, re_rows, :] = nr
                lam[rr, im_rows, :] = ni
                sr, si = nr, ni
            return sr, si, cr, ci

        zero = jnp.zeros((nt, LANE), F32)
        fin = lax.fori_loop(0, tb // SUBLANE, group, (st[0:nt, :], st[nt:2 * nt, :], zero, zero))
        st[0:nt, :] = fin[0]
        st[nt:2 * nt, :] = fin[1]
        da_ref[0:nt, :] += fin[2]
        da_ref[nt:2 * nt, :] += fin[3]
        lam_u = _unfold(lam, tb, 2 * nt)
        du_ref[...] = _dot(lam_u, bd_ref[...], NT)
        dbd_ref[...] += _dot(u_ref[...], lam_u, TN)
        dcd_ref[...] += _dot(_unfold(xs_ref, tb, 2 * nt), dyv, TN)

    def rows(r):
        return (nb - 1 - r) if reverse else r

    chunk_rows = pl.BlockSpec((tb, cu), lambda c, r: (rows(r), c))
    bd_spec = pl.BlockSpec((None, cu, hs2), lambda c, r: (c, 0, 0))
    cd_spec = pl.BlockSpec((None, hs2, cu), lambda c, r: (c, 0, 0))
    a_spec = pl.BlockSpec((None, 2 * nt, LANE), lambda c, r: (c, 0, 0))
    return pl.pallas_call(
        body, name=name, grid=(nch, nb),
        in_specs=[chunk_rows, chunk_rows, pl.BlockSpec((tb // SUBLANE, frows, LANE), lambda c, r: (rows(r), c, 0)), bd_spec, cd_spec, a_spec],
        out_specs=[chunk_rows, bd_spec, cd_spec, a_spec],
        out_shape=[_sds((l, nch * cu), F32), _sds((nch, cu, hs2), F32), _sds((nch, hs2, cu), F32), _sds((nch, 2 * nt, LANE), F32)],
        scratch_shapes=[pltpu.VMEM((tb // SUBLANE, frows, LANE), F32), pltpu.VMEM((2 * nt, LANE), F32)],
        compiler_params=_params(("arbitrary", "arbitrary")),
    )(proj, dy, xf, bd, cd, a_conj)


def _s5_du(name, du_f, du_b, dy, proj, d_row):
    l, w = dy.shape
    tm = _tile(l, 1024, SUBLANE)

    def body(f_ref, b_ref, dy_ref, u_ref, d_ref, du_ref, dd_ref):
        i = pl.program_id(0)
        dyv = dy_ref[...]
        du_ref[...] = (f_ref[...] + b_ref[...] + dyv * d_ref[...]).astype(du_ref.dtype)
        part = jnp.sum(dyv * u_ref[...].astype(F32), axis=0, keepdims=True)

        @pl.when(i == 0)
        def _():
            dd_ref[...] = part

        @pl.when(i > 0)
        def _():
            dd_ref[...] += part

    row = pl.BlockSpec((tm, w), lambda i: (i, 0))
    vec = pl.BlockSpec((1, w), lambda i: (0, 0))
    return pl.pallas_call(
        body, name=name, grid=(l // tm,), in_specs=[row, row, row, row, vec], out_specs=[row, vec],
        out_shape=[_sds((l, w), BF), _sds((1, w), F32)],
        compiler_params=_params(("arbitrary",)),
    )(du_f, du_b, dy, proj, d_row)


def _gelu(y):
    c = math.sqrt(2.0 / math.pi)
    return 0.5 * y * (1.0 + jnp.tanh(c * (y + 0.044715 * y * y * y)))


def _gelu_grad(y):
    c = math.sqrt(2.0 / math.pi)
    th = jnp.tanh(c * (y + 0.044715 * y * y * y))
    return 0.5 * (1.0 + th) + 0.5 * y * (1.0 - th * th) * c * (1.0 + 3.0 * 0.044715 * y * y)


def _glu_fwd(name, y_f, y_b, proj, d_row, w):
    l, wd = y_f.shape
    tm = _tile(l, 512, SUBLANE)

    def body(yf_ref, yb_ref, u_ref, d_ref, w_ref, y_ref, o_ref):
        y = yf_ref[...] + yb_ref[...] + u_ref[...].astype(F32) * d_ref[...]
        y_ref[...] = y
        gy = _gelu(y)
        z = _dot(gy, w_ref[...])
        o_ref[...] = (gy * _sigmoid(z)).astype(o_ref.dtype)

    row = pl.BlockSpec((tm, wd), lambda i: (i, 0))
    return pl.pallas_call(
        body, name=name, grid=(l // tm,),
        in_specs=[row, row, row, pl.BlockSpec((1, wd), lambda i: (0, 0)), pl.BlockSpec((wd, wd), lambda i: (0, 0))],
        out_specs=[row, row], out_shape=[_sds((l, wd), F32), _sds((l, wd), BF)],
        compiler_params=_params(("arbitrary",)),
    )(y_f, y_b, proj, d_row, w)


def _glu_bwd(name, y, w, dout, dcol):
    l, wd = y.shape
    tm = _tile(l, 512, SUBLANE)

    def body(y_ref, w_ref, d_ref, dy_ref, dw_ref):
        i = pl.program_id(0)
        yv = y_ref[...]
        gy = _gelu(yv)
        s = _sigmoid(_dot(gy, w_ref[...]))
        d = d_ref[...].astype(F32)
        t = d * gy * s * (1.0 - s)
        dgy = d * s + _dot(t, w_ref[...], NT)
        dy_ref[...] = dgy * _gelu_grad(yv)
        part = _dot(gy, t, TN)

        @pl.when(i == 0)
        def _():
            dw_ref[...] = part

        @pl.when(i > 0)
        def _():
            dw_ref[...] += part

    return pl.pallas_call(
        body, name=name, grid=(l // tm,),
        in_specs=[pl.BlockSpec((tm, wd), lambda i: (i, 0)), pl.BlockSpec((wd, wd), lambda i: (0, 0)),
                  pl.BlockSpec((tm, wd), lambda i: (i, dcol))],
        out_specs=[pl.BlockSpec((tm, wd), lambda i: (i, 0)), pl.BlockSpec((wd, wd), lambda i: (0, 0))],
        out_shape=[_sds((l, wd), F32), _sds((wd, wd), F32)],
        compiler_params=_params(("arbitrary",)),
    )(y, w, dout)


def _log_sigmoid(x):
    return jnp.minimum(x, 0.0) - jnp.log(1.0 + jnp.exp(-jnp.abs(x)))


def _gate_fwd(name, glo, wf, wb, bf, bb):
    l, r2 = glo.shape
    hk = wf.shape[1]
    tm = _tile(l, 1024, SUBLANE)

    def body(x_ref, wf_ref, wb_ref, bf_ref, bb_ref, gf_ref, gb_ref):
        xv = x_ref[...]
        gf_ref[...] = _log_sigmoid(_dot(xv, wf_ref[...]) + bf_ref[...]) * (1.0 / GLA_GATE_NORM)
        gb_ref[...] = _log_sigmoid(_dot(xv, wb_ref[...]) + bb_ref[...]) * (1.0 / GLA_GATE_NORM)

    w_spec = pl.BlockSpec((r2, hk), lambda i: (0, 0))
    b_spec = pl.BlockSpec((1, hk), lambda i: (0, 0))
    o_spec = pl.BlockSpec((tm, hk), lambda i: (i, 0))
    return pl.pallas_call(
        body, name=name, grid=(l // tm,),
        in_specs=[pl.BlockSpec((tm, r2), lambda i: (i, 0)), w_spec, w_spec, b_spec, b_spec],
        out_specs=[o_spec, o_spec], out_shape=[_sds((l, hk), F32), _sds((l, hk), F32)],
        compiler_params=_params(("arbitrary",)),
    )(glo, wf, wb, bf, bb)


def _gate_bwd(name, glo, wf, wb, bf, bb, dgf, dgb):
    l, r2 = glo.shape
    hk = wf.shape[1]
    tm = _tile(l, 1024, SUBLANE)

    def body(x_ref, wf_ref, wb_ref, bf_ref, bb_ref, dgf_ref, dgb_ref, dx_ref, dwf_ref, dwb_ref, dbf_ref, dbb_ref):
        i = pl.program_id(0)
        xv = x_ref[...]
        kf = _dot(xv, wf_ref[...]) + bf_ref[...]
        kb = _dot(xv, wb_ref[...]) + bb_ref[...]
        dkf = dgf_ref[...] * (1.0 / GLA_GATE_NORM) * _sigmoid(-kf)
        dkb = dgb_ref[...] * (1.0 / GLA_GATE_NORM) * _sigmoid(-kb)
        dx_ref[...] = _dot(dkf, wf_ref[...], NT) + _dot(dkb, wb_ref[...], NT)
        parts = (_dot(xv, dkf, TN), _dot(xv, dkb, TN), jnp.sum(dkf, axis=0, keepdims=True), jnp.sum(dkb, axis=0, keepdims=True))
        accs = (dwf_ref, dwb_ref, dbf_ref, dbb_ref)

        @pl.when(i == 0)
        def _():
            for a_, p_ in zip(accs, parts):
                a_[...] = p_

        @pl.when(i > 0)
        def _():
            for a_, p_ in zip(accs, parts):
                a_[...] += p_

    w_spec = pl.BlockSpec((r2, hk), lambda i: (0, 0))
    b_spec = pl.BlockSpec((1, hk), lambda i: (0, 0))
    g_spec = pl.BlockSpec((tm, hk), lambda i: (i, 0))
    x_spec = pl.BlockSpec((tm, r2), lambda i: (i, 0))
    return pl.pallas_call(
        body, name=name, grid=(l // tm,),
        in_specs=[x_spec, w_spec, w_spec, b_spec, b_spec, g_spec, g_spec],
        out_specs=[x_spec, w_spec, w_spec, b_spec, b_spec],
        out_shape=[_sds((l, r2), F32), _sds((r2, hk), F32), _sds((r2, hk), F32), _sds((1, hk), F32), _sds((1, hk), F32)],
        compiler_params=_params(("arbitrary",)),
    )(glo, wf, wb, bf, bb, dgf, dgb)


def _chunk_terms(qc, kc, gc, lg, chunk, reverse):
    ri = lax.broadcasted_iota(jnp.int32, (chunk, chunk), 0)
    ci = lax.broadcasted_iota(jnp.int32, (chunk, chunk), 1)
    if reverse:
        tri = ci >= ri
        mask = ci > ri
    else:
        tri = ci <= ri
        mask = ci <= ri
    if gc is not None:
        cum = _dot3(tri.astype(F32), gc)
        last = cum[0:1, :] if reverse else cum[chunk - 1:chunk, :]
    else:
        pos = lax.broadcasted_iota(jnp.int32, (chunk, 1), 0).astype(F32)
        cum = ((chunk - pos) if reverse else (pos + 1.0)) * lg
        last = chunk * lg
    e = jnp.exp(cum)
    einv = jnp.exp(-cum)
    dec = jnp.exp(last - cum)
    return e, einv, dec, qc * e, kc * einv, kc * dec, jnp.exp(last), mask, tri


def _lin_specs(arr, width, col, tb, nb, reverse, per_head):
    if per_head:
        return pl.BlockSpec((tb, width), lambda h, r: ((nb - 1 - r) if reverse else r, col + h))
    return pl.BlockSpec((tb, width), lambda h, r: ((nb - 1 - r) if reverse else r, col))


def _lin_fwd(name, q, k, v, g, lgtab, *, heads, hb, dk, dv, chunk, tb, qcol, kcol, vcol, qscale, reverse):
    l = q.shape[0]
    nb = l // tb
    ncb = tb // chunk
    ng = heads // hb
    gated = g is not None
    per_head = ng > 1

    def body(*refs):
        if gated:
            q_ref, k_ref, v_ref, g_ref, o_ref, sp_ref, st = refs
        else:
            q_ref, k_ref, v_ref, lg_ref, o_ref, sp_ref, st = refs
        r = pl.program_id(1)

        @pl.when(r == 0)
        def _():
            st[...] = jnp.zeros_like(st)

        for c in range(ncb):
            cc = (ncb - 1 - c) if reverse else c
            rows = pl.ds(cc * chunk, chunk)
            for h in range(hb):
                lg = None if gated else lg_ref[h, :, 0:1]
                qc = q_ref[rows, h * dk:(h + 1) * dk].astype(F32) * qscale
                kc = k_ref[rows, h * dk:(h + 1) * dk].astype(F32)
                vc = v_ref[rows, h * dv:(h + 1) * dv]
                gc = g_ref[rows, h * dk:(h + 1) * dk] if gated else None
                _, _, _, qd, ki, kdec, e_last, mask, _ = _chunk_terms(qc, kc, gc, lg, chunk, reverse)
                a = jnp.where(mask, _dot(qd, ki, NT), 0.0)
                s_t = st[h]
                o_ref[rows, h * dv:(h + 1) * dv] = _dot(a, vc) + _dot(qd, s_t, NT)
                sp_ref[cc, h] = s_t
                st[h] = s_t * e_last + _dot(vc, kdec, TN)

    in_specs = [_lin_specs(q, hb * dk, qcol, tb, nb, reverse, per_head), _lin_specs(k, hb * dk, kcol, tb, nb, reverse, per_head),
                _lin_specs(v, hb * dv, vcol, tb, nb, reverse, per_head)]
    args = [q, k, v]
    if gated:
        in_specs.append(_lin_specs(g, hb * dk, 0, tb, nb, reverse, per_head))
        args.append(g)
    else:
        in_specs.append(pl.BlockSpec((hb, 1, LANE), lambda h, r: (h, 0, 0)))
        args.append(lgtab)
    out_specs = [_lin_specs(None, hb * dv, 0, tb, nb, reverse, per_head),
                 pl.BlockSpec((ncb, hb, dv, dk), lambda h, r: ((nb - 1 - r) if reverse else r, h, 0, 0))]
    outs = [_sds((l, heads * dv), F32), _sds((l // chunk, heads, dv, dk), F32)]
    return pl.pallas_call(
        body, name=name, grid=(ng, nb), in_specs=in_specs, out_specs=out_specs, out_shape=outs,
        scratch_shapes=[pltpu.VMEM((hb, dv, dk), F32)],
        compiler_params=_params(("arbitrary", "arbitrary")),
    )(*args)


def _lin_bwd(name, q, k, v, g, lgtab, sprev, do, prev, *, heads, hb, dk, dv, chunk, tb, qcol, kcol, vcol, qscale, reverse):
    l = q.shape[0]
    nb = l // tb
    ncb = tb // chunk
    ng = heads // hb
    gated = g is not None
    per_head = ng > 1
    brev = not reverse
    n_prev = 0 if prev is None else len(prev)

    def body(*refs):
        q_ref, k_ref, v_ref, x_ref, sp_ref, do_ref = refs[:6]
        p_refs = refs[6:6 + n_prev]
        o_refs = refs[6 + n_prev:-1]
        dst = refs[-1]
        dq_ref, dk_ref, dv_ref = o_refs[:3]
        r = pl.program_id(1)

        @pl.when(r == 0)
        def _():
            dst[...] = jnp.zeros_like(dst)

        for c in range(ncb):
            cc = (ncb - 1 - c) if brev else c
            rows = pl.ds(cc * chunk, chunk)
            for h in range(hb):
                lg = None if gated else x_ref[h, :, 0:1]
                ks = slice(h * dk, (h + 1) * dk)
                vs = slice(h * dv, (h + 1) * dv)
                qc = q_ref[rows, ks].astype(F32) * qscale
                kc = k_ref[rows, ks].astype(F32)
                vc = v_ref[rows, vs]
                gc = x_ref[rows, ks] if gated else None
                e, einv, dec, qd, ki, kdec, e_last, mask, tri = _chunk_terms(qc, kc, gc, lg, chunk, reverse)
                a = jnp.where(mask, _dot(qd, ki, NT), 0.0)
                s_t = sp_ref[cc, h]
                ds_t = dst[h]
                doc = do_ref[rows, vs]
                dvc = _dot(a, doc, TN) + _dot(kdec, ds_t, NT)
                da = jnp.where(mask, _dot(doc, vc, NT), 0.0)
                dqd = _dot(da, ki) + _dot(doc, s_t)
                dki = _dot(da, qd, TN)
                dkdec = _dot(vc, ds_t)
                dst[h] = ds_t * e_last + _dot(doc, qd, TN)
                dqc = dqd * e * qscale
                dkc = dki * einv + dkdec * dec
                if n_prev:
                    dqc = dqc + p_refs[0][rows, ks]
                    dkc = dkc + p_refs[1][rows, ks]
                    dvc = dvc + p_refs[2][rows, vs]
                dq_ref[rows, ks] = dqc
                dk_ref[rows, ks] = dkc
                dv_ref[rows, vs] = dvc
                if gated:
                    dcum = dqd * qd - dki * ki - dkdec * kdec
                    dlast = jnp.sum(dkdec * kdec, axis=0, keepdims=True) + e_last * jnp.sum(s_t * ds_t, axis=0, keepdims=True)
                    rid = lax.broadcasted_iota(jnp.int32, (chunk, 1), 0)
                    dcum = dcum + jnp.where(rid == (0 if reverse else chunk - 1), dlast, 0.0)
                    dgc = _dot3(tri.astype(F32), dcum, TN)
                    o_refs[3][rows, ks] = dgc

    def spec(width, col):
        return _lin_specs(None, width, col, tb, nb, brev, per_head)

    in_specs = [spec(hb * dk, qcol), spec(hb * dk, kcol), spec(hb * dv, vcol)]
    args = [q, k, v]
    if gated:
        in_specs.append(spec(hb * dk, 0))
        args.append(g)
    else:
        in_specs.append(pl.BlockSpec((hb, 1, LANE), lambda h, r: (h, 0, 0)))
        args.append(lgtab)
    in_specs.append(pl.BlockSpec((ncb, hb, dv, dk), lambda h, r: ((nb - 1 - r) if brev else r, h, 0, 0)))
    args.append(sprev)
    in_specs.append(spec(hb * dv, 0))
    args.append(do)
    out_specs = [spec(hb * dk, 0), spec(hb * dk, 0), spec(hb * dv, 0)]
    outs = [_sds((l, heads * dk), F32), _sds((l, heads * dk), F32), _sds((l, heads * dv), F32)]
    if gated:
        out_specs.append(spec(hb * dk, 0))
        outs.append(_sds((l, heads * dk), F32))
    if n_prev:
        in_specs += out_specs[:3]
        args += list(prev)
    return pl.pallas_call(
        body, name=name, grid=(ng, nb), in_specs=in_specs, out_specs=out_specs, out_shape=outs,
        scratch_shapes=[pltpu.VMEM((hb, dv, dk), F32)],
        compiler_params=_params(("arbitrary", "arbitrary")),
    )(*args)


def _headgate_fwd(name, o_f, o_b, og_arr, og_col, gn, dv):
    l, w = o_f.shape
    tm = _tile(l, 512, SUBLANE)
    nh = w // dv

    def body(of_ref, ob_ref, og_ref, gn_ref, out_ref):
        for h in range(nh):
            cs = slice(h * dv, (h + 1) * dv)
            o = of_ref[:, cs] + ob_ref[:, cs]
            r = lax.rsqrt(jnp.mean(o * o, axis=-1, keepdims=True) + EPS)
            og = og_ref[:, cs].astype(F32)
            out_ref[:, cs] = (o * r * gn_ref[:, cs] * (og * _sigmoid(og))).astype(out_ref.dtype)

    row = pl.BlockSpec((tm, w), lambda i: (i, 0))
    return pl.pallas_call(
        body, name=name, grid=(l // tm,),
        in_specs=[row, row, pl.BlockSpec((tm, w), lambda i: (i, og_col)), pl.BlockSpec((1, w), lambda i: (0, 0))],
        out_specs=row, out_shape=_sds((l, w), BF),
        compiler_params=_params(("arbitrary",)),
    )(o_f, o_b, og_arr, gn)


def _headgate_bwd(name, o_f, o_b, og_arr, og_col, gn, dout, dcol, dv):
    l, w = o_f.shape
    tm = _tile(l, 512, SUBLANE)
    nh = w // dv

    def body(of_ref, ob_ref, og_ref, gn_ref, d_ref, do_ref, dog_ref, dgn_ref):
        i = pl.program_id(0)
        for h in range(nh):
            cs = slice(h * dv, (h + 1) * dv)
            o = of_ref[:, cs] + ob_ref[:, cs]
            r = lax.rsqrt(jnp.mean(o * o, axis=-1, keepdims=True) + EPS)
            oh = o * r
            og = og_ref[:, cs].astype(F32)
            s = _sigmoid(og)
            d = d_ref[:, cs].astype(F32)
            gnv = gn_ref[:, cs]
            d_on = d * (og * s)
            dog_ref[:, cs] = (d * (oh * gnv) * s * (1.0 + og * (1.0 - s))).astype(dog_ref.dtype)
            doh = d_on * gnv
            do_ref[:, cs] = r * (doh - oh * jnp.mean(doh * oh, axis=-1, keepdims=True))
            part = jnp.sum(d_on * oh, axis=0, keepdims=True)

            @pl.when(i == 0)
            def _():
                dgn_ref[:, cs] = part

            @pl.when(i > 0)
            def _():
                dgn_ref[:, cs] += part

    row = pl.BlockSpec((tm, w), lambda i: (i, 0))
    vec = pl.BlockSpec((1, w), lambda i: (0, 0))
    return pl.pallas_call(
        body, name=name, grid=(l // tm,),
        in_specs=[row, row, pl.BlockSpec((tm, w), lambda i: (i, og_col)), vec, pl.BlockSpec((tm, w), lambda i: (i, dcol))],
        out_specs=[row, row, vec], out_shape=[_sds((l, w), F32), _sds((l, w), BF), _sds((1, w), F32)],
        compiler_params=_params(("arbitrary",)),
    )(o_f, o_b, og_arr, gn, dout)


def _rot_tables(l, dk):
    half = dk // 2
    pos = jnp.arange(l, dtype=F32)
    inv = jnp.exp(-math.log(ROPE_BASE) * jnp.arange(half, dtype=F32) / half)
    ang = pos[:, None] * inv[None, :]
    cos, sin = jnp.cos(ang), jnp.sin(ang)
    return jnp.concatenate([cos, cos], axis=-1), jnp.concatenate([-sin, sin], axis=-1)


def _rot_apply(name, src_q, qcol, src_k, kcol, cos_t, sin_t, heads, dk, kscale, out_dtype, transpose):
    l = src_q.shape[0]
    w = heads * dk
    tm = _tile(l, 512, SUBLANE)

    def rot(t, cos_v, sin_v):
        if transpose:
            return t * cos_v + pltpu.roll(t * sin_v, dk // 2, 1)
        return t * cos_v + pltpu.roll(t, dk // 2, 1) * sin_v

    def body(q_ref, k_ref, c_ref, s_ref, qo_ref, ko_ref):
        cos_v, sin_v = c_ref[...], s_ref[...]
        for h in range(heads):
            cs = slice(h * dk, (h + 1) * dk)
            qo_ref[:, cs] = rot(q_ref[:, cs].astype(F32), cos_v, sin_v).astype(out_dtype)
            ko_ref[:, cs] = (rot(k_ref[:, cs].astype(F32), cos_v, sin_v) * kscale).astype(out_dtype)

    tab = pl.BlockSpec((tm, dk), lambda i: (i, 0))
    row = pl.BlockSpec((tm, w), lambda i: (i, 0))
    return pl.pallas_call(
        body, name=name, grid=(l // tm,),
        in_specs=[pl.BlockSpec((tm, w), lambda i: (i, qcol)), pl.BlockSpec((tm, w), lambda i: (i, kcol)), tab, tab],
        out_specs=[row, row], out_shape=[_sds((l, w), out_dtype), _sds((l, w), out_dtype)],
        compiler_params=_params(("arbitrary",)),
    )(src_q, src_k, cos_t, sin_t)


def _exchange(name, src, gather):
    shape = src.shape if gather else src.shape[1:]

    def body(src_ref, out_ref, send_sems, recv_sems, local_sem):
        me = _my_index()
        own = pltpu.make_async_copy(src_ref if gather else src_ref.at[me], out_ref.at[me], local_sem)
        own.start()
        copies = _peer_copies(src_ref, out_ref, send_sems, recv_sems, gather)
        for cp in copies:
            cp.start()
        for cp in copies:
            cp.wait_recv()
        for cp in copies:
            cp.wait_send()
        own.wait()

    return pl.pallas_call(
        body, name=name,
        in_specs=[pl.BlockSpec(memory_space=pl.ANY)], out_specs=pl.BlockSpec(memory_space=pl.ANY),
        out_shape=_sds((N_DEV,) + tuple(shape), src.dtype),
        scratch_shapes=[pltpu.SemaphoreType.DMA((N_DEV - 1,)), pltpu.SemaphoreType.DMA((N_DEV - 1,)), pltpu.SemaphoreType.DMA],
        )(src)


def _my_index():
    return 4 * lax.axis_index("x") + 2 * lax.axis_index("y") + lax.axis_index("c")


def _peer_copies(src_ref, out_ref, send_sems, recv_sems, gather):
    x, y, c = lax.axis_index("x"), lax.axis_index("y"), lax.axis_index("c")
    me = 4 * x + 2 * y + c
    copies = []
    for kk in range(1, N_DEV):
        px = (1 - x) if kk & 4 else x
        py = (1 - y) if kk & 2 else y
        pc = (1 - c) if kk & 1 else c
        peer = 4 * px + 2 * py + pc
        copies.append(pltpu.make_async_remote_copy(
            src_ref=src_ref if gather else src_ref.at[peer], dst_ref=out_ref.at[me],
            send_sem=send_sems.at[kk - 1], recv_sem=recv_sems.at[kk - 1],
            device_id=(px, py, pc), device_id_type=pl.DeviceIdType.MESH))
    return copies


_HBM = pl.BlockSpec(memory_space=pltpu.HBM)
_SEM = pl.BlockSpec(memory_space=pltpu.SEMAPHORE)
_EFFECT = pltpu.SideEffectType.DATAFLOW_SIDE_EFFECTING


def _exchange_start(name, src, gather):
    shape = src.shape if gather else src.shape[1:]
    land = lax.empty((N_DEV,) + tuple(shape), src.dtype)

    def body(src_ref, land_ref, send_sems, recv_sems, src_thru, land_thru, token):
        for cp in _peer_copies(src_ref, land_ref, send_sems, recv_sems, gather):
            cp.start()
        token[...] = jnp.zeros_like(token)

    return pl.pallas_call(
        body, name=name,
        out_shape=(pltpu.SemaphoreType.DMA((N_DEV - 1,)), pltpu.SemaphoreType.DMA((N_DEV - 1,)),
                   pltpu.HBM(src.shape, src.dtype), pltpu.HBM(land.shape, land.dtype), _sds((SUBLANE, LANE), F32)),
        in_specs=(_HBM, _HBM), out_specs=(_SEM, _SEM, _HBM, _HBM, pl.BlockSpec(memory_space=pltpu.VMEM)),
        input_output_aliases={0: 2, 1: 3},
        compiler_params=pltpu.CompilerParams(has_side_effects=_EFFECT),
    )(pltpu.with_memory_space_constraint(src, pltpu.HBM), pltpu.with_memory_space_constraint(land, pltpu.HBM))


def _exchange_wait(name, started, gather, after):
    send_sems, recv_sems, src_thru, land_thru, _ = started

    def body(src_ref, land_ref, send_sems, recv_sems, after_ref, src_out, land_out):
        copies = _peer_copies(src_ref, land_ref, send_sems, recv_sems, gather)
        for cp in copies:
            cp.wait_send()
        for cp in copies:
            cp.wait_recv()

    return pl.pallas_call(
        body, name=name,
        out_shape=(pltpu.HBM(src_thru.shape, src_thru.dtype), pltpu.HBM(land_thru.shape, land_thru.dtype)),
        in_specs=(_HBM, _HBM, _SEM, _SEM, pl.BlockSpec(memory_space=pl.ANY)), out_specs=(_HBM, _HBM),
        input_output_aliases={0: 0, 1: 1},
        compiler_params=pltpu.CompilerParams(has_side_effects=_EFFECT),
    )(src_thru, land_thru, send_sems, recv_sems, after)


def _adam_math(w, gsum, m, v):
    m2 = ADAM_B1 * m + (1.0 - ADAM_B1) * gsum
    v2 = ADAM_B2 * v + (1.0 - ADAM_B2) * (gsum * gsum)
    m_hat = m2 / (1.0 - ADAM_B1 ** ADAM_STEP)
    v_hat = v2 / (1.0 - ADAM_B2 ** ADAM_STEP)
    delta = -ADAM_LR * (m_hat / (jnp.sqrt(v_hat) + ADAM_EPS) + ADAM_WD * w)
    return delta, m2, v2


def _reduce_adam(name, parts, w, m, v):
    nl, r, c = w.shape
    tr = _tile(r, 256, 16)
    nr = r // tr

    def body(*refs):
        p_refs = refs[:nl]
        w_ref, m_ref, v_ref, g_ref, d_ref, m2_ref, v2_ref = refs[nl:]
        for li in range(nl):
            @pl.when(pl.program_id(0) == li)
            def _(p_ref=p_refs[li]):
                gsum = p_ref[0].astype(F32)
                for s in range(1, N_DEV):
                    gsum = gsum + p_ref[s].astype(F32)
                g_ref[...] = gsum
                delta, m2, v2 = _adam_math(w_ref[...], gsum, m_ref[...], v_ref[...])
                d_ref[...] = delta
                m2_ref[...] = m2
                v2_ref[...] = v2

    def part_spec(li):
        return pl.BlockSpec((N_DEV, tr, c), lambda lay, i: (0, jnp.where(lay == li, i, jnp.where(lay < li, 0, nr - 1)), 0))

    row = pl.BlockSpec((None, tr, c), lambda lay, i: (lay, i, 0))
    return pl.pallas_call(
        body, name=name, grid=(nl, nr),
        in_specs=[part_spec(li) for li in range(nl)] + [row, row, row],
        out_specs=[row, row, row, row], out_shape=[_sds((nl, r, c), F32)] * 4,
        compiler_params=_params(("arbitrary", "arbitrary")),
    )(*parts, w, m, v)


def _reduce8(name, parts):
    _, r, c = parts.shape

    def body(p_ref, g_ref):
        gsum = p_ref[0]
        for s in range(1, N_DEV):
            gsum = gsum + p_ref[s]
        g_ref[...] = gsum

    return pl.pallas_call(
        body, name=name, grid=(1,),
        in_specs=[pl.BlockSpec((N_DEV, r, c), lambda i: (0, 0, 0))],
        out_specs=pl.BlockSpec((r, c), lambda i: (0, 0)), out_shape=_sds((r, c), F32),
        compiler_params=_params(("arbitrary",)),
    )(parts)


def _adam_packed(name, w, g, m, v):
    r, c = w.shape

    def body(w_ref, g_ref, m_ref, v_ref, d_ref, m2_ref, v2_ref):
        delta, m2, v2 = _adam_math(w_ref[...], g_ref[...], m_ref[...], v_ref[...])
        d_ref[...] = delta
        m2_ref[...] = m2
        v2_ref[...] = v2

    spec = pl.BlockSpec((r, c), lambda i: (0, 0))
    return pl.pallas_call(
        body, name=name, grid=(1,), in_specs=[spec] * 4, out_specs=[spec] * 3, out_shape=[_sds((r, c), F32)] * 3,
        compiler_params=_params(("arbitrary",)),
    )(w, g, m, v)


def _pack(arrs):
    flat = jnp.concatenate([a.reshape(-1).astype(F32) for a in arrs])
    n = flat.shape[0]
    pad = (-n) % (SUBLANE * LANE)
    return jnp.pad(flat, (0, pad)).reshape(-1, LANE)


def _unpack(packed, like):
    flat = packed.reshape(-1)
    out, off = [], 0
    for a in like:
        n = math.prod(a.shape)
        out.append(flat[off:off + n].reshape(a.shape))
        off += n
    return out


def _row_blocks(full):
    return full.reshape(N_DEV, full.shape[0] // N_DEV, full.shape[1])


def _col_blocks(full):
    r, c = full.shape
    return full.reshape(r, N_DEV, c // N_DEV).transpose(1, 0, 2)


def _cols_natural(blocks):
    n, r, c = blocks.shape
    return blocks.transpose(1, 0, 2).reshape(r, n * c)


def kernel(x, ffn1_norm, ffn1_w1, ffn1_w2, mix_norm, ffn2_norm, ffn2_w1, ffn2_w2, ab_w_in, s5_lambda_re, s5_lambda_im, s5_b_re, s5_b_im, s5_c_re, s5_c_im, s5_log_dt, s5_d, s5_w_glu, gla_w_gk, gla_b_gk, gla_norm, ab_w_out, ret_w_in, ret_norm, ret_w_out, final_norm, loss_target, m_ffn1_norm, m_ffn1_w1, m_ffn1_w2, m_mix_norm, m_ffn2_norm, m_ffn2_w1, m_ffn2_w2, m_ab_w_in, m_s5_lambda_re, m_s5_lambda_im, m_s5_b_re, m_s5_b_im, m_s5_c_re, m_s5_c_im, m_s5_log_dt, m_s5_d, m_s5_w_glu, m_gla_w_gk, m_gla_b_gk, m_gla_norm, m_ab_w_out, m_ret_w_in, m_ret_norm, m_ret_w_out, m_final_norm, v_ffn1_norm, v_ffn1_w1, v_ffn1_w2, v_mix_norm, v_ffn2_norm, v_ffn2_w1, v_ffn2_w2, v_ab_w_in, v_s5_lambda_re, v_s5_lambda_im, v_s5_b_re, v_s5_b_im, v_s5_c_re, v_s5_c_im, v_s5_log_dt, v_s5_d, v_s5_w_glu, v_gla_w_gk, v_gla_b_gk, v_gla_norm, v_ab_w_out, v_ret_w_in, v_ret_norm, v_ret_w_out, v_final_norm):
    names = ['ffn1_norm', 'ffn1_w1', 'ffn1_w2', 'mix_norm', 'ffn2_norm', 'ffn2_w1', 'ffn2_w2', 'ab_w_in', 's5_lambda_re', 's5_lambda_im', 's5_b_re', 's5_b_im', 's5_c_re', 's5_c_im', 's5_log_dt', 's5_d', 's5_w_glu', 'gla_w_gk', 'gla_b_gk', 'gla_norm', 'ab_w_out', 'ret_w_in', 'ret_norm', 'ret_w_out', 'final_norm']
    loc = locals()
    W = {n: loc[n] for n in names}
    M = {n: loc["m_" + n] for n in names}
    V = {n: loc["v_" + n] for n in names}

    me = 4 * lax.axis_index("x") + 2 * lax.axis_index("y") + lax.axis_index("c")
    xs = x[0]
    tgt = loss_target[0]
    l, d = xs.shape
    depth = ffn1_norm.shape[0]

    pending, tokens = {}, []

    def start_gather(tag, shard):
        started = _exchange_start("ags_" + tag, shard, True)
        pending[tag] = (started, shard)
        tokens.append(started[4][0, 0])

    def finish_gather(tag, after):
        started, shard = pending.pop(tag)
        _, got = _exchange_wait("agw_" + tag, started, True, after)
        return lax.dynamic_update_index_in_dim(got, shard, me, 0)

    def finish_cols(tag, after):
        g = finish_gather(tag, after)
        return g.transpose(1, 0, 2).reshape(g.shape[1], -1)

    def finish_rows(tag, after):
        g = finish_gather(tag, after)
        return g.reshape(-1, g.shape[2])

    small_sharded = [gla_w_gk, gla_b_gk, ret_norm]
    for i in range(depth):
        j = i // 2
        start_gather(f"ffn1_w1_{i}", ffn1_w1[i].astype(BF))
        start_gather(f"ffn1_w2_{i}", ffn1_w2[i].astype(BF))
        if i % 2 == 0:
            start_gather(f"ab_w_in_{j}", ab_w_in[j].astype(BF))
            if i == 0:
                start_gather("small", _pack(small_sharded))
            start_gather(f"s5_w_glu_{j}", s5_w_glu[j].astype(BF))
            start_gather(f"ab_w_out_{j}", ab_w_out[j].astype(BF))
        else:
            start_gather(f"ret_w_in_{j}", ret_w_in[j].astype(BF))
            start_gather(f"ret_w_out_{j}", ret_w_out[j].astype(BF))
        start_gather(f"ffn2_w1_{i}", ffn2_w1[i].astype(BF))
        start_gather(f"ffn2_w2_{i}", ffn2_w2[i].astype(BF))
    started_all = functools.reduce(lambda a, b: a + b, tokens)
    full = {}

    s5w = s5_d.shape[1]
    g_s5, n_s5 = s5_lambda_re.shape[2], s5_lambda_re.shape[3]
    hs = min(SUBLANE * LANE, g_s5 * n_s5)
    gla_hk = gla_w_gk.shape[-1] * N_DEV
    gla_dk = gla_hk // GLA_HEADS
    gla_hv = gla_norm.shape[1]
    gla_dv = gla_hv // GLA_HEADS
    ret_hv = ret_norm.shape[1] * N_DEV
    ret_dv = ret_hv // RET_HEADS
    ret_hk = (ret_w_in.shape[2] * N_DEV - 2 * ret_hv) // 2
    ret_dk = ret_hk // RET_HEADS
    assert s5w == gla_hv and 2 * gla_hk == s5w, "column blocks of the mixer projection assume these widths"
    assert ret_hv == 2 * ret_hk
    main_w = s5w + 2 * gla_hk + 2 * gla_hv
    gla_tb = _tile(l, 256, GLA_CHUNK)
    ret_chunk = min(RET_CHUNK, l)

    cos_t, sin_t = _rot_tables(l, ret_dk)
    lg_f = jnp.log1p(-jnp.exp2(-5.0 - jnp.arange(RET_HEADS, dtype=F32)))
    lgtab_f = jnp.broadcast_to(lg_f[:, None, None], (RET_HEADS, 1, LANE))
    lgtab_b = jnp.broadcast_to(lg_f[::-1][:, None, None], (RET_HEADS, 1, LANE))
    s5_pre = {}
    for j in range((depth + 1) // 2):
        s5_args = (s5_lambda_re[j], s5_lambda_im[j], s5_b_re[j], s5_b_im[j], s5_c_re[j], s5_c_im[j], s5_log_dt[j])
        (a_tab, bd, cd), s5_vjp = jax.vjp(lambda *a: _s5_chunk_tables(*a, hs), *s5_args)
        s5_pre[j] = (a_tab, bd.astype(BF), cd.astype(BF), s5_vjp)
    tables_done = jnp.stack([cos_t[0, 0], sin_t[0, 0], lgtab_f[0, 0, 0], lgtab_b[0, 0, 0]]
                            + [t[0].reshape(-1)[0] + t[1].reshape(-1)[0].astype(F32) + t[2].reshape(-1)[0].astype(F32) for t in s5_pre.values()])

    saved = []
    cur = xs
    for i in range(depth):
        j = i // 2
        s = {}
        s['x0'] = cur
        s['f1w1'] = finish_cols(f"ffn1_w1_{i}", tables_done if i == 0 else cur)
        g1 = ffn1_norm[i:i + 1] + started_all if i == 0 else ffn1_norm[i:i + 1]
        cur, s['ffn1'], s['f1w2'] = _ffn_fwd(f"l{i}_ffn1", cur, g1, s['f1w1'], lambda after: finish_rows(f"ffn1_w2_{i}", after))
        s['x1'] = cur
        h = _rms_fwd(f"l{i}_mixnorm", cur, mix_norm[i:i + 1])
        s['h'] = h
        if i % 2 == 0:
            w_in = finish_cols(f"ab_w_in_{j}", cur)
            if i == 0:
                got = finish_gather("small", cur)
                per_dev = [_unpack(got[p], small_sharded) for p in range(N_DEV)]
                full['gla_w_gk'] = jnp.concatenate([t[0] for t in per_dev], axis=-1).astype(BF)
                full['gla_b_gk'] = jnp.concatenate([t[1] for t in per_dev], axis=-1)
                ret_norm_full = jnp.concatenate([t[2] for t in per_dev], axis=-1)
            s['w_glu'], s['w_out'] = finish_rows(f"s5_w_glu_{j}", cur), finish_rows(f"ab_w_out_{j}", cur)
            w_main, w_glo = w_in[:, :main_w], w_in[:, main_w:]
            proj = _mm_plain(f"l{i}_proj", h, w_main, NN, BF, tm=512, tn=1024, tk=d, b_outer=True)
            glo = _mm_plain(f"l{i}_glo", h, w_glo, NN, F32, tm=1024, tn=2 * GLA_RANK, tk=d)
            a_tab, bd16, cd16, s5_vjp = s5_pre[j]
            tm = _tile(l, 512, SUBLANE)
            x_f, y_f = _s5_fwd(f"l{i}_s5_fwd_f", proj, bd16[0], cd16[0], a_tab[0], False)
            x_b, y_b = _s5_fwd(f"l{i}_s5_fwd_b", proj, bd16[1], cd16[1], a_tab[1], True)
            d_row = s5_d[j:j + 1]
            y, s5_out = _glu_fwd(f"l{i}_s5_glu", y_f, y_b, proj, d_row, s['w_glu'])
            zeros_r = jnp.zeros((GLA_RANK, gla_hk), BF)
            w_gk = full['gla_w_gk'][j]
            wgk_f = jnp.concatenate([w_gk[0], zeros_r], axis=0)
            wgk_b = jnp.concatenate([zeros_r, w_gk[1]], axis=0)
            b_gk = full['gla_b_gk'][j]
            g_f, g_b = _gate_fwd(f"l{i}_gla_gate", glo, wgk_f, wgk_b, b_gk[0:1], b_gk[1:2])
            qcol, kcol, vcol, ogcol = s5w // gla_hk, s5w // gla_hk + 1, (s5w + 2 * gla_hk) // gla_hv, (s5w + 2 * gla_hk) // gla_hv + 1
            lin_kw = dict(heads=GLA_HEADS, hb=GLA_HEADS, dk=gla_dk, dv=gla_dv, chunk=GLA_CHUNK, tb=gla_tb, qcol=qcol, kcol=kcol, vcol=vcol,
                          qscale=gla_dk ** -0.5)
            o_f, sp_f = _lin_fwd(f"l{i}_gla_fwd_f", proj, proj, proj, g_f, None, reverse=False, **lin_kw)
            o_b, sp_b = _lin_fwd(f"l{i}_gla_fwd_b", proj, proj, proj, g_b, None, reverse=True, **lin_kw)
            gla_out = _headgate_fwd(f"l{i}_gla_out", o_f, o_b, proj, ogcol, gla_norm[j:j + 1], gla_dv)
            w_out = s['w_out']

            def epi_res(acc, e_refs, o_refs, ids):
                o_refs[0][...] = e_refs[0][...] + acc

            row = pl.BlockSpec((tm, d), lambda ii, jj, kk: (ii, 0))
            cur = _mm(f"l{i}_mix_out",
                      [(s5_out, pl.BlockSpec((tm, s5w), lambda ii, jj, kk: (ii, 0)), w_out, pl.BlockSpec((s5w, d), lambda ii, jj, kk: (0, 0))),
                       (gla_out, pl.BlockSpec((tm, gla_hv), lambda ii, jj, kk: (ii, 0)), w_out, pl.BlockSpec((gla_hv, d), lambda ii, jj, kk: (1, 0)))],
                      NN, (l // tm, 1, 1), [_sds((l, d), F32)], [row], (tm, d), epi_res, [cur], [row])[0]
            s.update(proj=proj, glo=glo, s5_vjp=s5_vjp, a_tab=a_tab, bd16=bd16, cd16=cd16, x_f=x_f, x_b=x_b, y=y, s5_out=s5_out,
                     wgk_f=wgk_f, wgk_b=wgk_b, b_gk=b_gk, g_f=g_f, g_b=g_b, o_f=o_f, o_b=o_b, sp_f=sp_f, sp_b=sp_b, gla_out=gla_out,
                     w_main=w_main, w_glo=w_glo, lin_kw=lin_kw, ogcol=ogcol)
        else:
            w_in = finish_cols(f"ret_w_in_{j}", cur)
            s['w_in'], s['w_out'] = w_in, finish_rows(f"ret_w_out_{j}", cur)
            proj = _mm_plain(f"l{i}_proj", h, w_in, NN, BF, tm=512, tn=1024, tk=d, b_outer=True)
            qr, kr = _rot_apply(f"l{i}_rot", proj, 0, proj, 1, cos_t, sin_t, RET_HEADS, ret_dk, ret_dk ** -0.5, BF, False)
            ret_hb = 4
            lin_kw = dict(heads=RET_HEADS, hb=ret_hb, dk=ret_dk, dv=ret_dv, chunk=ret_chunk, tb=ret_chunk, qcol=0, kcol=0,
                          vcol=(2 * ret_hk) // (ret_hb * ret_dv),
                          qscale=1.0)
            o_f, sp_f = _lin_fwd(f"l{i}_ret_fwd_f", qr, kr, proj, None, lgtab_f, reverse=False, **lin_kw)
            o_b, sp_b = _lin_fwd(f"l{i}_ret_fwd_b", qr, kr, proj, None, lgtab_b, reverse=True, **lin_kw)
            ogcol = (2 * ret_hk + ret_hv) // ret_hv
            r_out = _headgate_fwd(f"l{i}_ret_out", o_f, o_b, proj, ogcol, ret_norm_full, ret_dv)

            def epi_res(acc, e_refs, o_refs, ids):
                o_refs[0][...] = e_refs[0][...] + acc

            cur = _mm_plain(f"l{i}_mix_out", r_out, s['w_out'], NN, F32, tm=512, tn=d, tk=ret_hv, epi=epi_res, eins=[cur])
            s.update(proj=proj, qr=qr, kr=kr, o_f=o_f, o_b=o_b, sp_f=sp_f, sp_b=sp_b, r_out=r_out, lin_kw=lin_kw, ogcol=ogcol)
        s['x2'] = cur
        s['f2w1'] = finish_cols(f"ffn2_w1_{i}", cur)
        cur, s['ffn2'], s['f2w2'] = _ffn_fwd(f"l{i}_ffn2", cur, ffn2_norm[i:i + 1], s['f2w1'], lambda after: finish_rows(f"ffn2_w2_{i}", after))
        saved.append(s)

    dx, d_final_norm, loss_row = _loss_head("loss_head", cur, final_norm.reshape(1, -1), tgt)
    loss = lax.psum(loss_row[0, 0], ("x", "y", "c"))

    G = {}
    big = {}
    G['final_norm'] = d_final_norm.reshape(-1)
    per_layer = {n: [None] * depth for n in ['ffn1_norm', 'mix_norm', 'ffn2_norm']}
    a2a, tok = {}, [jnp.zeros((), F32)]

    def start_a2a(tag, blocks):
        started = _exchange_start("a2as_" + tag, blocks, False)
        a2a[tag] = started
        tok[0] = tok[0] + started[4][0, 0]

    def dep(vec):
        return vec + tok[0]

    for i in reversed(range(depth)):
        j = i // 2
        s = saved[i]
        def ffn_grads(which):
            def on_grads(dw1, dw2, gnorm):
                start_a2a(f"{which}_w1_{i}", _col_blocks(dw1))
                start_a2a(f"{which}_w2_{i}", _row_blocks(dw2))
                return dep(gnorm)
            return on_grads

        dx, dg = _ffn_bwd(f"l{i}_ffn2b", dx, s['x2'], ffn2_norm[i:i + 1], s['f2w1'], s['f2w2'], s['ffn2'], ffn_grads("ffn2"))
        per_layer['ffn2_norm'][i] = dg[0]
        tm = _tile(l, 512, SUBLANE)
        row = pl.BlockSpec((tm, d), lambda ii, jj, kk: (ii, 0))
        vec = pl.BlockSpec((1, d), lambda ii, jj, kk: (0, 0))
        if i % 2 == 0:
            proj, lin_kw = s['proj'], s['lin_kw']
            w_out = s['w_out']
            d_cat = _mm_plain(f"l{i}_dcat", dx, w_out, NT, BF, tm=512, tn=1024, tk=d)
            dwo_a = _mm_plain(f"l{i}_dwout_a", s['s5_out'], dx, TN, BF, tm=s5w, tn=d, tk=512)
            dwo_b = _mm_plain(f"l{i}_dwout_b", s['gla_out'], dx, TN, BF, tm=gla_hv, tn=d, tk=512)
            start_a2a(f"ab_w_out_{j}", _row_blocks(jnp.concatenate([dwo_a, dwo_b], axis=0)))
            do, dog, dgn = _headgate_bwd(f"l{i}_gla_outb", s['o_f'], s['o_b'], proj, s['ogcol'], dep(gla_norm[j:j + 1]), d_cat, 1, gla_dv)
            G['gla_norm'] = dgn
            r1 = _lin_bwd(f"l{i}_gla_bwd_f", proj, proj, proj, s['g_f'], None, s['sp_f'], do, None, reverse=False, **lin_kw)
            dq, dk_, dv_, dgf = r1
            r2 = _lin_bwd(f"l{i}_gla_bwd_b", proj, proj, proj, s['g_b'], None, s['sp_b'], do, (dq, dk_, dv_), reverse=True, **lin_kw)
            dq, dk_, dv_, dgb = r2
            dglo, dwf, dwb, dbf, dbb = _gate_bwd(f"l{i}_gla_gateb", s['glo'], s['wgk_f'], s['wgk_b'], s['b_gk'][0:1], s['b_gk'][1:2], dgf, dgb)
            G['gla_w_gk'] = jnp.stack([dwf[:GLA_RANK], dwb[GLA_RANK:]], axis=0)[None]
            G['gla_b_gk'] = jnp.concatenate([dbf, dbb], axis=0)[None]
            dy, dwglu = _glu_bwd(f"l{i}_s5_glub", s['y'], s['w_glu'], d_cat, 0)
            start_a2a(f"s5_w_glu_{j}", _row_blocks(dwglu.astype(BF)))
            cd16, bd16, a_tab = s['cd16'], s['bd16'], s['a_tab']
            nt2 = a_tab.shape[2]
            a_conj = a_tab * jnp.where(jnp.arange(nt2) < nt2 // 2, 1.0, -1.0)[None, None, :, None]
            du_f, dbd_f, dcd_f, da_f = _s5_bwd(f"l{i}_s5_bwd_f", proj, dy, s['x_f'], bd16[0], cd16[0], a_conj[0], True)
            du_b, dbd_b, dcd_b, da_b = _s5_bwd(f"l{i}_s5_bwd_b", proj, dy, s['x_b'], bd16[1], cd16[1], a_conj[1], False)
            du, dd = _s5_du(f"l{i}_s5_du", du_f, du_b, dy, proj, s5_d[j:j + 1])
            G['s5_d'] = dd
            cot = (jnp.stack([da_f, da_b]), jnp.stack([dbd_f, dbd_b]), jnp.stack([dcd_f, dcd_b]))
            g_lre, g_lim, g_bre, g_bim, g_cre, g_cim, g_ldt = s['s5_vjp'](cot)
            G['s5_lambda_re'], G['s5_lambda_im'], G['s5_b_re'], G['s5_b_im'] = g_lre[None], g_lim[None], g_bre[None], g_bim[None]
            G['s5_c_re'], G['s5_c_im'], G['s5_log_dt'] = g_cre[None], g_cim[None], g_ldt[None]
            dproj = jnp.concatenate([du, dq.astype(BF), dk_.astype(BF), dv_.astype(BF), dog], axis=1)
            r2w = 2 * GLA_RANK
            pairs = [(dproj, pl.BlockSpec((tm, main_w), lambda ii, jj, kk: (ii, 0)), s['w_main'], pl.BlockSpec((d, main_w), lambda ii, jj, kk: (0, 0))),
                     (dglo, pl.BlockSpec((tm, r2w), lambda ii, jj, kk: (ii, 0)), s['w_glo'], pl.BlockSpec((d, r2w), lambda ii, jj, kk: (0, 0)))]
            dx, dg = _mm(f"l{i}_dh", pairs, NT, (l // tm, 1, 1), [_sds((l, d), F32), _sds((1, d), F32)], [row, vec], (tm, d),
                         _rms_bwd_epi(0), [s['x1'], dep(mix_norm[i:i + 1]), dx], [row, vec, row])
            dw_main = _mm_plain(f"l{i}_dwin_main", s['h'], dproj, TN, BF, tm=d, tn=2048, tk=1024)
            dw_glo = _mm_plain(f"l{i}_dwin_glo", s['h'], dglo, TN, BF, tm=d, tn=r2w, tk=512)
            start_a2a(f"ab_w_in_{j}", _col_blocks(jnp.concatenate([dw_main, dw_glo], axis=1)))
        else:
            proj, lin_kw = s['proj'], s['lin_kw']
            w_out = s['w_out']
            d_ro = _mm_plain(f"l{i}_dro", dx, w_out, NT, BF, tm=512, tn=1024, tk=d, b_outer=True)
            dwo = _mm_plain(f"l{i}_dwout", s['r_out'], dx, TN, BF, tm=2048, tn=d, tk=1024)
            start_a2a(f"ret_w_out_{j}", _row_blocks(dwo))
            do, dog, dgn = _headgate_bwd(f"l{i}_ret_outb", s['o_f'], s['o_b'], proj, s['ogcol'], dep(ret_norm_full), d_ro, 0, ret_dv)
            G['ret_norm'] = dgn
            r1 = _lin_bwd(f"l{i}_ret_bwd_f", s['qr'], s['kr'], proj, None, lgtab_f, s['sp_f'], do, None, reverse=False, **lin_kw)
            r2 = _lin_bwd(f"l{i}_ret_bwd_b", s['qr'], s['kr'], proj, None, lgtab_b, s['sp_b'], do, r1, reverse=True, **lin_kw)
            dqr, dkr, dv_ = r2
            dq, dk_ = _rot_apply(f"l{i}_rotb", dqr, 0, dkr, 0, cos_t, sin_t, RET_HEADS, ret_dk, ret_dk ** -0.5, BF, True)
            dproj = jnp.concatenate([dq, dk_, dv_.astype(BF), dog], axis=1)
            dw_in = _mm_plain(f"l{i}_dwin", s['h'], dproj, TN, BF, tm=d, tn=2048, tk=1024)
            start_a2a(f"ret_w_in_{j}", _col_blocks(dw_in))
            ret_in = dproj.shape[1]
            dx, dg = _mm(f"l{i}_dh",
                         [(dproj, pl.BlockSpec((tm, ret_in), lambda ii, jj, kk: (ii, 0)),
                           s['w_in'], pl.BlockSpec((d, ret_in), lambda ii, jj, kk: (0, 0), pipeline_mode=pl.Buffered(1)))],
                         NT, (l // tm, 1, 1), [_sds((l, d), F32), _sds((1, d), F32)], [row, vec], (tm, d),
                         _rms_bwd_epi(0), [s['x1'], dep(mix_norm[i:i + 1]), dx], [row, vec, row])
        per_layer['mix_norm'][i] = dg[0]
        dx, dg = _ffn_bwd(f"l{i}_ffn1b", dx, s['x0'], ffn1_norm[i:i + 1], s['f1w1'], s['f1w2'], s['ffn1'], ffn_grads("ffn1"))
        per_layer['ffn1_norm'][i] = dg[0]
    for n in per_layer:
        G[n] = jnp.stack(per_layer[n], axis=0)
    grad_x = dx[None]

    out_g, out_d, out_m, out_v = {}, {}, {}, {}
    small = ['ffn1_norm', 'mix_norm', 'ffn2_norm', 's5_lambda_re', 's5_lambda_im', 's5_b_re', 's5_b_im', 's5_c_re', 's5_c_im',
             's5_log_dt', 's5_d', 'gla_w_gk', 'gla_b_gk', 'gla_norm', 'ret_norm', 'final_norm']
    packed = _pack([G[n] for n in small])
    small_started = _exchange_start("ags_small_grads", packed, True)

    def big_update(n, layers):
        parts = []
        for i in layers:
            blocks, got = _exchange_wait(f"a2aw_{n}_{i}", a2a.pop(f"{n}_{i}"), False, small_started[4])
            parts.append(lax.dynamic_update_index_in_dim(got, lax.dynamic_index_in_dim(blocks, me, 0, keepdims=False), me, 0))
        out_g[n], out_d[n], out_m[n], out_v[n] = _reduce_adam("upd_" + n, parts, W[n], M[n], V[n])

    for n in ['ffn2_w1', 'ffn2_w2', 'ffn1_w1', 'ffn1_w2']:
        big_update(n, range(depth))
    for n in ['ab_w_in', 's5_w_glu', 'ab_w_out', 'ret_w_in', 'ret_w_out']:
        big_update(n, [0])
    assert not a2a and not pending

    _, gathered = _exchange_wait("agw_small_grads", small_started, True, out_v['ret_w_out'])
    gathered = lax.dynamic_update_index_in_dim(gathered, packed, me, 0)
    summed = _reduce8("sum_small_grads", gathered)
    g_full = dict(zip(small, _unpack(summed, [G[n] for n in small])))
    g_small = {}
    for n in small:
        gf = g_full[n]
        if n in ('gla_w_gk', 'gla_b_gk', 'ret_norm'):
            width = W[n].shape[-1]
            gf = lax.dynamic_slice_in_dim(gf, me * width, width, axis=gf.ndim - 1)
        g_small[n] = gf.reshape(W[n].shape)
    pw, pg, pm, pv = (_pack([src[n] for n in small]) for src in (W, g_small, M, V))
    pd, pm2, pv2 = _adam_packed("upd_small", pw, pg, pm, pv)
    like = [W[n] for n in small]
    for n, dd_, mm_, vv_ in zip(small, _unpack(pd, like), _unpack(pm2, like), _unpack(pv2, like)):
        out_g[n], out_d[n], out_m[n], out_v[n] = g_small[n], dd_, mm_, vv_

    return (loss, grad_x, *[out_g[n] for n in names], *[out_d[n] for n in names], *[out_m[n] for n in names], *[out_v[n] for n in names])
```

```python
import functools
import math

import jax
import jax.numpy as jnp
from jax import lax
from jax.experimental import pallas as pl
from jax.experimental.pallas import tpu as pltpu

F32 = jnp.float32
BF = jnp.bfloat16
N_DEV = 8
EPS = 1e-6
S5_GROUP = 16
GLA_HEADS = 4
GLA_RANK = 16
GLA_GATE_NORM = 16.0
RET_HEADS = 8
ROPE_BASE = 10000.0
GLA_CHUNK = 64
RET_CHUNK = 256
ADAM_LR, ADAM_B1, ADAM_B2, ADAM_EPS, ADAM_WD, ADAM_STEP = 0.001, 0.9, 0.999, 1e-08, 0.01, 10
VMEM_LIMIT_BYTES = 56 * 1024 * 1024
LANE = 128
SUBLANE = 8

NN = (((1,), (0,)), ((), ()))
NT = (((1,), (1,)), ((), ()))
TN = (((0,), (0,)), ((), ()))


def _tile(n, pref, align):
    if n <= pref:
        return n
    t = (pref // align) * align
    while t >= align:
        if n % t == 0:
            return t
        t -= align
    return n


def _params(sem):
    return pltpu.CompilerParams(dimension_semantics=sem, vmem_limit_bytes=VMEM_LIMIT_BYTES)


def _dot(a, b, dims=NN):
    return lax.dot_general(a.astype(BF), b.astype(BF), dims, preferred_element_type=F32)


def _dot3(m01, g, dims=NN):
    g1 = g.astype(BF)
    r1 = g - g1.astype(F32)
    g2 = r1.astype(BF)
    g3 = (r1 - g2.astype(F32)).astype(BF)
    m = m01.astype(BF)
    return (lax.dot_general(m, g1, dims, preferred_element_type=F32)
            + lax.dot_general(m, g2, dims, preferred_element_type=F32)
            + lax.dot_general(m, g3, dims, preferred_element_type=F32))


def _sigmoid(x):
    return 1.0 / (1.0 + jnp.exp(-x))


def _mm(name, pairs, dims, grid, outs, out_specs, acc_shape, epi=None, eins=(), especs=()):
    n_p, n_e, n_o = len(pairs), len(eins), len(outs)
    nk = grid[2]

    def body(*refs):
        a_refs = refs[0:2 * n_p:2]
        b_refs = refs[1:2 * n_p:2]
        e_refs = refs[2 * n_p:2 * n_p + n_e]
        o_refs = refs[2 * n_p + n_e:2 * n_p + n_e + n_o]
        acc = refs[-1]
        ids = (pl.program_id(0), pl.program_id(1), pl.program_id(2))

        part = _dot(a_refs[0][...], b_refs[0][...], dims)
        for p in range(1, n_p):
            part = part + _dot(a_refs[p][...], b_refs[p][...], dims)

        def finish(total):
            if epi is None:
                o_refs[0][...] = total.astype(o_refs[0].dtype)
            else:
                epi(total, e_refs, o_refs, ids)

        if nk == 1:
            finish(part)
        else:
            @pl.when(ids[2] == 0)
            def _():
                acc[...] = part

            @pl.when(ids[2] > 0)
            def _():
                acc[...] += part

            @pl.when(ids[2] == nk - 1)
            def _():
                finish(acc[...])

    in_specs, args = [], []
    for a, a_spec, b, b_spec in pairs:
        in_specs += [a_spec, b_spec]
        args += [a, b]
    in_specs += list(especs)
    args += list(eins)
    res = pl.pallas_call(
        body, name=name, grid=grid, in_specs=in_specs, out_specs=list(out_specs), out_shape=list(outs),
        scratch_shapes=[pltpu.VMEM(acc_shape, F32)],
        compiler_params=_params(("arbitrary", "arbitrary", "arbitrary")),
    )(*args)
    return res


def _sds(shape, dtype):
    return jax.ShapeDtypeStruct(shape, dtype)


def _mm_plain(name, a, b, dims, out_dtype, tm=512, tn=1024, tk=1024, epi=None, eins=(), especs=None, extra_outs=(), extra_specs=(),
              b_outer=False):
    if dims == NN:
        (m, k), n = a.shape, b.shape[1]
    elif dims == NT:
        (m, k), n = a.shape, b.shape[0]
    else:
        (k, m), n = a.shape, b.shape[1]
    tm, tn = _tile(m, tm, LANE if dims == TN else SUBLANE), _tile(n, tn, LANE)
    tk = _tile(k, tk, SUBLANE if dims == TN else LANE)
    grid = (n // tn, m // tm, k // tk) if b_outer else (m // tm, n // tn, k // tk)

    def spec(block, index):
        if b_outer:
            return pl.BlockSpec(block, lambda j, i, kk: index(i, j, kk))
        return pl.BlockSpec(block, index)

    if dims == NN:
        a_spec = spec((tm, tk), lambda i, j, kk: (i, kk))
        b_spec = spec((tk, tn), lambda i, j, kk: (kk, j))
    elif dims == NT:
        a_spec = spec((tm, tk), lambda i, j, kk: (i, kk))
        b_spec = spec((tn, tk), lambda i, j, kk: (j, kk))
    else:
        a_spec = spec((tk, tm), lambda i, j, kk: (kk, i))
        b_spec = spec((tk, tn), lambda i, j, kk: (kk, j))
    o_spec = spec((tm, tn), lambda i, j, kk: (i, j))
    if especs is None:
        especs = [o_spec] * len(eins)
    else:
        especs = [o_spec if s is None else s for s in especs]
    res = _mm(name, [(a, a_spec, b, b_spec)], dims, grid, [_sds((m, n), out_dtype)] + list(extra_outs),
              [o_spec] + list(extra_specs), (tm, tn), epi, eins, especs)
    return res if extra_outs else res[0]


def _rms_fwd(name, x, g):
    l, d = x.shape
    tm = _tile(l, 1024, SUBLANE)

    def body(x_ref, g_ref, o_ref):
        xv = x_ref[...]
        r = lax.rsqrt(jnp.mean(xv * xv, axis=-1, keepdims=True) + EPS)
        o_ref[...] = (xv * r * g_ref[...]).astype(o_ref.dtype)

    return pl.pallas_call(
        body, name=name, grid=(l // tm,),
        in_specs=[pl.BlockSpec((tm, d), lambda i: (i, 0)), pl.BlockSpec((1, d), lambda i: (0, 0))],
        out_specs=pl.BlockSpec((tm, d), lambda i: (i, 0)), out_shape=_sds((l, d), BF),
        compiler_params=_params(("arbitrary",)),
    )(x, g)


def _rms_bwd_epi(first_axis):
    def epi(acc, e_refs, o_refs, ids):
        x_ref, g_ref, dr_ref = e_refs
        dx_ref, dg_ref = o_refs
        xv = x_ref[...]
        r = lax.rsqrt(jnp.mean(xv * xv, axis=-1, keepdims=True) + EPS)
        xh = xv * r
        dxh = acc * g_ref[...]
        dx_ref[...] = dr_ref[...] + r * (dxh - xh * jnp.mean(dxh * xh, axis=-1, keepdims=True))
        part = jnp.sum(acc * xh, axis=0, keepdims=True)

        @pl.when(ids[first_axis] == 0)
        def _():
            dg_ref[...] = part

        @pl.when(ids[first_axis] > 0)
        def _():
            dg_ref[...] += part

    return epi


def _loss_head(name, x, g, target):
    l, d = x.shape
    tm = _tile(l, 512, SUBLANE)
    n = l // tm

    def body(x_ref, g_ref, t_ref, dx_ref, dg_ref, loss_ref, lacc):
        i = pl.program_id(0)
        xv = x_ref[...]
        r = lax.rsqrt(jnp.mean(xv * xv, axis=-1, keepdims=True) + EPS)
        xh = xv * r
        e = xh * g_ref[...] - t_ref[...]
        dy = e * (1.0 / d)
        dxh = dy * g_ref[...]
        dx_ref[...] = r * (dxh - xh * jnp.mean(dxh * xh, axis=-1, keepdims=True))
        dg_part = jnp.sum(dy * xh, axis=0, keepdims=True)
        l_part = jnp.sum(e * e, axis=0, keepdims=True)

        @pl.when(i == 0)
        def _():
            dg_ref[...] = dg_part
            lacc[...] = l_part

        @pl.when(i > 0)
        def _():
            dg_ref[...] += dg_part
            lacc[...] += l_part

        @pl.when(i == n - 1)
        def _():
            loss_ref[...] = jnp.zeros_like(loss_ref) + jnp.sum(lacc[...]) * (0.5 / d)

    return pl.pallas_call(
        body, name=name, grid=(n,),
        in_specs=[pl.BlockSpec((tm, d), lambda i: (i, 0)), pl.BlockSpec((1, d), lambda i: (0, 0)),
                  pl.BlockSpec((tm, d), lambda i: (i, 0))],
        out_specs=[pl.BlockSpec((tm, d), lambda i: (i, 0)), pl.BlockSpec((1, d), lambda i: (0, 0)),
                   pl.BlockSpec((1, LANE), lambda i: (0, 0))],
        out_shape=[_sds((l, d), F32), _sds((1, d), F32), _sds((1, LANE), F32)],
        scratch_shapes=[pltpu.VMEM((1, d), F32)],
        compiler_params=_params(("arbitrary",)),
    )(x, g, target)


def _ffn_up(name, hn, w1):
    l, d = hn.shape
    f = w1.shape[1] // 2
    tm, tn = _tile(l, 512, SUBLANE), _tile(f, 1408, LANE)
    nj = f // tn

    def body(h_ref, wg_ref, wu_ref, gu_ref, a_ref):
        h = h_ref[...]
        g = jnp.dot(h, wg_ref[...], preferred_element_type=F32)
        u = jnp.dot(h, wu_ref[...], preferred_element_type=F32)
        s = _sigmoid(g)
        gs = g * s
        gu_ref[0] = (u * (s + gs * (1.0 - s))).astype(BF)
        gu_ref[1] = gs.astype(BF)
        a_ref[...] = (gs * u).astype(BF)

    return pl.pallas_call(
        body, name=name, grid=(nj, l // tm),
        in_specs=[pl.BlockSpec((tm, d), lambda j, i: (i, 0)), pl.BlockSpec((d, tn), lambda j, i: (0, j)),
                  pl.BlockSpec((d, tn), lambda j, i: (0, j + nj))],
        out_specs=[pl.BlockSpec((2, tm, tn), lambda j, i: (0, i, j)), pl.BlockSpec((tm, tn), lambda j, i: (i, j))],
        out_shape=[_sds((2, l, f), BF), _sds((l, f), BF)],
        compiler_params=_params(("arbitrary", "arbitrary")),
    )(hn, w1, w1)


def _ffn_fwd(tag, x, gnorm, w1, get_w2):
    hn = _rms_fwd(tag + "_norm", x, gnorm)
    gu, a = _ffn_up(tag + "_up", hn, w1)
    w2 = get_w2(a)

    def epi(acc, e_refs, o_refs, ids):
        o_refs[0][...] = e_refs[0][...] + 0.5 * acc

    l, d = x.shape
    x_new = _mm_plain(tag + "_down", a, w2, NN, F32, tm=512, tn=d, tk=w2.shape[0], epi=epi, eins=[x])
    return x_new, (hn, gu, a), w2


def _ffn_bwd(tag, dres, x, gnorm, w1, w2, saved, on_grads):
    hn, gu, a = saved
    l, d = x.shape
    f = w2.shape[0]
    tm, tn = _tile(l, 512, SUBLANE), _tile(f, 1408, LANE)
    nj = f // tn

    def epi_gu(acc, e_refs, o_refs, ids):
        da = 0.5 * acc
        o_refs[0][0] = (da * e_refs[0][0].astype(F32)).astype(BF)
        o_refs[0][1] = (da * e_refs[0][1].astype(F32)).astype(BF)

    gu_spec = pl.BlockSpec((2, tm, tn), lambda j, i, kk: (0, i, j))
    dgu = _mm(tag + "_dgu",
              [(dres, pl.BlockSpec((tm, d), lambda j, i, kk: (i, 0)), w2, pl.BlockSpec((tn, d), lambda j, i, kk: (j, 0)))],
              NT, (nj, l // tm, 1), [_sds((2, l, f), BF)], [gu_spec], (tm, tn), epi_gu, [gu], [gu_spec])[0]

    def epi_half(acc, e_refs, o_refs, ids):
        o_refs[0][...] = (0.5 * acc).astype(BF)

    dw2 = _mm_plain(tag + "_dw2", a, dres, TN, BF, tm=1408, tn=d, tk=1024, epi=epi_half)

    tk = _tile(l, 1024, SUBLANE)
    dw1 = _mm(tag + "_dw1",
              [(hn, pl.BlockSpec((tk, d), lambda i, j, kk: (kk, 0)), dgu, pl.BlockSpec((None, tk, f), lambda i, j, kk: (j, kk, 0)))],
              TN, (1, 2, l // tk), [_sds((d, 2 * f), BF)], [pl.BlockSpec((d, f), lambda i, j, kk: (0, j))], (d, f))[0]

    gnorm = on_grads(dw1, dw2, gnorm)
    row = pl.BlockSpec((tm, d), lambda i, j, kk: (i, 0))
    vec = pl.BlockSpec((1, d), lambda i, j, kk: (0, 0))
    once = pl.Buffered(1)
    dx, dg = _mm(tag + "_dhn",
                 [(dgu, pl.BlockSpec((None, tm, f), lambda i, j, kk: (0, i, 0)), w1, pl.BlockSpec((d, f), lambda i, j, kk: (0, 0), pipeline_mode=once)),
                  (dgu, pl.BlockSpec((None, tm, f), lambda i, j, kk: (1, i, 0)), w1, pl.BlockSpec((d, f), lambda i, j, kk: (0, 1), pipeline_mode=once))],
                 NT, (l // tm, 1, 1), [_sds((l, d), F32), _sds((1, d), F32)], [row, vec], (tm, d),
                 _rms_bwd_epi(0), [x, gnorm, dres], [row, vec, row])
    return dx, dg


def _s5_chunk_tables(lam_re, lam_im, b_re, b_im, c_re, c_im, log_dt, hs):
    f32 = F32
    g, n = lam_re.shape[1], lam_re.shape[2]
    p = b_re.shape[-1]
    nch, gpc, nt = (g * n) // hs, hs // n, hs // LANE
    lr = jnp.minimum(lam_re.astype(f32), -1e-4)
    li = lam_im.astype(f32)
    dt = jnp.exp(log_dt.astype(f32))[..., None]
    mag = jnp.exp(lr * dt)
    ar = mag * jnp.cos(li * dt)
    ai = mag * jnp.sin(li * dt)
    den = lr * lr + li * li
    cr = ((ar - 1.0) * lr + ai * li) / den
    ci = (ai * lr - (ar - 1.0) * li) / den
    bbr = cr[..., None] * b_re - ci[..., None] * b_im
    bbi = cr[..., None] * b_im + ci[..., None] * b_re
    a_f = jnp.stack([ar, ai], axis=1).reshape(2, 2, nch, nt, LANE).transpose(0, 2, 1, 3, 4).reshape(2, nch, 2 * nt, LANE)
    rows_g = jnp.arange(gpc * p) // p
    cols_g = (jnp.arange(2 * hs) % hs) // n
    diag = (rows_g[:, None] == cols_g[None, :]).astype(f32)
    bb = jnp.stack([bbr, bbi], axis=1).reshape(2, 2, nch, hs, p)
    bd = jnp.tile(bb.transpose(0, 2, 4, 1, 3).reshape(2, nch, p, 2 * hs), (1, 1, gpc, 1)) * diag
    cc = jnp.stack([c_re, -c_im], axis=1).reshape(2, 2, nch, gpc, p, n)
    cd = jnp.tile(cc.transpose(0, 2, 4, 1, 3, 5).reshape(2, nch, p, 2 * hs), (1, 1, gpc, 1)) * diag
    return a_f, bd, cd


def _fold_store(ref, val, tb, ntiles):
    for s in range(ntiles):
        ref[:, s * SUBLANE:(s + 1) * SUBLANE, :] = val[:, s * LANE:(s + 1) * LANE].reshape(tb // SUBLANE, SUBLANE, LANE)


def _unfold(ref, tb, ntiles):
    return jnp.concatenate([ref[:, s * SUBLANE:(s + 1) * SUBLANE, :].reshape(tb, LANE) for s in range(ntiles)], axis=1)


def _s5_fwd(name, proj, bd, cd, a_f, reverse):
    l = proj.shape[0]
    nch, cu, hs2 = bd.shape
    nt = hs2 // (2 * LANE)
    frows = 2 * nt * SUBLANE
    tb = _tile(l, 512, SUBLANE)
    nb = l // tb

    def body(u_ref, bd_ref, cd_ref, a_ref, xf_ref, y_ref, st):
        r = pl.program_id(1)

        @pl.when(r == 0)
        def _():
            st[...] = jnp.zeros_like(st)

        _fold_store(xf_ref, _dot(u_ref[...], bd_ref[...]), tb, 2 * nt)
        ar, ai = a_ref[0:nt, :], a_ref[nt:2 * nt, :]

        def group(gi, carry):
            rr = (tb // SUBLANE - 1 - gi) if reverse else gi
            sr, si = carry
            for qq in range(SUBLANE):
                q = (SUBLANE - 1 - qq) if reverse else qq
                re_rows, im_rows = pl.ds(q, nt, stride=SUBLANE), pl.ds(nt * SUBLANE + q, nt, stride=SUBLANE)
                nr = ar * sr - ai * si + xf_ref[rr, re_rows, :]
                ni = ar * si + ai * sr + xf_ref[rr, im_rows, :]
                xf_ref[rr, re_rows, :] = nr
                xf_ref[rr, im_rows, :] = ni
                sr, si = nr, ni
            return sr, si

        fin = lax.fori_loop(0, tb // SUBLANE, group, (st[0:nt, :], st[nt:2 * nt, :]))
        st[0:nt, :] = fin[0]
        st[nt:2 * nt, :] = fin[1]
        y_ref[...] = _dot(_unfold(xf_ref, tb, 2 * nt), cd_ref[...], NT)

    def rows(r):
        return (nb - 1 - r) if reverse else r

    return pl.pallas_call(
        body, name=name, grid=(nch, nb),
        in_specs=[pl.BlockSpec((tb, cu), lambda c, r: (rows(r), c)), pl.BlockSpec((None, cu, hs2), lambda c, r: (c, 0, 0)),
                  pl.BlockSpec((None, cu, hs2), lambda c, r: (c, 0, 0)), pl.BlockSpec((None, 2 * nt, LANE), lambda c, r: (c, 0, 0))],
        out_specs=[pl.BlockSpec((tb // SUBLANE, frows, LANE), lambda c, r: (rows(r), c, 0)), pl.BlockSpec((tb, cu), lambda c, r: (rows(r), c))],
        out_shape=[_sds((l // SUBLANE, nch * frows, LANE), F32), _sds((l, nch * cu), F32)],
        scratch_shapes=[pltpu.VMEM((2 * nt, LANE), F32)],
        compiler_params=_params(("arbitrary", "arbitrary")),
    )(proj, bd, cd, a_f)


def _s5_bwd(name, proj, dy, xf, bd, cd, a_conj, reverse):
    l = proj.shape[0]
    nch, cu, hs2 = bd.shape
    nt = hs2 // (2 * LANE)
    frows = 2 * nt * SUBLANE
    tb = _tile(l, 512, SUBLANE)
    nb = l // tb

    def body(u_ref, dy_ref, xs_ref, bd_ref, cd_ref, a_ref, du_ref, dbd_ref, dcd_ref, da_ref, lam, st):
        r = pl.program_id(1)

        @pl.when(r == 0)
        def _():
            st[...] = jnp.zeros_like(st)
            dbd_ref[...] = jnp.zeros_like(dbd_ref)
            dcd_ref[...] = jnp.zeros_like(dcd_ref)
            da_ref[...] = jnp.zeros_like(da_ref)

        dyv = dy_ref[...]
        _fold_store(lam, _dot(dyv, cd_ref[...]), tb, 2 * nt)
        ar, ai = a_ref[0:nt, :], a_ref[nt:2 * nt, :]

        def group(gi, carry):
            rr = (tb // SUBLANE - 1 - gi) if reverse else gi
            sr, si, cr, ci = carry
            for qq in range(SUBLANE):
                q = (SUBLANE - 1 - qq) if reverse else qq
                re_rows, im_rows = pl.ds(q, nt, stride=SUBLANE), pl.ds(nt * SUBLANE + q, nt, stride=SUBLANE)
                xr, xi = xs_ref[rr, re_rows, :], xs_ref[rr, im_rows, :]
                cr = cr + sr * xr + si * xi
                ci = ci + si * xr - sr * xi
                nr = ar * sr - ai * si + lam[rr, re_rows, :]
                ni = ar * si + ai * sr + lam[rr, im_rows, :]
                lam[rr, re_rows, :] = nr
                lam[rr, im_rows, :] = ni
                sr, si = nr, ni
            return sr, si, cr, ci

        zero = jnp.zeros((nt, LANE), F32)
        fin = lax.fori_loop(0, tb // SUBLANE, group, (st[0:nt, :], st[nt:2 * nt, :], zero, zero))
        st[0:nt, :] = fin[0]
        st[nt:2 * nt, :] = fin[1]
        da_ref[0:nt, :] += fin[2]
        da_ref[nt:2 * nt, :] += fin[3]
        lam_u = _unfold(lam, tb, 2 * nt)
        du_ref[...] = _dot(lam_u, bd_ref[...], NT)
        dbd_ref[...] += _dot(u_ref[...], lam_u, TN)
        dcd_ref[...] += _dot(dyv, _unfold(xs_ref, tb, 2 * nt), TN)

    def rows(r):
        return (nb - 1 - r) if reverse else r

    chunk_rows = pl.BlockSpec((tb, cu), lambda c, r: (rows(r), c))
    bd_spec = pl.BlockSpec((None, cu, hs2), lambda c, r: (c, 0, 0))
    cd_spec = bd_spec
    a_spec = pl.BlockSpec((None, 2 * nt, LANE), lambda c, r: (c, 0, 0))
    return pl.pallas_call(
        body, name=name, grid=(nch, nb),
        in_specs=[chunk_rows, chunk_rows, pl.BlockSpec((tb // SUBLANE, frows, LANE), lambda c, r: (rows(r), c, 0)), bd_spec, cd_spec, a_spec],
        out_specs=[chunk_rows, bd_spec, cd_spec, a_spec],
        out_shape=[_sds((l, nch * cu), F32), _sds((nch, cu, hs2), F32), _sds((nch, cu, hs2), F32), _sds((nch, 2 * nt, LANE), F32)],
        scratch_shapes=[pltpu.VMEM((tb // SUBLANE, frows, LANE), F32), pltpu.VMEM((2 * nt, LANE), F32)],
        compiler_params=_params(("arbitrary", "arbitrary")),
    )(proj, dy, xf, bd, cd, a_conj)


def _s5_du(name, du_f, du_b, dy, proj, d_row):
    l, w = dy.shape
    tm = _tile(l, 1024, SUBLANE)

    def body(f_ref, b_ref, dy_ref, u_ref, d_ref, du_ref, dd_ref):
        i = pl.program_id(0)
        dyv = dy_ref[...]
        du_ref[...] = (f_ref[...] + b_ref[...] + dyv * d_ref[...]).astype(du_ref.dtype)
        part = jnp.sum(dyv * u_ref[...].astype(F32), axis=0, keepdims=True)

        @pl.when(i == 0)
        def _():
            dd_ref[...] = part

        @pl.when(i > 0)
        def _():
            dd_ref[...] += part

    row = pl.BlockSpec((tm, w), lambda i: (i, 0))
    vec = pl.BlockSpec((1, w), lambda i: (0, 0))
    return pl.pallas_call(
        body, name=name, grid=(l // tm,), in_specs=[row, row, row, row, vec], out_specs=[row, vec],
        out_shape=[_sds((l, w), BF), _sds((1, w), F32)],
        compiler_params=_params(("arbitrary",)),
    )(du_f, du_b, dy, proj, d_row)


def _gelu(y):
    c = math.sqrt(2.0 / math.pi)
    return 0.5 * y * (1.0 + jnp.tanh(c * (y + 0.044715 * y * y * y)))


def _gelu_grad(y):
    c = math.sqrt(2.0 / math.pi)
    th = jnp.tanh(c * (y + 0.044715 * y * y * y))
    return 0.5 * (1.0 + th) + 0.5 * y * (1.0 - th * th) * c * (1.0 + 3.0 * 0.044715 * y * y)


def _glu_fwd(name, y_f, y_b, proj, d_row, w):
    l, wd = y_f.shape
    tm = _tile(l, 512, SUBLANE)

    def body(yf_ref, yb_ref, u_ref, d_ref, w_ref, y_ref, o_ref):
        y = yf_ref[...] + yb_ref[...] + u_ref[...].astype(F32) * d_ref[...]
        y_ref[...] = y
        gy = _gelu(y)
        z = _dot(gy, w_ref[...])
        o_ref[...] = (gy * _sigmoid(z)).astype(o_ref.dtype)

    row = pl.BlockSpec((tm, wd), lambda i: (i, 0))
    return pl.pallas_call(
        body, name=name, grid=(l // tm,),
        in_specs=[row, row, row, pl.BlockSpec((1, wd), lambda i: (0, 0)), pl.BlockSpec((wd, wd), lambda i: (0, 0))],
        out_specs=[row, row], out_shape=[_sds((l, wd), F32), _sds((l, wd), BF)],
        compiler_params=_params(("arbitrary",)),
    )(y_f, y_b, proj, d_row, w)


def _glu_bwd(name, y, w, dout, dcol):
    l, wd = y.shape
    tm = _tile(l, 512, SUBLANE)

    def body(y_ref, w_ref, d_ref, dy_ref, dw_ref):
        i = pl.program_id(0)
        yv = y_ref[...]
        gy = _gelu(yv)
        s = _sigmoid(_dot(gy, w_ref[...]))
        d = d_ref[...].astype(F32)
        t = d * gy * s * (1.0 - s)
        dgy = d * s + _dot(t, w_ref[...], NT)
        dy_ref[...] = dgy * _gelu_grad(yv)
        part = _dot(gy, t, TN)

        @pl.when(i == 0)
        def _():
            dw_ref[...] = part

        @pl.when(i > 0)
        def _():
            dw_ref[...] += part

    return pl.pallas_call(
        body, name=name, grid=(l // tm,),
        in_specs=[pl.BlockSpec((tm, wd), lambda i: (i, 0)), pl.BlockSpec((wd, wd), lambda i: (0, 0)),
                  pl.BlockSpec((tm, wd), lambda i: (i, dcol))],
        out_specs=[pl.BlockSpec((tm, wd), lambda i: (i, 0)), pl.BlockSpec((wd, wd), lambda i: (0, 0))],
        out_shape=[_sds((l, wd), F32), _sds((wd, wd), F32)],
        compiler_params=_params(("arbitrary",)),
    )(y, w, dout)


def _log_sigmoid(x):
    return jnp.minimum(x, 0.0) - jnp.log(1.0 + jnp.exp(-jnp.abs(x)))


def _gate_fwd(name, glo, wf, wb, bf, bb):
    l, r2 = glo.shape
    hk = wf.shape[1]
    tm = _tile(l, 1024, SUBLANE)

    def body(x_ref, wf_ref, wb_ref, bf_ref, bb_ref, gf_ref, gb_ref):
        xv = x_ref[...]
        gf_ref[...] = _log_sigmoid(_dot(xv, wf_ref[...]) + bf_ref[...]) * (1.0 / GLA_GATE_NORM)
        gb_ref[...] = _log_sigmoid(_dot(xv, wb_ref[...]) + bb_ref[...]) * (1.0 / GLA_GATE_NORM)

    w_spec = pl.BlockSpec((r2, hk), lambda i: (0, 0))
    b_spec = pl.BlockSpec((1, hk), lambda i: (0, 0))
    o_spec = pl.BlockSpec((tm, hk), lambda i: (i, 0))
    return pl.pallas_call(
        body, name=name, grid=(l // tm,),
        in_specs=[pl.BlockSpec((tm, r2), lambda i: (i, 0)), w_spec, w_spec, b_spec, b_spec],
        out_specs=[o_spec, o_spec], out_shape=[_sds((l, hk), F32), _sds((l, hk), F32)],
        compiler_params=_params(("arbitrary",)),
    )(glo, wf, wb, bf, bb)


def _gate_bwd(name, glo, wf, wb, bf, bb, dgf, dgb):
    l, r2 = glo.shape
    hk = wf.shape[1]
    tm = _tile(l, 1024, SUBLANE)

    def body(x_ref, wf_ref, wb_ref, bf_ref, bb_ref, dgf_ref, dgb_ref, dx_ref, dwf_ref, dwb_ref, dbf_ref, dbb_ref):
        i = pl.program_id(0)
        xv = x_ref[...]
        kf = _dot(xv, wf_ref[...]) + bf_ref[...]
        kb = _dot(xv, wb_ref[...]) + bb_ref[...]
        dkf = dgf_ref[...] * (1.0 / GLA_GATE_NORM) * _sigmoid(-kf)
        dkb = dgb_ref[...] * (1.0 / GLA_GATE_NORM) * _sigmoid(-kb)
        dx_ref[...] = _dot(dkf, wf_ref[...], NT) + _dot(dkb, wb_ref[...], NT)
        parts = (_dot(xv, dkf, TN), _dot(xv, dkb, TN), jnp.sum(dkf, axis=0, keepdims=True), jnp.sum(dkb, axis=0, keepdims=True))
        accs = (dwf_ref, dwb_ref, dbf_ref, dbb_ref)

        @pl.when(i == 0)
        def _():
            for a_, p_ in zip(accs, parts):
                a_[...] = p_

        @pl.when(i > 0)
        def _():
            for a_, p_ in zip(accs, parts):
                a_[...] += p_

    w_spec = pl.BlockSpec((r2, hk), lambda i: (0, 0))
    b_spec = pl.BlockSpec((1, hk), lambda i: (0, 0))
    g_spec = pl.BlockSpec((tm, hk), lambda i: (i, 0))
    x_spec = pl.BlockSpec((tm, r2), lambda i: (i, 0))
    return pl.pallas_call(
        body, name=name, grid=(l // tm,),
        in_specs=[x_spec, w_spec, w_spec, b_spec, b_spec, g_spec, g_spec],
        out_specs=[x_spec, w_spec, w_spec, b_spec, b_spec],
        out_shape=[_sds((l, r2), F32), _sds((r2, hk), F32), _sds((r2, hk), F32), _sds((1, hk), F32), _sds((1, hk), F32)],
        compiler_params=_params(("arbitrary",)),
    )(glo, wf, wb, bf, bb, dgf, dgb)


def _chunk_terms(qc, kc, gc, lg, chunk, reverse):
    ri = lax.broadcasted_iota(jnp.int32, (chunk, chunk), 0)
    ci = lax.broadcasted_iota(jnp.int32, (chunk, chunk), 1)
    if reverse:
        tri = ci >= ri
        mask = ci > ri
    else:
        tri = ci <= ri
        mask = ci <= ri
    if gc is not None:
        cum = _dot3(tri.astype(F32), gc)
        last = cum[0:1, :] if reverse else cum[chunk - 1:chunk, :]
    else:
        pos = lax.broadcasted_iota(jnp.int32, (chunk, 1), 0).astype(F32)
        cum = ((chunk - pos) if reverse else (pos + 1.0)) * lg
        last = chunk * lg
    e = jnp.exp(cum)
    einv = jnp.exp(-cum)
    dec = jnp.exp(last - cum)
    return e, einv, dec, qc * e, kc * einv, kc * dec, jnp.exp(last), mask, tri


def _lin_specs(arr, width, col, tb, nb, reverse, per_head):
    if per_head:
        return pl.BlockSpec((tb, width), lambda h, r: ((nb - 1 - r) if reverse else r, col + h))
    return pl.BlockSpec((tb, width), lambda h, r: ((nb - 1 - r) if reverse else r, col))


def _lin_fwd(name, q, k, v, g, lgtab, *, heads, hb, dk, dv, chunk, tb, qcol, kcol, vcol, qscale, reverse):
    l = q.shape[0]
    nb = l // tb
    ncb = tb // chunk
    ng = heads // hb
    gated = g is not None
    per_head = ng > 1

    def body(*refs):
        if gated:
            q_ref, k_ref, v_ref, g_ref, o_ref, sp_ref, st = refs
        else:
            q_ref, k_ref, v_ref, lg_ref, o_ref, sp_ref, st = refs
        r = pl.program_id(1)

        @pl.when(r == 0)
        def _():
            st[...] = jnp.zeros_like(st)

        for c in range(ncb):
            cc = (ncb - 1 - c) if reverse else c
            rows = pl.ds(cc * chunk, chunk)
            for h in range(hb):
                lg = None if gated else lg_ref[h, :, 0:1]
                qc = q_ref[rows, h * dk:(h + 1) * dk].astype(F32) * qscale
                kc = k_ref[rows, h * dk:(h + 1) * dk].astype(F32)
                vc = v_ref[rows, h * dv:(h + 1) * dv]
                gc = g_ref[rows, h * dk:(h + 1) * dk] if gated else None
                _, _, _, qd, ki, kdec, e_last, mask, _ = _chunk_terms(qc, kc, gc, lg, chunk, reverse)
                a = jnp.where(mask, _dot(qd, ki, NT), 0.0)
                s_t = st[h]
                o_ref[rows, h * dv:(h + 1) * dv] = _dot(a, vc) + _dot(qd, s_t, NT)
                sp_ref[cc, h] = s_t
                st[h] = s_t * e_last + _dot(vc, kdec, TN)

    in_specs = [_lin_specs(q, hb * dk, qcol, tb, nb, reverse, per_head), _lin_specs(k, hb * dk, kcol, tb, nb, reverse, per_head),
                _lin_specs(v, hb * dv, vcol, tb, nb, reverse, per_head)]
    args = [q, k, v]
    if gated:
        in_specs.append(_lin_specs(g, hb * dk, 0, tb, nb, reverse, per_head))
        args.append(g)
    else:
        in_specs.append(pl.BlockSpec((hb, 1, LANE), lambda h, r: (h, 0, 0)))
        args.append(lgtab)
    out_specs = [_lin_specs(None, hb * dv, 0, tb, nb, reverse, per_head),
                 pl.BlockSpec((ncb, hb, dv, dk), lambda h, r: ((nb - 1 - r) if reverse else r, h, 0, 0))]
    outs = [_sds((l, heads * dv), F32), _sds((l // chunk, heads, dv, dk), F32)]
    return pl.pallas_call(
        body, name=name, grid=(ng, nb), in_specs=in_specs, out_specs=out_specs, out_shape=outs,
        scratch_shapes=[pltpu.VMEM((hb, dv, dk), F32)],
        compiler_params=_params(("arbitrary", "arbitrary")),
    )(*args)


def _lin_bwd(name, q, k, v, g, lgtab, sprev, do, prev, *, heads, hb, dk, dv, chunk, tb, qcol, kcol, vcol, qscale, reverse):
    l = q.shape[0]
    nb = l // tb
    ncb = tb // chunk
    ng = heads // hb
    gated = g is not None
    per_head = ng > 1
    brev = not reverse
    n_prev = 0 if prev is None else len(prev)

    def body(*refs):
        q_ref, k_ref, v_ref, x_ref, sp_ref, do_ref = refs[:6]
        p_refs = refs[6:6 + n_prev]
        o_refs = refs[6 + n_prev:-1]
        dst = refs[-1]
        dq_ref, dk_ref, dv_ref = o_refs[:3]
        r = pl.program_id(1)

        @pl.when(r == 0)
        def _():
            dst[...] = jnp.zeros_like(dst)

        for c in range(ncb):
            cc = (ncb - 1 - c) if brev else c
            rows = pl.ds(cc * chunk, chunk)
            for h in range(hb):
                lg = None if gated else x_ref[h, :, 0:1]
                ks = slice(h * dk, (h + 1) * dk)
                vs = slice(h * dv, (h + 1) * dv)
                qc = q_ref[rows, ks].astype(F32) * qscale
                kc = k_ref[rows, ks].astype(F32)
                vc = v_ref[rows, vs]
                gc = x_ref[rows, ks] if gated else None
                e, einv, dec, qd, ki, kdec, e_last, mask, tri = _chunk_terms(qc, kc, gc, lg, chunk, reverse)
                a = jnp.where(mask, _dot(qd, ki, NT), 0.0)
                s_t = sp_ref[cc, h]
                ds_t = dst[h]
                doc = do_ref[rows, vs]
                dvc = _dot(a, doc, TN) + _dot(kdec, ds_t, NT)
                da = jnp.where(mask, _dot(doc, vc, NT), 0.0)
                dqd = _dot(da, ki) + _dot(doc, s_t)
                dki = _dot(da, qd, TN)
                dkdec = _dot(vc, ds_t)
                dst[h] = ds_t * e_last + _dot(doc, qd, TN)
                dqc = dqd * e * qscale
                dkc = dki * einv + dkdec * dec
                if n_prev:
                    dqc = dqc + p_refs[0][rows, ks]
                    dkc = dkc + p_refs[1][rows, ks]
                    dvc = dvc + p_refs[2][rows, vs]
                dq_ref[rows, ks] = dqc
                dk_ref[rows, ks] = dkc
                dv_ref[rows, vs] = dvc
                if gated:
                    dcum = dqd * qd - dki * ki - dkdec * kdec
                    dlast = jnp.sum(dkdec * kdec, axis=0, keepdims=True) + e_last * jnp.sum(s_t * ds_t, axis=0, keepdims=True)
                    rid = lax.broadcasted_iota(jnp.int32, (chunk, 1), 0)
                    dcum = dcum + jnp.where(rid == (0 if reverse else chunk - 1), dlast, 0.0)
                    dgc = _dot3(tri.astype(F32), dcum, TN)
                    o_refs[3][rows, ks] = dgc

    def spec(width, col):
        return _lin_specs(None, width, col, tb, nb, brev, per_head)

    in_specs = [spec(hb * dk, qcol), spec(hb * dk, kcol), spec(hb * dv, vcol)]
    args = [q, k, v]
    if gated:
        in_specs.append(spec(hb * dk, 0))
        args.append(g)
    else:
        in_specs.append(pl.BlockSpec((hb, 1, LANE), lambda h, r: (h, 0, 0)))
        args.append(lgtab)
    in_specs.append(pl.BlockSpec((ncb, hb, dv, dk), lambda h, r: ((nb - 1 - r) if brev else r, h, 0, 0)))
    args.append(sprev)
    in_specs.append(spec(hb * dv, 0))
    args.append(do)
    out_specs = [spec(hb * dk, 0), spec(hb * dk, 0), spec(hb * dv, 0)]
    outs = [_sds((l, heads * dk), F32), _sds((l, heads * dk), F32), _sds((l, heads * dv), F32)]
    if gated:
        out_specs.append(spec(hb * dk, 0))
        outs.append(_sds((l, heads * dk), F32))
    if n_prev:
        in_specs += out_specs[:3]
        args += list(prev)
    return pl.pallas_call(
        body, name=name, grid=(ng, nb), in_specs=in_specs, out_specs=out_specs, out_shape=outs,
        scratch_shapes=[pltpu.VMEM((hb, dv, dk), F32)],
        compiler_params=_params(("arbitrary", "arbitrary")),
    )(*args)


def _log2(n):
    assert n & (n - 1) == 0, "a power of two"
    return n.bit_length() - 1


def _gla_block_terms(q, k, g, qscale, chunk, tb, reverse):
    ri = lax.broadcasted_iota(jnp.int32, (tb, tb), 0)
    ci = lax.broadcasted_iota(jnp.int32, (tb, tb), 1)
    same = jnp.right_shift(ri, _log2(chunk)) == jnp.right_shift(ci, _log2(chunk))
    t_in = jnp.logical_and(same, (ci >= ri) if reverse else (ci <= ri)).astype(F32)
    cum = _dot3(t_in, g)
    tot = _dot3(same.astype(F32), g)
    e = jnp.exp(cum)
    einv = jnp.exp(-cum)
    dec = jnp.exp(tot - cum)
    return e, einv, dec, jnp.exp(tot), q * (qscale * e), k * einv, k * dec, t_in


def _gla_masks(hk, dk, heads, chunk, reverse):
    lane = lax.broadcasted_iota(jnp.int32, (1, hk), 1)
    head_of = jnp.right_shift(lane, _log2(dk))
    ri = lax.broadcasted_iota(jnp.int32, (chunk, chunk), 0)
    ci = lax.broadcasted_iota(jnp.int32, (chunk, chunk), 1)
    return [head_of == h for h in range(heads)], ((ci > ri) if reverse else (ci <= ri))


def _gla_fwd(name, proj, g, *, heads, dk, dv, chunk, tb, qcol, kcol, vcol, qscale, reverse):
    l = proj.shape[0]
    nb, ncb, hk, hv = l // tb, tb // chunk, heads * dk, heads * dv

    def body(q_ref, k_ref, v_ref, g_ref, o_ref, sp_ref, st):
        @pl.when(pl.program_id(0) == 0)
        def _():
            st[...] = jnp.zeros_like(st)

        _, _, _, etot, qd, ki, kdec, _ = _gla_block_terms(q_ref[...].astype(F32), k_ref[...].astype(F32), g_ref[...], qscale, chunk, tb, reverse)
        heads_m, causal = _gla_masks(hk, dk, heads, chunk, reverse)
        s_all = st[...]
        for c in range(ncb):
            cc = (ncb - 1 - c) if reverse else c
            rc = slice(cc * chunk, (cc + 1) * chunk)
            qd_c, ki_c, kdec_c = qd[rc], ki[rc], kdec[rc]
            sp_ref[cc] = s_all
            kv = jnp.zeros_like(s_all)
            for h in range(heads):
                vs = slice(h * dv, (h + 1) * dv)
                qm = jnp.where(heads_m[h], qd_c, 0.0)
                a = jnp.where(causal, _dot(qm, ki_c, NT), 0.0)
                vc = v_ref[rc, vs]
                o_ref[rc, vs] = _dot(a, vc) + _dot(qm, s_all, NT)
                kv = kv + jnp.where(heads_m[h], _dot(vc, kdec_c, TN), 0.0)
            s_all = s_all * etot[rc][0:1, :] + kv
        st[...] = s_all

    def rows(r):
        return (nb - 1 - r) if reverse else r

    return pl.pallas_call(
        body, name=name, grid=(nb,),
        in_specs=[pl.BlockSpec((tb, hk), lambda r: (rows(r), qcol)), pl.BlockSpec((tb, hk), lambda r: (rows(r), kcol)),
                  pl.BlockSpec((tb, hv), lambda r: (rows(r), vcol)), pl.BlockSpec((tb, hk), lambda r: (rows(r), 0))],
        out_specs=[pl.BlockSpec((tb, hv), lambda r: (rows(r), 0)), pl.BlockSpec((ncb, dv, hk), lambda r: (rows(r), 0, 0))],
        out_shape=[_sds((l, hv), F32), _sds((l // chunk, dv, hk), F32)],
        scratch_shapes=[pltpu.VMEM((dv, hk), F32)],
        compiler_params=_params(("arbitrary",)),
    )(proj, proj, proj, g)


def _gla_bwd(name, proj, g, sprev, do, prev, *, heads, dk, dv, chunk, tb, qcol, kcol, vcol, qscale, reverse):
    l = proj.shape[0]
    nb, ncb, hk, hv = l // tb, tb // chunk, heads * dk, heads * dv
    brev = not reverse
    n_prev = 0 if prev is None else len(prev)

    def body(q_ref, k_ref, v_ref, g_ref, sp_ref, do_ref, *rest):
        p_refs = rest[:n_prev]
        dq_ref, dk_ref, dv_ref, dg_ref, dst, dcs = rest[n_prev:]

        @pl.when(pl.program_id(0) == 0)
        def _():
            dst[...] = jnp.zeros_like(dst)

        e, einv, dec, etot, qd, ki, kdec, t_in = _gla_block_terms(q_ref[...].astype(F32), k_ref[...].astype(F32), g_ref[...], qscale, chunk, tb, reverse)
        heads_m, causal = _gla_masks(hk, dk, heads, chunk, reverse)
        last_row = lax.broadcasted_iota(jnp.int32, (chunk, 1), 0) == (0 if reverse else chunk - 1)
        ds_all = dst[...]
        for c in range(ncb):
            cc = (ncb - 1 - c) if brev else c
            rc = slice(cc * chunk, (cc + 1) * chunk)
            qd_c, ki_c, kdec_c = qd[rc], ki[rc], kdec[rc]
            s_all = sp_ref[cc]
            et = etot[rc][0:1, :]
            dqd = jnp.zeros((chunk, hk), F32)
            dki = jnp.zeros((chunk, hk), F32)
            dkdec = jnp.zeros((chunk, hk), F32)
            ds_add = jnp.zeros_like(ds_all)
            for h in range(heads):
                vs = slice(h * dv, (h + 1) * dv)
                m = heads_m[h]
                qm = jnp.where(m, qd_c, 0.0)
                a = jnp.where(causal, _dot(qm, ki_c, NT), 0.0)
                doc, vc = do_ref[rc, vs], v_ref[rc, vs]
                dvc = _dot(a, doc, TN) + _dot(jnp.where(m, kdec_c, 0.0), ds_all, NT)
                if n_prev:
                    dvc = dvc + p_refs[2][rc, vs]
                dv_ref[rc, vs] = dvc
                da = jnp.where(causal, _dot(doc, vc, NT), 0.0)
                dqd = dqd + jnp.where(m, _dot(da, ki_c) + _dot(doc, s_all), 0.0)
                dki = dki + _dot(da, qm, TN)
                dkdec = dkdec + jnp.where(m, _dot(vc, ds_all), 0.0)
                ds_add = ds_add + _dot(doc, qm, TN)
            dqc = dqd * e[rc] * qscale
            dkc = dki * einv[rc] + dkdec * dec[rc]
            if n_prev:
                dqc = dqc + p_refs[0][rc, :]
                dkc = dkc + p_refs[1][rc, :]
            dq_ref[rc, :] = dqc
            dk_ref[rc, :] = dkc
            dlast = jnp.sum(dkdec * kdec_c, axis=0, keepdims=True) + et * jnp.sum(s_all * ds_all, axis=0, keepdims=True)
            dcs[rc, :] = dqd * qd_c - dki * ki_c - dkdec * kdec_c + jnp.where(last_row, dlast, 0.0)
            ds_all = ds_all * et + ds_add
        dst[...] = ds_all
        dg_ref[...] = _dot3(t_in, dcs[...], TN)

    def rows(r):
        return (nb - 1 - r) if brev else r

    k_spec = pl.BlockSpec((tb, hk), lambda r: (rows(r), 0))
    v_spec = pl.BlockSpec((tb, hv), lambda r: (rows(r), 0))
    in_specs = [pl.BlockSpec((tb, hk), lambda r: (rows(r), qcol)), pl.BlockSpec((tb, hk), lambda r: (rows(r), kcol)),
                pl.BlockSpec((tb, hv), lambda r: (rows(r), vcol)), k_spec,
                pl.BlockSpec((ncb, dv, hk), lambda r: (rows(r), 0, 0)), v_spec]
    args = [proj, proj, proj, g, sprev, do]
    if n_prev:
        in_specs += [k_spec, k_spec, v_spec]
        args += list(prev)
    return pl.pallas_call(
        body, name=name, grid=(nb,), in_specs=in_specs, out_specs=[k_spec, k_spec, v_spec, k_spec],
        out_shape=[_sds((l, hk), F32), _sds((l, hk), F32), _sds((l, hv), F32), _sds((l, hk), F32)],
        scratch_shapes=[pltpu.VMEM((dv, hk), F32), pltpu.VMEM((tb, hk), F32)],
        compiler_params=_params(("arbitrary",)),
    )(*args)


def _headgate_fwd(name, o_f, o_b, og_arr, og_col, gn, dv):
    l, w = o_f.shape
    tm = _tile(l, 512, SUBLANE)
    nh = w // dv

    def body(of_ref, ob_ref, og_ref, gn_ref, out_ref):
        for h in range(nh):
            cs = slice(h * dv, (h + 1) * dv)
            o = of_ref[:, cs] + ob_ref[:, cs]
            r = lax.rsqrt(jnp.mean(o * o, axis=-1, keepdims=True) + EPS)
            og = og_ref[:, cs].astype(F32)
            out_ref[:, cs] = (o * r * gn_ref[:, cs] * (og * _sigmoid(og))).astype(out_ref.dtype)

    row = pl.BlockSpec((tm, w), lambda i: (i, 0))
    return pl.pallas_call(
        body, name=name, grid=(l // tm,),
        in_specs=[row, row, pl.BlockSpec((tm, w), lambda i: (i, og_col)), pl.BlockSpec((1, w), lambda i: (0, 0))],
        out_specs=row, out_shape=_sds((l, w), BF),
        compiler_params=_params(("arbitrary",)),
    )(o_f, o_b, og_arr, gn)


def _headgate_bwd(name, o_f, o_b, og_arr, og_col, gn, dout, dcol, dv):
    l, w = o_f.shape
    tm = _tile(l, 512, SUBLANE)
    nh = w // dv

    def body(of_ref, ob_ref, og_ref, gn_ref, d_ref, do_ref, dog_ref, dgn_ref):
        i = pl.program_id(0)
        for h in range(nh):
            cs = slice(h * dv, (h + 1) * dv)
            o = of_ref[:, cs] + ob_ref[:, cs]
            r = lax.rsqrt(jnp.mean(o * o, axis=-1, keepdims=True) + EPS)
            oh = o * r
            og = og_ref[:, cs].astype(F32)
            s = _sigmoid(og)
            d = d_ref[:, cs].astype(F32)
            gnv = gn_ref[:, cs]
            d_on = d * (og * s)
            dog_ref[:, cs] = (d * (oh * gnv) * s * (1.0 + og * (1.0 - s))).astype(dog_ref.dtype)
            doh = d_on * gnv
            do_ref[:, cs] = r * (doh - oh * jnp.mean(doh * oh, axis=-1, keepdims=True))
            part = jnp.sum(d_on * oh, axis=0, keepdims=True)

            @pl.when(i == 0)
            def _():
                dgn_ref[:, cs] = part

            @pl.when(i > 0)
            def _():
                dgn_ref[:, cs] += part

    row = pl.BlockSpec((tm, w), lambda i: (i, 0))
    vec = pl.BlockSpec((1, w), lambda i: (0, 0))
    return pl.pallas_call(
        body, name=name, grid=(l // tm,),
        in_specs=[row, row, pl.BlockSpec((tm, w), lambda i: (i, og_col)), vec, pl.BlockSpec((tm, w), lambda i: (i, dcol))],
        out_specs=[row, row, vec], out_shape=[_sds((l, w), F32), _sds((l, w), BF), _sds((1, w), F32)],
        compiler_params=_params(("arbitrary",)),
    )(o_f, o_b, og_arr, gn, dout)


def _rot_tables(l, dk):
    half = dk // 2
    pos = jnp.arange(l, dtype=F32)
    inv = jnp.exp(-math.log(ROPE_BASE) * jnp.arange(half, dtype=F32) / half)
    ang = pos[:, None] * inv[None, :]
    cos, sin = jnp.cos(ang), jnp.sin(ang)
    return jnp.concatenate([cos, cos], axis=-1), jnp.concatenate([-sin, sin], axis=-1)


def _rot_apply(name, src_q, qcol, src_k, kcol, cos_t, sin_t, heads, dk, kscale, out_dtype, transpose):
    l = src_q.shape[0]
    w = heads * dk
    tm = _tile(l, 512, SUBLANE)

    def rot(t, cos_v, sin_v):
        if transpose:
            return t * cos_v + pltpu.roll(t * sin_v, dk // 2, 1)
        return t * cos_v + pltpu.roll(t, dk // 2, 1) * sin_v

    def body(q_ref, k_ref, c_ref, s_ref, qo_ref, ko_ref):
        cos_v, sin_v = c_ref[...], s_ref[...]
        for h in range(heads):
            cs = slice(h * dk, (h + 1) * dk)
            qo_ref[:, cs] = rot(q_ref[:, cs].astype(F32), cos_v, sin_v).astype(out_dtype)
            ko_ref[:, cs] = (rot(k_ref[:, cs].astype(F32), cos_v, sin_v) * kscale).astype(out_dtype)

    tab = pl.BlockSpec((tm, dk), lambda i: (i, 0))
    row = pl.BlockSpec((tm, w), lambda i: (i, 0))
    return pl.pallas_call(
        body, name=name, grid=(l // tm,),
        in_specs=[pl.BlockSpec((tm, w), lambda i: (i, qcol)), pl.BlockSpec((tm, w), lambda i: (i, kcol)), tab, tab],
        out_specs=[row, row], out_shape=[_sds((l, w), out_dtype), _sds((l, w), out_dtype)],
        compiler_params=_params(("arbitrary",)),
    )(src_q, src_k, cos_t, sin_t)


def _exchange(name, src, gather):
    shape = src.shape if gather else src.shape[1:]

    def body(src_ref, out_ref, send_sems, recv_sems, local_sem):
        me = _my_index()
        own = pltpu.make_async_copy(src_ref if gather else src_ref.at[me], out_ref.at[me], local_sem)
        own.start()
        copies = _peer_copies(src_ref, out_ref, send_sems, recv_sems, gather)
        for cp in copies:
            cp.start()
        for cp in copies:
            cp.wait_recv()
        for cp in copies:
            cp.wait_send()
        own.wait()

    return pl.pallas_call(
        body, name=name,
        in_specs=[pl.BlockSpec(memory_space=pl.ANY)], out_specs=pl.BlockSpec(memory_space=pl.ANY),
        out_shape=_sds((N_DEV,) + tuple(shape), src.dtype),
        scratch_shapes=[pltpu.SemaphoreType.DMA((N_DEV - 1,)), pltpu.SemaphoreType.DMA((N_DEV - 1,)), pltpu.SemaphoreType.DMA],
        )(src)


def _my_index():
    return 4 * lax.axis_index("x") + 2 * lax.axis_index("y") + lax.axis_index("c")


def _peer_copies(src_ref, out_ref, send_sems, recv_sems, gather):
    x, y, c = lax.axis_index("x"), lax.axis_index("y"), lax.axis_index("c")
    me = 4 * x + 2 * y + c
    copies = []
    for kk in range(1, N_DEV):
        px = (1 - x) if kk & 4 else x
        py = (1 - y) if kk & 2 else y
        pc = (1 - c) if kk & 1 else c
        peer = 4 * px + 2 * py + pc
        copies.append(pltpu.make_async_remote_copy(
            src_ref=src_ref if gather else src_ref.at[peer], dst_ref=out_ref.at[me],
            send_sem=send_sems.at[kk - 1], recv_sem=recv_sems.at[kk - 1],
            device_id=(px, py, pc), device_id_type=pl.DeviceIdType.MESH))
    return copies


_HBM = pl.BlockSpec(memory_space=pltpu.HBM)
_SEM = pl.BlockSpec(memory_space=pltpu.SEMAPHORE)
_EFFECT = pltpu.SideEffectType.DATAFLOW_SIDE_EFFECTING


def _exchange_start(name, src, gather):
    shape = src.shape if gather else src.shape[1:]
    land = lax.empty((N_DEV,) + tuple(shape), src.dtype)

    def body(src_ref, land_ref, send_sems, recv_sems, src_thru, land_thru, token):
        for cp in _peer_copies(src_ref, land_ref, send_sems, recv_sems, gather):
            cp.start()
        token[...] = jnp.zeros_like(token)

    return pl.pallas_call(
        body, name=name,
        out_shape=(pltpu.SemaphoreType.DMA((N_DEV - 1,)), pltpu.SemaphoreType.DMA((N_DEV - 1,)),
                   pltpu.HBM(src.shape, src.dtype), pltpu.HBM(land.shape, land.dtype), _sds((SUBLANE, LANE), F32)),
        in_specs=(_HBM, _HBM), out_specs=(_SEM, _SEM, _HBM, _HBM, pl.BlockSpec(memory_space=pltpu.VMEM)),
        input_output_aliases={0: 2, 1: 3},
        compiler_params=pltpu.CompilerParams(has_side_effects=_EFFECT),
    )(pltpu.with_memory_space_constraint(src, pltpu.HBM), pltpu.with_memory_space_constraint(land, pltpu.HBM))


def _exchange_wait(name, started, gather, after):
    send_sems, recv_sems, src_thru, land_thru, _ = started

    def body(src_ref, land_ref, send_sems, recv_sems, after_ref, src_out, land_out):
        copies = _peer_copies(src_ref, land_ref, send_sems, recv_sems, gather)
        for cp in copies:
            cp.wait_send()
        for cp in copies:
            cp.wait_recv()

    return pl.pallas_call(
        body, name=name,
        out_shape=(pltpu.HBM(src_thru.shape, src_thru.dtype), pltpu.HBM(land_thru.shape, land_thru.dtype)),
        in_specs=(_HBM, _HBM, _SEM, _SEM, pl.BlockSpec(memory_space=pl.ANY)), out_specs=(_HBM, _HBM),
        input_output_aliases={0: 0, 1: 1},
        compiler_params=pltpu.CompilerParams(has_side_effects=_EFFECT),
    )(src_thru, land_thru, send_sems, recv_sems, after)


def _adam_math(w, gsum, m, v):
    m2 = ADAM_B1 * m + (1.0 - ADAM_B1) * gsum
    v2 = ADAM_B2 * v + (1.0 - ADAM_B2) * (gsum * gsum)
    m_hat = m2 / (1.0 - ADAM_B1 ** ADAM_STEP)
    v_hat = v2 / (1.0 - ADAM_B2 ** ADAM_STEP)
    delta = -ADAM_LR * (m_hat / (jnp.sqrt(v_hat) + ADAM_EPS) + ADAM_WD * w)
    return delta, m2, v2


def _reduce_adam(name, parts, w, m, v):
    nl, r, c = w.shape
    tr = _tile(r, 256, 16)
    nr = r // tr

    def body(*refs):
        p_refs = refs[:nl]
        w_ref, m_ref, v_ref, g_ref, d_ref, m2_ref, v2_ref = refs[nl:]
        for li in range(nl):
            @pl.when(pl.program_id(0) == li)
            def _(p_ref=p_refs[li]):
                gsum = p_ref[0].astype(F32)
                for s in range(1, N_DEV):
                    gsum = gsum + p_ref[s].astype(F32)
                g_ref[...] = gsum
                delta, m2, v2 = _adam_math(w_ref[...], gsum, m_ref[...], v_ref[...])
                d_ref[...] = delta
                m2_ref[...] = m2
                v2_ref[...] = v2

    def part_spec(li):
        return pl.BlockSpec((N_DEV, tr, c), lambda lay, i: (0, jnp.where(lay == li, i, jnp.where(lay < li, 0, nr - 1)), 0))

    row = pl.BlockSpec((None, tr, c), lambda lay, i: (lay, i, 0))
    return pl.pallas_call(
        body, name=name, grid=(nl, nr),
        in_specs=[part_spec(li) for li in range(nl)] + [row, row, row],
        out_specs=[row, row, row, row], out_shape=[_sds((nl, r, c), F32)] * 4,
        compiler_params=_params(("arbitrary", "arbitrary")),
    )(*parts, w, m, v)


def _reduce8(name, parts):
    _, r, c = parts.shape

    def body(p_ref, g_ref):
        gsum = p_ref[0]
        for s in range(1, N_DEV):
            gsum = gsum + p_ref[s]
        g_ref[...] = gsum

    return pl.pallas_call(
        body, name=name, grid=(1,),
        in_specs=[pl.BlockSpec((N_DEV, r, c), lambda i: (0, 0, 0))],
        out_specs=pl.BlockSpec((r, c), lambda i: (0, 0)), out_shape=_sds((r, c), F32),
        compiler_params=_params(("arbitrary",)),
    )(parts)


def _adam_packed(name, w, g, m, v):
    r, c = w.shape

    def body(w_ref, g_ref, m_ref, v_ref, d_ref, m2_ref, v2_ref):
        delta, m2, v2 = _adam_math(w_ref[...], g_ref[...], m_ref[...], v_ref[...])
        d_ref[...] = delta
        m2_ref[...] = m2
        v2_ref[...] = v2

    spec = pl.BlockSpec((r, c), lambda i: (0, 0))
    return pl.pallas_call(
        body, name=name, grid=(1,), in_specs=[spec] * 4, out_specs=[spec] * 3, out_shape=[_sds((r, c), F32)] * 3,
        compiler_params=_params(("arbitrary",)),
    )(w, g, m, v)


def _pack(arrs):
    flat = jnp.concatenate([a.reshape(-1).astype(F32) for a in arrs])
    n = flat.shape[0]
    pad = (-n) % (SUBLANE * LANE)
    return jnp.pad(flat, (0, pad)).reshape(-1, LANE)


def _unpack(packed, like):
    flat = packed.reshape(-1)
    out, off = [], 0
    for a in like:
        n = math.prod(a.shape)
        out.append(flat[off:off + n].reshape(a.shape))
        off += n
    return out


def _row_blocks(full):
    return full.reshape(N_DEV, full.shape[0] // N_DEV, full.shape[1])


def _col_blocks(full):
    r, c = full.shape
    return full.reshape(r, N_DEV, c // N_DEV).transpose(1, 0, 2)


def _cols_natural(blocks):
    n, r, c = blocks.shape
    return blocks.transpose(1, 0, 2).reshape(r, n * c)


def kernel(x, ffn1_norm, ffn1_w1, ffn1_w2, mix_norm, ffn2_norm, ffn2_w1, ffn2_w2, ab_w_in, s5_lambda_re, s5_lambda_im, s5_b_re, s5_b_im, s5_c_re, s5_c_im, s5_log_dt, s5_d, s5_w_glu, gla_w_gk, gla_b_gk, gla_norm, ab_w_out, ret_w_in, ret_norm, ret_w_out, final_norm, loss_target, m_ffn1_norm, m_ffn1_w1, m_ffn1_w2, m_mix_norm, m_ffn2_norm, m_ffn2_w1, m_ffn2_w2, m_ab_w_in, m_s5_lambda_re, m_s5_lambda_im, m_s5_b_re, m_s5_b_im, m_s5_c_re, m_s5_c_im, m_s5_log_dt, m_s5_d, m_s5_w_glu, m_gla_w_gk, m_gla_b_gk, m_gla_norm, m_ab_w_out, m_ret_w_in, m_ret_norm, m_ret_w_out, m_final_norm, v_ffn1_norm, v_ffn1_w1, v_ffn1_w2, v_mix_norm, v_ffn2_norm, v_ffn2_w1, v_ffn2_w2, v_ab_w_in, v_s5_lambda_re, v_s5_lambda_im, v_s5_b_re, v_s5_b_im, v_s5_c_re, v_s5_c_im, v_s5_log_dt, v_s5_d, v_s5_w_glu, v_gla_w_gk, v_gla_b_gk, v_gla_norm, v_ab_w_out, v_ret_w_in, v_ret_norm, v_ret_w_out, v_final_norm):
    names = ['ffn1_norm', 'ffn1_w1', 'ffn1_w2', 'mix_norm', 'ffn2_norm', 'ffn2_w1', 'ffn2_w2', 'ab_w_in', 's5_lambda_re', 's5_lambda_im', 's5_b_re', 's5_b_im', 's5_c_re', 's5_c_im', 's5_log_dt', 's5_d', 's5_w_glu', 'gla_w_gk', 'gla_b_gk', 'gla_norm', 'ab_w_out', 'ret_w_in', 'ret_norm', 'ret_w_out', 'final_norm']
    loc = locals()
    W = {n: loc[n] for n in names}
    M = {n: loc["m_" + n] for n in names}
    V = {n: loc["v_" + n] for n in names}

    me = 4 * lax.axis_index("x") + 2 * lax.axis_index("y") + lax.axis_index("c")
    xs = x[0]
    tgt = loss_target[0]
    l, d = xs.shape
    depth = ffn1_norm.shape[0]

    pending, tokens = {}, []

    def start_gather(tag, shard):
        started = _exchange_start("ags_" + tag, shard, True)
        pending[tag] = (started, shard)
        tokens.append(started[4][0, 0])

    def finish_gather(tag, after):
        started, shard = pending.pop(tag)
        _, got = _exchange_wait("agw_" + tag, started, True, after)
        return lax.dynamic_update_index_in_dim(got, shard, me, 0)

    def finish_cols(tag, after):
        g = finish_gather(tag, after)
        return g.transpose(1, 0, 2).reshape(g.shape[1], -1)

    def finish_rows(tag, after):
        g = finish_gather(tag, after)
        return g.reshape(-1, g.shape[2])

    small_sharded = [gla_w_gk, gla_b_gk, ret_norm]
    for i in range(depth):
        j = i // 2
        start_gather(f"ffn1_w1_{i}", ffn1_w1[i].astype(BF))
        start_gather(f"ffn1_w2_{i}", ffn1_w2[i].astype(BF))
        if i % 2 == 0:
            start_gather(f"ab_w_in_{j}", ab_w_in[j].astype(BF))
            if i == 0:
                start_gather("small", _pack(small_sharded))
            start_gather(f"s5_w_glu_{j}", s5_w_glu[j].astype(BF))
            start_gather(f"ab_w_out_{j}", ab_w_out[j].astype(BF))
        else:
            start_gather(f"ret_w_in_{j}", ret_w_in[j].astype(BF))
            start_gather(f"ret_w_out_{j}", ret_w_out[j].astype(BF))
        start_gather(f"ffn2_w1_{i}", ffn2_w1[i].astype(BF))
        start_gather(f"ffn2_w2_{i}", ffn2_w2[i].astype(BF))
    started_all = functools.reduce(lambda a, b: a + b, tokens)
    full = {}

    s5w = s5_d.shape[1]
    g_s5, n_s5 = s5_lambda_re.shape[2], s5_lambda_re.shape[3]
    hs = min(SUBLANE * LANE, g_s5 * n_s5)
    gla_hk = gla_w_gk.shape[-1] * N_DEV
    gla_dk = gla_hk // GLA_HEADS
    gla_hv = gla_norm.shape[1]
    gla_dv = gla_hv // GLA_HEADS
    ret_hv = ret_norm.shape[1] * N_DEV
    ret_dv = ret_hv // RET_HEADS
    ret_hk = (ret_w_in.shape[2] * N_DEV - 2 * ret_hv) // 2
    ret_dk = ret_hk // RET_HEADS
    assert s5w == gla_hv and 2 * gla_hk == s5w, "column blocks of the mixer projection assume these widths"
    assert ret_hv == 2 * ret_hk
    main_w = s5w + 2 * gla_hk + 2 * gla_hv
    gla_tb = _tile(l, 256, GLA_CHUNK)
    ret_chunk = min(RET_CHUNK, l)

    cos_t, sin_t = _rot_tables(l, ret_dk)
    lg_f = jnp.log1p(-jnp.exp2(-5.0 - jnp.arange(RET_HEADS, dtype=F32)))
    lgtab_f = jnp.broadcast_to(lg_f[:, None, None], (RET_HEADS, 1, LANE))
    lgtab_b = jnp.broadcast_to(lg_f[::-1][:, None, None], (RET_HEADS, 1, LANE))
    s5_pre = {}
    for j in range((depth + 1) // 2):
        s5_args = (s5_lambda_re[j], s5_lambda_im[j], s5_b_re[j], s5_b_im[j], s5_c_re[j], s5_c_im[j], s5_log_dt[j])
        (a_tab, bd, cd), s5_vjp = jax.vjp(lambda *a: _s5_chunk_tables(*a, hs), *s5_args)
        s5_pre[j] = (a_tab, bd.astype(BF), cd.astype(BF), s5_vjp)
    tables_done = jnp.stack([cos_t[0, 0], sin_t[0, 0], lgtab_f[0, 0, 0], lgtab_b[0, 0, 0]]
                            + [t[0].reshape(-1)[0] + t[1].reshape(-1)[0].astype(F32) + t[2].reshape(-1)[0].astype(F32) for t in s5_pre.values()])

    saved = []
    cur = xs
    for i in range(depth):
        j = i // 2
        s = {}
        s['x0'] = cur
        s['f1w1'] = finish_cols(f"ffn1_w1_{i}", tables_done if i == 0 else cur)
        g1 = ffn1_norm[i:i + 1] + started_all if i == 0 else ffn1_norm[i:i + 1]
        cur, s['ffn1'], s['f1w2'] = _ffn_fwd(f"l{i}_ffn1", cur, g1, s['f1w1'], lambda after: finish_rows(f"ffn1_w2_{i}", after))
        s['x1'] = cur
        h = _rms_fwd(f"l{i}_mixnorm", cur, mix_norm[i:i + 1])
        s['h'] = h
        if i % 2 == 0:
            w_in = finish_cols(f"ab_w_in_{j}", cur)
            if i == 0:
                got = finish_gather("small", cur)
                flat, off, joined = got.reshape(N_DEV, -1), 0, []
                for a in small_sharded:
                    n = math.prod(a.shape)
                    blk = jnp.moveaxis(flat[:, off:off + n].reshape((N_DEV,) + a.shape), 0, -2)
                    joined.append(blk.reshape(a.shape[:-1] + (N_DEV * a.shape[-1],)))
                    off += n
                full['gla_w_gk'], full['gla_b_gk'], ret_norm_full = joined[0].astype(BF), joined[1], joined[2]
            s['w_glu'], s['w_out'] = finish_rows(f"s5_w_glu_{j}", cur), finish_rows(f"ab_w_out_{j}", cur)
            w_main, w_glo = w_in[:, :main_w], w_in[:, main_w:]
            proj = _mm_plain(f"l{i}_proj", h, w_main, NN, BF, tm=512, tn=1024, tk=d, b_outer=True)
            glo = _mm_plain(f"l{i}_glo", h, w_glo, NN, F32, tm=1024, tn=2 * GLA_RANK, tk=d)
            a_tab, bd16, cd16, s5_vjp = s5_pre[j]
            tm = _tile(l, 512, SUBLANE)
            x_f, y_f = _s5_fwd(f"l{i}_s5_fwd_f", proj, bd16[0], cd16[0], a_tab[0], False)
            x_b, y_b = _s5_fwd(f"l{i}_s5_fwd_b", proj, bd16[1], cd16[1], a_tab[1], True)
            d_row = s5_d[j:j + 1]
            y, s5_out = _glu_fwd(f"l{i}_s5_glu", y_f, y_b, proj, d_row, s['w_glu'])
            zeros_r = jnp.zeros((GLA_RANK, gla_hk), BF)
            w_gk = full['gla_w_gk'][j]
            wgk_f = jnp.concatenate([w_gk[0], zeros_r], axis=0)
            wgk_b = jnp.concatenate([zeros_r, w_gk[1]], axis=0)
            b_gk = full['gla_b_gk'][j]
            g_f, g_b = _gate_fwd(f"l{i}_gla_gate", glo, wgk_f, wgk_b, b_gk[0:1], b_gk[1:2])
            qcol, kcol, vcol, ogcol = s5w // gla_hk, s5w // gla_hk + 1, (s5w + 2 * gla_hk) // gla_hv, (s5w + 2 * gla_hk) // gla_hv + 1
            lin_kw = dict(heads=GLA_HEADS, dk=gla_dk, dv=gla_dv, chunk=GLA_CHUNK, tb=gla_tb, qcol=qcol, kcol=kcol, vcol=vcol,
                          qscale=gla_dk ** -0.5)
            o_f, sp_f = _gla_fwd(f"l{i}_gla_fwd_f", proj, g_f, reverse=False, **lin_kw)
            o_b, sp_b = _gla_fwd(f"l{i}_gla_fwd_b", proj, g_b, reverse=True, **lin_kw)
            gla_out = _headgate_fwd(f"l{i}_gla_out", o_f, o_b, proj, ogcol, gla_norm[j:j + 1], gla_dv)
            w_out = s['w_out']

            def epi_res(acc, e_refs, o_refs, ids):
                o_refs[0][...] = e_refs[0][...] + acc

            row = pl.BlockSpec((tm, d), lambda ii, jj, kk: (ii, 0))
            cur = _mm(f"l{i}_mix_out",
                      [(s5_out, pl.BlockSpec((tm, s5w), lambda ii, jj, kk: (ii, 0)), w_out, pl.BlockSpec((s5w, d), lambda ii, jj, kk: (0, 0))),
                       (gla_out, pl.BlockSpec((tm, gla_hv), lambda ii, jj, kk: (ii, 0)), w_out, pl.BlockSpec((gla_hv, d), lambda ii, jj, kk: (1, 0)))],
                      NN, (l // tm, 1, 1), [_sds((l, d), F32)], [row], (tm, d), epi_res, [cur], [row])[0]
            s.update(proj=proj, glo=glo, s5_vjp=s5_vjp, a_tab=a_tab, bd16=bd16, cd16=cd16, x_f=x_f, x_b=x_b, y=y, s5_out=s5_out,
                     wgk_f=wgk_f, wgk_b=wgk_b, b_gk=b_gk, g_f=g_f, g_b=g_b, o_f=o_f, o_b=o_b, sp_f=sp_f, sp_b=sp_b, gla_out=gla_out,
                     w_main=w_main, w_glo=w_glo, lin_kw=lin_kw, ogcol=ogcol)
        else:
            w_in = finish_cols(f"ret_w_in_{j}", cur)
            s['w_in'], s['w_out'] = w_in, finish_rows(f"ret_w_out_{j}", cur)
            proj = _mm_plain(f"l{i}_proj", h, w_in, NN, BF, tm=512, tn=1024, tk=d, b_outer=True)
            qr, kr = _rot_apply(f"l{i}_rot", proj, 0, proj, 1, cos_t, sin_t, RET_HEADS, ret_dk, ret_dk ** -0.5, BF, False)
            ret_hb = 4
            lin_kw = dict(heads=RET_HEADS, hb=ret_hb, dk=ret_dk, dv=ret_dv, chunk=ret_chunk, tb=ret_chunk, qcol=0, kcol=0,
                          vcol=(2 * ret_hk) // (ret_hb * ret_dv),
                          qscale=1.0)
            o_f, sp_f = _lin_fwd(f"l{i}_ret_fwd_f", qr, kr, proj, None, lgtab_f, reverse=False, **lin_kw)
            o_b, sp_b = _lin_fwd(f"l{i}_ret_fwd_b", qr, kr, proj, None, lgtab_b, reverse=True, **lin_kw)
            ogcol = (2 * ret_hk + ret_hv) // ret_hv
            r_out = _headgate_fwd(f"l{i}_ret_out", o_f, o_b, proj, ogcol, ret_norm_full, ret_dv)

            def epi_res(acc, e_refs, o_refs, ids):
                o_refs[0][...] = e_refs[0][...] + acc

            cur = _mm_plain(f"l{i}_mix_out", r_out, s['w_out'], NN, F32, tm=512, tn=d, tk=ret_hv, epi=epi_res, eins=[cur])
            s.update(proj=proj, qr=qr, kr=kr, o_f=o_f, o_b=o_b, sp_f=sp_f, sp_b=sp_b, r_out=r_out, lin_kw=lin_kw, ogcol=ogcol)
        s['x2'] = cur
        s['f2w1'] = finish_cols(f"ffn2_w1_{i}", cur)
        cur, s['ffn2'], s['f2w2'] = _ffn_fwd(f"l{i}_ffn2", cur, ffn2_norm[i:i + 1], s['f2w1'], lambda after: finish_rows(f"ffn2_w2_{i}", after))
        saved.append(s)

    dx, d_final_norm, loss_row = _loss_head("loss_head", cur, final_norm.reshape(1, -1), tgt)
    loss = lax.psum(loss_row[0, 0], ("x", "y", "c"))

    G = {}
    big = {}
    G['final_norm'] = d_final_norm.reshape(-1)
    per_layer = {n: [None] * depth for n in ['ffn1_norm', 'mix_norm', 'ffn2_norm']}
    small_late = ['ffn1_norm', 'mix_norm']
    small_early = ['ffn2_norm', 's5_lambda_re', 's5_lambda_im', 's5_b_re', 's5_b_im', 's5_c_re', 's5_c_im',
                   's5_log_dt', 's5_d', 'gla_w_gk', 'gla_b_gk', 'gla_norm', 'ret_norm', 'final_norm']
    a2a, tok = {}, [jnp.zeros((), F32)]

    def start_a2a(tag, blocks):
        started = _exchange_start("a2as_" + tag, blocks, False)
        a2a[tag] = started
        tok[0] = tok[0] + started[4][0, 0]

    def dep(vec):
        return vec + tok[0]

    for i in reversed(range(depth)):
        j = i // 2
        s = saved[i]
        def ffn_grads(which):
            def on_grads(dw1, dw2, gnorm):
                start_a2a(f"{which}_w1_{i}", _col_blocks(dw1))
                start_a2a(f"{which}_w2_{i}", _row_blocks(dw2))
                return dep(gnorm)
            return on_grads

        dx, dg = _ffn_bwd(f"l{i}_ffn2b", dx, s['x2'], ffn2_norm[i:i + 1], s['f2w1'], s['f2w2'], s['ffn2'], ffn_grads("ffn2"))
        per_layer['ffn2_norm'][i] = dg[0]
        tm = _tile(l, 512, SUBLANE)
        row = pl.BlockSpec((tm, d), lambda ii, jj, kk: (ii, 0))
        vec = pl.BlockSpec((1, d), lambda ii, jj, kk: (0, 0))
        if i % 2 == 0:
            proj, lin_kw = s['proj'], s['lin_kw']
            w_out = s['w_out']
            d_cat = _mm_plain(f"l{i}_dcat", dx, w_out, NT, BF, tm=512, tn=1024, tk=d)
            dwo_a = _mm_plain(f"l{i}_dwout_a", s['s5_out'], dx, TN, BF, tm=s5w, tn=d, tk=512)
            dwo_b = _mm_plain(f"l{i}_dwout_b", s['gla_out'], dx, TN, BF, tm=gla_hv, tn=d, tk=512)
            start_a2a(f"ab_w_out_{j}", _row_blocks(jnp.concatenate([dwo_a, dwo_b], axis=0)))
            do, dog, dgn = _headgate_bwd(f"l{i}_gla_outb", s['o_f'], s['o_b'], proj, s['ogcol'], dep(gla_norm[j:j + 1]), d_cat, 1, gla_dv)
            G['gla_norm'] = dgn
            dq, dk_, dv_, dgf = _gla_bwd(f"l{i}_gla_bwd_f", proj, s['g_f'], s['sp_f'], do, None, reverse=False, **lin_kw)
            dq, dk_, dv_, dgb = _gla_bwd(f"l{i}_gla_bwd_b", proj, s['g_b'], s['sp_b'], do, (dq, dk_, dv_), reverse=True, **lin_kw)
            dglo, dwf, dwb, dbf, dbb = _gate_bwd(f"l{i}_gla_gateb", s['glo'], s['wgk_f'], s['wgk_b'], s['b_gk'][0:1], s['b_gk'][1:2], dgf, dgb)
            G['gla_w_gk'] = jnp.stack([dwf[:GLA_RANK], dwb[GLA_RANK:]], axis=0)[None]
            G['gla_b_gk'] = jnp.concatenate([dbf, dbb], axis=0)[None]
            dy, dwglu = _glu_bwd(f"l{i}_s5_glub", s['y'], s['w_glu'], d_cat, 0)
            start_a2a(f"s5_w_glu_{j}", _row_blocks(dwglu.astype(BF)))
            cd16, bd16, a_tab = s['cd16'], s['bd16'], s['a_tab']
            nt2 = a_tab.shape[2]
            a_conj = a_tab * jnp.where(jnp.arange(nt2) < nt2 // 2, 1.0, -1.0)[None, None, :, None]
            du_f, dbd_f, dcd_f, da_f = _s5_bwd(f"l{i}_s5_bwd_f", proj, dy, s['x_f'], bd16[0], cd16[0], a_conj[0], True)
            du_b, dbd_b, dcd_b, da_b = _s5_bwd(f"l{i}_s5_bwd_b", proj, dy, s['x_b'], bd16[1], cd16[1], a_conj[1], False)
            du, dd = _s5_du(f"l{i}_s5_du", du_f, du_b, dy, proj, s5_d[j:j + 1])
            G['s5_d'] = dd
            cot = (jnp.stack([da_f, da_b]), jnp.stack([dbd_f, dbd_b]), jnp.stack([dcd_f, dcd_b]))
            g_lre, g_lim, g_bre, g_bim, g_cre, g_cim, g_ldt = s['s5_vjp'](cot)
            G['s5_lambda_re'], G['s5_lambda_im'], G['s5_b_re'], G['s5_b_im'] = g_lre[None], g_lim[None], g_bre[None], g_bim[None]
            G['s5_c_re'], G['s5_c_im'], G['s5_log_dt'] = g_cre[None], g_cim[None], g_ldt[None]
            if i == 0:
                G['ffn2_norm'] = jnp.stack(per_layer['ffn2_norm'], axis=0)
                early_packed = _pack([G[n] for n in small_early])
                early_started = _exchange_start("ags_small_grads_early", early_packed, True)
                tok[0] = tok[0] + early_started[4][0, 0]
            dproj = jnp.concatenate([du, dq.astype(BF), dk_.astype(BF), dv_.astype(BF), dog], axis=1)
            r2w = 2 * GLA_RANK
            pairs = [(dproj, pl.BlockSpec((tm, main_w), lambda ii, jj, kk: (ii, 0)), s['w_main'], pl.BlockSpec((d, main_w), lambda ii, jj, kk: (0, 0))),
                     (dglo, pl.BlockSpec((tm, r2w), lambda ii, jj, kk: (ii, 0)), s['w_glo'], pl.BlockSpec((d, r2w), lambda ii, jj, kk: (0, 0)))]
            dx, dg = _mm(f"l{i}_dh", pairs, NT, (l // tm, 1, 1), [_sds((l, d), F32), _sds((1, d), F32)], [row, vec], (tm, d),
                         _rms_bwd_epi(0), [s['x1'], dep(mix_norm[i:i + 1]), dx], [row, vec, row])
            dw_main = _mm_plain(f"l{i}_dwin_main", s['h'], dproj, TN, BF, tm=d, tn=2048, tk=1024)
            dw_glo = _mm_plain(f"l{i}_dwin_glo", s['h'], dglo, TN, BF, tm=d, tn=r2w, tk=512)
            start_a2a(f"ab_w_in_{j}", _col_blocks(jnp.concatenate([dw_main, dw_glo], axis=1)))
        else:
            proj, lin_kw = s['proj'], s['lin_kw']
            w_out = s['w_out']
            d_ro = _mm_plain(f"l{i}_dro", dx, w_out, NT, BF, tm=512, tn=1024, tk=d, b_outer=True)
            dwo = _mm_plain(f"l{i}_dwout", s['r_out'], dx, TN, BF, tm=2048, tn=d, tk=1024)
            start_a2a(f"ret_w_out_{j}", _row_blocks(dwo))
            do, dog, dgn = _headgate_bwd(f"l{i}_ret_outb", s['o_f'], s['o_b'], proj, s['ogcol'], dep(ret_norm_full), d_ro, 0, ret_dv)
            G['ret_norm'] = dgn
            r1 = _lin_bwd(f"l{i}_ret_bwd_f", s['qr'], s['kr'], proj, None, lgtab_f, s['sp_f'], do, None, reverse=False, **lin_kw)
            r2 = _lin_bwd(f"l{i}_ret_bwd_b", s['qr'], s['kr'], proj, None, lgtab_b, s['sp_b'], do, r1, reverse=True, **lin_kw)
            dqr, dkr, dv_ = r2
            dq, dk_ = _rot_apply(f"l{i}_rotb", dqr, 0, dkr, 0, cos_t, sin_t, RET_HEADS, ret_dk, ret_dk ** -0.5, BF, True)
            dproj = jnp.concatenate([dq, dk_, dv_.astype(BF), dog], axis=1)
            dw_in = _mm_plain(f"l{i}_dwin", s['h'], dproj, TN, BF, tm=d, tn=2048, tk=1024)
            start_a2a(f"ret_w_in_{j}", _col_blocks(dw_in))
            ret_in = dproj.shape[1]
            dx, dg = _mm(f"l{i}_dh",
                         [(dproj, pl.BlockSpec((tm, ret_in), lambda ii, jj, kk: (ii, 0)),
                           s['w_in'], pl.BlockSpec((d, ret_in), lambda ii, jj, kk: (0, 0), pipeline_mode=pl.Buffered(1)))],
                         NT, (l // tm, 1, 1), [_sds((l, d), F32), _sds((1, d), F32)], [row, vec], (tm, d),
                         _rms_bwd_epi(0), [s['x1'], dep(mix_norm[i:i + 1]), dx], [row, vec, row])
        per_layer['mix_norm'][i] = dg[0]
        dx, dg = _ffn_bwd(f"l{i}_ffn1b", dx, s['x0'], ffn1_norm[i:i + 1], s['f1w1'], s['f1w2'], s['ffn1'], ffn_grads("ffn1"))
        per_layer['ffn1_norm'][i] = dg[0]
    for n in small_late:
        G[n] = jnp.stack(per_layer[n], axis=0)
    grad_x = dx[None]

    out_g, out_d, out_m, out_v = {}, {}, {}, {}
    small = small_early + small_late
    packed = _pack([G[n] for n in small_late])
    small_started = _exchange_start("ags_small_grads_late", packed, True)

    def big_update(n, layers):
        parts = []
        for i in layers:
            blocks, got = _exchange_wait(f"a2aw_{n}_{i}", a2a.pop(f"{n}_{i}"), False, small_started[4])
            parts.append(lax.dynamic_update_index_in_dim(got, lax.dynamic_index_in_dim(blocks, me, 0, keepdims=False), me, 0))
        out_g[n], out_d[n], out_m[n], out_v[n] = _reduce_adam("upd_" + n, parts, W[n], M[n], V[n])

    for n in ['ffn2_w1', 'ffn2_w2', 'ffn1_w1', 'ffn1_w2']:
        big_update(n, range(depth))
    for n in ['ab_w_in', 's5_w_glu', 'ab_w_out', 'ret_w_in', 'ret_w_out']:
        big_update(n, [0])
    assert not a2a and not pending

    g_full = {}
    for tag, started, mine, group in (("early", early_started, early_packed, small_early), ("late", small_started, packed, small_late)):
        _, gathered = _exchange_wait("agw_small_grads_" + tag, started, True, out_v['ret_w_out'])
        gathered = lax.dynamic_update_index_in_dim(gathered, mine, me, 0)
        summed = _reduce8("sum_small_grads_" + tag, gathered)
        g_full.update(zip(group, _unpack(summed, [G[n] for n in group])))
    g_small = {}
    for n in small:
        gf = g_full[n]
        if n in ('gla_w_gk', 'gla_b_gk', 'ret_norm'):
            width = W[n].shape[-1]
            gf = lax.dynamic_slice_in_dim(gf, me * width, width, axis=gf.ndim - 1)
        g_small[n] = gf.reshape(W[n].shape)
    pw, pg, pm, pv = (_pack([src[n] for n in small]) for src in (W, g_small, M, V))
    pd, pm2, pv2 = _adam_packed("upd_small", pw, pg, pm, pv)
    like = [W[n] for n in small]
    for n, dd_, mm_, vv_ in zip(small, _unpack(pd, like), _unpack(pm2, like), _unpack(pv2, like)):
        out_g[n], out_d[n], out_m[n], out_v[n] = g_small[n], dd_, mm_, vv_

    return (loss, grad_x, *[out_g[n] for n in names], *[out_d[n] for n in names], *[out_m[n] for n in names], *[out_v[n] for n in names])
```

```python
import functools
import math

import jax
import jax.numpy as jnp
from jax import lax
from jax.experimental import pallas as pl
from jax.experimental.pallas import tpu as pltpu

F32 = jnp.float32
BF = jnp.bfloat16
N_DEV = 8
EPS = 1e-6
S5_GROUP = 16
GLA_HEADS = 4
GLA_RANK = 16
GLA_GATE_NORM = 16.0
RET_HEADS = 8
ROPE_BASE = 10000.0
GLA_CHUNK = 64
RET_CHUNK = 256
ADAM_LR, ADAM_B1, ADAM_B2, ADAM_EPS, ADAM_WD, ADAM_STEP = 0.001, 0.9, 0.999, 1e-08, 0.01, 10
VMEM_LIMIT_BYTES = 56 * 1024 * 1024
LANE = 128
SUBLANE = 8

NN = (((1,), (0,)), ((), ()))
NT = (((1,), (1,)), ((), ()))
TN = (((0,), (0,)), ((), ()))


def _tile(n, pref, align):
    if n <= pref:
        return n
    t = (pref // align) * align
    while t >= align:
        if n % t == 0:
            return t
        t -= align
    return n


def _params(sem):
    return pltpu.CompilerParams(dimension_semantics=sem, vmem_limit_bytes=VMEM_LIMIT_BYTES)


def _dot(a, b, dims=NN):
    return lax.dot_general(a.astype(BF), b.astype(BF), dims, preferred_element_type=F32)


def _dot3(m01, g, dims=NN):
    g1 = g.astype(BF)
    r1 = g - g1.astype(F32)
    g2 = r1.astype(BF)
    g3 = (r1 - g2.astype(F32)).astype(BF)
    m = m01.astype(BF)
    return (lax.dot_general(m, g1, dims, preferred_element_type=F32)
            + lax.dot_general(m, g2, dims, preferred_element_type=F32)
            + lax.dot_general(m, g3, dims, preferred_element_type=F32))


def _sigmoid(x):
    return 1.0 / (1.0 + jnp.exp(-x))


def _mm(name, pairs, dims, grid, outs, out_specs, acc_shape, epi=None, eins=(), especs=()):
    n_p, n_e, n_o = len(pairs), len(eins), len(outs)
    nk = grid[2]

    def body(*refs):
        a_refs = refs[0:2 * n_p:2]
        b_refs = refs[1:2 * n_p:2]
        e_refs = refs[2 * n_p:2 * n_p + n_e]
        o_refs = refs[2 * n_p + n_e:2 * n_p + n_e + n_o]
        acc = refs[-1]
        ids = (pl.program_id(0), pl.program_id(1), pl.program_id(2))

        part = _dot(a_refs[0][...], b_refs[0][...], dims)
        for p in range(1, n_p):
            part = part + _dot(a_refs[p][...], b_refs[p][...], dims)

        def finish(total):
            if epi is None:
                o_refs[0][...] = total.astype(o_refs[0].dtype)
            else:
                epi(total, e_refs, o_refs, ids)

        if nk == 1:
            finish(part)
        else:
            @pl.when(ids[2] == 0)
            def _():
                acc[...] = part

            @pl.when(ids[2] > 0)
            def _():
                acc[...] += part

            @pl.when(ids[2] == nk - 1)
            def _():
                finish(acc[...])

    in_specs, args = [], []
    for a, a_spec, b, b_spec in pairs:
        in_specs += [a_spec, b_spec]
        args += [a, b]
    in_specs += list(especs)
    args += list(eins)
    res = pl.pallas_call(
        body, name=name, grid=grid, in_specs=in_specs, out_specs=list(out_specs), out_shape=list(outs),
        scratch_shapes=[pltpu.VMEM(acc_shape, F32)],
        compiler_params=_params(("arbitrary", "arbitrary", "arbitrary")),
    )(*args)
    return res


def _sds(shape, dtype):
    return jax.ShapeDtypeStruct(shape, dtype)


def _mm_plain(name, a, b, dims, out_dtype, tm=512, tn=1024, tk=1024, epi=None, eins=(), especs=None, extra_outs=(), extra_specs=(),
              b_outer=False):
    if dims == NN:
        (m, k), n = a.shape, b.shape[1]
    elif dims == NT:
        (m, k), n = a.shape, b.shape[0]
    else:
        (k, m), n = a.shape, b.shape[1]
    tm, tn = _tile(m, tm, LANE if dims == TN else SUBLANE), _tile(n, tn, LANE)
    tk = _tile(k, tk, SUBLANE if dims == TN else LANE)
    grid = (n // tn, m // tm, k // tk) if b_outer else (m // tm, n // tn, k // tk)

    def spec(block, index):
        if b_outer:
            return pl.BlockSpec(block, lambda j, i, kk: index(i, j, kk))
        return pl.BlockSpec(block, index)

    if dims == NN:
        a_spec = spec((tm, tk), lambda i, j, kk: (i, kk))
        b_spec = spec((tk, tn), lambda i, j, kk: (kk, j))
    elif dims == NT:
        a_spec = spec((tm, tk), lambda i, j, kk: (i, kk))
        b_spec = spec((tn, tk), lambda i, j, kk: (j, kk))
    else:
        a_spec = spec((tk, tm), lambda i, j, kk: (kk, i))
        b_spec = spec((tk, tn), lambda i, j, kk: (kk, j))
    o_spec = spec((tm, tn), lambda i, j, kk: (i, j))
    if especs is None:
        especs = [o_spec] * len(eins)
    else:
        especs = [o_spec if s is None else s for s in especs]
    res = _mm(name, [(a, a_spec, b, b_spec)], dims, grid, [_sds((m, n), out_dtype)] + list(extra_outs),
              [o_spec] + list(extra_specs), (tm, tn), epi, eins, especs)
    return res if extra_outs else res[0]


def _rms_fwd(name, x, g):
    l, d = x.shape
    tm = _tile(l, 1024, SUBLANE)

    def body(x_ref, g_ref, o_ref):
        xv = x_ref[...]
        r = lax.rsqrt(jnp.mean(xv * xv, axis=-1, keepdims=True) + EPS)
        o_ref[...] = (xv * r * g_ref[...]).astype(o_ref.dtype)

    return pl.pallas_call(
        body, name=name, grid=(l // tm,),
        in_specs=[pl.BlockSpec((tm, d), lambda i: (i, 0)), pl.BlockSpec((1, d), lambda i: (0, 0))],
        out_specs=pl.BlockSpec((tm, d), lambda i: (i, 0)), out_shape=_sds((l, d), BF),
        compiler_params=_params(("arbitrary",)),
    )(x, g)


def _rms_bwd_epi(first_axis):
    def epi(acc, e_refs, o_refs, ids):
        x_ref, g_ref, dr_ref = e_refs
        dx_ref, dg_ref = o_refs
        xv = x_ref[...]
        r = lax.rsqrt(jnp.mean(xv * xv, axis=-1, keepdims=True) + EPS)
        xh = xv * r
        dxh = acc * g_ref[...]
        dx_ref[...] = dr_ref[...] + r * (dxh - xh * jnp.mean(dxh * xh, axis=-1, keepdims=True))
        part = jnp.sum(acc * xh, axis=0, keepdims=True)

        @pl.when(ids[first_axis] == 0)
        def _():
            dg_ref[...] = part

        @pl.when(ids[first_axis] > 0)
        def _():
            dg_ref[...] += part

    return epi


def _loss_head(name, x, g, target):
    l, d = x.shape
    tm = _tile(l, 512, SUBLANE)
    n = l // tm

    def body(x_ref, g_ref, t_ref, dx_ref, dg_ref, loss_ref, lacc):
        i = pl.program_id(0)
        xv = x_ref[...]
        r = lax.rsqrt(jnp.mean(xv * xv, axis=-1, keepdims=True) + EPS)
        xh = xv * r
        e = xh * g_ref[...] - t_ref[...]
        dy = e * (1.0 / d)
        dxh = dy * g_ref[...]
        dx_ref[...] = r * (dxh - xh * jnp.mean(dxh * xh, axis=-1, keepdims=True))
        dg_part = jnp.sum(dy * xh, axis=0, keepdims=True)
        l_part = jnp.sum(e * e, axis=0, keepdims=True)

        @pl.when(i == 0)
        def _():
            dg_ref[...] = dg_part
            lacc[...] = l_part

        @pl.when(i > 0)
        def _():
            dg_ref[...] += dg_part
            lacc[...] += l_part

        @pl.when(i == n - 1)
        def _():
            loss_ref[...] = jnp.zeros_like(loss_ref) + jnp.sum(lacc[...]) * (0.5 / d)

    return pl.pallas_call(
        body, name=name, grid=(n,),
        in_specs=[pl.BlockSpec((tm, d), lambda i: (i, 0)), pl.BlockSpec((1, d), lambda i: (0, 0)),
                  pl.BlockSpec((tm, d), lambda i: (i, 0))],
        out_specs=[pl.BlockSpec((tm, d), lambda i: (i, 0)), pl.BlockSpec((1, d), lambda i: (0, 0)),
                   pl.BlockSpec((1, LANE), lambda i: (0, 0))],
        out_shape=[_sds((l, d), F32), _sds((1, d), F32), _sds((1, LANE), F32)],
        scratch_shapes=[pltpu.VMEM((1, d), F32)],
        compiler_params=_params(("arbitrary",)),
    )(x, g, target)


def _ffn_up(name, hn, w1):
    l, d = hn.shape
    f = w1.shape[1] // 2
    tm, tn = _tile(l, 512, SUBLANE), _tile(f, 1408, LANE)
    nj = f // tn

    def body(h_ref, wg_ref, wu_ref, gu_ref, a_ref):
        h = h_ref[...]
        g = jnp.dot(h, wg_ref[...], preferred_element_type=F32)
        u = jnp.dot(h, wu_ref[...], preferred_element_type=F32)
        s = _sigmoid(g)
        gs = g * s
        gu_ref[0] = (u * (s + gs * (1.0 - s))).astype(BF)
        gu_ref[1] = gs.astype(BF)
        a_ref[...] = (gs * u).astype(BF)

    return pl.pallas_call(
        body, name=name, grid=(nj, l // tm),
        in_specs=[pl.BlockSpec((tm, d), lambda j, i: (i, 0)), pl.BlockSpec((d, tn), lambda j, i: (0, j)),
                  pl.BlockSpec((d, tn), lambda j, i: (0, j + nj))],
        out_specs=[pl.BlockSpec((2, tm, tn), lambda j, i: (0, i, j)), pl.BlockSpec((tm, tn), lambda j, i: (i, j))],
        out_shape=[_sds((2, l, f), BF), _sds((l, f), BF)],
        compiler_params=_params(("arbitrary", "arbitrary")),
    )(hn, w1, w1)


def _ffn_fwd(tag, x, gnorm, get_w1, get_w2):
    hn = _rms_fwd(tag + "_norm", x, gnorm)
    w1 = get_w1(hn)
    gu, a = _ffn_up(tag + "_up", hn, w1)
    w2 = get_w2(a)

    def epi(acc, e_refs, o_refs, ids):
        o_refs[0][...] = e_refs[0][...] + 0.5 * acc

    l, d = x.shape
    x_new = _mm_plain(tag + "_down", a, w2, NN, F32, tm=512, tn=d, tk=w2.shape[0], epi=epi, eins=[x])
    return x_new, (hn, gu, a), w1, w2


def _ffn_bwd(tag, dres, x, gnorm, w1, w2, saved, on_grads):
    hn, gu, a = saved
    l, d = x.shape
    f = w2.shape[0]
    tm, tn = _tile(l, 512, SUBLANE), _tile(f, 1408, LANE)
    nj = f // tn

    def epi_gu(acc, e_refs, o_refs, ids):
        da = 0.5 * acc
        o_refs[0][0] = (da * e_refs[0][0].astype(F32)).astype(BF)
        o_refs[0][1] = (da * e_refs[0][1].astype(F32)).astype(BF)

    gu_spec = pl.BlockSpec((2, tm, tn), lambda j, i, kk: (0, i, j))
    dgu = _mm(tag + "_dgu",
              [(dres, pl.BlockSpec((tm, d), lambda j, i, kk: (i, 0)), w2, pl.BlockSpec((tn, d), lambda j, i, kk: (j, 0)))],
              NT, (nj, l // tm, 1), [_sds((2, l, f), BF)], [gu_spec], (tm, tn), epi_gu, [gu], [gu_spec])[0]

    def epi_half(acc, e_refs, o_refs, ids):
        o_refs[0][...] = (0.5 * acc).astype(BF)

    dw2 = _mm_plain(tag + "_dw2", a, dres, TN, BF, tm=1408, tn=d, tk=1024, epi=epi_half)

    tk = _tile(l, 1024, SUBLANE)
    dw1 = _mm(tag + "_dw1",
              [(hn, pl.BlockSpec((tk, d), lambda i, j, kk: (kk, 0)), dgu, pl.BlockSpec((None, tk, f), lambda i, j, kk: (j, kk, 0)))],
              TN, (1, 2, l // tk), [_sds((d, 2 * f), BF)], [pl.BlockSpec((d, f), lambda i, j, kk: (0, j))], (d, f))[0]

    gnorm = on_grads(dw1, dw2, gnorm)
    row = pl.BlockSpec((tm, d), lambda i, j, kk: (i, 0))
    vec = pl.BlockSpec((1, d), lambda i, j, kk: (0, 0))
    once = pl.Buffered(1)
    dx, dg = _mm(tag + "_dhn",
                 [(dgu, pl.BlockSpec((None, tm, f), lambda i, j, kk: (0, i, 0)), w1, pl.BlockSpec((d, f), lambda i, j, kk: (0, 0), pipeline_mode=once)),
                  (dgu, pl.BlockSpec((None, tm, f), lambda i, j, kk: (1, i, 0)), w1, pl.BlockSpec((d, f), lambda i, j, kk: (0, 1), pipeline_mode=once))],
                 NT, (l // tm, 1, 1), [_sds((l, d), F32), _sds((1, d), F32)], [row, vec], (tm, d),
                 _rms_bwd_epi(0), [x, gnorm, dres], [row, vec, row])
    return dx, dg


def _s5_chunk_tables(lam_re, lam_im, b_re, b_im, c_re, c_im, log_dt, hs):
    f32 = F32
    g, n = lam_re.shape[1], lam_re.shape[2]
    p = b_re.shape[-1]
    nch, gpc, nt = (g * n) // hs, hs // n, hs // LANE
    lr = jnp.minimum(lam_re.astype(f32), -1e-4)
    li = lam_im.astype(f32)
    dt = jnp.exp(log_dt.astype(f32))[..., None]
    mag = jnp.exp(lr * dt)
    ar = mag * jnp.cos(li * dt)
    ai = mag * jnp.sin(li * dt)
    den = lr * lr + li * li
    cr = ((ar - 1.0) * lr + ai * li) / den
    ci = (ai * lr - (ar - 1.0) * li) / den
    bbr = cr[..., None] * b_re - ci[..., None] * b_im
    bbi = cr[..., None] * b_im + ci[..., None] * b_re
    a_f = jnp.stack([ar, ai], axis=1).reshape(2, 2, nch, nt, LANE).transpose(0, 2, 1, 3, 4).reshape(2, nch, 2 * nt, LANE)
    rows_g = jnp.arange(gpc * p) // p
    cols_g = (jnp.arange(2 * hs) % hs) // n
    diag = (rows_g[:, None] == cols_g[None, :]).astype(f32)
    bb = jnp.stack([bbr, bbi], axis=1).reshape(2, 2, nch, hs, p)
    bd = jnp.tile(bb.transpose(0, 2, 4, 1, 3).reshape(2, nch, p, 2 * hs), (1, 1, gpc, 1)) * diag
    cc = jnp.stack([c_re, -c_im], axis=1).reshape(2, 2, nch, gpc, p, n)
    cd = jnp.tile(cc.transpose(0, 2, 4, 1, 3, 5).reshape(2, nch, p, 2 * hs), (1, 1, gpc, 1)) * diag
    return a_f, bd, cd


def _fold_store(ref, val, tb, ntiles):
    for s in range(ntiles):
        ref[:, s * SUBLANE:(s + 1) * SUBLANE, :] = val[:, s * LANE:(s + 1) * LANE].reshape(tb // SUBLANE, SUBLANE, LANE)


def _unfold(ref, tb, ntiles):
    return jnp.concatenate([ref[:, s * SUBLANE:(s + 1) * SUBLANE, :].reshape(tb, LANE) for s in range(ntiles)], axis=1)


def _s5_fwd(name, proj, bd, cd, a_f, reverse):
    l = proj.shape[0]
    nch, cu, hs2 = bd.shape
    nt = hs2 // (2 * LANE)
    frows = 2 * nt * SUBLANE
    tb = _tile(l, 512, SUBLANE)
    nb = l // tb

    def body(u_ref, bd_ref, cd_ref, a_ref, xf_ref, y_ref, st):
        r = pl.program_id(1)

        @pl.when(r == 0)
        def _():
            st[...] = jnp.zeros_like(st)

        _fold_store(xf_ref, _dot(u_ref[...], bd_ref[...]), tb, 2 * nt)
        ar, ai = a_ref[0:nt, :], a_ref[nt:2 * nt, :]

        def group(gi, carry):
            rr = (tb // SUBLANE - 1 - gi) if reverse else gi
            sr, si = carry
            for qq in range(SUBLANE):
                q = (SUBLANE - 1 - qq) if reverse else qq
                re_rows, im_rows = pl.ds(q, nt, stride=SUBLANE), pl.ds(nt * SUBLANE + q, nt, stride=SUBLANE)
                nr = ar * sr - ai * si + xf_ref[rr, re_rows, :]
                ni = ar * si + ai * sr + xf_ref[rr, im_rows, :]
                xf_ref[rr, re_rows, :] = nr
                xf_ref[rr, im_rows, :] = ni
                sr, si = nr, ni
            return sr, si

        fin = lax.fori_loop(0, tb // SUBLANE, group, (st[0:nt, :], st[nt:2 * nt, :]))
        st[0:nt, :] = fin[0]
        st[nt:2 * nt, :] = fin[1]
        y_ref[...] = _dot(_unfold(xf_ref, tb, 2 * nt), cd_ref[...], NT)

    def rows(r):
        return (nb - 1 - r) if reverse else r

    return pl.pallas_call(
        body, name=name, grid=(nch, nb),
        in_specs=[pl.BlockSpec((tb, cu), lambda c, r: (rows(r), c)), pl.BlockSpec((None, cu, hs2), lambda c, r: (c, 0, 0)),
                  pl.BlockSpec((None, cu, hs2), lambda c, r: (c, 0, 0)), pl.BlockSpec((None, 2 * nt, LANE), lambda c, r: (c, 0, 0))],
        out_specs=[pl.BlockSpec((tb // SUBLANE, frows, LANE), lambda c, r: (rows(r), c, 0)), pl.BlockSpec((tb, cu), lambda c, r: (rows(r), c))],
        out_shape=[_sds((l // SUBLANE, nch * frows, LANE), F32), _sds((l, nch * cu), F32)],
        scratch_shapes=[pltpu.VMEM((2 * nt, LANE), F32)],
        compiler_params=_params(("arbitrary", "arbitrary")),
    )(proj, bd, cd, a_f)


def _s5_bwd(name, proj, dy, xf, bd, cd, a_conj, reverse):
    l = proj.shape[0]
    nch, cu, hs2 = bd.shape
    nt = hs2 // (2 * LANE)
    frows = 2 * nt * SUBLANE
    tb = _tile(l, 512, SUBLANE)
    nb = l // tb

    def body(u_ref, dy_ref, xs_ref, bd_ref, cd_ref, a_ref, du_ref, dbd_ref, dcd_ref, da_ref, lam, st):
        r = pl.program_id(1)

        @pl.when(r == 0)
        def _():
            st[...] = jnp.zeros_like(st)
            dbd_ref[...] = jnp.zeros_like(dbd_ref)
            dcd_ref[...] = jnp.zeros_like(dcd_ref)
            da_ref[...] = jnp.zeros_like(da_ref)

        dyv = dy_ref[...]
        _fold_store(lam, _dot(dyv, cd_ref[...]), tb, 2 * nt)
        ar, ai = a_ref[0:nt, :], a_ref[nt:2 * nt, :]

        def group(gi, carry):
            rr = (tb // SUBLANE - 1 - gi) if reverse else gi
            sr, si, cr, ci = carry
            for qq in range(SUBLANE):
                q = (SUBLANE - 1 - qq) if reverse else qq
                re_rows, im_rows = pl.ds(q, nt, stride=SUBLANE), pl.ds(nt * SUBLANE + q, nt, stride=SUBLANE)
                xr, xi = xs_ref[rr, re_rows, :], xs_ref[rr, im_rows, :]
                cr = cr + sr * xr + si * xi
                ci = ci + si * xr - sr * xi
                nr = ar * sr - ai * si + lam[rr, re_rows, :]
                ni = ar * si + ai * sr + lam[rr, im_rows, :]
                lam[rr, re_rows, :] = nr
                lam[rr, im_rows, :] = ni
                sr, si = nr, ni
            return sr, si, cr, ci

        zero = jnp.zeros((nt, LANE), F32)
        fin = lax.fori_loop(0, tb // SUBLANE, group, (st[0:nt, :], st[nt:2 * nt, :], zero, zero))
        st[0:nt, :] = fin[0]
        st[nt:2 * nt, :] = fin[1]
        da_ref[0:nt, :] += fin[2]
        da_ref[nt:2 * nt, :] += fin[3]
        lam_u = _unfold(lam, tb, 2 * nt)
        du_ref[...] = _dot(lam_u, bd_ref[...], NT)
        dbd_ref[...] += _dot(u_ref[...], lam_u, TN)
        dcd_ref[...] += _dot(dyv, _unfold(xs_ref, tb, 2 * nt), TN)

    def rows(r):
        return (nb - 1 - r) if reverse else r

    chunk_rows = pl.BlockSpec((tb, cu), lambda c, r: (rows(r), c))
    bd_spec = pl.BlockSpec((None, cu, hs2), lambda c, r: (c, 0, 0))
    cd_spec = bd_spec
    a_spec = pl.BlockSpec((None, 2 * nt, LANE), lambda c, r: (c, 0, 0))
    return pl.pallas_call(
        body, name=name, grid=(nch, nb),
        in_specs=[chunk_rows, chunk_rows, pl.BlockSpec((tb // SUBLANE, frows, LANE), lambda c, r: (rows(r), c, 0)), bd_spec, cd_spec, a_spec],
        out_specs=[chunk_rows, bd_spec, cd_spec, a_spec],
        out_shape=[_sds((l, nch * cu), F32), _sds((nch, cu, hs2), F32), _sds((nch, cu, hs2), F32), _sds((nch, 2 * nt, LANE), F32)],
        scratch_shapes=[pltpu.VMEM((tb // SUBLANE, frows, LANE), F32), pltpu.VMEM((2 * nt, LANE), F32)],
        compiler_params=_params(("arbitrary", "arbitrary")),
    )(proj, dy, xf, bd, cd, a_conj)


def _s5_du(name, du_f, du_b, dy, proj, d_row):
    l, w = dy.shape
    tm = _tile(l, 1024, SUBLANE)

    def body(f_ref, b_ref, dy_ref, u_ref, d_ref, du_ref, dd_ref):
        i = pl.program_id(0)
        dyv = dy_ref[...]
        du_ref[...] = (f_ref[...] + b_ref[...] + dyv * d_ref[...]).astype(du_ref.dtype)
        part = jnp.sum(dyv * u_ref[...].astype(F32), axis=0, keepdims=True)

        @pl.when(i == 0)
        def _():
            dd_ref[...] = part

        @pl.when(i > 0)
        def _():
            dd_ref[...] += part

    row = pl.BlockSpec((tm, w), lambda i: (i, 0))
    vec = pl.BlockSpec((1, w), lambda i: (0, 0))
    return pl.pallas_call(
        body, name=name, grid=(l // tm,), in_specs=[row, row, row, row, vec], out_specs=[row, vec],
        out_shape=[_sds((l, w), BF), _sds((1, w), F32)],
        compiler_params=_params(("arbitrary",)),
    )(du_f, du_b, dy, proj, d_row)


def _gelu(y):
    c = math.sqrt(2.0 / math.pi)
    return 0.5 * y * (1.0 + jnp.tanh(c * (y + 0.044715 * y * y * y)))


def _gelu_grad(y):
    c = math.sqrt(2.0 / math.pi)
    th = jnp.tanh(c * (y + 0.044715 * y * y * y))
    return 0.5 * (1.0 + th) + 0.5 * y * (1.0 - th * th) * c * (1.0 + 3.0 * 0.044715 * y * y)


def _glu_fwd(name, y_f, y_b, proj, d_row, w):
    l, wd = y_f.shape
    tm = _tile(l, 512, SUBLANE)

    def body(yf_ref, yb_ref, u_ref, d_ref, w_ref, y_ref, o_ref):
        y = yf_ref[...] + yb_ref[...] + u_ref[...].astype(F32) * d_ref[...]
        y_ref[...] = y
        gy = _gelu(y)
        z = _dot(gy, w_ref[...])
        o_ref[...] = (gy * _sigmoid(z)).astype(o_ref.dtype)

    row = pl.BlockSpec((tm, wd), lambda i: (i, 0))
    return pl.pallas_call(
        body, name=name, grid=(l // tm,),
        in_specs=[row, row, row, pl.BlockSpec((1, wd), lambda i: (0, 0)), pl.BlockSpec((wd, wd), lambda i: (0, 0))],
        out_specs=[row, row], out_shape=[_sds((l, wd), F32), _sds((l, wd), BF)],
        compiler_params=_params(("arbitrary",)),
    )(y_f, y_b, proj, d_row, w)


def _glu_bwd(name, y, w, dout, dcol):
    l, wd = y.shape
    tm = _tile(l, 512, SUBLANE)

    def body(y_ref, w_ref, d_ref, dy_ref, dw_ref):
        i = pl.program_id(0)
        yv = y_ref[...]
        gy = _gelu(yv)
        s = _sigmoid(_dot(gy, w_ref[...]))
        d = d_ref[...].astype(F32)
        t = d * gy * s * (1.0 - s)
        dgy = d * s + _dot(t, w_ref[...], NT)
        dy_ref[...] = dgy * _gelu_grad(yv)
        part = _dot(gy, t, TN)

        @pl.when(i == 0)
        def _():
            dw_ref[...] = part

        @pl.when(i > 0)
        def _():
            dw_ref[...] += part

    return pl.pallas_call(
        body, name=name, grid=(l // tm,),
        in_specs=[pl.BlockSpec((tm, wd), lambda i: (i, 0)), pl.BlockSpec((wd, wd), lambda i: (0, 0)),
                  pl.BlockSpec((tm, wd), lambda i: (i, dcol))],
        out_specs=[pl.BlockSpec((tm, wd), lambda i: (i, 0)), pl.BlockSpec((wd, wd), lambda i: (0, 0))],
        out_shape=[_sds((l, wd), F32), _sds((wd, wd), F32)],
        compiler_params=_params(("arbitrary",)),
    )(y, w, dout)


def _log_sigmoid(x):
    return jnp.minimum(x, 0.0) - jnp.log(1.0 + jnp.exp(-jnp.abs(x)))


def _gate_fwd(name, glo, wf, wb, bf, bb):
    l, r2 = glo.shape
    hk = wf.shape[1]
    tm = _tile(l, 1024, SUBLANE)

    def body(x_ref, wf_ref, wb_ref, bf_ref, bb_ref, gf_ref, gb_ref):
        xv = x_ref[...]
        gf_ref[...] = _log_sigmoid(_dot(xv, wf_ref[...]) + bf_ref[...]) * (1.0 / GLA_GATE_NORM)
        gb_ref[...] = _log_sigmoid(_dot(xv, wb_ref[...]) + bb_ref[...]) * (1.0 / GLA_GATE_NORM)

    w_spec = pl.BlockSpec((r2, hk), lambda i: (0, 0))
    b_spec = pl.BlockSpec((1, hk), lambda i: (0, 0))
    o_spec = pl.BlockSpec((tm, hk), lambda i: (i, 0))
    return pl.pallas_call(
        body, name=name, grid=(l // tm,),
        in_specs=[pl.BlockSpec((tm, r2), lambda i: (i, 0)), w_spec, w_spec, b_spec, b_spec],
        out_specs=[o_spec, o_spec], out_shape=[_sds((l, hk), F32), _sds((l, hk), F32)],
        compiler_params=_params(("arbitrary",)),
    )(glo, wf, wb, bf, bb)


def _gate_bwd(name, glo, wf, wb, bf, bb, dgf, dgb):
    l, r2 = glo.shape
    hk = wf.shape[1]
    tm = _tile(l, 1024, SUBLANE)

    def body(x_ref, wf_ref, wb_ref, bf_ref, bb_ref, dgf_ref, dgb_ref, dx_ref, dwf_ref, dwb_ref, dbf_ref, dbb_ref):
        i = pl.program_id(0)
        xv = x_ref[...]
        kf = _dot(xv, wf_ref[...]) + bf_ref[...]
        kb = _dot(xv, wb_ref[...]) + bb_ref[...]
        dkf = dgf_ref[...] * (1.0 / GLA_GATE_NORM) * _sigmoid(-kf)
        dkb = dgb_ref[...] * (1.0 / GLA_GATE_NORM) * _sigmoid(-kb)
        dx_ref[...] = _dot(dkf, wf_ref[...], NT) + _dot(dkb, wb_ref[...], NT)
        parts = (_dot(xv, dkf, TN), _dot(xv, dkb, TN), jnp.sum(dkf, axis=0, keepdims=True), jnp.sum(dkb, axis=0, keepdims=True))
        accs = (dwf_ref, dwb_ref, dbf_ref, dbb_ref)

        @pl.when(i == 0)
        def _():
            for a_, p_ in zip(accs, parts):
                a_[...] = p_

        @pl.when(i > 0)
        def _():
            for a_, p_ in zip(accs, parts):
                a_[...] += p_

    w_spec = pl.BlockSpec((r2, hk), lambda i: (0, 0))
    b_spec = pl.BlockSpec((1, hk), lambda i: (0, 0))
    g_spec = pl.BlockSpec((tm, hk), lambda i: (i, 0))
    x_spec = pl.BlockSpec((tm, r2), lambda i: (i, 0))
    return pl.pallas_call(
        body, name=name, grid=(l // tm,),
        in_specs=[x_spec, w_spec, w_spec, b_spec, b_spec, g_spec, g_spec],
        out_specs=[x_spec, w_spec, w_spec, b_spec, b_spec],
        out_shape=[_sds((l, r2), F32), _sds((r2, hk), F32), _sds((r2, hk), F32), _sds((1, hk), F32), _sds((1, hk), F32)],
        compiler_params=_params(("arbitrary",)),
    )(glo, wf, wb, bf, bb, dgf, dgb)


def _chunk_terms(qc, kc, gc, lg, chunk, reverse):
    ri = lax.broadcasted_iota(jnp.int32, (chunk, chunk), 0)
    ci = lax.broadcasted_iota(jnp.int32, (chunk, chunk), 1)
    if reverse:
        tri = ci >= ri
        mask = ci > ri
    else:
        tri = ci <= ri
        mask = ci <= ri
    if gc is not None:
        cum = _dot3(tri.astype(F32), gc)
        last = cum[0:1, :] if reverse else cum[chunk - 1:chunk, :]
    else:
        pos = lax.broadcasted_iota(jnp.int32, (chunk, 1), 0).astype(F32)
        cum = ((chunk - pos) if reverse else (pos + 1.0)) * lg
        last = chunk * lg
    e = jnp.exp(cum)
    einv = jnp.exp(-cum)
    dec = jnp.exp(last - cum)
    return e, einv, dec, qc * e, kc * einv, kc * dec, jnp.exp(last), mask, tri


def _lin_specs(arr, width, col, tb, nb, reverse, per_head):
    if per_head:
        return pl.BlockSpec((tb, width), lambda h, r: ((nb - 1 - r) if reverse else r, col + h))
    return pl.BlockSpec((tb, width), lambda h, r: ((nb - 1 - r) if reverse else r, col))


def _lin_fwd(name, q, k, v, g, lgtab, *, heads, hb, dk, dv, chunk, tb, qcol, kcol, vcol, qscale, reverse):
    l = q.shape[0]
    nb = l // tb
    ncb = tb // chunk
    ng = heads // hb
    gated = g is not None
    per_head = ng > 1

    def body(*refs):
        if gated:
            q_ref, k_ref, v_ref, g_ref, o_ref, sp_ref, st = refs
        else:
            q_ref, k_ref, v_ref, lg_ref, o_ref, sp_ref, st = refs
        r = pl.program_id(1)

        @pl.when(r == 0)
        def _():
            st[...] = jnp.zeros_like(st)

        for c in range(ncb):
            cc = (ncb - 1 - c) if reverse else c
            rows = pl.ds(cc * chunk, chunk)
            for h in range(hb):
                lg = None if gated else lg_ref[h, :, 0:1]
                qc = q_ref[rows, h * dk:(h + 1) * dk].astype(F32) * qscale
                kc = k_ref[rows, h * dk:(h + 1) * dk].astype(F32)
                vc = v_ref[rows, h * dv:(h + 1) * dv]
                gc = g_ref[rows, h * dk:(h + 1) * dk] if gated else None
                _, _, _, qd, ki, kdec, e_last, mask, _ = _chunk_terms(qc, kc, gc, lg, chunk, reverse)
                a = jnp.where(mask, _dot(qd, ki, NT), 0.0)
                s_t = st[h]
                o_ref[rows, h * dv:(h + 1) * dv] = _dot(a, vc) + _dot(qd, s_t, NT)
                sp_ref[cc, h] = s_t
                st[h] = s_t * e_last + _dot(vc, kdec, TN)

    in_specs = [_lin_specs(q, hb * dk, qcol, tb, nb, reverse, per_head), _lin_specs(k, hb * dk, kcol, tb, nb, reverse, per_head),
                _lin_specs(v, hb * dv, vcol, tb, nb, reverse, per_head)]
    args = [q, k, v]
    if gated:
        in_specs.append(_lin_specs(g, hb * dk, 0, tb, nb, reverse, per_head))
        args.append(g)
    else:
        in_specs.append(pl.BlockSpec((hb, 1, LANE), lambda h, r: (h, 0, 0)))
        args.append(lgtab)
    out_specs = [_lin_specs(None, hb * dv, 0, tb, nb, reverse, per_head),
                 pl.BlockSpec((ncb, hb, dv, dk), lambda h, r: ((nb - 1 - r) if reverse else r, h, 0, 0))]
    outs = [_sds((l, heads * dv), F32), _sds((l // chunk, heads, dv, dk), F32)]
    return pl.pallas_call(
        body, name=name, grid=(ng, nb), in_specs=in_specs, out_specs=out_specs, out_shape=outs,
        scratch_shapes=[pltpu.VMEM((hb, dv, dk), F32)],
        compiler_params=_params(("arbitrary", "arbitrary")),
    )(*args)


def _lin_bwd(name, q, k, v, g, lgtab, sprev, do, prev, *, heads, hb, dk, dv, chunk, tb, qcol, kcol, vcol, qscale, reverse):
    l = q.shape[0]
    nb = l // tb
    ncb = tb // chunk
    ng = heads // hb
    gated = g is not None
    per_head = ng > 1
    brev = not reverse
    n_prev = 0 if prev is None else len(prev)

    def body(*refs):
        q_ref, k_ref, v_ref, x_ref, sp_ref, do_ref = refs[:6]
        p_refs = refs[6:6 + n_prev]
        o_refs = refs[6 + n_prev:-1]
        dst = refs[-1]
        dq_ref, dk_ref, dv_ref = o_refs[:3]
        r = pl.program_id(1)

        @pl.when(r == 0)
        def _():
            dst[...] = jnp.zeros_like(dst)

        for c in range(ncb):
            cc = (ncb - 1 - c) if brev else c
            rows = pl.ds(cc * chunk, chunk)
            for h in range(hb):
                lg = None if gated else x_ref[h, :, 0:1]
                ks = slice(h * dk, (h + 1) * dk)
                vs = slice(h * dv, (h + 1) * dv)
                qc = q_ref[rows, ks].astype(F32) * qscale
                kc = k_ref[rows, ks].astype(F32)
                vc = v_ref[rows, vs]
                gc = x_ref[rows, ks] if gated else None
                e, einv, dec, qd, ki, kdec, e_last, mask, tri = _chunk_terms(qc, kc, gc, lg, chunk, reverse)
                a = jnp.where(mask, _dot(qd, ki, NT), 0.0)
                s_t = sp_ref[cc, h]
                ds_t = dst[h]
                doc = do_ref[rows, vs]
                dvc = _dot(a, doc, TN) + _dot(kdec, ds_t, NT)
                da = jnp.where(mask, _dot(doc, vc, NT), 0.0)
                dqd = _dot(da, ki) + _dot(doc, s_t)
                dki = _dot(da, qd, TN)
                dkdec = _dot(vc, ds_t)
                dst[h] = ds_t * e_last + _dot(doc, qd, TN)
                dqc = dqd * e * qscale
                dkc = dki * einv + dkdec * dec
                if n_prev:
                    dqc = dqc + p_refs[0][rows, ks]
                    dkc = dkc + p_refs[1][rows, ks]
                    dvc = dvc + p_refs[2][rows, vs]
                dq_ref[rows, ks] = dqc
                dk_ref[rows, ks] = dkc
                dv_ref[rows, vs] = dvc.astype(dv_ref.dtype)
                if gated:
                    dcum = dqd * qd - dki * ki - dkdec * kdec
                    dlast = jnp.sum(dkdec * kdec, axis=0, keepdims=True) + e_last * jnp.sum(s_t * ds_t, axis=0, keepdims=True)
                    rid = lax.broadcasted_iota(jnp.int32, (chunk, 1), 0)
                    dcum = dcum + jnp.where(rid == (0 if reverse else chunk - 1), dlast, 0.0)
                    dgc = _dot3(tri.astype(F32), dcum, TN)
                    o_refs[3][rows, ks] = dgc

    def spec(width, col):
        return _lin_specs(None, width, col, tb, nb, brev, per_head)

    in_specs = [spec(hb * dk, qcol), spec(hb * dk, kcol), spec(hb * dv, vcol)]
    args = [q, k, v]
    if gated:
        in_specs.append(spec(hb * dk, 0))
        args.append(g)
    else:
        in_specs.append(pl.BlockSpec((hb, 1, LANE), lambda h, r: (h, 0, 0)))
        args.append(lgtab)
    in_specs.append(pl.BlockSpec((ncb, hb, dv, dk), lambda h, r: ((nb - 1 - r) if brev else r, h, 0, 0)))
    args.append(sprev)
    in_specs.append(spec(hb * dv, 0))
    args.append(do)
    out_specs = [spec(hb * dk, 0), spec(hb * dk, 0), spec(hb * dv, 0)]
    outs = [_sds((l, heads * dk), F32), _sds((l, heads * dk), F32), _sds((l, heads * dv), BF if n_prev else F32)]
    if gated:
        out_specs.append(spec(hb * dk, 0))
        outs.append(_sds((l, heads * dk), F32))
    if n_prev:
        in_specs += out_specs[:3]
        args += list(prev)
    return pl.pallas_call(
        body, name=name, grid=(ng, nb), in_specs=in_specs, out_specs=out_specs, out_shape=outs,
        scratch_shapes=[pltpu.VMEM((hb, dv, dk), F32)],
        compiler_params=_params(("arbitrary", "arbitrary")),
    )(*args)


def _log2(n):
    assert n & (n - 1) == 0, "a power of two"
    return n.bit_length() - 1


def _gla_block_terms(q, k, g, qscale, chunk, tb, reverse):
    ri = lax.broadcasted_iota(jnp.int32, (tb, tb), 0)
    ci = lax.broadcasted_iota(jnp.int32, (tb, tb), 1)
    same = jnp.right_shift(ri, _log2(chunk)) == jnp.right_shift(ci, _log2(chunk))
    t_in = jnp.logical_and(same, (ci >= ri) if reverse else (ci <= ri)).astype(F32)
    cum = _dot3(t_in, g)
    tot = _dot3(same.astype(F32), g)
    e = jnp.exp(cum)
    einv = jnp.exp(-cum)
    dec = jnp.exp(tot - cum)
    return e, einv, dec, jnp.exp(tot), q * (qscale * e), k * einv, k * dec, t_in


def _gla_masks(hk, dk, heads, chunk, reverse):
    lane = lax.broadcasted_iota(jnp.int32, (1, hk), 1)
    head_of = jnp.right_shift(lane, _log2(dk))
    ri = lax.broadcasted_iota(jnp.int32, (chunk, chunk), 0)
    ci = lax.broadcasted_iota(jnp.int32, (chunk, chunk), 1)
    return [head_of == h for h in range(heads)], ((ci > ri) if reverse else (ci <= ri))


def _gla_fwd(name, proj, g, *, heads, dk, dv, chunk, tb, qcol, kcol, vcol, qscale, reverse):
    l = proj.shape[0]
    nb, ncb, hk, hv = l // tb, tb // chunk, heads * dk, heads * dv

    def body(q_ref, k_ref, v_ref, g_ref, o_ref, sp_ref, st):
        @pl.when(pl.program_id(0) == 0)
        def _():
            st[...] = jnp.zeros_like(st)

        _, _, _, etot, qd, ki, kdec, _ = _gla_block_terms(q_ref[...].astype(F32), k_ref[...].astype(F32), g_ref[...], qscale, chunk, tb, reverse)
        heads_m, causal = _gla_masks(hk, dk, heads, chunk, reverse)
        s_all = st[...]
        for c in range(ncb):
            cc = (ncb - 1 - c) if reverse else c
            rc = slice(cc * chunk, (cc + 1) * chunk)
            qd_c, ki_c, kdec_c = qd[rc], ki[rc], kdec[rc]
            sp_ref[cc] = s_all
            kv = jnp.zeros_like(s_all)
            for h in range(heads):
                vs = slice(h * dv, (h + 1) * dv)
                qm = jnp.where(heads_m[h], qd_c, 0.0)
                a = jnp.where(causal, _dot(qm, ki_c, NT), 0.0)
                vc = v_ref[rc, vs]
                o_ref[rc, vs] = _dot(a, vc) + _dot(qm, s_all, NT)
                kv = kv + jnp.where(heads_m[h], _dot(vc, kdec_c, TN), 0.0)
            s_all = s_all * etot[rc][0:1, :] + kv
        st[...] = s_all

    def rows(r):
        return (nb - 1 - r) if reverse else r

    return pl.pallas_call(
        body, name=name, grid=(nb,),
        in_specs=[pl.BlockSpec((tb, hk), lambda r: (rows(r), qcol)), pl.BlockSpec((tb, hk), lambda r: (rows(r), kcol)),
                  pl.BlockSpec((tb, hv), lambda r: (rows(r), vcol)), pl.BlockSpec((tb, hk), lambda r: (rows(r), 0))],
        out_specs=[pl.BlockSpec((tb, hv), lambda r: (rows(r), 0)), pl.BlockSpec((ncb, dv, hk), lambda r: (rows(r), 0, 0))],
        out_shape=[_sds((l, hv), F32), _sds((l // chunk, dv, hk), F32)],
        scratch_shapes=[pltpu.VMEM((dv, hk), F32)],
        compiler_params=_params(("arbitrary",)),
    )(proj, proj, proj, g)


def _gla_bwd(name, proj, g, sprev, do, prev, *, heads, dk, dv, chunk, tb, qcol, kcol, vcol, qscale, reverse):
    l = proj.shape[0]
    nb, ncb, hk, hv = l // tb, tb // chunk, heads * dk, heads * dv
    brev = not reverse
    n_prev = 0 if prev is None else len(prev)

    def body(q_ref, k_ref, v_ref, g_ref, sp_ref, do_ref, *rest):
        p_refs = rest[:n_prev]
        dq_ref, dk_ref, dv_ref, dg_ref, dst, dcs = rest[n_prev:]

        @pl.when(pl.program_id(0) == 0)
        def _():
            dst[...] = jnp.zeros_like(dst)

        e, einv, dec, etot, qd, ki, kdec, t_in = _gla_block_terms(q_ref[...].astype(F32), k_ref[...].astype(F32), g_ref[...], qscale, chunk, tb, reverse)
        heads_m, causal = _gla_masks(hk, dk, heads, chunk, reverse)
        last_row = lax.broadcasted_iota(jnp.int32, (chunk, 1), 0) == (0 if reverse else chunk - 1)
        ds_all = dst[...]
        for c in range(ncb):
            cc = (ncb - 1 - c) if brev else c
            rc = slice(cc * chunk, (cc + 1) * chunk)
            qd_c, ki_c, kdec_c = qd[rc], ki[rc], kdec[rc]
            s_all = sp_ref[cc]
            et = etot[rc][0:1, :]
            dqd = jnp.zeros((chunk, hk), F32)
            dki = jnp.zeros((chunk, hk), F32)
            dkdec = jnp.zeros((chunk, hk), F32)
            ds_add = jnp.zeros_like(ds_all)
            for h in range(heads):
                vs = slice(h * dv, (h + 1) * dv)
                m = heads_m[h]
                qm = jnp.where(m, qd_c, 0.0)
                a = jnp.where(causal, _dot(qm, ki_c, NT), 0.0)
                doc, vc = do_ref[rc, vs], v_ref[rc, vs]
                dvc = _dot(a, doc, TN) + _dot(jnp.where(m, kdec_c, 0.0), ds_all, NT)
                if n_prev:
                    dvc = dvc + p_refs[2][rc, vs]
                dv_ref[rc, vs] = dvc.astype(dv_ref.dtype)
                da = jnp.where(causal, _dot(doc, vc, NT), 0.0)
                dqd = dqd + jnp.where(m, _dot(da, ki_c) + _dot(doc, s_all), 0.0)
                dki = dki + _dot(da, qm, TN)
                dkdec = dkdec + jnp.where(m, _dot(vc, ds_all), 0.0)
                ds_add = ds_add + _dot(doc, qm, TN)
            dqc = dqd * e[rc] * qscale
            dkc = dki * einv[rc] + dkdec * dec[rc]
            if n_prev:
                dqc = dqc + p_refs[0][rc, :]
                dkc = dkc + p_refs[1][rc, :]
            dq_ref[rc, :] = dqc
            dk_ref[rc, :] = dkc
            dlast = jnp.sum(dkdec * kdec_c, axis=0, keepdims=True) + et * jnp.sum(s_all * ds_all, axis=0, keepdims=True)
            dcs[rc, :] = dqd * qd_c - dki * ki_c - dkdec * kdec_c + jnp.where(last_row, dlast, 0.0)
            ds_all = ds_all * et + ds_add
        dst[...] = ds_all
        dg_ref[...] = _dot3(t_in, dcs[...], TN)

    def rows(r):
        return (nb - 1 - r) if brev else r

    k_spec = pl.BlockSpec((tb, hk), lambda r: (rows(r), 0))
    v_spec = pl.BlockSpec((tb, hv), lambda r: (rows(r), 0))
    in_specs = [pl.BlockSpec((tb, hk), lambda r: (rows(r), qcol)), pl.BlockSpec((tb, hk), lambda r: (rows(r), kcol)),
                pl.BlockSpec((tb, hv), lambda r: (rows(r), vcol)), k_spec,
                pl.BlockSpec((ncb, dv, hk), lambda r: (rows(r), 0, 0)), v_spec]
    args = [proj, proj, proj, g, sprev, do]
    if n_prev:
        in_specs += [k_spec, k_spec, v_spec]
        args += list(prev)
    return pl.pallas_call(
        body, name=name, grid=(nb,), in_specs=in_specs, out_specs=[k_spec, k_spec, v_spec, k_spec],
        out_shape=[_sds((l, hk), F32), _sds((l, hk), F32), _sds((l, hv), BF if n_prev else F32), _sds((l, hk), F32)],
        scratch_shapes=[pltpu.VMEM((dv, hk), F32), pltpu.VMEM((tb, hk), F32)],
        compiler_params=_params(("arbitrary",)),
    )(*args)


def _headgate_fwd(name, o_f, o_b, og_arr, og_col, gn, dv):
    l, w = o_f.shape
    tm = _tile(l, 512, SUBLANE)
    nh = w // dv

    def body(of_ref, ob_ref, og_ref, gn_ref, out_ref):
        for h in range(nh):
            cs = slice(h * dv, (h + 1) * dv)
            o = of_ref[:, cs] + ob_ref[:, cs]
            r = lax.rsqrt(jnp.mean(o * o, axis=-1, keepdims=True) + EPS)
            og = og_ref[:, cs].astype(F32)
            out_ref[:, cs] = (o * r * gn_ref[:, cs] * (og * _sigmoid(og))).astype(out_ref.dtype)

    row = pl.BlockSpec((tm, w), lambda i: (i, 0))
    return pl.pallas_call(
        body, name=name, grid=(l // tm,),
        in_specs=[row, row, pl.BlockSpec((tm, w), lambda i: (i, og_col)), pl.BlockSpec((1, w), lambda i: (0, 0))],
        out_specs=row, out_shape=_sds((l, w), BF),
        compiler_params=_params(("arbitrary",)),
    )(o_f, o_b, og_arr, gn)


def _headgate_bwd(name, o_f, o_b, og_arr, og_col, gn, dout, dcol, dv):
    l, w = o_f.shape
    tm = _tile(l, 512, SUBLANE)
    nh = w // dv

    def body(of_ref, ob_ref, og_ref, gn_ref, d_ref, do_ref, dog_ref, dgn_ref):
        i = pl.program_id(0)
        for h in range(nh):
            cs = slice(h * dv, (h + 1) * dv)
            o = of_ref[:, cs] + ob_ref[:, cs]
            r = lax.rsqrt(jnp.mean(o * o, axis=-1, keepdims=True) + EPS)
            oh = o * r
            og = og_ref[:, cs].astype(F32)
            s = _sigmoid(og)
            d = d_ref[:, cs].astype(F32)
            gnv = gn_ref[:, cs]
            d_on = d * (og * s)
            dog_ref[:, cs] = (d * (oh * gnv) * s * (1.0 + og * (1.0 - s))).astype(dog_ref.dtype)
            doh = d_on * gnv
            do_ref[:, cs] = r * (doh - oh * jnp.mean(doh * oh, axis=-1, keepdims=True))
            part = jnp.sum(d_on * oh, axis=0, keepdims=True)

            @pl.when(i == 0)
            def _():
                dgn_ref[:, cs] = part

            @pl.when(i > 0)
            def _():
                dgn_ref[:, cs] += part

    row = pl.BlockSpec((tm, w), lambda i: (i, 0))
    vec = pl.BlockSpec((1, w), lambda i: (0, 0))
    return pl.pallas_call(
        body, name=name, grid=(l // tm,),
        in_specs=[row, row, pl.BlockSpec((tm, w), lambda i: (i, og_col)), vec, pl.BlockSpec((tm, w), lambda i: (i, dcol))],
        out_specs=[row, row, vec], out_shape=[_sds((l, w), F32), _sds((l, w), BF), _sds((1, w), F32)],
        compiler_params=_params(("arbitrary",)),
    )(o_f, o_b, og_arr, gn, dout)


def _rot_tables(l, dk):
    half = dk // 2
    pos = jnp.arange(l, dtype=F32)
    inv = jnp.exp(-math.log(ROPE_BASE) * jnp.arange(half, dtype=F32) / half)
    ang = pos[:, None] * inv[None, :]
    cos, sin = jnp.cos(ang), jnp.sin(ang)
    return jnp.concatenate([cos, cos], axis=-1), jnp.concatenate([-sin, sin], axis=-1)


def _rot_apply(name, src_q, qcol, src_k, kcol, cos_t, sin_t, heads, dk, kscale, out_dtype, transpose):
    l = src_q.shape[0]
    w = heads * dk
    tm = _tile(l, 512, SUBLANE)

    def rot(t, cos_v, sin_v):
        if transpose:
            return t * cos_v + pltpu.roll(t * sin_v, dk // 2, 1)
        return t * cos_v + pltpu.roll(t, dk // 2, 1) * sin_v

    def body(q_ref, k_ref, c_ref, s_ref, qo_ref, ko_ref):
        cos_v, sin_v = c_ref[...], s_ref[...]
        for h in range(heads):
            cs = slice(h * dk, (h + 1) * dk)
            qo_ref[:, cs] = rot(q_ref[:, cs].astype(F32), cos_v, sin_v).astype(out_dtype)
            ko_ref[:, cs] = (rot(k_ref[:, cs].astype(F32), cos_v, sin_v) * kscale).astype(out_dtype)

    tab = pl.BlockSpec((tm, dk), lambda i: (i, 0))
    row = pl.BlockSpec((tm, w), lambda i: (i, 0))
    return pl.pallas_call(
        body, name=name, grid=(l // tm,),
        in_specs=[pl.BlockSpec((tm, w), lambda i: (i, qcol)), pl.BlockSpec((tm, w), lambda i: (i, kcol)), tab, tab],
        out_specs=[row, row], out_shape=[_sds((l, w), out_dtype), _sds((l, w), out_dtype)],
        compiler_params=_params(("arbitrary",)),
    )(src_q, src_k, cos_t, sin_t)


def _exchange(name, src, gather):
    shape = src.shape if gather else src.shape[1:]

    def body(src_ref, out_ref, send_sems, recv_sems, local_sem):
        me = _my_index()
        own = pltpu.make_async_copy(src_ref if gather else src_ref.at[me], out_ref.at[me], local_sem)
        own.start()
        copies = _peer_copies(src_ref, out_ref, send_sems, recv_sems, gather)
        for cp in copies:
            cp.start()
        for cp in copies:
            cp.wait_recv()
        for cp in copies:
            cp.wait_send()
        own.wait()

    return pl.pallas_call(
        body, name=name,
        in_specs=[pl.BlockSpec(memory_space=pl.ANY)], out_specs=pl.BlockSpec(memory_space=pl.ANY),
        out_shape=_sds((N_DEV,) + tuple(shape), src.dtype),
        scratch_shapes=[pltpu.SemaphoreType.DMA((N_DEV - 1,)), pltpu.SemaphoreType.DMA((N_DEV - 1,)), pltpu.SemaphoreType.DMA],
        )(src)


def _my_index():
    return 4 * lax.axis_index("x") + 2 * lax.axis_index("y") + lax.axis_index("c")


def _peer_copies(src_ref, out_ref, send_sems, recv_sems, gather):
    x, y, c = lax.axis_index("x"), lax.axis_index("y"), lax.axis_index("c")
    me = 4 * x + 2 * y + c
    copies = []
    for kk in range(1, N_DEV):
        px = (1 - x) if kk & 4 else x
        py = (1 - y) if kk & 2 else y
        pc = (1 - c) if kk & 1 else c
        peer = 4 * px + 2 * py + pc
        copies.append(pltpu.make_async_remote_copy(
            src_ref=src_ref if gather else src_ref.at[peer], dst_ref=out_ref.at[me],
            send_sem=send_sems.at[kk - 1], recv_sem=recv_sems.at[kk - 1],
            device_id=(px, py, pc), device_id_type=pl.DeviceIdType.MESH))
    return copies


_HBM = pl.BlockSpec(memory_space=pltpu.HBM)
_SEM = pl.BlockSpec(memory_space=pltpu.SEMAPHORE)
_EFFECT = pltpu.SideEffectType.DATAFLOW_SIDE_EFFECTING


def _exchange_start(name, src, gather):
    shape = src.shape if gather else src.shape[1:]
    land = lax.empty((N_DEV,) + tuple(shape), src.dtype)

    def body(src_ref, land_ref, send_sems, recv_sems, src_thru, land_thru, token):
        for cp in _peer_copies(src_ref, land_ref, send_sems, recv_sems, gather):
            cp.start()
        token[...] = jnp.zeros_like(token)

    return pl.pallas_call(
        body, name=name,
        out_shape=(pltpu.SemaphoreType.DMA((N_DEV - 1,)), pltpu.SemaphoreType.DMA((N_DEV - 1,)),
                   pltpu.HBM(src.shape, src.dtype), pltpu.HBM(land.shape, land.dtype), _sds((SUBLANE, LANE), F32)),
        in_specs=(_HBM, _HBM), out_specs=(_SEM, _SEM, _HBM, _HBM, pl.BlockSpec(memory_space=pltpu.VMEM)),
        input_output_aliases={0: 2, 1: 3},
        compiler_params=pltpu.CompilerParams(has_side_effects=_EFFECT),
    )(pltpu.with_memory_space_constraint(src, pltpu.HBM), pltpu.with_memory_space_constraint(land, pltpu.HBM))


def _gather_start_all(name, srcs):
    n = len(srcs)
    lands = [lax.empty((N_DEV,) + tuple(s.shape), s.dtype) for s in srcs]

    def body(*refs):
        src_refs, land_refs = refs[:n], refs[n:2 * n]
        send, recv = refs[2 * n:3 * n], refs[3 * n:4 * n]
        token = refs[-1]
        for k in range(n):
            for cp in _peer_copies(src_refs[k], land_refs[k], send[k], recv[k], True):
                cp.start()
        token[...] = jnp.zeros_like(token)

    sem = pltpu.SemaphoreType.DMA((N_DEV - 1,))
    outs = pl.pallas_call(
        body, name=name,
        out_shape=tuple([sem] * (2 * n) + [pltpu.HBM(s.shape, s.dtype) for s in srcs] + [pltpu.HBM(a.shape, a.dtype) for a in lands]
                        + [_sds((SUBLANE, LANE), F32)]),
        in_specs=tuple([_HBM] * (2 * n)), out_specs=tuple([_SEM] * (2 * n) + [_HBM] * (2 * n) + [pl.BlockSpec(memory_space=pltpu.VMEM)]),
        input_output_aliases={k: 2 * n + k for k in range(2 * n)},
        compiler_params=pltpu.CompilerParams(has_side_effects=_EFFECT),
    )(*[pltpu.with_memory_space_constraint(a, pltpu.HBM) for a in list(srcs) + lands])
    return [(outs[k], outs[n + k], outs[2 * n + k], outs[3 * n + k], outs[-1]) for k in range(n)]


def _exchange_wait(name, started, gather, after):
    send_sems, recv_sems, src_thru, land_thru, _ = started

    def body(src_ref, land_ref, send_sems, recv_sems, after_ref, src_out, land_out):
        copies = _peer_copies(src_ref, land_ref, send_sems, recv_sems, gather)
        for cp in copies:
            cp.wait_send()
        for cp in copies:
            cp.wait_recv()

    return pl.pallas_call(
        body, name=name,
        out_shape=(pltpu.HBM(src_thru.shape, src_thru.dtype), pltpu.HBM(land_thru.shape, land_thru.dtype)),
        in_specs=(_HBM, _HBM, _SEM, _SEM, pl.BlockSpec(memory_space=pl.ANY)), out_specs=(_HBM, _HBM),
        input_output_aliases={0: 0, 1: 1},
        compiler_params=pltpu.CompilerParams(has_side_effects=_EFFECT),
    )(src_thru, land_thru, send_sems, recv_sems, after)


def _adam_math(w, gsum, m, v):
    m2 = ADAM_B1 * m + (1.0 - ADAM_B1) * gsum
    v2 = ADAM_B2 * v + (1.0 - ADAM_B2) * (gsum * gsum)
    m_hat = m2 / (1.0 - ADAM_B1 ** ADAM_STEP)
    v_hat = v2 / (1.0 - ADAM_B2 ** ADAM_STEP)
    delta = -ADAM_LR * (m_hat / (jnp.sqrt(v_hat) + ADAM_EPS) + ADAM_WD * w)
    return delta, m2, v2


def _reduce_adam(name, parts, w, m, v):
    nl, r, c = w.shape
    tr = _tile(r, 256, 16)
    nr = r // tr

    def body(*refs):
        p_refs = refs[:nl]
        w_ref, m_ref, v_ref, g_ref, d_ref, m2_ref, v2_ref = refs[nl:]
        for li in range(nl):
            @pl.when(pl.program_id(0) == li)
            def _(p_ref=p_refs[li]):
                gsum = p_ref[0].astype(F32)
                for s in range(1, N_DEV):
                    gsum = gsum + p_ref[s].astype(F32)
                g_ref[...] = gsum
                delta, m2, v2 = _adam_math(w_ref[...], gsum, m_ref[...], v_ref[...])
                d_ref[...] = delta
                m2_ref[...] = m2
                v2_ref[...] = v2

    def part_spec(li):
        return pl.BlockSpec((N_DEV, tr, c), lambda lay, i: (0, jnp.where(lay == li, i, jnp.where(lay < li, 0, nr - 1)), 0))

    row = pl.BlockSpec((None, tr, c), lambda lay, i: (lay, i, 0))
    return pl.pallas_call(
        body, name=name, grid=(nl, nr),
        in_specs=[part_spec(li) for li in range(nl)] + [row, row, row],
        out_specs=[row, row, row, row], out_shape=[_sds((nl, r, c), F32)] * 4,
        compiler_params=_params(("arbitrary", "arbitrary")),
    )(*parts, w, m, v)


def _reduce8(name, parts):
    _, r, c = parts.shape

    def body(p_ref, g_ref):
        gsum = p_ref[0]
        for s in range(1, N_DEV):
            gsum = gsum + p_ref[s]
        g_ref[...] = gsum

    return pl.pallas_call(
        body, name=name, grid=(1,),
        in_specs=[pl.BlockSpec((N_DEV, r, c), lambda i: (0, 0, 0))],
        out_specs=pl.BlockSpec((r, c), lambda i: (0, 0)), out_shape=_sds((r, c), F32),
        compiler_params=_params(("arbitrary",)),
    )(parts)


def _adam_packed(name, w, g, m, v):
    r, c = w.shape

    def body(w_ref, g_ref, m_ref, v_ref, d_ref, m2_ref, v2_ref):
        delta, m2, v2 = _adam_math(w_ref[...], g_ref[...], m_ref[...], v_ref[...])
        d_ref[...] = delta
        m2_ref[...] = m2
        v2_ref[...] = v2

    spec = pl.BlockSpec((r, c), lambda i: (0, 0))
    return pl.pallas_call(
        body, name=name, grid=(1,), in_specs=[spec] * 4, out_specs=[spec] * 3, out_shape=[_sds((r, c), F32)] * 3,
        compiler_params=_params(("arbitrary",)),
    )(w, g, m, v)


def _pack(arrs):
    flat = jnp.concatenate([a.reshape(-1).astype(F32) for a in arrs])
    n = flat.shape[0]
    pad = (-n) % (SUBLANE * LANE)
    return jnp.pad(flat, (0, pad)).reshape(-1, LANE)


def _unpack(packed, like):
    flat = packed.reshape(-1)
    out, off = [], 0
    for a in like:
        n = math.prod(a.shape)
        out.append(flat[off:off + n].reshape(a.shape))
        off += n
    return out


def _row_blocks(full):
    return full.reshape(N_DEV, full.shape[0] // N_DEV, full.shape[1])


def _col_blocks(full):
    r, c = full.shape
    return full.reshape(r, N_DEV, c // N_DEV).transpose(1, 0, 2)


def _cols_natural(blocks):
    n, r, c = blocks.shape
    return blocks.transpose(1, 0, 2).reshape(r, n * c)


def kernel(x, ffn1_norm, ffn1_w1, ffn1_w2, mix_norm, ffn2_norm, ffn2_w1, ffn2_w2, ab_w_in, s5_lambda_re, s5_lambda_im, s5_b_re, s5_b_im, s5_c_re, s5_c_im, s5_log_dt, s5_d, s5_w_glu, gla_w_gk, gla_b_gk, gla_norm, ab_w_out, ret_w_in, ret_norm, ret_w_out, final_norm, loss_target, m_ffn1_norm, m_ffn1_w1, m_ffn1_w2, m_mix_norm, m_ffn2_norm, m_ffn2_w1, m_ffn2_w2, m_ab_w_in, m_s5_lambda_re, m_s5_lambda_im, m_s5_b_re, m_s5_b_im, m_s5_c_re, m_s5_c_im, m_s5_log_dt, m_s5_d, m_s5_w_glu, m_gla_w_gk, m_gla_b_gk, m_gla_norm, m_ab_w_out, m_ret_w_in, m_ret_norm, m_ret_w_out, m_final_norm, v_ffn1_norm, v_ffn1_w1, v_ffn1_w2, v_mix_norm, v_ffn2_norm, v_ffn2_w1, v_ffn2_w2, v_ab_w_in, v_s5_lambda_re, v_s5_lambda_im, v_s5_b_re, v_s5_b_im, v_s5_c_re, v_s5_c_im, v_s5_log_dt, v_s5_d, v_s5_w_glu, v_gla_w_gk, v_gla_b_gk, v_gla_norm, v_ab_w_out, v_ret_w_in, v_ret_norm, v_ret_w_out, v_final_norm):
    names = ['ffn1_norm', 'ffn1_w1', 'ffn1_w2', 'mix_norm', 'ffn2_norm', 'ffn2_w1', 'ffn2_w2', 'ab_w_in', 's5_lambda_re', 's5_lambda_im', 's5_b_re', 's5_b_im', 's5_c_re', 's5_c_im', 's5_log_dt', 's5_d', 's5_w_glu', 'gla_w_gk', 'gla_b_gk', 'gla_norm', 'ab_w_out', 'ret_w_in', 'ret_norm', 'ret_w_out', 'final_norm']
    loc = locals()
    W = {n: loc[n] for n in names}
    M = {n: loc["m_" + n] for n in names}
    V = {n: loc["v_" + n] for n in names}

    me = 4 * lax.axis_index("x") + 2 * lax.axis_index("y") + lax.axis_index("c")
    xs = x[0]
    tgt = loss_target[0]
    l, d = xs.shape
    depth = ffn1_norm.shape[0]

    pending, to_start = {}, []

    def start_gather(tag, shard):
        to_start.append((tag, shard))

    def finish_gather(tag, after):
        started, shard = pending.pop(tag)
        _, got = _exchange_wait("agw_" + tag, started, True, after)
        return lax.dynamic_update_index_in_dim(got, shard, me, 0)

    def finish_cols(tag, after):
        g = finish_gather(tag, after)
        return g.transpose(1, 0, 2).reshape(g.shape[1], -1)

    def finish_rows(tag, after):
        g = finish_gather(tag, after)
        return g.reshape(-1, g.shape[2])

    small_sharded = [gla_w_gk, gla_b_gk, ret_norm]
    for i in range(depth):
        j = i // 2
        start_gather(f"ffn1_w1_{i}", ffn1_w1[i].astype(BF))
        start_gather(f"ffn1_w2_{i}", ffn1_w2[i].astype(BF))
        if i % 2 == 0:
            start_gather(f"ab_w_in_{j}", ab_w_in[j].astype(BF))
            if i == 0:
                start_gather("small", _pack(small_sharded))
            start_gather(f"s5_w_glu_{j}", s5_w_glu[j].astype(BF))
            start_gather(f"ab_w_out_{j}", ab_w_out[j].astype(BF))
        else:
            start_gather(f"ret_w_in_{j}", ret_w_in[j].astype(BF))
            start_gather(f"ret_w_out_{j}", ret_w_out[j].astype(BF))
        start_gather(f"ffn2_w1_{i}", ffn2_w1[i].astype(BF))
        start_gather(f"ffn2_w2_{i}", ffn2_w2[i].astype(BF))
    for (tag, shard), started in zip(to_start, _gather_start_all("ags_weights", [s_ for _, s_ in to_start])):
        pending[tag] = (started, shard)
    started_all = started[4][0, 0]
    full = {}

    s5w = s5_d.shape[1]
    g_s5, n_s5 = s5_lambda_re.shape[2], s5_lambda_re.shape[3]
    hs = min(SUBLANE * LANE, g_s5 * n_s5)
    gla_hk = gla_w_gk.shape[-1] * N_DEV
    gla_dk = gla_hk // GLA_HEADS
    gla_hv = gla_norm.shape[1]
    gla_dv = gla_hv // GLA_HEADS
    ret_hv = ret_norm.shape[1] * N_DEV
    ret_dv = ret_hv // RET_HEADS
    ret_hk = (ret_w_in.shape[2] * N_DEV - 2 * ret_hv) // 2
    ret_dk = ret_hk // RET_HEADS
    assert s5w == gla_hv and 2 * gla_hk == s5w, "column blocks of the mixer projection assume these widths"
    assert ret_hv == 2 * ret_hk
    main_w = s5w + 2 * gla_hk + 2 * gla_hv
    gla_tb = _tile(l, 256, GLA_CHUNK)
    ret_chunk = min(RET_CHUNK, l)

    cos_t, sin_t = _rot_tables(l, ret_dk)
    lg_f = jnp.log1p(-jnp.exp2(-5.0 - jnp.arange(RET_HEADS, dtype=F32)))
    lgtab_f = jnp.broadcast_to(lg_f[:, None, None], (RET_HEADS, 1, LANE))
    lgtab_b = jnp.broadcast_to(lg_f[::-1][:, None, None], (RET_HEADS, 1, LANE))
    s5_pre = {}
    for j in range((depth + 1) // 2):
        s5_args = (s5_lambda_re[j], s5_lambda_im[j], s5_b_re[j], s5_b_im[j], s5_c_re[j], s5_c_im[j], s5_log_dt[j])
        (a_tab, bd, cd), s5_vjp = jax.vjp(lambda *a: _s5_chunk_tables(*a, hs), *s5_args)
        s5_pre[j] = (a_tab, bd.astype(BF), cd.astype(BF), s5_vjp)
    tables_done = jnp.stack([cos_t[0, 0], sin_t[0, 0], lgtab_f[0, 0, 0], lgtab_b[0, 0, 0]]
                            + [t[0].reshape(-1)[0] + t[1].reshape(-1)[0].astype(F32) + t[2].reshape(-1)[0].astype(F32) for t in s5_pre.values()])

    saved = []
    cur = xs
    for i in range(depth):
        j = i // 2
        s = {}
        s['x0'] = cur
        g1 = ffn1_norm[i:i + 1] + started_all if i == 0 else ffn1_norm[i:i + 1]

        def first_w1(after):
            if i == 0:
                after = jnp.concatenate([after[0, 0:1].astype(F32), tables_done])
            return finish_cols(f"ffn1_w1_{i}", after)

        cur, s['ffn1'], s['f1w1'], s['f1w2'] = _ffn_fwd(f"l{i}_ffn1", cur, g1, first_w1, lambda after: finish_rows(f"ffn1_w2_{i}", after))
        s['x1'] = cur
        h = _rms_fwd(f"l{i}_mixnorm", cur, mix_norm[i:i + 1])
        s['h'] = h
        if i % 2 == 0:
            w_in = finish_cols(f"ab_w_in_{j}", cur)
            if i == 0:
                got = finish_gather("small", cur)
                flat, off, joined = got.reshape(N_DEV, -1), 0, []
                for a in small_sharded:
                    n = math.prod(a.shape)
                    blk = jnp.moveaxis(flat[:, off:off + n].reshape((N_DEV,) + a.shape), 0, -2)
                    joined.append(blk.reshape(a.shape[:-1] + (N_DEV * a.shape[-1],)))
                    off += n
                full['gla_w_gk'], full['gla_b_gk'], ret_norm_full = joined[0].astype(BF), joined[1], joined[2]
            s['w_glu'], s['w_out'] = finish_rows(f"s5_w_glu_{j}", cur), finish_rows(f"ab_w_out_{j}", cur)
            w_main, w_glo = w_in[:, :main_w], w_in[:, main_w:]
            proj = _mm_plain(f"l{i}_proj", h, w_main, NN, BF, tm=512, tn=1024, tk=d, b_outer=True)
            glo = _mm_plain(f"l{i}_glo", h, w_glo, NN, F32, tm=1024, tn=2 * GLA_RANK, tk=d)
            a_tab, bd16, cd16, s5_vjp = s5_pre[j]
            tm = _tile(l, 512, SUBLANE)
            x_f, y_f = _s5_fwd(f"l{i}_s5_fwd_f", proj, bd16[0], cd16[0], a_tab[0], False)
            x_b, y_b = _s5_fwd(f"l{i}_s5_fwd_b", proj, bd16[1], cd16[1], a_tab[1], True)
            d_row = s5_d[j:j + 1]
            y, s5_out = _glu_fwd(f"l{i}_s5_glu", y_f, y_b, proj, d_row, s['w_glu'])
            zeros_r = jnp.zeros((GLA_RANK, gla_hk), BF)
            w_gk = full['gla_w_gk'][j]
            wgk_f = jnp.concatenate([w_gk[0], zeros_r], axis=0)
            wgk_b = jnp.concatenate([zeros_r, w_gk[1]], axis=0)
            b_gk = full['gla_b_gk'][j]
            g_f, g_b = _gate_fwd(f"l{i}_gla_gate", glo, wgk_f, wgk_b, b_gk[0:1], b_gk[1:2])
            qcol, kcol, vcol, ogcol = s5w // gla_hk, s5w // gla_hk + 1, (s5w + 2 * gla_hk) // gla_hv, (s5w + 2 * gla_hk) // gla_hv + 1
            lin_kw = dict(heads=GLA_HEADS, dk=gla_dk, dv=gla_dv, chunk=GLA_CHUNK, tb=gla_tb, qcol=qcol, kcol=kcol, vcol=vcol,
                          qscale=gla_dk ** -0.5)
            o_f, sp_f = _gla_fwd(f"l{i}_gla_fwd_f", proj, g_f, reverse=False, **lin_kw)
            o_b, sp_b = _gla_fwd(f"l{i}_gla_fwd_b", proj, g_b, reverse=True, **lin_kw)
            gla_out = _headgate_fwd(f"l{i}_gla_out", o_f, o_b, proj, ogcol, gla_norm[j:j + 1], gla_dv)
            w_out = s['w_out']

            def epi_res(acc, e_refs, o_refs, ids):
                o_refs[0][...] = e_refs[0][...] + acc

            row = pl.BlockSpec((tm, d), lambda ii, jj, kk: (ii, 0))
            cur = _mm(f"l{i}_mix_out",
                      [(s5_out, pl.BlockSpec((tm, s5w), lambda ii, jj, kk: (ii, 0)), w_out, pl.BlockSpec((s5w, d), lambda ii, jj, kk: (0, 0))),
                       (gla_out, pl.BlockSpec((tm, gla_hv), lambda ii, jj, kk: (ii, 0)), w_out, pl.BlockSpec((gla_hv, d), lambda ii, jj, kk: (1, 0)))],
                      NN, (l // tm, 1, 1), [_sds((l, d), F32)], [row], (tm, d), epi_res, [cur], [row])[0]
            s.update(proj=proj, glo=glo, s5_vjp=s5_vjp, a_tab=a_tab, bd16=bd16, cd16=cd16, x_f=x_f, x_b=x_b, y=y, s5_out=s5_out,
                     wgk_f=wgk_f, wgk_b=wgk_b, b_gk=b_gk, g_f=g_f, g_b=g_b, o_f=o_f, o_b=o_b, sp_f=sp_f, sp_b=sp_b, gla_out=gla_out,
                     w_main=w_main, w_glo=w_glo, lin_kw=lin_kw, ogcol=ogcol)
        else:
            w_in = finish_cols(f"ret_w_in_{j}", cur)
            s['w_in'], s['w_out'] = w_in, finish_rows(f"ret_w_out_{j}", cur)
            proj = _mm_plain(f"l{i}_proj", h, w_in, NN, BF, tm=512, tn=1024, tk=d, b_outer=True)
            qr, kr = _rot_apply(f"l{i}_rot", proj, 0, proj, 1, cos_t, sin_t, RET_HEADS, ret_dk, ret_dk ** -0.5, BF, False)
            ret_hb = 4
            lin_kw = dict(heads=RET_HEADS, hb=ret_hb, dk=ret_dk, dv=ret_dv, chunk=ret_chunk, tb=ret_chunk, qcol=0, kcol=0,
                          vcol=(2 * ret_hk) // (ret_hb * ret_dv),
                          qscale=1.0)
            o_f, sp_f = _lin_fwd(f"l{i}_ret_fwd_f", qr, kr, proj, None, lgtab_f, reverse=False, **lin_kw)
            o_b, sp_b = _lin_fwd(f"l{i}_ret_fwd_b", qr, kr, proj, None, lgtab_b, reverse=True, **lin_kw)
            ogcol = (2 * ret_hk + ret_hv) // ret_hv
            r_out = _headgate_fwd(f"l{i}_ret_out", o_f, o_b, proj, ogcol, ret_norm_full, ret_dv)

            def epi_res(acc, e_refs, o_refs, ids):
                o_refs[0][...] = e_refs[0][...] + acc

            cur = _mm_plain(f"l{i}_mix_out", r_out, s['w_out'], NN, F32, tm=512, tn=d, tk=ret_hv, epi=epi_res, eins=[cur])
            s.update(proj=proj, qr=qr, kr=kr, o_f=o_f, o_b=o_b, sp_f=sp_f, sp_b=sp_b, r_out=r_out, lin_kw=lin_kw, ogcol=ogcol)
        s['x2'] = cur
        cur, s['ffn2'], s['f2w1'], s['f2w2'] = _ffn_fwd(f"l{i}_ffn2", cur, ffn2_norm[i:i + 1], lambda after: finish_cols(f"ffn2_w1_{i}", after),
                                                         lambda after: finish_rows(f"ffn2_w2_{i}", after))
        saved.append(s)

    dx, d_final_norm, loss_row = _loss_head("loss_head", cur, final_norm.reshape(1, -1), tgt)
    loss = lax.psum(loss_row[0, 0], ("x", "y", "c"))

    G = {}
    big = {}
    G['final_norm'] = d_final_norm.reshape(-1)
    per_layer = {n: [None] * depth for n in ['ffn1_norm', 'mix_norm', 'ffn2_norm']}
    small_late = ['ffn1_norm', 'mix_norm']
    small_early = ['ffn2_norm', 's5_lambda_re', 's5_lambda_im', 's5_b_re', 's5_b_im', 's5_c_re', 's5_c_im',
                   's5_log_dt', 's5_d', 'gla_w_gk', 'gla_b_gk', 'gla_norm', 'ret_norm', 'final_norm']
    a2a, tok = {}, [jnp.zeros((), F32)]

    def start_a2a(tag, blocks):
        started = _exchange_start("a2as_" + tag, blocks, False)
        a2a[tag] = started
        tok[0] = tok[0] + started[4][0, 0]

    def dep(vec):
        return vec + tok[0]

    def proj_backward(tag, pieces, s, dres, a2a_tag):
        tm = _tile(l, 512, SUBLANE)
        row = pl.BlockSpec((tm, d), lambda ii, jj, kk: (ii, 0))
        vec = pl.BlockSpec((1, d), lambda ii, jj, kk: (0, 0))
        dws = [_mm_plain(f"{tag}_dwin_{k}", s['h'], piece, TN, BF, tm=d, tn=2048, tk=1024) for k, (piece, _, _) in enumerate(pieces)]
        start_a2a(a2a_tag, _col_blocks(jnp.concatenate(dws, axis=1)))
        pairs = []
        for piece, w, col in pieces:
            wd = piece.shape[1]
            pairs.append((piece, pl.BlockSpec((tm, wd), lambda ii, jj, kk: (ii, 0)),
                          w, pl.BlockSpec((d, wd), lambda ii, jj, kk, col=col: (0, col), pipeline_mode=pl.Buffered(1))))
        return _mm(f"{tag}_dh", pairs, NT, (l // tm, 1, 1), [_sds((l, d), F32), _sds((1, d), F32)], [row, vec], (tm, d),
                   _rms_bwd_epi(0), [s['x1'], dep(mix_norm[i:i + 1]), dres], [row, vec, row])

    for i in reversed(range(depth)):
        j = i // 2
        s = saved[i]
        def ffn_grads(which):
            def on_grads(dw1, dw2, gnorm):
                start_a2a(f"{which}_w1_{i}", _col_blocks(dw1))
                start_a2a(f"{which}_w2_{i}", _row_blocks(dw2))
                return dep(gnorm)
            return on_grads

        dx, dg = _ffn_bwd(f"l{i}_ffn2b", dx, s['x2'], ffn2_norm[i:i + 1], s['f2w1'], s['f2w2'], s['ffn2'], ffn_grads("ffn2"))
        per_layer['ffn2_norm'][i] = dg[0]
        tm = _tile(l, 512, SUBLANE)
        row = pl.BlockSpec((tm, d), lambda ii, jj, kk: (ii, 0))
        vec = pl.BlockSpec((1, d), lambda ii, jj, kk: (0, 0))
        if i % 2 == 0:
            proj, lin_kw = s['proj'], s['lin_kw']
            w_out = s['w_out']
            d_cat = _mm_plain(f"l{i}_dcat", dx, w_out, NT, BF, tm=512, tn=1024, tk=d)
            dwo_a = _mm_plain(f"l{i}_dwout_a", s['s5_out'], dx, TN, BF, tm=s5w, tn=d, tk=512)
            dwo_b = _mm_plain(f"l{i}_dwout_b", s['gla_out'], dx, TN, BF, tm=gla_hv, tn=d, tk=512)
            start_a2a(f"ab_w_out_{j}", _row_blocks(jnp.concatenate([dwo_a, dwo_b], axis=0)))
            do, dog, dgn = _headgate_bwd(f"l{i}_gla_outb", s['o_f'], s['o_b'], proj, s['ogcol'], dep(gla_norm[j:j + 1]), d_cat, 1, gla_dv)
            G['gla_norm'] = dgn
            dq, dk_, dv_, dgf = _gla_bwd(f"l{i}_gla_bwd_f", proj, s['g_f'], s['sp_f'], do, None, reverse=False, **lin_kw)
            dq, dk_, dv_, dgb = _gla_bwd(f"l{i}_gla_bwd_b", proj, s['g_b'], s['sp_b'], do, (dq, dk_, dv_), reverse=True, **lin_kw)
            dglo, dwf, dwb, dbf, dbb = _gate_bwd(f"l{i}_gla_gateb", s['glo'], s['wgk_f'], s['wgk_b'], s['b_gk'][0:1], s['b_gk'][1:2], dgf, dgb)
            G['gla_w_gk'] = jnp.stack([dwf[:GLA_RANK], dwb[GLA_RANK:]], axis=0)[None]
            G['gla_b_gk'] = jnp.concatenate([dbf, dbb], axis=0)[None]
            dy, dwglu = _glu_bwd(f"l{i}_s5_glub", s['y'], s['w_glu'], d_cat, 0)
            start_a2a(f"s5_w_glu_{j}", _row_blocks(dwglu.astype(BF)))
            cd16, bd16, a_tab = s['cd16'], s['bd16'], s['a_tab']
            nt2 = a_tab.shape[2]
            a_conj = a_tab * jnp.where(jnp.arange(nt2) < nt2 // 2, 1.0, -1.0)[None, None, :, None]
            du_f, dbd_f, dcd_f, da_f = _s5_bwd(f"l{i}_s5_bwd_f", proj, dy, s['x_f'], bd16[0], cd16[0], a_conj[0], True)
            du_b, dbd_b, dcd_b, da_b = _s5_bwd(f"l{i}_s5_bwd_b", proj, dy, s['x_b'], bd16[1], cd16[1], a_conj[1], False)
            du, dd = _s5_du(f"l{i}_s5_du", du_f, du_b, dy, proj, s5_d[j:j + 1])
            G['s5_d'] = dd
            cot = (jnp.stack([da_f, da_b]), jnp.stack([dbd_f, dbd_b]), jnp.stack([dcd_f, dcd_b]))
            g_lre, g_lim, g_bre, g_bim, g_cre, g_cim, g_ldt = s['s5_vjp'](cot)
            G['s5_lambda_re'], G['s5_lambda_im'], G['s5_b_re'], G['s5_b_im'] = g_lre[None], g_lim[None], g_bre[None], g_bim[None]
            G['s5_c_re'], G['s5_c_im'], G['s5_log_dt'] = g_cre[None], g_cim[None], g_ldt[None]
            if i == 0:
                G['ffn2_norm'] = jnp.stack(per_layer['ffn2_norm'], axis=0)
                early_packed = _pack([G[n] for n in small_early])
                early_started = _exchange_start("ags_small_grads_early", early_packed, True)
                tok[0] = tok[0] + early_started[4][0, 0]
            w_main, w_glo = s['w_main'], s['w_glo']
            pieces = [(du, w_main, 0), (dq, w_main, s5w // gla_hk), (dk_, w_main, s5w // gla_hk + 1),
                      (dv_, w_main, (s5w + 2 * gla_hk) // gla_hv), (dog, w_main, (s5w + 2 * gla_hk) // gla_hv + 1), (dglo, w_glo, 0)]
            dx, dg = proj_backward(f"l{i}", pieces, s, dx, f"ab_w_in_{j}")
        else:
            proj, lin_kw = s['proj'], s['lin_kw']
            w_out = s['w_out']
            d_ro = _mm_plain(f"l{i}_dro", dx, w_out, NT, BF, tm=512, tn=1024, tk=d, b_outer=True)
            dwo = _mm_plain(f"l{i}_dwout", s['r_out'], dx, TN, BF, tm=2048, tn=d, tk=1024)
            start_a2a(f"ret_w_out_{j}", _row_blocks(dwo))
            do, dog, dgn = _headgate_bwd(f"l{i}_ret_outb", s['o_f'], s['o_b'], proj, s['ogcol'], dep(ret_norm_full), d_ro, 0, ret_dv)
            G['ret_norm'] = dgn
            r1 = _lin_bwd(f"l{i}_ret_bwd_f", s['qr'], s['kr'], proj, None, lgtab_f, s['sp_f'], do, None, reverse=False, **lin_kw)
            r2 = _lin_bwd(f"l{i}_ret_bwd_b", s['qr'], s['kr'], proj, None, lgtab_b, s['sp_b'], do, r1, reverse=True, **lin_kw)
            dqr, dkr, dv_ = r2
            dq, dk_ = _rot_apply(f"l{i}_rotb", dqr, 0, dkr, 0, cos_t, sin_t, RET_HEADS, ret_dk, ret_dk ** -0.5, BF, True)
            w_in = s['w_in']
            pieces = [(dq, w_in, 0), (dk_, w_in, 1), (dv_, w_in, (2 * ret_hk) // ret_hv), (dog, w_in, (2 * ret_hk) // ret_hv + 1)]
            dx, dg = proj_backward(f"l{i}", pieces, s, dx, f"ret_w_in_{j}")
        per_layer['mix_norm'][i] = dg[0]
        dx, dg = _ffn_bwd(f"l{i}_ffn1b", dx, s['x0'], ffn1_norm[i:i + 1], s['f1w1'], s['f1w2'], s['ffn1'], ffn_grads("ffn1"))
        per_layer['ffn1_norm'][i] = dg[0]
    for n in small_late:
        G[n] = jnp.stack(per_layer[n], axis=0)
    grad_x = dx[None]

    out_g, out_d, out_m, out_v = {}, {}, {}, {}
    small = small_early + small_late
    packed = _pack([G[n] for n in small_late])
    small_started = _exchange_start("ags_small_grads_late", packed, True)

    def big_update(n, layers):
        parts = []
        for i in layers:
            blocks, got = _exchange_wait(f"a2aw_{n}_{i}", a2a.pop(f"{n}_{i}"), False, small_started[4])
            parts.append(lax.dynamic_update_index_in_dim(got, lax.dynamic_index_in_dim(blocks, me, 0, keepdims=False), me, 0))
        out_g[n], out_d[n], out_m[n], out_v[n] = _reduce_adam("upd_" + n, parts, W[n], M[n], V[n])

    for n in ['ffn2_w1', 'ffn2_w2', 'ffn1_w1', 'ffn1_w2']:
        big_update(n, range(depth))
    for n in ['ab_w_in', 's5_w_glu', 'ab_w_out', 'ret_w_in', 'ret_w_out']:
        big_update(n, [0])
    assert not a2a and not pending

    g_full = {}
    for tag, started, mine, group in (("early", early_started, early_packed, small_early), ("late", small_started, packed, small_late)):
        _, gathered = _exchange_wait("agw_small_grads_" + tag, started, True, out_v['ret_w_out'])
        gathered = lax.dynamic_update_index_in_dim(gathered, mine, me, 0)
        summed = _reduce8("sum_small_grads_" + tag, gathered)
        g_full.update(zip(group, _unpack(summed, [G[n] for n in group])))
    g_small = {}
    for n in small:
        gf = g_full[n]
        if n in ('gla_w_gk', 'gla_b_gk', 'ret_norm'):
            width = W[n].shape[-1]
            gf = lax.dynamic_slice_in_dim(gf, me * width, width, axis=gf.ndim - 1)
        g_small[n] = gf.reshape(W[n].shape)
    pw, pg, pm, pv = (_pack([src[n] for n in small]) for src in (W, g_small, M, V))
    pd, pm2, pv2 = _adam_packed("upd_small", pw, pg, pm, pv)
    like = [W[n] for n in small]
    for n, dd_, mm_, vv_ in zip(small, _unpack(pd, like), _unpack(pm2, like), _unpack(pv2, like)):
        out_g[n], out_d[n], out_m[n], out_v[n] = g_small[n], dd_, mm_, vv_

    return (loss, grad_x, *[out_g[n] for n in names], *[out_d[n] for n in names], *[out_m[n] for n in names], *[out_v[n] for n in names])
```

```python
import functools
import math

import jax
import jax.numpy as jnp
from jax import lax
from jax.experimental import pallas as pl
from jax.experimental.pallas import tpu as pltpu

F32 = jnp.float32
BF = jnp.bfloat16
N_DEV = 8
EPS = 1e-6
S5_GROUP = 16
GLA_HEADS = 4
GLA_RANK = 16
GLA_GATE_NORM = 16.0
RET_HEADS = 8
ROPE_BASE = 10000.0
GLA_CHUNK = 64
RET_CHUNK = 256
ADAM_LR, ADAM_B1, ADAM_B2, ADAM_EPS, ADAM_WD, ADAM_STEP = 0.001, 0.9, 0.999, 1e-08, 0.01, 10
VMEM_LIMIT_BYTES = 56 * 1024 * 1024
LANE = 128
SUBLANE = 8

NN = (((1,), (0,)), ((), ()))
NT = (((1,), (1,)), ((), ()))
TN = (((0,), (0,)), ((), ()))


def _tile(n, pref, align):
    if n <= pref:
        return n
    t = (pref // align) * align
    while t >= align:
        if n % t == 0:
            return t
        t -= align
    return n


def _params(sem):
    return pltpu.CompilerParams(dimension_semantics=sem, vmem_limit_bytes=VMEM_LIMIT_BYTES)


def _dot(a, b, dims=NN):
    return lax.dot_general(a.astype(BF), b.astype(BF), dims, preferred_element_type=F32)


def _dot3(m01, g, dims=NN):
    g1 = g.astype(BF)
    r1 = g - g1.astype(F32)
    g2 = r1.astype(BF)
    g3 = (r1 - g2.astype(F32)).astype(BF)
    m = m01.astype(BF)
    return (lax.dot_general(m, g1, dims, preferred_element_type=F32)
            + lax.dot_general(m, g2, dims, preferred_element_type=F32)
            + lax.dot_general(m, g3, dims, preferred_element_type=F32))


def _sigmoid(x):
    return 1.0 / (1.0 + jnp.exp(-x))


def _mm(name, pairs, dims, grid, outs, out_specs, acc_shape, epi=None, eins=(), especs=()):
    n_p, n_e, n_o = len(pairs), len(eins), len(outs)
    nk = grid[2]

    def body(*refs):
        a_refs = refs[0:2 * n_p:2]
        b_refs = refs[1:2 * n_p:2]
        e_refs = refs[2 * n_p:2 * n_p + n_e]
        o_refs = refs[2 * n_p + n_e:2 * n_p + n_e + n_o]
        acc = refs[-1]
        ids = (pl.program_id(0), pl.program_id(1), pl.program_id(2))

        part = _dot(a_refs[0][...], b_refs[0][...], dims)
        for p in range(1, n_p):
            part = part + _dot(a_refs[p][...], b_refs[p][...], dims)

        def finish(total):
            if epi is None:
                o_refs[0][...] = total.astype(o_refs[0].dtype)
            else:
                epi(total, e_refs, o_refs, ids)

        if nk == 1:
            finish(part)
        else:
            @pl.when(ids[2] == 0)
            def _():
                acc[...] = part

            @pl.when(ids[2] > 0)
            def _():
                acc[...] += part

            @pl.when(ids[2] == nk - 1)
            def _():
                finish(acc[...])

    in_specs, args = [], []
    for a, a_spec, b, b_spec in pairs:
        in_specs += [a_spec, b_spec]
        args += [a, b]
    in_specs += list(especs)
    args += list(eins)
    res = pl.pallas_call(
        body, name=name, grid=grid, in_specs=in_specs, out_specs=list(out_specs), out_shape=list(outs),
        scratch_shapes=[pltpu.VMEM(acc_shape, F32)],
        compiler_params=_params(("arbitrary", "arbitrary", "arbitrary")),
    )(*args)
    return res


def _sds(shape, dtype):
    return jax.ShapeDtypeStruct(shape, dtype)


def _mm_plain(name, a, b, dims, out_dtype, tm=512, tn=1024, tk=1024, epi=None, eins=(), especs=None, extra_outs=(), extra_specs=(),
              b_outer=False):
    if dims == NN:
        (m, k), n = a.shape, b.shape[1]
    elif dims == NT:
        (m, k), n = a.shape, b.shape[0]
    else:
        (k, m), n = a.shape, b.shape[1]
    tm, tn = _tile(m, tm, LANE if dims == TN else SUBLANE), _tile(n, tn, LANE)
    tk = _tile(k, tk, SUBLANE if dims == TN else LANE)
    grid = (n // tn, m // tm, k // tk) if b_outer else (m // tm, n // tn, k // tk)

    def spec(block, index):
        if b_outer:
            return pl.BlockSpec(block, lambda j, i, kk: index(i, j, kk))
        return pl.BlockSpec(block, index)

    if dims == NN:
        a_spec = spec((tm, tk), lambda i, j, kk: (i, kk))
        b_spec = spec((tk, tn), lambda i, j, kk: (kk, j))
    elif dims == NT:
        a_spec = spec((tm, tk), lambda i, j, kk: (i, kk))
        b_spec = spec((tn, tk), lambda i, j, kk: (j, kk))
    else:
        a_spec = spec((tk, tm), lambda i, j, kk: (kk, i))
        b_spec = spec((tk, tn), lambda i, j, kk: (kk, j))
    o_spec = spec((tm, tn), lambda i, j, kk: (i, j))
    if especs is None:
        especs = [o_spec] * len(eins)
    else:
        especs = [o_spec if s is None else s for s in especs]
    res = _mm(name, [(a, a_spec, b, b_spec)], dims, grid, [_sds((m, n), out_dtype)] + list(extra_outs),
              [o_spec] + list(extra_specs), (tm, tn), epi, eins, especs)
    return res if extra_outs else res[0]


def _rms_fwd(name, x, g):
    l, d = x.shape
    tm = _tile(l, 1024, SUBLANE)

    def body(x_ref, g_ref, o_ref):
        xv = x_ref[...]
        r = lax.rsqrt(jnp.mean(xv * xv, axis=-1, keepdims=True) + EPS)
        o_ref[...] = (xv * r * g_ref[...]).astype(o_ref.dtype)

    return pl.pallas_call(
        body, name=name, grid=(l // tm,),
        in_specs=[pl.BlockSpec((tm, d), lambda i: (i, 0)), pl.BlockSpec((1, d), lambda i: (0, 0))],
        out_specs=pl.BlockSpec((tm, d), lambda i: (i, 0)), out_shape=_sds((l, d), BF),
        compiler_params=_params(("arbitrary",)),
    )(x, g)


def _rms_bwd_epi(first_axis):
    def epi(acc, e_refs, o_refs, ids):
        x_ref, g_ref, dr_ref = e_refs
        dx_ref, dg_ref = o_refs
        xv = x_ref[...]
        r = lax.rsqrt(jnp.mean(xv * xv, axis=-1, keepdims=True) + EPS)
        xh = xv * r
        dxh = acc * g_ref[...]
        dx_ref[...] = dr_ref[...] + r * (dxh - xh * jnp.mean(dxh * xh, axis=-1, keepdims=True))
        part = jnp.sum(acc * xh, axis=0, keepdims=True)

        @pl.when(ids[first_axis] == 0)
        def _():
            dg_ref[...] = part

        @pl.when(ids[first_axis] > 0)
        def _():
            dg_ref[...] += part

    return epi


def _loss_head(name, x, g, target):
    l, d = x.shape
    tm = _tile(l, 512, SUBLANE)
    n = l // tm

    def body(x_ref, g_ref, t_ref, dx_ref, dg_ref, loss_ref, lacc):
        i = pl.program_id(0)
        xv = x_ref[...]
        r = lax.rsqrt(jnp.mean(xv * xv, axis=-1, keepdims=True) + EPS)
        xh = xv * r
        e = xh * g_ref[...] - t_ref[...]
        dy = e * (1.0 / d)
        dxh = dy * g_ref[...]
        dx_ref[...] = r * (dxh - xh * jnp.mean(dxh * xh, axis=-1, keepdims=True))
        dg_part = jnp.sum(dy * xh, axis=0, keepdims=True)
        l_part = jnp.sum(e * e, axis=0, keepdims=True)

        @pl.when(i == 0)
        def _():
            dg_ref[...] = dg_part
            lacc[...] = l_part

        @pl.when(i > 0)
        def _():
            dg_ref[...] += dg_part
            lacc[...] += l_part

        @pl.when(i == n - 1)
        def _():
            loss_ref[...] = jnp.zeros_like(loss_ref) + jnp.sum(lacc[...]) * (0.5 / d)

    return pl.pallas_call(
        body, name=name, grid=(n,),
        in_specs=[pl.BlockSpec((tm, d), lambda i: (i, 0)), pl.BlockSpec((1, d), lambda i: (0, 0)),
                  pl.BlockSpec((tm, d), lambda i: (i, 0))],
        out_specs=[pl.BlockSpec((tm, d), lambda i: (i, 0)), pl.BlockSpec((1, d), lambda i: (0, 0)),
                   pl.BlockSpec((1, LANE), lambda i: (0, 0))],
        out_shape=[_sds((l, d), F32), _sds((1, d), F32), _sds((1, LANE), F32)],
        scratch_shapes=[pltpu.VMEM((1, d), F32)],
        compiler_params=_params(("arbitrary",)),
    )(x, g, target)


def _ffn_up(name, hn, w1):
    l, d = hn.shape
    f = w1.shape[1] // 2
    tm, tn = _tile(l, 512, SUBLANE), _tile(f, 1408, LANE)
    nj = f // tn

    def body(h_ref, wg_ref, wu_ref, gu_ref, a_ref):
        h = h_ref[...]
        g = jnp.dot(h, wg_ref[...], preferred_element_type=F32)
        u = jnp.dot(h, wu_ref[...], preferred_element_type=F32)
        s = _sigmoid(g)
        gs = g * s
        gu_ref[0] = (u * (s + gs * (1.0 - s))).astype(BF)
        gu_ref[1] = gs.astype(BF)
        a_ref[...] = (gs * u).astype(BF)

    return pl.pallas_call(
        body, name=name, grid=(nj, l // tm),
        in_specs=[pl.BlockSpec((tm, d), lambda j, i: (i, 0)), pl.BlockSpec((d, tn), lambda j, i: (0, j)),
                  pl.BlockSpec((d, tn), lambda j, i: (0, j + nj))],
        out_specs=[pl.BlockSpec((2, tm, tn), lambda j, i: (0, i, j)), pl.BlockSpec((tm, tn), lambda j, i: (i, j))],
        out_shape=[_sds((2, l, f), BF), _sds((l, f), BF)],
        compiler_params=_params(("arbitrary", "arbitrary")),
    )(hn, w1, w1)


def _residual_epi(scale, with_norm):
    def epi(acc, e_refs, o_refs, ids):
        xn = e_refs[0][...] + scale * acc
        o_refs[0][...] = xn
        if with_norm:
            r = lax.rsqrt(jnp.mean(xn * xn, axis=-1, keepdims=True) + EPS)
            o_refs[1][...] = (xn * r * e_refs[1][...]).astype(o_refs[1].dtype)

    return epi


def _ffn_fwd(tag, x, hn, get_w1, get_w2, next_gnorm):
    w1 = get_w1(hn)
    gu, a = _ffn_up(tag + "_up", hn, w1)
    w2 = get_w2(a)
    l, d = x.shape
    if next_gnorm is None:
        x_new = _mm_plain(tag + "_down", a, w2, NN, F32, tm=512, tn=d, tk=w2.shape[0], epi=_residual_epi(0.5, False), eins=[x])
        hn_next = None
    else:
        vec = pl.BlockSpec((1, d), lambda i, j, kk: (0, 0))
        tm = _tile(l, 512, SUBLANE)
        x_new, hn_next = _mm_plain(tag + "_down", a, w2, NN, F32, tm=512, tn=d, tk=w2.shape[0], epi=_residual_epi(0.5, True),
                                   eins=[x, next_gnorm], especs=[None, vec],
                                   extra_outs=[_sds((l, d), BF)], extra_specs=[pl.BlockSpec((tm, d), lambda i, j, kk: (i, 0))])
    return x_new, hn_next, (hn, gu, a), w1, w2


def _ffn_bwd(tag, dres, x, gnorm, w1, w2, saved, on_grads):
    hn, gu, a = saved
    l, d = x.shape
    f = w2.shape[0]
    tm, tn = _tile(l, 512, SUBLANE), _tile(f, 1408, LANE)
    nj = f // tn

    def epi_gu(acc, e_refs, o_refs, ids):
        da = 0.5 * acc
        o_refs[0][0] = (da * e_refs[0][0].astype(F32)).astype(BF)
        o_refs[0][1] = (da * e_refs[0][1].astype(F32)).astype(BF)

    gu_spec = pl.BlockSpec((2, tm, tn), lambda j, i, kk: (0, i, j))
    dgu = _mm(tag + "_dgu",
              [(dres, pl.BlockSpec((tm, d), lambda j, i, kk: (i, 0)), w2, pl.BlockSpec((tn, d), lambda j, i, kk: (j, 0)))],
              NT, (nj, l // tm, 1), [_sds((2, l, f), BF)], [gu_spec], (tm, tn), epi_gu, [gu], [gu_spec])[0]

    def epi_half(acc, e_refs, o_refs, ids):
        o_refs[0][...] = (0.5 * acc).astype(BF)

    dw2 = _mm_plain(tag + "_dw2", a, dres, TN, BF, tm=1408, tn=d, tk=1024, epi=epi_half)

    tk = _tile(l, 1024, SUBLANE)
    dw1 = _mm(tag + "_dw1",
              [(hn, pl.BlockSpec((tk, d), lambda i, j, kk: (kk, 0)), dgu, pl.BlockSpec((None, tk, f), lambda i, j, kk: (j, kk, 0)))],
              TN, (1, 2, l // tk), [_sds((d, 2 * f), BF)], [pl.BlockSpec((d, f), lambda i, j, kk: (0, j))], (d, f))[0]

    gnorm = on_grads(dw1, dw2, gnorm)
    row = pl.BlockSpec((tm, d), lambda i, j, kk: (i, 0))
    vec = pl.BlockSpec((1, d), lambda i, j, kk: (0, 0))
    once = pl.Buffered(1)
    dx, dg = _mm(tag + "_dhn",
                 [(dgu, pl.BlockSpec((None, tm, f), lambda i, j, kk: (0, i, 0)), w1, pl.BlockSpec((d, f), lambda i, j, kk: (0, 0), pipeline_mode=once)),
                  (dgu, pl.BlockSpec((None, tm, f), lambda i, j, kk: (1, i, 0)), w1, pl.BlockSpec((d, f), lambda i, j, kk: (0, 1), pipeline_mode=once))],
                 NT, (l // tm, 1, 1), [_sds((l, d), F32), _sds((1, d), F32)], [row, vec], (tm, d),
                 _rms_bwd_epi(0), [x, gnorm, dres], [row, vec, row])
    return dx, dg


def _s5_chunk_tables(lam_re, lam_im, b_re, b_im, c_re, c_im, log_dt, hs):
    f32 = F32
    g, n = lam_re.shape[1], lam_re.shape[2]
    p = b_re.shape[-1]
    nch, gpc, nt = (g * n) // hs, hs // n, hs // LANE
    lr = jnp.minimum(lam_re.astype(f32), -1e-4)
    li = lam_im.astype(f32)
    dt = jnp.exp(log_dt.astype(f32))[..., None]
    mag = jnp.exp(lr * dt)
    ar = mag * jnp.cos(li * dt)
    ai = mag * jnp.sin(li * dt)
    den = lr * lr + li * li
    cr = ((ar - 1.0) * lr + ai * li) / den
    ci = (ai * lr - (ar - 1.0) * li) / den
    bbr = cr[..., None] * b_re - ci[..., None] * b_im
    bbi = cr[..., None] * b_im + ci[..., None] * b_re
    a_f = jnp.stack([ar, ai], axis=1).reshape(2, 2, nch, nt, LANE).transpose(0, 2, 1, 3, 4).reshape(2, nch, 2 * nt, LANE)
    rows_g = jnp.arange(gpc * p) // p
    cols_g = (jnp.arange(2 * hs) % hs) // n
    diag = (rows_g[:, None] == cols_g[None, :]).astype(f32)
    bb = jnp.stack([bbr, bbi], axis=1).reshape(2, 2, nch, hs, p)
    bd = jnp.tile(bb.transpose(0, 2, 4, 1, 3).reshape(2, nch, p, 2 * hs), (1, 1, gpc, 1)) * diag
    cc = jnp.stack([c_re, -c_im], axis=1).reshape(2, 2, nch, gpc, p, n)
    cd = jnp.tile(cc.transpose(0, 2, 4, 1, 3, 5).reshape(2, nch, p, 2 * hs), (1, 1, gpc, 1)) * diag
    return a_f, bd, cd


def _fold_store(ref, val, tb, ntiles):
    for s in range(ntiles):
        ref[:, s * SUBLANE:(s + 1) * SUBLANE, :] = val[:, s * LANE:(s + 1) * LANE].reshape(tb // SUBLANE, SUBLANE, LANE)


def _unfold(ref, tb, ntiles):
    return jnp.concatenate([ref[:, s * SUBLANE:(s + 1) * SUBLANE, :].reshape(tb, LANE) for s in range(ntiles)], axis=1)


def _s5_fwd(name, proj, bd, cd, a_f, reverse):
    l = proj.shape[0]
    nch, cu, hs2 = bd.shape
    nt = hs2 // (2 * LANE)
    frows = 2 * nt * SUBLANE
    tb = _tile(l, 512, SUBLANE)
    nb = l // tb

    def body(u_ref, bd_ref, cd_ref, a_ref, xf_ref, y_ref, st):
        r = pl.program_id(1)

        @pl.when(r == 0)
        def _():
            st[...] = jnp.zeros_like(st)

        _fold_store(xf_ref, _dot(u_ref[...], bd_ref[...]), tb, 2 * nt)
        ar, ai = a_ref[0:nt, :], a_ref[nt:2 * nt, :]

        def group(gi, carry):
            rr = (tb // SUBLANE - 1 - gi) if reverse else gi
            sr, si = carry
            for qq in range(SUBLANE):
                q = (SUBLANE - 1 - qq) if reverse else qq
                re_rows, im_rows = pl.ds(q, nt, stride=SUBLANE), pl.ds(nt * SUBLANE + q, nt, stride=SUBLANE)
                nr = ar * sr - ai * si + xf_ref[rr, re_rows, :]
                ni = ar * si + ai * sr + xf_ref[rr, im_rows, :]
                xf_ref[rr, re_rows, :] = nr
                xf_ref[rr, im_rows, :] = ni
                sr, si = nr, ni
            return sr, si

        fin = lax.fori_loop(0, tb // SUBLANE, group, (st[0:nt, :], st[nt:2 * nt, :]))
        st[0:nt, :] = fin[0]
        st[nt:2 * nt, :] = fin[1]
        y_ref[...] = _dot(_unfold(xf_ref, tb, 2 * nt), cd_ref[...], NT)

    def rows(r):
        return (nb - 1 - r) if reverse else r

    return pl.pallas_call(
        body, name=name, grid=(nch, nb),
        in_specs=[pl.BlockSpec((tb, cu), lambda c, r: (rows(r), c)), pl.BlockSpec((None, cu, hs2), lambda c, r: (c, 0, 0)),
                  pl.BlockSpec((None, cu, hs2), lambda c, r: (c, 0, 0)), pl.BlockSpec((None, 2 * nt, LANE), lambda c, r: (c, 0, 0))],
        out_specs=[pl.BlockSpec((tb // SUBLANE, frows, LANE), lambda c, r: (rows(r), c, 0)), pl.BlockSpec((tb, cu), lambda c, r: (rows(r), c))],
        out_shape=[_sds((l // SUBLANE, nch * frows, LANE), F32), _sds((l, nch * cu), F32)],
        scratch_shapes=[pltpu.VMEM((2 * nt, LANE), F32)],
        compiler_params=_params(("arbitrary", "arbitrary")),
    )(proj, bd, cd, a_f)


def _s5_bwd(name, proj, dy, xf, bd, cd, a_conj, reverse):
    l = proj.shape[0]
    nch, cu, hs2 = bd.shape
    nt = hs2 // (2 * LANE)
    frows = 2 * nt * SUBLANE
    tb = _tile(l, 512, SUBLANE)
    nb = l // tb

    def body(u_ref, dy_ref, xs_ref, bd_ref, cd_ref, a_ref, du_ref, dbd_ref, dcd_ref, da_ref, lam, st):
        r = pl.program_id(1)

        @pl.when(r == 0)
        def _():
            st[...] = jnp.zeros_like(st)
            dbd_ref[...] = jnp.zeros_like(dbd_ref)
            dcd_ref[...] = jnp.zeros_like(dcd_ref)
            da_ref[...] = jnp.zeros_like(da_ref)

        dyv = dy_ref[...]
        _fold_store(lam, _dot(dyv, cd_ref[...]), tb, 2 * nt)
        ar, ai = a_ref[0:nt, :], a_ref[nt:2 * nt, :]

        def group(gi, carry):
            rr = (tb // SUBLANE - 1 - gi) if reverse else gi
            sr, si, cr, ci = carry
            for qq in range(SUBLANE):
                q = (SUBLANE - 1 - qq) if reverse else qq
                re_rows, im_rows = pl.ds(q, nt, stride=SUBLANE), pl.ds(nt * SUBLANE + q, nt, stride=SUBLANE)
                xr, xi = xs_ref[rr, re_rows, :], xs_ref[rr, im_rows, :]
                cr = cr + sr * xr + si * xi
                ci = ci + si * xr - sr * xi
                nr = ar * sr - ai * si + lam[rr, re_rows, :]
                ni = ar * si + ai * sr + lam[rr, im_rows, :]
                lam[rr, re_rows, :] = nr
                lam[rr, im_rows, :] = ni
                sr, si = nr, ni
            return sr, si, cr, ci

        zero = jnp.zeros((nt, LANE), F32)
        fin = lax.fori_loop(0, tb // SUBLANE, group, (st[0:nt, :], st[nt:2 * nt, :], zero, zero))
        st[0:nt, :] = fin[0]
        st[nt:2 * nt, :] = fin[1]
        da_ref[0:nt, :] += fin[2]
        da_ref[nt:2 * nt, :] += fin[3]
        lam_u = _unfold(lam, tb, 2 * nt)
        du_ref[...] = _dot(lam_u, bd_ref[...], NT)
        dbd_ref[...] += _dot(u_ref[...], lam_u, TN)
        dcd_ref[...] += _dot(dyv, _unfold(xs_ref, tb, 2 * nt), TN)

    def rows(r):
        return (nb - 1 - r) if reverse else r

    chunk_rows = pl.BlockSpec((tb, cu), lambda c, r: (rows(r), c))
    bd_spec = pl.BlockSpec((None, cu, hs2), lambda c, r: (c, 0, 0))
    cd_spec = bd_spec
    a_spec = pl.BlockSpec((None, 2 * nt, LANE), lambda c, r: (c, 0, 0))
    return pl.pallas_call(
        body, name=name, grid=(nch, nb),
        in_specs=[chunk_rows, chunk_rows, pl.BlockSpec((tb // SUBLANE, frows, LANE), lambda c, r: (rows(r), c, 0)), bd_spec, cd_spec, a_spec],
        out_specs=[chunk_rows, bd_spec, cd_spec, a_spec],
        out_shape=[_sds((l, nch * cu), F32), _sds((nch, cu, hs2), F32), _sds((nch, cu, hs2), F32), _sds((nch, 2 * nt, LANE), F32)],
        scratch_shapes=[pltpu.VMEM((tb // SUBLANE, frows, LANE), F32), pltpu.VMEM((2 * nt, LANE), F32)],
        compiler_params=_params(("arbitrary", "arbitrary")),
    )(proj, dy, xf, bd, cd, a_conj)


def _s5_du(name, du_f, du_b, dy, proj, d_row):
    l, w = dy.shape
    tm = _tile(l, 1024, SUBLANE)

    def body(f_ref, b_ref, dy_ref, u_ref, d_ref, du_ref, dd_ref):
        i = pl.program_id(0)
        dyv = dy_ref[...]
        du_ref[...] = (f_ref[...] + b_ref[...] + dyv * d_ref[...]).astype(du_ref.dtype)
        part = jnp.sum(dyv * u_ref[...].astype(F32), axis=0, keepdims=True)

        @pl.when(i == 0)
        def _():
            dd_ref[...] = part

        @pl.when(i > 0)
        def _():
            dd_ref[...] += part

    row = pl.BlockSpec((tm, w), lambda i: (i, 0))
    vec = pl.BlockSpec((1, w), lambda i: (0, 0))
    return pl.pallas_call(
        body, name=name, grid=(l // tm,), in_specs=[row, row, row, row, vec], out_specs=[row, vec],
        out_shape=[_sds((l, w), BF), _sds((1, w), F32)],
        compiler_params=_params(("arbitrary",)),
    )(du_f, du_b, dy, proj, d_row)


def _gelu(y):
    c = math.sqrt(2.0 / math.pi)
    return 0.5 * y * (1.0 + jnp.tanh(c * (y + 0.044715 * y * y * y)))


def _gelu_grad(y):
    c = math.sqrt(2.0 / math.pi)
    th = jnp.tanh(c * (y + 0.044715 * y * y * y))
    return 0.5 * (1.0 + th) + 0.5 * y * (1.0 - th * th) * c * (1.0 + 3.0 * 0.044715 * y * y)


def _glu_fwd(name, y_f, y_b, proj, d_row, w):
    l, wd = y_f.shape
    tm = _tile(l, 512, SUBLANE)

    def body(yf_ref, yb_ref, u_ref, d_ref, w_ref, y_ref, o_ref):
        y = yf_ref[...] + yb_ref[...] + u_ref[...].astype(F32) * d_ref[...]
        y_ref[...] = y
        gy = _gelu(y)
        z = _dot(gy, w_ref[...])
        o_ref[...] = (gy * _sigmoid(z)).astype(o_ref.dtype)

    row = pl.BlockSpec((tm, wd), lambda i: (i, 0))
    return pl.pallas_call(
        body, name=name, grid=(l // tm,),
        in_specs=[row, row, row, pl.BlockSpec((1, wd), lambda i: (0, 0)), pl.BlockSpec((wd, wd), lambda i: (0, 0))],
        out_specs=[row, row], out_shape=[_sds((l, wd), F32), _sds((l, wd), BF)],
        compiler_params=_params(("arbitrary",)),
    )(y_f, y_b, proj, d_row, w)


def _glu_bwd(name, y, w, dout, dcol):
    l, wd = y.shape
    tm = _tile(l, 512, SUBLANE)

    def body(y_ref, w_ref, d_ref, dy_ref, dw_ref):
        i = pl.program_id(0)
        yv = y_ref[...]
        gy = _gelu(yv)
        s = _sigmoid(_dot(gy, w_ref[...]))
        d = d_ref[...].astype(F32)
        t = d * gy * s * (1.0 - s)
        dgy = d * s + _dot(t, w_ref[...], NT)
        dy_ref[...] = dgy * _gelu_grad(yv)
        part = _dot(gy, t, TN)

        @pl.when(i == 0)
        def _():
            dw_ref[...] = part

        @pl.when(i > 0)
        def _():
            dw_ref[...] += part

    return pl.pallas_call(
        body, name=name, grid=(l // tm,),
        in_specs=[pl.BlockSpec((tm, wd), lambda i: (i, 0)), pl.BlockSpec((wd, wd), lambda i: (0, 0)),
                  pl.BlockSpec((tm, wd), lambda i: (i, dcol))],
        out_specs=[pl.BlockSpec((tm, wd), lambda i: (i, 0)), pl.BlockSpec((wd, wd), lambda i: (0, 0))],
        out_shape=[_sds((l, wd), F32), _sds((wd, wd), F32)],
        compiler_params=_params(("arbitrary",)),
    )(y, w, dout)


def _log_sigmoid(x):
    return jnp.minimum(x, 0.0) - jnp.log(1.0 + jnp.exp(-jnp.abs(x)))


def _gate_fwd(name, glo, wf, wb, bf, bb):
    l, r2 = glo.shape
    hk = wf.shape[1]
    tm = _tile(l, 1024, SUBLANE)

    def body(x_ref, wf_ref, wb_ref, bf_ref, bb_ref, gf_ref, gb_ref):
        xv = x_ref[...]
        gf_ref[...] = _log_sigmoid(_dot(xv, wf_ref[...]) + bf_ref[...]) * (1.0 / GLA_GATE_NORM)
        gb_ref[...] = _log_sigmoid(_dot(xv, wb_ref[...]) + bb_ref[...]) * (1.0 / GLA_GATE_NORM)

    w_spec = pl.BlockSpec((r2, hk), lambda i: (0, 0))
    b_spec = pl.BlockSpec((1, hk), lambda i: (0, 0))
    o_spec = pl.BlockSpec((tm, hk), lambda i: (i, 0))
    return pl.pallas_call(
        body, name=name, grid=(l // tm,),
        in_specs=[pl.BlockSpec((tm, r2), lambda i: (i, 0)), w_spec, w_spec, b_spec, b_spec],
        out_specs=[o_spec, o_spec], out_shape=[_sds((l, hk), F32), _sds((l, hk), F32)],
        compiler_params=_params(("arbitrary",)),
    )(glo, wf, wb, bf, bb)


def _gate_bwd(name, glo, wf, wb, bf, bb, dgf, dgb):
    l, r2 = glo.shape
    hk = wf.shape[1]
    tm = _tile(l, 1024, SUBLANE)

    def body(x_ref, wf_ref, wb_ref, bf_ref, bb_ref, dgf_ref, dgb_ref, dx_ref, dwf_ref, dwb_ref, dbf_ref, dbb_ref):
        i = pl.program_id(0)
        xv = x_ref[...]
        kf = _dot(xv, wf_ref[...]) + bf_ref[...]
        kb = _dot(xv, wb_ref[...]) + bb_ref[...]
        dkf = dgf_ref[...] * (1.0 / GLA_GATE_NORM) * _sigmoid(-kf)
        dkb = dgb_ref[...] * (1.0 / GLA_GATE_NORM) * _sigmoid(-kb)
        dx_ref[...] = _dot(dkf, wf_ref[...], NT) + _dot(dkb, wb_ref[...], NT)
        parts = (_dot(xv, dkf, TN), _dot(xv, dkb, TN), jnp.sum(dkf, axis=0, keepdims=True), jnp.sum(dkb, axis=0, keepdims=True))
        accs = (dwf_ref, dwb_ref, dbf_ref, dbb_ref)

        @pl.when(i == 0)
        def _():
            for a_, p_ in zip(accs, parts):
                a_[...] = p_

        @pl.when(i > 0)
        def _():
            for a_, p_ in zip(accs, parts):
                a_[...] += p_

    w_spec = pl.BlockSpec((r2, hk), lambda i: (0, 0))
    b_spec = pl.BlockSpec((1, hk), lambda i: (0, 0))
    g_spec = pl.BlockSpec((tm, hk), lambda i: (i, 0))
    x_spec = pl.BlockSpec((tm, r2), lambda i: (i, 0))
    return pl.pallas_call(
        body, name=name, grid=(l // tm,),
        in_specs=[x_spec, w_spec, w_spec, b_spec, b_spec, g_spec, g_spec],
        out_specs=[x_spec, w_spec, w_spec, b_spec, b_spec],
        out_shape=[_sds((l, r2), F32), _sds((r2, hk), F32), _sds((r2, hk), F32), _sds((1, hk), F32), _sds((1, hk), F32)],
        compiler_params=_params(("arbitrary",)),
    )(glo, wf, wb, bf, bb, dgf, dgb)


def _chunk_terms(qc, kc, gc, lg, chunk, reverse):
    ri = lax.broadcasted_iota(jnp.int32, (chunk, chunk), 0)
    ci = lax.broadcasted_iota(jnp.int32, (chunk, chunk), 1)
    if reverse:
        tri = ci >= ri
        mask = ci > ri
    else:
        tri = ci <= ri
        mask = ci <= ri
    if gc is not None:
        cum = _dot3(tri.astype(F32), gc)
        last = cum[0:1, :] if reverse else cum[chunk - 1:chunk, :]
    else:
        pos = lax.broadcasted_iota(jnp.int32, (chunk, 1), 0).astype(F32)
        cum = ((chunk - pos) if reverse else (pos + 1.0)) * lg
        last = chunk * lg
    e = jnp.exp(cum)
    einv = jnp.exp(-cum)
    dec = jnp.exp(last - cum)
    return e, einv, dec, qc * e, kc * einv, kc * dec, jnp.exp(last), mask, tri


def _lin_specs(arr, width, col, tb, nb, reverse, per_head):
    if per_head:
        return pl.BlockSpec((tb, width), lambda h, r: ((nb - 1 - r) if reverse else r, col + h))
    return pl.BlockSpec((tb, width), lambda h, r: ((nb - 1 - r) if reverse else r, col))


def _lin_fwd(name, q, k, v, g, lgtab, prev_o=None, *, heads, hb, dk, dv, chunk, tb, qcol, kcol, vcol, qscale, reverse):
    l = q.shape[0]
    nb = l // tb
    ncb = tb // chunk
    ng = heads // hb
    gated = g is not None
    per_head = ng > 1

    def body(*refs):
        q_ref, k_ref, v_ref = refs[:3]
        g_ref = lg_ref = refs[3]
        p_ref = refs[4] if prev_o is not None else None
        o_ref, sp_ref, st = refs[-3:]
        r = pl.program_id(1)

        @pl.when(r == 0)
        def _():
            st[...] = jnp.zeros_like(st)

        for c in range(ncb):
            cc = (ncb - 1 - c) if reverse else c
            rows = pl.ds(cc * chunk, chunk)
            for h in range(hb):
                lg = None if gated else lg_ref[h, :, 0:1]
                qc = q_ref[rows, h * dk:(h + 1) * dk].astype(F32) * qscale
                kc = k_ref[rows, h * dk:(h + 1) * dk].astype(F32)
                vc = v_ref[rows, h * dv:(h + 1) * dv]
                gc = g_ref[rows, h * dk:(h + 1) * dk] if gated else None
                _, _, _, qd, ki, kdec, e_last, mask, _ = _chunk_terms(qc, kc, gc, lg, chunk, reverse)
                a = jnp.where(mask, _dot(qd, ki, NT), 0.0)
                s_t = st[h]
                oc = _dot(a, vc) + _dot(qd, s_t, NT)
                if p_ref is not None:
                    oc = oc + p_ref[rows, h * dv:(h + 1) * dv]
                o_ref[rows, h * dv:(h + 1) * dv] = oc.astype(o_ref.dtype)
                sp_ref[cc, h] = s_t
                st[h] = s_t * e_last + _dot(vc, kdec, TN)

    in_specs = [_lin_specs(q, hb * dk, qcol, tb, nb, reverse, per_head), _lin_specs(k, hb * dk, kcol, tb, nb, reverse, per_head),
                _lin_specs(v, hb * dv, vcol, tb, nb, reverse, per_head)]
    args = [q, k, v]
    if gated:
        in_specs.append(_lin_specs(g, hb * dk, 0, tb, nb, reverse, per_head))
        args.append(g)
    else:
        in_specs.append(pl.BlockSpec((hb, 1, LANE), lambda h, r: (h, 0, 0)))
        args.append(lgtab)
    out_specs = [_lin_specs(None, hb * dv, 0, tb, nb, reverse, per_head),
                 pl.BlockSpec((ncb, hb, dv, dk), lambda h, r: ((nb - 1 - r) if reverse else r, h, 0, 0))]
    if prev_o is not None:
        in_specs.append(out_specs[0])
        args.append(prev_o)
    outs = [_sds((l, heads * dv), F32 if prev_o is None else BF), _sds((l // chunk, heads, dv, dk), F32)]
    return pl.pallas_call(
        body, name=name, grid=(ng, nb), in_specs=in_specs, out_specs=out_specs, out_shape=outs,
        scratch_shapes=[pltpu.VMEM((hb, dv, dk), F32)],
        compiler_params=_params(("arbitrary", "arbitrary")),
    )(*args)


def _lin_bwd(name, q, k, v, g, lgtab, sprev, do, prev, *, heads, hb, dk, dv, chunk, tb, qcol, kcol, vcol, qscale, reverse):
    l = q.shape[0]
    nb = l // tb
    ncb = tb // chunk
    ng = heads // hb
    gated = g is not None
    per_head = ng > 1
    brev = not reverse
    n_prev = 0 if prev is None else len(prev)

    def body(*refs):
        q_ref, k_ref, v_ref, x_ref, sp_ref, do_ref = refs[:6]
        p_refs = refs[6:6 + n_prev]
        o_refs = refs[6 + n_prev:-1]
        dst = refs[-1]
        dq_ref, dk_ref, dv_ref = o_refs[:3]
        r = pl.program_id(1)

        @pl.when(r == 0)
        def _():
            dst[...] = jnp.zeros_like(dst)

        for c in range(ncb):
            cc = (ncb - 1 - c) if brev else c
            rows = pl.ds(cc * chunk, chunk)
            for h in range(hb):
                lg = None if gated else x_ref[h, :, 0:1]
                ks = slice(h * dk, (h + 1) * dk)
                vs = slice(h * dv, (h + 1) * dv)
                qc = q_ref[rows, ks].astype(F32) * qscale
                kc = k_ref[rows, ks].astype(F32)
                vc = v_ref[rows, vs]
                gc = x_ref[rows, ks] if gated else None
                e, einv, dec, qd, ki, kdec, e_last, mask, tri = _chunk_terms(qc, kc, gc, lg, chunk, reverse)
                a = jnp.where(mask, _dot(qd, ki, NT), 0.0)
                s_t = sp_ref[cc, h]
                ds_t = dst[h]
                doc = do_ref[rows, vs]
                dvc = _dot(a, doc, TN) + _dot(kdec, ds_t, NT)
                da = jnp.where(mask, _dot(doc, vc, NT), 0.0)
                dqd = _dot(da, ki) + _dot(doc, s_t)
                dki = _dot(da, qd, TN)
                dkdec = _dot(vc, ds_t)
                dst[h] = ds_t * e_last + _dot(doc, qd, TN)
                dqc = dqd * e * qscale
                dkc = dki * einv + dkdec * dec
                if n_prev:
                    dqc = dqc + p_refs[0][rows, ks]
                    dkc = dkc + p_refs[1][rows, ks]
                    dvc = dvc + p_refs[2][rows, vs]
                dq_ref[rows, ks] = dqc
                dk_ref[rows, ks] = dkc
                dv_ref[rows, vs] = dvc.astype(dv_ref.dtype)
                if gated:
                    dcum = dqd * qd - dki * ki - dkdec * kdec
                    dlast = jnp.sum(dkdec * kdec, axis=0, keepdims=True) + e_last * jnp.sum(s_t * ds_t, axis=0, keepdims=True)
                    rid = lax.broadcasted_iota(jnp.int32, (chunk, 1), 0)
                    dcum = dcum + jnp.where(rid == (0 if reverse else chunk - 1), dlast, 0.0)
                    dgc = _dot3(tri.astype(F32), dcum, TN)
                    o_refs[3][rows, ks] = dgc

    def spec(width, col):
        return _lin_specs(None, width, col, tb, nb, brev, per_head)

    in_specs = [spec(hb * dk, qcol), spec(hb * dk, kcol), spec(hb * dv, vcol)]
    args = [q, k, v]
    if gated:
        in_specs.append(spec(hb * dk, 0))
        args.append(g)
    else:
        in_specs.append(pl.BlockSpec((hb, 1, LANE), lambda h, r: (h, 0, 0)))
        args.append(lgtab)
    in_specs.append(pl.BlockSpec((ncb, hb, dv, dk), lambda h, r: ((nb - 1 - r) if brev else r, h, 0, 0)))
    args.append(sprev)
    in_specs.append(spec(hb * dv, 0))
    args.append(do)
    out_specs = [spec(hb * dk, 0), spec(hb * dk, 0), spec(hb * dv, 0)]
    outs = [_sds((l, heads * dk), F32), _sds((l, heads * dk), F32), _sds((l, heads * dv), BF if n_prev else F32)]
    if gated:
        out_specs.append(spec(hb * dk, 0))
        outs.append(_sds((l, heads * dk), F32))
    if n_prev:
        in_specs += out_specs[:3]
        args += list(prev)
    return pl.pallas_call(
        body, name=name, grid=(ng, nb), in_specs=in_specs, out_specs=out_specs, out_shape=outs,
        scratch_shapes=[pltpu.VMEM((hb, dv, dk), F32)],
        compiler_params=_params(("arbitrary", "arbitrary")),
    )(*args)


def _log2(n):
    assert n & (n - 1) == 0, "a power of two"
    return n.bit_length() - 1


def _gla_block_terms(q, k, g, qscale, chunk, tb, reverse):
    ri = lax.broadcasted_iota(jnp.int32, (tb, tb), 0)
    ci = lax.broadcasted_iota(jnp.int32, (tb, tb), 1)
    same = jnp.right_shift(ri, _log2(chunk)) == jnp.right_shift(ci, _log2(chunk))
    t_in = jnp.logical_and(same, (ci >= ri) if reverse else (ci <= ri)).astype(F32)
    cum = _dot3(t_in, g)
    tot = _dot3(same.astype(F32), g)
    e = jnp.exp(cum)
    einv = jnp.exp(-cum)
    dec = jnp.exp(tot - cum)
    return e, einv, dec, jnp.exp(tot), q * (qscale * e), k * einv, k * dec, t_in


def _gla_masks(hk, dk, heads, chunk, reverse):
    lane = lax.broadcasted_iota(jnp.int32, (1, hk), 1)
    head_of = jnp.right_shift(lane, _log2(dk))
    ri = lax.broadcasted_iota(jnp.int32, (chunk, chunk), 0)
    ci = lax.broadcasted_iota(jnp.int32, (chunk, chunk), 1)
    return [head_of == h for h in range(heads)], ((ci > ri) if reverse else (ci <= ri))


def _gla_fwd(name, proj, g, prev_o=None, *, heads, dk, dv, chunk, tb, qcol, kcol, vcol, qscale, reverse):
    l = proj.shape[0]
    nb, ncb, hk, hv = l // tb, tb // chunk, heads * dk, heads * dv

    def body(q_ref, k_ref, v_ref, g_ref, *rest):
        p_ref = rest[0] if prev_o is not None else None
        o_ref, sp_ref, st = rest[-3:]

        @pl.when(pl.program_id(0) == 0)
        def _():
            st[...] = jnp.zeros_like(st)

        _, _, _, etot, qd, ki, kdec, _ = _gla_block_terms(q_ref[...].astype(F32), k_ref[...].astype(F32), g_ref[...], qscale, chunk, tb, reverse)
        heads_m, causal = _gla_masks(hk, dk, heads, chunk, reverse)
        s_all = st[...]
        for c in range(ncb):
            cc = (ncb - 1 - c) if reverse else c
            rc = slice(cc * chunk, (cc + 1) * chunk)
            qd_c, ki_c, kdec_c = qd[rc], ki[rc], kdec[rc]
            sp_ref[cc] = s_all
            kv = jnp.zeros_like(s_all)
            for h in range(heads):
                vs = slice(h * dv, (h + 1) * dv)
                qm = jnp.where(heads_m[h], qd_c, 0.0)
                a = jnp.where(causal, _dot(qm, ki_c, NT), 0.0)
                vc = v_ref[rc, vs]
                oc = _dot(a, vc) + _dot(qm, s_all, NT)
                if p_ref is not None:
                    oc = oc + p_ref[rc, vs]
                o_ref[rc, vs] = oc.astype(o_ref.dtype)
                kv = kv + jnp.where(heads_m[h], _dot(vc, kdec_c, TN), 0.0)
            s_all = s_all * etot[rc][0:1, :] + kv
        st[...] = s_all

    def rows(r):
        return (nb - 1 - r) if reverse else r

    o_spec = pl.BlockSpec((tb, hv), lambda r: (rows(r), 0))
    extra = [] if prev_o is None else [prev_o]
    return pl.pallas_call(
        body, name=name, grid=(nb,),
        in_specs=[pl.BlockSpec((tb, hk), lambda r: (rows(r), qcol)), pl.BlockSpec((tb, hk), lambda r: (rows(r), kcol)),
                  pl.BlockSpec((tb, hv), lambda r: (rows(r), vcol)), pl.BlockSpec((tb, hk), lambda r: (rows(r), 0))] + [o_spec] * len(extra),
        out_specs=[o_spec, pl.BlockSpec((ncb, dv, hk), lambda r: (rows(r), 0, 0))],
        out_shape=[_sds((l, hv), F32 if prev_o is None else BF), _sds((l // chunk, dv, hk), F32)],
        scratch_shapes=[pltpu.VMEM((dv, hk), F32)],
        compiler_params=_params(("arbitrary",)),
    )(proj, proj, proj, g, *extra)


def _gla_bwd(name, proj, g, sprev, do, prev, *, heads, dk, dv, chunk, tb, qcol, kcol, vcol, qscale, reverse):
    l = proj.shape[0]
    nb, ncb, hk, hv = l // tb, tb // chunk, heads * dk, heads * dv
    brev = not reverse
    n_prev = 0 if prev is None else len(prev)

    def body(q_ref, k_ref, v_ref, g_ref, sp_ref, do_ref, *rest):
        p_refs = rest[:n_prev]
        dq_ref, dk_ref, dv_ref, dg_ref, dst, dcs = rest[n_prev:]

        @pl.when(pl.program_id(0) == 0)
        def _():
            dst[...] = jnp.zeros_like(dst)

        e, einv, dec, etot, qd, ki, kdec, t_in = _gla_block_terms(q_ref[...].astype(F32), k_ref[...].astype(F32), g_ref[...], qscale, chunk, tb, reverse)
        heads_m, causal = _gla_masks(hk, dk, heads, chunk, reverse)
        last_row = lax.broadcasted_iota(jnp.int32, (chunk, 1), 0) == (0 if reverse else chunk - 1)
        ds_all = dst[...]
        for c in range(ncb):
            cc = (ncb - 1 - c) if brev else c
            rc = slice(cc * chunk, (cc + 1) * chunk)
            qd_c, ki_c, kdec_c = qd[rc], ki[rc], kdec[rc]
            s_all = sp_ref[cc]
            et = etot[rc][0:1, :]
            dqd = jnp.zeros((chunk, hk), F32)
            dki = jnp.zeros((chunk, hk), F32)
            dkdec = jnp.zeros((chunk, hk), F32)
            ds_add = jnp.zeros_like(ds_all)
            for h in range(heads):
                vs = slice(h * dv, (h + 1) * dv)
                m = heads_m[h]
                qm = jnp.where(m, qd_c, 0.0)
                a = jnp.where(causal, _dot(qm, ki_c, NT), 0.0)
                doc, vc = do_ref[rc, vs], v_ref[rc, vs]
                dvc = _dot(a, doc, TN) + _dot(jnp.where(m, kdec_c, 0.0), ds_all, NT)
                if n_prev:
                    dvc = dvc + p_refs[2][rc, vs]
                dv_ref[rc, vs] = dvc.astype(dv_ref.dtype)
                da = jnp.where(causal, _dot(doc, vc, NT), 0.0)
                dqd = dqd + jnp.where(m, _dot(da, ki_c) + _dot(doc, s_all), 0.0)
                dki = dki + _dot(da, qm, TN)
                dkdec = dkdec + jnp.where(m, _dot(vc, ds_all), 0.0)
                ds_add = ds_add + _dot(doc, qm, TN)
            dqc = dqd * e[rc] * qscale
            dkc = dki * einv[rc] + dkdec * dec[rc]
            if n_prev:
                dqc = dqc + p_refs[0][rc, :]
                dkc = dkc + p_refs[1][rc, :]
            dq_ref[rc, :] = dqc
            dk_ref[rc, :] = dkc
            dlast = jnp.sum(dkdec * kdec_c, axis=0, keepdims=True) + et * jnp.sum(s_all * ds_all, axis=0, keepdims=True)
            dcs[rc, :] = dqd * qd_c - dki * ki_c - dkdec * kdec_c + jnp.where(last_row, dlast, 0.0)
            ds_all = ds_all * et + ds_add
        dst[...] = ds_all
        dg_ref[...] = _dot3(t_in, dcs[...], TN)

    def rows(r):
        return (nb - 1 - r) if brev else r

    k_spec = pl.BlockSpec((tb, hk), lambda r: (rows(r), 0))
    v_spec = pl.BlockSpec((tb, hv), lambda r: (rows(r), 0))
    in_specs = [pl.BlockSpec((tb, hk), lambda r: (rows(r), qcol)), pl.BlockSpec((tb, hk), lambda r: (rows(r), kcol)),
                pl.BlockSpec((tb, hv), lambda r: (rows(r), vcol)), k_spec,
                pl.BlockSpec((ncb, dv, hk), lambda r: (rows(r), 0, 0)), v_spec]
    args = [proj, proj, proj, g, sprev, do]
    if n_prev:
        in_specs += [k_spec, k_spec, v_spec]
        args += list(prev)
    return pl.pallas_call(
        body, name=name, grid=(nb,), in_specs=in_specs, out_specs=[k_spec, k_spec, v_spec, k_spec],
        out_shape=[_sds((l, hk), F32), _sds((l, hk), F32), _sds((l, hv), BF if n_prev else F32), _sds((l, hk), F32)],
        scratch_shapes=[pltpu.VMEM((dv, hk), F32), pltpu.VMEM((tb, hk), F32)],
        compiler_params=_params(("arbitrary",)),
    )(*args)


def _headgate_fwd(name, o_sum, og_arr, og_col, gn, dv):
    l, w = o_sum.shape
    tm = _tile(l, 512, SUBLANE)
    nh = w // dv

    def body(o_ref, og_ref, gn_ref, out_ref):
        for h in range(nh):
            cs = slice(h * dv, (h + 1) * dv)
            o = o_ref[:, cs].astype(F32)
            r = lax.rsqrt(jnp.mean(o * o, axis=-1, keepdims=True) + EPS)
            og = og_ref[:, cs].astype(F32)
            out_ref[:, cs] = (o * r * gn_ref[:, cs] * (og * _sigmoid(og))).astype(out_ref.dtype)

    row = pl.BlockSpec((tm, w), lambda i: (i, 0))
    return pl.pallas_call(
        body, name=name, grid=(l // tm,),
        in_specs=[row, pl.BlockSpec((tm, w), lambda i: (i, og_col)), pl.BlockSpec((1, w), lambda i: (0, 0))],
        out_specs=row, out_shape=_sds((l, w), BF),
        compiler_params=_params(("arbitrary",)),
    )(o_sum, og_arr, gn)


def _headgate_bwd(name, o_sum, og_arr, og_col, gn, dout, dcol, dv):
    l, w = o_sum.shape
    tm = _tile(l, 512, SUBLANE)
    nh = w // dv

    def body(o_ref, og_ref, gn_ref, d_ref, do_ref, dog_ref, dgn_ref):
        i = pl.program_id(0)
        for h in range(nh):
            cs = slice(h * dv, (h + 1) * dv)
            o = o_ref[:, cs].astype(F32)
            r = lax.rsqrt(jnp.mean(o * o, axis=-1, keepdims=True) + EPS)
            oh = o * r
            og = og_ref[:, cs].astype(F32)
            s = _sigmoid(og)
            d = d_ref[:, cs].astype(F32)
            gnv = gn_ref[:, cs]
            d_on = d * (og * s)
            dog_ref[:, cs] = (d * (oh * gnv) * s * (1.0 + og * (1.0 - s))).astype(dog_ref.dtype)
            doh = d_on * gnv
            do_ref[:, cs] = (r * (doh - oh * jnp.mean(doh * oh, axis=-1, keepdims=True))).astype(do_ref.dtype)
            part = jnp.sum(d_on * oh, axis=0, keepdims=True)

            @pl.when(i == 0)
            def _():
                dgn_ref[:, cs] = part

            @pl.when(i > 0)
            def _():
                dgn_ref[:, cs] += part

    row = pl.BlockSpec((tm, w), lambda i: (i, 0))
    vec = pl.BlockSpec((1, w), lambda i: (0, 0))
    return pl.pallas_call(
        body, name=name, grid=(l // tm,),
        in_specs=[row, pl.BlockSpec((tm, w), lambda i: (i, og_col)), vec, pl.BlockSpec((tm, w), lambda i: (i, dcol))],
        out_specs=[row, row, vec], out_shape=[_sds((l, w), BF), _sds((l, w), BF), _sds((1, w), F32)],
        compiler_params=_params(("arbitrary",)),
    )(o_sum, og_arr, gn, dout)


def _rot_tables(l, dk):
    half = dk // 2
    pos = jnp.arange(l, dtype=F32)
    inv = jnp.exp(-math.log(ROPE_BASE) * jnp.arange(half, dtype=F32) / half)
    ang = pos[:, None] * inv[None, :]
    cos, sin = jnp.cos(ang), jnp.sin(ang)
    return jnp.concatenate([cos, cos], axis=-1), jnp.concatenate([-sin, sin], axis=-1)


def _rot_apply(name, src_q, qcol, src_k, kcol, cos_t, sin_t, heads, dk, kscale, out_dtype, transpose):
    l = src_q.shape[0]
    w = heads * dk
    tm = _tile(l, 512, SUBLANE)

    def rot(t, cos_v, sin_v):
        if transpose:
            return t * cos_v + pltpu.roll(t * sin_v, dk // 2, 1)
        return t * cos_v + pltpu.roll(t, dk // 2, 1) * sin_v

    def body(q_ref, k_ref, c_ref, s_ref, qo_ref, ko_ref):
        cos_v, sin_v = c_ref[...], s_ref[...]
        for h in range(heads):
            cs = slice(h * dk, (h + 1) * dk)
            qo_ref[:, cs] = rot(q_ref[:, cs].astype(F32), cos_v, sin_v).astype(out_dtype)
            ko_ref[:, cs] = (rot(k_ref[:, cs].astype(F32), cos_v, sin_v) * kscale).astype(out_dtype)

    tab = pl.BlockSpec((tm, dk), lambda i: (i, 0))
    row = pl.BlockSpec((tm, w), lambda i: (i, 0))
    return pl.pallas_call(
        body, name=name, grid=(l // tm,),
        in_specs=[pl.BlockSpec((tm, w), lambda i: (i, qcol)), pl.BlockSpec((tm, w), lambda i: (i, kcol)), tab, tab],
        out_specs=[row, row], out_shape=[_sds((l, w), out_dtype), _sds((l, w), out_dtype)],
        compiler_params=_params(("arbitrary",)),
    )(src_q, src_k, cos_t, sin_t)


def _exchange(name, src, gather):
    shape = src.shape if gather else src.shape[1:]

    def body(src_ref, out_ref, send_sems, recv_sems, local_sem):
        me = _my_index()
        own = pltpu.make_async_copy(src_ref if gather else src_ref.at[me], out_ref.at[me], local_sem)
        own.start()
        copies = _peer_copies(src_ref, out_ref, send_sems, recv_sems, gather)
        for cp in copies:
            cp.start()
        for cp in copies:
            cp.wait_recv()
        for cp in copies:
            cp.wait_send()
        own.wait()

    return pl.pallas_call(
        body, name=name,
        in_specs=[pl.BlockSpec(memory_space=pl.ANY)], out_specs=pl.BlockSpec(memory_space=pl.ANY),
        out_shape=_sds((N_DEV,) + tuple(shape), src.dtype),
        scratch_shapes=[pltpu.SemaphoreType.DMA((N_DEV - 1,)), pltpu.SemaphoreType.DMA((N_DEV - 1,)), pltpu.SemaphoreType.DMA],
        )(src)


def _my_index():
    return 4 * lax.axis_index("x") + 2 * lax.axis_index("y") + lax.axis_index("c")


def _peer_copies(src_ref, out_ref, send_sems, recv_sems, gather):
    x, y, c = lax.axis_index("x"), lax.axis_index("y"), lax.axis_index("c")
    me = 4 * x + 2 * y + c
    copies = []
    for kk in range(1, N_DEV):
        px = (1 - x) if kk & 4 else x
        py = (1 - y) if kk & 2 else y
        pc = (1 - c) if kk & 1 else c
        peer = 4 * px + 2 * py + pc
        copies.append(pltpu.make_async_remote_copy(
            src_ref=src_ref if gather else src_ref.at[peer], dst_ref=out_ref.at[me],
            send_sem=send_sems.at[kk - 1], recv_sem=recv_sems.at[kk - 1],
            device_id=(px, py, pc), device_id_type=pl.DeviceIdType.MESH))
    return copies


_HBM = pl.BlockSpec(memory_space=pltpu.HBM)
_SEM = pl.BlockSpec(memory_space=pltpu.SEMAPHORE)
_EFFECT = pltpu.SideEffectType.DATAFLOW_SIDE_EFFECTING


def _exchange_start(name, src, gather):
    shape = src.shape if gather else src.shape[1:]
    land = lax.empty((N_DEV,) + tuple(shape), src.dtype)

    def body(src_ref, land_ref, send_sems, recv_sems, src_thru, land_thru, token):
        for cp in _peer_copies(src_ref, land_ref, send_sems, recv_sems, gather):
            cp.start()
        token[...] = jnp.zeros_like(token)

    return pl.pallas_call(
        body, name=name,
        out_shape=(pltpu.SemaphoreType.DMA((N_DEV - 1,)), pltpu.SemaphoreType.DMA((N_DEV - 1,)),
                   pltpu.HBM(src.shape, src.dtype), pltpu.HBM(land.shape, land.dtype), _sds((SUBLANE, LANE), F32)),
        in_specs=(_HBM, _HBM), out_specs=(_SEM, _SEM, _HBM, _HBM, pl.BlockSpec(memory_space=pltpu.VMEM)),
        input_output_aliases={0: 2, 1: 3},
        compiler_params=pltpu.CompilerParams(has_side_effects=_EFFECT),
    )(pltpu.with_memory_space_constraint(src, pltpu.HBM), pltpu.with_memory_space_constraint(land, pltpu.HBM))


def _gather_start_all(name, srcs):
    n = len(srcs)
    lands = [lax.empty((N_DEV,) + tuple(s.shape), s.dtype) for s in srcs]

    def body(*refs):
        src_refs, land_refs = refs[:n], refs[n:2 * n]
        send, recv = refs[2 * n:3 * n], refs[3 * n:4 * n]
        token = refs[-1]
        for k in range(n):
            for cp in _peer_copies(src_refs[k], land_refs[k], send[k], recv[k], True):
                cp.start()
        token[...] = jnp.zeros_like(token)

    sem = pltpu.SemaphoreType.DMA((N_DEV - 1,))
    outs = pl.pallas_call(
        body, name=name,
        out_shape=tuple([sem] * (2 * n) + [pltpu.HBM(s.shape, s.dtype) for s in srcs] + [pltpu.HBM(a.shape, a.dtype) for a in lands]
                        + [_sds((SUBLANE, LANE), F32)]),
        in_specs=tuple([_HBM] * (2 * n)), out_specs=tuple([_SEM] * (2 * n) + [_HBM] * (2 * n) + [pl.BlockSpec(memory_space=pltpu.VMEM)]),
        input_output_aliases={k: 2 * n + k for k in range(2 * n)},
        compiler_params=pltpu.CompilerParams(has_side_effects=_EFFECT),
    )(*[pltpu.with_memory_space_constraint(a, pltpu.HBM) for a in list(srcs) + lands])
    return [(outs[k], outs[n + k], outs[2 * n + k], outs[3 * n + k], outs[-1]) for k in range(n)]


def _exchange_wait(name, started, gather, after):
    send_sems, recv_sems, src_thru, land_thru, _ = started

    def body(src_ref, land_ref, send_sems, recv_sems, after_ref, src_out, land_out):
        copies = _peer_copies(src_ref, land_ref, send_sems, recv_sems, gather)
        for cp in copies:
            cp.wait_send()
        for cp in copies:
            cp.wait_recv()

    return pl.pallas_call(
        body, name=name,
        out_shape=(pltpu.HBM(src_thru.shape, src_thru.dtype), pltpu.HBM(land_thru.shape, land_thru.dtype)),
        in_specs=(_HBM, _HBM, _SEM, _SEM, pl.BlockSpec(memory_space=pl.ANY)), out_specs=(_HBM, _HBM),
        input_output_aliases={0: 0, 1: 1},
        compiler_params=pltpu.CompilerParams(has_side_effects=_EFFECT),
    )(src_thru, land_thru, send_sems, recv_sems, after)


def _adam_math(w, gsum, m, v):
    m2 = ADAM_B1 * m + (1.0 - ADAM_B1) * gsum
    v2 = ADAM_B2 * v + (1.0 - ADAM_B2) * (gsum * gsum)
    m_hat = m2 / (1.0 - ADAM_B1 ** ADAM_STEP)
    v_hat = v2 / (1.0 - ADAM_B2 ** ADAM_STEP)
    delta = -ADAM_LR * (m_hat / (jnp.sqrt(v_hat) + ADAM_EPS) + ADAM_WD * w)
    return delta, m2, v2


def _reduce_adam(name, parts, w, m, v):
    nl, r, c = w.shape
    tr = _tile(r, 256, 16)
    nr = r // tr

    def body(*refs):
        p_refs = refs[:nl]
        w_ref, m_ref, v_ref, g_ref, d_ref, m2_ref, v2_ref = refs[nl:]
        for li in range(nl):
            @pl.when(pl.program_id(0) == li)
            def _(p_ref=p_refs[li]):
                gsum = p_ref[0].astype(F32)
                for s in range(1, N_DEV):
                    gsum = gsum + p_ref[s].astype(F32)
                g_ref[...] = gsum
                delta, m2, v2 = _adam_math(w_ref[...], gsum, m_ref[...], v_ref[...])
                d_ref[...] = delta
                m2_ref[...] = m2
                v2_ref[...] = v2

    def part_spec(li):
        return pl.BlockSpec((N_DEV, tr, c), lambda lay, i: (0, jnp.where(lay == li, i, jnp.where(lay < li, 0, nr - 1)), 0))

    row = pl.BlockSpec((None, tr, c), lambda lay, i: (lay, i, 0))
    return pl.pallas_call(
        body, name=name, grid=(nl, nr),
        in_specs=[part_spec(li) for li in range(nl)] + [row, row, row],
        out_specs=[row, row, row, row], out_shape=[_sds((nl, r, c), F32)] * 4,
        compiler_params=_params(("arbitrary", "arbitrary")),
    )(*parts, w, m, v)


def _reduce8(name, parts):
    _, r, c = parts.shape

    def body(p_ref, g_ref):
        gsum = p_ref[0]
        for s in range(1, N_DEV):
            gsum = gsum + p_ref[s]
        g_ref[...] = gsum

    return pl.pallas_call(
        body, name=name, grid=(1,),
        in_specs=[pl.BlockSpec((N_DEV, r, c), lambda i: (0, 0, 0))],
        out_specs=pl.BlockSpec((r, c), lambda i: (0, 0)), out_shape=_sds((r, c), F32),
        compiler_params=_params(("arbitrary",)),
    )(parts)


def _adam_packed(name, w, g, m, v):
    r, c = w.shape

    def body(w_ref, g_ref, m_ref, v_ref, d_ref, m2_ref, v2_ref):
        delta, m2, v2 = _adam_math(w_ref[...], g_ref[...], m_ref[...], v_ref[...])
        d_ref[...] = delta
        m2_ref[...] = m2
        v2_ref[...] = v2

    spec = pl.BlockSpec((r, c), lambda i: (0, 0))
    return pl.pallas_call(
        body, name=name, grid=(1,), in_specs=[spec] * 4, out_specs=[spec] * 3, out_shape=[_sds((r, c), F32)] * 3,
        compiler_params=_params(("arbitrary",)),
    )(w, g, m, v)


def _pack(arrs):
    flat = jnp.concatenate([a.reshape(-1).astype(F32) for a in arrs])
    n = flat.shape[0]
    pad = (-n) % (SUBLANE * LANE)
    return jnp.pad(flat, (0, pad)).reshape(-1, LANE)


def _unpack(packed, like):
    flat = packed.reshape(-1)
    out, off = [], 0
    for a in like:
        n = math.prod(a.shape)
        out.append(flat[off:off + n].reshape(a.shape))
        off += n
    return out


def _row_blocks(full):
    return full.reshape(N_DEV, full.shape[0] // N_DEV, full.shape[1])


def _col_blocks(full):
    r, c = full.shape
    return full.reshape(r, N_DEV, c // N_DEV).transpose(1, 0, 2)


def _cols_natural(blocks):
    n, r, c = blocks.shape
    return blocks.transpose(1, 0, 2).reshape(r, n * c)


def kernel(x, ffn1_norm, ffn1_w1, ffn1_w2, mix_norm, ffn2_norm, ffn2_w1, ffn2_w2, ab_w_in, s5_lambda_re, s5_lambda_im, s5_b_re, s5_b_im, s5_c_re, s5_c_im, s5_log_dt, s5_d, s5_w_glu, gla_w_gk, gla_b_gk, gla_norm, ab_w_out, ret_w_in, ret_norm, ret_w_out, final_norm, loss_target, m_ffn1_norm, m_ffn1_w1, m_ffn1_w2, m_mix_norm, m_ffn2_norm, m_ffn2_w1, m_ffn2_w2, m_ab_w_in, m_s5_lambda_re, m_s5_lambda_im, m_s5_b_re, m_s5_b_im, m_s5_c_re, m_s5_c_im, m_s5_log_dt, m_s5_d, m_s5_w_glu, m_gla_w_gk, m_gla_b_gk, m_gla_norm, m_ab_w_out, m_ret_w_in, m_ret_norm, m_ret_w_out, m_final_norm, v_ffn1_norm, v_ffn1_w1, v_ffn1_w2, v_mix_norm, v_ffn2_norm, v_ffn2_w1, v_ffn2_w2, v_ab_w_in, v_s5_lambda_re, v_s5_lambda_im, v_s5_b_re, v_s5_b_im, v_s5_c_re, v_s5_c_im, v_s5_log_dt, v_s5_d, v_s5_w_glu, v_gla_w_gk, v_gla_b_gk, v_gla_norm, v_ab_w_out, v_ret_w_in, v_ret_norm, v_ret_w_out, v_final_norm):
    names = ['ffn1_norm', 'ffn1_w1', 'ffn1_w2', 'mix_norm', 'ffn2_norm', 'ffn2_w1', 'ffn2_w2', 'ab_w_in', 's5_lambda_re', 's5_lambda_im', 's5_b_re', 's5_b_im', 's5_c_re', 's5_c_im', 's5_log_dt', 's5_d', 's5_w_glu', 'gla_w_gk', 'gla_b_gk', 'gla_norm', 'ab_w_out', 'ret_w_in', 'ret_norm', 'ret_w_out', 'final_norm']
    loc = locals()
    W = {n: loc[n] for n in names}
    M = {n: loc["m_" + n] for n in names}
    V = {n: loc["v_" + n] for n in names}

    me = 4 * lax.axis_index("x") + 2 * lax.axis_index("y") + lax.axis_index("c")
    xs = x[0]
    tgt = loss_target[0]
    l, d = xs.shape
    depth = ffn1_norm.shape[0]

    pending, to_start = {}, []

    def start_gather(tag, shard):
        to_start.append((tag, shard))

    def finish_gather(tag, after):
        started, shard = pending.pop(tag)
        _, got = _exchange_wait("agw_" + tag, started, True, after)
        return lax.dynamic_update_index_in_dim(got, shard, me, 0)

    def finish_cols(tag, after):
        g = finish_gather(tag, after)
        return g.transpose(1, 0, 2).reshape(g.shape[1], -1)

    def finish_rows(tag, after):
        g = finish_gather(tag, after)
        return g.reshape(-1, g.shape[2])

    small_sharded = [gla_w_gk, gla_b_gk, ret_norm]
    for i in range(depth):
        j = i // 2
        start_gather(f"ffn1_w1_{i}", ffn1_w1[i].astype(BF))
        start_gather(f"ffn1_w2_{i}", ffn1_w2[i].astype(BF))
        if i % 2 == 0:
            start_gather(f"ab_w_in_{j}", ab_w_in[j].astype(BF))
            if i == 0:
                start_gather("small", _pack(small_sharded))
            start_gather(f"s5_w_glu_{j}", s5_w_glu[j].astype(BF))
            start_gather(f"ab_w_out_{j}", ab_w_out[j].astype(BF))
        else:
            start_gather(f"ret_w_in_{j}", ret_w_in[j].astype(BF))
            start_gather(f"ret_w_out_{j}", ret_w_out[j].astype(BF))
        start_gather(f"ffn2_w1_{i}", ffn2_w1[i].astype(BF))
        start_gather(f"ffn2_w2_{i}", ffn2_w2[i].astype(BF))
    for (tag, shard), started in zip(to_start, _gather_start_all("ags_weights", [s_ for _, s_ in to_start])):
        pending[tag] = (started, shard)
    started_all = started[4][0, 0]
    full = {}

    s5w = s5_d.shape[1]
    g_s5, n_s5 = s5_lambda_re.shape[2], s5_lambda_re.shape[3]
    hs = min(SUBLANE * LANE, g_s5 * n_s5)
    gla_hk = gla_w_gk.shape[-1] * N_DEV
    gla_dk = gla_hk // GLA_HEADS
    gla_hv = gla_norm.shape[1]
    gla_dv = gla_hv // GLA_HEADS
    ret_hv = ret_norm.shape[1] * N_DEV
    ret_dv = ret_hv // RET_HEADS
    ret_hk = (ret_w_in.shape[2] * N_DEV - 2 * ret_hv) // 2
    ret_dk = ret_hk // RET_HEADS
    assert s5w == gla_hv and 2 * gla_hk == s5w, "column blocks of the mixer projection assume these widths"
    assert ret_hv == 2 * ret_hk
    main_w = s5w + 2 * gla_hk + 2 * gla_hv
    gla_tb = _tile(l, 256, GLA_CHUNK)
    ret_chunk = min(RET_CHUNK, l)

    cos_t, sin_t = _rot_tables(l, ret_dk)
    lg_f = jnp.log1p(-jnp.exp2(-5.0 - jnp.arange(RET_HEADS, dtype=F32)))
    lgtab_f = jnp.broadcast_to(lg_f[:, None, None], (RET_HEADS, 1, LANE))
    lgtab_b = jnp.broadcast_to(lg_f[::-1][:, None, None], (RET_HEADS, 1, LANE))
    s5_pre = {}
    for j in range((depth + 1) // 2):
        s5_args = (s5_lambda_re[j], s5_lambda_im[j], s5_b_re[j], s5_b_im[j], s5_c_re[j], s5_c_im[j], s5_log_dt[j])
        (a_tab, bd, cd), s5_vjp = jax.vjp(lambda *a: _s5_chunk_tables(*a, hs), *s5_args)
        s5_pre[j] = (a_tab, bd.astype(BF), cd.astype(BF), s5_vjp)
    tables_done = jnp.stack([cos_t[0, 0], sin_t[0, 0], lgtab_f[0, 0, 0], lgtab_b[0, 0, 0]]
                            + [t[0].reshape(-1)[0] + t[1].reshape(-1)[0].astype(F32) + t[2].reshape(-1)[0].astype(F32) for t in s5_pre.values()])

    saved = []
    cur = xs
    hn = _rms_fwd("l0_ffn1_norm", cur, ffn1_norm[0:1] + started_all)
    for i in range(depth):
        j = i // 2
        s = {}
        s['x0'] = cur

        def first_w1(after):
            if i == 0:
                after = jnp.concatenate([after[0, 0:1].astype(F32), tables_done])
            return finish_cols(f"ffn1_w1_{i}", after)

        cur, h, s['ffn1'], s['f1w1'], s['f1w2'] = _ffn_fwd(f"l{i}_ffn1", cur, hn, first_w1, lambda after: finish_rows(f"ffn1_w2_{i}", after),
                                                            mix_norm[i:i + 1])
        s['x1'] = cur
        s['h'] = h
        if i % 2 == 0:
            w_in = finish_cols(f"ab_w_in_{j}", cur)
            if i == 0:
                got = finish_gather("small", cur)
                flat, off, joined = got.reshape(N_DEV, -1), 0, []
                for a in small_sharded:
                    n = math.prod(a.shape)
                    blk = jnp.moveaxis(flat[:, off:off + n].reshape((N_DEV,) + a.shape), 0, -2)
                    joined.append(blk.reshape(a.shape[:-1] + (N_DEV * a.shape[-1],)))
                    off += n
                full['gla_w_gk'], full['gla_b_gk'], ret_norm_full = joined[0].astype(BF), joined[1], joined[2]
            s['w_glu'], s['w_out'] = finish_rows(f"s5_w_glu_{j}", cur), finish_rows(f"ab_w_out_{j}", cur)
            w_main, w_glo = w_in[:, :main_w], w_in[:, main_w:]
            proj = _mm_plain(f"l{i}_proj", h, w_main, NN, BF, tm=512, tn=1024, tk=d, b_outer=True)
            glo = _mm_plain(f"l{i}_glo", h, w_glo, NN, F32, tm=1024, tn=2 * GLA_RANK, tk=d)
            a_tab, bd16, cd16, s5_vjp = s5_pre[j]
            tm = _tile(l, 512, SUBLANE)
            x_f, y_f = _s5_fwd(f"l{i}_s5_fwd_f", proj, bd16[0], cd16[0], a_tab[0], False)
            x_b, y_b = _s5_fwd(f"l{i}_s5_fwd_b", proj, bd16[1], cd16[1], a_tab[1], True)
            d_row = s5_d[j:j + 1]
            y, s5_out = _glu_fwd(f"l{i}_s5_glu", y_f, y_b, proj, d_row, s['w_glu'])
            zeros_r = jnp.zeros((GLA_RANK, gla_hk), BF)
            w_gk = full['gla_w_gk'][j]
            wgk_f = jnp.concatenate([w_gk[0], zeros_r], axis=0)
            wgk_b = jnp.concatenate([zeros_r, w_gk[1]], axis=0)
            b_gk = full['gla_b_gk'][j]
            g_f, g_b = _gate_fwd(f"l{i}_gla_gate", glo, wgk_f, wgk_b, b_gk[0:1], b_gk[1:2])
            qcol, kcol, vcol, ogcol = s5w // gla_hk, s5w // gla_hk + 1, (s5w + 2 * gla_hk) // gla_hv, (s5w + 2 * gla_hk) // gla_hv + 1
            lin_kw = dict(heads=GLA_HEADS, dk=gla_dk, dv=gla_dv, chunk=GLA_CHUNK, tb=gla_tb, qcol=qcol, kcol=kcol, vcol=vcol,
                          qscale=gla_dk ** -0.5)
            o_f, sp_f = _gla_fwd(f"l{i}_gla_fwd_f", proj, g_f, reverse=False, **lin_kw)
            o_b, sp_b = _gla_fwd(f"l{i}_gla_fwd_b", proj, g_b, o_f, reverse=True, **lin_kw)
            gla_out = _headgate_fwd(f"l{i}_gla_out", o_b, proj, ogcol, gla_norm[j:j + 1], gla_dv)
            w_out = s['w_out']

            row = pl.BlockSpec((tm, d), lambda ii, jj, kk: (ii, 0))
            vec = pl.BlockSpec((1, d), lambda ii, jj, kk: (0, 0))
            cur, hn = _mm(f"l{i}_mix_out",
                          [(s5_out, pl.BlockSpec((tm, s5w), lambda ii, jj, kk: (ii, 0)), w_out, pl.BlockSpec((s5w, d), lambda ii, jj, kk: (0, 0))),
                           (gla_out, pl.BlockSpec((tm, gla_hv), lambda ii, jj, kk: (ii, 0)), w_out, pl.BlockSpec((gla_hv, d), lambda ii, jj, kk: (1, 0)))],
                          NN, (l // tm, 1, 1), [_sds((l, d), F32), _sds((l, d), BF)], [row, row], (tm, d), _residual_epi(1.0, True),
                          [cur, ffn2_norm[i:i + 1]], [row, vec])
            s.update(proj=proj, glo=glo, s5_vjp=s5_vjp, a_tab=a_tab, bd16=bd16, cd16=cd16, x_f=x_f, x_b=x_b, y=y, s5_out=s5_out,
                     wgk_f=wgk_f, wgk_b=wgk_b, b_gk=b_gk, g_f=g_f, g_b=g_b, o_f=o_f, o_b=o_b, sp_f=sp_f, sp_b=sp_b, gla_out=gla_out,
                     w_main=w_main, w_glo=w_glo, lin_kw=lin_kw, ogcol=ogcol)
        else:
            w_in = finish_cols(f"ret_w_in_{j}", cur)
            s['w_in'], s['w_out'] = w_in, finish_rows(f"ret_w_out_{j}", cur)
            proj = _mm_plain(f"l{i}_proj", h, w_in, NN, BF, tm=512, tn=1024, tk=d, b_outer=True)
            qr, kr = _rot_apply(f"l{i}_rot", proj, 0, proj, 1, cos_t, sin_t, RET_HEADS, ret_dk, ret_dk ** -0.5, BF, False)
            ret_hb = 4
            lin_kw = dict(heads=RET_HEADS, hb=ret_hb, dk=ret_dk, dv=ret_dv, chunk=ret_chunk, tb=ret_chunk, qcol=0, kcol=0,
                          vcol=(2 * ret_hk) // (ret_hb * ret_dv),
                          qscale=1.0)
            o_f, sp_f = _lin_fwd(f"l{i}_ret_fwd_f", qr, kr, proj, None, lgtab_f, reverse=False, **lin_kw)
            o_b, sp_b = _lin_fwd(f"l{i}_ret_fwd_b", qr, kr, proj, None, lgtab_b, o_f, reverse=True, **lin_kw)
            ogcol = (2 * ret_hk + ret_hv) // ret_hv
            r_out = _headgate_fwd(f"l{i}_ret_out", o_b, proj, ogcol, ret_norm_full, ret_dv)

            tm = _tile(l, 512, SUBLANE)
            cur, hn = _mm_plain(f"l{i}_mix_out", r_out, s['w_out'], NN, F32, tm=512, tn=d, tk=ret_hv, epi=_residual_epi(1.0, True),
                                eins=[cur, ffn2_norm[i:i + 1]], especs=[None, pl.BlockSpec((1, d), lambda ii, jj, kk: (0, 0))],
                                extra_outs=[_sds((l, d), BF)], extra_specs=[pl.BlockSpec((tm, d), lambda ii, jj, kk: (ii, 0))])
            s.update(proj=proj, qr=qr, kr=kr, o_f=o_f, o_b=o_b, sp_f=sp_f, sp_b=sp_b, r_out=r_out, lin_kw=lin_kw, ogcol=ogcol)
        s['x2'] = cur
        cur, hn, s['ffn2'], s['f2w1'], s['f2w2'] = _ffn_fwd(f"l{i}_ffn2", cur, hn, lambda after: finish_cols(f"ffn2_w1_{i}", after),
                                                             lambda after: finish_rows(f"ffn2_w2_{i}", after),
                                                             ffn1_norm[i + 1:i + 2] if i + 1 < depth else None)
        saved.append(s)

    dx, d_final_norm, loss_row = _loss_head("loss_head", cur, final_norm.reshape(1, -1), tgt)
    loss = lax.psum(loss_row[0, 0], ("x", "y", "c"))

    G = {}
    big = {}
    G['final_norm'] = d_final_norm.reshape(-1)
    per_layer = {n: [None] * depth for n in ['ffn1_norm', 'mix_norm', 'ffn2_norm']}
    small_late = ['ffn1_norm', 'mix_norm']
    small_early = ['ffn2_norm', 's5_lambda_re', 's5_lambda_im', 's5_b_re', 's5_b_im', 's5_c_re', 's5_c_im',
                   's5_log_dt', 's5_d', 'gla_w_gk', 'gla_b_gk', 'gla_norm', 'ret_norm', 'final_norm']
    a2a, tok = {}, [jnp.zeros((), F32)]

    def start_a2a(tag, blocks):
        started = _exchange_start("a2as_" + tag, blocks, False)
        a2a[tag] = started
        tok[0] = tok[0] + started[4][0, 0]

    def dep(vec):
        return vec + tok[0]

    def proj_backward(tag, pieces, s, dres, a2a_tag):
        tm = _tile(l, 512, SUBLANE)
        row = pl.BlockSpec((tm, d), lambda ii, jj, kk: (ii, 0))
        vec = pl.BlockSpec((1, d), lambda ii, jj, kk: (0, 0))
        dws = [_mm_plain(f"{tag}_dwin_{k}", s['h'], piece, TN, BF, tm=d, tn=2048, tk=1024) for k, (piece, _, _) in enumerate(pieces)]
        start_a2a(a2a_tag, _col_blocks(jnp.concatenate(dws, axis=1)))
        pairs = []
        for piece, w, col in pieces:
            wd = piece.shape[1]
            pairs.append((piece, pl.BlockSpec((tm, wd), lambda ii, jj, kk: (ii, 0)),
                          w, pl.BlockSpec((d, wd), lambda ii, jj, kk, col=col: (0, col), pipeline_mode=pl.Buffered(1))))
        return _mm(f"{tag}_dh", pairs, NT, (l // tm, 1, 1), [_sds((l, d), F32), _sds((1, d), F32)], [row, vec], (tm, d),
                   _rms_bwd_epi(0), [s['x1'], dep(mix_norm[i:i + 1]), dres], [row, vec, row])

    for i in reversed(range(depth)):
        j = i // 2
        s = saved[i]
        def ffn_grads(which):
            def on_grads(dw1, dw2, gnorm):
                start_a2a(f"{which}_w1_{i}", _col_blocks(dw1))
                start_a2a(f"{which}_w2_{i}", _row_blocks(dw2))
                return dep(gnorm)
            return on_grads

        dx, dg = _ffn_bwd(f"l{i}_ffn2b", dx, s['x2'], ffn2_norm[i:i + 1], s['f2w1'], s['f2w2'], s['ffn2'], ffn_grads("ffn2"))
        per_layer['ffn2_norm'][i] = dg[0]
        tm = _tile(l, 512, SUBLANE)
        row = pl.BlockSpec((tm, d), lambda ii, jj, kk: (ii, 0))
        vec = pl.BlockSpec((1, d), lambda ii, jj, kk: (0, 0))
        if i % 2 == 0:
            proj, lin_kw = s['proj'], s['lin_kw']
            w_out = s['w_out']
            d_cat = _mm_plain(f"l{i}_dcat", dx, w_out, NT, BF, tm=512, tn=1024, tk=d)
            dwo_a = _mm_plain(f"l{i}_dwout_a", s['s5_out'], dx, TN, BF, tm=s5w, tn=d, tk=512)
            dwo_b = _mm_plain(f"l{i}_dwout_b", s['gla_out'], dx, TN, BF, tm=gla_hv, tn=d, tk=512)
            start_a2a(f"ab_w_out_{j}", _row_blocks(jnp.concatenate([dwo_a, dwo_b], axis=0)))
            do, dog, dgn = _headgate_bwd(f"l{i}_gla_outb", s['o_b'], proj, s['ogcol'], dep(gla_norm[j:j + 1]), d_cat, 1, gla_dv)
            G['gla_norm'] = dgn
            dq, dk_, dv_, dgf = _gla_bwd(f"l{i}_gla_bwd_f", proj, s['g_f'], s['sp_f'], do, None, reverse=False, **lin_kw)
            dq, dk_, dv_, dgb = _gla_bwd(f"l{i}_gla_bwd_b", proj, s['g_b'], s['sp_b'], do, (dq, dk_, dv_), reverse=True, **lin_kw)
            dglo, dwf, dwb, dbf, dbb = _gate_bwd(f"l{i}_gla_gateb", s['glo'], s['wgk_f'], s['wgk_b'], s['b_gk'][0:1], s['b_gk'][1:2], dgf, dgb)
            G['gla_w_gk'] = jnp.stack([dwf[:GLA_RANK], dwb[GLA_RANK:]], axis=0)[None]
            G['gla_b_gk'] = jnp.concatenate([dbf, dbb], axis=0)[None]
            dy, dwglu = _glu_bwd(f"l{i}_s5_glub", s['y'], s['w_glu'], d_cat, 0)
            start_a2a(f"s5_w_glu_{j}", _row_blocks(dwglu.astype(BF)))
            cd16, bd16, a_tab = s['cd16'], s['bd16'], s['a_tab']
            nt2 = a_tab.shape[2]
            a_conj = a_tab * jnp.where(jnp.arange(nt2) < nt2 // 2, 1.0, -1.0)[None, None, :, None]
            du_f, dbd_f, dcd_f, da_f = _s5_bwd(f"l{i}_s5_bwd_f", proj, dy, s['x_f'], bd16[0], cd16[0], a_conj[0], True)
            du_b, dbd_b, dcd_b, da_b = _s5_bwd(f"l{i}_s5_bwd_b", proj, dy, s['x_b'], bd16[1], cd16[1], a_conj[1], False)
            du, dd = _s5_du(f"l{i}_s5_du", du_f, du_b, dy, proj, s5_d[j:j + 1])
            G['s5_d'] = dd
            cot = (jnp.stack([da_f, da_b]), jnp.stack([dbd_f, dbd_b]), jnp.stack([dcd_f, dcd_b]))
            g_lre, g_lim, g_bre, g_bim, g_cre, g_cim, g_ldt = s['s5_vjp'](cot)
            G['s5_lambda_re'], G['s5_lambda_im'], G['s5_b_re'], G['s5_b_im'] = g_lre[None], g_lim[None], g_bre[None], g_bim[None]
            G['s5_c_re'], G['s5_c_im'], G['s5_log_dt'] = g_cre[None], g_cim[None], g_ldt[None]
            if i == 0:
                G['ffn2_norm'] = jnp.stack(per_layer['ffn2_norm'], axis=0)
                early_packed = _pack([G[n] for n in small_early])
                early_started = _exchange_start("ags_small_grads_early", early_packed, True)
                tok[0] = tok[0] + early_started[4][0, 0]
            w_main, w_glo = s['w_main'], s['w_glo']
            pieces = [(du, w_main, 0), (dq, w_main, s5w // gla_hk), (dk_, w_main, s5w // gla_hk + 1),
                      (dv_, w_main, (s5w + 2 * gla_hk) // gla_hv), (dog, w_main, (s5w + 2 * gla_hk) // gla_hv + 1), (dglo, w_glo, 0)]
            dx, dg = proj_backward(f"l{i}", pieces, s, dx, f"ab_w_in_{j}")
        else:
            proj, lin_kw = s['proj'], s['lin_kw']
            w_out = s['w_out']
            d_ro = _mm_plain(f"l{i}_dro", dx, w_out, NT, BF, tm=512, tn=1024, tk=d, b_outer=True)
            dwo = _mm_plain(f"l{i}_dwout", s['r_out'], dx, TN, BF, tm=2048, tn=d, tk=1024)
            start_a2a(f"ret_w_out_{j}", _row_blocks(dwo))
            do, dog, dgn = _headgate_bwd(f"l{i}_ret_outb", s['o_b'], proj, s['ogcol'], dep(ret_norm_full), d_ro, 0, ret_dv)
            G['ret_norm'] = dgn
            r1 = _lin_bwd(f"l{i}_ret_bwd_f", s['qr'], s['kr'], proj, None, lgtab_f, s['sp_f'], do, None, reverse=False, **lin_kw)
            r2 = _lin_bwd(f"l{i}_ret_bwd_b", s['qr'], s['kr'], proj, None, lgtab_b, s['sp_b'], do, r1, reverse=True, **lin_kw)
            dqr, dkr, dv_ = r2
            dq, dk_ = _rot_apply(f"l{i}_rotb", dqr, 0, dkr, 0, cos_t, sin_t, RET_HEADS, ret_dk, ret_dk ** -0.5, BF, True)
            w_in = s['w_in']
            pieces = [(dq, w_in, 0), (dk_, w_in, 1), (dv_, w_in, (2 * ret_hk) // ret_hv), (dog, w_in, (2 * ret_hk) // ret_hv + 1)]
            dx, dg = proj_backward(f"l{i}", pieces, s, dx, f"ret_w_in_{j}")
        per_layer['mix_norm'][i] = dg[0]
        dx, dg = _ffn_bwd(f"l{i}_ffn1b", dx, s['x0'], ffn1_norm[i:i + 1], s['f1w1'], s['f1w2'], s['ffn1'], ffn_grads("ffn1"))
        per_layer['ffn1_norm'][i] = dg[0]
    for n in small_late:
        G[n] = jnp.stack(per_layer[n], axis=0)
    grad_x = dx[None]

    out_g, out_d, out_m, out_v = {}, {}, {}, {}
    small = small_early + small_late
    packed = _pack([G[n] for n in small_late])
    small_started = _exchange_start("ags_small_grads_late", packed, True)

    def big_update(n, layers):
        parts = []
        for i in layers:
            blocks, got = _exchange_wait(f"a2aw_{n}_{i}", a2a.pop(f"{n}_{i}"), False, small_started[4])
            parts.append(lax.dynamic_update_index_in_dim(got, lax.dynamic_index_in_dim(blocks, me, 0, keepdims=False), me, 0))
        out_g[n], out_d[n], out_m[n], out_v[n] = _reduce_adam("upd_" + n, parts, W[n], M[n], V[n])

    for n in ['ffn2_w1', 'ffn2_w2', 'ffn1_w1', 'ffn1_w2']:
        big_update(n, range(depth))
    for n in ['ab_w_in', 's5_w_glu', 'ab_w_out', 'ret_w_in', 'ret_w_out']:
        big_update(n, [0])
    assert not a2a and not pending

    g_full = {}
    for tag, started, mine, group in (("early", early_started, early_packed, small_early), ("late", small_started, packed, small_late)):
        _, gathered = _exchange_wait("agw_small_grads_" + tag, started, True, out_v['ret_w_out'])
        gathered = lax.dynamic_update_index_in_dim(gathered, mine, me, 0)
        summed = _reduce8("sum_small_grads_" + tag, gathered)
        g_full.update(zip(group, _unpack(summed, [G[n] for n in group])))
    g_small = {}
    for n in small:
        gf = g_full[n]
        if n in ('gla_w_gk', 'gla_b_gk', 'ret_norm'):
            width = W[n].shape[-1]
            gf = lax.dynamic_slice_in_dim(gf, me * width, width, axis=gf.ndim - 1)
        g_small[n] = gf.reshape(W[n].shape)
    pw, pg, pm, pv = (_pack([src[n] for n in small]) for src in (W, g_small, M, V))
    pd, pm2, pv2 = _adam_packed("upd_small", pw, pg, pm, pv)
    like = [W[n] for n in small]
    for n, dd_, mm_, vv_ in zip(small, _unpack(pd, like), _unpack(pm2, like), _unpack(pv2, like)):
        out_g[n], out_d[n], out_m[n], out_v[n] = g_small[n], dd_, mm_, vv_

    return (loss, grad_x, *[out_g[n] for n in names], *[out_d[n] for n in names], *[out_m[n] for n in names], *[out_v[n] for n in names])
```

```python
import math

import jax
import jax.numpy as jnp
from jax import lax
from jax.experimental import pallas as pl
from jax.experimental.pallas import tpu as pltpu

F32 = jnp.float32
BF = jnp.bfloat16
N_DEV = 8
EPS = 1e-6
S5_GROUP = 16
GLA_HEADS = 4
GLA_RANK = 16
GLA_GATE_NORM = 16.0
RET_HEADS = 8
ROPE_BASE = 10000.0
GLA_CHUNK = 64
RET_CHUNK = 256
ADAM_LR, ADAM_B1, ADAM_B2, ADAM_EPS, ADAM_WD, ADAM_STEP = 0.001, 0.9, 0.999, 1e-08, 0.01, 10
VMEM_LIMIT_BYTES = 56 * 1024 * 1024
LANE = 128
SUBLANE = 8

NN = (((1,), (0,)), ((), ()))
NT = (((1,), (1,)), ((), ()))
TN = (((0,), (0,)), ((), ()))


def _tile(n, pref, align):
    if n <= pref:
        return n
    t = (pref // align) * align
    while t >= align:
        if n % t == 0:
            return t
        t -= align
    return n


def _params(sem):
    return pltpu.CompilerParams(dimension_semantics=sem, vmem_limit_bytes=VMEM_LIMIT_BYTES)


def _dot(a, b, dims=NN):
    return lax.dot_general(a.astype(BF), b.astype(BF), dims, preferred_element_type=F32)


def _dot3(m01, g, dims=NN):
    g1 = g.astype(BF)
    r1 = g - g1.astype(F32)
    g2 = r1.astype(BF)
    g3 = (r1 - g2.astype(F32)).astype(BF)
    m = m01.astype(BF)
    return (lax.dot_general(m, g1, dims, preferred_element_type=F32)
            + lax.dot_general(m, g2, dims, preferred_element_type=F32)
            + lax.dot_general(m, g3, dims, preferred_element_type=F32))


def _sigmoid(x):
    return 0.5 + 0.5 * jnp.tanh(0.5 * x)


def _mm(name, pairs, dims, grid, outs, out_specs, acc_shape, epi=None, eins=(), especs=()):
    n_p, n_e, n_o = len(pairs), len(eins), len(outs)
    nk = grid[2]

    def body(*refs):
        a_refs = refs[0:2 * n_p:2]
        b_refs = refs[1:2 * n_p:2]
        e_refs = refs[2 * n_p:2 * n_p + n_e]
        o_refs = refs[2 * n_p + n_e:2 * n_p + n_e + n_o]
        acc = refs[-1]
        ids = (pl.program_id(0), pl.program_id(1), pl.program_id(2))

        part = _dot(a_refs[0][...], b_refs[0][...], dims)
        for p in range(1, n_p):
            part = part + _dot(a_refs[p][...], b_refs[p][...], dims)

        def finish(total):
            if epi is None:
                o_refs[0][...] = total.astype(o_refs[0].dtype)
            else:
                epi(total, e_refs, o_refs, ids)

        if nk == 1:
            finish(part)
        else:
            @pl.when(ids[2] == 0)
            def _():
                acc[...] = part

            @pl.when(ids[2] > 0)
            def _():
                acc[...] += part

            @pl.when(ids[2] == nk - 1)
            def _():
                finish(acc[...])

    in_specs, args = [], []
    for a, a_spec, b, b_spec in pairs:
        in_specs += [a_spec, b_spec]
        args += [a, b]
    in_specs += list(especs)
    args += list(eins)
    res = pl.pallas_call(
        body, name=name, grid=grid, in_specs=in_specs, out_specs=list(out_specs), out_shape=list(outs),
        scratch_shapes=[pltpu.VMEM(acc_shape, F32)],
        compiler_params=_params(("arbitrary", "arbitrary", "arbitrary")),
    )(*args)
    return res


def _sds(shape, dtype):
    return jax.ShapeDtypeStruct(shape, dtype)


def _mm_plain(name, a, b, dims, out_dtype, tm=512, tn=1024, tk=1024, epi=None, eins=(), especs=None, extra_outs=(), extra_specs=(),
              b_outer=False):
    if dims == NN:
        (m, k), n = a.shape, b.shape[1]
    elif dims == NT:
        (m, k), n = a.shape, b.shape[0]
    else:
        (k, m), n = a.shape, b.shape[1]
    tm, tn = _tile(m, tm, LANE if dims == TN else SUBLANE), _tile(n, tn, LANE)
    tk = _tile(k, tk, SUBLANE if dims == TN else LANE)
    grid = (n // tn, m // tm, k // tk) if b_outer else (m // tm, n // tn, k // tk)

    def spec(block, index):
        if b_outer:
            return pl.BlockSpec(block, lambda j, i, kk: index(i, j, kk))
        return pl.BlockSpec(block, index)

    if dims == NN:
        a_spec = spec((tm, tk), lambda i, j, kk: (i, kk))
        b_spec = spec((tk, tn), lambda i, j, kk: (kk, j))
    elif dims == NT:
        a_spec = spec((tm, tk), lambda i, j, kk: (i, kk))
        b_spec = spec((tn, tk), lambda i, j, kk: (j, kk))
    else:
        a_spec = spec((tk, tm), lambda i, j, kk: (kk, i))
        b_spec = spec((tk, tn), lambda i, j, kk: (kk, j))
    o_spec = spec((tm, tn), lambda i, j, kk: (i, j))
    if especs is None:
        especs = [o_spec] * len(eins)
    else:
        especs = [o_spec if s is None else s for s in especs]
    res = _mm(name, [(a, a_spec, b, b_spec)], dims, grid, [_sds((m, n), out_dtype)] + list(extra_outs),
              [o_spec] + list(extra_specs), (tm, tn), epi, eins, especs)
    return res if extra_outs else res[0]


def _rms_fwd(name, x, g):
    l, d = x.shape
    tm = _tile(l, 1024, SUBLANE)

    def body(x_ref, g_ref, o_ref):
        xv = x_ref[...]
        r = lax.rsqrt(jnp.mean(xv * xv, axis=-1, keepdims=True) + EPS)
        o_ref[...] = (xv * r * g_ref[...]).astype(o_ref.dtype)

    return pl.pallas_call(
        body, name=name, grid=(l // tm,),
        in_specs=[pl.BlockSpec((tm, d), lambda i: (i, 0)), pl.BlockSpec((1, d), lambda i: (0, 0))],
        out_specs=pl.BlockSpec((tm, d), lambda i: (i, 0)), out_shape=_sds((l, d), BF),
        compiler_params=_params(("arbitrary",)),
    )(x, g)


def _rms_bwd_epi(first_axis):
    def epi(acc, e_refs, o_refs, ids):
        x_ref, g_ref, dr_ref = e_refs
        dx_ref, dg_ref = o_refs
        xv = x_ref[...]
        r = lax.rsqrt(jnp.mean(xv * xv, axis=-1, keepdims=True) + EPS)
        xh = xv * r
        dxh = acc * g_ref[...]
        dx_ref[...] = dr_ref[...] + r * (dxh - xh * jnp.mean(dxh * xh, axis=-1, keepdims=True))
        part = jnp.sum(acc * xh, axis=0, keepdims=True)

        @pl.when(ids[first_axis] == 0)
        def _():
            dg_ref[...] = part

        @pl.when(ids[first_axis] > 0)
        def _():
            dg_ref[...] += part

    return epi


def _loss_head(name, x, g, target):
    l, d = x.shape
    tm = _tile(l, 512, SUBLANE)
    n = l // tm

    def body(x_ref, g_ref, t_ref, dx_ref, dg_ref, loss_ref, lacc):
        i = pl.program_id(0)
        xv = x_ref[...]
        r = lax.rsqrt(jnp.mean(xv * xv, axis=-1, keepdims=True) + EPS)
        xh = xv * r
        e = xh * g_ref[...] - t_ref[...]
        dy = e * (1.0 / d)
        dxh = dy * g_ref[...]
        dx_ref[...] = r * (dxh - xh * jnp.mean(dxh * xh, axis=-1, keepdims=True))
        dg_part = jnp.sum(dy * xh, axis=0, keepdims=True)
        l_part = jnp.sum(e * e, axis=0, keepdims=True)

        @pl.when(i == 0)
        def _():
            dg_ref[...] = dg_part
            lacc[...] = l_part

        @pl.when(i > 0)
        def _():
            dg_ref[...] += dg_part
            lacc[...] += l_part

        @pl.when(i == n - 1)
        def _():
            loss_ref[...] = jnp.zeros_like(loss_ref) + jnp.sum(lacc[...]) * (0.5 / d)

    return pl.pallas_call(
        body, name=name, grid=(n,),
        in_specs=[pl.BlockSpec((tm, d), lambda i: (i, 0)), pl.BlockSpec((1, d), lambda i: (0, 0)),
                  pl.BlockSpec((tm, d), lambda i: (i, 0))],
        out_specs=[pl.BlockSpec((tm, d), lambda i: (i, 0)), pl.BlockSpec((1, d), lambda i: (0, 0)),
                   pl.BlockSpec((1, LANE), lambda i: (0, 0))],
        out_shape=[_sds((l, d), F32), _sds((1, d), F32), _sds((1, LANE), F32)],
        scratch_shapes=[pltpu.VMEM((1, d), F32)],
        compiler_params=_params(("arbitrary",)),
    )(x, g, target)


def _ffn_up(name, hn, w1):
    l, d = hn.shape
    f = w1.shape[1] // 2
    tm, tn = _tile(l, 512, SUBLANE), _tile(f, 1408, LANE)
    nj = f // tn

    def body(h_ref, wg_ref, wu_ref, gu_ref, a_ref):
        h = h_ref[...]
        g = jnp.dot(h, wg_ref[...], preferred_element_type=F32)
        u = jnp.dot(h, wu_ref[...], preferred_element_type=F32)
        s = _sigmoid(g)
        gs = g * s
        gu_ref[0] = (u * (s + gs * (1.0 - s))).astype(BF)
        gu_ref[1] = gs.astype(BF)
        a_ref[...] = (gs * u).astype(BF)

    return pl.pallas_call(
        body, name=name, grid=(nj, l // tm),
        in_specs=[pl.BlockSpec((tm, d), lambda j, i: (i, 0)), pl.BlockSpec((d, tn), lambda j, i: (0, j)),
                  pl.BlockSpec((d, tn), lambda j, i: (0, j + nj))],
        out_specs=[pl.BlockSpec((2, tm, tn), lambda j, i: (0, i, j)), pl.BlockSpec((tm, tn), lambda j, i: (i, j))],
        out_shape=[_sds((2, l, f), BF), _sds((l, f), BF)],
        compiler_params=_params(("arbitrary", "arbitrary")),
    )(hn, w1, w1)


def _residual_epi(scale, with_norm):
    def epi(acc, e_refs, o_refs, ids):
        xn = e_refs[0][...] + scale * acc
        o_refs[0][...] = xn
        if with_norm:
            r = lax.rsqrt(jnp.mean(xn * xn, axis=-1, keepdims=True) + EPS)
            o_refs[1][...] = (xn * r * e_refs[1][...]).astype(o_refs[1].dtype)

    return epi


def _ffn_fwd(tag, x, hn, get_w1, get_w2, next_gnorm):
    w1 = get_w1(hn)
    gu, a = _ffn_up(tag + "_up", hn, w1)
    w2 = get_w2(a)
    l, d = x.shape
    if next_gnorm is None:
        x_new = _mm_plain(tag + "_down", a, w2, NN, F32, tm=512, tn=d, tk=w2.shape[0], epi=_residual_epi(0.5, False), eins=[x])
        hn_next = None
    else:
        vec = pl.BlockSpec((1, d), lambda i, j, kk: (0, 0))
        tm = _tile(l, 512, SUBLANE)
        x_new, hn_next = _mm_plain(tag + "_down", a, w2, NN, F32, tm=512, tn=d, tk=w2.shape[0], epi=_residual_epi(0.5, True),
                                   eins=[x, next_gnorm], especs=[None, vec],
                                   extra_outs=[_sds((l, d), BF)], extra_specs=[pl.BlockSpec((tm, d), lambda i, j, kk: (i, 0))])
    return x_new, hn_next, (hn, gu, a), w1, w2


def _ffn_bwd(tag, dres, x, gnorm, w1, w2, saved, on_grads):
    hn, gu, a = saved
    l, d = x.shape
    f = w2.shape[0]
    tm, tn = _tile(l, 512, SUBLANE), _tile(f, 1408, LANE)
    nj = f // tn

    def epi_gu(acc, e_refs, o_refs, ids):
        da = 0.5 * acc
        o_refs[0][0] = (da * e_refs[0][0].astype(F32)).astype(BF)
        o_refs[0][1] = (da * e_refs[0][1].astype(F32)).astype(BF)

    gu_spec = pl.BlockSpec((2, tm, tn), lambda j, i, kk: (0, i, j))
    dgu = _mm(tag + "_dgu",
              [(dres, pl.BlockSpec((tm, d), lambda j, i, kk: (i, 0)), w2, pl.BlockSpec((tn, d), lambda j, i, kk: (j, 0)))],
              NT, (nj, l // tm, 1), [_sds((2, l, f), BF)], [gu_spec], (tm, tn), epi_gu, [gu], [gu_spec])[0]

    def epi_half(acc, e_refs, o_refs, ids):
        o_refs[0][...] = (0.5 * acc).astype(BF)

    dw2 = _mm_plain(tag + "_dw2", a, dres, TN, BF, tm=1408, tn=d, tk=1024, epi=epi_half)

    tk = _tile(l, 1024, SUBLANE)
    dw1 = _mm(tag + "_dw1",
              [(hn, pl.BlockSpec((tk, d), lambda i, j, kk: (kk, 0)), dgu, pl.BlockSpec((None, tk, f), lambda i, j, kk: (j, kk, 0)))],
              TN, (1, 2, l // tk), [_sds((d, 2 * f), BF)], [pl.BlockSpec((d, f), lambda i, j, kk: (0, j))], (d, f))[0]

    gnorm = on_grads(dw1, dw2, gnorm)
    row = pl.BlockSpec((tm, d), lambda i, j, kk: (i, 0))
    vec = pl.BlockSpec((1, d), lambda i, j, kk: (0, 0))
    once = pl.Buffered(1)
    dx, dg = _mm(tag + "_dhn",
                 [(dgu, pl.BlockSpec((None, tm, f), lambda i, j, kk: (0, i, 0)), w1, pl.BlockSpec((d, f), lambda i, j, kk: (0, 0), pipeline_mode=once)),
                  (dgu, pl.BlockSpec((None, tm, f), lambda i, j, kk: (1, i, 0)), w1, pl.BlockSpec((d, f), lambda i, j, kk: (0, 1), pipeline_mode=once))],
                 NT, (l // tm, 1, 1), [_sds((l, d), F32), _sds((1, d), F32)], [row, vec], (tm, d),
                 _rms_bwd_epi(0), [x, gnorm, dres], [row, vec, row])
    return dx, dg


def _s5_chunk_tables(lam_re, lam_im, b_re, b_im, c_re, c_im, log_dt, hs):
    f32 = F32
    g, n = lam_re.shape[1], lam_re.shape[2]
    p = b_re.shape[-1]
    nch, gpc, nt = (g * n) // hs, hs // n, hs // LANE
    lr = jnp.minimum(lam_re.astype(f32), -1e-4)
    li = lam_im.astype(f32)
    dt = jnp.exp(log_dt.astype(f32))[..., None]
    mag = jnp.exp(lr * dt)
    ar = mag * jnp.cos(li * dt)
    ai = mag * jnp.sin(li * dt)
    den = lr * lr + li * li
    cr = ((ar - 1.0) * lr + ai * li) / den
    ci = (ai * lr - (ar - 1.0) * li) / den
    bbr = cr[..., None] * b_re - ci[..., None] * b_im
    bbi = cr[..., None] * b_im + ci[..., None] * b_re
    a_f = jnp.stack([ar, ai], axis=1).reshape(2, 2, nch, nt, LANE).transpose(0, 2, 1, 3, 4).reshape(2, nch, 2 * nt, LANE)
    rows_g = jnp.arange(gpc * p) // p
    cols_g = (jnp.arange(2 * hs) % hs) // n
    diag = (rows_g[:, None] == cols_g[None, :]).astype(f32)
    bb = jnp.stack([bbr, bbi], axis=1).reshape(2, 2, nch, hs, p)
    bd = jnp.tile(bb.transpose(0, 2, 4, 1, 3).reshape(2, nch, p, 2 * hs), (1, 1, gpc, 1)) * diag
    cc = jnp.stack([c_re, -c_im], axis=1).reshape(2, 2, nch, gpc, p, n)
    cd = jnp.tile(cc.transpose(0, 2, 4, 1, 3, 5).reshape(2, nch, p, 2 * hs), (1, 1, gpc, 1)) * diag
    return a_f, bd, cd


def _fold_store(ref, val, tb, ntiles):
    for s in range(ntiles):
        ref[:, s * SUBLANE:(s + 1) * SUBLANE, :] = val[:, s * LANE:(s + 1) * LANE].reshape(tb // SUBLANE, SUBLANE, LANE)


def _unfold(ref, tb, ntiles):
    return jnp.concatenate([ref[:, s * SUBLANE:(s + 1) * SUBLANE, :].reshape(tb, LANE) for s in range(ntiles)], axis=1)


def _s5_fwd(name, proj, bd, cd, a_f, reverse):
    l = proj.shape[0]
    nch, cu, hs2 = bd.shape
    nt = hs2 // (2 * LANE)
    frows = 2 * nt * SUBLANE
    tb = _tile(l, 512, SUBLANE)
    nb = l // tb

    def body(u_ref, bd_ref, cd_ref, a_ref, xf_ref, y_ref, st):
        r = pl.program_id(1)

        @pl.when(r == 0)
        def _():
            st[...] = jnp.zeros_like(st)

        _fold_store(xf_ref, _dot(u_ref[...], bd_ref[...]), tb, 2 * nt)
        ar, ai = a_ref[0:nt, :], a_ref[nt:2 * nt, :]

        def group(gi, carry):
            rr = (tb // SUBLANE - 1 - gi) if reverse else gi
            sr, si = carry
            for qq in range(SUBLANE):
                q = (SUBLANE - 1 - qq) if reverse else qq
                re_rows, im_rows = pl.ds(q, nt, stride=SUBLANE), pl.ds(nt * SUBLANE + q, nt, stride=SUBLANE)
                nr = ar * sr - ai * si + xf_ref[rr, re_rows, :]
                ni = ar * si + ai * sr + xf_ref[rr, im_rows, :]
                xf_ref[rr, re_rows, :] = nr
                xf_ref[rr, im_rows, :] = ni
                sr, si = nr, ni
            return sr, si

        fin = lax.fori_loop(0, tb // SUBLANE, group, (st[0:nt, :], st[nt:2 * nt, :]))
        st[0:nt, :] = fin[0]
        st[nt:2 * nt, :] = fin[1]
        y_ref[...] = _dot(_unfold(xf_ref, tb, 2 * nt), cd_ref[...], NT)

    def rows(r):
        return (nb - 1 - r) if reverse else r

    return pl.pallas_call(
        body, name=name, grid=(nch, nb),
        in_specs=[pl.BlockSpec((tb, cu), lambda c, r: (rows(r), c)), pl.BlockSpec((None, cu, hs2), lambda c, r: (c, 0, 0)),
                  pl.BlockSpec((None, cu, hs2), lambda c, r: (c, 0, 0)), pl.BlockSpec((None, 2 * nt, LANE), lambda c, r: (c, 0, 0))],
        out_specs=[pl.BlockSpec((tb // SUBLANE, frows, LANE), lambda c, r: (rows(r), c, 0)), pl.BlockSpec((tb, cu), lambda c, r: (rows(r), c))],
        out_shape=[_sds((l // SUBLANE, nch * frows, LANE), F32), _sds((l, nch * cu), F32)],
        scratch_shapes=[pltpu.VMEM((2 * nt, LANE), F32)],
        compiler_params=_params(("arbitrary", "arbitrary")),
    )(proj, bd, cd, a_f)


def _s5_bwd(name, proj, dy, xf, bd, cd, a_conj, reverse):
    l = proj.shape[0]
    nch, cu, hs2 = bd.shape
    nt = hs2 // (2 * LANE)
    frows = 2 * nt * SUBLANE
    tb = _tile(l, 512, SUBLANE)
    nb = l // tb

    def body(u_ref, dy_ref, xs_ref, bd_ref, cd_ref, a_ref, du_ref, dbd_ref, dcd_ref, da_ref, lam, st):
        r = pl.program_id(1)

        @pl.when(r == 0)
        def _():
            st[...] = jnp.zeros_like(st)
            dbd_ref[...] = jnp.zeros_like(dbd_ref)
            dcd_ref[...] = jnp.zeros_like(dcd_ref)
            da_ref[...] = jnp.zeros_like(da_ref)

        dyv = dy_ref[...]
        _fold_store(lam, _dot(dyv, cd_ref[...]), tb, 2 * nt)
        ar, ai = a_ref[0:nt, :], a_ref[nt:2 * nt, :]

        def group(gi, carry):
            rr = (tb // SUBLANE - 1 - gi) if reverse else gi
            sr, si, cr, ci = carry
            for qq in range(SUBLANE):
                q = (SUBLANE - 1 - qq) if reverse else qq
                re_rows, im_rows = pl.ds(q, nt, stride=SUBLANE), pl.ds(nt * SUBLANE + q, nt, stride=SUBLANE)
                xr, xi = xs_ref[rr, re_rows, :], xs_ref[rr, im_rows, :]
                cr = cr + sr * xr + si * xi
                ci = ci + si * xr - sr * xi
                nr = ar * sr - ai * si + lam[rr, re_rows, :]
                ni = ar * si + ai * sr + lam[rr, im_rows, :]
                lam[rr, re_rows, :] = nr
                lam[rr, im_rows, :] = ni
                sr, si = nr, ni
            return sr, si, cr, ci

        zero = jnp.zeros((nt, LANE), F32)
        fin = lax.fori_loop(0, tb // SUBLANE, group, (st[0:nt, :], st[nt:2 * nt, :], zero, zero))
        st[0:nt, :] = fin[0]
        st[nt:2 * nt, :] = fin[1]
        da_ref[0:nt, :] += fin[2]
        da_ref[nt:2 * nt, :] += fin[3]
        lam_u = _unfold(lam, tb, 2 * nt)
        du_ref[...] = _dot(lam_u, bd_ref[...], NT)
        dbd_ref[...] += _dot(u_ref[...], lam_u, TN)
        dcd_ref[...] += _dot(dyv, _unfold(xs_ref, tb, 2 * nt), TN)

    def rows(r):
        return (nb - 1 - r) if reverse else r

    chunk_rows = pl.BlockSpec((tb, cu), lambda c, r: (rows(r), c))
    bd_spec = pl.BlockSpec((None, cu, hs2), lambda c, r: (c, 0, 0))
    cd_spec = bd_spec
    a_spec = pl.BlockSpec((None, 2 * nt, LANE), lambda c, r: (c, 0, 0))
    return pl.pallas_call(
        body, name=name, grid=(nch, nb),
        in_specs=[chunk_rows, chunk_rows, pl.BlockSpec((tb // SUBLANE, frows, LANE), lambda c, r: (rows(r), c, 0)), bd_spec, cd_spec, a_spec],
        out_specs=[chunk_rows, bd_spec, cd_spec, a_spec],
        out_shape=[_sds((l, nch * cu), F32), _sds((nch, cu, hs2), F32), _sds((nch, cu, hs2), F32), _sds((nch, 2 * nt, LANE), F32)],
        scratch_shapes=[pltpu.VMEM((tb // SUBLANE, frows, LANE), F32), pltpu.VMEM((2 * nt, LANE), F32)],
        compiler_params=_params(("arbitrary", "arbitrary")),
    )(proj, dy, xf, bd, cd, a_conj)


def _s5_du(name, du_f, du_b, dy, proj, d_row):
    l, w = dy.shape
    tm = _tile(l, 1024, SUBLANE)

    def body(f_ref, b_ref, dy_ref, u_ref, d_ref, du_ref, dd_ref):
        i = pl.program_id(0)
        dyv = dy_ref[...]
        du_ref[...] = (f_ref[...] + b_ref[...] + dyv * d_ref[...]).astype(du_ref.dtype)
        part = jnp.sum(dyv * u_ref[...].astype(F32), axis=0, keepdims=True)

        @pl.when(i == 0)
        def _():
            dd_ref[...] = part

        @pl.when(i > 0)
        def _():
            dd_ref[...] += part

    row = pl.BlockSpec((tm, w), lambda i: (i, 0))
    vec = pl.BlockSpec((1, w), lambda i: (0, 0))
    return pl.pallas_call(
        body, name=name, grid=(l // tm,), in_specs=[row, row, row, row, vec], out_specs=[row, vec],
        out_shape=[_sds((l, w), BF), _sds((1, w), F32)],
        compiler_params=_params(("arbitrary",)),
    )(du_f, du_b, dy, proj, d_row)


def _gelu(y):
    c = math.sqrt(2.0 / math.pi)
    return 0.5 * y * (1.0 + jnp.tanh(c * (y + 0.044715 * y * y * y)))


def _gelu_grad(y):
    c = math.sqrt(2.0 / math.pi)
    th = jnp.tanh(c * (y + 0.044715 * y * y * y))
    return 0.5 * (1.0 + th) + 0.5 * y * (1.0 - th * th) * c * (1.0 + 3.0 * 0.044715 * y * y)


def _glu_fwd(name, y_f, y_b, proj, d_row, w):
    l, wd = y_f.shape
    tm = _tile(l, 512, SUBLANE)

    def body(yf_ref, yb_ref, u_ref, d_ref, w_ref, y_ref, o_ref):
        y = yf_ref[...] + yb_ref[...] + u_ref[...].astype(F32) * d_ref[...]
        y_ref[...] = y
        gy = _gelu(y)
        z = _dot(gy, w_ref[...])
        o_ref[...] = (gy * _sigmoid(z)).astype(o_ref.dtype)

    row = pl.BlockSpec((tm, wd), lambda i: (i, 0))
    return pl.pallas_call(
        body, name=name, grid=(l // tm,),
        in_specs=[row, row, row, pl.BlockSpec((1, wd), lambda i: (0, 0)), pl.BlockSpec((wd, wd), lambda i: (0, 0))],
        out_specs=[row, row], out_shape=[_sds((l, wd), F32), _sds((l, wd), BF)],
        compiler_params=_params(("arbitrary",)),
    )(y_f, y_b, proj, d_row, w)


def _glu_bwd(name, y, w, dout, dcol):
    l, wd = y.shape
    tm = _tile(l, 512, SUBLANE)

    def body(y_ref, w_ref, d_ref, dy_ref, dw_ref):
        i = pl.program_id(0)
        yv = y_ref[...]
        gy = _gelu(yv)
        s = _sigmoid(_dot(gy, w_ref[...]))
        d = d_ref[...].astype(F32)
        t = d * gy * s * (1.0 - s)
        dgy = d * s + _dot(t, w_ref[...], NT)
        dy_ref[...] = dgy * _gelu_grad(yv)
        part = _dot(gy, t, TN)

        @pl.when(i == 0)
        def _():
            dw_ref[...] = part

        @pl.when(i > 0)
        def _():
            dw_ref[...] += part

    return pl.pallas_call(
        body, name=name, grid=(l // tm,),
        in_specs=[pl.BlockSpec((tm, wd), lambda i: (i, 0)), pl.BlockSpec((wd, wd), lambda i: (0, 0)),
                  pl.BlockSpec((tm, wd), lambda i: (i, dcol))],
        out_specs=[pl.BlockSpec((tm, wd), lambda i: (i, 0)), pl.BlockSpec((wd, wd), lambda i: (0, 0))],
        out_shape=[_sds((l, wd), F32), _sds((wd, wd), F32)],
        compiler_params=_params(("arbitrary",)),
    )(y, w, dout)


def _log_sigmoid(x):
    return jnp.minimum(x, 0.0) - jnp.log(1.0 + jnp.exp(-jnp.abs(x)))


def _gate_fwd(name, glo, wf, wb, bf, bb):
    l, r2 = glo.shape
    hk = wf.shape[1]
    tm = _tile(l, 1024, SUBLANE)

    def body(x_ref, wf_ref, wb_ref, bf_ref, bb_ref, gf_ref, gb_ref):
        xv = x_ref[...]
        gf_ref[...] = _log_sigmoid(_dot(xv, wf_ref[...]) + bf_ref[...]) * (1.0 / GLA_GATE_NORM)
        gb_ref[...] = _log_sigmoid(_dot(xv, wb_ref[...]) + bb_ref[...]) * (1.0 / GLA_GATE_NORM)

    w_spec = pl.BlockSpec((r2, hk), lambda i: (0, 0))
    b_spec = pl.BlockSpec((1, hk), lambda i: (0, 0))
    o_spec = pl.BlockSpec((tm, hk), lambda i: (i, 0))
    return pl.pallas_call(
        body, name=name, grid=(l // tm,),
        in_specs=[pl.BlockSpec((tm, r2), lambda i: (i, 0)), w_spec, w_spec, b_spec, b_spec],
        out_specs=[o_spec, o_spec], out_shape=[_sds((l, hk), F32), _sds((l, hk), F32)],
        compiler_params=_params(("arbitrary",)),
    )(glo, wf, wb, bf, bb)


def _gate_bwd(name, glo, wf, wb, bf, bb, dgf, dgb):
    l, r2 = glo.shape
    hk = wf.shape[1]
    tm = _tile(l, 1024, SUBLANE)

    def body(x_ref, wf_ref, wb_ref, bf_ref, bb_ref, dgf_ref, dgb_ref, dx_ref, dwf_ref, dwb_ref, dbf_ref, dbb_ref):
        i = pl.program_id(0)
        xv = x_ref[...]
        kf = _dot(xv, wf_ref[...]) + bf_ref[...]
        kb = _dot(xv, wb_ref[...]) + bb_ref[...]
        dkf = dgf_ref[...] * (1.0 / GLA_GATE_NORM) * _sigmoid(-kf)
        dkb = dgb_ref[...] * (1.0 / GLA_GATE_NORM) * _sigmoid(-kb)
        dx_ref[...] = _dot(dkf, wf_ref[...], NT) + _dot(dkb, wb_ref[...], NT)
        parts = (_dot(xv, dkf, TN), _dot(xv, dkb, TN), jnp.sum(dkf, axis=0, keepdims=True), jnp.sum(dkb, axis=0, keepdims=True))
        accs = (dwf_ref, dwb_ref, dbf_ref, dbb_ref)

        @pl.when(i == 0)
        def _():
            for a_, p_ in zip(accs, parts):
                a_[...] = p_

        @pl.when(i > 0)
        def _():
            for a_, p_ in zip(accs, parts):
                a_[...] += p_

    w_spec = pl.BlockSpec((r2, hk), lambda i: (0, 0))
    b_spec = pl.BlockSpec((1, hk), lambda i: (0, 0))
    g_spec = pl.BlockSpec((tm, hk), lambda i: (i, 0))
    x_spec = pl.BlockSpec((tm, r2), lambda i: (i, 0))
    return pl.pallas_call(
        body, name=name, grid=(l // tm,),
        in_specs=[x_spec, w_spec, w_spec, b_spec, b_spec, g_spec, g_spec],
        out_specs=[x_spec, w_spec, w_spec, b_spec, b_spec],
        out_shape=[_sds((l, r2), F32), _sds((r2, hk), F32), _sds((r2, hk), F32), _sds((1, hk), F32), _sds((1, hk), F32)],
        compiler_params=_params(("arbitrary",)),
    )(glo, wf, wb, bf, bb, dgf, dgb)


def _chunk_terms(qc, kc, lg, chunk, reverse):
    ri = lax.broadcasted_iota(jnp.int32, (chunk, chunk), 0)
    ci = lax.broadcasted_iota(jnp.int32, (chunk, chunk), 1)
    mask = (ci > ri) if reverse else (ci <= ri)
    pos = lax.broadcasted_iota(jnp.int32, (chunk, 1), 0).astype(F32)
    cum = ((chunk - pos) if reverse else (pos + 1.0)) * lg
    last = chunk * lg
    e = jnp.exp(cum)
    einv = jnp.exp(-cum)
    dec = jnp.exp(last - cum)
    return e, einv, dec, qc * e, kc * einv, kc * dec, jnp.exp(last), mask


def _lin_specs(width, col, tb, nb, reverse):
    return pl.BlockSpec((tb, width), lambda h, r: ((nb - 1 - r) if reverse else r, col + h))


def _lin_fwd(name, q, k, v, lgtab, prev_o=None, *, heads, hb, dk, dv, chunk, tb, qcol, kcol, vcol, reverse):
    l = q.shape[0]
    nb, ncb, ng = l // tb, tb // chunk, heads // hb

    def body(q_ref, k_ref, v_ref, lg_ref, *rest):
        p_ref = rest[0] if prev_o is not None else None
        o_ref, sp_ref, st = rest[-3:]

        @pl.when(pl.program_id(1) == 0)
        def _():
            st[...] = jnp.zeros_like(st)

        for c in range(ncb):
            cc = (ncb - 1 - c) if reverse else c
            rows = pl.ds(cc * chunk, chunk)
            for h in range(hb):
                ks, vs = slice(h * dk, (h + 1) * dk), slice(h * dv, (h + 1) * dv)
                qc, kc, vc = q_ref[rows, ks].astype(F32), k_ref[rows, ks].astype(F32), v_ref[rows, vs]
                _, _, _, qd, ki, kdec, e_last, mask = _chunk_terms(qc, kc, lg_ref[h, :, 0:1], chunk, reverse)
                a = jnp.where(mask, _dot(qd, ki, NT), 0.0)
                s_t = st[h]
                oc = _dot(a, vc) + _dot(qd, s_t, NT)
                if p_ref is not None:
                    oc = oc + p_ref[rows, vs]
                o_ref[rows, vs] = oc.astype(o_ref.dtype)
                sp_ref[cc, h] = s_t
                st[h] = s_t * e_last + _dot(vc, kdec, TN)

    o_spec = _lin_specs(hb * dv, 0, tb, nb, reverse)
    extra = [] if prev_o is None else [prev_o]
    return pl.pallas_call(
        body, name=name, grid=(ng, nb),
        in_specs=[_lin_specs(hb * dk, qcol, tb, nb, reverse), _lin_specs(hb * dk, kcol, tb, nb, reverse),
                  _lin_specs(hb * dv, vcol, tb, nb, reverse), pl.BlockSpec((hb, 1, LANE), lambda h, r: (h, 0, 0))] + [o_spec] * len(extra),
        out_specs=[o_spec, pl.BlockSpec((ncb, hb, dv, dk), lambda h, r: ((nb - 1 - r) if reverse else r, h, 0, 0))],
        out_shape=[_sds((l, heads * dv), F32 if prev_o is None else BF), _sds((l // chunk, heads, dv, dk), F32)],
        scratch_shapes=[pltpu.VMEM((hb, dv, dk), F32)],
        compiler_params=_params(("arbitrary", "arbitrary")),
    )(q, k, v, lgtab, *extra)


def _lin_bwd(name, q, k, v, lgtab, sprev, do, prev, *, heads, hb, dk, dv, chunk, tb, qcol, kcol, vcol, reverse):
    l = q.shape[0]
    nb, ncb, ng = l // tb, tb // chunk, heads // hb
    brev = not reverse
    n_prev = 0 if prev is None else len(prev)

    def body(q_ref, k_ref, v_ref, lg_ref, sp_ref, do_ref, *rest):
        p_refs = rest[:n_prev]
        dq_ref, dk_ref, dv_ref, dst = rest[n_prev:]

        @pl.when(pl.program_id(1) == 0)
        def _():
            dst[...] = jnp.zeros_like(dst)

        for c in range(ncb):
            cc = (ncb - 1 - c) if brev else c
            rows = pl.ds(cc * chunk, chunk)
            for h in range(hb):
                ks, vs = slice(h * dk, (h + 1) * dk), slice(h * dv, (h + 1) * dv)
                qc, kc, vc = q_ref[rows, ks].astype(F32), k_ref[rows, ks].astype(F32), v_ref[rows, vs]
                e, einv, dec, qd, ki, kdec, e_last, mask = _chunk_terms(qc, kc, lg_ref[h, :, 0:1], chunk, reverse)
                a = jnp.where(mask, _dot(qd, ki, NT), 0.0)
                s_t, ds_t, doc = sp_ref[cc, h], dst[h], do_ref[rows, vs]
                dvc = _dot(a, doc, TN) + _dot(kdec, ds_t, NT)
                da = jnp.where(mask, _dot(doc, vc, NT), 0.0)
                dqc = (_dot(da, ki) + _dot(doc, s_t)) * e
                dkc = _dot(da, qd, TN) * einv + _dot(vc, ds_t) * dec
                dst[h] = ds_t * e_last + _dot(doc, qd, TN)
                if n_prev:
                    dqc = dqc + p_refs[0][rows, ks]
                    dkc = dkc + p_refs[1][rows, ks]
                    dvc = dvc + p_refs[2][rows, vs]
                dq_ref[rows, ks] = dqc
                dk_ref[rows, ks] = dkc
                dv_ref[rows, vs] = dvc.astype(dv_ref.dtype)

    k_spec, v_spec = _lin_specs(hb * dk, 0, tb, nb, brev), _lin_specs(hb * dv, 0, tb, nb, brev)
    in_specs = [_lin_specs(hb * dk, qcol, tb, nb, brev), _lin_specs(hb * dk, kcol, tb, nb, brev), _lin_specs(hb * dv, vcol, tb, nb, brev),
                pl.BlockSpec((hb, 1, LANE), lambda h, r: (h, 0, 0)),
                pl.BlockSpec((ncb, hb, dv, dk), lambda h, r: ((nb - 1 - r) if brev else r, h, 0, 0)), v_spec]
    args = [q, k, v, lgtab, sprev, do]
    if n_prev:
        in_specs += [k_spec, k_spec, v_spec]
        args += list(prev)
    return pl.pallas_call(
        body, name=name, grid=(ng, nb), in_specs=in_specs, out_specs=[k_spec, k_spec, v_spec],
        out_shape=[_sds((l, heads * dk), F32), _sds((l, heads * dk), F32), _sds((l, heads * dv), BF if n_prev else F32)],
        scratch_shapes=[pltpu.VMEM((hb, dv, dk), F32)],
        compiler_params=_params(("arbitrary", "arbitrary")),
    )(*args)


def _log2(n):
    assert n & (n - 1) == 0, "a power of two"
    return n.bit_length() - 1


def _gla_block_terms(q, k, g, qscale, chunk, tb, reverse):
    ri = lax.broadcasted_iota(jnp.int32, (tb, tb), 0)
    ci = lax.broadcasted_iota(jnp.int32, (tb, tb), 1)
    same = jnp.right_shift(ri, _log2(chunk)) == jnp.right_shift(ci, _log2(chunk))
    t_in = jnp.logical_and(same, (ci >= ri) if reverse else (ci <= ri)).astype(F32)
    cum = _dot3(t_in, g)
    tot = _dot3(same.astype(F32), g)
    e = jnp.exp(cum)
    einv = jnp.exp(-cum)
    dec = jnp.exp(tot - cum)
    return e, einv, dec, jnp.exp(tot), q * (qscale * e), k * einv, k * dec, t_in


def _gla_masks(hk, dk, heads, chunk, reverse):
    lane = lax.broadcasted_iota(jnp.int32, (1, hk), 1)
    head_of = jnp.right_shift(lane, _log2(dk))
    ri = lax.broadcasted_iota(jnp.int32, (chunk, chunk), 0)
    ci = lax.broadcasted_iota(jnp.int32, (chunk, chunk), 1)
    return [head_of == h for h in range(heads)], ((ci > ri) if reverse else (ci <= ri))


def _gla_fwd(name, proj, g, prev_o=None, *, heads, dk, dv, chunk, tb, qcol, kcol, vcol, qscale, reverse):
    l = proj.shape[0]
    nb, ncb, hk, hv = l // tb, tb // chunk, heads * dk, heads * dv

    def body(q_ref, k_ref, v_ref, g_ref, *rest):
        p_ref = rest[0] if prev_o is not None else None
        o_ref, sp_ref, st = rest[-3:]

        @pl.when(pl.program_id(0) == 0)
        def _():
            st[...] = jnp.zeros_like(st)

        _, _, _, etot, qd, ki, kdec, _ = _gla_block_terms(q_ref[...].astype(F32), k_ref[...].astype(F32), g_ref[...], qscale, chunk, tb, reverse)
        heads_m, causal = _gla_masks(hk, dk, heads, chunk, reverse)
        s_all = st[...]
        for c in range(ncb):
            cc = (ncb - 1 - c) if reverse else c
            rc = slice(cc * chunk, (cc + 1) * chunk)
            qd_c, ki_c, kdec_c = qd[rc], ki[rc], kdec[rc]
            sp_ref[cc] = s_all
            kv = jnp.zeros_like(s_all)
            for h in range(heads):
                vs = slice(h * dv, (h + 1) * dv)
                qm = jnp.where(heads_m[h], qd_c, 0.0)
                a = jnp.where(causal, _dot(qm, ki_c, NT), 0.0)
                vc = v_ref[rc, vs]
                oc = _dot(a, vc) + _dot(qm, s_all, NT)
                if p_ref is not None:
                    oc = oc + p_ref[rc, vs]
                o_ref[rc, vs] = oc.astype(o_ref.dtype)
                kv = kv + jnp.where(heads_m[h], _dot(vc, kdec_c, TN), 0.0)
            s_all = s_all * etot[rc][0:1, :] + kv
        st[...] = s_all

    def rows(r):
        return (nb - 1 - r) if reverse else r

    o_spec = pl.BlockSpec((tb, hv), lambda r: (rows(r), 0))
    extra = [] if prev_o is None else [prev_o]
    return pl.pallas_call(
        body, name=name, grid=(nb,),
        in_specs=[pl.BlockSpec((tb, hk), lambda r: (rows(r), qcol)), pl.BlockSpec((tb, hk), lambda r: (rows(r), kcol)),
                  pl.BlockSpec((tb, hv), lambda r: (rows(r), vcol)), pl.BlockSpec((tb, hk), lambda r: (rows(r), 0))] + [o_spec] * len(extra),
        out_specs=[o_spec, pl.BlockSpec((ncb, dv, hk), lambda r: (rows(r), 0, 0))],
        out_shape=[_sds((l, hv), F32 if prev_o is None else BF), _sds((l // chunk, dv, hk), F32)],
        scratch_shapes=[pltpu.VMEM((dv, hk), F32)],
        compiler_params=_params(("arbitrary",)),
    )(proj, proj, proj, g, *extra)


def _gla_bwd(name, proj, g, sprev, do, prev, *, heads, dk, dv, chunk, tb, qcol, kcol, vcol, qscale, reverse):
    l = proj.shape[0]
    nb, ncb, hk, hv = l // tb, tb // chunk, heads * dk, heads * dv
    brev = not reverse
    n_prev = 0 if prev is None else len(prev)

    def body(q_ref, k_ref, v_ref, g_ref, sp_ref, do_ref, *rest):
        p_refs = rest[:n_prev]
        dq_ref, dk_ref, dv_ref, dg_ref, dst, dcs = rest[n_prev:]

        @pl.when(pl.program_id(0) == 0)
        def _():
            dst[...] = jnp.zeros_like(dst)

        e, einv, dec, etot, qd, ki, kdec, t_in = _gla_block_terms(q_ref[...].astype(F32), k_ref[...].astype(F32), g_ref[...], qscale, chunk, tb, reverse)
        heads_m, causal = _gla_masks(hk, dk, heads, chunk, reverse)
        last_row = lax.broadcasted_iota(jnp.int32, (chunk, 1), 0) == (0 if reverse else chunk - 1)
        ds_all = dst[...]
        for c in range(ncb):
            cc = (ncb - 1 - c) if brev else c
            rc = slice(cc * chunk, (cc + 1) * chunk)
            qd_c, ki_c, kdec_c = qd[rc], ki[rc], kdec[rc]
            s_all = sp_ref[cc]
            et = etot[rc][0:1, :]
            dqd = jnp.zeros((chunk, hk), F32)
            dki = jnp.zeros((chunk, hk), F32)
            dkdec = jnp.zeros((chunk, hk), F32)
            ds_add = jnp.zeros_like(ds_all)
            for h in range(heads):
                vs = slice(h * dv, (h + 1) * dv)
                m = heads_m[h]
                qm = jnp.where(m, qd_c, 0.0)
                a = jnp.where(causal, _dot(qm, ki_c, NT), 0.0)
                doc, vc = do_ref[rc, vs], v_ref[rc, vs]
                dvc = _dot(a, doc, TN) + _dot(jnp.where(m, kdec_c, 0.0), ds_all, NT)
                if n_prev:
                    dvc = dvc + p_refs[2][rc, vs]
                dv_ref[rc, vs] = dvc.astype(dv_ref.dtype)
                da = jnp.where(causal, _dot(doc, vc, NT), 0.0)
                dqd = dqd + jnp.where(m, _dot(da, ki_c) + _dot(doc, s_all), 0.0)
                dki = dki + _dot(da, qm, TN)
                dkdec = dkdec + jnp.where(m, _dot(vc, ds_all), 0.0)
                ds_add = ds_add + _dot(doc, qm, TN)
            dqc = dqd * e[rc] * qscale
            dkc = dki * einv[rc] + dkdec * dec[rc]
            if n_prev:
                dqc = dqc + p_refs[0][rc, :]
                dkc = dkc + p_refs[1][rc, :]
            dq_ref[rc, :] = dqc
            dk_ref[rc, :] = dkc
            dlast = jnp.sum(dkdec * kdec_c, axis=0, keepdims=True) + et * jnp.sum(s_all * ds_all, axis=0, keepdims=True)
            dcs[rc, :] = dqd * qd_c - dki * ki_c - dkdec * kdec_c + jnp.where(last_row, dlast, 0.0)
            ds_all = ds_all * et + ds_add
        dst[...] = ds_all
        dg_ref[...] = _dot3(t_in, dcs[...], TN)

    def rows(r):
        return (nb - 1 - r) if brev else r

    k_spec = pl.BlockSpec((tb, hk), lambda r: (rows(r), 0))
    v_spec = pl.BlockSpec((tb, hv), lambda r: (rows(r), 0))
    in_specs = [pl.BlockSpec((tb, hk), lambda r: (rows(r), qcol)), pl.BlockSpec((tb, hk), lambda r: (rows(r), kcol)),
                pl.BlockSpec((tb, hv), lambda r: (rows(r), vcol)), k_spec,
                pl.BlockSpec((ncb, dv, hk), lambda r: (rows(r), 0, 0)), v_spec]
    args = [proj, proj, proj, g, sprev, do]
    if n_prev:
        in_specs += [k_spec, k_spec, v_spec]
        args += list(prev)
    return pl.pallas_call(
        body, name=name, grid=(nb,), in_specs=in_specs, out_specs=[k_spec, k_spec, v_spec, k_spec],
        out_shape=[_sds((l, hk), F32), _sds((l, hk), F32), _sds((l, hv), BF if n_prev else F32), _sds((l, hk), F32)],
        scratch_shapes=[pltpu.VMEM((dv, hk), F32), pltpu.VMEM((tb, hk), F32)],
        compiler_params=_params(("arbitrary",)),
    )(*args)


def _headgate_fwd(name, o_sum, og_arr, og_col, gn, dv):
    l, w = o_sum.shape
    tm = _tile(l, 512, SUBLANE)
    nh = w // dv

    def body(o_ref, og_ref, gn_ref, out_ref):
        for h in range(nh):
            cs = slice(h * dv, (h + 1) * dv)
            o = o_ref[:, cs].astype(F32)
            r = lax.rsqrt(jnp.mean(o * o, axis=-1, keepdims=True) + EPS)
            og = og_ref[:, cs].astype(F32)
            out_ref[:, cs] = (o * r * gn_ref[:, cs] * (og * _sigmoid(og))).astype(out_ref.dtype)

    row = pl.BlockSpec((tm, w), lambda i: (i, 0))
    return pl.pallas_call(
        body, name=name, grid=(l // tm,),
        in_specs=[row, pl.BlockSpec((tm, w), lambda i: (i, og_col)), pl.BlockSpec((1, w), lambda i: (0, 0))],
        out_specs=row, out_shape=_sds((l, w), BF),
        compiler_params=_params(("arbitrary",)),
    )(o_sum, og_arr, gn)


def _headgate_bwd(name, o_sum, og_arr, og_col, gn, dout, dcol, dv):
    l, w = o_sum.shape
    tm = _tile(l, 512, SUBLANE)
    nh = w // dv

    def body(o_ref, og_ref, gn_ref, d_ref, do_ref, dog_ref, dgn_ref):
        i = pl.program_id(0)
        for h in range(nh):
            cs = slice(h * dv, (h + 1) * dv)
            o = o_ref[:, cs].astype(F32)
            r = lax.rsqrt(jnp.mean(o * o, axis=-1, keepdims=True) + EPS)
            oh = o * r
            og = og_ref[:, cs].astype(F32)
            s = _sigmoid(og)
            d = d_ref[:, cs].astype(F32)
            gnv = gn_ref[:, cs]
            d_on = d * (og * s)
            dog_ref[:, cs] = (d * (oh * gnv) * s * (1.0 + og * (1.0 - s))).astype(dog_ref.dtype)
            doh = d_on * gnv
            do_ref[:, cs] = (r * (doh - oh * jnp.mean(doh * oh, axis=-1, keepdims=True))).astype(do_ref.dtype)
            part = jnp.sum(d_on * oh, axis=0, keepdims=True)

            @pl.when(i == 0)
            def _():
                dgn_ref[:, cs] = part

            @pl.when(i > 0)
            def _():
                dgn_ref[:, cs] += part

    row = pl.BlockSpec((tm, w), lambda i: (i, 0))
    vec = pl.BlockSpec((1, w), lambda i: (0, 0))
    return pl.pallas_call(
        body, name=name, grid=(l // tm,),
        in_specs=[row, pl.BlockSpec((tm, w), lambda i: (i, og_col)), vec, pl.BlockSpec((tm, w), lambda i: (i, dcol))],
        out_specs=[row, row, vec], out_shape=[_sds((l, w), BF), _sds((l, w), BF), _sds((1, w), F32)],
        compiler_params=_params(("arbitrary",)),
    )(o_sum, og_arr, gn, dout)


def _rot_tables(l, dk):
    half = dk // 2
    pos = jnp.arange(l, dtype=F32)
    inv = jnp.exp(-math.log(ROPE_BASE) * jnp.arange(half, dtype=F32) / half)
    ang = pos[:, None] * inv[None, :]
    cos, sin = jnp.cos(ang), jnp.sin(ang)
    return jnp.concatenate([cos, cos], axis=-1), jnp.concatenate([-sin, sin], axis=-1)


def _rot_apply(name, src_q, qcol, src_k, kcol, cos_t, sin_t, heads, dk, kscale, out_dtype, transpose):
    l = src_q.shape[0]
    w = heads * dk
    tm = _tile(l, 512, SUBLANE)

    def rot(t, cos_v, sin_v):
        if transpose:
            return t * cos_v + pltpu.roll(t * sin_v, dk // 2, 1)
        return t * cos_v + pltpu.roll(t, dk // 2, 1) * sin_v

    def body(q_ref, k_ref, c_ref, s_ref, qo_ref, ko_ref):
        cos_v, sin_v = c_ref[...], s_ref[...]
        for h in range(heads):
            cs = slice(h * dk, (h + 1) * dk)
            qo_ref[:, cs] = rot(q_ref[:, cs].astype(F32), cos_v, sin_v).astype(out_dtype)
            ko_ref[:, cs] = (rot(k_ref[:, cs].astype(F32), cos_v, sin_v) * kscale).astype(out_dtype)

    tab = pl.BlockSpec((tm, dk), lambda i: (i, 0))
    row = pl.BlockSpec((tm, w), lambda i: (i, 0))
    return pl.pallas_call(
        body, name=name, grid=(l // tm,),
        in_specs=[pl.BlockSpec((tm, w), lambda i: (i, qcol)), pl.BlockSpec((tm, w), lambda i: (i, kcol)), tab, tab],
        out_specs=[row, row], out_shape=[_sds((l, w), out_dtype), _sds((l, w), out_dtype)],
        compiler_params=_params(("arbitrary",)),
    )(src_q, src_k, cos_t, sin_t)


def _peer_copies(src_ref, out_ref, send_sems, recv_sems, gather):
    x, y, c = lax.axis_index("x"), lax.axis_index("y"), lax.axis_index("c")
    me = 4 * x + 2 * y + c
    copies = []
    for kk in range(1, N_DEV):
        px = (1 - x) if kk & 4 else x
        py = (1 - y) if kk & 2 else y
        pc = (1 - c) if kk & 1 else c
        peer = 4 * px + 2 * py + pc
        copies.append(pltpu.make_async_remote_copy(
            src_ref=src_ref if gather else src_ref.at[peer], dst_ref=out_ref.at[me],
            send_sem=send_sems.at[kk - 1], recv_sem=recv_sems.at[kk - 1],
            device_id=(px, py, pc), device_id_type=pl.DeviceIdType.MESH))
    return copies


_HBM = pl.BlockSpec(memory_space=pltpu.HBM)
_SEM = pl.BlockSpec(memory_space=pltpu.SEMAPHORE)
_EFFECT = pltpu.SideEffectType.DATAFLOW_SIDE_EFFECTING


def _exchange_start(name, srcs, gather):
    n = len(srcs)
    lands = [lax.empty((N_DEV,) + tuple(s.shape if gather else s.shape[1:]), s.dtype) for s in srcs]

    def body(*refs):
        src_refs, land_refs = refs[:n], refs[n:2 * n]
        send, recv = refs[2 * n:3 * n], refs[3 * n:4 * n]
        token = refs[-1]
        for k in range(n):
            for cp in _peer_copies(src_refs[k], land_refs[k], send[k], recv[k], gather):
                cp.start()
        token[...] = jnp.zeros_like(token)

    sem = pltpu.SemaphoreType.DMA((N_DEV - 1,))
    outs = pl.pallas_call(
        body, name=name,
        out_shape=tuple([sem] * (2 * n) + [pltpu.HBM(s.shape, s.dtype) for s in srcs] + [pltpu.HBM(a.shape, a.dtype) for a in lands]
                        + [_sds((SUBLANE, LANE), F32)]),
        in_specs=tuple([_HBM] * (2 * n)), out_specs=tuple([_SEM] * (2 * n) + [_HBM] * (2 * n) + [pl.BlockSpec(memory_space=pltpu.VMEM)]),
        input_output_aliases={k: 2 * n + k for k in range(2 * n)},
        compiler_params=pltpu.CompilerParams(has_side_effects=_EFFECT),
    )(*[pltpu.with_memory_space_constraint(a, pltpu.HBM) for a in list(srcs) + lands])
    return [(outs[k], outs[n + k], outs[2 * n + k], outs[3 * n + k], outs[-1]) for k in range(n)]


def _exchange_wait(name, started, gather, after):
    send_sems, recv_sems, src_thru, land_thru, _ = started

    def body(src_ref, land_ref, send_sems, recv_sems, after_ref, src_out, land_out):
        copies = _peer_copies(src_ref, land_ref, send_sems, recv_sems, gather)
        for cp in copies:
            cp.wait_send()
        for cp in copies:
            cp.wait_recv()

    return pl.pallas_call(
        body, name=name,
        out_shape=(pltpu.HBM(src_thru.shape, src_thru.dtype), pltpu.HBM(land_thru.shape, land_thru.dtype)),
        in_specs=(_HBM, _HBM, _SEM, _SEM, pl.BlockSpec(memory_space=pl.ANY)), out_specs=(_HBM, _HBM),
        input_output_aliases={0: 0, 1: 1},
        compiler_params=pltpu.CompilerParams(has_side_effects=_EFFECT),
    )(src_thru, land_thru, send_sems, recv_sems, after)


def _adam_math(w, gsum, m, v):
    m2 = ADAM_B1 * m + (1.0 - ADAM_B1) * gsum
    v2 = ADAM_B2 * v + (1.0 - ADAM_B2) * (gsum * gsum)
    m_hat = m2 / (1.0 - ADAM_B1 ** ADAM_STEP)
    v_hat = v2 / (1.0 - ADAM_B2 ** ADAM_STEP)
    delta = -ADAM_LR * (m_hat / (jnp.sqrt(v_hat) + ADAM_EPS) + ADAM_WD * w)
    return delta, m2, v2


def _reduce_adam(name, parts, w, m, v):
    nl, r, c = w.shape
    tr = _tile(r, 256, 16)
    nr = r // tr

    def body(*refs):
        p_refs = refs[:nl]
        w_ref, m_ref, v_ref, g_ref, d_ref, m2_ref, v2_ref = refs[nl:]
        for li in range(nl):
            @pl.when(pl.program_id(0) == li)
            def _(p_ref=p_refs[li]):
                gsum = p_ref[0].astype(F32)
                for s in range(1, N_DEV):
                    gsum = gsum + p_ref[s].astype(F32)
                g_ref[...] = gsum
                delta, m2, v2 = _adam_math(w_ref[...], gsum, m_ref[...], v_ref[...])
                d_ref[...] = delta
                m2_ref[...] = m2
                v2_ref[...] = v2

    def part_spec(li):
        return pl.BlockSpec((N_DEV, tr, c), lambda lay, i: (0, jnp.where(lay == li, i, jnp.where(lay < li, 0, nr - 1)), 0))

    row = pl.BlockSpec((None, tr, c), lambda lay, i: (lay, i, 0))
    return pl.pallas_call(
        body, name=name, grid=(nl, nr),
        in_specs=[part_spec(li) for li in range(nl)] + [row, row, row],
        out_specs=[row, row, row, row], out_shape=[_sds((nl, r, c), F32)] * 4,
        compiler_params=_params(("arbitrary", "arbitrary")),
    )(*parts, w, m, v)


def _reduce8(name, parts):
    _, r, c = parts.shape

    def body(p_ref, g_ref):
        gsum = p_ref[0]
        for s in range(1, N_DEV):
            gsum = gsum + p_ref[s]
        g_ref[...] = gsum

    return pl.pallas_call(
        body, name=name, grid=(1,),
        in_specs=[pl.BlockSpec((N_DEV, r, c), lambda i: (0, 0, 0))],
        out_specs=pl.BlockSpec((r, c), lambda i: (0, 0)), out_shape=_sds((r, c), F32),
        compiler_params=_params(("arbitrary",)),
    )(parts)


def _adam_packed(name, w, g, m, v):
    r, c = w.shape

    def body(w_ref, g_ref, m_ref, v_ref, d_ref, m2_ref, v2_ref):
        delta, m2, v2 = _adam_math(w_ref[...], g_ref[...], m_ref[...], v_ref[...])
        d_ref[...] = delta
        m2_ref[...] = m2
        v2_ref[...] = v2

    spec = pl.BlockSpec((r, c), lambda i: (0, 0))
    return pl.pallas_call(
        body, name=name, grid=(1,), in_specs=[spec] * 4, out_specs=[spec] * 3, out_shape=[_sds((r, c), F32)] * 3,
        compiler_params=_params(("arbitrary",)),
    )(w, g, m, v)


def _pack(arrs):
    flat = jnp.concatenate([a.reshape(-1).astype(F32) for a in arrs])
    n = flat.shape[0]
    pad = (-n) % (SUBLANE * LANE)
    return jnp.pad(flat, (0, pad)).reshape(-1, LANE)


def _unpack(packed, like):
    flat = packed.reshape(-1)
    out, off = [], 0
    for a in like:
        n = math.prod(a.shape)
        out.append(flat[off:off + n].reshape(a.shape))
        off += n
    return out


def _row_blocks(full):
    return full.reshape(N_DEV, full.shape[0] // N_DEV, full.shape[1])


def _col_blocks(full):
    r, c = full.shape
    return full.reshape(r, N_DEV, c // N_DEV).transpose(1, 0, 2)


def kernel(x, ffn1_norm, ffn1_w1, ffn1_w2, mix_norm, ffn2_norm, ffn2_w1, ffn2_w2, ab_w_in, s5_lambda_re, s5_lambda_im, s5_b_re, s5_b_im, s5_c_re, s5_c_im, s5_log_dt, s5_d, s5_w_glu, gla_w_gk, gla_b_gk, gla_norm, ab_w_out, ret_w_in, ret_norm, ret_w_out, final_norm, loss_target, m_ffn1_norm, m_ffn1_w1, m_ffn1_w2, m_mix_norm, m_ffn2_norm, m_ffn2_w1, m_ffn2_w2, m_ab_w_in, m_s5_lambda_re, m_s5_lambda_im, m_s5_b_re, m_s5_b_im, m_s5_c_re, m_s5_c_im, m_s5_log_dt, m_s5_d, m_s5_w_glu, m_gla_w_gk, m_gla_b_gk, m_gla_norm, m_ab_w_out, m_ret_w_in, m_ret_norm, m_ret_w_out, m_final_norm, v_ffn1_norm, v_ffn1_w1, v_ffn1_w2, v_mix_norm, v_ffn2_norm, v_ffn2_w1, v_ffn2_w2, v_ab_w_in, v_s5_lambda_re, v_s5_lambda_im, v_s5_b_re, v_s5_b_im, v_s5_c_re, v_s5_c_im, v_s5_log_dt, v_s5_d, v_s5_w_glu, v_gla_w_gk, v_gla_b_gk, v_gla_norm, v_ab_w_out, v_ret_w_in, v_ret_norm, v_ret_w_out, v_final_norm):
    names = ['ffn1_norm', 'ffn1_w1', 'ffn1_w2', 'mix_norm', 'ffn2_norm', 'ffn2_w1', 'ffn2_w2', 'ab_w_in', 's5_lambda_re', 's5_lambda_im', 's5_b_re', 's5_b_im', 's5_c_re', 's5_c_im', 's5_log_dt', 's5_d', 's5_w_glu', 'gla_w_gk', 'gla_b_gk', 'gla_norm', 'ab_w_out', 'ret_w_in', 'ret_norm', 'ret_w_out', 'final_norm']
    loc = locals()
    W = {n: loc[n] for n in names}
    M = {n: loc["m_" + n] for n in names}
    V = {n: loc["v_" + n] for n in names}

    me = 4 * lax.axis_index("x") + 2 * lax.axis_index("y") + lax.axis_index("c")
    xs = x[0]
    tgt = loss_target[0]
    l, d = xs.shape
    depth = ffn1_norm.shape[0]

    pending, to_start = {}, []

    def start_gather(tag, shard):
        to_start.append((tag, shard))

    def finish_gather(tag, after):
        started, shard = pending.pop(tag)
        _, got = _exchange_wait("agw_" + tag, started, True, after)
        return lax.dynamic_update_index_in_dim(got, shard, me, 0)

    def finish_cols(tag, after):
        g = finish_gather(tag, after)
        return g.transpose(1, 0, 2).reshape(g.shape[1], -1)

    def finish_rows(tag, after):
        g = finish_gather(tag, after)
        return g.reshape(-1, g.shape[2])

    small_sharded = [gla_w_gk, gla_b_gk, ret_norm]
    for i in range(depth):
        j = i // 2
        start_gather(f"ffn1_w1_{i}", ffn1_w1[i].astype(BF))
        start_gather(f"ffn1_w2_{i}", ffn1_w2[i].astype(BF))
        if i % 2 == 0:
            start_gather(f"ab_w_in_{j}", ab_w_in[j].astype(BF))
            if i == 0:
                start_gather("small", _pack(small_sharded))
            start_gather(f"s5_w_glu_{j}", s5_w_glu[j].astype(BF))
            start_gather(f"ab_w_out_{j}", ab_w_out[j].astype(BF))
        else:
            start_gather(f"ret_w_in_{j}", ret_w_in[j].astype(BF))
            start_gather(f"ret_w_out_{j}", ret_w_out[j].astype(BF))
        start_gather(f"ffn2_w1_{i}", ffn2_w1[i].astype(BF))
        start_gather(f"ffn2_w2_{i}", ffn2_w2[i].astype(BF))
    for (tag, shard), started in zip(to_start, _exchange_start("ags_weights", [s_ for _, s_ in to_start], True)):
        pending[tag] = (started, shard)
    started_all = started[4][0, 0]
    full = {}

    s5w = s5_d.shape[1]
    g_s5, n_s5 = s5_lambda_re.shape[2], s5_lambda_re.shape[3]
    hs = min(SUBLANE * LANE, g_s5 * n_s5)
    gla_hk = gla_w_gk.shape[-1] * N_DEV
    gla_dk = gla_hk // GLA_HEADS
    gla_hv = gla_norm.shape[1]
    gla_dv = gla_hv // GLA_HEADS
    ret_hv = ret_norm.shape[1] * N_DEV
    ret_dv = ret_hv // RET_HEADS
    ret_hk = (ret_w_in.shape[2] * N_DEV - 2 * ret_hv) // 2
    ret_dk = ret_hk // RET_HEADS
    assert s5w == gla_hv and 2 * gla_hk == s5w, "column blocks of the mixer projection assume these widths"
    assert ret_hv == 2 * ret_hk
    main_w = s5w + 2 * gla_hk + 2 * gla_hv
    gla_tb = _tile(l, 256, GLA_CHUNK)
    ret_chunk = min(RET_CHUNK, l)

    cos_t, sin_t = _rot_tables(l, ret_dk)
    lg_f = jnp.log1p(-jnp.exp2(-5.0 - jnp.arange(RET_HEADS, dtype=F32)))
    lgtab_f = jnp.broadcast_to(lg_f[:, None, None], (RET_HEADS, 1, LANE))
    lgtab_b = jnp.broadcast_to(lg_f[::-1][:, None, None], (RET_HEADS, 1, LANE))
    s5_pre = {}
    for j in range((depth + 1) // 2):
        s5_args = (s5_lambda_re[j], s5_lambda_im[j], s5_b_re[j], s5_b_im[j], s5_c_re[j], s5_c_im[j], s5_log_dt[j])
        (a_tab, bd, cd), s5_vjp = jax.vjp(lambda *a: _s5_chunk_tables(*a, hs), *s5_args)
        s5_pre[j] = (a_tab, bd.astype(BF), cd.astype(BF), s5_vjp)
    tables_done = jnp.stack([cos_t[0, 0], sin_t[0, 0], lgtab_f[0, 0, 0], lgtab_b[0, 0, 0]]
                            + [t[0].reshape(-1)[0] + t[1].reshape(-1)[0].astype(F32) + t[2].reshape(-1)[0].astype(F32) for t in s5_pre.values()])

    saved = []
    cur = xs
    hn = _rms_fwd("l0_ffn1_norm", cur, ffn1_norm[0:1] + started_all)
    for i in range(depth):
        j = i // 2
        s = {}
        s['x0'] = cur

        def first_w1(after):
            if i == 0:
                after = jnp.concatenate([after[0, 0:1].astype(F32), tables_done])
            return finish_cols(f"ffn1_w1_{i}", after)

        cur, h, s['ffn1'], s['f1w1'], s['f1w2'] = _ffn_fwd(f"l{i}_ffn1", cur, hn, first_w1, lambda after: finish_rows(f"ffn1_w2_{i}", after),
                                                            mix_norm[i:i + 1])
        s['x1'] = cur
        s['h'] = h
        if i % 2 == 0:
            w_in = finish_cols(f"ab_w_in_{j}", cur)
            if i == 0:
                got = finish_gather("small", cur)
                flat, off, joined = got.reshape(N_DEV, -1), 0, []
                for a in small_sharded:
                    n = math.prod(a.shape)
                    blk = jnp.moveaxis(flat[:, off:off + n].reshape((N_DEV,) + a.shape), 0, -2)
                    joined.append(blk.reshape(a.shape[:-1] + (N_DEV * a.shape[-1],)))
                    off += n
                full['gla_w_gk'], full['gla_b_gk'], ret_norm_full = joined[0].astype(BF), joined[1], joined[2]
            s['w_glu'], s['w_out'] = finish_rows(f"s5_w_glu_{j}", cur), finish_rows(f"ab_w_out_{j}", cur)
            w_main, w_glo = w_in[:, :main_w], w_in[:, main_w:]
            proj = _mm_plain(f"l{i}_proj", h, w_main, NN, BF, tm=512, tn=1024, tk=d, b_outer=True)
            glo = _mm_plain(f"l{i}_glo", h, w_glo, NN, F32, tm=1024, tn=2 * GLA_RANK, tk=d)
            a_tab, bd16, cd16, s5_vjp = s5_pre[j]
            tm = _tile(l, 512, SUBLANE)
            x_f, y_f = _s5_fwd(f"l{i}_s5_fwd_f", proj, bd16[0], cd16[0], a_tab[0], False)
            x_b, y_b = _s5_fwd(f"l{i}_s5_fwd_b", proj, bd16[1], cd16[1], a_tab[1], True)
            d_row = s5_d[j:j + 1]
            y, s5_out = _glu_fwd(f"l{i}_s5_glu", y_f, y_b, proj, d_row, s['w_glu'])
            zeros_r = jnp.zeros((GLA_RANK, gla_hk), BF)
            w_gk = full['gla_w_gk'][j]
            wgk_f = jnp.concatenate([w_gk[0], zeros_r], axis=0)
            wgk_b = jnp.concatenate([zeros_r, w_gk[1]], axis=0)
            b_gk = full['gla_b_gk'][j]
            g_f, g_b = _gate_fwd(f"l{i}_gla_gate", glo, wgk_f, wgk_b, b_gk[0:1], b_gk[1:2])
            qcol, kcol, vcol, ogcol = s5w // gla_hk, s5w // gla_hk + 1, (s5w + 2 * gla_hk) // gla_hv, (s5w + 2 * gla_hk) // gla_hv + 1
            lin_kw = dict(heads=GLA_HEADS, dk=gla_dk, dv=gla_dv, chunk=GLA_CHUNK, tb=gla_tb, qcol=qcol, kcol=kcol, vcol=vcol,
                          qscale=gla_dk ** -0.5)
            o_f, sp_f = _gla_fwd(f"l{i}_gla_fwd_f", proj, g_f, reverse=False, **lin_kw)
            o_b, sp_b = _gla_fwd(f"l{i}_gla_fwd_b", proj, g_b, o_f, reverse=True, **lin_kw)
            gla_out = _headgate_fwd(f"l{i}_gla_out", o_b, proj, ogcol, gla_norm[j:j + 1], gla_dv)
            w_out = s['w_out']

            row = pl.BlockSpec((tm, d), lambda ii, jj, kk: (ii, 0))
            vec = pl.BlockSpec((1, d), lambda ii, jj, kk: (0, 0))
            cur, hn = _mm(f"l{i}_mix_out",
                          [(s5_out, pl.BlockSpec((tm, s5w), lambda ii, jj, kk: (ii, 0)), w_out, pl.BlockSpec((s5w, d), lambda ii, jj, kk: (0, 0))),
                           (gla_out, pl.BlockSpec((tm, gla_hv), lambda ii, jj, kk: (ii, 0)), w_out, pl.BlockSpec((gla_hv, d), lambda ii, jj, kk: (1, 0)))],
                          NN, (l // tm, 1, 1), [_sds((l, d), F32), _sds((l, d), BF)], [row, row], (tm, d), _residual_epi(1.0, True),
                          [cur, ffn2_norm[i:i + 1]], [row, vec])
            s.update(proj=proj, glo=glo, s5_vjp=s5_vjp, a_tab=a_tab, bd16=bd16, cd16=cd16, x_f=x_f, x_b=x_b, y=y, s5_out=s5_out,
                     wgk_f=wgk_f, wgk_b=wgk_b, b_gk=b_gk, g_f=g_f, g_b=g_b, o_f=o_f, o_b=o_b, sp_f=sp_f, sp_b=sp_b, gla_out=gla_out,
                     w_main=w_main, w_glo=w_glo, lin_kw=lin_kw, ogcol=ogcol)
        else:
            w_in = finish_cols(f"ret_w_in_{j}", cur)
            s['w_in'], s['w_out'] = w_in, finish_rows(f"ret_w_out_{j}", cur)
            proj = _mm_plain(f"l{i}_proj", h, w_in, NN, BF, tm=512, tn=1024, tk=d, b_outer=True)
            qr, kr = _rot_apply(f"l{i}_rot", proj, 0, proj, 1, cos_t, sin_t, RET_HEADS, ret_dk, ret_dk ** -0.5, BF, False)
            ret_hb = 4
            lin_kw = dict(heads=RET_HEADS, hb=ret_hb, dk=ret_dk, dv=ret_dv, chunk=ret_chunk, tb=ret_chunk, qcol=0, kcol=0,
                          vcol=(2 * ret_hk) // (ret_hb * ret_dv))
            o_f, sp_f = _lin_fwd(f"l{i}_ret_fwd_f", qr, kr, proj, lgtab_f, reverse=False, **lin_kw)
            o_b, sp_b = _lin_fwd(f"l{i}_ret_fwd_b", qr, kr, proj, lgtab_b, o_f, reverse=True, **lin_kw)
            ogcol = (2 * ret_hk + ret_hv) // ret_hv
            r_out = _headgate_fwd(f"l{i}_ret_out", o_b, proj, ogcol, ret_norm_full, ret_dv)

            tm = _tile(l, 512, SUBLANE)
            cur, hn = _mm_plain(f"l{i}_mix_out", r_out, s['w_out'], NN, F32, tm=512, tn=d, tk=ret_hv, epi=_residual_epi(1.0, True),
                                eins=[cur, ffn2_norm[i:i + 1]], especs=[None, pl.BlockSpec((1, d), lambda ii, jj, kk: (0, 0))],
                                extra_outs=[_sds((l, d), BF)], extra_specs=[pl.BlockSpec((tm, d), lambda ii, jj, kk: (ii, 0))])
            s.update(proj=proj, qr=qr, kr=kr, o_f=o_f, o_b=o_b, sp_f=sp_f, sp_b=sp_b, r_out=r_out, lin_kw=lin_kw, ogcol=ogcol)
        s['x2'] = cur
        cur, hn, s['ffn2'], s['f2w1'], s['f2w2'] = _ffn_fwd(f"l{i}_ffn2", cur, hn, lambda after: finish_cols(f"ffn2_w1_{i}", after),
                                                             lambda after: finish_rows(f"ffn2_w2_{i}", after),
                                                             ffn1_norm[i + 1:i + 2] if i + 1 < depth else None)
        saved.append(s)

    dx, d_final_norm, loss_row = _loss_head("loss_head", cur, final_norm.reshape(1, -1), tgt)
    loss = lax.psum(loss_row[0, 0], ("x", "y", "c"))

    G = {}
    big = {}
    G['final_norm'] = d_final_norm.reshape(-1)
    per_layer = {n: [None] * depth for n in ['ffn1_norm', 'mix_norm', 'ffn2_norm']}
    small_late = ['ffn1_norm', 'mix_norm']
    small_early = ['ffn2_norm', 's5_lambda_re', 's5_lambda_im', 's5_b_re', 's5_b_im', 's5_c_re', 's5_c_im',
                   's5_log_dt', 's5_d', 'gla_w_gk', 'gla_b_gk', 'gla_norm', 'ret_norm', 'final_norm']
    a2a, tok = {}, [jnp.zeros((), F32)]

    def start_a2a(*tagged):
        for (tag, _), started in zip(tagged, _exchange_start("a2as_" + tagged[0][0], [b for _, b in tagged], False)):
            a2a[tag] = started
        tok[0] = tok[0] + started[4][0, 0]

    def dep(vec):
        return vec + tok[0]

    def proj_backward(tag, pieces, s, dres, a2a_tag):
        tm = _tile(l, 512, SUBLANE)
        row = pl.BlockSpec((tm, d), lambda ii, jj, kk: (ii, 0))
        vec = pl.BlockSpec((1, d), lambda ii, jj, kk: (0, 0))
        dws = [_mm_plain(f"{tag}_dwin_{k}", s['h'], piece, TN, BF, tm=d, tn=2048, tk=1024) for k, (piece, _, _) in enumerate(pieces)]
        start_a2a((a2a_tag, _col_blocks(jnp.concatenate(dws, axis=1))))
        pairs = []
        for piece, w, col in pieces:
            wd = piece.shape[1]
            pairs.append((piece, pl.BlockSpec((tm, wd), lambda ii, jj, kk: (ii, 0)),
                          w, pl.BlockSpec((d, wd), lambda ii, jj, kk, col=col: (0, col), pipeline_mode=pl.Buffered(1))))
        return _mm(f"{tag}_dh", pairs, NT, (l // tm, 1, 1), [_sds((l, d), F32), _sds((1, d), F32)], [row, vec], (tm, d),
                   _rms_bwd_epi(0), [s['x1'], dep(mix_norm[i:i + 1]), dres], [row, vec, row])

    for i in reversed(range(depth)):
        j = i // 2
        s = saved[i]
        def ffn_grads(which):
            def on_grads(dw1, dw2, gnorm):
                start_a2a((f"{which}_w1_{i}", _col_blocks(dw1)), (f"{which}_w2_{i}", _row_blocks(dw2)))
                return dep(gnorm)
            return on_grads

        dx, dg = _ffn_bwd(f"l{i}_ffn2b", dx, s['x2'], ffn2_norm[i:i + 1], s['f2w1'], s['f2w2'], s['ffn2'], ffn_grads("ffn2"))
        per_layer['ffn2_norm'][i] = dg[0]
        tm = _tile(l, 512, SUBLANE)
        row = pl.BlockSpec((tm, d), lambda ii, jj, kk: (ii, 0))
        vec = pl.BlockSpec((1, d), lambda ii, jj, kk: (0, 0))
        if i % 2 == 0:
            proj, lin_kw = s['proj'], s['lin_kw']
            w_out = s['w_out']
            d_cat = _mm_plain(f"l{i}_dcat", dx, w_out, NT, BF, tm=512, tn=1024, tk=d)
            dwo_a = _mm_plain(f"l{i}_dwout_a", s['s5_out'], dx, TN, BF, tm=s5w, tn=d, tk=512)
            dwo_b = _mm_plain(f"l{i}_dwout_b", s['gla_out'], dx, TN, BF, tm=gla_hv, tn=d, tk=512)
            start_a2a((f"ab_w_out_{j}", _row_blocks(jnp.concatenate([dwo_a, dwo_b], axis=0))))
            do, dog, dgn = _headgate_bwd(f"l{i}_gla_outb", s['o_b'], proj, s['ogcol'], dep(gla_norm[j:j + 1]), d_cat, 1, gla_dv)
            G['gla_norm'] = dgn
            dq, dk_, dv_, dgf = _gla_bwd(f"l{i}_gla_bwd_f", proj, s['g_f'], s['sp_f'], do, None, reverse=False, **lin_kw)
            dq, dk_, dv_, dgb = _gla_bwd(f"l{i}_gla_bwd_b", proj, s['g_b'], s['sp_b'], do, (dq, dk_, dv_), reverse=True, **lin_kw)
            dglo, dwf, dwb, dbf, dbb = _gate_bwd(f"l{i}_gla_gateb", s['glo'], s['wgk_f'], s['wgk_b'], s['b_gk'][0:1], s['b_gk'][1:2], dgf, dgb)
            G['gla_w_gk'] = jnp.stack([dwf[:GLA_RANK], dwb[GLA_RANK:]], axis=0)[None]
            G['gla_b_gk'] = jnp.concatenate([dbf, dbb], axis=0)[None]
            dy, dwglu = _glu_bwd(f"l{i}_s5_glub", s['y'], s['w_glu'], d_cat, 0)
            start_a2a((f"s5_w_glu_{j}", _row_blocks(dwglu.astype(BF))))
            cd16, bd16, a_tab = s['cd16'], s['bd16'], s['a_tab']
            nt2 = a_tab.shape[2]
            a_conj = a_tab * jnp.where(jnp.arange(nt2) < nt2 // 2, 1.0, -1.0)[None, None, :, None]
            du_f, dbd_f, dcd_f, da_f = _s5_bwd(f"l{i}_s5_bwd_f", proj, dy, s['x_f'], bd16[0], cd16[0], a_conj[0], True)
            du_b, dbd_b, dcd_b, da_b = _s5_bwd(f"l{i}_s5_bwd_b", proj, dy, s['x_b'], bd16[1], cd16[1], a_conj[1], False)
            du, dd = _s5_du(f"l{i}_s5_du", du_f, du_b, dy, proj, s5_d[j:j + 1])
            G['s5_d'] = dd
            cot = (jnp.stack([da_f, da_b]), jnp.stack([dbd_f, dbd_b]), jnp.stack([dcd_f, dcd_b]))
            g_lre, g_lim, g_bre, g_bim, g_cre, g_cim, g_ldt = s['s5_vjp'](cot)
            G['s5_lambda_re'], G['s5_lambda_im'], G['s5_b_re'], G['s5_b_im'] = g_lre[None], g_lim[None], g_bre[None], g_bim[None]
            G['s5_c_re'], G['s5_c_im'], G['s5_log_dt'] = g_cre[None], g_cim[None], g_ldt[None]
            if i == 0:
                G['ffn2_norm'] = jnp.stack(per_layer['ffn2_norm'], axis=0)
                early_packed = _pack([G[n] for n in small_early])
                early_started = _exchange_start("ags_small_grads_early", [early_packed], True)[0]
                tok[0] = tok[0] + early_started[4][0, 0]
            w_main, w_glo = s['w_main'], s['w_glo']
            pieces = [(du, w_main, 0), (dq, w_main, s5w // gla_hk), (dk_, w_main, s5w // gla_hk + 1),
                      (dv_, w_main, (s5w + 2 * gla_hk) // gla_hv), (dog, w_main, (s5w + 2 * gla_hk) // gla_hv + 1), (dglo, w_glo, 0)]
            dx, dg = proj_backward(f"l{i}", pieces, s, dx, f"ab_w_in_{j}")
        else:
            proj, lin_kw = s['proj'], s['lin_kw']
            w_out = s['w_out']
            d_ro = _mm_plain(f"l{i}_dro", dx, w_out, NT, BF, tm=512, tn=1024, tk=d, b_outer=True)
            dwo = _mm_plain(f"l{i}_dwout", s['r_out'], dx, TN, BF, tm=2048, tn=d, tk=1024)
            start_a2a((f"ret_w_out_{j}", _row_blocks(dwo)))
            do, dog, dgn = _headgate_bwd(f"l{i}_ret_outb", s['o_b'], proj, s['ogcol'], dep(ret_norm_full), d_ro, 0, ret_dv)
            G['ret_norm'] = dgn
            r1 = _lin_bwd(f"l{i}_ret_bwd_f", s['qr'], s['kr'], proj, lgtab_f, s['sp_f'], do, None, reverse=False, **lin_kw)
            r2 = _lin_bwd(f"l{i}_ret_bwd_b", s['qr'], s['kr'], proj, lgtab_b, s['sp_b'], do, r1, reverse=True, **lin_kw)
            dqr, dkr, dv_ = r2
            dq, dk_ = _rot_apply(f"l{i}_rotb", dqr, 0, dkr, 0, cos_t, sin_t, RET_HEADS, ret_dk, ret_dk ** -0.5, BF, True)
            w_in = s['w_in']
            pieces = [(dq, w_in, 0), (dk_, w_in, 1), (dv_, w_in, (2 * ret_hk) // ret_hv), (dog, w_in, (2 * ret_hk) // ret_hv + 1)]
            dx, dg = proj_backward(f"l{i}", pieces, s, dx, f"ret_w_in_{j}")
        per_layer['mix_norm'][i] = dg[0]
        dx, dg = _ffn_bwd(f"l{i}_ffn1b", dx, s['x0'], ffn1_norm[i:i + 1], s['f1w1'], s['f1w2'], s['ffn1'], ffn_grads("ffn1"))
        per_layer['ffn1_norm'][i] = dg[0]
    for n in small_late:
        G[n] = jnp.stack(per_layer[n], axis=0)
    grad_x = dx[None]

    out_g, out_d, out_m, out_v = {}, {}, {}, {}
    small = small_early + small_late
    packed = _pack([G[n] for n in small_late])
    small_started = _exchange_start("ags_small_grads_late", [packed], True)[0]

    def big_update(n, layers):
        parts = []
        for i in layers:
            blocks, got = _exchange_wait(f"a2aw_{n}_{i}", a2a.pop(f"{n}_{i}"), False, small_started[4])
            parts.append(lax.dynamic_update_index_in_dim(got, lax.dynamic_index_in_dim(blocks, me, 0, keepdims=False), me, 0))
        out_g[n], out_d[n], out_m[n], out_v[n] = _reduce_adam("upd_" + n, parts, W[n], M[n], V[n])

    for n in ['ffn2_w1', 'ffn2_w2', 'ffn1_w1', 'ffn1_w2']:
        big_update(n, range(depth))
    for n in ['ab_w_in', 's5_w_glu', 'ab_w_out', 'ret_w_in', 'ret_w_out']:
        big_update(n, [0])
    assert not a2a and not pending

    g_full = {}
    for tag, started, mine, group in (("early", early_started, early_packed, small_early), ("late", small_started, packed, small_late)):
        _, gathered = _exchange_wait("agw_small_grads_" + tag, started, True, out_v['ret_w_out'])
        gathered = lax.dynamic_update_index_in_dim(gathered, mine, me, 0)
        summed = _reduce8("sum_small_grads_" + tag, gathered)
        g_full.update(zip(group, _unpack(summed, [G[n] for n in group])))
    g_small = {}
    for n in small:
        gf = g_full[n]
        if n in ('gla_w_gk', 'gla_b_gk', 'ret_norm'):
            width = W[n].shape[-1]
            gf = lax.dynamic_slice_in_dim(gf, me * width, width, axis=gf.ndim - 1)
        g_small[n] = gf.reshape(W[n].shape)
    pw, pg, pm, pv = (_pack([src[n] for n in small]) for src in (W, g_small, M, V))
    pd, pm2, pv2 = _adam_packed("upd_small", pw, pg, pm, pv)
    like = [W[n] for n in small]
    for n, dd_, mm_, vv_ in zip(small, _unpack(pd, like), _unpack(pm2, like), _unpack(pv2, like)):
        out_g[n], out_d[n], out_m[n], out_v[n] = g_small[n], dd_, mm_, vv_

    return (loss, grad_x, *[out_g[n] for n in names], *[out_d[n] for n in names], *[out_m[n] for n in names], *[out_v[n] for n in names])
```

```python
import math

import jax
import jax.numpy as jnp
from jax import lax
from jax.experimental import pallas as pl
from jax.experimental.pallas import tpu as pltpu

F32 = jnp.float32
BF = jnp.bfloat16
N_DEV = 8
EPS = 1e-6
S5_GROUP = 16
GLA_HEADS = 4
GLA_RANK = 16
GLA_GATE_NORM = 16.0
RET_HEADS = 8
ROPE_BASE = 10000.0
GLA_CHUNK = 64
RET_CHUNK = 256
ADAM_LR, ADAM_B1, ADAM_B2, ADAM_EPS, ADAM_WD, ADAM_STEP = 0.001, 0.9, 0.999, 1e-08, 0.01, 10
VMEM_LIMIT_BYTES = 56 * 1024 * 1024
LANE = 128
SUBLANE = 8

NN = (((1,), (0,)), ((), ()))
NT = (((1,), (1,)), ((), ()))
TN = (((0,), (0,)), ((), ()))


def _tile(n, pref, align):
    if n <= pref:
        return n
    t = (pref // align) * align
    while t >= align:
        if n % t == 0:
            return t
        t -= align
    return n


def _params(sem):
    return pltpu.CompilerParams(dimension_semantics=sem, vmem_limit_bytes=VMEM_LIMIT_BYTES)


def _dot(a, b, dims=NN):
    return lax.dot_general(a.astype(BF), b.astype(BF), dims, preferred_element_type=F32)


def _dot3(m01, g, dims=NN):
    g1 = g.astype(BF)
    r1 = g - g1.astype(F32)
    g2 = r1.astype(BF)
    g3 = (r1 - g2.astype(F32)).astype(BF)
    m = m01.astype(BF)
    return (lax.dot_general(m, g1, dims, preferred_element_type=F32)
            + lax.dot_general(m, g2, dims, preferred_element_type=F32)
            + lax.dot_general(m, g3, dims, preferred_element_type=F32))


def _sigmoid(x):
    return 0.5 + 0.5 * jnp.tanh(0.5 * x)


def _mm(name, pairs, dims, grid, outs, out_specs, acc_shape, epi=None, eins=(), especs=()):
    n_p, n_e, n_o = len(pairs), len(eins), len(outs)
    nk = grid[2]

    def body(*refs):
        a_refs = refs[0:2 * n_p:2]
        b_refs = refs[1:2 * n_p:2]
        e_refs = refs[2 * n_p:2 * n_p + n_e]
        o_refs = refs[2 * n_p + n_e:2 * n_p + n_e + n_o]
        acc = refs[-1]
        ids = (pl.program_id(0), pl.program_id(1), pl.program_id(2))

        part = _dot(a_refs[0][...], b_refs[0][...], dims)
        for p in range(1, n_p):
            part = part + _dot(a_refs[p][...], b_refs[p][...], dims)

        def finish(total):
            if epi is None:
                o_refs[0][...] = total.astype(o_refs[0].dtype)
            else:
                epi(total, e_refs, o_refs, ids)

        if nk == 1:
            finish(part)
        else:
            @pl.when(ids[2] == 0)
            def _():
                acc[...] = part

            @pl.when(ids[2] > 0)
            def _():
                acc[...] += part

            @pl.when(ids[2] == nk - 1)
            def _():
                finish(acc[...])

    in_specs, args = [], []
    for a, a_spec, b, b_spec in pairs:
        in_specs += [a_spec, b_spec]
        args += [a, b]
    in_specs += list(especs)
    args += list(eins)
    res = pl.pallas_call(
        body, name=name, grid=grid, in_specs=in_specs, out_specs=list(out_specs), out_shape=list(outs),
        scratch_shapes=[pltpu.VMEM(acc_shape, F32)],
        compiler_params=_params(("arbitrary", "arbitrary", "arbitrary")),
    )(*args)
    return res


def _sds(shape, dtype):
    return jax.ShapeDtypeStruct(shape, dtype)


def _mm_plain(name, a, b, dims, out_dtype, tm=512, tn=1024, tk=1024, epi=None, eins=(), especs=None, extra_outs=(), extra_specs=(),
              b_outer=False):
    if dims == NN:
        (m, k), n = a.shape, b.shape[1]
    elif dims == NT:
        (m, k), n = a.shape, b.shape[0]
    else:
        (k, m), n = a.shape, b.shape[1]
    tm, tn = _tile(m, tm, LANE if dims == TN else SUBLANE), _tile(n, tn, LANE)
    tk = _tile(k, tk, SUBLANE if dims == TN else LANE)
    grid = (n // tn, m // tm, k // tk) if b_outer else (m // tm, n // tn, k // tk)

    def spec(block, index):
        if b_outer:
            return pl.BlockSpec(block, lambda j, i, kk: index(i, j, kk))
        return pl.BlockSpec(block, index)

    if dims == NN:
        a_spec = spec((tm, tk), lambda i, j, kk: (i, kk))
        b_spec = spec((tk, tn), lambda i, j, kk: (kk, j))
    elif dims == NT:
        a_spec = spec((tm, tk), lambda i, j, kk: (i, kk))
        b_spec = spec((tn, tk), lambda i, j, kk: (j, kk))
    else:
        a_spec = spec((tk, tm), lambda i, j, kk: (kk, i))
        b_spec = spec((tk, tn), lambda i, j, kk: (kk, j))
    o_spec = spec((tm, tn), lambda i, j, kk: (i, j))
    if especs is None:
        especs = [o_spec] * len(eins)
    else:
        especs = [o_spec if s is None else s for s in especs]
    res = _mm(name, [(a, a_spec, b, b_spec)], dims, grid, [_sds((m, n), out_dtype)] + list(extra_outs),
              [o_spec] + list(extra_specs), (tm, tn), epi, eins, especs)
    return res if extra_outs else res[0]


def _rms_fwd(name, x, g):
    l, d = x.shape
    tm = _tile(l, 1024, SUBLANE)

    def body(x_ref, g_ref, o_ref):
        xv = x_ref[...]
        r = lax.rsqrt(jnp.mean(xv * xv, axis=-1, keepdims=True) + EPS)
        o_ref[...] = (xv * r * g_ref[...]).astype(o_ref.dtype)

    return pl.pallas_call(
        body, name=name, grid=(l // tm,),
        in_specs=[pl.BlockSpec((tm, d), lambda i: (i, 0)), pl.BlockSpec((1, d), lambda i: (0, 0))],
        out_specs=pl.BlockSpec((tm, d), lambda i: (i, 0)), out_shape=_sds((l, d), BF),
        compiler_params=_params(("arbitrary",)),
    )(x, g)


def _rms_bwd_epi(first_axis):
    def epi(acc, e_refs, o_refs, ids):
        x_ref, g_ref, dr_ref = e_refs
        dx_ref, dg_ref = o_refs
        xv = x_ref[...]
        r = lax.rsqrt(jnp.mean(xv * xv, axis=-1, keepdims=True) + EPS)
        xh = xv * r
        dxh = acc * g_ref[...]
        dx_ref[...] = dr_ref[...] + r * (dxh - xh * jnp.mean(dxh * xh, axis=-1, keepdims=True))
        part = jnp.sum(acc * xh, axis=0, keepdims=True)

        @pl.when(ids[first_axis] == 0)
        def _():
            dg_ref[...] = part

        @pl.when(ids[first_axis] > 0)
        def _():
            dg_ref[...] += part

    return epi


def _loss_head(name, x, g, target):
    l, d = x.shape
    tm = _tile(l, 512, SUBLANE)
    n = l // tm

    def body(x_ref, g_ref, t_ref, dx_ref, dg_ref, loss_ref, lacc):
        i = pl.program_id(0)
        xv = x_ref[...]
        r = lax.rsqrt(jnp.mean(xv * xv, axis=-1, keepdims=True) + EPS)
        xh = xv * r
        e = xh * g_ref[...] - t_ref[...]
        dy = e * (1.0 / d)
        dxh = dy * g_ref[...]
        dx_ref[...] = r * (dxh - xh * jnp.mean(dxh * xh, axis=-1, keepdims=True))
        dg_part = jnp.sum(dy * xh, axis=0, keepdims=True)
        l_part = jnp.sum(e * e, axis=0, keepdims=True)

        @pl.when(i == 0)
        def _():
            dg_ref[...] = dg_part
            lacc[...] = l_part

        @pl.when(i > 0)
        def _():
            dg_ref[...] += dg_part
            lacc[...] += l_part

        @pl.when(i == n - 1)
        def _():
            loss_ref[...] = jnp.zeros_like(loss_ref) + jnp.sum(lacc[...]) * (0.5 / d)

    return pl.pallas_call(
        body, name=name, grid=(n,),
        in_specs=[pl.BlockSpec((tm, d), lambda i: (i, 0)), pl.BlockSpec((1, d), lambda i: (0, 0)),
                  pl.BlockSpec((tm, d), lambda i: (i, 0))],
        out_specs=[pl.BlockSpec((tm, d), lambda i: (i, 0)), pl.BlockSpec((1, d), lambda i: (0, 0)),
                   pl.BlockSpec((1, LANE), lambda i: (0, 0))],
        out_shape=[_sds((l, d), F32), _sds((1, d), F32), _sds((1, LANE), F32)],
        scratch_shapes=[pltpu.VMEM((1, d), F32)],
        compiler_params=_params(("arbitrary",)),
    )(x, g, target)


def _ffn_up(name, hn, w1):
    l, d = hn.shape
    f = w1.shape[1] // 2
    tm, tn = _tile(l, 512, SUBLANE), _tile(f, 1408, LANE)
    nj = f // tn

    def body(h_ref, wg_ref, wu_ref, gu_ref, a_ref):
        h = h_ref[...]
        g = jnp.dot(h, wg_ref[...], preferred_element_type=F32)
        u = jnp.dot(h, wu_ref[...], preferred_element_type=F32)
        s = _sigmoid(g)
        gs = g * s
        gu_ref[0] = (u * (s + gs * (1.0 - s))).astype(BF)
        gu_ref[1] = gs.astype(BF)
        a_ref[...] = (gs * u).astype(BF)

    return pl.pallas_call(
        body, name=name, grid=(nj, l // tm),
        in_specs=[pl.BlockSpec((tm, d), lambda j, i: (i, 0)), pl.BlockSpec((d, tn), lambda j, i: (0, j)),
                  pl.BlockSpec((d, tn), lambda j, i: (0, j + nj))],
        out_specs=[pl.BlockSpec((2, tm, tn), lambda j, i: (0, i, j)), pl.BlockSpec((tm, tn), lambda j, i: (i, j))],
        out_shape=[_sds((2, l, f), BF), _sds((l, f), BF)],
        compiler_params=_params(("arbitrary", "arbitrary")),
    )(hn, w1, w1)


def _residual_epi(scale, with_norm):
    def epi(acc, e_refs, o_refs, ids):
        xn = e_refs[0][...] + scale * acc
        o_refs[0][...] = xn
        if with_norm:
            r = lax.rsqrt(jnp.mean(xn * xn, axis=-1, keepdims=True) + EPS)
            o_refs[1][...] = (xn * r * e_refs[1][...]).astype(o_refs[1].dtype)

    return epi


def _ffn_fwd(tag, x, hn, get_w1, get_w2, next_gnorm):
    w1 = get_w1(hn)
    gu, a = _ffn_up(tag + "_up", hn, w1)
    w2 = get_w2(a)
    l, d = x.shape
    if next_gnorm is None:
        x_new = _mm_plain(tag + "_down", a, w2, NN, F32, tm=512, tn=d, tk=w2.shape[0], epi=_residual_epi(0.5, False), eins=[x])
        hn_next = None
    else:
        vec = pl.BlockSpec((1, d), lambda i, j, kk: (0, 0))
        tm = _tile(l, 512, SUBLANE)
        x_new, hn_next = _mm_plain(tag + "_down", a, w2, NN, F32, tm=512, tn=d, tk=w2.shape[0], epi=_residual_epi(0.5, True),
                                   eins=[x, next_gnorm], especs=[None, vec],
                                   extra_outs=[_sds((l, d), BF)], extra_specs=[pl.BlockSpec((tm, d), lambda i, j, kk: (i, 0))])
    return x_new, hn_next, (hn, gu, a), w1, w2


def _ffn_bwd(tag, dres, x, gnorm, w1, w2, saved, on_grads):
    hn, gu, a = saved
    l, d = x.shape
    f = w2.shape[0]
    tm, tn = _tile(l, 512, SUBLANE), _tile(f, 1408, LANE)
    nj = f // tn

    def epi_gu(acc, e_refs, o_refs, ids):
        da = 0.5 * acc
        o_refs[0][0] = (da * e_refs[0][0].astype(F32)).astype(BF)
        o_refs[0][1] = (da * e_refs[0][1].astype(F32)).astype(BF)

    gu_spec = pl.BlockSpec((2, tm, tn), lambda j, i, kk: (0, i, j))
    dgu = _mm(tag + "_dgu",
              [(dres, pl.BlockSpec((tm, d), lambda j, i, kk: (i, 0)), w2, pl.BlockSpec((tn, d), lambda j, i, kk: (j, 0)))],
              NT, (nj, l // tm, 1), [_sds((2, l, f), BF)], [gu_spec], (tm, tn), epi_gu, [gu], [gu_spec])[0]

    def epi_half(acc, e_refs, o_refs, ids):
        o_refs[0][...] = (0.5 * acc).astype(BF)

    dw2 = _mm_plain(tag + "_dw2", a, dres, TN, BF, tm=1408, tn=d, tk=1024, epi=epi_half)

    tk = _tile(l, 1024, SUBLANE)
    dw1 = _mm(tag + "_dw1",
              [(hn, pl.BlockSpec((tk, d), lambda i, j, kk: (kk, 0)), dgu, pl.BlockSpec((None, tk, f), lambda i, j, kk: (j, kk, 0)))],
              TN, (1, 2, l // tk), [_sds((d, 2 * f), BF)], [pl.BlockSpec((d, f), lambda i, j, kk: (0, j))], (d, f))[0]

    gnorm = on_grads(dw1, dw2, gnorm)
    row = pl.BlockSpec((tm, d), lambda i, j, kk: (i, 0))
    vec = pl.BlockSpec((1, d), lambda i, j, kk: (0, 0))
    once = pl.Buffered(1)
    dx, dg = _mm(tag + "_dhn",
                 [(dgu, pl.BlockSpec((None, tm, f), lambda i, j, kk: (0, i, 0)), w1, pl.BlockSpec((d, f), lambda i, j, kk: (0, 0), pipeline_mode=once)),
                  (dgu, pl.BlockSpec((None, tm, f), lambda i, j, kk: (1, i, 0)), w1, pl.BlockSpec((d, f), lambda i, j, kk: (0, 1), pipeline_mode=once))],
                 NT, (l // tm, 1, 1), [_sds((l, d), F32), _sds((1, d), F32)], [row, vec], (tm, d),
                 _rms_bwd_epi(0), [x, gnorm, dres], [row, vec, row])
    return dx, dg


def _s5_chunk_tables(lam_re, lam_im, b_re, b_im, c_re, c_im, log_dt, hs):
    f32 = F32
    g, n = lam_re.shape[1], lam_re.shape[2]
    p = b_re.shape[-1]
    nch, gpc, nt = (g * n) // hs, hs // n, hs // LANE
    lr = jnp.minimum(lam_re.astype(f32), -1e-4)
    li = lam_im.astype(f32)
    dt = jnp.exp(log_dt.astype(f32))[..., None]
    mag = jnp.exp(lr * dt)
    ar = mag * jnp.cos(li * dt)
    ai = mag * jnp.sin(li * dt)
    den = lr * lr + li * li
    cr = ((ar - 1.0) * lr + ai * li) / den
    ci = (ai * lr - (ar - 1.0) * li) / den
    bbr = cr[..., None] * b_re - ci[..., None] * b_im
    bbi = cr[..., None] * b_im + ci[..., None] * b_re
    a_f = jnp.stack([ar, ai], axis=1).reshape(2, 2, nch, nt, LANE).transpose(0, 2, 1, 3, 4).reshape(2, nch, 2 * nt, LANE)
    rows_g = jnp.arange(gpc * p) // p
    cols_g = (jnp.arange(2 * hs) % hs) // n
    diag = (rows_g[:, None] == cols_g[None, :]).astype(f32)
    bb = jnp.stack([bbr, bbi], axis=1).reshape(2, 2, nch, hs, p)
    bd = jnp.tile(bb.transpose(0, 2, 4, 1, 3).reshape(2, nch, p, 2 * hs), (1, 1, gpc, 1)) * diag
    cc = jnp.stack([c_re, -c_im], axis=1).reshape(2, 2, nch, gpc, p, n)
    cd = jnp.tile(cc.transpose(0, 2, 4, 1, 3, 5).reshape(2, nch, p, 2 * hs), (1, 1, gpc, 1)) * diag
    return a_f, bd, cd


def _fold_store(ref, val, tb, ntiles):
    for s in range(ntiles):
        ref[:, s * SUBLANE:(s + 1) * SUBLANE, :] = val[:, s * LANE:(s + 1) * LANE].reshape(tb // SUBLANE, SUBLANE, LANE)


def _unfold(ref, tb, ntiles):
    return jnp.concatenate([ref[:, s * SUBLANE:(s + 1) * SUBLANE, :].reshape(tb, LANE) for s in range(ntiles)], axis=1)


def _s5_fwd(name, proj, bd, cd, a_f, reverse):
    l = proj.shape[0]
    nch, cu, hs2 = bd.shape
    nt = hs2 // (2 * LANE)
    frows = 2 * nt * SUBLANE
    tb = _tile(l, 512, SUBLANE)
    nb = l // tb

    def body(u_ref, bd_ref, cd_ref, a_ref, xf_ref, y_ref, st):
        r = pl.program_id(1)

        @pl.when(r == 0)
        def _():
            st[...] = jnp.zeros_like(st)

        _fold_store(xf_ref, _dot(u_ref[...], bd_ref[...]), tb, 2 * nt)
        ar, ai = a_ref[0:nt, :], a_ref[nt:2 * nt, :]

        def group(gi, carry):
            rr = (tb // SUBLANE - 1 - gi) if reverse else gi
            sr, si = carry
            for qq in range(SUBLANE):
                q = (SUBLANE - 1 - qq) if reverse else qq
                re_rows, im_rows = pl.ds(q, nt, stride=SUBLANE), pl.ds(nt * SUBLANE + q, nt, stride=SUBLANE)
                nr = ar * sr - ai * si + xf_ref[rr, re_rows, :]
                ni = ar * si + ai * sr + xf_ref[rr, im_rows, :]
                xf_ref[rr, re_rows, :] = nr
                xf_ref[rr, im_rows, :] = ni
                sr, si = nr, ni
            return sr, si

        fin = lax.fori_loop(0, tb // SUBLANE, group, (st[0:nt, :], st[nt:2 * nt, :]))
        st[0:nt, :] = fin[0]
        st[nt:2 * nt, :] = fin[1]
        y_ref[...] = _dot(_unfold(xf_ref, tb, 2 * nt), cd_ref[...], NT)

    def rows(r):
        return (nb - 1 - r) if reverse else r

    return pl.pallas_call(
        body, name=name, grid=(nch, nb),
        in_specs=[pl.BlockSpec((tb, cu), lambda c, r: (rows(r), c)), pl.BlockSpec((None, cu, hs2), lambda c, r: (c, 0, 0)),
                  pl.BlockSpec((None, cu, hs2), lambda c, r: (c, 0, 0)), pl.BlockSpec((None, 2 * nt, LANE), lambda c, r: (c, 0, 0))],
        out_specs=[pl.BlockSpec((tb // SUBLANE, frows, LANE), lambda c, r: (rows(r), c, 0)), pl.BlockSpec((tb, cu), lambda c, r: (rows(r), c))],
        out_shape=[_sds((l // SUBLANE, nch * frows, LANE), F32), _sds((l, nch * cu), F32)],
        scratch_shapes=[pltpu.VMEM((2 * nt, LANE), F32)],
        compiler_params=_params(("arbitrary", "arbitrary")),
    )(proj, bd, cd, a_f)


def _s5_bwd(name, proj, dy, xf, bd, cd, a_conj, reverse):
    l = proj.shape[0]
    nch, cu, hs2 = bd.shape
    nt = hs2 // (2 * LANE)
    frows = 2 * nt * SUBLANE
    tb = _tile(l, 512, SUBLANE)
    nb = l // tb

    def body(u_ref, dy_ref, xs_ref, bd_ref, cd_ref, a_ref, du_ref, dbd_ref, dcd_ref, da_ref, lam, st):
        r = pl.program_id(1)

        @pl.when(r == 0)
        def _():
            st[...] = jnp.zeros_like(st)
            dbd_ref[...] = jnp.zeros_like(dbd_ref)
            dcd_ref[...] = jnp.zeros_like(dcd_ref)
            da_ref[...] = jnp.zeros_like(da_ref)

        dyv = dy_ref[...]
        _fold_store(lam, _dot(dyv, cd_ref[...]), tb, 2 * nt)
        ar, ai = a_ref[0:nt, :], a_ref[nt:2 * nt, :]

        def group(gi, carry):
            rr = (tb // SUBLANE - 1 - gi) if reverse else gi
            sr, si, cr, ci = carry
            for qq in range(SUBLANE):
                q = (SUBLANE - 1 - qq) if reverse else qq
                re_rows, im_rows = pl.ds(q, nt, stride=SUBLANE), pl.ds(nt * SUBLANE + q, nt, stride=SUBLANE)
                xr, xi = xs_ref[rr, re_rows, :], xs_ref[rr, im_rows, :]
                cr = cr + sr * xr + si * xi
                ci = ci + si * xr - sr * xi
                nr = ar * sr - ai * si + lam[rr, re_rows, :]
                ni = ar * si + ai * sr + lam[rr, im_rows, :]
                lam[rr, re_rows, :] = nr
                lam[rr, im_rows, :] = ni
                sr, si = nr, ni
            return sr, si, cr, ci

        zero = jnp.zeros((nt, LANE), F32)
        fin = lax.fori_loop(0, tb // SUBLANE, group, (st[0:nt, :], st[nt:2 * nt, :], zero, zero))
        st[0:nt, :] = fin[0]
        st[nt:2 * nt, :] = fin[1]
        da_ref[0:nt, :] += fin[2]
        da_ref[nt:2 * nt, :] += fin[3]
        lam_u = _unfold(lam, tb, 2 * nt)
        du_ref[...] = _dot(lam_u, bd_ref[...], NT)
        dbd_ref[...] += _dot(u_ref[...], lam_u, TN)
        dcd_ref[...] += _dot(dyv, _unfold(xs_ref, tb, 2 * nt), TN)

    def rows(r):
        return (nb - 1 - r) if reverse else r

    chunk_rows = pl.BlockSpec((tb, cu), lambda c, r: (rows(r), c))
    bd_spec = pl.BlockSpec((None, cu, hs2), lambda c, r: (c, 0, 0))
    cd_spec = bd_spec
    a_spec = pl.BlockSpec((None, 2 * nt, LANE), lambda c, r: (c, 0, 0))
    return pl.pallas_call(
        body, name=name, grid=(nch, nb),
        in_specs=[chunk_rows, chunk_rows, pl.BlockSpec((tb // SUBLANE, frows, LANE), lambda c, r: (rows(r), c, 0)), bd_spec, cd_spec, a_spec],
        out_specs=[chunk_rows, bd_spec, cd_spec, a_spec],
        out_shape=[_sds((l, nch * cu), F32), _sds((nch, cu, hs2), F32), _sds((nch, cu, hs2), F32), _sds((nch, 2 * nt, LANE), F32)],
        scratch_shapes=[pltpu.VMEM((tb // SUBLANE, frows, LANE), F32), pltpu.VMEM((2 * nt, LANE), F32)],
        compiler_params=_params(("arbitrary", "arbitrary")),
    )(proj, dy, xf, bd, cd, a_conj)


def _s5_du(name, du_f, du_b, dy, proj, d_row):
    l, w = dy.shape
    tm = _tile(l, 1024, SUBLANE)

    def body(f_ref, b_ref, dy_ref, u_ref, d_ref, du_ref, dd_ref):
        i = pl.program_id(0)
        dyv = dy_ref[...]
        du_ref[...] = (f_ref[...] + b_ref[...] + dyv * d_ref[...]).astype(du_ref.dtype)
        part = jnp.sum(dyv * u_ref[...].astype(F32), axis=0, keepdims=True)

        @pl.when(i == 0)
        def _():
            dd_ref[...] = part

        @pl.when(i > 0)
        def _():
            dd_ref[...] += part

    row = pl.BlockSpec((tm, w), lambda i: (i, 0))
    vec = pl.BlockSpec((1, w), lambda i: (0, 0))
    return pl.pallas_call(
        body, name=name, grid=(l // tm,), in_specs=[row, row, row, row, vec], out_specs=[row, vec],
        out_shape=[_sds((l, w), BF), _sds((1, w), F32)],
        compiler_params=_params(("arbitrary",)),
    )(du_f, du_b, dy, proj, d_row)


def _gelu(y):
    c = math.sqrt(2.0 / math.pi)
    return 0.5 * y * (1.0 + jnp.tanh(c * (y + 0.044715 * y * y * y)))


def _gelu_grad(y):
    c = math.sqrt(2.0 / math.pi)
    th = jnp.tanh(c * (y + 0.044715 * y * y * y))
    return 0.5 * (1.0 + th) + 0.5 * y * (1.0 - th * th) * c * (1.0 + 3.0 * 0.044715 * y * y)


def _glu_fwd(name, y_f, y_b, proj, d_row, w):
    l, wd = y_f.shape
    tm = _tile(l, 512, SUBLANE)

    def body(yf_ref, yb_ref, u_ref, d_ref, w_ref, y_ref, o_ref):
        y = yf_ref[...] + yb_ref[...] + u_ref[...].astype(F32) * d_ref[...]
        y_ref[...] = y
        gy = _gelu(y)
        z = _dot(gy, w_ref[...])
        o_ref[...] = (gy * _sigmoid(z)).astype(o_ref.dtype)

    row = pl.BlockSpec((tm, wd), lambda i: (i, 0))
    return pl.pallas_call(
        body, name=name, grid=(l // tm,),
        in_specs=[row, row, row, pl.BlockSpec((1, wd), lambda i: (0, 0)), pl.BlockSpec((wd, wd), lambda i: (0, 0))],
        out_specs=[row, row], out_shape=[_sds((l, wd), F32), _sds((l, wd), BF)],
        compiler_params=_params(("arbitrary",)),
    )(y_f, y_b, proj, d_row, w)


def _glu_bwd(name, y, w, dout, dcol):
    l, wd = y.shape
    tm = _tile(l, 512, SUBLANE)

    def body(y_ref, w_ref, d_ref, dy_ref, dw_ref):
        i = pl.program_id(0)
        yv = y_ref[...]
        gy = _gelu(yv)
        s = _sigmoid(_dot(gy, w_ref[...]))
        d = d_ref[...].astype(F32)
        t = d * gy * s * (1.0 - s)
        dgy = d * s + _dot(t, w_ref[...], NT)
        dy_ref[...] = dgy * _gelu_grad(yv)
        part = _dot(gy, t, TN)

        @pl.when(i == 0)
        def _():
            dw_ref[...] = part

        @pl.when(i > 0)
        def _():
            dw_ref[...] += part

    return pl.pallas_call(
        body, name=name, grid=(l // tm,),
        in_specs=[pl.BlockSpec((tm, wd), lambda i: (i, 0)), pl.BlockSpec((wd, wd), lambda i: (0, 0)),
                  pl.BlockSpec((tm, wd), lambda i: (i, dcol))],
        out_specs=[pl.BlockSpec((tm, wd), lambda i: (i, 0)), pl.BlockSpec((wd, wd), lambda i: (0, 0))],
        out_shape=[_sds((l, wd), F32), _sds((wd, wd), F32)],
        compiler_params=_params(("arbitrary",)),
    )(y, w, dout)


def _log_sigmoid(x):
    return jnp.minimum(x, 0.0) - jnp.log(1.0 + jnp.exp(-jnp.abs(x)))


def _gate_fwd(name, glo, wf, wb, bf, bb):
    l, r2 = glo.shape
    hk = wf.shape[1]
    tm = _tile(l, 1024, SUBLANE)

    def body(x_ref, wf_ref, wb_ref, bf_ref, bb_ref, gf_ref, gb_ref):
        xv = x_ref[...]
        gf_ref[...] = _log_sigmoid(_dot(xv, wf_ref[...]) + bf_ref[...]) * (1.0 / GLA_GATE_NORM)
        gb_ref[...] = _log_sigmoid(_dot(xv, wb_ref[...]) + bb_ref[...]) * (1.0 / GLA_GATE_NORM)

    w_spec = pl.BlockSpec((r2, hk), lambda i: (0, 0))
    b_spec = pl.BlockSpec((1, hk), lambda i: (0, 0))
    o_spec = pl.BlockSpec((tm, hk), lambda i: (i, 0))
    return pl.pallas_call(
        body, name=name, grid=(l // tm,),
        in_specs=[pl.BlockSpec((tm, r2), lambda i: (i, 0)), w_spec, w_spec, b_spec, b_spec],
        out_specs=[o_spec, o_spec], out_shape=[_sds((l, hk), F32), _sds((l, hk), F32)],
        compiler_params=_params(("arbitrary",)),
    )(glo, wf, wb, bf, bb)


def _gate_bwd(name, glo, wf, wb, bf, bb, dgf, dgb):
    l, r2 = glo.shape
    hk = wf.shape[1]
    tm = _tile(l, 1024, SUBLANE)

    def body(x_ref, wf_ref, wb_ref, bf_ref, bb_ref, dgf_ref, dgb_ref, dx_ref, dwf_ref, dwb_ref, dbf_ref, dbb_ref):
        i = pl.program_id(0)
        xv = x_ref[...]
        kf = _dot(xv, wf_ref[...]) + bf_ref[...]
        kb = _dot(xv, wb_ref[...]) + bb_ref[...]
        dkf = dgf_ref[...] * (1.0 / GLA_GATE_NORM) * _sigmoid(-kf)
        dkb = dgb_ref[...] * (1.0 / GLA_GATE_NORM) * _sigmoid(-kb)
        dx_ref[...] = _dot(dkf, wf_ref[...], NT) + _dot(dkb, wb_ref[...], NT)
        parts = (_dot(xv, dkf, TN), _dot(xv, dkb, TN), jnp.sum(dkf, axis=0, keepdims=True), jnp.sum(dkb, axis=0, keepdims=True))
        accs = (dwf_ref, dwb_ref, dbf_ref, dbb_ref)

        @pl.when(i == 0)
        def _():
            for a_, p_ in zip(accs, parts):
                a_[...] = p_

        @pl.when(i > 0)
        def _():
            for a_, p_ in zip(accs, parts):
                a_[...] += p_

    w_spec = pl.BlockSpec((r2, hk), lambda i: (0, 0))
    b_spec = pl.BlockSpec((1, hk), lambda i: (0, 0))
    g_spec = pl.BlockSpec((tm, hk), lambda i: (i, 0))
    x_spec = pl.BlockSpec((tm, r2), lambda i: (i, 0))
    return pl.pallas_call(
        body, name=name, grid=(l // tm,),
        in_specs=[x_spec, w_spec, w_spec, b_spec, b_spec, g_spec, g_spec],
        out_specs=[x_spec, w_spec, w_spec, b_spec, b_spec],
        out_shape=[_sds((l, r2), F32), _sds((r2, hk), F32), _sds((r2, hk), F32), _sds((1, hk), F32), _sds((1, hk), F32)],
        compiler_params=_params(("arbitrary",)),
    )(glo, wf, wb, bf, bb, dgf, dgb)


def _chunk_terms(qc, kc, lg, chunk, reverse):
    ri = lax.broadcasted_iota(jnp.int32, (chunk, chunk), 0)
    ci = lax.broadcasted_iota(jnp.int32, (chunk, chunk), 1)
    mask = (ci > ri) if reverse else (ci <= ri)
    pos = lax.broadcasted_iota(jnp.int32, (chunk, 1), 0).astype(F32)
    cum = ((chunk - pos) if reverse else (pos + 1.0)) * lg
    last = chunk * lg
    e = jnp.exp(cum)
    einv = jnp.exp(-cum)
    dec = jnp.exp(last - cum)
    return e, einv, dec, qc * e, kc * einv, kc * dec, jnp.exp(last), mask


def _lin_specs(width, col, tb, nb, reverse):
    return pl.BlockSpec((tb, width), lambda h, r: ((nb - 1 - r) if reverse else r, col + h))


def _lin_fwd(name, q, k, v, lgtab, prev_o=None, *, heads, hb, dk, dv, chunk, tb, qcol, kcol, vcol, reverse):
    l = q.shape[0]
    nb, ncb, ng = l // tb, tb // chunk, heads // hb

    def body(q_ref, k_ref, v_ref, lg_ref, *rest):
        p_ref = rest[0] if prev_o is not None else None
        o_ref, sp_ref, st = rest[-3:]

        @pl.when(pl.program_id(1) == 0)
        def _():
            st[...] = jnp.zeros_like(st)

        for c in range(ncb):
            cc = (ncb - 1 - c) if reverse else c
            rows = pl.ds(cc * chunk, chunk)
            for h in range(hb):
                ks, vs = slice(h * dk, (h + 1) * dk), slice(h * dv, (h + 1) * dv)
                qc, kc, vc = q_ref[rows, ks].astype(F32), k_ref[rows, ks].astype(F32), v_ref[rows, vs]
                _, _, _, qd, ki, kdec, e_last, mask = _chunk_terms(qc, kc, lg_ref[h, :, 0:1], chunk, reverse)
                a = jnp.where(mask, _dot(qd, ki, NT), 0.0)
                s_t = st[h]
                oc = _dot(a, vc) + _dot(qd, s_t, NT)
                if p_ref is not None:
                    oc = oc + p_ref[rows, vs]
                o_ref[rows, vs] = oc.astype(o_ref.dtype)
                sp_ref[cc, h] = s_t
                st[h] = s_t * e_last + _dot(vc, kdec, TN)

    o_spec = _lin_specs(hb * dv, 0, tb, nb, reverse)
    extra = [] if prev_o is None else [prev_o]
    return pl.pallas_call(
        body, name=name, grid=(ng, nb),
        in_specs=[_lin_specs(hb * dk, qcol, tb, nb, reverse), _lin_specs(hb * dk, kcol, tb, nb, reverse),
                  _lin_specs(hb * dv, vcol, tb, nb, reverse), pl.BlockSpec((hb, 1, LANE), lambda h, r: (h, 0, 0))] + [o_spec] * len(extra),
        out_specs=[o_spec, pl.BlockSpec((ncb, hb, dv, dk), lambda h, r: ((nb - 1 - r) if reverse else r, h, 0, 0))],
        out_shape=[_sds((l, heads * dv), F32 if prev_o is None else BF), _sds((l // chunk, heads, dv, dk), F32)],
        scratch_shapes=[pltpu.VMEM((hb, dv, dk), F32)],
        compiler_params=_params(("arbitrary", "arbitrary")),
    )(q, k, v, lgtab, *extra)


def _lin_bwd(name, q, k, v, lgtab, sprev, do, prev, *, heads, hb, dk, dv, chunk, tb, qcol, kcol, vcol, reverse):
    l = q.shape[0]
    nb, ncb, ng = l // tb, tb // chunk, heads // hb
    brev = not reverse
    n_prev = 0 if prev is None else len(prev)

    def body(q_ref, k_ref, v_ref, lg_ref, sp_ref, do_ref, *rest):
        p_refs = rest[:n_prev]
        dq_ref, dk_ref, dv_ref, dst = rest[n_prev:]

        @pl.when(pl.program_id(1) == 0)
        def _():
            dst[...] = jnp.zeros_like(dst)

        for c in range(ncb):
            cc = (ncb - 1 - c) if brev else c
            rows = pl.ds(cc * chunk, chunk)
            for h in range(hb):
                ks, vs = slice(h * dk, (h + 1) * dk), slice(h * dv, (h + 1) * dv)
                qc, kc, vc = q_ref[rows, ks].astype(F32), k_ref[rows, ks].astype(F32), v_ref[rows, vs]
                e, einv, dec, qd, ki, kdec, e_last, mask = _chunk_terms(qc, kc, lg_ref[h, :, 0:1], chunk, reverse)
                a = jnp.where(mask, _dot(qd, ki, NT), 0.0)
                s_t, ds_t, doc = sp_ref[cc, h], dst[h], do_ref[rows, vs]
                dvc = _dot(a, doc, TN) + _dot(kdec, ds_t, NT)
                da = jnp.where(mask, _dot(doc, vc, NT), 0.0)
                dqc = (_dot(da, ki) + _dot(doc, s_t)) * e
                dkc = _dot(da, qd, TN) * einv + _dot(vc, ds_t) * dec
                dst[h] = ds_t * e_last + _dot(doc, qd, TN)
                if n_prev:
                    dqc = dqc + p_refs[0][rows, ks]
                    dkc = dkc + p_refs[1][rows, ks]
                    dvc = dvc + p_refs[2][rows, vs]
                dq_ref[rows, ks] = dqc
                dk_ref[rows, ks] = dkc
                dv_ref[rows, vs] = dvc.astype(dv_ref.dtype)

    k_spec, v_spec = _lin_specs(hb * dk, 0, tb, nb, brev), _lin_specs(hb * dv, 0, tb, nb, brev)
    in_specs = [_lin_specs(hb * dk, qcol, tb, nb, brev), _lin_specs(hb * dk, kcol, tb, nb, brev), _lin_specs(hb * dv, vcol, tb, nb, brev),
                pl.BlockSpec((hb, 1, LANE), lambda h, r: (h, 0, 0)),
                pl.BlockSpec((ncb, hb, dv, dk), lambda h, r: ((nb - 1 - r) if brev else r, h, 0, 0)), v_spec]
    args = [q, k, v, lgtab, sprev, do]
    if n_prev:
        in_specs += [k_spec, k_spec, v_spec]
        args += list(prev)
    return pl.pallas_call(
        body, name=name, grid=(ng, nb), in_specs=in_specs, out_specs=[k_spec, k_spec, v_spec],
        out_shape=[_sds((l, heads * dk), F32), _sds((l, heads * dk), F32), _sds((l, heads * dv), BF if n_prev else F32)],
        scratch_shapes=[pltpu.VMEM((hb, dv, dk), F32)],
        compiler_params=_params(("arbitrary", "arbitrary")),
    )(*args)


def _log2(n):
    assert n & (n - 1) == 0, "a power of two"
    return n.bit_length() - 1


def _gla_block_terms(q, k, g, qscale, chunk, tb, reverse):
    ri = lax.broadcasted_iota(jnp.int32, (tb, tb), 0)
    ci = lax.broadcasted_iota(jnp.int32, (tb, tb), 1)
    same = jnp.right_shift(ri, _log2(chunk)) == jnp.right_shift(ci, _log2(chunk))
    t_in = jnp.logical_and(same, (ci >= ri) if reverse else (ci <= ri)).astype(F32)
    cum = _dot3(t_in, g)
    tot = _dot3(same.astype(F32), g)
    e = jnp.exp(cum)
    einv = jnp.exp(-cum)
    dec = jnp.exp(tot - cum)
    return e, einv, dec, jnp.exp(tot), q * (qscale * e), k * einv, k * dec, t_in


def _gla_masks(hk, dk, heads, chunk, reverse):
    lane = lax.broadcasted_iota(jnp.int32, (1, hk), 1)
    head_of = jnp.right_shift(lane, _log2(dk))
    ri = lax.broadcasted_iota(jnp.int32, (chunk, chunk), 0)
    ci = lax.broadcasted_iota(jnp.int32, (chunk, chunk), 1)
    return [head_of == h for h in range(heads)], ((ci > ri) if reverse else (ci <= ri))


def _gla_fwd(name, proj, g, prev_o=None, *, heads, dk, dv, chunk, tb, qcol, kcol, vcol, qscale, reverse):
    l = proj.shape[0]
    nb, ncb, hk, hv = l // tb, tb // chunk, heads * dk, heads * dv

    def body(q_ref, k_ref, v_ref, g_ref, *rest):
        p_ref = rest[0] if prev_o is not None else None
        o_ref, sp_ref, st = rest[-3:]

        @pl.when(pl.program_id(0) == 0)
        def _():
            st[...] = jnp.zeros_like(st)

        _, _, _, etot, qd, ki, kdec, _ = _gla_block_terms(q_ref[...].astype(F32), k_ref[...].astype(F32), g_ref[...], qscale, chunk, tb, reverse)
        heads_m, causal = _gla_masks(hk, dk, heads, chunk, reverse)
        s_all = st[...]
        for c in range(ncb):
            cc = (ncb - 1 - c) if reverse else c
            rc = slice(cc * chunk, (cc + 1) * chunk)
            qd_c, ki_c, kdec_c = qd[rc], ki[rc], kdec[rc]
            sp_ref[cc] = s_all
            kv = jnp.zeros_like(s_all)
            for h in range(heads):
                vs = slice(h * dv, (h + 1) * dv)
                qm = jnp.where(heads_m[h], qd_c, 0.0)
                a = jnp.where(causal, _dot(qm, ki_c, NT), 0.0)
                vc = v_ref[rc, vs]
                oc = _dot(a, vc) + _dot(qm, s_all, NT)
                if p_ref is not None:
                    oc = oc + p_ref[rc, vs]
                o_ref[rc, vs] = oc.astype(o_ref.dtype)
                kv = kv + jnp.where(heads_m[h], _dot(vc, kdec_c, TN), 0.0)
            s_all = s_all * etot[rc][0:1, :] + kv
        st[...] = s_all

    def rows(r):
        return (nb - 1 - r) if reverse else r

    o_spec = pl.BlockSpec((tb, hv), lambda r: (rows(r), 0))
    extra = [] if prev_o is None else [prev_o]
    return pl.pallas_call(
        body, name=name, grid=(nb,),
        in_specs=[pl.BlockSpec((tb, hk), lambda r: (rows(r), qcol)), pl.BlockSpec((tb, hk), lambda r: (rows(r), kcol)),
                  pl.BlockSpec((tb, hv), lambda r: (rows(r), vcol)), pl.BlockSpec((tb, hk), lambda r: (rows(r), 0))] + [o_spec] * len(extra),
        out_specs=[o_spec, pl.BlockSpec((ncb, dv, hk), lambda r: (rows(r), 0, 0))],
        out_shape=[_sds((l, hv), F32 if prev_o is None else BF), _sds((l // chunk, dv, hk), F32)],
        scratch_shapes=[pltpu.VMEM((dv, hk), F32)],
        compiler_params=_params(("arbitrary",)),
    )(proj, proj, proj, g, *extra)


def _gla_bwd(name, proj, g, sprev, do, prev, *, heads, dk, dv, chunk, tb, qcol, kcol, vcol, qscale, reverse):
    l = proj.shape[0]
    nb, ncb, hk, hv = l // tb, tb // chunk, heads * dk, heads * dv
    brev = not reverse
    n_prev = 0 if prev is None else len(prev)

    def body(q_ref, k_ref, v_ref, g_ref, sp_ref, do_ref, *rest):
        p_refs = rest[:n_prev]
        dq_ref, dk_ref, dv_ref, dg_ref, dst, dcs = rest[n_prev:]

        @pl.when(pl.program_id(0) == 0)
        def _():
            dst[...] = jnp.zeros_like(dst)

        e, einv, dec, etot, qd, ki, kdec, t_in = _gla_block_terms(q_ref[...].astype(F32), k_ref[...].astype(F32), g_ref[...], qscale, chunk, tb, reverse)
        heads_m, causal = _gla_masks(hk, dk, heads, chunk, reverse)
        last_row = lax.broadcasted_iota(jnp.int32, (chunk, 1), 0) == (0 if reverse else chunk - 1)
        ds_all = dst[...]
        for c in range(ncb):
            cc = (ncb - 1 - c) if brev else c
            rc = slice(cc * chunk, (cc + 1) * chunk)
            qd_c, ki_c, kdec_c = qd[rc], ki[rc], kdec[rc]
            s_all = sp_ref[cc]
            et = etot[rc][0:1, :]
            dqd = jnp.zeros((chunk, hk), F32)
            dki = jnp.zeros((chunk, hk), F32)
            dkdec = jnp.zeros((chunk, hk), F32)
            ds_add = jnp.zeros_like(ds_all)
            for h in range(heads):
                vs = slice(h * dv, (h + 1) * dv)
                m = heads_m[h]
                qm = jnp.where(m, qd_c, 0.0)
                a = jnp.where(causal, _dot(qm, ki_c, NT), 0.0)
                doc, vc = do_ref[rc, vs], v_ref[rc, vs]
                dvc = _dot(a, doc, TN) + _dot(jnp.where(m, kdec_c, 0.0), ds_all, NT)
                if n_prev:
                    dvc = dvc + p_refs[2][rc, vs]
                dv_ref[rc, vs] = dvc.astype(dv_ref.dtype)
                da = jnp.where(causal, _dot(doc, vc, NT), 0.0)
                dqd = dqd + jnp.where(m, _dot(da, ki_c) + _dot(doc, s_all), 0.0)
                dki = dki + _dot(da, qm, TN)
                dkdec = dkdec + jnp.where(m, _dot(vc, ds_all), 0.0)
                ds_add = ds_add + _dot(doc, qm, TN)
            dqc = dqd * e[rc] * qscale
            dkc = dki * einv[rc] + dkdec * dec[rc]
            if n_prev:
                dqc = dqc + p_refs[0][rc, :]
                dkc = dkc + p_refs[1][rc, :]
            dq_ref[rc, :] = dqc
            dk_ref[rc, :] = dkc
            dlast = jnp.sum(dkdec * kdec_c, axis=0, keepdims=True) + et * jnp.sum(s_all * ds_all, axis=0, keepdims=True)
            dcs[rc, :] = dqd * qd_c - dki * ki_c - dkdec * kdec_c + jnp.where(last_row, dlast, 0.0)
            ds_all = ds_all * et + ds_add
        dst[...] = ds_all
        dg_ref[...] = _dot3(t_in, dcs[...], TN)

    def rows(r):
        return (nb - 1 - r) if brev else r

    k_spec = pl.BlockSpec((tb, hk), lambda r: (rows(r), 0))
    v_spec = pl.BlockSpec((tb, hv), lambda r: (rows(r), 0))
    in_specs = [pl.BlockSpec((tb, hk), lambda r: (rows(r), qcol)), pl.BlockSpec((tb, hk), lambda r: (rows(r), kcol)),
                pl.BlockSpec((tb, hv), lambda r: (rows(r), vcol)), k_spec,
                pl.BlockSpec((ncb, dv, hk), lambda r: (rows(r), 0, 0)), v_spec]
    args = [proj, proj, proj, g, sprev, do]
    if n_prev:
        in_specs += [k_spec, k_spec, v_spec]
        args += list(prev)
    return pl.pallas_call(
        body, name=name, grid=(nb,), in_specs=in_specs, out_specs=[k_spec, k_spec, v_spec, k_spec],
        out_shape=[_sds((l, hk), F32), _sds((l, hk), F32), _sds((l, hv), BF if n_prev else F32), _sds((l, hk), F32)],
        scratch_shapes=[pltpu.VMEM((dv, hk), F32), pltpu.VMEM((tb, hk), F32)],
        compiler_params=_params(("arbitrary",)),
    )(*args)


def _headgate_fwd(name, o_sum, og_arr, og_col, gn, dv):
    l, w = o_sum.shape
    tm = _tile(l, 512, SUBLANE)
    nh = w // dv

    def body(o_ref, og_ref, gn_ref, out_ref):
        for h in range(nh):
            cs = slice(h * dv, (h + 1) * dv)
            o = o_ref[:, cs].astype(F32)
            r = lax.rsqrt(jnp.mean(o * o, axis=-1, keepdims=True) + EPS)
            og = og_ref[:, cs].astype(F32)
            out_ref[:, cs] = (o * r * gn_ref[:, cs] * (og * _sigmoid(og))).astype(out_ref.dtype)

    row = pl.BlockSpec((tm, w), lambda i: (i, 0))
    return pl.pallas_call(
        body, name=name, grid=(l // tm,),
        in_specs=[row, pl.BlockSpec((tm, w), lambda i: (i, og_col)), pl.BlockSpec((1, w), lambda i: (0, 0))],
        out_specs=row, out_shape=_sds((l, w), BF),
        compiler_params=_params(("arbitrary",)),
    )(o_sum, og_arr, gn)


def _headgate_bwd(name, o_sum, og_arr, og_col, gn, dout, dcol, dv):
    l, w = o_sum.shape
    tm = _tile(l, 512, SUBLANE)
    nh = w // dv

    def body(o_ref, og_ref, gn_ref, d_ref, do_ref, dog_ref, dgn_ref):
        i = pl.program_id(0)
        for h in range(nh):
            cs = slice(h * dv, (h + 1) * dv)
            o = o_ref[:, cs].astype(F32)
            r = lax.rsqrt(jnp.mean(o * o, axis=-1, keepdims=True) + EPS)
            oh = o * r
            og = og_ref[:, cs].astype(F32)
            s = _sigmoid(og)
            d = d_ref[:, cs].astype(F32)
            gnv = gn_ref[:, cs]
            d_on = d * (og * s)
            dog_ref[:, cs] = (d * (oh * gnv) * s * (1.0 + og * (1.0 - s))).astype(dog_ref.dtype)
            doh = d_on * gnv
            do_ref[:, cs] = (r * (doh - oh * jnp.mean(doh * oh, axis=-1, keepdims=True))).astype(do_ref.dtype)
            part = jnp.sum(d_on * oh, axis=0, keepdims=True)

            @pl.when(i == 0)
            def _():
                dgn_ref[:, cs] = part

            @pl.when(i > 0)
            def _():
                dgn_ref[:, cs] += part

    row = pl.BlockSpec((tm, w), lambda i: (i, 0))
    vec = pl.BlockSpec((1, w), lambda i: (0, 0))
    return pl.pallas_call(
        body, name=name, grid=(l // tm,),
        in_specs=[row, pl.BlockSpec((tm, w), lambda i: (i, og_col)), vec, pl.BlockSpec((tm, w), lambda i: (i, dcol))],
        out_specs=[row, row, vec], out_shape=[_sds((l, w), BF), _sds((l, w), BF), _sds((1, w), F32)],
        compiler_params=_params(("arbitrary",)),
    )(o_sum, og_arr, gn, dout)


def _rot_tables(l, dk):
    half = dk // 2
    pos = jnp.arange(l, dtype=F32)
    inv = jnp.exp(-math.log(ROPE_BASE) * jnp.arange(half, dtype=F32) / half)
    ang = pos[:, None] * inv[None, :]
    cos, sin = jnp.cos(ang), jnp.sin(ang)
    return jnp.concatenate([cos, cos], axis=-1), jnp.concatenate([-sin, sin], axis=-1)


def _rot_apply(name, src_q, qcol, src_k, kcol, cos_t, sin_t, heads, dk, kscale, out_dtype, transpose):
    l = src_q.shape[0]
    w = heads * dk
    tm = _tile(l, 512, SUBLANE)

    def rot(t, cos_v, sin_v):
        if transpose:
            return t * cos_v + pltpu.roll(t * sin_v, dk // 2, 1)
        return t * cos_v + pltpu.roll(t, dk // 2, 1) * sin_v

    def body(q_ref, k_ref, c_ref, s_ref, qo_ref, ko_ref):
        cos_v, sin_v = c_ref[...], s_ref[...]
        for h in range(heads):
            cs = slice(h * dk, (h + 1) * dk)
            qo_ref[:, cs] = rot(q_ref[:, cs].astype(F32), cos_v, sin_v).astype(out_dtype)
            ko_ref[:, cs] = (rot(k_ref[:, cs].astype(F32), cos_v, sin_v) * kscale).astype(out_dtype)

    tab = pl.BlockSpec((tm, dk), lambda i: (i, 0))
    row = pl.BlockSpec((tm, w), lambda i: (i, 0))
    return pl.pallas_call(
        body, name=name, grid=(l // tm,),
        in_specs=[pl.BlockSpec((tm, w), lambda i: (i, qcol)), pl.BlockSpec((tm, w), lambda i: (i, kcol)), tab, tab],
        out_specs=[row, row], out_shape=[_sds((l, w), out_dtype), _sds((l, w), out_dtype)],
        compiler_params=_params(("arbitrary",)),
    )(src_q, src_k, cos_t, sin_t)


def _peer_copies(src_ref, out_ref, send_sems, recv_sems, gather):
    x, y, c = lax.axis_index("x"), lax.axis_index("y"), lax.axis_index("c")
    me = 4 * x + 2 * y + c
    copies = []
    for kk in range(1, N_DEV):
        px = (1 - x) if kk & 4 else x
        py = (1 - y) if kk & 2 else y
        pc = (1 - c) if kk & 1 else c
        peer = 4 * px + 2 * py + pc
        copies.append(pltpu.make_async_remote_copy(
            src_ref=src_ref if gather else src_ref.at[peer], dst_ref=out_ref.at[me],
            send_sem=send_sems.at[kk - 1], recv_sem=recv_sems.at[kk - 1],
            device_id=(px, py, pc), device_id_type=pl.DeviceIdType.MESH))
    return copies


_HBM = pl.BlockSpec(memory_space=pltpu.HBM)
_SEM = pl.BlockSpec(memory_space=pltpu.SEMAPHORE)
_EFFECT = pltpu.SideEffectType.DATAFLOW_SIDE_EFFECTING


def _exchange_start(name, srcs, gather):
    n = len(srcs)
    lands = [lax.empty((N_DEV,) + tuple(s.shape if gather else s.shape[1:]), s.dtype) for s in srcs]

    def body(*refs):
        src_refs, land_refs = refs[:n], refs[n:2 * n]
        send, recv = refs[2 * n:3 * n], refs[3 * n:4 * n]
        token = refs[-1]
        for k in range(n):
            for cp in _peer_copies(src_refs[k], land_refs[k], send[k], recv[k], gather):
                cp.start()
        token[...] = jnp.zeros_like(token)

    sem = pltpu.SemaphoreType.DMA((N_DEV - 1,))
    outs = pl.pallas_call(
        body, name=name,
        out_shape=tuple([sem] * (2 * n) + [pltpu.HBM(s.shape, s.dtype) for s in srcs] + [pltpu.HBM(a.shape, a.dtype) for a in lands]
                        + [_sds((SUBLANE, LANE), F32)]),
        in_specs=tuple([_HBM] * (2 * n)), out_specs=tuple([_SEM] * (2 * n) + [_HBM] * (2 * n) + [pl.BlockSpec(memory_space=pltpu.VMEM)]),
        input_output_aliases={k: 2 * n + k for k in range(2 * n)},
        compiler_params=pltpu.CompilerParams(has_side_effects=_EFFECT),
    )(*[pltpu.with_memory_space_constraint(a, pltpu.HBM) for a in list(srcs) + lands])
    return [(outs[k], outs[n + k], outs[2 * n + k], outs[3 * n + k], outs[-1]) for k in range(n)]


def _exchange_wait(name, started, gather, after):
    send_sems, recv_sems, src_thru, land_thru, _ = started

    def body(src_ref, land_ref, send_sems, recv_sems, after_ref, src_out, land_out):
        copies = _peer_copies(src_ref, land_ref, send_sems, recv_sems, gather)
        for cp in copies:
            cp.wait_send()
        for cp in copies:
            cp.wait_recv()

    return pl.pallas_call(
        body, name=name,
        out_shape=(pltpu.HBM(src_thru.shape, src_thru.dtype), pltpu.HBM(land_thru.shape, land_thru.dtype)),
        in_specs=(_HBM, _HBM, _SEM, _SEM, pl.BlockSpec(memory_space=pl.ANY)), out_specs=(_HBM, _HBM),
        input_output_aliases={0: 0, 1: 1},
        compiler_params=pltpu.CompilerParams(has_side_effects=_EFFECT),
    )(src_thru, land_thru, send_sems, recv_sems, after)


def _adam_math(w, gsum, m, v):
    m2 = ADAM_B1 * m + (1.0 - ADAM_B1) * gsum
    v2 = ADAM_B2 * v + (1.0 - ADAM_B2) * (gsum * gsum)
    m_hat = m2 / (1.0 - ADAM_B1 ** ADAM_STEP)
    v_hat = v2 / (1.0 - ADAM_B2 ** ADAM_STEP)
    delta = -ADAM_LR * (m_hat / (jnp.sqrt(v_hat) + ADAM_EPS) + ADAM_WD * w)
    return delta, m2, v2


def _reduce_adam(name, parts, w, m, v):
    nl, r, c = w.shape
    tr = _tile(r, 256, 16)
    nr = r // tr

    def body(*refs):
        p_refs = refs[:nl]
        w_ref, m_ref, v_ref, g_ref, d_ref, m2_ref, v2_ref = refs[nl:]
        for li in range(nl):
            @pl.when(pl.program_id(0) == li)
            def _(p_ref=p_refs[li]):
                gsum = p_ref[0].astype(F32)
                for s in range(1, N_DEV):
                    gsum = gsum + p_ref[s].astype(F32)
                g_ref[...] = gsum
                delta, m2, v2 = _adam_math(w_ref[...], gsum, m_ref[...], v_ref[...])
                d_ref[...] = delta
                m2_ref[...] = m2
                v2_ref[...] = v2

    def part_spec(li):
        return pl.BlockSpec((N_DEV, tr, c), lambda lay, i: (0, jnp.where(lay == li, i, jnp.where(lay < li, 0, nr - 1)), 0))

    row = pl.BlockSpec((None, tr, c), lambda lay, i: (lay, i, 0))
    return pl.pallas_call(
        body, name=name, grid=(nl, nr),
        in_specs=[part_spec(li) for li in range(nl)] + [row, row, row],
        out_specs=[row, row, row, row], out_shape=[_sds((nl, r, c), F32)] * 4,
        compiler_params=_params(("arbitrary", "arbitrary")),
    )(*parts, w, m, v)


def _reduce8(name, parts):
    _, r, c = parts.shape

    def body(p_ref, g_ref):
        gsum = p_ref[0]
        for s in range(1, N_DEV):
            gsum = gsum + p_ref[s]
        g_ref[...] = gsum

    return pl.pallas_call(
        body, name=name, grid=(1,),
        in_specs=[pl.BlockSpec((N_DEV, r, c), lambda i: (0, 0, 0))],
        out_specs=pl.BlockSpec((r, c), lambda i: (0, 0)), out_shape=_sds((r, c), F32),
        compiler_params=_params(("arbitrary",)),
    )(parts)


def _adam_packed(name, w, g, m, v):
    r, c = w.shape

    def body(w_ref, g_ref, m_ref, v_ref, d_ref, m2_ref, v2_ref):
        delta, m2, v2 = _adam_math(w_ref[...], g_ref[...], m_ref[...], v_ref[...])
        d_ref[...] = delta
        m2_ref[...] = m2
        v2_ref[...] = v2

    spec = pl.BlockSpec((r, c), lambda i: (0, 0))
    return pl.pallas_call(
        body, name=name, grid=(1,), in_specs=[spec] * 4, out_specs=[spec] * 3, out_shape=[_sds((r, c), F32)] * 3,
        compiler_params=_params(("arbitrary",)),
    )(w, g, m, v)


def _pack(arrs):
    flat = jnp.concatenate([a.reshape(-1).astype(F32) for a in arrs])
    n = flat.shape[0]
    pad = (-n) % (SUBLANE * LANE)
    return jnp.pad(flat, (0, pad)).reshape(-1, LANE)


def _unpack(packed, like):
    flat = packed.reshape(-1)
    out, off = [], 0
    for a in like:
        n = math.prod(a.shape)
        out.append(flat[off:off + n].reshape(a.shape))
        off += n
    return out


def _row_blocks(full):
    return full.reshape(N_DEV, full.shape[0] // N_DEV, full.shape[1])


def _col_blocks(full):
    r, c = full.shape
    return full.reshape(r, N_DEV, c // N_DEV).transpose(1, 0, 2)


def kernel(x, ffn1_norm, ffn1_w1, ffn1_w2, mix_norm, ffn2_norm, ffn2_w1, ffn2_w2, ab_w_in, s5_lambda_re, s5_lambda_im, s5_b_re, s5_b_im, s5_c_re, s5_c_im, s5_log_dt, s5_d, s5_w_glu, gla_w_gk, gla_b_gk, gla_norm, ab_w_out, ret_w_in, ret_norm, ret_w_out, final_norm, loss_target, m_ffn1_norm, m_ffn1_w1, m_ffn1_w2, m_mix_norm, m_ffn2_norm, m_ffn2_w1, m_ffn2_w2, m_ab_w_in, m_s5_lambda_re, m_s5_lambda_im, m_s5_b_re, m_s5_b_im, m_s5_c_re, m_s5_c_im, m_s5_log_dt, m_s5_d, m_s5_w_glu, m_gla_w_gk, m_gla_b_gk, m_gla_norm, m_ab_w_out, m_ret_w_in, m_ret_norm, m_ret_w_out, m_final_norm, v_ffn1_norm, v_ffn1_w1, v_ffn1_w2, v_mix_norm, v_ffn2_norm, v_ffn2_w1, v_ffn2_w2, v_ab_w_in, v_s5_lambda_re, v_s5_lambda_im, v_s5_b_re, v_s5_b_im, v_s5_c_re, v_s5_c_im, v_s5_log_dt, v_s5_d, v_s5_w_glu, v_gla_w_gk, v_gla_b_gk, v_gla_norm, v_ab_w_out, v_ret_w_in, v_ret_norm, v_ret_w_out, v_final_norm):
    names = ['ffn1_norm', 'ffn1_w1', 'ffn1_w2', 'mix_norm', 'ffn2_norm', 'ffn2_w1', 'ffn2_w2', 'ab_w_in', 's5_lambda_re', 's5_lambda_im', 's5_b_re', 's5_b_im', 's5_c_re', 's5_c_im', 's5_log_dt', 's5_d', 's5_w_glu', 'gla_w_gk', 'gla_b_gk', 'gla_norm', 'ab_w_out', 'ret_w_in', 'ret_norm', 'ret_w_out', 'final_norm']
    loc = locals()
    W = {n: loc[n] for n in names}
    M = {n: loc["m_" + n] for n in names}
    V = {n: loc["v_" + n] for n in names}

    me = 4 * lax.axis_index("x") + 2 * lax.axis_index("y") + lax.axis_index("c")
    xs = x[0]
    tgt = loss_target[0]
    l, d = xs.shape
    depth = ffn1_norm.shape[0]

    pending, to_start = {}, []

    def start_gather(tag, shard):
        to_start.append((tag, shard))

    def finish_gather(tag, after):
        started, shard = pending.pop(tag)
        _, got = _exchange_wait("agw_" + tag, started, True, after)
        return lax.dynamic_update_index_in_dim(got, shard, me, 0)

    def finish_cols(tag, after):
        g = finish_gather(tag, after)
        return g.transpose(1, 0, 2).reshape(g.shape[1], -1)

    def finish_rows(tag, after):
        g = finish_gather(tag, after)
        return g.reshape(-1, g.shape[2])

    small_sharded = [gla_w_gk, gla_b_gk, ret_norm]
    for i in range(depth):
        j = i // 2
        start_gather(f"ffn1_w1_{i}", ffn1_w1[i].astype(BF))
        start_gather(f"ffn1_w2_{i}", ffn1_w2[i].astype(BF))
        if i % 2 == 0:
            start_gather(f"ab_w_in_{j}", ab_w_in[j].astype(BF))
            if i == 0:
                start_gather("small", _pack(small_sharded))
            start_gather(f"s5_w_glu_{j}", s5_w_glu[j].astype(BF))
            start_gather(f"ab_w_out_{j}", ab_w_out[j].astype(BF))
        else:
            start_gather(f"ret_w_in_{j}", ret_w_in[j].astype(BF))
            start_gather(f"ret_w_out_{j}", ret_w_out[j].astype(BF))
        start_gather(f"ffn2_w1_{i}", ffn2_w1[i].astype(BF))
        start_gather(f"ffn2_w2_{i}", ffn2_w2[i].astype(BF))
    for (tag, shard), started in zip(to_start, _exchange_start("ags_weights", [s_ for _, s_ in to_start], True)):
        pending[tag] = (started, shard)
    started_all = started[4][0, 0]
    full = {}

    s5w = s5_d.shape[1]
    g_s5, n_s5 = s5_lambda_re.shape[2], s5_lambda_re.shape[3]
    hs = min(SUBLANE * LANE, g_s5 * n_s5)
    gla_hk = gla_w_gk.shape[-1] * N_DEV
    gla_dk = gla_hk // GLA_HEADS
    gla_hv = gla_norm.shape[1]
    gla_dv = gla_hv // GLA_HEADS
    ret_hv = ret_norm.shape[1] * N_DEV
    ret_dv = ret_hv // RET_HEADS
    ret_hk = (ret_w_in.shape[2] * N_DEV - 2 * ret_hv) // 2
    ret_dk = ret_hk // RET_HEADS
    assert s5w == gla_hv and 2 * gla_hk == s5w, "column blocks of the mixer projection assume these widths"
    assert ret_hv == 2 * ret_hk
    main_w = s5w + 2 * gla_hk + 2 * gla_hv
    gla_tb = _tile(l, 256, GLA_CHUNK)
    ret_chunk = min(RET_CHUNK, l)

    cos_t, sin_t = _rot_tables(l, ret_dk)
    lg_f = jnp.log1p(-jnp.exp2(-5.0 - jnp.arange(RET_HEADS, dtype=F32)))
    lgtab_f = jnp.broadcast_to(lg_f[:, None, None], (RET_HEADS, 1, LANE))
    lgtab_b = jnp.broadcast_to(lg_f[::-1][:, None, None], (RET_HEADS, 1, LANE))
    s5_pre = {}
    for j in range((depth + 1) // 2):
        s5_args = (s5_lambda_re[j], s5_lambda_im[j], s5_b_re[j], s5_b_im[j], s5_c_re[j], s5_c_im[j], s5_log_dt[j])
        (a_tab, bd, cd), s5_vjp = jax.vjp(lambda *a: _s5_chunk_tables(*a, hs), *s5_args)
        s5_pre[j] = (a_tab, bd.astype(BF), cd.astype(BF), s5_vjp)
    tables_done = jnp.stack([cos_t[0, 0], sin_t[0, 0], lgtab_f[0, 0, 0], lgtab_b[0, 0, 0]]
                            + [t[0].reshape(-1)[0] + t[1].reshape(-1)[0].astype(F32) + t[2].reshape(-1)[0].astype(F32) for t in s5_pre.values()])

    saved = []
    cur = xs
    hn = _rms_fwd("l0_ffn1_norm", cur, ffn1_norm[0:1] + started_all)
    for i in range(depth):
        j = i // 2
        s = {}
        s['x0'] = cur

        def first_w1(after):
            if i == 0:
                after = jnp.concatenate([after[0, 0:1].astype(F32), tables_done])
            return finish_cols(f"ffn1_w1_{i}", after)

        cur, h, s['ffn1'], s['f1w1'], s['f1w2'] = _ffn_fwd(f"l{i}_ffn1", cur, hn, first_w1, lambda after: finish_rows(f"ffn1_w2_{i}", after),
                                                            mix_norm[i:i + 1])
        s['x1'] = cur
        s['h'] = h
        if i % 2 == 0:
            w_in = finish_cols(f"ab_w_in_{j}", cur)
            if i == 0:
                got = finish_gather("small", cur)
                flat, off, joined = got.reshape(N_DEV, -1), 0, []
                for a in small_sharded:
                    n = math.prod(a.shape)
                    blk = jnp.moveaxis(flat[:, off:off + n].reshape((N_DEV,) + a.shape), 0, -2)
                    joined.append(blk.reshape(a.shape[:-1] + (N_DEV * a.shape[-1],)))
                    off += n
                full['gla_w_gk'], full['gla_b_gk'], ret_norm_full = joined[0].astype(BF), joined[1], joined[2]
            s['w_glu'], s['w_out'] = finish_rows(f"s5_w_glu_{j}", cur), finish_rows(f"ab_w_out_{j}", cur)
            w_main, w_glo = w_in[:, :main_w], w_in[:, main_w:]
            proj = _mm_plain(f"l{i}_proj", h, w_main, NN, BF, tm=1024, tn=1024, tk=d, b_outer=True)
            glo = _mm_plain(f"l{i}_glo", h, w_glo, NN, F32, tm=1024, tn=2 * GLA_RANK, tk=d)
            a_tab, bd16, cd16, s5_vjp = s5_pre[j]
            tm = _tile(l, 512, SUBLANE)
            x_f, y_f = _s5_fwd(f"l{i}_s5_fwd_f", proj, bd16[0], cd16[0], a_tab[0], False)
            x_b, y_b = _s5_fwd(f"l{i}_s5_fwd_b", proj, bd16[1], cd16[1], a_tab[1], True)
            d_row = s5_d[j:j + 1]
            y, s5_out = _glu_fwd(f"l{i}_s5_glu", y_f, y_b, proj, d_row, s['w_glu'])
            zeros_r = jnp.zeros((GLA_RANK, gla_hk), BF)
            w_gk = full['gla_w_gk'][j]
            wgk_f = jnp.concatenate([w_gk[0], zeros_r], axis=0)
            wgk_b = jnp.concatenate([zeros_r, w_gk[1]], axis=0)
            b_gk = full['gla_b_gk'][j]
            g_f, g_b = _gate_fwd(f"l{i}_gla_gate", glo, wgk_f, wgk_b, b_gk[0:1], b_gk[1:2])
            qcol, kcol, vcol, ogcol = s5w // gla_hk, s5w // gla_hk + 1, (s5w + 2 * gla_hk) // gla_hv, (s5w + 2 * gla_hk) // gla_hv + 1
            lin_kw = dict(heads=GLA_HEADS, dk=gla_dk, dv=gla_dv, chunk=GLA_CHUNK, tb=gla_tb, qcol=qcol, kcol=kcol, vcol=vcol,
                          qscale=gla_dk ** -0.5)
            o_f, sp_f = _gla_fwd(f"l{i}_gla_fwd_f", proj, g_f, reverse=False, **lin_kw)
            o_b, sp_b = _gla_fwd(f"l{i}_gla_fwd_b", proj, g_b, o_f, reverse=True, **lin_kw)
            gla_out = _headgate_fwd(f"l{i}_gla_out", o_b, proj, ogcol, gla_norm[j:j + 1], gla_dv)
            w_out = s['w_out']

            row = pl.BlockSpec((tm, d), lambda ii, jj, kk: (ii, 0))
            vec = pl.BlockSpec((1, d), lambda ii, jj, kk: (0, 0))
            cur, hn = _mm(f"l{i}_mix_out",
                          [(s5_out, pl.BlockSpec((tm, s5w), lambda ii, jj, kk: (ii, 0)), w_out, pl.BlockSpec((s5w, d), lambda ii, jj, kk: (0, 0))),
                           (gla_out, pl.BlockSpec((tm, gla_hv), lambda ii, jj, kk: (ii, 0)), w_out, pl.BlockSpec((gla_hv, d), lambda ii, jj, kk: (1, 0)))],
                          NN, (l // tm, 1, 1), [_sds((l, d), F32), _sds((l, d), BF)], [row, row], (tm, d), _residual_epi(1.0, True),
                          [cur, ffn2_norm[i:i + 1]], [row, vec])
            s.update(proj=proj, glo=glo, s5_vjp=s5_vjp, a_tab=a_tab, bd16=bd16, cd16=cd16, x_f=x_f, x_b=x_b, y=y, s5_out=s5_out,
                     wgk_f=wgk_f, wgk_b=wgk_b, b_gk=b_gk, g_f=g_f, g_b=g_b, o_f=o_f, o_b=o_b, sp_f=sp_f, sp_b=sp_b, gla_out=gla_out,
                     w_main=w_main, w_glo=w_glo, lin_kw=lin_kw, ogcol=ogcol)
        else:
            w_in = finish_cols(f"ret_w_in_{j}", cur)
            s['w_in'], s['w_out'] = w_in, finish_rows(f"ret_w_out_{j}", cur)
            proj = _mm_plain(f"l{i}_proj", h, w_in, NN, BF, tm=1024, tn=1024, tk=d, b_outer=True)
            qr, kr = _rot_apply(f"l{i}_rot", proj, 0, proj, 1, cos_t, sin_t, RET_HEADS, ret_dk, ret_dk ** -0.5, BF, False)
            ret_hb = 4
            lin_kw = dict(heads=RET_HEADS, hb=ret_hb, dk=ret_dk, dv=ret_dv, chunk=ret_chunk, tb=_tile(l, 4 * ret_chunk, ret_chunk), qcol=0, kcol=0,
                          vcol=(2 * ret_hk) // (ret_hb * ret_dv))
            o_f, sp_f = _lin_fwd(f"l{i}_ret_fwd_f", qr, kr, proj, lgtab_f, reverse=False, **lin_kw)
            o_b, sp_b = _lin_fwd(f"l{i}_ret_fwd_b", qr, kr, proj, lgtab_b, o_f, reverse=True, **lin_kw)
            ogcol = (2 * ret_hk + ret_hv) // ret_hv
            r_out = _headgate_fwd(f"l{i}_ret_out", o_b, proj, ogcol, ret_norm_full, ret_dv)

            tm = _tile(l, 512, SUBLANE)
            cur, hn = _mm_plain(f"l{i}_mix_out", r_out, s['w_out'], NN, F32, tm=512, tn=d, tk=ret_hv, epi=_residual_epi(1.0, True),
                                eins=[cur, ffn2_norm[i:i + 1]], especs=[None, pl.BlockSpec((1, d), lambda ii, jj, kk: (0, 0))],
                                extra_outs=[_sds((l, d), BF)], extra_specs=[pl.BlockSpec((tm, d), lambda ii, jj, kk: (ii, 0))])
            s.update(proj=proj, qr=qr, kr=kr, o_f=o_f, o_b=o_b, sp_f=sp_f, sp_b=sp_b, r_out=r_out, lin_kw=lin_kw, ogcol=ogcol)
        s['x2'] = cur
        cur, hn, s['ffn2'], s['f2w1'], s['f2w2'] = _ffn_fwd(f"l{i}_ffn2", cur, hn, lambda after: finish_cols(f"ffn2_w1_{i}", after),
                                                             lambda after: finish_rows(f"ffn2_w2_{i}", after),
                                                             ffn1_norm[i + 1:i + 2] if i + 1 < depth else None)
        saved.append(s)

    dx, d_final_norm, loss_row = _loss_head("loss_head", cur, final_norm.reshape(1, -1), tgt)
    loss = lax.psum(loss_row[0, 0], ("x", "y", "c"))

    G = {}
    big = {}
    G['final_norm'] = d_final_norm.reshape(-1)
    per_layer = {n: [None] * depth for n in ['ffn1_norm', 'mix_norm', 'ffn2_norm']}
    small_late = ['ffn1_norm', 'mix_norm']
    small_early = ['ffn2_norm', 's5_lambda_re', 's5_lambda_im', 's5_b_re', 's5_b_im', 's5_c_re', 's5_c_im',
                   's5_log_dt', 's5_d', 'gla_w_gk', 'gla_b_gk', 'gla_norm', 'ret_norm', 'final_norm']
    a2a, tok = {}, [jnp.zeros((), F32)]

    def start_a2a(*tagged):
        for (tag, _), started in zip(tagged, _exchange_start("a2as_" + tagged[0][0], [b for _, b in tagged], False)):
            a2a[tag] = started
        tok[0] = tok[0] + started[4][0, 0]

    def dep(vec):
        return vec + tok[0]

    def proj_backward(tag, pieces, s, dres, a2a_tag):
        tm = _tile(l, 512, SUBLANE)
        row = pl.BlockSpec((tm, d), lambda ii, jj, kk: (ii, 0))
        vec = pl.BlockSpec((1, d), lambda ii, jj, kk: (0, 0))
        dws = [_mm_plain(f"{tag}_dwin_{k}", s['h'], piece, TN, BF, tm=d, tn=2048, tk=1024) for k, (piece, _, _) in enumerate(pieces)]
        start_a2a((a2a_tag, _col_blocks(jnp.concatenate(dws, axis=1))))
        pairs = []
        for piece, w, col in pieces:
            wd = piece.shape[1]
            pairs.append((piece, pl.BlockSpec((tm, wd), lambda ii, jj, kk: (ii, 0)),
                          w, pl.BlockSpec((d, wd), lambda ii, jj, kk, col=col: (0, col), pipeline_mode=pl.Buffered(1))))
        return _mm(f"{tag}_dh", pairs, NT, (l // tm, 1, 1), [_sds((l, d), F32), _sds((1, d), F32)], [row, vec], (tm, d),
                   _rms_bwd_epi(0), [s['x1'], dep(mix_norm[i:i + 1]), dres], [row, vec, row])

    for i in reversed(range(depth)):
        j = i // 2
        s = saved[i]
        def ffn_grads(which):
            def on_grads(dw1, dw2, gnorm):
                start_a2a((f"{which}_w1_{i}", _col_blocks(dw1)), (f"{which}_w2_{i}", _row_blocks(dw2)))
                return dep(gnorm)
            return on_grads

        dx, dg = _ffn_bwd(f"l{i}_ffn2b", dx, s['x2'], ffn2_norm[i:i + 1], s['f2w1'], s['f2w2'], s['ffn2'], ffn_grads("ffn2"))
        per_layer['ffn2_norm'][i] = dg[0]
        tm = _tile(l, 512, SUBLANE)
        row = pl.BlockSpec((tm, d), lambda ii, jj, kk: (ii, 0))
        vec = pl.BlockSpec((1, d), lambda ii, jj, kk: (0, 0))
        if i % 2 == 0:
            proj, lin_kw = s['proj'], s['lin_kw']
            w_out = s['w_out']
            d_cat = _mm_plain(f"l{i}_dcat", dx, w_out, NT, BF, tm=512, tn=1024, tk=d)
            dwo_a = _mm_plain(f"l{i}_dwout_a", s['s5_out'], dx, TN, BF, tm=s5w, tn=d, tk=512)
            dwo_b = _mm_plain(f"l{i}_dwout_b", s['gla_out'], dx, TN, BF, tm=gla_hv, tn=d, tk=512)
            start_a2a((f"ab_w_out_{j}", _row_blocks(jnp.concatenate([dwo_a, dwo_b], axis=0))))
            do, dog, dgn = _headgate_bwd(f"l{i}_gla_outb", s['o_b'], proj, s['ogcol'], dep(gla_norm[j:j + 1]), d_cat, 1, gla_dv)
            G['gla_norm'] = dgn
            dq, dk_, dv_, dgf = _gla_bwd(f"l{i}_gla_bwd_f", proj, s['g_f'], s['sp_f'], do, None, reverse=False, **lin_kw)
            dq, dk_, dv_, dgb = _gla_bwd(f"l{i}_gla_bwd_b", proj, s['g_b'], s['sp_b'], do, (dq, dk_, dv_), reverse=True, **lin_kw)
            dglo, dwf, dwb, dbf, dbb = _gate_bwd(f"l{i}_gla_gateb", s['glo'], s['wgk_f'], s['wgk_b'], s['b_gk'][0:1], s['b_gk'][1:2], dgf, dgb)
            G['gla_w_gk'] = jnp.stack([dwf[:GLA_RANK], dwb[GLA_RANK:]], axis=0)[None]
            G['gla_b_gk'] = jnp.concatenate([dbf, dbb], axis=0)[None]
            dy, dwglu = _glu_bwd(f"l{i}_s5_glub", s['y'], s['w_glu'], d_cat, 0)
            start_a2a((f"s5_w_glu_{j}", _row_blocks(dwglu.astype(BF))))
            cd16, bd16, a_tab = s['cd16'], s['bd16'], s['a_tab']
            nt2 = a_tab.shape[2]
            a_conj = a_tab * jnp.where(jnp.arange(nt2) < nt2 // 2, 1.0, -1.0)[None, None, :, None]
            du_f, dbd_f, dcd_f, da_f = _s5_bwd(f"l{i}_s5_bwd_f", proj, dy, s['x_f'], bd16[0], cd16[0], a_conj[0], True)
            du_b, dbd_b, dcd_b, da_b = _s5_bwd(f"l{i}_s5_bwd_b", proj, dy, s['x_b'], bd16[1], cd16[1], a_conj[1], False)
            du, dd = _s5_du(f"l{i}_s5_du", du_f, du_b, dy, proj, s5_d[j:j + 1])
            G['s5_d'] = dd
            cot = (jnp.stack([da_f, da_b]), jnp.stack([dbd_f, dbd_b]), jnp.stack([dcd_f, dcd_b]))
            g_lre, g_lim, g_bre, g_bim, g_cre, g_cim, g_ldt = s['s5_vjp'](cot)
            G['s5_lambda_re'], G['s5_lambda_im'], G['s5_b_re'], G['s5_b_im'] = g_lre[None], g_lim[None], g_bre[None], g_bim[None]
            G['s5_c_re'], G['s5_c_im'], G['s5_log_dt'] = g_cre[None], g_cim[None], g_ldt[None]
            if i == 0:
                G['ffn2_norm'] = jnp.stack(per_layer['ffn2_norm'], axis=0)
                early_packed = _pack([G[n] for n in small_early])
                early_started = _exchange_start("ags_small_grads_early", [early_packed], True)[0]
                tok[0] = tok[0] + early_started[4][0, 0]
            w_main, w_glo = s['w_main'], s['w_glo']
            pieces = [(du, w_main, 0), (dq, w_main, s5w // gla_hk), (dk_, w_main, s5w // gla_hk + 1),
                      (dv_, w_main, (s5w + 2 * gla_hk) // gla_hv), (dog, w_main, (s5w + 2 * gla_hk) // gla_hv + 1), (dglo, w_glo, 0)]
            dx, dg = proj_backward(f"l{i}", pieces, s, dx, f"ab_w_in_{j}")
        else:
            proj, lin_kw = s['proj'], s['lin_kw']
            w_out = s['w_out']
            d_ro = _mm_plain(f"l{i}_dro", dx, w_out, NT, BF, tm=1024, tn=1024, tk=d, b_outer=True)
            dwo = _mm_plain(f"l{i}_dwout", s['r_out'], dx, TN, BF, tm=2048, tn=d, tk=1024)
            start_a2a((f"ret_w_out_{j}", _row_blocks(dwo)))
            do, dog, dgn = _headgate_bwd(f"l{i}_ret_outb", s['o_b'], proj, s['ogcol'], dep(ret_norm_full), d_ro, 0, ret_dv)
            G['ret_norm'] = dgn
            r1 = _lin_bwd(f"l{i}_ret_bwd_f", s['qr'], s['kr'], proj, lgtab_f, s['sp_f'], do, None, reverse=False, **lin_kw)
            r2 = _lin_bwd(f"l{i}_ret_bwd_b", s['qr'], s['kr'], proj, lgtab_b, s['sp_b'], do, r1, reverse=True, **lin_kw)
            dqr, dkr, dv_ = r2
            dq, dk_ = _rot_apply(f"l{i}_rotb", dqr, 0, dkr, 0, cos_t, sin_t, RET_HEADS, ret_dk, ret_dk ** -0.5, BF, True)
            w_in = s['w_in']
            pieces = [(dq, w_in, 0), (dk_, w_in, 1), (dv_, w_in, (2 * ret_hk) // ret_hv), (dog, w_in, (2 * ret_hk) // ret_hv + 1)]
            dx, dg = proj_backward(f"l{i}", pieces, s, dx, f"ret_w_in_{j}")
        per_layer['mix_norm'][i] = dg[0]
        dx, dg = _ffn_bwd(f"l{i}_ffn1b", dx, s['x0'], ffn1_norm[i:i + 1], s['f1w1'], s['f1w2'], s['ffn1'], ffn_grads("ffn1"))
        per_layer['ffn1_norm'][i] = dg[0]
    for n in small_late:
        G[n] = jnp.stack(per_layer[n], axis=0)
    grad_x = dx[None]

    out_g, out_d, out_m, out_v = {}, {}, {}, {}
    small = small_early + small_late
    packed = _pack([G[n] for n in small_late])
    small_started = _exchange_start("ags_small_grads_late", [packed], True)[0]

    def big_update(n, layers):
        parts = []
        for i in layers:
            blocks, got = _exchange_wait(f"a2aw_{n}_{i}", a2a.pop(f"{n}_{i}"), False, small_started[4])
            parts.append(lax.dynamic_update_index_in_dim(got, lax.dynamic_index_in_dim(blocks, me, 0, keepdims=False), me, 0))
        out_g[n], out_d[n], out_m[n], out_v[n] = _reduce_adam("upd_" + n, parts, W[n], M[n], V[n])

    for n in ['ffn2_w1', 'ffn2_w2', 'ffn1_w1', 'ffn1_w2']:
        big_update(n, range(depth))
    for n in ['ab_w_in', 's5_w_glu', 'ab_w_out', 'ret_w_in', 'ret_w_out']:
        big_update(n, [0])
    assert not a2a and not pending

    g_full = {}
    for tag, started, mine, group in (("early", early_started, early_packed, small_early), ("late", small_started, packed, small_late)):
        _, gathered = _exchange_wait("agw_small_grads_" + tag, started, True, out_v['ret_w_out'])
        gathered = lax.dynamic_update_index_in_dim(gathered, mine, me, 0)
        summed = _reduce8("sum_small_grads_" + tag, gathered)
        g_full.update(zip(group, _unpack(summed, [G[n] for n in group])))
    g_small = {}
    for n in small:
        gf = g_full[n]
        if n in ('gla_w_gk', 'gla_b_gk', 'ret_norm'):
            width = W[n].shape[-1]
            gf = lax.dynamic_slice_in_dim(gf, me * width, width, axis=gf.ndim - 1)
        g_small[n] = gf.reshape(W[n].shape)
    pw, pg, pm, pv = (_pack([src[n] for n in small]) for src in (W, g_small, M, V))
    pd, pm2, pv2 = _adam_packed("upd_small", pw, pg, pm, pv)
    like = [W[n] for n in small]
    for n, dd_, mm_, vv_ in zip(small, _unpack(pd, like), _unpack(pm2, like), _unpack(pv2, like)):
        out_g[n], out_d[n], out_m[n], out_v[n] = g_small[n], dd_, mm_, vv_

    return (loss, grad_x, *[out_g[n] for n in names], *[out_d[n] for n in names], *[out_m[n] for n in names], *[out_v[n] for n in names])
```

```python
import math

import jax
import jax.numpy as jnp
from jax import lax
from jax.experimental import pallas as pl
from jax.experimental.pallas import tpu as pltpu

F32 = jnp.float32
BF = jnp.bfloat16
N_DEV = 8
EPS = 1e-6
S5_GROUP = 16
GLA_HEADS = 4
GLA_RANK = 16
GLA_GATE_NORM = 16.0
RET_HEADS = 8
ROPE_BASE = 10000.0
GLA_CHUNK = 64
RET_CHUNK = 256
ADAM_LR, ADAM_B1, ADAM_B2, ADAM_EPS, ADAM_WD, ADAM_STEP = 0.001, 0.9, 0.999, 1e-08, 0.01, 10
VMEM_LIMIT_BYTES = 56 * 1024 * 1024
LANE = 128
SUBLANE = 8

NN = (((1,), (0,)), ((), ()))
NT = (((1,), (1,)), ((), ()))
TN = (((0,), (0,)), ((), ()))


def _tile(n, pref, align):
    if n <= pref:
        return n
    t = (pref // align) * align
    while t >= align:
        if n % t == 0:
            return t
        t -= align
    return n


def _params(sem):
    return pltpu.CompilerParams(dimension_semantics=sem, vmem_limit_bytes=VMEM_LIMIT_BYTES)


def _dot(a, b, dims=NN):
    return lax.dot_general(a.astype(BF), b.astype(BF), dims, preferred_element_type=F32)


def _dot3(m01, g, dims=NN):
    g1 = g.astype(BF)
    r1 = g - g1.astype(F32)
    g2 = r1.astype(BF)
    g3 = (r1 - g2.astype(F32)).astype(BF)
    m = m01.astype(BF)
    return (lax.dot_general(m, g1, dims, preferred_element_type=F32)
            + lax.dot_general(m, g2, dims, preferred_element_type=F32)
            + lax.dot_general(m, g3, dims, preferred_element_type=F32))


def _sigmoid(x):
    return 0.5 + 0.5 * jnp.tanh(0.5 * x)


def _mm(name, pairs, dims, grid, outs, out_specs, acc_shape, epi=None, eins=(), especs=()):
    n_p, n_e, n_o = len(pairs), len(eins), len(outs)
    nk = grid[2]

    def body(*refs):
        a_refs = refs[0:2 * n_p:2]
        b_refs = refs[1:2 * n_p:2]
        e_refs = refs[2 * n_p:2 * n_p + n_e]
        o_refs = refs[2 * n_p + n_e:2 * n_p + n_e + n_o]
        acc = refs[-1]
        ids = (pl.program_id(0), pl.program_id(1), pl.program_id(2))

        part = _dot(a_refs[0][...], b_refs[0][...], dims)
        for p in range(1, n_p):
            part = part + _dot(a_refs[p][...], b_refs[p][...], dims)

        def finish(total):
            if epi is None:
                o_refs[0][...] = total.astype(o_refs[0].dtype)
            else:
                epi(total, e_refs, o_refs, ids)

        if nk == 1:
            finish(part)
        else:
            @pl.when(ids[2] == 0)
            def _():
                acc[...] = part

            @pl.when(ids[2] > 0)
            def _():
                acc[...] += part

            @pl.when(ids[2] == nk - 1)
            def _():
                finish(acc[...])

    in_specs, args = [], []
    for a, a_spec, b, b_spec in pairs:
        in_specs += [a_spec, b_spec]
        args += [a, b]
    in_specs += list(especs)
    args += list(eins)
    res = pl.pallas_call(
        body, name=name, grid=grid, in_specs=in_specs, out_specs=list(out_specs), out_shape=list(outs),
        scratch_shapes=[pltpu.VMEM(acc_shape, F32)],
        compiler_params=_params(("arbitrary", "arbitrary", "arbitrary")),
    )(*args)
    return res


def _sds(shape, dtype):
    return jax.ShapeDtypeStruct(shape, dtype)


def _mm_plain(name, a, b, dims, out_dtype, tm=512, tn=1024, tk=1024, epi=None, eins=(), especs=None, extra_outs=(), extra_specs=(),
              b_outer=False):
    if dims == NN:
        (m, k), n = a.shape, b.shape[1]
    elif dims == NT:
        (m, k), n = a.shape, b.shape[0]
    else:
        (k, m), n = a.shape, b.shape[1]
    tm, tn = _tile(m, tm, LANE if dims == TN else SUBLANE), _tile(n, tn, LANE)
    tk = _tile(k, tk, SUBLANE if dims == TN else LANE)
    grid = (n // tn, m // tm, k // tk) if b_outer else (m // tm, n // tn, k // tk)

    def spec(block, index):
        if b_outer:
            return pl.BlockSpec(block, lambda j, i, kk: index(i, j, kk))
        return pl.BlockSpec(block, index)

    if dims == NN:
        a_spec = spec((tm, tk), lambda i, j, kk: (i, kk))
        b_spec = spec((tk, tn), lambda i, j, kk: (kk, j))
    elif dims == NT:
        a_spec = spec((tm, tk), lambda i, j, kk: (i, kk))
        b_spec = spec((tn, tk), lambda i, j, kk: (j, kk))
    else:
        a_spec = spec((tk, tm), lambda i, j, kk: (kk, i))
        b_spec = spec((tk, tn), lambda i, j, kk: (kk, j))
    o_spec = spec((tm, tn), lambda i, j, kk: (i, j))
    if especs is None:
        especs = [o_spec] * len(eins)
    else:
        especs = [o_spec if s is None else s for s in especs]
    res = _mm(name, [(a, a_spec, b, b_spec)], dims, grid, [_sds((m, n), out_dtype)] + list(extra_outs),
              [o_spec] + list(extra_specs), (tm, tn), epi, eins, especs)
    return res if extra_outs else res[0]


def _rms_fwd(name, x, g):
    l, d = x.shape
    tm = _tile(l, 1024, SUBLANE)

    def body(x_ref, g_ref, o_ref):
        xv = x_ref[...]
        r = lax.rsqrt(jnp.mean(xv * xv, axis=-1, keepdims=True) + EPS)
        o_ref[...] = (xv * r * g_ref[...]).astype(o_ref.dtype)

    return pl.pallas_call(
        body, name=name, grid=(l // tm,),
        in_specs=[pl.BlockSpec((tm, d), lambda i: (i, 0)), pl.BlockSpec((1, d), lambda i: (0, 0))],
        out_specs=pl.BlockSpec((tm, d), lambda i: (i, 0)), out_shape=_sds((l, d), BF),
        compiler_params=_params(("arbitrary",)),
    )(x, g)


def _rms_bwd_epi(first_axis):
    def epi(acc, e_refs, o_refs, ids):
        x_ref, g_ref, dr_ref = e_refs
        dx_ref, dg_ref = o_refs
        xv = x_ref[...]
        r = lax.rsqrt(jnp.mean(xv * xv, axis=-1, keepdims=True) + EPS)
        xh = xv * r
        dxh = acc * g_ref[...]
        dx_ref[...] = dr_ref[...] + r * (dxh - xh * jnp.mean(dxh * xh, axis=-1, keepdims=True))
        part = jnp.sum(acc * xh, axis=0, keepdims=True)

        @pl.when(ids[first_axis] == 0)
        def _():
            dg_ref[...] = part

        @pl.when(ids[first_axis] > 0)
        def _():
            dg_ref[...] += part

    return epi


def _loss_head(name, x, g, target):
    l, d = x.shape
    tm = _tile(l, 512, SUBLANE)
    n = l // tm

    def body(x_ref, g_ref, t_ref, dx_ref, dg_ref, loss_ref, lacc):
        i = pl.program_id(0)
        xv = x_ref[...]
        r = lax.rsqrt(jnp.mean(xv * xv, axis=-1, keepdims=True) + EPS)
        xh = xv * r
        e = xh * g_ref[...] - t_ref[...]
        dy = e * (1.0 / d)
        dxh = dy * g_ref[...]
        dx_ref[...] = r * (dxh - xh * jnp.mean(dxh * xh, axis=-1, keepdims=True))
        dg_part = jnp.sum(dy * xh, axis=0, keepdims=True)
        l_part = jnp.sum(e * e, axis=0, keepdims=True)

        @pl.when(i == 0)
        def _():
            dg_ref[...] = dg_part
            lacc[...] = l_part

        @pl.when(i > 0)
        def _():
            dg_ref[...] += dg_part
            lacc[...] += l_part

        @pl.when(i == n - 1)
        def _():
            loss_ref[...] = jnp.zeros_like(loss_ref) + jnp.sum(lacc[...]) * (0.5 / d)

    return pl.pallas_call(
        body, name=name, grid=(n,),
        in_specs=[pl.BlockSpec((tm, d), lambda i: (i, 0)), pl.BlockSpec((1, d), lambda i: (0, 0)),
                  pl.BlockSpec((tm, d), lambda i: (i, 0))],
        out_specs=[pl.BlockSpec((tm, d), lambda i: (i, 0)), pl.BlockSpec((1, d), lambda i: (0, 0)),
                   pl.BlockSpec((1, LANE), lambda i: (0, 0))],
        out_shape=[_sds((l, d), F32), _sds((1, d), F32), _sds((1, LANE), F32)],
        scratch_shapes=[pltpu.VMEM((1, d), F32)],
        compiler_params=_params(("arbitrary",)),
    )(x, g, target)


def _ffn_up(name, hn, w1):
    l, d = hn.shape
    f = w1.shape[1] // 2
    tm, tn = _tile(l, 512, SUBLANE), _tile(f, 1408, LANE)
    nj = f // tn

    def body(h_ref, wg_ref, wu_ref, gu_ref, a_ref):
        h = h_ref[...]
        g = jnp.dot(h, wg_ref[...], preferred_element_type=F32)
        u = jnp.dot(h, wu_ref[...], preferred_element_type=F32)
        s = _sigmoid(g)
        gs = g * s
        gu_ref[0] = (u * (s + gs * (1.0 - s))).astype(BF)
        gu_ref[1] = gs.astype(BF)
        a_ref[...] = (gs * u).astype(BF)

    return pl.pallas_call(
        body, name=name, grid=(nj, l // tm),
        in_specs=[pl.BlockSpec((tm, d), lambda j, i: (i, 0)), pl.BlockSpec((d, tn), lambda j, i: (0, j)),
                  pl.BlockSpec((d, tn), lambda j, i: (0, j + nj))],
        out_specs=[pl.BlockSpec((2, tm, tn), lambda j, i: (0, i, j)), pl.BlockSpec((tm, tn), lambda j, i: (i, j))],
        out_shape=[_sds((2, l, f), BF), _sds((l, f), BF)],
        compiler_params=_params(("arbitrary", "arbitrary")),
    )(hn, w1, w1)


def _residual_epi(scale, with_norm):
    def epi(acc, e_refs, o_refs, ids):
        xn = e_refs[0][...] + scale * acc
        o_refs[0][...] = xn
        if with_norm:
            r = lax.rsqrt(jnp.mean(xn * xn, axis=-1, keepdims=True) + EPS)
            o_refs[1][...] = (xn * r * e_refs[1][...]).astype(o_refs[1].dtype)

    return epi


def _ffn_fwd(tag, x, hn, get_w1, get_w2, next_gnorm):
    w1 = get_w1(hn)
    gu, a = _ffn_up(tag + "_up", hn, w1)
    w2 = get_w2(a)
    l, d = x.shape
    if next_gnorm is None:
        x_new = _mm_plain(tag + "_down", a, w2, NN, F32, tm=512, tn=d, tk=w2.shape[0], epi=_residual_epi(0.5, False), eins=[x])
        hn_next = None
    else:
        vec = pl.BlockSpec((1, d), lambda i, j, kk: (0, 0))
        tm = _tile(l, 512, SUBLANE)
        x_new, hn_next = _mm_plain(tag + "_down", a, w2, NN, F32, tm=512, tn=d, tk=w2.shape[0], epi=_residual_epi(0.5, True),
                                   eins=[x, next_gnorm], especs=[None, vec],
                                   extra_outs=[_sds((l, d), BF)], extra_specs=[pl.BlockSpec((tm, d), lambda i, j, kk: (i, 0))])
    return x_new, hn_next, (hn, gu, a), w1, w2


def _ffn_bwd(tag, dres, x, gnorm, w1, w2, saved, on_grads):
    hn, gu, a = saved
    l, d = x.shape
    f = w2.shape[0]
    tm, tn = _tile(l, 512, SUBLANE), _tile(f, 1408, LANE)
    nj = f // tn

    def epi_gu(acc, e_refs, o_refs, ids):
        da = 0.5 * acc
        o_refs[0][0] = (da * e_refs[0][0].astype(F32)).astype(BF)
        o_refs[0][1] = (da * e_refs[0][1].astype(F32)).astype(BF)

    gu_spec = pl.BlockSpec((2, tm, tn), lambda j, i, kk: (0, i, j))
    dgu = _mm(tag + "_dgu",
              [(dres, pl.BlockSpec((tm, d), lambda j, i, kk: (i, 0)), w2, pl.BlockSpec((tn, d), lambda j, i, kk: (j, 0)))],
              NT, (nj, l // tm, 1), [_sds((2, l, f), BF)], [gu_spec], (tm, tn), epi_gu, [gu], [gu_spec])[0]

    def epi_half(acc, e_refs, o_refs, ids):
        o_refs[0][...] = (0.5 * acc).astype(BF)

    dw2 = _mm_plain(tag + "_dw2", a, dres, TN, BF, tm=1408, tn=d, tk=2048, epi=epi_half)

    tk = _tile(l, 2048, SUBLANE)
    dw1 = _mm(tag + "_dw1",
              [(hn, pl.BlockSpec((tk, d), lambda i, j, kk: (kk, 0)), dgu, pl.BlockSpec((None, tk, tn), lambda i, j, kk: (j // nj, kk, j % nj)))],
              TN, (1, 2 * nj, l // tk), [_sds((d, 2 * f), BF)], [pl.BlockSpec((d, tn), lambda i, j, kk: (0, j))], (d, tn))[0]

    gnorm = on_grads(dw1, dw2, gnorm)
    row = pl.BlockSpec((tm, d), lambda i, j, kk: (i, 0))
    vec = pl.BlockSpec((1, d), lambda i, j, kk: (0, 0))
    once = pl.Buffered(1)
    dx, dg = _mm(tag + "_dhn",
                 [(dgu, pl.BlockSpec((None, tm, f), lambda i, j, kk: (0, i, 0)), w1, pl.BlockSpec((d, f), lambda i, j, kk: (0, 0), pipeline_mode=once)),
                  (dgu, pl.BlockSpec((None, tm, f), lambda i, j, kk: (1, i, 0)), w1, pl.BlockSpec((d, f), lambda i, j, kk: (0, 1), pipeline_mode=once))],
                 NT, (l // tm, 1, 1), [_sds((l, d), F32), _sds((1, d), F32)], [row, vec], (tm, d),
                 _rms_bwd_epi(0), [x, gnorm, dres], [row, vec, row])
    return dx, dg


def _s5_chunk_tables(lam_re, lam_im, b_re, b_im, c_re, c_im, log_dt, hs):
    f32 = F32
    g, n = lam_re.shape[1], lam_re.shape[2]
    p = b_re.shape[-1]
    nch, gpc, nt = (g * n) // hs, hs // n, hs // LANE
    lr = jnp.minimum(lam_re.astype(f32), -1e-4)
    li = lam_im.astype(f32)
    dt = jnp.exp(log_dt.astype(f32))[..., None]
    mag = jnp.exp(lr * dt)
    ar = mag * jnp.cos(li * dt)
    ai = mag * jnp.sin(li * dt)
    den = lr * lr + li * li
    cr = ((ar - 1.0) * lr + ai * li) / den
    ci = (ai * lr - (ar - 1.0) * li) / den
    bbr = cr[..., None] * b_re - ci[..., None] * b_im
    bbi = cr[..., None] * b_im + ci[..., None] * b_re
    a_f = jnp.stack([ar, ai], axis=1).reshape(2, 2, nch, nt, LANE).transpose(0, 2, 1, 3, 4).reshape(2, nch, 2 * nt, LANE)
    rows_g = jnp.arange(gpc * p) // p
    cols_g = (jnp.arange(2 * hs) % hs) // n
    diag = (rows_g[:, None] == cols_g[None, :]).astype(f32)
    bb = jnp.stack([bbr, bbi], axis=1).reshape(2, 2, nch, hs, p)
    bd = jnp.tile(bb.transpose(0, 2, 4, 1, 3).reshape(2, nch, p, 2 * hs), (1, 1, gpc, 1)) * diag
    cc = jnp.stack([c_re, -c_im], axis=1).reshape(2, 2, nch, gpc, p, n)
    cd = jnp.tile(cc.transpose(0, 2, 4, 1, 3, 5).reshape(2, nch, p, 2 * hs), (1, 1, gpc, 1)) * diag
    return a_f, bd, cd


def _fold_store(ref, val, tb, ntiles):
    for s in range(ntiles):
        ref[:, s * SUBLANE:(s + 1) * SUBLANE, :] = val[:, s * LANE:(s + 1) * LANE].reshape(tb // SUBLANE, SUBLANE, LANE)


def _unfold(ref, tb, ntiles):
    return jnp.concatenate([ref[:, s * SUBLANE:(s + 1) * SUBLANE, :].reshape(tb, LANE) for s in range(ntiles)], axis=1)


def _s5_fwd(name, proj, bd, cd, a_f, reverse):
    l = proj.shape[0]
    nch, cu, hs2 = bd.shape
    nt = hs2 // (2 * LANE)
    frows = 2 * nt * SUBLANE
    tb = _tile(l, 512, SUBLANE)
    nb = l // tb

    def body(u_ref, bd_ref, cd_ref, a_ref, xf_ref, y_ref, st):
        r = pl.program_id(1)

        @pl.when(r == 0)
        def _():
            st[...] = jnp.zeros_like(st)

        _fold_store(xf_ref, _dot(u_ref[...], bd_ref[...]), tb, 2 * nt)
        ar, ai = a_ref[0:nt, :], a_ref[nt:2 * nt, :]

        def group(gi, carry):
            rr = (tb // SUBLANE - 1 - gi) if reverse else gi
            sr, si = carry
            for qq in range(SUBLANE):
                q = (SUBLANE - 1 - qq) if reverse else qq
                re_rows, im_rows = pl.ds(q, nt, stride=SUBLANE), pl.ds(nt * SUBLANE + q, nt, stride=SUBLANE)
                nr = ar * sr - ai * si + xf_ref[rr, re_rows, :]
                ni = ar * si + ai * sr + xf_ref[rr, im_rows, :]
                xf_ref[rr, re_rows, :] = nr
                xf_ref[rr, im_rows, :] = ni
                sr, si = nr, ni
            return sr, si

        fin = lax.fori_loop(0, tb // SUBLANE, group, (st[0:nt, :], st[nt:2 * nt, :]))
        st[0:nt, :] = fin[0]
        st[nt:2 * nt, :] = fin[1]
        y_ref[...] = _dot(_unfold(xf_ref, tb, 2 * nt), cd_ref[...], NT)

    def rows(r):
        return (nb - 1 - r) if reverse else r

    return pl.pallas_call(
        body, name=name, grid=(nch, nb),
        in_specs=[pl.BlockSpec((tb, cu), lambda c, r: (rows(r), c)), pl.BlockSpec((None, cu, hs2), lambda c, r: (c, 0, 0)),
                  pl.BlockSpec((None, cu, hs2), lambda c, r: (c, 0, 0)), pl.BlockSpec((None, 2 * nt, LANE), lambda c, r: (c, 0, 0))],
        out_specs=[pl.BlockSpec((tb // SUBLANE, frows, LANE), lambda c, r: (rows(r), c, 0)), pl.BlockSpec((tb, cu), lambda c, r: (rows(r), c))],
        out_shape=[_sds((l // SUBLANE, nch * frows, LANE), F32), _sds((l, nch * cu), F32)],
        scratch_shapes=[pltpu.VMEM((2 * nt, LANE), F32)],
        compiler_params=_params(("arbitrary", "arbitrary")),
    )(proj, bd, cd, a_f)


def _s5_bwd(name, proj, dy, xf, bd, cd, a_conj, reverse):
    l = proj.shape[0]
    nch, cu, hs2 = bd.shape
    nt = hs2 // (2 * LANE)
    frows = 2 * nt * SUBLANE
    tb = _tile(l, 512, SUBLANE)
    nb = l // tb

    def body(u_ref, dy_ref, xs_ref, bd_ref, cd_ref, a_ref, du_ref, dbd_ref, dcd_ref, da_ref, lam, st):
        r = pl.program_id(1)

        @pl.when(r == 0)
        def _():
            st[...] = jnp.zeros_like(st)
            dbd_ref[...] = jnp.zeros_like(dbd_ref)
            dcd_ref[...] = jnp.zeros_like(dcd_ref)
            da_ref[...] = jnp.zeros_like(da_ref)

        dyv = dy_ref[...]
        _fold_store(lam, _dot(dyv, cd_ref[...]), tb, 2 * nt)
        ar, ai = a_ref[0:nt, :], a_ref[nt:2 * nt, :]

        def group(gi, carry):
            rr = (tb // SUBLANE - 1 - gi) if reverse else gi
            sr, si, cr, ci = carry
            for qq in range(SUBLANE):
                q = (SUBLANE - 1 - qq) if reverse else qq
                re_rows, im_rows = pl.ds(q, nt, stride=SUBLANE), pl.ds(nt * SUBLANE + q, nt, stride=SUBLANE)
                xr, xi = xs_ref[rr, re_rows, :], xs_ref[rr, im_rows, :]
                cr = cr + sr * xr + si * xi
                ci = ci + si * xr - sr * xi
                nr = ar * sr - ai * si + lam[rr, re_rows, :]
                ni = ar * si + ai * sr + lam[rr, im_rows, :]
                lam[rr, re_rows, :] = nr
                lam[rr, im_rows, :] = ni
                sr, si = nr, ni
            return sr, si, cr, ci

        zero = jnp.zeros((nt, LANE), F32)
        fin = lax.fori_loop(0, tb // SUBLANE, group, (st[0:nt, :], st[nt:2 * nt, :], zero, zero))
        st[0:nt, :] = fin[0]
        st[nt:2 * nt, :] = fin[1]
        da_ref[0:nt, :] += fin[2]
        da_ref[nt:2 * nt, :] += fin[3]
        lam_u = _unfold(lam, tb, 2 * nt)
        du_ref[...] = _dot(lam_u, bd_ref[...], NT)
        dbd_ref[...] += _dot(u_ref[...], lam_u, TN)
        dcd_ref[...] += _dot(dyv, _unfold(xs_ref, tb, 2 * nt), TN)

    def rows(r):
        return (nb - 1 - r) if reverse else r

    chunk_rows = pl.BlockSpec((tb, cu), lambda c, r: (rows(r), c))
    bd_spec = pl.BlockSpec((None, cu, hs2), lambda c, r: (c, 0, 0))
    cd_spec = bd_spec
    a_spec = pl.BlockSpec((None, 2 * nt, LANE), lambda c, r: (c, 0, 0))
    return pl.pallas_call(
        body, name=name, grid=(nch, nb),
        in_specs=[chunk_rows, chunk_rows, pl.BlockSpec((tb // SUBLANE, frows, LANE), lambda c, r: (rows(r), c, 0)), bd_spec, cd_spec, a_spec],
        out_specs=[chunk_rows, bd_spec, cd_spec, a_spec],
        out_shape=[_sds((l, nch * cu), F32), _sds((nch, cu, hs2), F32), _sds((nch, cu, hs2), F32), _sds((nch, 2 * nt, LANE), F32)],
        scratch_shapes=[pltpu.VMEM((tb // SUBLANE, frows, LANE), F32), pltpu.VMEM((2 * nt, LANE), F32)],
        compiler_params=_params(("arbitrary", "arbitrary")),
    )(proj, dy, xf, bd, cd, a_conj)


def _s5_du(name, du_f, du_b, dy, proj, d_row):
    l, w = dy.shape
    tm = _tile(l, 1024, SUBLANE)

    def body(f_ref, b_ref, dy_ref, u_ref, d_ref, du_ref, dd_ref):
        i = pl.program_id(0)
        dyv = dy_ref[...]
        du_ref[...] = (f_ref[...] + b_ref[...] + dyv * d_ref[...]).astype(du_ref.dtype)
        part = jnp.sum(dyv * u_ref[...].astype(F32), axis=0, keepdims=True)

        @pl.when(i == 0)
        def _():
            dd_ref[...] = part

        @pl.when(i > 0)
        def _():
            dd_ref[...] += part

    row = pl.BlockSpec((tm, w), lambda i: (i, 0))
    vec = pl.BlockSpec((1, w), lambda i: (0, 0))
    return pl.pallas_call(
        body, name=name, grid=(l // tm,), in_specs=[row, row, row, row, vec], out_specs=[row, vec],
        out_shape=[_sds((l, w), BF), _sds((1, w), F32)],
        compiler_params=_params(("arbitrary",)),
    )(du_f, du_b, dy, proj, d_row)


def _gelu(y):
    c = math.sqrt(2.0 / math.pi)
    return 0.5 * y * (1.0 + jnp.tanh(c * (y + 0.044715 * y * y * y)))


def _gelu_grad(y):
    c = math.sqrt(2.0 / math.pi)
    th = jnp.tanh(c * (y + 0.044715 * y * y * y))
    return 0.5 * (1.0 + th) + 0.5 * y * (1.0 - th * th) * c * (1.0 + 3.0 * 0.044715 * y * y)


def _glu_fwd(name, y_f, y_b, proj, d_row, w):
    l, wd = y_f.shape
    tm = _tile(l, 512, SUBLANE)

    def body(yf_ref, yb_ref, u_ref, d_ref, w_ref, y_ref, o_ref):
        y = yf_ref[...] + yb_ref[...] + u_ref[...].astype(F32) * d_ref[...]
        y_ref[...] = y
        gy = _gelu(y)
        z = _dot(gy, w_ref[...])
        o_ref[...] = (gy * _sigmoid(z)).astype(o_ref.dtype)

    row = pl.BlockSpec((tm, wd), lambda i: (i, 0))
    return pl.pallas_call(
        body, name=name, grid=(l // tm,),
        in_specs=[row, row, row, pl.BlockSpec((1, wd), lambda i: (0, 0)), pl.BlockSpec((wd, wd), lambda i: (0, 0))],
        out_specs=[row, row], out_shape=[_sds((l, wd), F32), _sds((l, wd), BF)],
        compiler_params=_params(("arbitrary",)),
    )(y_f, y_b, proj, d_row, w)


def _glu_bwd(name, y, w, dout, dcol):
    l, wd = y.shape
    tm = _tile(l, 512, SUBLANE)

    def body(y_ref, w_ref, d_ref, dy_ref, dw_ref):
        i = pl.program_id(0)
        yv = y_ref[...]
        gy = _gelu(yv)
        s = _sigmoid(_dot(gy, w_ref[...]))
        d = d_ref[...].astype(F32)
        t = d * gy * s * (1.0 - s)
        dgy = d * s + _dot(t, w_ref[...], NT)
        dy_ref[...] = dgy * _gelu_grad(yv)
        part = _dot(gy, t, TN)

        @pl.when(i == 0)
        def _():
            dw_ref[...] = part

        @pl.when(i > 0)
        def _():
            dw_ref[...] += part

    return pl.pallas_call(
        body, name=name, grid=(l // tm,),
        in_specs=[pl.BlockSpec((tm, wd), lambda i: (i, 0)), pl.BlockSpec((wd, wd), lambda i: (0, 0)),
                  pl.BlockSpec((tm, wd), lambda i: (i, dcol))],
        out_specs=[pl.BlockSpec((tm, wd), lambda i: (i, 0)), pl.BlockSpec((wd, wd), lambda i: (0, 0))],
        out_shape=[_sds((l, wd), F32), _sds((wd, wd), F32)],
        compiler_params=_params(("arbitrary",)),
    )(y, w, dout)


def _log_sigmoid(x):
    return jnp.minimum(x, 0.0) - jnp.log(1.0 + jnp.exp(-jnp.abs(x)))


def _gate_fwd(name, glo, wf, wb, bf, bb):
    l, r2 = glo.shape
    hk = wf.shape[1]
    tm = _tile(l, 1024, SUBLANE)

    def body(x_ref, wf_ref, wb_ref, bf_ref, bb_ref, gf_ref, gb_ref):
        xv = x_ref[...]
        gf_ref[...] = _log_sigmoid(_dot(xv, wf_ref[...]) + bf_ref[...]) * (1.0 / GLA_GATE_NORM)
        gb_ref[...] = _log_sigmoid(_dot(xv, wb_ref[...]) + bb_ref[...]) * (1.0 / GLA_GATE_NORM)

    w_spec = pl.BlockSpec((r2, hk), lambda i: (0, 0))
    b_spec = pl.BlockSpec((1, hk), lambda i: (0, 0))
    o_spec = pl.BlockSpec((tm, hk), lambda i: (i, 0))
    return pl.pallas_call(
        body, name=name, grid=(l // tm,),
        in_specs=[pl.BlockSpec((tm, r2), lambda i: (i, 0)), w_spec, w_spec, b_spec, b_spec],
        out_specs=[o_spec, o_spec], out_shape=[_sds((l, hk), F32), _sds((l, hk), F32)],
        compiler_params=_params(("arbitrary",)),
    )(glo, wf, wb, bf, bb)


def _gate_bwd(name, glo, wf, wb, bf, bb, dgf, dgb):
    l, r2 = glo.shape
    hk = wf.shape[1]
    tm = _tile(l, 1024, SUBLANE)

    def body(x_ref, wf_ref, wb_ref, bf_ref, bb_ref, dgf_ref, dgb_ref, dx_ref, dwf_ref, dwb_ref, dbf_ref, dbb_ref):
        i = pl.program_id(0)
        xv = x_ref[...]
        kf = _dot(xv, wf_ref[...]) + bf_ref[...]
        kb = _dot(xv, wb_ref[...]) + bb_ref[...]
        dkf = dgf_ref[...] * (1.0 / GLA_GATE_NORM) * _sigmoid(-kf)
        dkb = dgb_ref[...] * (1.0 / GLA_GATE_NORM) * _sigmoid(-kb)
        dx_ref[...] = _dot(dkf, wf_ref[...], NT) + _dot(dkb, wb_ref[...], NT)
        parts = (_dot(xv, dkf, TN), _dot(xv, dkb, TN), jnp.sum(dkf, axis=0, keepdims=True), jnp.sum(dkb, axis=0, keepdims=True))
        accs = (dwf_ref, dwb_ref, dbf_ref, dbb_ref)

        @pl.when(i == 0)
        def _():
            for a_, p_ in zip(accs, parts):
                a_[...] = p_

        @pl.when(i > 0)
        def _():
            for a_, p_ in zip(accs, parts):
                a_[...] += p_

    w_spec = pl.BlockSpec((r2, hk), lambda i: (0, 0))
    b_spec = pl.BlockSpec((1, hk), lambda i: (0, 0))
    g_spec = pl.BlockSpec((tm, hk), lambda i: (i, 0))
    x_spec = pl.BlockSpec((tm, r2), lambda i: (i, 0))
    return pl.pallas_call(
        body, name=name, grid=(l // tm,),
        in_specs=[x_spec, w_spec, w_spec, b_spec, b_spec, g_spec, g_spec],
        out_specs=[x_spec, w_spec, w_spec, b_spec, b_spec],
        out_shape=[_sds((l, r2), F32), _sds((r2, hk), F32), _sds((r2, hk), F32), _sds((1, hk), F32), _sds((1, hk), F32)],
        compiler_params=_params(("arbitrary",)),
    )(glo, wf, wb, bf, bb, dgf, dgb)


def _chunk_terms(qc, kc, lg, chunk, reverse):
    ri = lax.broadcasted_iota(jnp.int32, (chunk, chunk), 0)
    ci = lax.broadcasted_iota(jnp.int32, (chunk, chunk), 1)
    mask = (ci > ri) if reverse else (ci <= ri)
    pos = lax.broadcasted_iota(jnp.int32, (chunk, 1), 0).astype(F32)
    cum = ((chunk - pos) if reverse else (pos + 1.0)) * lg
    last = chunk * lg
    e = jnp.exp(cum)
    einv = jnp.exp(-cum)
    dec = jnp.exp(last - cum)
    return e, einv, dec, qc * e, kc * einv, kc * dec, jnp.exp(last), mask


def _lin_specs(width, col, tb, nb, reverse):
    return pl.BlockSpec((tb, width), lambda h, r: ((nb - 1 - r) if reverse else r, col + h))


def _lin_fwd(name, q, k, v, lgtab, prev_o=None, *, heads, hb, dk, dv, chunk, tb, qcol, kcol, vcol, reverse):
    l = q.shape[0]
    nb, ncb, ng = l // tb, tb // chunk, heads // hb

    def body(q_ref, k_ref, v_ref, lg_ref, *rest):
        p_ref = rest[0] if prev_o is not None else None
        o_ref, sp_ref, st = rest[-3:]

        @pl.when(pl.program_id(1) == 0)
        def _():
            st[...] = jnp.zeros_like(st)

        for c in range(ncb):
            cc = (ncb - 1 - c) if reverse else c
            rows = pl.ds(cc * chunk, chunk)
            for h in range(hb):
                ks, vs = slice(h * dk, (h + 1) * dk), slice(h * dv, (h + 1) * dv)
                qc, kc, vc = q_ref[rows, ks].astype(F32), k_ref[rows, ks].astype(F32), v_ref[rows, vs]
                _, _, _, qd, ki, kdec, e_last, mask = _chunk_terms(qc, kc, lg_ref[h, :, 0:1], chunk, reverse)
                a = jnp.where(mask, _dot(qd, ki, NT), 0.0)
                s_t = st[h]
                oc = _dot(a, vc) + _dot(qd, s_t, NT)
                if p_ref is not None:
                    oc = oc + p_ref[rows, vs]
                o_ref[rows, vs] = oc.astype(o_ref.dtype)
                sp_ref[cc, h] = s_t
                st[h] = s_t * e_last + _dot(vc, kdec, TN)

    o_spec = _lin_specs(hb * dv, 0, tb, nb, reverse)
    extra = [] if prev_o is None else [prev_o]
    return pl.pallas_call(
        body, name=name, grid=(ng, nb),
        in_specs=[_lin_specs(hb * dk, qcol, tb, nb, reverse), _lin_specs(hb * dk, kcol, tb, nb, reverse),
                  _lin_specs(hb * dv, vcol, tb, nb, reverse), pl.BlockSpec((hb, 1, LANE), lambda h, r: (h, 0, 0))] + [o_spec] * len(extra),
        out_specs=[o_spec, pl.BlockSpec((ncb, hb, dv, dk), lambda h, r: ((nb - 1 - r) if reverse else r, h, 0, 0))],
        out_shape=[_sds((l, heads * dv), F32 if prev_o is None else BF), _sds((l // chunk, heads, dv, dk), F32)],
        scratch_shapes=[pltpu.VMEM((hb, dv, dk), F32)],
        compiler_params=_params(("arbitrary", "arbitrary")),
    )(q, k, v, lgtab, *extra)


def _lin_bwd(name, q, k, v, lgtab, sprev, do, prev, *, heads, hb, dk, dv, chunk, tb, qcol, kcol, vcol, reverse):
    l = q.shape[0]
    nb, ncb, ng = l // tb, tb // chunk, heads // hb
    brev = not reverse
    n_prev = 0 if prev is None else len(prev)

    def body(q_ref, k_ref, v_ref, lg_ref, sp_ref, do_ref, *rest):
        p_refs = rest[:n_prev]
        dq_ref, dk_ref, dv_ref, dst = rest[n_prev:]

        @pl.when(pl.program_id(1) == 0)
        def _():
            dst[...] = jnp.zeros_like(dst)

        for c in range(ncb):
            cc = (ncb - 1 - c) if brev else c
            rows = pl.ds(cc * chunk, chunk)
            for h in range(hb):
                ks, vs = slice(h * dk, (h + 1) * dk), slice(h * dv, (h + 1) * dv)
                qc, kc, vc = q_ref[rows, ks].astype(F32), k_ref[rows, ks].astype(F32), v_ref[rows, vs]
                e, einv, dec, qd, ki, kdec, e_last, mask = _chunk_terms(qc, kc, lg_ref[h, :, 0:1], chunk, reverse)
                a = jnp.where(mask, _dot(qd, ki, NT), 0.0)
                s_t, ds_t, doc = sp_ref[cc, h], dst[h], do_ref[rows, vs]
                dvc = _dot(a, doc, TN) + _dot(kdec, ds_t, NT)
                da = jnp.where(mask, _dot(doc, vc, NT), 0.0)
                dqc = (_dot(da, ki) + _dot(doc, s_t)) * e
                dkc = _dot(da, qd, TN) * einv + _dot(vc, ds_t) * dec
                dst[h] = ds_t * e_last + _dot(doc, qd, TN)
                if n_prev:
                    dqc = dqc + p_refs[0][rows, ks]
                    dkc = dkc + p_refs[1][rows, ks]
                    dvc = dvc + p_refs[2][rows, vs]
                dq_ref[rows, ks] = dqc
                dk_ref[rows, ks] = dkc
                dv_ref[rows, vs] = dvc.astype(dv_ref.dtype)

    k_spec, v_spec = _lin_specs(hb * dk, 0, tb, nb, brev), _lin_specs(hb * dv, 0, tb, nb, brev)
    in_specs = [_lin_specs(hb * dk, qcol, tb, nb, brev), _lin_specs(hb * dk, kcol, tb, nb, brev), _lin_specs(hb * dv, vcol, tb, nb, brev),
                pl.BlockSpec((hb, 1, LANE), lambda h, r: (h, 0, 0)),
                pl.BlockSpec((ncb, hb, dv, dk), lambda h, r: ((nb - 1 - r) if brev else r, h, 0, 0)), v_spec]
    args = [q, k, v, lgtab, sprev, do]
    if n_prev:
        in_specs += [k_spec, k_spec, v_spec]
        args += list(prev)
    return pl.pallas_call(
        body, name=name, grid=(ng, nb), in_specs=in_specs, out_specs=[k_spec, k_spec, v_spec],
        out_shape=[_sds((l, heads * dk), F32), _sds((l, heads * dk), F32), _sds((l, heads * dv), BF if n_prev else F32)],
        scratch_shapes=[pltpu.VMEM((hb, dv, dk), F32)],
        compiler_params=_params(("arbitrary", "arbitrary")),
    )(*args)


def _log2(n):
    assert n & (n - 1) == 0, "a power of two"
    return n.bit_length() - 1


def _gla_block_terms(q, k, g, qscale, chunk, tb, reverse):
    ri = lax.broadcasted_iota(jnp.int32, (tb, tb), 0)
    ci = lax.broadcasted_iota(jnp.int32, (tb, tb), 1)
    same = jnp.right_shift(ri, _log2(chunk)) == jnp.right_shift(ci, _log2(chunk))
    t_in = jnp.logical_and(same, (ci >= ri) if reverse else (ci <= ri)).astype(F32)
    cum = _dot3(t_in, g)
    tot = _dot3(same.astype(F32), g)
    e = jnp.exp(cum)
    einv = jnp.exp(-cum)
    dec = jnp.exp(tot - cum)
    return e, einv, dec, jnp.exp(tot), q * (qscale * e), k * einv, k * dec, t_in


def _gla_masks(hk, dk, heads, chunk, reverse):
    lane = lax.broadcasted_iota(jnp.int32, (1, hk), 1)
    head_of = jnp.right_shift(lane, _log2(dk))
    ri = lax.broadcasted_iota(jnp.int32, (chunk, chunk), 0)
    ci = lax.broadcasted_iota(jnp.int32, (chunk, chunk), 1)
    return [head_of == h for h in range(heads)], ((ci > ri) if reverse else (ci <= ri))


def _gla_fwd(name, proj, g, prev_o=None, *, heads, dk, dv, chunk, tb, qcol, kcol, vcol, qscale, reverse):
    l = proj.shape[0]
    nb, ncb, hk, hv = l // tb, tb // chunk, heads * dk, heads * dv

    def body(q_ref, k_ref, v_ref, g_ref, *rest):
        p_ref = rest[0] if prev_o is not None else None
        o_ref, sp_ref, st = rest[-3:]

        @pl.when(pl.program_id(0) == 0)
        def _():
            st[...] = jnp.zeros_like(st)

        _, _, _, etot, qd, ki, kdec, _ = _gla_block_terms(q_ref[...].astype(F32), k_ref[...].astype(F32), g_ref[...], qscale, chunk, tb, reverse)
        heads_m, causal = _gla_masks(hk, dk, heads, chunk, reverse)
        s_all = st[...]
        for c in range(ncb):
            cc = (ncb - 1 - c) if reverse else c
            rc = slice(cc * chunk, (cc + 1) * chunk)
            qd_c, ki_c, kdec_c = qd[rc], ki[rc], kdec[rc]
            sp_ref[cc] = s_all
            kv = jnp.zeros_like(s_all)
            for h in range(heads):
                vs = slice(h * dv, (h + 1) * dv)
                qm = jnp.where(heads_m[h], qd_c, 0.0)
                a = jnp.where(causal, _dot(qm, ki_c, NT), 0.0)
                vc = v_ref[rc, vs]
                oc = _dot(a, vc) + _dot(qm, s_all, NT)
                if p_ref is not None:
                    oc = oc + p_ref[rc, vs]
                o_ref[rc, vs] = oc.astype(o_ref.dtype)
                kv = kv + jnp.where(heads_m[h], _dot(vc, kdec_c, TN), 0.0)
            s_all = s_all * etot[rc][0:1, :] + kv
        st[...] = s_all

    def rows(r):
        return (nb - 1 - r) if reverse else r

    o_spec = pl.BlockSpec((tb, hv), lambda r: (rows(r), 0))
    extra = [] if prev_o is None else [prev_o]
    return pl.pallas_call(
        body, name=name, grid=(nb,),
        in_specs=[pl.BlockSpec((tb, hk), lambda r: (rows(r), qcol)), pl.BlockSpec((tb, hk), lambda r: (rows(r), kcol)),
                  pl.BlockSpec((tb, hv), lambda r: (rows(r), vcol)), pl.BlockSpec((tb, hk), lambda r: (rows(r), 0))] + [o_spec] * len(extra),
        out_specs=[o_spec, pl.BlockSpec((ncb, dv, hk), lambda r: (rows(r), 0, 0))],
        out_shape=[_sds((l, hv), F32 if prev_o is None else BF), _sds((l // chunk, dv, hk), F32)],
        scratch_shapes=[pltpu.VMEM((dv, hk), F32)],
        compiler_params=_params(("arbitrary",)),
    )(proj, proj, proj, g, *extra)


def _gla_bwd(name, proj, g, sprev, do, prev, *, heads, dk, dv, chunk, tb, qcol, kcol, vcol, qscale, reverse):
    l = proj.shape[0]
    nb, ncb, hk, hv = l // tb, tb // chunk, heads * dk, heads * dv
    brev = not reverse
    n_prev = 0 if prev is None else len(prev)

    def body(q_ref, k_ref, v_ref, g_ref, sp_ref, do_ref, *rest):
        p_refs = rest[:n_prev]
        dq_ref, dk_ref, dv_ref, dg_ref, dst, dcs = rest[n_prev:]

        @pl.when(pl.program_id(0) == 0)
        def _():
            dst[...] = jnp.zeros_like(dst)

        e, einv, dec, etot, qd, ki, kdec, t_in = _gla_block_terms(q_ref[...].astype(F32), k_ref[...].astype(F32), g_ref[...], qscale, chunk, tb, reverse)
        heads_m, causal = _gla_masks(hk, dk, heads, chunk, reverse)
        last_row = lax.broadcasted_iota(jnp.int32, (chunk, 1), 0) == (0 if reverse else chunk - 1)
        ds_all = dst[...]
        for c in range(ncb):
            cc = (ncb - 1 - c) if brev else c
            rc = slice(cc * chunk, (cc + 1) * chunk)
            qd_c, ki_c, kdec_c = qd[rc], ki[rc], kdec[rc]
            s_all = sp_ref[cc]
            et = etot[rc][0:1, :]
            dqd = jnp.zeros((chunk, hk), F32)
            dki = jnp.zeros((chunk, hk), F32)
            dkdec = jnp.zeros((chunk, hk), F32)
            ds_add = jnp.zeros_like(ds_all)
            for h in range(heads):
                vs = slice(h * dv, (h + 1) * dv)
                m = heads_m[h]
                qm = jnp.where(m, qd_c, 0.0)
                a = jnp.where(causal, _dot(qm, ki_c, NT), 0.0)
                doc, vc = do_ref[rc, vs], v_ref[rc, vs]
                dvc = _dot(a, doc, TN) + _dot(jnp.where(m, kdec_c, 0.0), ds_all, NT)
                if n_prev:
                    dvc = dvc + p_refs[2][rc, vs]
                dv_ref[rc, vs] = dvc.astype(dv_ref.dtype)
                da = jnp.where(causal, _dot(doc, vc, NT), 0.0)
                dqd = dqd + jnp.where(m, _dot(da, ki_c) + _dot(doc, s_all), 0.0)
                dki = dki + _dot(da, qm, TN)
                dkdec = dkdec + jnp.where(m, _dot(vc, ds_all), 0.0)
                ds_add = ds_add + _dot(doc, qm, TN)
            dqc = dqd * e[rc] * qscale
            dkc = dki * einv[rc] + dkdec * dec[rc]
            if n_prev:
                dqc = dqc + p_refs[0][rc, :]
                dkc = dkc + p_refs[1][rc, :]
            dq_ref[rc, :] = dqc
            dk_ref[rc, :] = dkc
            dlast = jnp.sum(dkdec * kdec_c, axis=0, keepdims=True) + et * jnp.sum(s_all * ds_all, axis=0, keepdims=True)
            dcs[rc, :] = dqd * qd_c - dki * ki_c - dkdec * kdec_c + jnp.where(last_row, dlast, 0.0)
            ds_all = ds_all * et + ds_add
        dst[...] = ds_all
        dg_ref[...] = _dot3(t_in, dcs[...], TN)

    def rows(r):
        return (nb - 1 - r) if brev else r

    k_spec = pl.BlockSpec((tb, hk), lambda r: (rows(r), 0))
    v_spec = pl.BlockSpec((tb, hv), lambda r: (rows(r), 0))
    in_specs = [pl.BlockSpec((tb, hk), lambda r: (rows(r), qcol)), pl.BlockSpec((tb, hk), lambda r: (rows(r), kcol)),
                pl.BlockSpec((tb, hv), lambda r: (rows(r), vcol)), k_spec,
                pl.BlockSpec((ncb, dv, hk), lambda r: (rows(r), 0, 0)), v_spec]
    args = [proj, proj, proj, g, sprev, do]
    if n_prev:
        in_specs += [k_spec, k_spec, v_spec]
        args += list(prev)
    return pl.pallas_call(
        body, name=name, grid=(nb,), in_specs=in_specs, out_specs=[k_spec, k_spec, v_spec, k_spec],
        out_shape=[_sds((l, hk), F32), _sds((l, hk), F32), _sds((l, hv), BF if n_prev else F32), _sds((l, hk), F32)],
        scratch_shapes=[pltpu.VMEM((dv, hk), F32), pltpu.VMEM((tb, hk), F32)],
        compiler_params=_params(("arbitrary",)),
    )(*args)


def _headgate_fwd(name, o_sum, og_arr, og_col, gn, dv):
    l, w = o_sum.shape
    tm = _tile(l, 512, SUBLANE)
    nh = w // dv

    def body(o_ref, og_ref, gn_ref, out_ref):
        for h in range(nh):
            cs = slice(h * dv, (h + 1) * dv)
            o = o_ref[:, cs].astype(F32)
            r = lax.rsqrt(jnp.mean(o * o, axis=-1, keepdims=True) + EPS)
            og = og_ref[:, cs].astype(F32)
            out_ref[:, cs] = (o * r * gn_ref[:, cs] * (og * _sigmoid(og))).astype(out_ref.dtype)

    row = pl.BlockSpec((tm, w), lambda i: (i, 0))
    return pl.pallas_call(
        body, name=name, grid=(l // tm,),
        in_specs=[row, pl.BlockSpec((tm, w), lambda i: (i, og_col)), pl.BlockSpec((1, w), lambda i: (0, 0))],
        out_specs=row, out_shape=_sds((l, w), BF),
        compiler_params=_params(("arbitrary",)),
    )(o_sum, og_arr, gn)


def _headgate_bwd(name, o_sum, og_arr, og_col, gn, dout, dcol, dv):
    l, w = o_sum.shape
    tm = _tile(l, 512, SUBLANE)
    nh = w // dv

    def body(o_ref, og_ref, gn_ref, d_ref, do_ref, dog_ref, dgn_ref):
        i = pl.program_id(0)
        for h in range(nh):
            cs = slice(h * dv, (h + 1) * dv)
            o = o_ref[:, cs].astype(F32)
            r = lax.rsqrt(jnp.mean(o * o, axis=-1, keepdims=True) + EPS)
            oh = o * r
            og = og_ref[:, cs].astype(F32)
            s = _sigmoid(og)
            d = d_ref[:, cs].astype(F32)
            gnv = gn_ref[:, cs]
            d_on = d * (og * s)
            dog_ref[:, cs] = (d * (oh * gnv) * s * (1.0 + og * (1.0 - s))).astype(dog_ref.dtype)
            doh = d_on * gnv
            do_ref[:, cs] = (r * (doh - oh * jnp.mean(doh * oh, axis=-1, keepdims=True))).astype(do_ref.dtype)
            part = jnp.sum(d_on * oh, axis=0, keepdims=True)

            @pl.when(i == 0)
            def _():
                dgn_ref[:, cs] = part

            @pl.when(i > 0)
            def _():
                dgn_ref[:, cs] += part

    row = pl.BlockSpec((tm, w), lambda i: (i, 0))
    vec = pl.BlockSpec((1, w), lambda i: (0, 0))
    return pl.pallas_call(
        body, name=name, grid=(l // tm,),
        in_specs=[row, pl.BlockSpec((tm, w), lambda i: (i, og_col)), vec, pl.BlockSpec((tm, w), lambda i: (i, dcol))],
        out_specs=[row, row, vec], out_shape=[_sds((l, w), BF), _sds((l, w), BF), _sds((1, w), F32)],
        compiler_params=_params(("arbitrary",)),
    )(o_sum, og_arr, gn, dout)


def _rot_tables(l, dk):
    half = dk // 2
    pos = jnp.arange(l, dtype=F32)
    inv = jnp.exp(-math.log(ROPE_BASE) * jnp.arange(half, dtype=F32) / half)
    ang = pos[:, None] * inv[None, :]
    cos, sin = jnp.cos(ang), jnp.sin(ang)
    return jnp.concatenate([cos, cos], axis=-1), jnp.concatenate([-sin, sin], axis=-1)


def _rot_apply(name, src_q, qcol, src_k, kcol, cos_t, sin_t, heads, dk, kscale, out_dtype, transpose):
    l = src_q.shape[0]
    w = heads * dk
    tm = _tile(l, 512, SUBLANE)

    def rot(t, cos_v, sin_v):
        if transpose:
            return t * cos_v + pltpu.roll(t * sin_v, dk // 2, 1)
        return t * cos_v + pltpu.roll(t, dk // 2, 1) * sin_v

    def body(q_ref, k_ref, c_ref, s_ref, qo_ref, ko_ref):
        cos_v, sin_v = c_ref[...], s_ref[...]
        for h in range(heads):
            cs = slice(h * dk, (h + 1) * dk)
            qo_ref[:, cs] = rot(q_ref[:, cs].astype(F32), cos_v, sin_v).astype(out_dtype)
            ko_ref[:, cs] = (rot(k_ref[:, cs].astype(F32), cos_v, sin_v) * kscale).astype(out_dtype)

    tab = pl.BlockSpec((tm, dk), lambda i: (i, 0))
    row = pl.BlockSpec((tm, w), lambda i: (i, 0))
    return pl.pallas_call(
        body, name=name, grid=(l // tm,),
        in_specs=[pl.BlockSpec((tm, w), lambda i: (i, qcol)), pl.BlockSpec((tm, w), lambda i: (i, kcol)), tab, tab],
        out_specs=[row, row], out_shape=[_sds((l, w), out_dtype), _sds((l, w), out_dtype)],
        compiler_params=_params(("arbitrary",)),
    )(src_q, src_k, cos_t, sin_t)


def _peer_copies(src_ref, out_ref, send_sems, recv_sems, gather):
    x, y, c = lax.axis_index("x"), lax.axis_index("y"), lax.axis_index("c")
    me = 4 * x + 2 * y + c
    copies = []
    for kk in range(1, N_DEV):
        px = (1 - x) if kk & 4 else x
        py = (1 - y) if kk & 2 else y
        pc = (1 - c) if kk & 1 else c
        peer = 4 * px + 2 * py + pc
        copies.append(pltpu.make_async_remote_copy(
            src_ref=src_ref if gather else src_ref.at[peer], dst_ref=out_ref.at[me],
            send_sem=send_sems.at[kk - 1], recv_sem=recv_sems.at[kk - 1],
            device_id=(px, py, pc), device_id_type=pl.DeviceIdType.MESH))
    return copies


_HBM = pl.BlockSpec(memory_space=pltpu.HBM)
_SEM = pl.BlockSpec(memory_space=pltpu.SEMAPHORE)
_EFFECT = pltpu.SideEffectType.DATAFLOW_SIDE_EFFECTING


def _exchange_start(name, srcs, gather):
    n = len(srcs)
    lands = [lax.empty((N_DEV,) + tuple(s.shape if gather else s.shape[1:]), s.dtype) for s in srcs]

    def body(*refs):
        src_refs, land_refs = refs[:n], refs[n:2 * n]
        send, recv = refs[2 * n:3 * n], refs[3 * n:4 * n]
        token = refs[-1]
        for k in range(n):
            for cp in _peer_copies(src_refs[k], land_refs[k], send[k], recv[k], gather):
                cp.start()
        token[...] = jnp.zeros_like(token)

    sem = pltpu.SemaphoreType.DMA((N_DEV - 1,))
    outs = pl.pallas_call(
        body, name=name,
        out_shape=tuple([sem] * (2 * n) + [pltpu.HBM(s.shape, s.dtype) for s in srcs] + [pltpu.HBM(a.shape, a.dtype) for a in lands]
                        + [_sds((SUBLANE, LANE), F32)]),
        in_specs=tuple([_HBM] * (2 * n)), out_specs=tuple([_SEM] * (2 * n) + [_HBM] * (2 * n) + [pl.BlockSpec(memory_space=pltpu.VMEM)]),
        input_output_aliases={k: 2 * n + k for k in range(2 * n)},
        compiler_params=pltpu.CompilerParams(has_side_effects=_EFFECT),
    )(*[pltpu.with_memory_space_constraint(a, pltpu.HBM) for a in list(srcs) + lands])
    return [(outs[k], outs[n + k], outs[2 * n + k], outs[3 * n + k], outs[-1]) for k in range(n)]


def _exchange_wait(name, started, gather, after):
    send_sems, recv_sems, src_thru, land_thru, _ = started

    def body(src_ref, land_ref, send_sems, recv_sems, after_ref, src_out, land_out):
        copies = _peer_copies(src_ref, land_ref, send_sems, recv_sems, gather)
        for cp in copies:
            cp.wait_send()
        for cp in copies:
            cp.wait_recv()

    return pl.pallas_call(
        body, name=name,
        out_shape=(pltpu.HBM(src_thru.shape, src_thru.dtype), pltpu.HBM(land_thru.shape, land_thru.dtype)),
        in_specs=(_HBM, _HBM, _SEM, _SEM, pl.BlockSpec(memory_space=pl.ANY)), out_specs=(_HBM, _HBM),
        input_output_aliases={0: 0, 1: 1},
        compiler_params=pltpu.CompilerParams(has_side_effects=_EFFECT),
    )(src_thru, land_thru, send_sems, recv_sems, after)


def _adam_math(w, gsum, m, v):
    m2 = ADAM_B1 * m + (1.0 - ADAM_B1) * gsum
    v2 = ADAM_B2 * v + (1.0 - ADAM_B2) * (gsum * gsum)
    m_hat = m2 / (1.0 - ADAM_B1 ** ADAM_STEP)
    v_hat = v2 / (1.0 - ADAM_B2 ** ADAM_STEP)
    delta = -ADAM_LR * (m_hat / (jnp.sqrt(v_hat) + ADAM_EPS) + ADAM_WD * w)
    return delta, m2, v2


def _reduce_adam(name, parts, w, m, v):
    nl, r, c = w.shape
    tr = _tile(r, 256, 16)
    nr = r // tr

    def body(*refs):
        p_refs = refs[:nl]
        w_ref, m_ref, v_ref, g_ref, d_ref, m2_ref, v2_ref = refs[nl:]
        for li in range(nl):
            @pl.when(pl.program_id(0) == li)
            def _(p_ref=p_refs[li]):
                gsum = p_ref[0].astype(F32)
                for s in range(1, N_DEV):
                    gsum = gsum + p_ref[s].astype(F32)
                g_ref[...] = gsum
                delta, m2, v2 = _adam_math(w_ref[...], gsum, m_ref[...], v_ref[...])
                d_ref[...] = delta
                m2_ref[...] = m2
                v2_ref[...] = v2

    def part_spec(li):
        return pl.BlockSpec((N_DEV, tr, c), lambda lay, i: (0, jnp.where(lay == li, i, jnp.where(lay < li, 0, nr - 1)), 0))

    row = pl.BlockSpec((None, tr, c), lambda lay, i: (lay, i, 0))
    return pl.pallas_call(
        body, name=name, grid=(nl, nr),
        in_specs=[part_spec(li) for li in range(nl)] + [row, row, row],
        out_specs=[row, row, row, row], out_shape=[_sds((nl, r, c), F32)] * 4,
        compiler_params=_params(("arbitrary", "arbitrary")),
    )(*parts, w, m, v)


def _reduce8(name, parts):
    _, r, c = parts.shape

    def body(p_ref, g_ref):
        gsum = p_ref[0]
        for s in range(1, N_DEV):
            gsum = gsum + p_ref[s]
        g_ref[...] = gsum

    return pl.pallas_call(
        body, name=name, grid=(1,),
        in_specs=[pl.BlockSpec((N_DEV, r, c), lambda i: (0, 0, 0))],
        out_specs=pl.BlockSpec((r, c), lambda i: (0, 0)), out_shape=_sds((r, c), F32),
        compiler_params=_params(("arbitrary",)),
    )(parts)


def _adam_packed(name, w, g, m, v):
    r, c = w.shape

    def body(w_ref, g_ref, m_ref, v_ref, d_ref, m2_ref, v2_ref):
        delta, m2, v2 = _adam_math(w_ref[...], g_ref[...], m_ref[...], v_ref[...])
        d_ref[...] = delta
        m2_ref[...] = m2
        v2_ref[...] = v2

    spec = pl.BlockSpec((r, c), lambda i: (0, 0))
    return pl.pallas_call(
        body, name=name, grid=(1,), in_specs=[spec] * 4, out_specs=[spec] * 3, out_shape=[_sds((r, c), F32)] * 3,
        compiler_params=_params(("arbitrary",)),
    )(w, g, m, v)


def _pack(arrs):
    flat = jnp.concatenate([a.reshape(-1).astype(F32) for a in arrs])
    n = flat.shape[0]
    pad = (-n) % (SUBLANE * LANE)
    return jnp.pad(flat, (0, pad)).reshape(-1, LANE)


def _unpack(packed, like):
    flat = packed.reshape(-1)
    out, off = [], 0
    for a in like:
        n = math.prod(a.shape)
        out.append(flat[off:off + n].reshape(a.shape))
        off += n
    return out


def _row_blocks(full):
    return full.reshape(N_DEV, full.shape[0] // N_DEV, full.shape[1])


def _col_blocks(full):
    r, c = full.shape
    return full.reshape(r, N_DEV, c // N_DEV).transpose(1, 0, 2)


def kernel(x, ffn1_norm, ffn1_w1, ffn1_w2, mix_norm, ffn2_norm, ffn2_w1, ffn2_w2, ab_w_in, s5_lambda_re, s5_lambda_im, s5_b_re, s5_b_im, s5_c_re, s5_c_im, s5_log_dt, s5_d, s5_w_glu, gla_w_gk, gla_b_gk, gla_norm, ab_w_out, ret_w_in, ret_norm, ret_w_out, final_norm, loss_target, m_ffn1_norm, m_ffn1_w1, m_ffn1_w2, m_mix_norm, m_ffn2_norm, m_ffn2_w1, m_ffn2_w2, m_ab_w_in, m_s5_lambda_re, m_s5_lambda_im, m_s5_b_re, m_s5_b_im, m_s5_c_re, m_s5_c_im, m_s5_log_dt, m_s5_d, m_s5_w_glu, m_gla_w_gk, m_gla_b_gk, m_gla_norm, m_ab_w_out, m_ret_w_in, m_ret_norm, m_ret_w_out, m_final_norm, v_ffn1_norm, v_ffn1_w1, v_ffn1_w2, v_mix_norm, v_ffn2_norm, v_ffn2_w1, v_ffn2_w2, v_ab_w_in, v_s5_lambda_re, v_s5_lambda_im, v_s5_b_re, v_s5_b_im, v_s5_c_re, v_s5_c_im, v_s5_log_dt, v_s5_d, v_s5_w_glu, v_gla_w_gk, v_gla_b_gk, v_gla_norm, v_ab_w_out, v_ret_w_in, v_ret_norm, v_ret_w_out, v_final_norm):
    names = ['ffn1_norm', 'ffn1_w1', 'ffn1_w2', 'mix_norm', 'ffn2_norm', 'ffn2_w1', 'ffn2_w2', 'ab_w_in', 's5_lambda_re', 's5_lambda_im', 's5_b_re', 's5_b_im', 's5_c_re', 's5_c_im', 's5_log_dt', 's5_d', 's5_w_glu', 'gla_w_gk', 'gla_b_gk', 'gla_norm', 'ab_w_out', 'ret_w_in', 'ret_norm', 'ret_w_out', 'final_norm']
    loc = locals()
    W = {n: loc[n] for n in names}
    M = {n: loc["m_" + n] for n in names}
    V = {n: loc["v_" + n] for n in names}

    me = 4 * lax.axis_index("x") + 2 * lax.axis_index("y") + lax.axis_index("c")
    xs = x[0]
    tgt = loss_target[0]
    l, d = xs.shape
    depth = ffn1_norm.shape[0]

    pending, to_start = {}, []

    def start_gather(tag, shard):
        to_start.append((tag, shard))

    def finish_gather(tag, after):
        started, shard = pending.pop(tag)
        _, got = _exchange_wait("agw_" + tag, started, True, after)
        return lax.dynamic_update_index_in_dim(got, shard, me, 0)

    def finish_cols(tag, after):
        g = finish_gather(tag, after)
        return g.transpose(1, 0, 2).reshape(g.shape[1], -1)

    def finish_rows(tag, after):
        g = finish_gather(tag, after)
        return g.reshape(-1, g.shape[2])

    small_sharded = [gla_w_gk, gla_b_gk, ret_norm]
    for i in range(depth):
        j = i // 2
        start_gather(f"ffn1_w1_{i}", ffn1_w1[i].astype(BF))
        start_gather(f"ffn1_w2_{i}", ffn1_w2[i].astype(BF))
        if i % 2 == 0:
            start_gather(f"ab_w_in_{j}", ab_w_in[j].astype(BF))
            if i == 0:
                start_gather("small", _pack(small_sharded))
            start_gather(f"s5_w_glu_{j}", s5_w_glu[j].astype(BF))
            start_gather(f"ab_w_out_{j}", ab_w_out[j].astype(BF))
        else:
            start_gather(f"ret_w_in_{j}", ret_w_in[j].astype(BF))
            start_gather(f"ret_w_out_{j}", ret_w_out[j].astype(BF))
        start_gather(f"ffn2_w1_{i}", ffn2_w1[i].astype(BF))
        start_gather(f"ffn2_w2_{i}", ffn2_w2[i].astype(BF))
    for (tag, shard), started in zip(to_start, _exchange_start("ags_weights", [s_ for _, s_ in to_start], True)):
        pending[tag] = (started, shard)
    started_all = started[4][0, 0]
    full = {}

    s5w = s5_d.shape[1]
    g_s5, n_s5 = s5_lambda_re.shape[2], s5_lambda_re.shape[3]
    hs = min(SUBLANE * LANE, g_s5 * n_s5)
    gla_hk = gla_w_gk.shape[-1] * N_DEV
    gla_dk = gla_hk // GLA_HEADS
    gla_hv = gla_norm.shape[1]
    gla_dv = gla_hv // GLA_HEADS
    ret_hv = ret_norm.shape[1] * N_DEV
    ret_dv = ret_hv // RET_HEADS
    ret_hk = (ret_w_in.shape[2] * N_DEV - 2 * ret_hv) // 2
    ret_dk = ret_hk // RET_HEADS
    assert s5w == gla_hv and 2 * gla_hk == s5w, "column blocks of the mixer projection assume these widths"
    assert ret_hv == 2 * ret_hk
    main_w = s5w + 2 * gla_hk + 2 * gla_hv
    gla_tb = _tile(l, 256, GLA_CHUNK)
    ret_chunk = min(RET_CHUNK, l)

    cos_t, sin_t = _rot_tables(l, ret_dk)
    lg_f = jnp.log1p(-jnp.exp2(-5.0 - jnp.arange(RET_HEADS, dtype=F32)))
    lgtab_f = jnp.broadcast_to(lg_f[:, None, None], (RET_HEADS, 1, LANE))
    lgtab_b = jnp.broadcast_to(lg_f[::-1][:, None, None], (RET_HEADS, 1, LANE))
    s5_pre = {}
    for j in range((depth + 1) // 2):
        s5_args = (s5_lambda_re[j], s5_lambda_im[j], s5_b_re[j], s5_b_im[j], s5_c_re[j], s5_c_im[j], s5_log_dt[j])
        (a_tab, bd, cd), s5_vjp = jax.vjp(lambda *a: _s5_chunk_tables(*a, hs), *s5_args)
        s5_pre[j] = (a_tab, bd.astype(BF), cd.astype(BF), s5_vjp)
    tables_done = jnp.stack([cos_t[0, 0], sin_t[0, 0], lgtab_f[0, 0, 0], lgtab_b[0, 0, 0]]
                            + [t[0].reshape(-1)[0] + t[1].reshape(-1)[0].astype(F32) + t[2].reshape(-1)[0].astype(F32) for t in s5_pre.values()])

    saved = []
    cur = xs
    hn = _rms_fwd("l0_ffn1_norm", cur, ffn1_norm[0:1] + started_all)
    for i in range(depth):
        j = i // 2
        s = {}
        s['x0'] = cur

        def first_w1(after):
            if i == 0:
                after = jnp.concatenate([after[0, 0:1].astype(F32), tables_done])
            return finish_cols(f"ffn1_w1_{i}", after)

        cur, h, s['ffn1'], s['f1w1'], s['f1w2'] = _ffn_fwd(f"l{i}_ffn1", cur, hn, first_w1, lambda after: finish_rows(f"ffn1_w2_{i}", after),
                                                            mix_norm[i:i + 1])
        s['x1'] = cur
        s['h'] = h
        if i % 2 == 0:
            w_in = finish_cols(f"ab_w_in_{j}", cur)
            if i == 0:
                got = finish_gather("small", cur)
                flat, off, joined = got.reshape(N_DEV, -1), 0, []
                for a in small_sharded:
                    n = math.prod(a.shape)
                    blk = jnp.moveaxis(flat[:, off:off + n].reshape((N_DEV,) + a.shape), 0, -2)
                    joined.append(blk.reshape(a.shape[:-1] + (N_DEV * a.shape[-1],)))
                    off += n
                full['gla_w_gk'], full['gla_b_gk'], ret_norm_full = joined[0].astype(BF), joined[1], joined[2]
            s['w_glu'], s['w_out'] = finish_rows(f"s5_w_glu_{j}", cur), finish_rows(f"ab_w_out_{j}", cur)
            w_main, w_glo = w_in[:, :main_w], w_in[:, main_w:]
            proj = _mm_plain(f"l{i}_proj", h, w_main, NN, BF, tm=1024, tn=1024, tk=d, b_outer=True)
            glo = _mm_plain(f"l{i}_glo", h, w_glo, NN, F32, tm=1024, tn=2 * GLA_RANK, tk=d)
            a_tab, bd16, cd16, s5_vjp = s5_pre[j]
            tm = _tile(l, 512, SUBLANE)
            x_f, y_f = _s5_fwd(f"l{i}_s5_fwd_f", proj, bd16[0], cd16[0], a_tab[0], False)
            x_b, y_b = _s5_fwd(f"l{i}_s5_fwd_b", proj, bd16[1], cd16[1], a_tab[1], True)
            d_row = s5_d[j:j + 1]
            y, s5_out = _glu_fwd(f"l{i}_s5_glu", y_f, y_b, proj, d_row, s['w_glu'])
            zeros_r = jnp.zeros((GLA_RANK, gla_hk), BF)
            w_gk = full['gla_w_gk'][j]
            wgk_f = jnp.concatenate([w_gk[0], zeros_r], axis=0)
            wgk_b = jnp.concatenate([zeros_r, w_gk[1]], axis=0)
            b_gk = full['gla_b_gk'][j]
            g_f, g_b = _gate_fwd(f"l{i}_gla_gate", glo, wgk_f, wgk_b, b_gk[0:1], b_gk[1:2])
            qcol, kcol, vcol, ogcol = s5w // gla_hk, s5w // gla_hk + 1, (s5w + 2 * gla_hk) // gla_hv, (s5w + 2 * gla_hk) // gla_hv + 1
            lin_kw = dict(heads=GLA_HEADS, dk=gla_dk, dv=gla_dv, chunk=GLA_CHUNK, tb=gla_tb, qcol=qcol, kcol=kcol, vcol=vcol,
                          qscale=gla_dk ** -0.5)
            o_f, sp_f = _gla_fwd(f"l{i}_gla_fwd_f", proj, g_f, reverse=False, **lin_kw)
            o_b, sp_b = _gla_fwd(f"l{i}_gla_fwd_b", proj, g_b, o_f, reverse=True, **lin_kw)
            gla_out = _headgate_fwd(f"l{i}_gla_out", o_b, proj, ogcol, gla_norm[j:j + 1], gla_dv)
            w_out = s['w_out']

            row = pl.BlockSpec((tm, d), lambda ii, jj, kk: (ii, 0))
            vec = pl.BlockSpec((1, d), lambda ii, jj, kk: (0, 0))
            cur, hn = _mm(f"l{i}_mix_out",
                          [(s5_out, pl.BlockSpec((tm, s5w), lambda ii, jj, kk: (ii, 0)), w_out, pl.BlockSpec((s5w, d), lambda ii, jj, kk: (0, 0))),
                           (gla_out, pl.BlockSpec((tm, gla_hv), lambda ii, jj, kk: (ii, 0)), w_out, pl.BlockSpec((gla_hv, d), lambda ii, jj, kk: (1, 0)))],
                          NN, (l // tm, 1, 1), [_sds((l, d), F32), _sds((l, d), BF)], [row, row], (tm, d), _residual_epi(1.0, True),
                          [cur, ffn2_norm[i:i + 1]], [row, vec])
            s.update(proj=proj, glo=glo, s5_vjp=s5_vjp, a_tab=a_tab, bd16=bd16, cd16=cd16, x_f=x_f, x_b=x_b, y=y, s5_out=s5_out,
                     wgk_f=wgk_f, wgk_b=wgk_b, b_gk=b_gk, g_f=g_f, g_b=g_b, o_f=o_f, o_b=o_b, sp_f=sp_f, sp_b=sp_b, gla_out=gla_out,
                     w_main=w_main, w_glo=w_glo, lin_kw=lin_kw, ogcol=ogcol)
        else:
            w_in = finish_cols(f"ret_w_in_{j}", cur)
            s['w_in'], s['w_out'] = w_in, finish_rows(f"ret_w_out_{j}", cur)
            proj = _mm_plain(f"l{i}_proj", h, w_in, NN, BF, tm=1024, tn=1024, tk=d, b_outer=True)
            qr, kr = _rot_apply(f"l{i}_rot", proj, 0, proj, 1, cos_t, sin_t, RET_HEADS, ret_dk, ret_dk ** -0.5, BF, False)
            ret_hb = 4
            lin_kw = dict(heads=RET_HEADS, hb=ret_hb, dk=ret_dk, dv=ret_dv, chunk=ret_chunk, tb=_tile(l, 4 * ret_chunk, ret_chunk), qcol=0, kcol=0,
                          vcol=(2 * ret_hk) // (ret_hb * ret_dv))
            o_f, sp_f = _lin_fwd(f"l{i}_ret_fwd_f", qr, kr, proj, lgtab_f, reverse=False, **lin_kw)
            o_b, sp_b = _lin_fwd(f"l{i}_ret_fwd_b", qr, kr, proj, lgtab_b, o_f, reverse=True, **lin_kw)
            ogcol = (2 * ret_hk + ret_hv) // ret_hv
            r_out = _headgate_fwd(f"l{i}_ret_out", o_b, proj, ogcol, ret_norm_full, ret_dv)

            tm = _tile(l, 512, SUBLANE)
            cur, hn = _mm_plain(f"l{i}_mix_out", r_out, s['w_out'], NN, F32, tm=512, tn=d, tk=ret_hv, epi=_residual_epi(1.0, True),
                                eins=[cur, ffn2_norm[i:i + 1]], especs=[None, pl.BlockSpec((1, d), lambda ii, jj, kk: (0, 0))],
                                extra_outs=[_sds((l, d), BF)], extra_specs=[pl.BlockSpec((tm, d), lambda ii, jj, kk: (ii, 0))])
            s.update(proj=proj, qr=qr, kr=kr, o_f=o_f, o_b=o_b, sp_f=sp_f, sp_b=sp_b, r_out=r_out, lin_kw=lin_kw, ogcol=ogcol)
        s['x2'] = cur
        cur, hn, s['ffn2'], s['f2w1'], s['f2w2'] = _ffn_fwd(f"l{i}_ffn2", cur, hn, lambda after: finish_cols(f"ffn2_w1_{i}", after),
                                                             lambda after: finish_rows(f"ffn2_w2_{i}", after),
                                                             ffn1_norm[i + 1:i + 2] if i + 1 < depth else None)
        saved.append(s)

    dx, d_final_norm, loss_row = _loss_head("loss_head", cur, final_norm.reshape(1, -1), tgt)
    loss = lax.psum(loss_row[0, 0], ("x", "y", "c"))

    G = {}
    big = {}
    G['final_norm'] = d_final_norm.reshape(-1)
    per_layer = {n: [None] * depth for n in ['ffn1_norm', 'mix_norm', 'ffn2_norm']}
    small_late = ['ffn1_norm', 'mix_norm']
    small_early = ['ffn2_norm', 's5_lambda_re', 's5_lambda_im', 's5_b_re', 's5_b_im', 's5_c_re', 's5_c_im',
                   's5_log_dt', 's5_d', 'gla_w_gk', 'gla_b_gk', 'gla_norm', 'ret_norm', 'final_norm']
    a2a, tok = {}, [jnp.zeros((), F32)]

    def start_a2a(*tagged):
        for (tag, _), started in zip(tagged, _exchange_start("a2as_" + tagged[0][0], [b for _, b in tagged], False)):
            a2a[tag] = started
        tok[0] = tok[0] + started[4][0, 0]

    def dep(vec):
        return vec + tok[0]

    def proj_backward(tag, pieces, s, dres, a2a_tag):
        tm = _tile(l, 512, SUBLANE)
        row = pl.BlockSpec((tm, d), lambda ii, jj, kk: (ii, 0))
        vec = pl.BlockSpec((1, d), lambda ii, jj, kk: (0, 0))
        dws = [_mm_plain(f"{tag}_dwin_{k}", s['h'], piece, TN, BF, tm=d, tn=2048, tk=2048) for k, (piece, _, _) in enumerate(pieces)]
        start_a2a((a2a_tag, _col_blocks(jnp.concatenate(dws, axis=1))))
        pairs = []
        for piece, w, col in pieces:
            wd = piece.shape[1]
            pairs.append((piece, pl.BlockSpec((tm, wd), lambda ii, jj, kk: (ii, 0)),
                          w, pl.BlockSpec((d, wd), lambda ii, jj, kk, col=col: (0, col), pipeline_mode=pl.Buffered(1))))
        return _mm(f"{tag}_dh", pairs, NT, (l // tm, 1, 1), [_sds((l, d), F32), _sds((1, d), F32)], [row, vec], (tm, d),
                   _rms_bwd_epi(0), [s['x1'], dep(mix_norm[i:i + 1]), dres], [row, vec, row])

    for i in reversed(range(depth)):
        j = i // 2
        s = saved[i]
        def ffn_grads(which):
            def on_grads(dw1, dw2, gnorm):
                start_a2a((f"{which}_w1_{i}", _col_blocks(dw1)), (f"{which}_w2_{i}", _row_blocks(dw2)))
                return dep(gnorm)
            return on_grads

        dx, dg = _ffn_bwd(f"l{i}_ffn2b", dx, s['x2'], ffn2_norm[i:i + 1], s['f2w1'], s['f2w2'], s['ffn2'], ffn_grads("ffn2"))
        per_layer['ffn2_norm'][i] = dg[0]
        tm = _tile(l, 512, SUBLANE)
        row = pl.BlockSpec((tm, d), lambda ii, jj, kk: (ii, 0))
        vec = pl.BlockSpec((1, d), lambda ii, jj, kk: (0, 0))
        if i % 2 == 0:
            proj, lin_kw = s['proj'], s['lin_kw']
            w_out = s['w_out']
            d_cat = _mm_plain(f"l{i}_dcat", dx, w_out, NT, BF, tm=512, tn=1024, tk=d)
            dwo_a = _mm_plain(f"l{i}_dwout_a", s['s5_out'], dx, TN, BF, tm=s5w, tn=d, tk=2048)
            dwo_b = _mm_plain(f"l{i}_dwout_b", s['gla_out'], dx, TN, BF, tm=gla_hv, tn=d, tk=2048)
            start_a2a((f"ab_w_out_{j}", _row_blocks(jnp.concatenate([dwo_a, dwo_b], axis=0))))
            do, dog, dgn = _headgate_bwd(f"l{i}_gla_outb", s['o_b'], proj, s['ogcol'], dep(gla_norm[j:j + 1]), d_cat, 1, gla_dv)
            G['gla_norm'] = dgn
            dq, dk_, dv_, dgf = _gla_bwd(f"l{i}_gla_bwd_f", proj, s['g_f'], s['sp_f'], do, None, reverse=False, **lin_kw)
            dq, dk_, dv_, dgb = _gla_bwd(f"l{i}_gla_bwd_b", proj, s['g_b'], s['sp_b'], do, (dq, dk_, dv_), reverse=True, **lin_kw)
            dglo, dwf, dwb, dbf, dbb = _gate_bwd(f"l{i}_gla_gateb", s['glo'], s['wgk_f'], s['wgk_b'], s['b_gk'][0:1], s['b_gk'][1:2], dgf, dgb)
            G['gla_w_gk'] = jnp.stack([dwf[:GLA_RANK], dwb[GLA_RANK:]], axis=0)[None]
            G['gla_b_gk'] = jnp.concatenate([dbf, dbb], axis=0)[None]
            dy, dwglu = _glu_bwd(f"l{i}_s5_glub", s['y'], s['w_glu'], d_cat, 0)
            start_a2a((f"s5_w_glu_{j}", _row_blocks(dwglu.astype(BF))))
            cd16, bd16, a_tab = s['cd16'], s['bd16'], s['a_tab']
            nt2 = a_tab.shape[2]
            a_conj = a_tab * jnp.where(jnp.arange(nt2) < nt2 // 2, 1.0, -1.0)[None, None, :, None]
            du_f, dbd_f, dcd_f, da_f = _s5_bwd(f"l{i}_s5_bwd_f", proj, dy, s['x_f'], bd16[0], cd16[0], a_conj[0], True)
            du_b, dbd_b, dcd_b, da_b = _s5_bwd(f"l{i}_s5_bwd_b", proj, dy, s['x_b'], bd16[1], cd16[1], a_conj[1], False)
            du, dd = _s5_du(f"l{i}_s5_du", du_f, du_b, dy, proj, s5_d[j:j + 1])
            G['s5_d'] = dd
            cot = (jnp.stack([da_f, da_b]), jnp.stack([dbd_f, dbd_b]), jnp.stack([dcd_f, dcd_b]))
            g_lre, g_lim, g_bre, g_bim, g_cre, g_cim, g_ldt = s['s5_vjp'](cot)
            G['s5_lambda_re'], G['s5_lambda_im'], G['s5_b_re'], G['s5_b_im'] = g_lre[None], g_lim[None], g_bre[None], g_bim[None]
            G['s5_c_re'], G['s5_c_im'], G['s5_log_dt'] = g_cre[None], g_cim[None], g_ldt[None]
            if i == 0:
                G['ffn2_norm'] = jnp.stack(per_layer['ffn2_norm'], axis=0)
                early_packed = _pack([G[n] for n in small_early])
                early_started = _exchange_start("ags_small_grads_early", [early_packed], True)[0]
                tok[0] = tok[0] + early_started[4][0, 0]
            w_main, w_glo = s['w_main'], s['w_glo']
            pieces = [(du, w_main, 0), (dq, w_main, s5w // gla_hk), (dk_, w_main, s5w // gla_hk + 1),
                      (dv_, w_main, (s5w + 2 * gla_hk) // gla_hv), (dog, w_main, (s5w + 2 * gla_hk) // gla_hv + 1), (dglo, w_glo, 0)]
            dx, dg = proj_backward(f"l{i}", pieces, s, dx, f"ab_w_in_{j}")
        else:
            proj, lin_kw = s['proj'], s['lin_kw']
            w_out = s['w_out']
            d_ro = _mm_plain(f"l{i}_dro", dx, w_out, NT, BF, tm=1024, tn=1024, tk=d, b_outer=True)
            dwo = _mm_plain(f"l{i}_dwout", s['r_out'], dx, TN, BF, tm=1024, tn=d, tk=2048)
            start_a2a((f"ret_w_out_{j}", _row_blocks(dwo)))
            do, dog, dgn = _headgate_bwd(f"l{i}_ret_outb", s['o_b'], proj, s['ogcol'], dep(ret_norm_full), d_ro, 0, ret_dv)
            G['ret_norm'] = dgn
            r1 = _lin_bwd(f"l{i}_ret_bwd_f", s['qr'], s['kr'], proj, lgtab_f, s['sp_f'], do, None, reverse=False, **lin_kw)
            r2 = _lin_bwd(f"l{i}_ret_bwd_b", s['qr'], s['kr'], proj, lgtab_b, s['sp_b'], do, r1, reverse=True, **lin_kw)
            dqr, dkr, dv_ = r2
            dq, dk_ = _rot_apply(f"l{i}_rotb", dqr, 0, dkr, 0, cos_t, sin_t, RET_HEADS, ret_dk, ret_dk ** -0.5, BF, True)
            w_in = s['w_in']
            pieces = [(dq, w_in, 0), (dk_, w_in, 1), (dv_, w_in, (2 * ret_hk) // ret_hv), (dog, w_in, (2 * ret_hk) // ret_hv + 1)]
            dx, dg = proj_backward(f"l{i}", pieces, s, dx, f"ret_w_in_{j}")
        per_layer['mix_norm'][i] = dg[0]
        dx, dg = _ffn_bwd(f"l{i}_ffn1b", dx, s['x0'], ffn1_norm[i:i + 1], s['f1w1'], s['f1w2'], s['ffn1'], ffn_grads("ffn1"))
        per_layer['ffn1_norm'][i] = dg[0]
    for n in small_late:
        G[n] = jnp.stack(per_layer[n], axis=0)
    grad_x = dx[None]

    out_g, out_d, out_m, out_v = {}, {}, {}, {}
    small = small_early + small_late
    packed = _pack([G[n] for n in small_late])
    small_started = _exchange_start("ags_small_grads_late", [packed], True)[0]

    def big_update(n, layers):
        parts = []
        for i in layers:
            blocks, got = _exchange_wait(f"a2aw_{n}_{i}", a2a.pop(f"{n}_{i}"), False, small_started[4])
            parts.append(lax.dynamic_update_index_in_dim(got, lax.dynamic_index_in_dim(blocks, me, 0, keepdims=False), me, 0))
        out_g[n], out_d[n], out_m[n], out_v[n] = _reduce_adam("upd_" + n, parts, W[n], M[n], V[n])

    for n in ['ffn2_w1', 'ffn2_w2', 'ffn1_w1', 'ffn1_w2']:
        big_update(n, range(depth))
    for n in ['ab_w_in', 's5_w_glu', 'ab_w_out', 'ret_w_in', 'ret_w_out']:
        big_update(n, [0])
    assert not a2a and not pending

    g_full = {}
    for tag, started, mine, group in (("early", early_started, early_packed, small_early), ("late", small_started, packed, small_late)):
        _, gathered = _exchange_wait("agw_small_grads_" + tag, started, True, out_v['ret_w_out'])
        gathered = lax.dynamic_update_index_in_dim(gathered, mine, me, 0)
        summed = _reduce8("sum_small_grads_" + tag, gathered)
        g_full.update(zip(group, _unpack(summed, [G[n] for n in group])))
    g_small = {}
    for n in small:
        gf = g_full[n]
        if n in ('gla_w_gk', 'gla_b_gk', 'ret_norm'):
            width = W[n].shape[-1]
            gf = lax.dynamic_slice_in_dim(gf, me * width, width, axis=gf.ndim - 1)
        g_small[n] = gf.reshape(W[n].shape)
    pw, pg, pm, pv = (_pack([src[n] for n in small]) for src in (W, g_small, M, V))
    pd, pm2, pv2 = _adam_packed("upd_small", pw, pg, pm, pv)
    like = [W[n] for n in small]
    for n, dd_, mm_, vv_ in zip(small, _unpack(pd, like), _unpack(pm2, like), _unpack(pv2, like)):
        out_g[n], out_d[n], out_m[n], out_v[n] = g_small[n], dd_, mm_, vv_

    return (loss, grad_x, *[out_g[n] for n in names], *[out_d[n] for n in names], *[out_m[n] for n in names], *[out_v[n] for n in names])
```

```python
import math

import jax
import jax.numpy as jnp
from jax import lax
from jax.experimental import pallas as pl
from jax.experimental.pallas import tpu as pltpu

F32 = jnp.float32
BF = jnp.bfloat16
N_DEV = 8
EPS = 1e-6
S5_GROUP = 16
GLA_HEADS = 4
GLA_RANK = 16
GLA_GATE_NORM = 16.0
RET_HEADS = 8
ROPE_BASE = 10000.0
GLA_CHUNK = 64
RET_CHUNK = 256
ADAM_LR, ADAM_B1, ADAM_B2, ADAM_EPS, ADAM_WD, ADAM_STEP = 0.001, 0.9, 0.999, 1e-08, 0.01, 10
VMEM_LIMIT_BYTES = 56 * 1024 * 1024
LANE = 128
SUBLANE = 8

NN = (((1,), (0,)), ((), ()))
NT = (((1,), (1,)), ((), ()))
TN = (((0,), (0,)), ((), ()))


def _tile(n, pref, align):
    if n <= pref:
        return n
    t = (pref // align) * align
    while t >= align:
        if n % t == 0:
            return t
        t -= align
    return n


def _params(sem):
    return pltpu.CompilerParams(dimension_semantics=sem, vmem_limit_bytes=VMEM_LIMIT_BYTES)


def _dot(a, b, dims=NN):
    return lax.dot_general(a.astype(BF), b.astype(BF), dims, preferred_element_type=F32)


def _dot3(m01, g, dims=NN):
    g1 = g.astype(BF)
    r1 = g - g1.astype(F32)
    g2 = r1.astype(BF)
    g3 = (r1 - g2.astype(F32)).astype(BF)
    m = m01.astype(BF)
    return (lax.dot_general(m, g1, dims, preferred_element_type=F32)
            + lax.dot_general(m, g2, dims, preferred_element_type=F32)
            + lax.dot_general(m, g3, dims, preferred_element_type=F32))


def _sigmoid(x):
    return 0.5 + 0.5 * jnp.tanh(0.5 * x)


def _mm(name, pairs, dims, grid, outs, out_specs, acc_shape, epi=None, eins=(), especs=()):
    n_p, n_e, n_o = len(pairs), len(eins), len(outs)
    nk = grid[2]

    def body(*refs):
        a_refs = refs[0:2 * n_p:2]
        b_refs = refs[1:2 * n_p:2]
        e_refs = refs[2 * n_p:2 * n_p + n_e]
        o_refs = refs[2 * n_p + n_e:2 * n_p + n_e + n_o]
        acc = refs[-1]
        ids = (pl.program_id(0), pl.program_id(1), pl.program_id(2))

        part = _dot(a_refs[0][...], b_refs[0][...], dims)
        for p in range(1, n_p):
            part = part + _dot(a_refs[p][...], b_refs[p][...], dims)

        def finish(total):
            if epi is None:
                o_refs[0][...] = total.astype(o_refs[0].dtype)
            else:
                epi(total, e_refs, o_refs, ids)

        if nk == 1:
            finish(part)
        else:
            @pl.when(ids[2] == 0)
            def _():
                acc[...] = part

            @pl.when(ids[2] > 0)
            def _():
                acc[...] += part

            @pl.when(ids[2] == nk - 1)
            def _():
                finish(acc[...])

    in_specs, args = [], []
    for a, a_spec, b, b_spec in pairs:
        in_specs += [a_spec, b_spec]
        args += [a, b]
    in_specs += list(especs)
    args += list(eins)
    res = pl.pallas_call(
        body, name=name, grid=grid, in_specs=in_specs, out_specs=list(out_specs), out_shape=list(outs),
        scratch_shapes=[pltpu.VMEM(acc_shape, F32)],
        compiler_params=_params(("arbitrary", "arbitrary", "arbitrary")),
    )(*args)
    return res


def _sds(shape, dtype):
    return jax.ShapeDtypeStruct(shape, dtype)


def _mm_plain(name, a, b, dims, out_dtype, tm=512, tn=1024, tk=1024, epi=None, eins=(), especs=None, extra_outs=(), extra_specs=(),
              b_outer=False):
    if dims == NN:
        (m, k), n = a.shape, b.shape[1]
    elif dims == NT:
        (m, k), n = a.shape, b.shape[0]
    else:
        (k, m), n = a.shape, b.shape[1]
    tm, tn = _tile(m, tm, LANE if dims == TN else SUBLANE), _tile(n, tn, LANE)
    tk = _tile(k, tk, SUBLANE if dims == TN else LANE)
    grid = (n // tn, m // tm, k // tk) if b_outer else (m // tm, n // tn, k // tk)

    def spec(block, index):
        if b_outer:
            return pl.BlockSpec(block, lambda j, i, kk: index(i, j, kk))
        return pl.BlockSpec(block, index)

    if dims == NN:
        a_spec = spec((tm, tk), lambda i, j, kk: (i, kk))
        b_spec = spec((tk, tn), lambda i, j, kk: (kk, j))
    elif dims == NT:
        a_spec = spec((tm, tk), lambda i, j, kk: (i, kk))
        b_spec = spec((tn, tk), lambda i, j, kk: (j, kk))
    else:
        a_spec = spec((tk, tm), lambda i, j, kk: (kk, i))
        b_spec = spec((tk, tn), lambda i, j, kk: (kk, j))
    o_spec = spec((tm, tn), lambda i, j, kk: (i, j))
    if especs is None:
        especs = [o_spec] * len(eins)
    else:
        especs = [o_spec if s is None else s for s in especs]
    res = _mm(name, [(a, a_spec, b, b_spec)], dims, grid, [_sds((m, n), out_dtype)] + list(extra_outs),
              [o_spec] + list(extra_specs), (tm, tn), epi, eins, especs)
    return res if extra_outs else res[0]


def _rms_fwd(name, x, g):
    l, d = x.shape
    tm = _tile(l, 1024, SUBLANE)

    def body(x_ref, g_ref, o_ref):
        xv = x_ref[...]
        r = lax.rsqrt(jnp.mean(xv * xv, axis=-1, keepdims=True) + EPS)
        o_ref[...] = (xv * r * g_ref[...]).astype(o_ref.dtype)

    return pl.pallas_call(
        body, name=name, grid=(l // tm,),
        in_specs=[pl.BlockSpec((tm, d), lambda i: (i, 0)), pl.BlockSpec((1, d), lambda i: (0, 0))],
        out_specs=pl.BlockSpec((tm, d), lambda i: (i, 0)), out_shape=_sds((l, d), BF),
        compiler_params=_params(("arbitrary",)),
    )(x, g)


def _rms_bwd_epi(first_axis):
    def epi(acc, e_refs, o_refs, ids):
        x_ref, g_ref, dr_ref = e_refs
        dx_ref, dg_ref = o_refs
        xv = x_ref[...]
        r = lax.rsqrt(jnp.mean(xv * xv, axis=-1, keepdims=True) + EPS)
        xh = xv * r
        dxh = acc * g_ref[...]
        dx_ref[...] = dr_ref[...] + r * (dxh - xh * jnp.mean(dxh * xh, axis=-1, keepdims=True))
        part = jnp.sum(acc * xh, axis=0, keepdims=True)

        @pl.when(ids[first_axis] == 0)
        def _():
            dg_ref[...] = part

        @pl.when(ids[first_axis] > 0)
        def _():
            dg_ref[...] += part

    return epi


def _loss_head(name, x, g, target):
    l, d = x.shape
    tm = _tile(l, 512, SUBLANE)
    n = l // tm

    def body(x_ref, g_ref, t_ref, dx_ref, dg_ref, loss_ref, lacc):
        i = pl.program_id(0)
        xv = x_ref[...]
        r = lax.rsqrt(jnp.mean(xv * xv, axis=-1, keepdims=True) + EPS)
        xh = xv * r
        e = xh * g_ref[...] - t_ref[...]
        dy = e * (1.0 / d)
        dxh = dy * g_ref[...]
        dx_ref[...] = r * (dxh - xh * jnp.mean(dxh * xh, axis=-1, keepdims=True))
        dg_part = jnp.sum(dy * xh, axis=0, keepdims=True)
        l_part = jnp.sum(e * e, axis=0, keepdims=True)

        @pl.when(i == 0)
        def _():
            dg_ref[...] = dg_part
            lacc[...] = l_part

        @pl.when(i > 0)
        def _():
            dg_ref[...] += dg_part
            lacc[...] += l_part

        @pl.when(i == n - 1)
        def _():
            loss_ref[...] = jnp.zeros_like(loss_ref) + jnp.sum(lacc[...]) * (0.5 / d)

    return pl.pallas_call(
        body, name=name, grid=(n,),
        in_specs=[pl.BlockSpec((tm, d), lambda i: (i, 0)), pl.BlockSpec((1, d), lambda i: (0, 0)),
                  pl.BlockSpec((tm, d), lambda i: (i, 0))],
        out_specs=[pl.BlockSpec((tm, d), lambda i: (i, 0)), pl.BlockSpec((1, d), lambda i: (0, 0)),
                   pl.BlockSpec((1, LANE), lambda i: (0, 0))],
        out_shape=[_sds((l, d), F32), _sds((1, d), F32), _sds((1, LANE), F32)],
        scratch_shapes=[pltpu.VMEM((1, d), F32)],
        compiler_params=_params(("arbitrary",)),
    )(x, g, target)


def _ffn_up(name, hn, w1):
    l, d = hn.shape
    f = w1.shape[1] // 2
    tm, tn = _tile(l, 512, SUBLANE), _tile(f, 1408, LANE)
    nj = f // tn

    def body(h_ref, wg_ref, wu_ref, gu_ref, a_ref):
        h = h_ref[...]
        g = jnp.dot(h, wg_ref[...], preferred_element_type=F32)
        u = jnp.dot(h, wu_ref[...], preferred_element_type=F32)
        s = _sigmoid(g)
        gs = g * s
        gu_ref[0] = (u * (s + gs * (1.0 - s))).astype(BF)
        gu_ref[1] = gs.astype(BF)
        a_ref[...] = (gs * u).astype(BF)

    return pl.pallas_call(
        body, name=name, grid=(nj, l // tm),
        in_specs=[pl.BlockSpec((tm, d), lambda j, i: (i, 0)), pl.BlockSpec((d, tn), lambda j, i: (0, j)),
                  pl.BlockSpec((d, tn), lambda j, i: (0, j + nj))],
        out_specs=[pl.BlockSpec((2, tm, tn), lambda j, i: (0, i, j)), pl.BlockSpec((tm, tn), lambda j, i: (i, j))],
        out_shape=[_sds((2, l, f), BF), _sds((l, f), BF)],
        compiler_params=_params(("arbitrary", "arbitrary")),
    )(hn, w1, w1)


def _residual_epi(scale, with_norm):
    def epi(acc, e_refs, o_refs, ids):
        xn = e_refs[0][...] + scale * acc
        o_refs[0][...] = xn
        if with_norm:
            r = lax.rsqrt(jnp.mean(xn * xn, axis=-1, keepdims=True) + EPS)
            o_refs[1][...] = (xn * r * e_refs[1][...]).astype(o_refs[1].dtype)

    return epi


def _ffn_fwd(tag, x, hn, get_w1, get_w2, next_gnorm):
    w1 = get_w1(hn)
    gu, a = _ffn_up(tag + "_up", hn, w1)
    w2 = get_w2(a)
    l, d = x.shape
    if next_gnorm is None:
        x_new = _mm_plain(tag + "_down", a, w2, NN, F32, tm=512, tn=d, tk=w2.shape[0], epi=_residual_epi(0.5, False), eins=[x])
        hn_next = None
    else:
        vec = pl.BlockSpec((1, d), lambda i, j, kk: (0, 0))
        tm = _tile(l, 512, SUBLANE)
        x_new, hn_next = _mm_plain(tag + "_down", a, w2, NN, F32, tm=512, tn=d, tk=w2.shape[0], epi=_residual_epi(0.5, True),
                                   eins=[x, next_gnorm], especs=[None, vec],
                                   extra_outs=[_sds((l, d), BF)], extra_specs=[pl.BlockSpec((tm, d), lambda i, j, kk: (i, 0))])
    return x_new, hn_next, (hn, gu, a), w1, w2


def _ffn_bwd(tag, dres, x, gnorm, w1, w2, saved, on_grads):
    hn, gu, a = saved
    l, d = x.shape
    f = w2.shape[0]
    tm, tn = _tile(l, 512, SUBLANE), _tile(f, 1408, LANE)
    nj = f // tn

    def epi_gu(acc, e_refs, o_refs, ids):
        da = 0.5 * acc
        o_refs[0][0] = (da * e_refs[0][0].astype(F32)).astype(BF)
        o_refs[0][1] = (da * e_refs[0][1].astype(F32)).astype(BF)

    tmg = _tile(l, 256, SUBLANE)
    gu_spec = pl.BlockSpec((2, tmg, f), lambda j, i, kk: (0, i, 0))
    dgu = _mm(tag + "_dgu",
              [(dres, pl.BlockSpec((tmg, d), lambda j, i, kk: (i, 0)), w2, pl.BlockSpec((f, d), lambda j, i, kk: (0, 0)))],
              NT, (1, l // tmg, 1), [_sds((2, l, f), BF)], [gu_spec], (tmg, f), epi_gu, [gu], [gu_spec])[0]

    def epi_half(acc, e_refs, o_refs, ids):
        o_refs[0][...] = (0.5 * acc).astype(BF)

    dw2 = _mm_plain(tag + "_dw2", a, dres, TN, BF, tm=1408, tn=d, tk=2048, epi=epi_half)

    tk = _tile(l, 2048, SUBLANE)
    dw1 = _mm(tag + "_dw1",
              [(hn, pl.BlockSpec((tk, d), lambda i, j, kk: (kk, 0)), dgu, pl.BlockSpec((None, tk, tn), lambda i, j, kk: (j // nj, kk, j % nj)))],
              TN, (1, 2 * nj, l // tk), [_sds((d, 2 * f), BF)], [pl.BlockSpec((d, tn), lambda i, j, kk: (0, j))], (d, tn))[0]

    gnorm = on_grads(dw1, dw2, gnorm)
    row = pl.BlockSpec((tm, d), lambda i, j, kk: (i, 0))
    vec = pl.BlockSpec((1, d), lambda i, j, kk: (0, 0))
    once = pl.Buffered(1)
    dx, dg = _mm(tag + "_dhn",
                 [(dgu, pl.BlockSpec((None, tm, f), lambda i, j, kk: (0, i, 0)), w1, pl.BlockSpec((d, f), lambda i, j, kk: (0, 0), pipeline_mode=once)),
                  (dgu, pl.BlockSpec((None, tm, f), lambda i, j, kk: (1, i, 0)), w1, pl.BlockSpec((d, f), lambda i, j, kk: (0, 1), pipeline_mode=once))],
                 NT, (l // tm, 1, 1), [_sds((l, d), F32), _sds((1, d), F32)], [row, vec], (tm, d),
                 _rms_bwd_epi(0), [x, gnorm, dres], [row, vec, row])
    return dx, dg


def _s5_chunk_tables(lam_re, lam_im, b_re, b_im, c_re, c_im, log_dt, hs):
    f32 = F32
    g, n = lam_re.shape[1], lam_re.shape[2]
    p = b_re.shape[-1]
    nch, gpc, nt = (g * n) // hs, hs // n, hs // LANE
    lr = jnp.minimum(lam_re.astype(f32), -1e-4)
    li = lam_im.astype(f32)
    dt = jnp.exp(log_dt.astype(f32))[..., None]
    mag = jnp.exp(lr * dt)
    ar = mag * jnp.cos(li * dt)
    ai = mag * jnp.sin(li * dt)
    den = lr * lr + li * li
    cr = ((ar - 1.0) * lr + ai * li) / den
    ci = (ai * lr - (ar - 1.0) * li) / den
    bbr = cr[..., None] * b_re - ci[..., None] * b_im
    bbi = cr[..., None] * b_im + ci[..., None] * b_re
    a_f = jnp.stack([ar, ai], axis=1).reshape(2, 2, nch, nt, LANE).transpose(0, 2, 1, 3, 4).reshape(2, nch, 2 * nt, LANE)
    rows_g = jnp.arange(gpc * p) // p
    cols_g = (jnp.arange(2 * hs) % hs) // n
    diag = (rows_g[:, None] == cols_g[None, :]).astype(f32)
    bb = jnp.stack([bbr, bbi], axis=1).reshape(2, 2, nch, hs, p)
    bd = jnp.tile(bb.transpose(0, 2, 4, 1, 3).reshape(2, nch, p, 2 * hs), (1, 1, gpc, 1)) * diag
    cc = jnp.stack([c_re, -c_im], axis=1).reshape(2, 2, nch, gpc, p, n)
    cd = jnp.tile(cc.transpose(0, 2, 4, 1, 3, 5).reshape(2, nch, p, 2 * hs), (1, 1, gpc, 1)) * diag
    return a_f, bd, cd


def _fold_store(ref, val, tb, ntiles):
    for s in range(ntiles):
        ref[:, s * SUBLANE:(s + 1) * SUBLANE, :] = val[:, s * LANE:(s + 1) * LANE].reshape(tb // SUBLANE, SUBLANE, LANE)


def _unfold(ref, tb, ntiles):
    return jnp.concatenate([ref[:, s * SUBLANE:(s + 1) * SUBLANE, :].reshape(tb, LANE) for s in range(ntiles)], axis=1)


def _s5_fwd(name, proj, bd, cd, a_f, reverse):
    l = proj.shape[0]
    nch, cu, hs2 = bd.shape
    nt = hs2 // (2 * LANE)
    frows = 2 * nt * SUBLANE
    tb = _tile(l, 512, SUBLANE)
    nb = l // tb

    def body(u_ref, bd_ref, cd_ref, a_ref, xf_ref, y_ref, st):
        r = pl.program_id(1)

        @pl.when(r == 0)
        def _():
            st[...] = jnp.zeros_like(st)

        _fold_store(xf_ref, _dot(u_ref[...], bd_ref[...]), tb, 2 * nt)
        ar, ai = a_ref[0:nt, :], a_ref[nt:2 * nt, :]

        def group(gi, carry):
            rr = (tb // SUBLANE - 1 - gi) if reverse else gi
            sr, si = carry
            for qq in range(SUBLANE):
                q = (SUBLANE - 1 - qq) if reverse else qq
                re_rows, im_rows = pl.ds(q, nt, stride=SUBLANE), pl.ds(nt * SUBLANE + q, nt, stride=SUBLANE)
                nr = ar * sr - ai * si + xf_ref[rr, re_rows, :]
                ni = ar * si + ai * sr + xf_ref[rr, im_rows, :]
                xf_ref[rr, re_rows, :] = nr
                xf_ref[rr, im_rows, :] = ni
                sr, si = nr, ni
            return sr, si

        fin = lax.fori_loop(0, tb // SUBLANE, group, (st[0:nt, :], st[nt:2 * nt, :]))
        st[0:nt, :] = fin[0]
        st[nt:2 * nt, :] = fin[1]
        y_ref[...] = _dot(_unfold(xf_ref, tb, 2 * nt), cd_ref[...], NT)

    def rows(r):
        return (nb - 1 - r) if reverse else r

    return pl.pallas_call(
        body, name=name, grid=(nch, nb),
        in_specs=[pl.BlockSpec((tb, cu), lambda c, r: (rows(r), c)), pl.BlockSpec((None, cu, hs2), lambda c, r: (c, 0, 0)),
                  pl.BlockSpec((None, cu, hs2), lambda c, r: (c, 0, 0)), pl.BlockSpec((None, 2 * nt, LANE), lambda c, r: (c, 0, 0))],
        out_specs=[pl.BlockSpec((tb // SUBLANE, frows, LANE), lambda c, r: (rows(r), c, 0)), pl.BlockSpec((tb, cu), lambda c, r: (rows(r), c))],
        out_shape=[_sds((l // SUBLANE, nch * frows, LANE), F32), _sds((l, nch * cu), F32)],
        scratch_shapes=[pltpu.VMEM((2 * nt, LANE), F32)],
        compiler_params=_params(("arbitrary", "arbitrary")),
    )(proj, bd, cd, a_f)


def _s5_bwd(name, proj, dy, xf, bd, cd, a_conj, reverse):
    l = proj.shape[0]
    nch, cu, hs2 = bd.shape
    nt = hs2 // (2 * LANE)
    frows = 2 * nt * SUBLANE
    tb = _tile(l, 512, SUBLANE)
    nb = l // tb

    def body(u_ref, dy_ref, xs_ref, bd_ref, cd_ref, a_ref, du_ref, dbd_ref, dcd_ref, da_ref, lam, st):
        r = pl.program_id(1)

        @pl.when(r == 0)
        def _():
            st[...] = jnp.zeros_like(st)
            dbd_ref[...] = jnp.zeros_like(dbd_ref)
            dcd_ref[...] = jnp.zeros_like(dcd_ref)
            da_ref[...] = jnp.zeros_like(da_ref)

        dyv = dy_ref[...]
        _fold_store(lam, _dot(dyv, cd_ref[...]), tb, 2 * nt)
        ar, ai = a_ref[0:nt, :], a_ref[nt:2 * nt, :]

        def group(gi, carry):
            rr = (tb // SUBLANE - 1 - gi) if reverse else gi
            sr, si, cr, ci = carry
            for qq in range(SUBLANE):
                q = (SUBLANE - 1 - qq) if reverse else qq
                re_rows, im_rows = pl.ds(q, nt, stride=SUBLANE), pl.ds(nt * SUBLANE + q, nt, stride=SUBLANE)
                xr, xi = xs_ref[rr, re_rows, :], xs_ref[rr, im_rows, :]
                cr = cr + sr * xr + si * xi
                ci = ci + si * xr - sr * xi
                nr = ar * sr - ai * si + lam[rr, re_rows, :]
                ni = ar * si + ai * sr + lam[rr, im_rows, :]
                lam[rr, re_rows, :] = nr
                lam[rr, im_rows, :] = ni
                sr, si = nr, ni
            return sr, si, cr, ci

        zero = jnp.zeros((nt, LANE), F32)
        fin = lax.fori_loop(0, tb // SUBLANE, group, (st[0:nt, :], st[nt:2 * nt, :], zero, zero))
        st[0:nt, :] = fin[0]
        st[nt:2 * nt, :] = fin[1]
        da_ref[0:nt, :] += fin[2]
        da_ref[nt:2 * nt, :] += fin[3]
        lam_u = _unfold(lam, tb, 2 * nt)
        du_ref[...] = _dot(lam_u, bd_ref[...], NT)
        dbd_ref[...] += _dot(u_ref[...], lam_u, TN)
        dcd_ref[...] += _dot(dyv, _unfold(xs_ref, tb, 2 * nt), TN)

    def rows(r):
        return (nb - 1 - r) if reverse else r

    chunk_rows = pl.BlockSpec((tb, cu), lambda c, r: (rows(r), c))
    bd_spec = pl.BlockSpec((None, cu, hs2), lambda c, r: (c, 0, 0))
    cd_spec = bd_spec
    a_spec = pl.BlockSpec((None, 2 * nt, LANE), lambda c, r: (c, 0, 0))
    return pl.pallas_call(
        body, name=name, grid=(nch, nb),
        in_specs=[chunk_rows, chunk_rows, pl.BlockSpec((tb // SUBLANE, frows, LANE), lambda c, r: (rows(r), c, 0)), bd_spec, cd_spec, a_spec],
        out_specs=[chunk_rows, bd_spec, cd_spec, a_spec],
        out_shape=[_sds((l, nch * cu), F32), _sds((nch, cu, hs2), F32), _sds((nch, cu, hs2), F32), _sds((nch, 2 * nt, LANE), F32)],
        scratch_shapes=[pltpu.VMEM((tb // SUBLANE, frows, LANE), F32), pltpu.VMEM((2 * nt, LANE), F32)],
        compiler_params=_params(("arbitrary", "arbitrary")),
    )(proj, dy, xf, bd, cd, a_conj)


def _s5_du(name, du_f, du_b, dy, proj, d_row):
    l, w = dy.shape
    tm = _tile(l, 1024, SUBLANE)

    def body(f_ref, b_ref, dy_ref, u_ref, d_ref, du_ref, dd_ref):
        i = pl.program_id(0)
        dyv = dy_ref[...]
        du_ref[...] = (f_ref[...] + b_ref[...] + dyv * d_ref[...]).astype(du_ref.dtype)
        part = jnp.sum(dyv * u_ref[...].astype(F32), axis=0, keepdims=True)

        @pl.when(i == 0)
        def _():
            dd_ref[...] = part

        @pl.when(i > 0)
        def _():
            dd_ref[...] += part

    row = pl.BlockSpec((tm, w), lambda i: (i, 0))
    vec = pl.BlockSpec((1, w), lambda i: (0, 0))
    return pl.pallas_call(
        body, name=name, grid=(l // tm,), in_specs=[row, row, row, row, vec], out_specs=[row, vec],
        out_shape=[_sds((l, w), BF), _sds((1, w), F32)],
        compiler_params=_params(("arbitrary",)),
    )(du_f, du_b, dy, proj, d_row)


def _gelu(y):
    c = math.sqrt(2.0 / math.pi)
    return 0.5 * y * (1.0 + jnp.tanh(c * (y + 0.044715 * y * y * y)))


def _gelu_grad(y):
    c = math.sqrt(2.0 / math.pi)
    th = jnp.tanh(c * (y + 0.044715 * y * y * y))
    return 0.5 * (1.0 + th) + 0.5 * y * (1.0 - th * th) * c * (1.0 + 3.0 * 0.044715 * y * y)


def _glu_fwd(name, y_f, y_b, proj, d_row, w):
    l, wd = y_f.shape
    tm = _tile(l, 512, SUBLANE)

    def body(yf_ref, yb_ref, u_ref, d_ref, w_ref, y_ref, o_ref):
        y = yf_ref[...] + yb_ref[...] + u_ref[...].astype(F32) * d_ref[...]
        y_ref[...] = y
        gy = _gelu(y)
        z = _dot(gy, w_ref[...])
        o_ref[...] = (gy * _sigmoid(z)).astype(o_ref.dtype)

    row = pl.BlockSpec((tm, wd), lambda i: (i, 0))
    return pl.pallas_call(
        body, name=name, grid=(l // tm,),
        in_specs=[row, row, row, pl.BlockSpec((1, wd), lambda i: (0, 0)), pl.BlockSpec((wd, wd), lambda i: (0, 0))],
        out_specs=[row, row], out_shape=[_sds((l, wd), F32), _sds((l, wd), BF)],
        compiler_params=_params(("arbitrary",)),
    )(y_f, y_b, proj, d_row, w)


def _glu_bwd(name, y, w, dout, dcol):
    l, wd = y.shape
    tm = _tile(l, 512, SUBLANE)

    def body(y_ref, w_ref, d_ref, dy_ref, dw_ref):
        i = pl.program_id(0)
        yv = y_ref[...]
        gy = _gelu(yv)
        s = _sigmoid(_dot(gy, w_ref[...]))
        d = d_ref[...].astype(F32)
        t = d * gy * s * (1.0 - s)
        dgy = d * s + _dot(t, w_ref[...], NT)
        dy_ref[...] = dgy * _gelu_grad(yv)
        part = _dot(gy, t, TN)

        @pl.when(i == 0)
        def _():
            dw_ref[...] = part

        @pl.when(i > 0)
        def _():
            dw_ref[...] += part

    return pl.pallas_call(
        body, name=name, grid=(l // tm,),
        in_specs=[pl.BlockSpec((tm, wd), lambda i: (i, 0)), pl.BlockSpec((wd, wd), lambda i: (0, 0)),
                  pl.BlockSpec((tm, wd), lambda i: (i, dcol))],
        out_specs=[pl.BlockSpec((tm, wd), lambda i: (i, 0)), pl.BlockSpec((wd, wd), lambda i: (0, 0))],
        out_shape=[_sds((l, wd), F32), _sds((wd, wd), F32)],
        compiler_params=_params(("arbitrary",)),
    )(y, w, dout)


def _log_sigmoid(x):
    return jnp.minimum(x, 0.0) - jnp.log(1.0 + jnp.exp(-jnp.abs(x)))


def _gate_fwd(name, glo, wf, wb, bf, bb):
    l, r2 = glo.shape
    hk = wf.shape[1]
    tm = _tile(l, 1024, SUBLANE)

    def body(x_ref, wf_ref, wb_ref, bf_ref, bb_ref, gf_ref, gb_ref):
        xv = x_ref[...]
        gf_ref[...] = _log_sigmoid(_dot(xv, wf_ref[...]) + bf_ref[...]) * (1.0 / GLA_GATE_NORM)
        gb_ref[...] = _log_sigmoid(_dot(xv, wb_ref[...]) + bb_ref[...]) * (1.0 / GLA_GATE_NORM)

    w_spec = pl.BlockSpec((r2, hk), lambda i: (0, 0))
    b_spec = pl.BlockSpec((1, hk), lambda i: (0, 0))
    o_spec = pl.BlockSpec((tm, hk), lambda i: (i, 0))
    return pl.pallas_call(
        body, name=name, grid=(l // tm,),
        in_specs=[pl.BlockSpec((tm, r2), lambda i: (i, 0)), w_spec, w_spec, b_spec, b_spec],
        out_specs=[o_spec, o_spec], out_shape=[_sds((l, hk), F32), _sds((l, hk), F32)],
        compiler_params=_params(("arbitrary",)),
    )(glo, wf, wb, bf, bb)


def _gate_bwd(name, glo, wf, wb, bf, bb, dgf, dgb):
    l, r2 = glo.shape
    hk = wf.shape[1]
    tm = _tile(l, 1024, SUBLANE)

    def body(x_ref, wf_ref, wb_ref, bf_ref, bb_ref, dgf_ref, dgb_ref, dx_ref, dwf_ref, dwb_ref, dbf_ref, dbb_ref):
        i = pl.program_id(0)
        xv = x_ref[...]
        kf = _dot(xv, wf_ref[...]) + bf_ref[...]
        kb = _dot(xv, wb_ref[...]) + bb_ref[...]
        dkf = dgf_ref[...] * (1.0 / GLA_GATE_NORM) * _sigmoid(-kf)
        dkb = dgb_ref[...] * (1.0 / GLA_GATE_NORM) * _sigmoid(-kb)
        dx_ref[...] = _dot(dkf, wf_ref[...], NT) + _dot(dkb, wb_ref[...], NT)
        parts = (_dot(xv, dkf, TN), _dot(xv, dkb, TN), jnp.sum(dkf, axis=0, keepdims=True), jnp.sum(dkb, axis=0, keepdims=True))
        accs = (dwf_ref, dwb_ref, dbf_ref, dbb_ref)

        @pl.when(i == 0)
        def _():
            for a_, p_ in zip(accs, parts):
                a_[...] = p_

        @pl.when(i > 0)
        def _():
            for a_, p_ in zip(accs, parts):
                a_[...] += p_

    w_spec = pl.BlockSpec((r2, hk), lambda i: (0, 0))
    b_spec = pl.BlockSpec((1, hk), lambda i: (0, 0))
    g_spec = pl.BlockSpec((tm, hk), lambda i: (i, 0))
    x_spec = pl.BlockSpec((tm, r2), lambda i: (i, 0))
    return pl.pallas_call(
        body, name=name, grid=(l // tm,),
        in_specs=[x_spec, w_spec, w_spec, b_spec, b_spec, g_spec, g_spec],
        out_specs=[x_spec, w_spec, w_spec, b_spec, b_spec],
        out_shape=[_sds((l, r2), F32), _sds((r2, hk), F32), _sds((r2, hk), F32), _sds((1, hk), F32), _sds((1, hk), F32)],
        compiler_params=_params(("arbitrary",)),
    )(glo, wf, wb, bf, bb, dgf, dgb)


def _chunk_terms(qc, kc, lg, chunk, reverse):
    ri = lax.broadcasted_iota(jnp.int32, (chunk, chunk), 0)
    ci = lax.broadcasted_iota(jnp.int32, (chunk, chunk), 1)
    mask = (ci > ri) if reverse else (ci <= ri)
    pos = lax.broadcasted_iota(jnp.int32, (chunk, 1), 0).astype(F32)
    cum = ((chunk - pos) if reverse else (pos + 1.0)) * lg
    last = chunk * lg
    e = jnp.exp(cum)
    einv = jnp.exp(-cum)
    dec = jnp.exp(last - cum)
    return e, einv, dec, qc * e, kc * einv, kc * dec, jnp.exp(last), mask


def _lin_specs(width, col, tb, nb, reverse):
    return pl.BlockSpec((tb, width), lambda h, r: ((nb - 1 - r) if reverse else r, col + h))


def _lin_fwd(name, q, k, v, lgtab, prev_o=None, *, heads, hb, dk, dv, chunk, tb, qcol, kcol, vcol, reverse):
    l = q.shape[0]
    nb, ncb, ng = l // tb, tb // chunk, heads // hb

    def body(q_ref, k_ref, v_ref, lg_ref, *rest):
        p_ref = rest[0] if prev_o is not None else None
        o_ref, sp_ref, st = rest[-3:]

        @pl.when(pl.program_id(1) == 0)
        def _():
            st[...] = jnp.zeros_like(st)

        for c in range(ncb):
            cc = (ncb - 1 - c) if reverse else c
            rows = pl.ds(cc * chunk, chunk)
            for h in range(hb):
                ks, vs = slice(h * dk, (h + 1) * dk), slice(h * dv, (h + 1) * dv)
                qc, kc, vc = q_ref[rows, ks].astype(F32), k_ref[rows, ks].astype(F32), v_ref[rows, vs]
                _, _, _, qd, ki, kdec, e_last, mask = _chunk_terms(qc, kc, lg_ref[h, :, 0:1], chunk, reverse)
                a = jnp.where(mask, _dot(qd, ki, NT), 0.0)
                s_t = st[h]
                oc = _dot(a, vc) + _dot(qd, s_t, NT)
                if p_ref is not None:
                    oc = oc + p_ref[rows, vs]
                o_ref[rows, vs] = oc.astype(o_ref.dtype)
                sp_ref[cc, h] = s_t
                st[h] = s_t * e_last + _dot(vc, kdec, TN)

    o_spec = _lin_specs(hb * dv, 0, tb, nb, reverse)
    extra = [] if prev_o is None else [prev_o]
    return pl.pallas_call(
        body, name=name, grid=(ng, nb),
        in_specs=[_lin_specs(hb * dk, qcol, tb, nb, reverse), _lin_specs(hb * dk, kcol, tb, nb, reverse),
                  _lin_specs(hb * dv, vcol, tb, nb, reverse), pl.BlockSpec((hb, 1, LANE), lambda h, r: (h, 0, 0))] + [o_spec] * len(extra),
        out_specs=[o_spec, pl.BlockSpec((ncb, hb, dv, dk), lambda h, r: ((nb - 1 - r) if reverse else r, h, 0, 0))],
        out_shape=[_sds((l, heads * dv), F32 if prev_o is None else BF), _sds((l // chunk, heads, dv, dk), F32)],
        scratch_shapes=[pltpu.VMEM((hb, dv, dk), F32)],
        compiler_params=_params(("arbitrary", "arbitrary")),
    )(q, k, v, lgtab, *extra)


def _lin_bwd(name, q, k, v, lgtab, sprev, do, prev, *, heads, hb, dk, dv, chunk, tb, qcol, kcol, vcol, reverse):
    l = q.shape[0]
    nb, ncb, ng = l // tb, tb // chunk, heads // hb
    brev = not reverse
    n_prev = 0 if prev is None else len(prev)

    def body(q_ref, k_ref, v_ref, lg_ref, sp_ref, do_ref, *rest):
        p_refs = rest[:n_prev]
        dq_ref, dk_ref, dv_ref, dst = rest[n_prev:]

        @pl.when(pl.program_id(1) == 0)
        def _():
            dst[...] = jnp.zeros_like(dst)

        for c in range(ncb):
            cc = (ncb - 1 - c) if brev else c
            rows = pl.ds(cc * chunk, chunk)
            for h in range(hb):
                ks, vs = slice(h * dk, (h + 1) * dk), slice(h * dv, (h + 1) * dv)
                qc, kc, vc = q_ref[rows, ks].astype(F32), k_ref[rows, ks].astype(F32), v_ref[rows, vs]
                e, einv, dec, qd, ki, kdec, e_last, mask = _chunk_terms(qc, kc, lg_ref[h, :, 0:1], chunk, reverse)
                a = jnp.where(mask, _dot(qd, ki, NT), 0.0)
                s_t, ds_t, doc = sp_ref[cc, h], dst[h], do_ref[rows, vs]
                dvc = _dot(a, doc, TN) + _dot(kdec, ds_t, NT)
                da = jnp.where(mask, _dot(doc, vc, NT), 0.0)
                dqc = (_dot(da, ki) + _dot(doc, s_t)) * e
                dkc = _dot(da, qd, TN) * einv + _dot(vc, ds_t) * dec
                dst[h] = ds_t * e_last + _dot(doc, qd, TN)
                if n_prev:
                    dqc = dqc + p_refs[0][rows, ks]
                    dkc = dkc + p_refs[1][rows, ks]
                    dvc = dvc + p_refs[2][rows, vs]
                dq_ref[rows, ks] = dqc
                dk_ref[rows, ks] = dkc
                dv_ref[rows, vs] = dvc.astype(dv_ref.dtype)

    k_spec, v_spec = _lin_specs(hb * dk, 0, tb, nb, brev), _lin_specs(hb * dv, 0, tb, nb, brev)
    in_specs = [_lin_specs(hb * dk, qcol, tb, nb, brev), _lin_specs(hb * dk, kcol, tb, nb, brev), _lin_specs(hb * dv, vcol, tb, nb, brev),
                pl.BlockSpec((hb, 1, LANE), lambda h, r: (h, 0, 0)),
                pl.BlockSpec((ncb, hb, dv, dk), lambda h, r: ((nb - 1 - r) if brev else r, h, 0, 0)), v_spec]
    args = [q, k, v, lgtab, sprev, do]
    if n_prev:
        in_specs += [k_spec, k_spec, v_spec]
        args += list(prev)
    return pl.pallas_call(
        body, name=name, grid=(ng, nb), in_specs=in_specs, out_specs=[k_spec, k_spec, v_spec],
        out_shape=[_sds((l, heads * dk), F32), _sds((l, heads * dk), F32), _sds((l, heads * dv), BF if n_prev else F32)],
        scratch_shapes=[pltpu.VMEM((hb, dv, dk), F32)],
        compiler_params=_params(("arbitrary", "arbitrary")),
    )(*args)


def _log2(n):
    assert n & (n - 1) == 0, "a power of two"
    return n.bit_length() - 1


def _gla_block_terms(q, k, g, qscale, chunk, tb, reverse):
    ri = lax.broadcasted_iota(jnp.int32, (tb, tb), 0)
    ci = lax.broadcasted_iota(jnp.int32, (tb, tb), 1)
    same = jnp.right_shift(ri, _log2(chunk)) == jnp.right_shift(ci, _log2(chunk))
    t_in = jnp.logical_and(same, (ci >= ri) if reverse else (ci <= ri)).astype(F32)
    cum = _dot3(t_in, g)
    tot = _dot3(same.astype(F32), g)
    e = jnp.exp(cum)
    einv = jnp.exp(-cum)
    dec = jnp.exp(tot - cum)
    return e, einv, dec, jnp.exp(tot), q * (qscale * e), k * einv, k * dec, t_in


def _gla_masks(hk, dk, heads, chunk, reverse):
    lane = lax.broadcasted_iota(jnp.int32, (1, hk), 1)
    head_of = jnp.right_shift(lane, _log2(dk))
    ri = lax.broadcasted_iota(jnp.int32, (chunk, chunk), 0)
    ci = lax.broadcasted_iota(jnp.int32, (chunk, chunk), 1)
    return [head_of == h for h in range(heads)], ((ci > ri) if reverse else (ci <= ri))


def _gla_fwd(name, proj, g, prev_o=None, *, heads, dk, dv, chunk, tb, qcol, kcol, vcol, qscale, reverse):
    l = proj.shape[0]
    nb, ncb, hk, hv = l // tb, tb // chunk, heads * dk, heads * dv

    def body(q_ref, k_ref, v_ref, g_ref, *rest):
        p_ref = rest[0] if prev_o is not None else None
        o_ref, sp_ref, st = rest[-3:]

        @pl.when(pl.program_id(0) == 0)
        def _():
            st[...] = jnp.zeros_like(st)

        _, _, _, etot, qd, ki, kdec, _ = _gla_block_terms(q_ref[...].astype(F32), k_ref[...].astype(F32), g_ref[...], qscale, chunk, tb, reverse)
        heads_m, causal = _gla_masks(hk, dk, heads, chunk, reverse)
        s_all = st[...]
        for c in range(ncb):
            cc = (ncb - 1 - c) if reverse else c
            rc = slice(cc * chunk, (cc + 1) * chunk)
            qd_c, ki_c, kdec_c = qd[rc], ki[rc], kdec[rc]
            sp_ref[cc] = s_all
            kv = jnp.zeros_like(s_all)
            for h in range(heads):
                vs = slice(h * dv, (h + 1) * dv)
                qm = jnp.where(heads_m[h], qd_c, 0.0)
                a = jnp.where(causal, _dot(qm, ki_c, NT), 0.0)
                vc = v_ref[rc, vs]
                oc = _dot(a, vc) + _dot(qm, s_all, NT)
                if p_ref is not None:
                    oc = oc + p_ref[rc, vs]
                o_ref[rc, vs] = oc.astype(o_ref.dtype)
                kv = kv + jnp.where(heads_m[h], _dot(vc, kdec_c, TN), 0.0)
            s_all = s_all * etot[rc][0:1, :] + kv
        st[...] = s_all

    def rows(r):
        return (nb - 1 - r) if reverse else r

    o_spec = pl.BlockSpec((tb, hv), lambda r: (rows(r), 0))
    extra = [] if prev_o is None else [prev_o]
    return pl.pallas_call(
        body, name=name, grid=(nb,),
        in_specs=[pl.BlockSpec((tb, hk), lambda r: (rows(r), qcol)), pl.BlockSpec((tb, hk), lambda r: (rows(r), kcol)),
                  pl.BlockSpec((tb, hv), lambda r: (rows(r), vcol)), pl.BlockSpec((tb, hk), lambda r: (rows(r), 0))] + [o_spec] * len(extra),
        out_specs=[o_spec, pl.BlockSpec((ncb, dv, hk), lambda r: (rows(r), 0, 0))],
        out_shape=[_sds((l, hv), F32 if prev_o is None else BF), _sds((l // chunk, dv, hk), F32)],
        scratch_shapes=[pltpu.VMEM((dv, hk), F32)],
        compiler_params=_params(("arbitrary",)),
    )(proj, proj, proj, g, *extra)


def _gla_bwd(name, proj, g, sprev, do, prev, *, heads, dk, dv, chunk, tb, qcol, kcol, vcol, qscale, reverse):
    l = proj.shape[0]
    nb, ncb, hk, hv = l // tb, tb // chunk, heads * dk, heads * dv
    brev = not reverse
    n_prev = 0 if prev is None else len(prev)

    def body(q_ref, k_ref, v_ref, g_ref, sp_ref, do_ref, *rest):
        p_refs = rest[:n_prev]
        dq_ref, dk_ref, dv_ref, dg_ref, dst, dcs = rest[n_prev:]

        @pl.when(pl.program_id(0) == 0)
        def _():
            dst[...] = jnp.zeros_like(dst)

        e, einv, dec, etot, qd, ki, kdec, t_in = _gla_block_terms(q_ref[...].astype(F32), k_ref[...].astype(F32), g_ref[...], qscale, chunk, tb, reverse)
        heads_m, causal = _gla_masks(hk, dk, heads, chunk, reverse)
        last_row = lax.broadcasted_iota(jnp.int32, (chunk, 1), 0) == (0 if reverse else chunk - 1)
        ds_all = dst[...]
        for c in range(ncb):
            cc = (ncb - 1 - c) if brev else c
            rc = slice(cc * chunk, (cc + 1) * chunk)
            qd_c, ki_c, kdec_c = qd[rc], ki[rc], kdec[rc]
            s_all = sp_ref[cc]
            et = etot[rc][0:1, :]
            dqd = jnp.zeros((chunk, hk), F32)
            dki = jnp.zeros((chunk, hk), F32)
            dkdec = jnp.zeros((chunk, hk), F32)
            ds_add = jnp.zeros_like(ds_all)
            for h in range(heads):
                vs = slice(h * dv, (h + 1) * dv)
                m = heads_m[h]
                qm = jnp.where(m, qd_c, 0.0)
                a = jnp.where(causal, _dot(qm, ki_c, NT), 0.0)
                doc, vc = do_ref[rc, vs], v_ref[rc, vs]
                dvc = _dot(a, doc, TN) + _dot(jnp.where(m, kdec_c, 0.0), ds_all, NT)
                if n_prev:
                    dvc = dvc + p_refs[2][rc, vs]
                dv_ref[rc, vs] = dvc.astype(dv_ref.dtype)
                da = jnp.where(causal, _dot(doc, vc, NT), 0.0)
                dqd = dqd + jnp.where(m, _dot(da, ki_c) + _dot(doc, s_all), 0.0)
                dki = dki + _dot(da, qm, TN)
                dkdec = dkdec + jnp.where(m, _dot(vc, ds_all), 0.0)
                ds_add = ds_add + _dot(doc, qm, TN)
            dqc = dqd * e[rc] * qscale
            dkc = dki * einv[rc] + dkdec * dec[rc]
            if n_prev:
                dqc = dqc + p_refs[0][rc, :]
                dkc = dkc + p_refs[1][rc, :]
            dq_ref[rc, :] = dqc
            dk_ref[rc, :] = dkc
            dlast = jnp.sum(dkdec * kdec_c, axis=0, keepdims=True) + et * jnp.sum(s_all * ds_all, axis=0, keepdims=True)
            dcs[rc, :] = dqd * qd_c - dki * ki_c - dkdec * kdec_c + jnp.where(last_row, dlast, 0.0)
            ds_all = ds_all * et + ds_add
        dst[...] = ds_all
        dg_ref[...] = _dot3(t_in, dcs[...], TN)

    def rows(r):
        return (nb - 1 - r) if brev else r

    k_spec = pl.BlockSpec((tb, hk), lambda r: (rows(r), 0))
    v_spec = pl.BlockSpec((tb, hv), lambda r: (rows(r), 0))
    in_specs = [pl.BlockSpec((tb, hk), lambda r: (rows(r), qcol)), pl.BlockSpec((tb, hk), lambda r: (rows(r), kcol)),
                pl.BlockSpec((tb, hv), lambda r: (rows(r), vcol)), k_spec,
                pl.BlockSpec((ncb, dv, hk), lambda r: (rows(r), 0, 0)), v_spec]
    args = [proj, proj, proj, g, sprev, do]
    if n_prev:
        in_specs += [k_spec, k_spec, v_spec]
        args += list(prev)
    return pl.pallas_call(
        body, name=name, grid=(nb,), in_specs=in_specs, out_specs=[k_spec, k_spec, v_spec, k_spec],
        out_shape=[_sds((l, hk), F32), _sds((l, hk), F32), _sds((l, hv), BF if n_prev else F32), _sds((l, hk), F32)],
        scratch_shapes=[pltpu.VMEM((dv, hk), F32), pltpu.VMEM((tb, hk), F32)],
        compiler_params=_params(("arbitrary",)),
    )(*args)


def _headgate_fwd(name, o_sum, og_arr, og_col, gn, dv):
    l, w = o_sum.shape
    tm = _tile(l, 512, SUBLANE)
    nh = w // dv

    def body(o_ref, og_ref, gn_ref, out_ref):
        for h in range(nh):
            cs = slice(h * dv, (h + 1) * dv)
            o = o_ref[:, cs].astype(F32)
            r = lax.rsqrt(jnp.mean(o * o, axis=-1, keepdims=True) + EPS)
            og = og_ref[:, cs].astype(F32)
            out_ref[:, cs] = (o * r * gn_ref[:, cs] * (og * _sigmoid(og))).astype(out_ref.dtype)

    row = pl.BlockSpec((tm, w), lambda i: (i, 0))
    return pl.pallas_call(
        body, name=name, grid=(l // tm,),
        in_specs=[row, pl.BlockSpec((tm, w), lambda i: (i, og_col)), pl.BlockSpec((1, w), lambda i: (0, 0))],
        out_specs=row, out_shape=_sds((l, w), BF),
        compiler_params=_params(("arbitrary",)),
    )(o_sum, og_arr, gn)


def _headgate_bwd(name, o_sum, og_arr, og_col, gn, dout, dcol, dv):
    l, w = o_sum.shape
    tm = _tile(l, 512, SUBLANE)
    nh = w // dv

    def body(o_ref, og_ref, gn_ref, d_ref, do_ref, dog_ref, dgn_ref):
        i = pl.program_id(0)
        for h in range(nh):
            cs = slice(h * dv, (h + 1) * dv)
            o = o_ref[:, cs].astype(F32)
            r = lax.rsqrt(jnp.mean(o * o, axis=-1, keepdims=True) + EPS)
            oh = o * r
            og = og_ref[:, cs].astype(F32)
            s = _sigmoid(og)
            d = d_ref[:, cs].astype(F32)
            gnv = gn_ref[:, cs]
            d_on = d * (og * s)
            dog_ref[:, cs] = (d * (oh * gnv) * s * (1.0 + og * (1.0 - s))).astype(dog_ref.dtype)
            doh = d_on * gnv
            do_ref[:, cs] = (r * (doh - oh * jnp.mean(doh * oh, axis=-1, keepdims=True))).astype(do_ref.dtype)
            part = jnp.sum(d_on * oh, axis=0, keepdims=True)

            @pl.when(i == 0)
            def _():
                dgn_ref[:, cs] = part

            @pl.when(i > 0)
            def _():
                dgn_ref[:, cs] += part

    row = pl.BlockSpec((tm, w), lambda i: (i, 0))
    vec = pl.BlockSpec((1, w), lambda i: (0, 0))
    return pl.pallas_call(
        body, name=name, grid=(l // tm,),
        in_specs=[row, pl.BlockSpec((tm, w), lambda i: (i, og_col)), vec, pl.BlockSpec((tm, w), lambda i: (i, dcol))],
        out_specs=[row, row, vec], out_shape=[_sds((l, w), BF), _sds((l, w), BF), _sds((1, w), F32)],
        compiler_params=_params(("arbitrary",)),
    )(o_sum, og_arr, gn, dout)


def _rot_tables(l, dk):
    half = dk // 2
    pos = jnp.arange(l, dtype=F32)
    inv = jnp.exp(-math.log(ROPE_BASE) * jnp.arange(half, dtype=F32) / half)
    ang = pos[:, None] * inv[None, :]
    cos, sin = jnp.cos(ang), jnp.sin(ang)
    return jnp.concatenate([cos, cos], axis=-1), jnp.concatenate([-sin, sin], axis=-1)


def _rot_apply(name, src_q, qcol, src_k, kcol, cos_t, sin_t, heads, dk, kscale, out_dtype, transpose):
    l = src_q.shape[0]
    w = heads * dk
    tm = _tile(l, 512, SUBLANE)

    def rot(t, cos_v, sin_v):
        if transpose:
            return t * cos_v + pltpu.roll(t * sin_v, dk // 2, 1)
        return t * cos_v + pltpu.roll(t, dk // 2, 1) * sin_v

    def body(q_ref, k_ref, c_ref, s_ref, qo_ref, ko_ref):
        cos_v, sin_v = c_ref[...], s_ref[...]
        for h in range(heads):
            cs = slice(h * dk, (h + 1) * dk)
            qo_ref[:, cs] = rot(q_ref[:, cs].astype(F32), cos_v, sin_v).astype(out_dtype)
            ko_ref[:, cs] = (rot(k_ref[:, cs].astype(F32), cos_v, sin_v) * kscale).astype(out_dtype)

    tab = pl.BlockSpec((tm, dk), lambda i: (i, 0))
    row = pl.BlockSpec((tm, w), lambda i: (i, 0))
    return pl.pallas_call(
        body, name=name, grid=(l // tm,),
        in_specs=[pl.BlockSpec((tm, w), lambda i: (i, qcol)), pl.BlockSpec((tm, w), lambda i: (i, kcol)), tab, tab],
        out_specs=[row, row], out_shape=[_sds((l, w), out_dtype), _sds((l, w), out_dtype)],
        compiler_params=_params(("arbitrary",)),
    )(src_q, src_k, cos_t, sin_t)


def _peer_copies(src_ref, out_ref, send_sems, recv_sems, gather):
    x, y, c = lax.axis_index("x"), lax.axis_index("y"), lax.axis_index("c")
    me = 4 * x + 2 * y + c
    copies = []
    for kk in range(1, N_DEV):
        px = (1 - x) if kk & 4 else x
        py = (1 - y) if kk & 2 else y
        pc = (1 - c) if kk & 1 else c
        peer = 4 * px + 2 * py + pc
        copies.append(pltpu.make_async_remote_copy(
            src_ref=src_ref if gather else src_ref.at[peer], dst_ref=out_ref.at[me],
            send_sem=send_sems.at[kk - 1], recv_sem=recv_sems.at[kk - 1],
            device_id=(px, py, pc), device_id_type=pl.DeviceIdType.MESH))
    return copies


_HBM = pl.BlockSpec(memory_space=pltpu.HBM)
_SEM = pl.BlockSpec(memory_space=pltpu.SEMAPHORE)
_EFFECT = pltpu.SideEffectType.DATAFLOW_SIDE_EFFECTING


def _exchange_start(name, srcs, gather):
    n = len(srcs)
    lands = [lax.empty((N_DEV,) + tuple(s.shape if gather else s.shape[1:]), s.dtype) for s in srcs]

    def body(*refs):
        src_refs, land_refs = refs[:n], refs[n:2 * n]
        send, recv = refs[2 * n:3 * n], refs[3 * n:4 * n]
        token = refs[-1]
        for k in range(n):
            for cp in _peer_copies(src_refs[k], land_refs[k], send[k], recv[k], gather):
                cp.start()
        token[...] = jnp.zeros_like(token)

    sem = pltpu.SemaphoreType.DMA((N_DEV - 1,))
    outs = pl.pallas_call(
        body, name=name,
        out_shape=tuple([sem] * (2 * n) + [pltpu.HBM(s.shape, s.dtype) for s in srcs] + [pltpu.HBM(a.shape, a.dtype) for a in lands]
                        + [_sds((SUBLANE, LANE), F32)]),
        in_specs=tuple([_HBM] * (2 * n)), out_specs=tuple([_SEM] * (2 * n) + [_HBM] * (2 * n) + [pl.BlockSpec(memory_space=pltpu.VMEM)]),
        input_output_aliases={k: 2 * n + k for k in range(2 * n)},
        compiler_params=pltpu.CompilerParams(has_side_effects=_EFFECT),
    )(*[pltpu.with_memory_space_constraint(a, pltpu.HBM) for a in list(srcs) + lands])
    return [(outs[k], outs[n + k], outs[2 * n + k], outs[3 * n + k], outs[-1]) for k in range(n)]


def _exchange_wait(name, started, gather, after):
    send_sems, recv_sems, src_thru, land_thru, _ = started

    def body(src_ref, land_ref, send_sems, recv_sems, after_ref, src_out, land_out):
        copies = _peer_copies(src_ref, land_ref, send_sems, recv_sems, gather)
        for cp in copies:
            cp.wait_send()
        for cp in copies:
            cp.wait_recv()

    return pl.pallas_call(
        body, name=name,
        out_shape=(pltpu.HBM(src_thru.shape, src_thru.dtype), pltpu.HBM(land_thru.shape, land_thru.dtype)),
        in_specs=(_HBM, _HBM, _SEM, _SEM, pl.BlockSpec(memory_space=pl.ANY)), out_specs=(_HBM, _HBM),
        input_output_aliases={0: 0, 1: 1},
        compiler_params=pltpu.CompilerParams(has_side_effects=_EFFECT),
    )(src_thru, land_thru, send_sems, recv_sems, after)


def _adam_math(w, gsum, m, v):
    m2 = ADAM_B1 * m + (1.0 - ADAM_B1) * gsum
    v2 = ADAM_B2 * v + (1.0 - ADAM_B2) * (gsum * gsum)
    m_hat = m2 / (1.0 - ADAM_B1 ** ADAM_STEP)
    v_hat = v2 / (1.0 - ADAM_B2 ** ADAM_STEP)
    delta = -ADAM_LR * (m_hat / (jnp.sqrt(v_hat) + ADAM_EPS) + ADAM_WD * w)
    return delta, m2, v2


def _reduce_adam(name, parts, w, m, v):
    nl, r, c = w.shape
    tr = _tile(r, 256, 16)
    nr = r // tr

    def body(*refs):
        p_refs = refs[:nl]
        w_ref, m_ref, v_ref, g_ref, d_ref, m2_ref, v2_ref = refs[nl:]
        for li in range(nl):
            @pl.when(pl.program_id(0) == li)
            def _(p_ref=p_refs[li]):
                gsum = p_ref[0].astype(F32)
                for s in range(1, N_DEV):
                    gsum = gsum + p_ref[s].astype(F32)
                g_ref[...] = gsum
                delta, m2, v2 = _adam_math(w_ref[...], gsum, m_ref[...], v_ref[...])
                d_ref[...] = delta
                m2_ref[...] = m2
                v2_ref[...] = v2

    def part_spec(li):
        return pl.BlockSpec((N_DEV, tr, c), lambda lay, i: (0, jnp.where(lay == li, i, jnp.where(lay < li, 0, nr - 1)), 0))

    row = pl.BlockSpec((None, tr, c), lambda lay, i: (lay, i, 0))
    return pl.pallas_call(
        body, name=name, grid=(nl, nr),
        in_specs=[part_spec(li) for li in range(nl)] + [row, row, row],
        out_specs=[row, row, row, row], out_shape=[_sds((nl, r, c), F32)] * 4,
        compiler_params=_params(("arbitrary", "arbitrary")),
    )(*parts, w, m, v)


def _reduce8(name, parts):
    _, r, c = parts.shape

    def body(p_ref, g_ref):
        gsum = p_ref[0]
        for s in range(1, N_DEV):
            gsum = gsum + p_ref[s]
        g_ref[...] = gsum

    return pl.pallas_call(
        body, name=name, grid=(1,),
        in_specs=[pl.BlockSpec((N_DEV, r, c), lambda i: (0, 0, 0))],
        out_specs=pl.BlockSpec((r, c), lambda i: (0, 0)), out_shape=_sds((r, c), F32),
        compiler_params=_params(("arbitrary",)),
    )(parts)


def _adam_packed(name, w, g, m, v):
    r, c = w.shape

    def body(w_ref, g_ref, m_ref, v_ref, d_ref, m2_ref, v2_ref):
        delta, m2, v2 = _adam_math(w_ref[...], g_ref[...], m_ref[...], v_ref[...])
        d_ref[...] = delta
        m2_ref[...] = m2
        v2_ref[...] = v2

    spec = pl.BlockSpec((r, c), lambda i: (0, 0))
    return pl.pallas_call(
        body, name=name, grid=(1,), in_specs=[spec] * 4, out_specs=[spec] * 3, out_shape=[_sds((r, c), F32)] * 3,
        compiler_params=_params(("arbitrary",)),
    )(w, g, m, v)


def _pack(arrs):
    flat = jnp.concatenate([a.reshape(-1).astype(F32) for a in arrs])
    n = flat.shape[0]
    pad = (-n) % (SUBLANE * LANE)
    return jnp.pad(flat, (0, pad)).reshape(-1, LANE)


def _unpack(packed, like):
    flat = packed.reshape(-1)
    out, off = [], 0
    for a in like:
        n = math.prod(a.shape)
        out.append(flat[off:off + n].reshape(a.shape))
        off += n
    return out


def _row_blocks(full):
    return full.reshape(N_DEV, full.shape[0] // N_DEV, full.shape[1])


def _col_blocks(full):
    r, c = full.shape
    return full.reshape(r, N_DEV, c // N_DEV).transpose(1, 0, 2)


def kernel(x, ffn1_norm, ffn1_w1, ffn1_w2, mix_norm, ffn2_norm, ffn2_w1, ffn2_w2, ab_w_in, s5_lambda_re, s5_lambda_im, s5_b_re, s5_b_im, s5_c_re, s5_c_im, s5_log_dt, s5_d, s5_w_glu, gla_w_gk, gla_b_gk, gla_norm, ab_w_out, ret_w_in, ret_norm, ret_w_out, final_norm, loss_target, m_ffn1_norm, m_ffn1_w1, m_ffn1_w2, m_mix_norm, m_ffn2_norm, m_ffn2_w1, m_ffn2_w2, m_ab_w_in, m_s5_lambda_re, m_s5_lambda_im, m_s5_b_re, m_s5_b_im, m_s5_c_re, m_s5_c_im, m_s5_log_dt, m_s5_d, m_s5_w_glu, m_gla_w_gk, m_gla_b_gk, m_gla_norm, m_ab_w_out, m_ret_w_in, m_ret_norm, m_ret_w_out, m_final_norm, v_ffn1_norm, v_ffn1_w1, v_ffn1_w2, v_mix_norm, v_ffn2_norm, v_ffn2_w1, v_ffn2_w2, v_ab_w_in, v_s5_lambda_re, v_s5_lambda_im, v_s5_b_re, v_s5_b_im, v_s5_c_re, v_s5_c_im, v_s5_log_dt, v_s5_d, v_s5_w_glu, v_gla_w_gk, v_gla_b_gk, v_gla_norm, v_ab_w_out, v_ret_w_in, v_ret_norm, v_ret_w_out, v_final_norm):
    names = ['ffn1_norm', 'ffn1_w1', 'ffn1_w2', 'mix_norm', 'ffn2_norm', 'ffn2_w1', 'ffn2_w2', 'ab_w_in', 's5_lambda_re', 's5_lambda_im', 's5_b_re', 's5_b_im', 's5_c_re', 's5_c_im', 's5_log_dt', 's5_d', 's5_w_glu', 'gla_w_gk', 'gla_b_gk', 'gla_norm', 'ab_w_out', 'ret_w_in', 'ret_norm', 'ret_w_out', 'final_norm']
    loc = locals()
    W = {n: loc[n] for n in names}
    M = {n: loc["m_" + n] for n in names}
    V = {n: loc["v_" + n] for n in names}

    me = 4 * lax.axis_index("x") + 2 * lax.axis_index("y") + lax.axis_index("c")
    xs = x[0]
    tgt = loss_target[0]
    l, d = xs.shape
    depth = ffn1_norm.shape[0]

    pending, to_start = {}, []

    def start_gather(tag, shard):
        to_start.append((tag, shard))

    def finish_gather(tag, after):
        started, shard = pending.pop(tag)
        _, got = _exchange_wait("agw_" + tag, started, True, after)
        return lax.dynamic_update_index_in_dim(got, shard, me, 0)

    def finish_cols(tag, after):
        g = finish_gather(tag, after)
        return g.transpose(1, 0, 2).reshape(g.shape[1], -1)

    def finish_rows(tag, after):
        g = finish_gather(tag, after)
        return g.reshape(-1, g.shape[2])

    small_sharded = [gla_w_gk, gla_b_gk, ret_norm]
    for i in range(depth):
        j = i // 2
        start_gather(f"ffn1_w1_{i}", ffn1_w1[i].astype(BF))
        start_gather(f"ffn1_w2_{i}", ffn1_w2[i].astype(BF))
        if i % 2 == 0:
            start_gather(f"ab_w_in_{j}", ab_w_in[j].astype(BF))
            if i == 0:
                start_gather("small", _pack(small_sharded))
            start_gather(f"s5_w_glu_{j}", s5_w_glu[j].astype(BF))
            start_gather(f"ab_w_out_{j}", ab_w_out[j].astype(BF))
        else:
            start_gather(f"ret_w_in_{j}", ret_w_in[j].astype(BF))
            start_gather(f"ret_w_out_{j}", ret_w_out[j].astype(BF))
        start_gather(f"ffn2_w1_{i}", ffn2_w1[i].astype(BF))
        start_gather(f"ffn2_w2_{i}", ffn2_w2[i].astype(BF))
    for (tag, shard), started in zip(to_start, _exchange_start("ags_weights", [s_ for _, s_ in to_start], True)):
        pending[tag] = (started, shard)
    started_all = started[4][0, 0]
    full = {}

    s5w = s5_d.shape[1]
    g_s5, n_s5 = s5_lambda_re.shape[2], s5_lambda_re.shape[3]
    hs = min(SUBLANE * LANE, g_s5 * n_s5)
    gla_hk = gla_w_gk.shape[-1] * N_DEV
    gla_dk = gla_hk // GLA_HEADS
    gla_hv = gla_norm.shape[1]
    gla_dv = gla_hv // GLA_HEADS
    ret_hv = ret_norm.shape[1] * N_DEV
    ret_dv = ret_hv // RET_HEADS
    ret_hk = (ret_w_in.shape[2] * N_DEV - 2 * ret_hv) // 2
    ret_dk = ret_hk // RET_HEADS
    assert s5w == gla_hv and 2 * gla_hk == s5w, "column blocks of the mixer projection assume these widths"
    assert ret_hv == 2 * ret_hk
    main_w = s5w + 2 * gla_hk + 2 * gla_hv
    gla_tb = _tile(l, 256, GLA_CHUNK)
    ret_chunk = min(RET_CHUNK, l)

    cos_t, sin_t = _rot_tables(l, ret_dk)
    lg_f = jnp.log1p(-jnp.exp2(-5.0 - jnp.arange(RET_HEADS, dtype=F32)))
    lgtab_f = jnp.broadcast_to(lg_f[:, None, None], (RET_HEADS, 1, LANE))
    lgtab_b = jnp.broadcast_to(lg_f[::-1][:, None, None], (RET_HEADS, 1, LANE))
    s5_pre = {}
    for j in range((depth + 1) // 2):
        s5_args = (s5_lambda_re[j], s5_lambda_im[j], s5_b_re[j], s5_b_im[j], s5_c_re[j], s5_c_im[j], s5_log_dt[j])
        (a_tab, bd, cd), s5_vjp = jax.vjp(lambda *a: _s5_chunk_tables(*a, hs), *s5_args)
        s5_pre[j] = (a_tab, bd.astype(BF), cd.astype(BF), s5_vjp)
    tables_done = jnp.stack([cos_t[0, 0], sin_t[0, 0], lgtab_f[0, 0, 0], lgtab_b[0, 0, 0]]
                            + [t[0].reshape(-1)[0] + t[1].reshape(-1)[0].astype(F32) + t[2].reshape(-1)[0].astype(F32) for t in s5_pre.values()])

    saved = []
    cur = xs
    hn = _rms_fwd("l0_ffn1_norm", cur, ffn1_norm[0:1] + started_all)
    for i in range(depth):
        j = i // 2
        s = {}
        s['x0'] = cur

        def first_w1(after):
            if i == 0:
                after = jnp.concatenate([after[0, 0:1].astype(F32), tables_done])
            return finish_cols(f"ffn1_w1_{i}", after)

        cur, h, s['ffn1'], s['f1w1'], s['f1w2'] = _ffn_fwd(f"l{i}_ffn1", cur, hn, first_w1, lambda after: finish_rows(f"ffn1_w2_{i}", after),
                                                            mix_norm[i:i + 1])
        s['x1'] = cur
        s['h'] = h
        if i % 2 == 0:
            w_in = finish_cols(f"ab_w_in_{j}", cur)
            if i == 0:
                got = finish_gather("small", cur)
                flat, off, joined = got.reshape(N_DEV, -1), 0, []
                for a in small_sharded:
                    n = math.prod(a.shape)
                    blk = jnp.moveaxis(flat[:, off:off + n].reshape((N_DEV,) + a.shape), 0, -2)
                    joined.append(blk.reshape(a.shape[:-1] + (N_DEV * a.shape[-1],)))
                    off += n
                full['gla_w_gk'], full['gla_b_gk'], ret_norm_full = joined[0].astype(BF), joined[1], joined[2]
            s['w_glu'], s['w_out'] = finish_rows(f"s5_w_glu_{j}", cur), finish_rows(f"ab_w_out_{j}", cur)
            w_main, w_glo = w_in[:, :main_w], w_in[:, main_w:]
            proj = _mm_plain(f"l{i}_proj", h, w_main, NN, BF, tm=1024, tn=1024, tk=d, b_outer=True)
            glo = _mm_plain(f"l{i}_glo", h, w_glo, NN, F32, tm=1024, tn=2 * GLA_RANK, tk=d)
            a_tab, bd16, cd16, s5_vjp = s5_pre[j]
            tm = _tile(l, 512, SUBLANE)
            x_f, y_f = _s5_fwd(f"l{i}_s5_fwd_f", proj, bd16[0], cd16[0], a_tab[0], False)
            x_b, y_b = _s5_fwd(f"l{i}_s5_fwd_b", proj, bd16[1], cd16[1], a_tab[1], True)
            d_row = s5_d[j:j + 1]
            y, s5_out = _glu_fwd(f"l{i}_s5_glu", y_f, y_b, proj, d_row, s['w_glu'])
            zeros_r = jnp.zeros((GLA_RANK, gla_hk), BF)
            w_gk = full['gla_w_gk'][j]
            wgk_f = jnp.concatenate([w_gk[0], zeros_r], axis=0)
            wgk_b = jnp.concatenate([zeros_r, w_gk[1]], axis=0)
            b_gk = full['gla_b_gk'][j]
            g_f, g_b = _gate_fwd(f"l{i}_gla_gate", glo, wgk_f, wgk_b, b_gk[0:1], b_gk[1:2])
            qcol, kcol, vcol, ogcol = s5w // gla_hk, s5w // gla_hk + 1, (s5w + 2 * gla_hk) // gla_hv, (s5w + 2 * gla_hk) // gla_hv + 1
            lin_kw = dict(heads=GLA_HEADS, dk=gla_dk, dv=gla_dv, chunk=GLA_CHUNK, tb=gla_tb, qcol=qcol, kcol=kcol, vcol=vcol,
                          qscale=gla_dk ** -0.5)
            o_f, sp_f = _gla_fwd(f"l{i}_gla_fwd_f", proj, g_f, reverse=False, **lin_kw)
            o_b, sp_b = _gla_fwd(f"l{i}_gla_fwd_b", proj, g_b, o_f, reverse=True, **lin_kw)
            gla_out = _headgate_fwd(f"l{i}_gla_out", o_b, proj, ogcol, gla_norm[j:j + 1], gla_dv)
            w_out = s['w_out']

            row = pl.BlockSpec((tm, d), lambda ii, jj, kk: (ii, 0))
            vec = pl.BlockSpec((1, d), lambda ii, jj, kk: (0, 0))
            cur, hn = _mm(f"l{i}_mix_out",
                          [(s5_out, pl.BlockSpec((tm, s5w), lambda ii, jj, kk: (ii, 0)), w_out, pl.BlockSpec((s5w, d), lambda ii, jj, kk: (0, 0))),
                           (gla_out, pl.BlockSpec((tm, gla_hv), lambda ii, jj, kk: (ii, 0)), w_out, pl.BlockSpec((gla_hv, d), lambda ii, jj, kk: (1, 0)))],
                          NN, (l // tm, 1, 1), [_sds((l, d), F32), _sds((l, d), BF)], [row, row], (tm, d), _residual_epi(1.0, True),
                          [cur, ffn2_norm[i:i + 1]], [row, vec])
            s.update(proj=proj, glo=glo, s5_vjp=s5_vjp, a_tab=a_tab, bd16=bd16, cd16=cd16, x_f=x_f, x_b=x_b, y=y, s5_out=s5_out,
                     wgk_f=wgk_f, wgk_b=wgk_b, b_gk=b_gk, g_f=g_f, g_b=g_b, o_f=o_f, o_b=o_b, sp_f=sp_f, sp_b=sp_b, gla_out=gla_out,
                     w_main=w_main, w_glo=w_glo, lin_kw=lin_kw, ogcol=ogcol)
        else:
            w_in = finish_cols(f"ret_w_in_{j}", cur)
            s['w_in'], s['w_out'] = w_in, finish_rows(f"ret_w_out_{j}", cur)
            proj = _mm_plain(f"l{i}_proj", h, w_in, NN, BF, tm=1024, tn=1024, tk=d, b_outer=True)
            qr, kr = _rot_apply(f"l{i}_rot", proj, 0, proj, 1, cos_t, sin_t, RET_HEADS, ret_dk, ret_dk ** -0.5, BF, False)
            ret_hb = 4
            lin_kw = dict(heads=RET_HEADS, hb=ret_hb, dk=ret_dk, dv=ret_dv, chunk=ret_chunk, tb=_tile(l, 4 * ret_chunk, ret_chunk), qcol=0, kcol=0,
                          vcol=(2 * ret_hk) // (ret_hb * ret_dv))
            o_f, sp_f = _lin_fwd(f"l{i}_ret_fwd_f", qr, kr, proj, lgtab_f, reverse=False, **lin_kw)
            o_b, sp_b = _lin_fwd(f"l{i}_ret_fwd_b", qr, kr, proj, lgtab_b, o_f, reverse=True, **lin_kw)
            ogcol = (2 * ret_hk + ret_hv) // ret_hv
            r_out = _headgate_fwd(f"l{i}_ret_out", o_b, proj, ogcol, ret_norm_full, ret_dv)

            tm = _tile(l, 512, SUBLANE)
            cur, hn = _mm_plain(f"l{i}_mix_out", r_out, s['w_out'], NN, F32, tm=512, tn=d, tk=ret_hv, epi=_residual_epi(1.0, True),
                                eins=[cur, ffn2_norm[i:i + 1]], especs=[None, pl.BlockSpec((1, d), lambda ii, jj, kk: (0, 0))],
                                extra_outs=[_sds((l, d), BF)], extra_specs=[pl.BlockSpec((tm, d), lambda ii, jj, kk: (ii, 0))])
            s.update(proj=proj, qr=qr, kr=kr, o_f=o_f, o_b=o_b, sp_f=sp_f, sp_b=sp_b, r_out=r_out, lin_kw=lin_kw, ogcol=ogcol)
        s['x2'] = cur
        cur, hn, s['ffn2'], s['f2w1'], s['f2w2'] = _ffn_fwd(f"l{i}_ffn2", cur, hn, lambda after: finish_cols(f"ffn2_w1_{i}", after),
                                                             lambda after: finish_rows(f"ffn2_w2_{i}", after),
                                                             ffn1_norm[i + 1:i + 2] if i + 1 < depth else None)
        saved.append(s)

    dx, d_final_norm, loss_row = _loss_head("loss_head", cur, final_norm.reshape(1, -1), tgt)
    loss = lax.psum(loss_row[0, 0], ("x", "y", "c"))

    G = {}
    big = {}
    G['final_norm'] = d_final_norm.reshape(-1)
    per_layer = {n: [None] * depth for n in ['ffn1_norm', 'mix_norm', 'ffn2_norm']}
    small_late = ['ffn1_norm', 'mix_norm']
    small_early = ['ffn2_norm', 's5_lambda_re', 's5_lambda_im', 's5_b_re', 's5_b_im', 's5_c_re', 's5_c_im',
                   's5_log_dt', 's5_d', 'gla_w_gk', 'gla_b_gk', 'gla_norm', 'ret_norm', 'final_norm']
    a2a, tok = {}, [jnp.zeros((), F32)]

    def start_a2a(*tagged):
        for (tag, _), started in zip(tagged, _exchange_start("a2as_" + tagged[0][0], [b for _, b in tagged], False)):
            a2a[tag] = started
        tok[0] = tok[0] + started[4][0, 0]

    def dep(vec):
        return vec + tok[0]

    def proj_backward(tag, pieces, s, dres, a2a_tag):
        tm = _tile(l, 512, SUBLANE)
        row = pl.BlockSpec((tm, d), lambda ii, jj, kk: (ii, 0))
        vec = pl.BlockSpec((1, d), lambda ii, jj, kk: (0, 0))
        dws = [_mm_plain(f"{tag}_dwin_{k}", s['h'], piece, TN, BF, tm=d, tn=2048, tk=2048) for k, (piece, _, _) in enumerate(pieces)]
        start_a2a((a2a_tag, _col_blocks(jnp.concatenate(dws, axis=1))))
        pairs = []
        for piece, w, col in pieces:
            wd = piece.shape[1]
            pairs.append((piece, pl.BlockSpec((tm, wd), lambda ii, jj, kk: (ii, 0)),
                          w, pl.BlockSpec((d, wd), lambda ii, jj, kk, col=col: (0, col), pipeline_mode=pl.Buffered(1))))
        return _mm(f"{tag}_dh", pairs, NT, (l // tm, 1, 1), [_sds((l, d), F32), _sds((1, d), F32)], [row, vec], (tm, d),
                   _rms_bwd_epi(0), [s['x1'], dep(mix_norm[i:i + 1]), dres], [row, vec, row])

    for i in reversed(range(depth)):
        j = i // 2
        s = saved[i]
        def ffn_grads(which):
            def on_grads(dw1, dw2, gnorm):
                start_a2a((f"{which}_w1_{i}", _col_blocks(dw1)), (f"{which}_w2_{i}", _row_blocks(dw2)))
                return dep(gnorm)
            return on_grads

        dx, dg = _ffn_bwd(f"l{i}_ffn2b", dx, s['x2'], ffn2_norm[i:i + 1], s['f2w1'], s['f2w2'], s['ffn2'], ffn_grads("ffn2"))
        per_layer['ffn2_norm'][i] = dg[0]
        tm = _tile(l, 512, SUBLANE)
        row = pl.BlockSpec((tm, d), lambda ii, jj, kk: (ii, 0))
        vec = pl.BlockSpec((1, d), lambda ii, jj, kk: (0, 0))
        if i % 2 == 0:
            proj, lin_kw = s['proj'], s['lin_kw']
            w_out = s['w_out']
            d_cat = _mm_plain(f"l{i}_dcat", dx, w_out, NT, BF, tm=512, tn=1024, tk=d)
            dwo_a = _mm_plain(f"l{i}_dwout_a", s['s5_out'], dx, TN, BF, tm=s5w, tn=d, tk=2048)
            dwo_b = _mm_plain(f"l{i}_dwout_b", s['gla_out'], dx, TN, BF, tm=gla_hv, tn=d, tk=2048)
            start_a2a((f"ab_w_out_{j}", _row_blocks(jnp.concatenate([dwo_a, dwo_b], axis=0))))
            do, dog, dgn = _headgate_bwd(f"l{i}_gla_outb", s['o_b'], proj, s['ogcol'], dep(gla_norm[j:j + 1]), d_cat, 1, gla_dv)
            G['gla_norm'] = dgn
            dq, dk_, dv_, dgf = _gla_bwd(f"l{i}_gla_bwd_f", proj, s['g_f'], s['sp_f'], do, None, reverse=False, **lin_kw)
            dq, dk_, dv_, dgb = _gla_bwd(f"l{i}_gla_bwd_b", proj, s['g_b'], s['sp_b'], do, (dq, dk_, dv_), reverse=True, **lin_kw)
            dglo, dwf, dwb, dbf, dbb = _gate_bwd(f"l{i}_gla_gateb", s['glo'], s['wgk_f'], s['wgk_b'], s['b_gk'][0:1], s['b_gk'][1:2], dgf, dgb)
            G['gla_w_gk'] = jnp.stack([dwf[:GLA_RANK], dwb[GLA_RANK:]], axis=0)[None]
            G['gla_b_gk'] = jnp.concatenate([dbf, dbb], axis=0)[None]
            dy, dwglu = _glu_bwd(f"l{i}_s5_glub", s['y'], s['w_glu'], d_cat, 0)
            start_a2a((f"s5_w_glu_{j}", _row_blocks(dwglu.astype(BF))))
            cd16, bd16, a_tab = s['cd16'], s['bd16'], s['a_tab']
            nt2 = a_tab.shape[2]
            a_conj = a_tab * jnp.where(jnp.arange(nt2) < nt2 // 2, 1.0, -1.0)[None, None, :, None]
            du_f, dbd_f, dcd_f, da_f = _s5_bwd(f"l{i}_s5_bwd_f", proj, dy, s['x_f'], bd16[0], cd16[0], a_conj[0], True)
            du_b, dbd_b, dcd_b, da_b = _s5_bwd(f"l{i}_s5_bwd_b", proj, dy, s['x_b'], bd16[1], cd16[1], a_conj[1], False)
            du, dd = _s5_du(f"l{i}_s5_du", du_f, du_b, dy, proj, s5_d[j:j + 1])
            G['s5_d'] = dd
            cot = (jnp.stack([da_f, da_b]), jnp.stack([dbd_f, dbd_b]), jnp.stack([dcd_f, dcd_b]))
            g_lre, g_lim, g_bre, g_bim, g_cre, g_cim, g_ldt = s['s5_vjp'](cot)
            G['s5_lambda_re'], G['s5_lambda_im'], G['s5_b_re'], G['s5_b_im'] = g_lre[None], g_lim[None], g_bre[None], g_bim[None]
            G['s5_c_re'], G['s5_c_im'], G['s5_log_dt'] = g_cre[None], g_cim[None], g_ldt[None]
            if i == 0:
                G['ffn2_norm'] = jnp.stack(per_layer['ffn2_norm'], axis=0)
                early_packed = _pack([G[n] for n in small_early])
                early_started = _exchange_start("ags_small_grads_early", [early_packed], True)[0]
                tok[0] = tok[0] + early_started[4][0, 0]
            w_main, w_glo = s['w_main'], s['w_glo']
            pieces = [(du, w_main, 0), (dq, w_main, s5w // gla_hk), (dk_, w_main, s5w // gla_hk + 1),
                      (dv_, w_main, (s5w + 2 * gla_hk) // gla_hv), (dog, w_main, (s5w + 2 * gla_hk) // gla_hv + 1), (dglo, w_glo, 0)]
            dx, dg = proj_backward(f"l{i}", pieces, s, dx, f"ab_w_in_{j}")
        else:
            proj, lin_kw = s['proj'], s['lin_kw']
            w_out = s['w_out']
            d_ro = _mm_plain(f"l{i}_dro", dx, w_out, NT, BF, tm=1024, tn=1024, tk=d, b_outer=True)
            dwo = _mm_plain(f"l{i}_dwout", s['r_out'], dx, TN, BF, tm=1024, tn=d, tk=2048)
            start_a2a((f"ret_w_out_{j}", _row_blocks(dwo)))
            do, dog, dgn = _headgate_bwd(f"l{i}_ret_outb", s['o_b'], proj, s['ogcol'], dep(ret_norm_full), d_ro, 0, ret_dv)
            G['ret_norm'] = dgn
            r1 = _lin_bwd(f"l{i}_ret_bwd_f", s['qr'], s['kr'], proj, lgtab_f, s['sp_f'], do, None, reverse=False, **lin_kw)
            r2 = _lin_bwd(f"l{i}_ret_bwd_b", s['qr'], s['kr'], proj, lgtab_b, s['sp_b'], do, r1, reverse=True, **lin_kw)
            dqr, dkr, dv_ = r2
            dq, dk_ = _rot_apply(f"l{i}_rotb", dqr, 0, dkr, 0, cos_t, sin_t, RET_HEADS, ret_dk, ret_dk ** -0.5, BF, True)
            w_in = s['w_in']
            pieces = [(dq, w_in, 0), (dk_, w_in, 1), (dv_, w_in, (2 * ret_hk) // ret_hv), (dog, w_in, (2 * ret_hk) // ret_hv + 1)]
            dx, dg = proj_backward(f"l{i}", pieces, s, dx, f"ret_w_in_{j}")
        per_layer['mix_norm'][i] = dg[0]
        dx, dg = _ffn_bwd(f"l{i}_ffn1b", dx, s['x0'], ffn1_norm[i:i + 1], s['f1w1'], s['f1w2'], s['ffn1'], ffn_grads("ffn1"))
        per_layer['ffn1_norm'][i] = dg[0]
    for n in small_late:
        G[n] = jnp.stack(per_layer[n], axis=0)
    grad_x = dx[None]

    out_g, out_d, out_m, out_v = {}, {}, {}, {}
    small = small_early + small_late
    packed = _pack([G[n] for n in small_late])
    small_started = _exchange_start("ags_small_grads_late", [packed], True)[0]

    def big_update(n, layers):
        parts = []
        for i in layers:
            blocks, got = _exchange_wait(f"a2aw_{n}_{i}", a2a.pop(f"{n}_{i}"), False, small_started[4])
            parts.append(lax.dynamic_update_index_in_dim(got, lax.dynamic_index_in_dim(blocks, me, 0, keepdims=False), me, 0))
        out_g[n], out_d[n], out_m[n], out_v[n] = _reduce_adam("upd_" + n, parts, W[n], M[n], V[n])

    for n in ['ffn2_w1', 'ffn2_w2', 'ffn1_w1', 'ffn1_w2']:
        big_update(n, range(depth))
    for n in ['ab_w_in', 's5_w_glu', 'ab_w_out', 'ret_w_in', 'ret_w_out']:
        big_update(n, [0])
    assert not a2a and not pending

    g_full = {}
    for tag, started, mine, group in (("early", early_started, early_packed, small_early), ("late", small_started, packed, small_late)):
        _, gathered = _exchange_wait("agw_small_grads_" + tag, started, True, out_v['ret_w_out'])
        gathered = lax.dynamic_update_index_in_dim(gathered, mine, me, 0)
        summed = _reduce8("sum_small_grads_" + tag, gathered)
        g_full.update(zip(group, _unpack(summed, [G[n] for n in group])))
    g_small = {}
    for n in small:
        gf = g_full[n]
        if n in ('gla_w_gk', 'gla_b_gk', 'ret_norm'):
            width = W[n].shape[-1]
            gf = lax.dynamic_slice_in_dim(gf, me * width, width, axis=gf.ndim - 1)
        g_small[n] = gf.reshape(W[n].shape)
    pw, pg, pm, pv = (_pack([src[n] for n in small]) for src in (W, g_small, M, V))
    pd, pm2, pv2 = _adam_packed("upd_small", pw, pg, pm, pv)
    like = [W[n] for n in small]
    for n, dd_, mm_, vv_ in zip(small, _unpack(pd, like), _unpack(pm2, like), _unpack(pv2, like)):
        out_g[n], out_d[n], out_m[n], out_v[n] = g_small[n], dd_, mm_, vv_

    return (loss, grad_x, *[out_g[n] for n in names], *[out_d[n] for n in names], *[out_m[n] for n in names], *[out_v[n] for n in names])
```

```python
import math

import jax
import jax.numpy as jnp
from jax import lax
from jax.experimental import pallas as pl
from jax.experimental.pallas import tpu as pltpu

F32 = jnp.float32
BF = jnp.bfloat16
N_DEV = 8
EPS = 1e-6
S5_GROUP = 16
GLA_HEADS = 4
GLA_RANK = 16
GLA_GATE_NORM = 16.0
RET_HEADS = 8
ROPE_BASE = 10000.0
GLA_CHUNK = 64
RET_CHUNK = 256
ADAM_LR, ADAM_B1, ADAM_B2, ADAM_EPS, ADAM_WD, ADAM_STEP = 0.001, 0.9, 0.999, 1e-08, 0.01, 10
VMEM_LIMIT_BYTES = 56 * 1024 * 1024
LANE = 128
SUBLANE = 8

NN = (((1,), (0,)), ((), ()))
NT = (((1,), (1,)), ((), ()))
TN = (((0,), (0,)), ((), ()))


def _tile(n, pref, align):
    if n <= pref:
        return n
    t = (pref // align) * align
    while t >= align:
        if n % t == 0:
            return t
        t -= align
    return n


def _params(sem):
    return pltpu.CompilerParams(dimension_semantics=sem, vmem_limit_bytes=VMEM_LIMIT_BYTES)


def _dot(a, b, dims=NN):
    return lax.dot_general(a.astype(BF), b.astype(BF), dims, preferred_element_type=F32)


def _dot3(m01, g, dims=NN):
    g1 = g.astype(BF)
    r1 = g - g1.astype(F32)
    g2 = r1.astype(BF)
    g3 = (r1 - g2.astype(F32)).astype(BF)
    m = m01.astype(BF)
    return (lax.dot_general(m, g1, dims, preferred_element_type=F32)
            + lax.dot_general(m, g2, dims, preferred_element_type=F32)
            + lax.dot_general(m, g3, dims, preferred_element_type=F32))


def _sigmoid(x):
    return 0.5 + 0.5 * jnp.tanh(0.5 * x)


def _mm(name, pairs, dims, grid, outs, out_specs, acc_shape, epi=None, eins=(), especs=()):
    n_p, n_e, n_o = len(pairs), len(eins), len(outs)
    nk = grid[2]

    def body(*refs):
        a_refs = refs[0:2 * n_p:2]
        b_refs = refs[1:2 * n_p:2]
        e_refs = refs[2 * n_p:2 * n_p + n_e]
        o_refs = refs[2 * n_p + n_e:2 * n_p + n_e + n_o]
        acc = refs[-1]
        ids = (pl.program_id(0), pl.program_id(1), pl.program_id(2))

        part = _dot(a_refs[0][...], b_refs[0][...], dims)
        for p in range(1, n_p):
            part = part + _dot(a_refs[p][...], b_refs[p][...], dims)

        def finish(total):
            if epi is None:
                o_refs[0][...] = total.astype(o_refs[0].dtype)
            else:
                epi(total, e_refs, o_refs, ids)

        if nk == 1:
            finish(part)
        else:
            @pl.when(ids[2] == 0)
            def _():
                acc[...] = part

            @pl.when(ids[2] > 0)
            def _():
                acc[...] += part

            @pl.when(ids[2] == nk - 1)
            def _():
                finish(acc[...])

    in_specs, args = [], []
    for a, a_spec, b, b_spec in pairs:
        in_specs += [a_spec, b_spec]
        args += [a, b]
    in_specs += list(especs)
    args += list(eins)
    res = pl.pallas_call(
        body, name=name, grid=grid, in_specs=in_specs, out_specs=list(out_specs), out_shape=list(outs),
        scratch_shapes=[pltpu.VMEM(acc_shape, F32)],
        compiler_params=_params(("arbitrary", "arbitrary", "arbitrary")),
    )(*args)
    return res


def _sds(shape, dtype):
    return jax.ShapeDtypeStruct(shape, dtype)


def _mm_plain(name, a, b, dims, out_dtype, tm=512, tn=1024, tk=1024, epi=None, eins=(), especs=None, extra_outs=(), extra_specs=(),
              b_outer=False):
    if dims == NN:
        (m, k), n = a.shape, b.shape[1]
    elif dims == NT:
        (m, k), n = a.shape, b.shape[0]
    else:
        (k, m), n = a.shape, b.shape[1]
    tm, tn = _tile(m, tm, LANE if dims == TN else SUBLANE), _tile(n, tn, LANE)
    tk = _tile(k, tk, SUBLANE if dims == TN else LANE)
    grid = (n // tn, m // tm, k // tk) if b_outer else (m // tm, n // tn, k // tk)

    def spec(block, index):
        if b_outer:
            return pl.BlockSpec(block, lambda j, i, kk: index(i, j, kk))
        return pl.BlockSpec(block, index)

    if dims == NN:
        a_spec = spec((tm, tk), lambda i, j, kk: (i, kk))
        b_spec = spec((tk, tn), lambda i, j, kk: (kk, j))
    elif dims == NT:
        a_spec = spec((tm, tk), lambda i, j, kk: (i, kk))
        b_spec = spec((tn, tk), lambda i, j, kk: (j, kk))
    else:
        a_spec = spec((tk, tm), lambda i, j, kk: (kk, i))
        b_spec = spec((tk, tn), lambda i, j, kk: (kk, j))
    o_spec = spec((tm, tn), lambda i, j, kk: (i, j))
    if especs is None:
        especs = [o_spec] * len(eins)
    else:
        especs = [o_spec if s is None else s for s in especs]
    res = _mm(name, [(a, a_spec, b, b_spec)], dims, grid, [_sds((m, n), out_dtype)] + list(extra_outs),
              [o_spec] + list(extra_specs), (tm, tn), epi, eins, especs)
    return res if extra_outs else res[0]


def _rms_fwd(name, x, g):
    l, d = x.shape
    tm = _tile(l, 1024, SUBLANE)

    def body(x_ref, g_ref, o_ref):
        xv = x_ref[...]
        r = lax.rsqrt(jnp.mean(xv * xv, axis=-1, keepdims=True) + EPS)
        o_ref[...] = (xv * r * g_ref[...]).astype(o_ref.dtype)

    return pl.pallas_call(
        body, name=name, grid=(l // tm,),
        in_specs=[pl.BlockSpec((tm, d), lambda i: (i, 0)), pl.BlockSpec((1, d), lambda i: (0, 0))],
        out_specs=pl.BlockSpec((tm, d), lambda i: (i, 0)), out_shape=_sds((l, d), BF),
        compiler_params=_params(("arbitrary",)),
    )(x, g)


def _rms_bwd_epi(first_axis):
    def epi(acc, e_refs, o_refs, ids):
        x_ref, g_ref, dr_ref = e_refs
        dx_ref, dg_ref = o_refs
        xv = x_ref[...]
        r = lax.rsqrt(jnp.mean(xv * xv, axis=-1, keepdims=True) + EPS)
        xh = xv * r
        dxh = acc * g_ref[...]
        dx_ref[...] = dr_ref[...] + r * (dxh - xh * jnp.mean(dxh * xh, axis=-1, keepdims=True))
        part = jnp.sum(acc * xh, axis=0, keepdims=True)

        @pl.when(ids[first_axis] == 0)
        def _():
            dg_ref[...] = part

        @pl.when(ids[first_axis] > 0)
        def _():
            dg_ref[...] += part

    return epi


def _loss_head(name, x, g, target):
    l, d = x.shape
    tm = _tile(l, 512, SUBLANE)
    n = l // tm

    def body(x_ref, g_ref, t_ref, dx_ref, dg_ref, loss_ref, lacc):
        i = pl.program_id(0)
        xv = x_ref[...]
        r = lax.rsqrt(jnp.mean(xv * xv, axis=-1, keepdims=True) + EPS)
        xh = xv * r
        e = xh * g_ref[...] - t_ref[...]
        dy = e * (1.0 / d)
        dxh = dy * g_ref[...]
        dx_ref[...] = r * (dxh - xh * jnp.mean(dxh * xh, axis=-1, keepdims=True))
        dg_part = jnp.sum(dy * xh, axis=0, keepdims=True)
        l_part = jnp.sum(e * e, axis=0, keepdims=True)

        @pl.when(i == 0)
        def _():
            dg_ref[...] = dg_part
            lacc[...] = l_part

        @pl.when(i > 0)
        def _():
            dg_ref[...] += dg_part
            lacc[...] += l_part

        @pl.when(i == n - 1)
        def _():
            loss_ref[...] = jnp.zeros_like(loss_ref) + jnp.sum(lacc[...]) * (0.5 / d)

    return pl.pallas_call(
        body, name=name, grid=(n,),
        in_specs=[pl.BlockSpec((tm, d), lambda i: (i, 0)), pl.BlockSpec((1, d), lambda i: (0, 0)),
                  pl.BlockSpec((tm, d), lambda i: (i, 0))],
        out_specs=[pl.BlockSpec((tm, d), lambda i: (i, 0)), pl.BlockSpec((1, d), lambda i: (0, 0)),
                   pl.BlockSpec((1, LANE), lambda i: (0, 0))],
        out_shape=[_sds((l, d), F32), _sds((1, d), F32), _sds((1, LANE), F32)],
        scratch_shapes=[pltpu.VMEM((1, d), F32)],
        compiler_params=_params(("arbitrary",)),
    )(x, g, target)


def _ffn_up(name, hn, w1):
    l, d = hn.shape
    f = w1.shape[1] // 2
    tm, tn = _tile(l, 512, SUBLANE), _tile(f, 1408, LANE)
    nj = f // tn

    def body(h_ref, wg_ref, wu_ref, gu_ref, a_ref):
        h = h_ref[...]
        g = jnp.dot(h, wg_ref[...], preferred_element_type=F32)
        u = jnp.dot(h, wu_ref[...], preferred_element_type=F32)
        s = _sigmoid(g)
        gs = g * s
        gu_ref[0] = (u * (s + gs * (1.0 - s))).astype(BF)
        gu_ref[1] = gs.astype(BF)
        a_ref[...] = (gs * u).astype(BF)

    return pl.pallas_call(
        body, name=name, grid=(nj, l // tm),
        in_specs=[pl.BlockSpec((tm, d), lambda j, i: (i, 0)), pl.BlockSpec((d, tn), lambda j, i: (0, j)),
                  pl.BlockSpec((d, tn), lambda j, i: (0, j + nj))],
        out_specs=[pl.BlockSpec((2, tm, tn), lambda j, i: (0, i, j)), pl.BlockSpec((tm, tn), lambda j, i: (i, j))],
        out_shape=[_sds((2, l, f), BF), _sds((l, f), BF)],
        compiler_params=_params(("arbitrary", "arbitrary")),
    )(hn, w1, w1)


def _residual_epi(scale, with_norm):
    def epi(acc, e_refs, o_refs, ids):
        xn = e_refs[0][...] + scale * acc
        o_refs[0][...] = xn
        if with_norm:
            r = lax.rsqrt(jnp.mean(xn * xn, axis=-1, keepdims=True) + EPS)
            o_refs[1][...] = (xn * r * e_refs[1][...]).astype(o_refs[1].dtype)

    return epi


def _ffn_fwd(tag, x, hn, get_w1, get_w2, next_gnorm):
    w1 = get_w1(hn)
    gu, a = _ffn_up(tag + "_up", hn, w1)
    w2 = get_w2(a)
    l, d = x.shape
    if next_gnorm is None:
        x_new = _mm_plain(tag + "_down", a, w2, NN, F32, tm=512, tn=d, tk=w2.shape[0], epi=_residual_epi(0.5, False), eins=[x])
        hn_next = None
    else:
        vec = pl.BlockSpec((1, d), lambda i, j, kk: (0, 0))
        tm = _tile(l, 512, SUBLANE)
        x_new, hn_next = _mm_plain(tag + "_down", a, w2, NN, F32, tm=512, tn=d, tk=w2.shape[0], epi=_residual_epi(0.5, True),
                                   eins=[x, next_gnorm], especs=[None, vec],
                                   extra_outs=[_sds((l, d), BF)], extra_specs=[pl.BlockSpec((tm, d), lambda i, j, kk: (i, 0))])
    return x_new, hn_next, (hn, gu, a), w1, w2


def _ffn_bwd(tag, dres, x, gnorm, w1, w2, saved, on_grads):
    hn, gu, a = saved
    l, d = x.shape
    f = w2.shape[0]
    tm, tn = _tile(l, 512, SUBLANE), _tile(f, 1408, LANE)
    nj = f // tn

    def epi_gu(acc, e_refs, o_refs, ids):
        da = 0.5 * acc
        o_refs[0][0] = (da * e_refs[0][0].astype(F32)).astype(BF)
        o_refs[0][1] = (da * e_refs[0][1].astype(F32)).astype(BF)

    tmg = _tile(l, 256, SUBLANE)
    gu_spec = pl.BlockSpec((2, tmg, f), lambda j, i, kk: (0, i, 0))
    dgu = _mm(tag + "_dgu",
              [(dres, pl.BlockSpec((tmg, d), lambda j, i, kk: (i, 0)), w2, pl.BlockSpec((f, d), lambda j, i, kk: (0, 0)))],
              NT, (1, l // tmg, 1), [_sds((2, l, f), BF)], [gu_spec], (tmg, f), epi_gu, [gu], [gu_spec])[0]

    def epi_half(acc, e_refs, o_refs, ids):
        o_refs[0][...] = (0.5 * acc).astype(BF)

    dw2 = _mm_plain(tag + "_dw2", a, dres, TN, BF, tm=1408, tn=d, tk=2048, epi=epi_half)

    tk = _tile(l, 2048, SUBLANE)
    dw1 = _mm(tag + "_dw1",
              [(hn, pl.BlockSpec((tk, d), lambda i, j, kk: (kk, 0)), dgu, pl.BlockSpec((None, tk, tn), lambda i, j, kk: (j // nj, kk, j % nj)))],
              TN, (1, 2 * nj, l // tk), [_sds((d, 2 * f), BF)], [pl.BlockSpec((d, tn), lambda i, j, kk: (0, j))], (d, tn))[0]

    gnorm = on_grads(dw1, dw2, gnorm)
    row = pl.BlockSpec((tm, d), lambda i, j, kk: (i, 0))
    vec = pl.BlockSpec((1, d), lambda i, j, kk: (0, 0))
    once = pl.Buffered(1)
    dx, dg = _mm(tag + "_dhn",
                 [(dgu, pl.BlockSpec((None, tm, f), lambda i, j, kk: (0, i, 0)), w1, pl.BlockSpec((d, f), lambda i, j, kk: (0, 0), pipeline_mode=once)),
                  (dgu, pl.BlockSpec((None, tm, f), lambda i, j, kk: (1, i, 0)), w1, pl.BlockSpec((d, f), lambda i, j, kk: (0, 1), pipeline_mode=once))],
                 NT, (l // tm, 1, 1), [_sds((l, d), F32), _sds((1, d), F32)], [row, vec], (tm, d),
                 _rms_bwd_epi(0), [x, gnorm, dres], [row, vec, row])
    return dx, dg


def _s5_chunk_tables(lam_re, lam_im, b_re, b_im, c_re, c_im, log_dt, hs):
    f32 = F32
    g, n = lam_re.shape[1], lam_re.shape[2]
    p = b_re.shape[-1]
    nch, gpc, nt = (g * n) // hs, hs // n, hs // LANE
    lr = jnp.minimum(lam_re.astype(f32), -1e-4)
    li = lam_im.astype(f32)
    dt = jnp.exp(log_dt.astype(f32))[..., None]
    mag = jnp.exp(lr * dt)
    ar = mag * jnp.cos(li * dt)
    ai = mag * jnp.sin(li * dt)
    den = lr * lr + li * li
    cr = ((ar - 1.0) * lr + ai * li) / den
    ci = (ai * lr - (ar - 1.0) * li) / den
    bbr = cr[..., None] * b_re - ci[..., None] * b_im
    bbi = cr[..., None] * b_im + ci[..., None] * b_re
    a_f = jnp.stack([ar, ai], axis=1).reshape(2, 2, nch, nt, LANE).transpose(0, 2, 1, 3, 4).reshape(2, nch, 2 * nt, LANE)
    rows_g = jnp.arange(gpc * p) // p
    cols_g = (jnp.arange(2 * hs) % hs) // n
    diag = (rows_g[:, None] == cols_g[None, :]).astype(f32)
    bb = jnp.stack([bbr, bbi], axis=1).reshape(2, 2, nch, hs, p)
    bd = jnp.tile(bb.transpose(0, 2, 4, 1, 3).reshape(2, nch, p, 2 * hs), (1, 1, gpc, 1)) * diag
    cc = jnp.stack([c_re, -c_im], axis=1).reshape(2, 2, nch, gpc, p, n)
    cd = jnp.tile(cc.transpose(0, 2, 4, 1, 3, 5).reshape(2, nch, p, 2 * hs), (1, 1, gpc, 1)) * diag
    return a_f, bd, cd


def _fold_store(ref, val, tb, ntiles):
    for s in range(ntiles):
        ref[:, s * SUBLANE:(s + 1) * SUBLANE, :] = val[:, s * LANE:(s + 1) * LANE].reshape(tb // SUBLANE, SUBLANE, LANE)


def _unfold(ref, tb, ntiles):
    return jnp.concatenate([ref[:, s * SUBLANE:(s + 1) * SUBLANE, :].reshape(tb, LANE) for s in range(ntiles)], axis=1)


def _s5_fwd(name, proj, bd, cd, a_f, reverse):
    l = proj.shape[0]
    nch, cu, hs2 = bd.shape
    nt = hs2 // (2 * LANE)
    frows = 2 * nt * SUBLANE
    tb = _tile(l, 512, SUBLANE)
    nb = l // tb

    def body(u_ref, bd_ref, cd_ref, a_ref, xf_ref, y_ref, st):
        r = pl.program_id(1)

        @pl.when(r == 0)
        def _():
            st[...] = jnp.zeros_like(st)

        _fold_store(xf_ref, _dot(u_ref[...], bd_ref[...]), tb, 2 * nt)
        ar, ai = a_ref[0:nt, :], a_ref[nt:2 * nt, :]

        def group(gi, carry):
            rr = (tb // SUBLANE - 1 - gi) if reverse else gi
            sr, si = carry
            for qq in range(SUBLANE):
                q = (SUBLANE - 1 - qq) if reverse else qq
                re_rows, im_rows = pl.ds(q, nt, stride=SUBLANE), pl.ds(nt * SUBLANE + q, nt, stride=SUBLANE)
                nr = ar * sr - ai * si + xf_ref[rr, re_rows, :]
                ni = ar * si + ai * sr + xf_ref[rr, im_rows, :]
                xf_ref[rr, re_rows, :] = nr
                xf_ref[rr, im_rows, :] = ni
                sr, si = nr, ni
            return sr, si

        fin = lax.fori_loop(0, tb // SUBLANE, group, (st[0:nt, :], st[nt:2 * nt, :]))
        st[0:nt, :] = fin[0]
        st[nt:2 * nt, :] = fin[1]
        y_ref[...] = _dot(_unfold(xf_ref, tb, 2 * nt), cd_ref[...], NT)

    def rows(r):
        return (nb - 1 - r) if reverse else r

    return pl.pallas_call(
        body, name=name, grid=(nch, nb),
        in_specs=[pl.BlockSpec((tb, cu), lambda c, r: (rows(r), c)), pl.BlockSpec((None, cu, hs2), lambda c, r: (c, 0, 0)),
                  pl.BlockSpec((None, cu, hs2), lambda c, r: (c, 0, 0)), pl.BlockSpec((None, 2 * nt, LANE), lambda c, r: (c, 0, 0))],
        out_specs=[pl.BlockSpec((tb // SUBLANE, frows, LANE), lambda c, r: (rows(r), c, 0)), pl.BlockSpec((tb, cu), lambda c, r: (rows(r), c))],
        out_shape=[_sds((l // SUBLANE, nch * frows, LANE), F32), _sds((l, nch * cu), F32)],
        scratch_shapes=[pltpu.VMEM((2 * nt, LANE), F32)],
        compiler_params=_params(("arbitrary", "arbitrary")),
    )(proj, bd, cd, a_f)


def _s5_bwd(name, proj, dy, xf, bd, cd, a_conj, reverse):
    l = proj.shape[0]
    nch, cu, hs2 = bd.shape
    nt = hs2 // (2 * LANE)
    frows = 2 * nt * SUBLANE
    tb = _tile(l, 512, SUBLANE)
    nb = l // tb

    def body(u_ref, dy_ref, xs_ref, bd_ref, cd_ref, a_ref, du_ref, dbd_ref, dcd_ref, da_ref, lam, st):
        r = pl.program_id(1)

        @pl.when(r == 0)
        def _():
            st[...] = jnp.zeros_like(st)
            dbd_ref[...] = jnp.zeros_like(dbd_ref)
            dcd_ref[...] = jnp.zeros_like(dcd_ref)
            da_ref[...] = jnp.zeros_like(da_ref)

        dyv = dy_ref[...]
        _fold_store(lam, _dot(dyv, cd_ref[...]), tb, 2 * nt)
        ar, ai = a_ref[0:nt, :], a_ref[nt:2 * nt, :]

        def group(gi, carry):
            rr = (tb // SUBLANE - 1 - gi) if reverse else gi
            sr, si, cr, ci = carry
            for qq in range(SUBLANE):
                q = (SUBLANE - 1 - qq) if reverse else qq
                re_rows, im_rows = pl.ds(q, nt, stride=SUBLANE), pl.ds(nt * SUBLANE + q, nt, stride=SUBLANE)
                xr, xi = xs_ref[rr, re_rows, :], xs_ref[rr, im_rows, :]
                cr = cr + sr * xr + si * xi
                ci = ci + si * xr - sr * xi
                nr = ar * sr - ai * si + lam[rr, re_rows, :]
                ni = ar * si + ai * sr + lam[rr, im_rows, :]
                lam[rr, re_rows, :] = nr
                lam[rr, im_rows, :] = ni
                sr, si = nr, ni
            return sr, si, cr, ci

        zero = jnp.zeros((nt, LANE), F32)
        fin = lax.fori_loop(0, tb // SUBLANE, group, (st[0:nt, :], st[nt:2 * nt, :], zero, zero))
        st[0:nt, :] = fin[0]
        st[nt:2 * nt, :] = fin[1]
        da_ref[0:nt, :] += fin[2]
        da_ref[nt:2 * nt, :] += fin[3]
        lam_u = _unfold(lam, tb, 2 * nt)
        du_ref[...] = _dot(lam_u, bd_ref[...], NT)
        dbd_ref[...] += _dot(u_ref[...], lam_u, TN)
        dcd_ref[...] += _dot(dyv, _unfold(xs_ref, tb, 2 * nt), TN)

    def rows(r):
        return (nb - 1 - r) if reverse else r

    chunk_rows = pl.BlockSpec((tb, cu), lambda c, r: (rows(r), c))
    bd_spec = pl.BlockSpec((None, cu, hs2), lambda c, r: (c, 0, 0))
    cd_spec = bd_spec
    a_spec = pl.BlockSpec((None, 2 * nt, LANE), lambda c, r: (c, 0, 0))
    return pl.pallas_call(
        body, name=name, grid=(nch, nb),
        in_specs=[chunk_rows, chunk_rows, pl.BlockSpec((tb // SUBLANE, frows, LANE), lambda c, r: (rows(r), c, 0)), bd_spec, cd_spec, a_spec],
        out_specs=[chunk_rows, bd_spec, cd_spec, a_spec],
        out_shape=[_sds((l, nch * cu), F32), _sds((nch, cu, hs2), F32), _sds((nch, cu, hs2), F32), _sds((nch, 2 * nt, LANE), F32)],
        scratch_shapes=[pltpu.VMEM((tb // SUBLANE, frows, LANE), F32), pltpu.VMEM((2 * nt, LANE), F32)],
        compiler_params=_params(("arbitrary", "arbitrary")),
    )(proj, dy, xf, bd, cd, a_conj)


def _s5_du(name, du_f, du_b, dy, proj, d_row):
    l, w = dy.shape
    tm = _tile(l, 1024, SUBLANE)

    def body(f_ref, b_ref, dy_ref, u_ref, d_ref, du_ref, dd_ref):
        i = pl.program_id(0)
        dyv = dy_ref[...]
        du_ref[...] = (f_ref[...] + b_ref[...] + dyv * d_ref[...]).astype(du_ref.dtype)
        part = jnp.sum(dyv * u_ref[...].astype(F32), axis=0, keepdims=True)

        @pl.when(i == 0)
        def _():
            dd_ref[...] = part

        @pl.when(i > 0)
        def _():
            dd_ref[...] += part

    row = pl.BlockSpec((tm, w), lambda i: (i, 0))
    vec = pl.BlockSpec((1, w), lambda i: (0, 0))
    return pl.pallas_call(
        body, name=name, grid=(l // tm,), in_specs=[row, row, row, row, vec], out_specs=[row, vec],
        out_shape=[_sds((l, w), BF), _sds((1, w), F32)],
        compiler_params=_params(("arbitrary",)),
    )(du_f, du_b, dy, proj, d_row)


def _gelu(y):
    c = math.sqrt(2.0 / math.pi)
    return 0.5 * y * (1.0 + jnp.tanh(c * (y + 0.044715 * y * y * y)))


def _gelu_grad(y):
    c = math.sqrt(2.0 / math.pi)
    th = jnp.tanh(c * (y + 0.044715 * y * y * y))
    return 0.5 * (1.0 + th) + 0.5 * y * (1.0 - th * th) * c * (1.0 + 3.0 * 0.044715 * y * y)


def _glu_fwd(name, y_f, y_b, proj, d_row, w):
    l, wd = y_f.shape
    tm = _tile(l, 1024, SUBLANE)

    def body(yf_ref, yb_ref, u_ref, d_ref, w_ref, y_ref, o_ref):
        y = yf_ref[...] + yb_ref[...] + u_ref[...].astype(F32) * d_ref[...]
        y_ref[...] = y
        gy = _gelu(y)
        z = _dot(gy, w_ref[...])
        o_ref[...] = (gy * _sigmoid(z)).astype(o_ref.dtype)

    row = pl.BlockSpec((tm, wd), lambda i: (i, 0))
    return pl.pallas_call(
        body, name=name, grid=(l // tm,),
        in_specs=[row, row, row, pl.BlockSpec((1, wd), lambda i: (0, 0)), pl.BlockSpec((wd, wd), lambda i: (0, 0))],
        out_specs=[row, row], out_shape=[_sds((l, wd), F32), _sds((l, wd), BF)],
        compiler_params=_params(("arbitrary",)),
    )(y_f, y_b, proj, d_row, w)


def _glu_bwd(name, y, w, dout, dcol):
    l, wd = y.shape
    tm = _tile(l, 1024, SUBLANE)

    def body(y_ref, w_ref, d_ref, dy_ref, dw_ref):
        i = pl.program_id(0)
        yv = y_ref[...]
        gy = _gelu(yv)
        s = _sigmoid(_dot(gy, w_ref[...]))
        d = d_ref[...].astype(F32)
        t = d * gy * s * (1.0 - s)
        dgy = d * s + _dot(t, w_ref[...], NT)
        dy_ref[...] = dgy * _gelu_grad(yv)
        part = _dot(gy, t, TN)

        @pl.when(i == 0)
        def _():
            dw_ref[...] = part

        @pl.when(i > 0)
        def _():
            dw_ref[...] += part

    return pl.pallas_call(
        body, name=name, grid=(l // tm,),
        in_specs=[pl.BlockSpec((tm, wd), lambda i: (i, 0)), pl.BlockSpec((wd, wd), lambda i: (0, 0)),
                  pl.BlockSpec((tm, wd), lambda i: (i, dcol))],
        out_specs=[pl.BlockSpec((tm, wd), lambda i: (i, 0)), pl.BlockSpec((wd, wd), lambda i: (0, 0))],
        out_shape=[_sds((l, wd), F32), _sds((wd, wd), F32)],
        compiler_params=_params(("arbitrary",)),
    )(y, w, dout)


def _log_sigmoid(x):
    return jnp.minimum(x, 0.0) - jnp.log(1.0 + jnp.exp(-jnp.abs(x)))


def _gate_fwd(name, glo, wf, wb, bf, bb):
    l, r2 = glo.shape
    hk = wf.shape[1]
    tm = _tile(l, 1024, SUBLANE)

    def body(x_ref, wf_ref, wb_ref, bf_ref, bb_ref, gf_ref, gb_ref):
        xv = x_ref[...]
        gf_ref[...] = _log_sigmoid(_dot(xv, wf_ref[...]) + bf_ref[...]) * (1.0 / GLA_GATE_NORM)
        gb_ref[...] = _log_sigmoid(_dot(xv, wb_ref[...]) + bb_ref[...]) * (1.0 / GLA_GATE_NORM)

    w_spec = pl.BlockSpec((r2, hk), lambda i: (0, 0))
    b_spec = pl.BlockSpec((1, hk), lambda i: (0, 0))
    o_spec = pl.BlockSpec((tm, hk), lambda i: (i, 0))
    return pl.pallas_call(
        body, name=name, grid=(l // tm,),
        in_specs=[pl.BlockSpec((tm, r2), lambda i: (i, 0)), w_spec, w_spec, b_spec, b_spec],
        out_specs=[o_spec, o_spec], out_shape=[_sds((l, hk), F32), _sds((l, hk), F32)],
        compiler_params=_params(("arbitrary",)),
    )(glo, wf, wb, bf, bb)


def _gate_bwd(name, glo, wf, wb, bf, bb, dgf, dgb):
    l, r2 = glo.shape
    hk = wf.shape[1]
    tm = _tile(l, 1024, SUBLANE)

    def body(x_ref, wf_ref, wb_ref, bf_ref, bb_ref, dgf_ref, dgb_ref, dx_ref, dwf_ref, dwb_ref, dbf_ref, dbb_ref):
        i = pl.program_id(0)
        xv = x_ref[...]
        kf = _dot(xv, wf_ref[...]) + bf_ref[...]
        kb = _dot(xv, wb_ref[...]) + bb_ref[...]
        dkf = dgf_ref[...] * (1.0 / GLA_GATE_NORM) * _sigmoid(-kf)
        dkb = dgb_ref[...] * (1.0 / GLA_GATE_NORM) * _sigmoid(-kb)
        dx_ref[...] = _dot(dkf, wf_ref[...], NT) + _dot(dkb, wb_ref[...], NT)
        parts = (_dot(xv, dkf, TN), _dot(xv, dkb, TN), jnp.sum(dkf, axis=0, keepdims=True), jnp.sum(dkb, axis=0, keepdims=True))
        accs = (dwf_ref, dwb_ref, dbf_ref, dbb_ref)

        @pl.when(i == 0)
        def _():
            for a_, p_ in zip(accs, parts):
                a_[...] = p_

        @pl.when(i > 0)
        def _():
            for a_, p_ in zip(accs, parts):
                a_[...] += p_

    w_spec = pl.BlockSpec((r2, hk), lambda i: (0, 0))
    b_spec = pl.BlockSpec((1, hk), lambda i: (0, 0))
    g_spec = pl.BlockSpec((tm, hk), lambda i: (i, 0))
    x_spec = pl.BlockSpec((tm, r2), lambda i: (i, 0))
    return pl.pallas_call(
        body, name=name, grid=(l // tm,),
        in_specs=[x_spec, w_spec, w_spec, b_spec, b_spec, g_spec, g_spec],
        out_specs=[x_spec, w_spec, w_spec, b_spec, b_spec],
        out_shape=[_sds((l, r2), F32), _sds((r2, hk), F32), _sds((r2, hk), F32), _sds((1, hk), F32), _sds((1, hk), F32)],
        compiler_params=_params(("arbitrary",)),
    )(glo, wf, wb, bf, bb, dgf, dgb)


def _chunk_terms(qc, kc, lg, chunk, reverse):
    ri = lax.broadcasted_iota(jnp.int32, (chunk, chunk), 0)
    ci = lax.broadcasted_iota(jnp.int32, (chunk, chunk), 1)
    mask = (ci > ri) if reverse else (ci <= ri)
    pos = lax.broadcasted_iota(jnp.int32, (chunk, 1), 0).astype(F32)
    cum = ((chunk - pos) if reverse else (pos + 1.0)) * lg
    last = chunk * lg
    e = jnp.exp(cum)
    einv = jnp.exp(-cum)
    dec = jnp.exp(last - cum)
    return e, einv, dec, qc * e, kc * einv, kc * dec, jnp.exp(last), mask


def _lin_specs(width, col, tb, nb, reverse):
    return pl.BlockSpec((tb, width), lambda h, r: ((nb - 1 - r) if reverse else r, col + h))


def _lin_fwd(name, q, k, v, lgtab, prev_o=None, *, heads, hb, dk, dv, chunk, tb, qcol, kcol, vcol, reverse):
    l = q.shape[0]
    nb, ncb, ng = l // tb, tb // chunk, heads // hb

    def body(q_ref, k_ref, v_ref, lg_ref, *rest):
        p_ref = rest[0] if prev_o is not None else None
        o_ref, sp_ref, st = rest[-3:]

        @pl.when(pl.program_id(1) == 0)
        def _():
            st[...] = jnp.zeros_like(st)

        for c in range(ncb):
            cc = (ncb - 1 - c) if reverse else c
            rows = pl.ds(cc * chunk, chunk)
            for h in range(hb):
                ks, vs = slice(h * dk, (h + 1) * dk), slice(h * dv, (h + 1) * dv)
                qc, kc, vc = q_ref[rows, ks].astype(F32), k_ref[rows, ks].astype(F32), v_ref[rows, vs]
                _, _, _, qd, ki, kdec, e_last, mask = _chunk_terms(qc, kc, lg_ref[h, :, 0:1], chunk, reverse)
                a = jnp.where(mask, _dot(qd, ki, NT), 0.0)
                s_t = st[h]
                oc = _dot(a, vc) + _dot(qd, s_t, NT)
                if p_ref is not None:
                    oc = oc + p_ref[rows, vs]
                o_ref[rows, vs] = oc.astype(o_ref.dtype)
                sp_ref[cc, h] = s_t
                st[h] = s_t * e_last + _dot(vc, kdec, TN)

    o_spec = _lin_specs(hb * dv, 0, tb, nb, reverse)
    extra = [] if prev_o is None else [prev_o]
    return pl.pallas_call(
        body, name=name, grid=(ng, nb),
        in_specs=[_lin_specs(hb * dk, qcol, tb, nb, reverse), _lin_specs(hb * dk, kcol, tb, nb, reverse),
                  _lin_specs(hb * dv, vcol, tb, nb, reverse), pl.BlockSpec((hb, 1, LANE), lambda h, r: (h, 0, 0))] + [o_spec] * len(extra),
        out_specs=[o_spec, pl.BlockSpec((ncb, hb, dv, dk), lambda h, r: ((nb - 1 - r) if reverse else r, h, 0, 0))],
        out_shape=[_sds((l, heads * dv), F32 if prev_o is None else BF), _sds((l // chunk, heads, dv, dk), F32)],
        scratch_shapes=[pltpu.VMEM((hb, dv, dk), F32)],
        compiler_params=_params(("arbitrary", "arbitrary")),
    )(q, k, v, lgtab, *extra)


def _lin_bwd(name, q, k, v, lgtab, sprev, do, prev, *, heads, hb, dk, dv, chunk, tb, qcol, kcol, vcol, reverse):
    l = q.shape[0]
    nb, ncb, ng = l // tb, tb // chunk, heads // hb
    brev = not reverse
    n_prev = 0 if prev is None else len(prev)

    def body(q_ref, k_ref, v_ref, lg_ref, sp_ref, do_ref, *rest):
        p_refs = rest[:n_prev]
        dq_ref, dk_ref, dv_ref, dst = rest[n_prev:]

        @pl.when(pl.program_id(1) == 0)
        def _():
            dst[...] = jnp.zeros_like(dst)

        for c in range(ncb):
            cc = (ncb - 1 - c) if brev else c
            rows = pl.ds(cc * chunk, chunk)
            for h in range(hb):
                ks, vs = slice(h * dk, (h + 1) * dk), slice(h * dv, (h + 1) * dv)
                qc, kc, vc = q_ref[rows, ks].astype(F32), k_ref[rows, ks].astype(F32), v_ref[rows, vs]
                e, einv, dec, qd, ki, kdec, e_last, mask = _chunk_terms(qc, kc, lg_ref[h, :, 0:1], chunk, reverse)
                a = jnp.where(mask, _dot(qd, ki, NT), 0.0)
                s_t, ds_t, doc = sp_ref[cc, h], dst[h], do_ref[rows, vs]
                dvc = _dot(a, doc, TN) + _dot(kdec, ds_t, NT)
                da = jnp.where(mask, _dot(doc, vc, NT), 0.0)
                dqc = (_dot(da, ki) + _dot(doc, s_t)) * e
                dkc = _dot(da, qd, TN) * einv + _dot(vc, ds_t) * dec
                dst[h] = ds_t * e_last + _dot(doc, qd, TN)
                if n_prev:
                    dqc = dqc + p_refs[0][rows, ks]
                    dkc = dkc + p_refs[1][rows, ks]
                    dvc = dvc + p_refs[2][rows, vs]
                dq_ref[rows, ks] = dqc
                dk_ref[rows, ks] = dkc
                dv_ref[rows, vs] = dvc.astype(dv_ref.dtype)

    k_spec, v_spec = _lin_specs(hb * dk, 0, tb, nb, brev), _lin_specs(hb * dv, 0, tb, nb, brev)
    in_specs = [_lin_specs(hb * dk, qcol, tb, nb, brev), _lin_specs(hb * dk, kcol, tb, nb, brev), _lin_specs(hb * dv, vcol, tb, nb, brev),
                pl.BlockSpec((hb, 1, LANE), lambda h, r: (h, 0, 0)),
                pl.BlockSpec((ncb, hb, dv, dk), lambda h, r: ((nb - 1 - r) if brev else r, h, 0, 0)), v_spec]
    args = [q, k, v, lgtab, sprev, do]
    if n_prev:
        in_specs += [k_spec, k_spec, v_spec]
        args += list(prev)
    return pl.pallas_call(
        body, name=name, grid=(ng, nb), in_specs=in_specs, out_specs=[k_spec, k_spec, v_spec],
        out_shape=[_sds((l, heads * dk), F32), _sds((l, heads * dk), F32), _sds((l, heads * dv), BF if n_prev else F32)],
        scratch_shapes=[pltpu.VMEM((hb, dv, dk), F32)],
        compiler_params=_params(("arbitrary", "arbitrary")),
    )(*args)


def _log2(n):
    assert n & (n - 1) == 0, "a power of two"
    return n.bit_length() - 1


def _gla_block_terms(q, k, g, qscale, chunk, tb, reverse):
    ri = lax.broadcasted_iota(jnp.int32, (tb, tb), 0)
    ci = lax.broadcasted_iota(jnp.int32, (tb, tb), 1)
    same = jnp.right_shift(ri, _log2(chunk)) == jnp.right_shift(ci, _log2(chunk))
    t_in = jnp.logical_and(same, (ci >= ri) if reverse else (ci <= ri)).astype(F32)
    cum = _dot3(t_in, g)
    tot = _dot3(same.astype(F32), g)
    e = jnp.exp(cum)
    einv = jnp.exp(-cum)
    dec = jnp.exp(tot - cum)
    return e, einv, dec, jnp.exp(tot), q * (qscale * e), k * einv, k * dec, t_in


def _gla_masks(hk, dk, heads, chunk, reverse):
    lane = lax.broadcasted_iota(jnp.int32, (1, hk), 1)
    head_of = jnp.right_shift(lane, _log2(dk))
    ri = lax.broadcasted_iota(jnp.int32, (chunk, chunk), 0)
    ci = lax.broadcasted_iota(jnp.int32, (chunk, chunk), 1)
    return [head_of == h for h in range(heads)], ((ci > ri) if reverse else (ci <= ri))


def _gla_fwd(name, proj, g, prev_o=None, *, heads, dk, dv, chunk, tb, qcol, kcol, vcol, qscale, reverse):
    l = proj.shape[0]
    nb, ncb, hk, hv = l // tb, tb // chunk, heads * dk, heads * dv

    def body(q_ref, k_ref, v_ref, g_ref, *rest):
        p_ref = rest[0] if prev_o is not None else None
        o_ref, sp_ref, st = rest[-3:]

        @pl.when(pl.program_id(0) == 0)
        def _():
            st[...] = jnp.zeros_like(st)

        _, _, _, etot, qd, ki, kdec, _ = _gla_block_terms(q_ref[...].astype(F32), k_ref[...].astype(F32), g_ref[...], qscale, chunk, tb, reverse)
        heads_m, causal = _gla_masks(hk, dk, heads, chunk, reverse)
        s_all = st[...]
        for c in range(ncb):
            cc = (ncb - 1 - c) if reverse else c
            rc = slice(cc * chunk, (cc + 1) * chunk)
            qd_c, ki_c, kdec_c = qd[rc], ki[rc], kdec[rc]
            sp_ref[cc] = s_all
            kv = jnp.zeros_like(s_all)
            for h in range(heads):
                vs = slice(h * dv, (h + 1) * dv)
                qm = jnp.where(heads_m[h], qd_c, 0.0)
                a = jnp.where(causal, _dot(qm, ki_c, NT), 0.0)
                vc = v_ref[rc, vs]
                oc = _dot(a, vc) + _dot(qm, s_all, NT)
                if p_ref is not None:
                    oc = oc + p_ref[rc, vs]
                o_ref[rc, vs] = oc.astype(o_ref.dtype)
                kv = kv + jnp.where(heads_m[h], _dot(vc, kdec_c, TN), 0.0)
            s_all = s_all * etot[rc][0:1, :] + kv
        st[...] = s_all

    def rows(r):
        return (nb - 1 - r) if reverse else r

    o_spec = pl.BlockSpec((tb, hv), lambda r: (rows(r), 0))
    extra = [] if prev_o is None else [prev_o]
    return pl.pallas_call(
        body, name=name, grid=(nb,),
        in_specs=[pl.BlockSpec((tb, hk), lambda r: (rows(r), qcol)), pl.BlockSpec((tb, hk), lambda r: (rows(r), kcol)),
                  pl.BlockSpec((tb, hv), lambda r: (rows(r), vcol)), pl.BlockSpec((tb, hk), lambda r: (rows(r), 0))] + [o_spec] * len(extra),
        out_specs=[o_spec, pl.BlockSpec((ncb, dv, hk), lambda r: (rows(r), 0, 0))],
        out_shape=[_sds((l, hv), F32 if prev_o is None else BF), _sds((l // chunk, dv, hk), F32)],
        scratch_shapes=[pltpu.VMEM((dv, hk), F32)],
        compiler_params=_params(("arbitrary",)),
    )(proj, proj, proj, g, *extra)


def _gla_bwd(name, proj, g, sprev, do, prev, *, heads, dk, dv, chunk, tb, qcol, kcol, vcol, qscale, reverse):
    l = proj.shape[0]
    nb, ncb, hk, hv = l // tb, tb // chunk, heads * dk, heads * dv
    brev = not reverse
    n_prev = 0 if prev is None else len(prev)

    def body(q_ref, k_ref, v_ref, g_ref, sp_ref, do_ref, *rest):
        p_refs = rest[:n_prev]
        dq_ref, dk_ref, dv_ref, dg_ref, dst, dcs = rest[n_prev:]

        @pl.when(pl.program_id(0) == 0)
        def _():
            dst[...] = jnp.zeros_like(dst)

        e, einv, dec, etot, qd, ki, kdec, t_in = _gla_block_terms(q_ref[...].astype(F32), k_ref[...].astype(F32), g_ref[...], qscale, chunk, tb, reverse)
        heads_m, causal = _gla_masks(hk, dk, heads, chunk, reverse)
        last_row = lax.broadcasted_iota(jnp.int32, (chunk, 1), 0) == (0 if reverse else chunk - 1)
        ds_all = dst[...]
        for c in range(ncb):
            cc = (ncb - 1 - c) if brev else c
            rc = slice(cc * chunk, (cc + 1) * chunk)
            qd_c, ki_c, kdec_c = qd[rc], ki[rc], kdec[rc]
            s_all = sp_ref[cc]
            et = etot[rc][0:1, :]
            dqd = jnp.zeros((chunk, hk), F32)
            dki = jnp.zeros((chunk, hk), F32)
            dkdec = jnp.zeros((chunk, hk), F32)
            ds_add = jnp.zeros_like(ds_all)
            for h in range(heads):
                vs = slice(h * dv, (h + 1) * dv)
                m = heads_m[h]
                qm = jnp.where(m, qd_c, 0.0)
                a = jnp.where(causal, _dot(qm, ki_c, NT), 0.0)
                doc, vc = do_ref[rc, vs], v_ref[rc, vs]
                dvc = _dot(a, doc, TN) + _dot(jnp.where(m, kdec_c, 0.0), ds_all, NT)
                if n_prev:
                    dvc = dvc + p_refs[2][rc, vs]
                dv_ref[rc, vs] = dvc.astype(dv_ref.dtype)
                da = jnp.where(causal, _dot(doc, vc, NT), 0.0)
                dqd = dqd + jnp.where(m, _dot(da, ki_c) + _dot(doc, s_all), 0.0)
                dki = dki + _dot(da, qm, TN)
                dkdec = dkdec + jnp.where(m, _dot(vc, ds_all), 0.0)
                ds_add = ds_add + _dot(doc, qm, TN)
            dqc = dqd * e[rc] * qscale
            dkc = dki * einv[rc] + dkdec * dec[rc]
            if n_prev:
                dqc = dqc + p_refs[0][rc, :]
                dkc = dkc + p_refs[1][rc, :]
            dq_ref[rc, :] = dqc
            dk_ref[rc, :] = dkc
            dlast = jnp.sum(dkdec * kdec_c, axis=0, keepdims=True) + et * jnp.sum(s_all * ds_all, axis=0, keepdims=True)
            dcs[rc, :] = dqd * qd_c - dki * ki_c - dkdec * kdec_c + jnp.where(last_row, dlast, 0.0)
            ds_all = ds_all * et + ds_add
        dst[...] = ds_all
        dg_ref[...] = _dot3(t_in, dcs[...], TN)

    def rows(r):
        return (nb - 1 - r) if brev else r

    k_spec = pl.BlockSpec((tb, hk), lambda r: (rows(r), 0))
    v_spec = pl.BlockSpec((tb, hv), lambda r: (rows(r), 0))
    in_specs = [pl.BlockSpec((tb, hk), lambda r: (rows(r), qcol)), pl.BlockSpec((tb, hk), lambda r: (rows(r), kcol)),
                pl.BlockSpec((tb, hv), lambda r: (rows(r), vcol)), k_spec,
                pl.BlockSpec((ncb, dv, hk), lambda r: (rows(r), 0, 0)), v_spec]
    args = [proj, proj, proj, g, sprev, do]
    if n_prev:
        in_specs += [k_spec, k_spec, v_spec]
        args += list(prev)
    return pl.pallas_call(
        body, name=name, grid=(nb,), in_specs=in_specs, out_specs=[k_spec, k_spec, v_spec, k_spec],
        out_shape=[_sds((l, hk), F32), _sds((l, hk), F32), _sds((l, hv), BF if n_prev else F32), _sds((l, hk), F32)],
        scratch_shapes=[pltpu.VMEM((dv, hk), F32), pltpu.VMEM((tb, hk), F32)],
        compiler_params=_params(("arbitrary",)),
    )(*args)


def _headgate_fwd(name, o_sum, og_arr, og_col, gn, dv):
    l, w = o_sum.shape
    tm = _tile(l, 1024, SUBLANE)
    nh = w // dv

    def body(o_ref, og_ref, gn_ref, out_ref):
        for h in range(nh):
            cs = slice(h * dv, (h + 1) * dv)
            o = o_ref[:, cs].astype(F32)
            r = lax.rsqrt(jnp.mean(o * o, axis=-1, keepdims=True) + EPS)
            og = og_ref[:, cs].astype(F32)
            out_ref[:, cs] = (o * r * gn_ref[:, cs] * (og * _sigmoid(og))).astype(out_ref.dtype)

    row = pl.BlockSpec((tm, w), lambda i: (i, 0))
    return pl.pallas_call(
        body, name=name, grid=(l // tm,),
        in_specs=[row, pl.BlockSpec((tm, w), lambda i: (i, og_col)), pl.BlockSpec((1, w), lambda i: (0, 0))],
        out_specs=row, out_shape=_sds((l, w), BF),
        compiler_params=_params(("arbitrary",)),
    )(o_sum, og_arr, gn)


def _headgate_bwd(name, o_sum, og_arr, og_col, gn, dout, dcol, dv):
    l, w = o_sum.shape
    tm = _tile(l, 1024, SUBLANE)
    nh = w // dv

    def body(o_ref, og_ref, gn_ref, d_ref, do_ref, dog_ref, dgn_ref):
        i = pl.program_id(0)
        for h in range(nh):
            cs = slice(h * dv, (h + 1) * dv)
            o = o_ref[:, cs].astype(F32)
            r = lax.rsqrt(jnp.mean(o * o, axis=-1, keepdims=True) + EPS)
            oh = o * r
            og = og_ref[:, cs].astype(F32)
            s = _sigmoid(og)
            d = d_ref[:, cs].astype(F32)
            gnv = gn_ref[:, cs]
            d_on = d * (og * s)
            dog_ref[:, cs] = (d * (oh * gnv) * s * (1.0 + og * (1.0 - s))).astype(dog_ref.dtype)
            doh = d_on * gnv
            do_ref[:, cs] = (r * (doh - oh * jnp.mean(doh * oh, axis=-1, keepdims=True))).astype(do_ref.dtype)
            part = jnp.sum(d_on * oh, axis=0, keepdims=True)

            @pl.when(i == 0)
            def _():
                dgn_ref[:, cs] = part

            @pl.when(i > 0)
            def _():
                dgn_ref[:, cs] += part

    row = pl.BlockSpec((tm, w), lambda i: (i, 0))
    vec = pl.BlockSpec((1, w), lambda i: (0, 0))
    return pl.pallas_call(
        body, name=name, grid=(l // tm,),
        in_specs=[row, pl.BlockSpec((tm, w), lambda i: (i, og_col)), vec, pl.BlockSpec((tm, w), lambda i: (i, dcol))],
        out_specs=[row, row, vec], out_shape=[_sds((l, w), BF), _sds((l, w), BF), _sds((1, w), F32)],
        compiler_params=_params(("arbitrary",)),
    )(o_sum, og_arr, gn, dout)


def _rot_tables(l, dk):
    half = dk // 2
    pos = jnp.arange(l, dtype=F32)
    inv = jnp.exp(-math.log(ROPE_BASE) * jnp.arange(half, dtype=F32) / half)
    ang = pos[:, None] * inv[None, :]
    cos, sin = jnp.cos(ang), jnp.sin(ang)
    return jnp.concatenate([cos, cos], axis=-1), jnp.concatenate([-sin, sin], axis=-1)


def _rot_apply(name, src_q, qcol, src_k, kcol, cos_t, sin_t, heads, dk, kscale, out_dtype, transpose):
    l = src_q.shape[0]
    w = heads * dk
    tm = _tile(l, 1024, SUBLANE)

    def rot(t, cos_v, sin_v):
        if transpose:
            return t * cos_v + pltpu.roll(t * sin_v, dk // 2, 1)
        return t * cos_v + pltpu.roll(t, dk // 2, 1) * sin_v

    def body(q_ref, k_ref, c_ref, s_ref, qo_ref, ko_ref):
        cos_v, sin_v = c_ref[...], s_ref[...]
        for h in range(heads):
            cs = slice(h * dk, (h + 1) * dk)
            qo_ref[:, cs] = rot(q_ref[:, cs].astype(F32), cos_v, sin_v).astype(out_dtype)
            ko_ref[:, cs] = (rot(k_ref[:, cs].astype(F32), cos_v, sin_v) * kscale).astype(out_dtype)

    tab = pl.BlockSpec((tm, dk), lambda i: (i, 0))
    row = pl.BlockSpec((tm, w), lambda i: (i, 0))
    return pl.pallas_call(
        body, name=name, grid=(l // tm,),
        in_specs=[pl.BlockSpec((tm, w), lambda i: (i, qcol)), pl.BlockSpec((tm, w), lambda i: (i, kcol)), tab, tab],
        out_specs=[row, row], out_shape=[_sds((l, w), out_dtype), _sds((l, w), out_dtype)],
        compiler_params=_params(("arbitrary",)),
    )(src_q, src_k, cos_t, sin_t)


def _peer_copies(src_ref, out_ref, send_sems, recv_sems, gather):
    x, y, c = lax.axis_index("x"), lax.axis_index("y"), lax.axis_index("c")
    me = 4 * x + 2 * y + c
    copies = []
    for kk in range(1, N_DEV):
        px = (1 - x) if kk & 4 else x
        py = (1 - y) if kk & 2 else y
        pc = (1 - c) if kk & 1 else c
        peer = 4 * px + 2 * py + pc
        copies.append(pltpu.make_async_remote_copy(
            src_ref=src_ref if gather else src_ref.at[peer], dst_ref=out_ref.at[me],
            send_sem=send_sems.at[kk - 1], recv_sem=recv_sems.at[kk - 1],
            device_id=(px, py, pc), device_id_type=pl.DeviceIdType.MESH))
    return copies


_HBM = pl.BlockSpec(memory_space=pltpu.HBM)
_SEM = pl.BlockSpec(memory_space=pltpu.SEMAPHORE)
_EFFECT = pltpu.SideEffectType.DATAFLOW_SIDE_EFFECTING


def _exchange_start(name, srcs, gather):
    n = len(srcs)
    lands = [lax.empty((N_DEV,) + tuple(s.shape if gather else s.shape[1:]), s.dtype) for s in srcs]

    def body(*refs):
        src_refs, land_refs = refs[:n], refs[n:2 * n]
        send, recv = refs[2 * n:3 * n], refs[3 * n:4 * n]
        token = refs[-1]
        for k in range(n):
            for cp in _peer_copies(src_refs[k], land_refs[k], send[k], recv[k], gather):
                cp.start()
        token[...] = jnp.zeros_like(token)

    sem = pltpu.SemaphoreType.DMA((N_DEV - 1,))
    outs = pl.pallas_call(
        body, name=name,
        out_shape=tuple([sem] * (2 * n) + [pltpu.HBM(s.shape, s.dtype) for s in srcs] + [pltpu.HBM(a.shape, a.dtype) for a in lands]
                        + [_sds((SUBLANE, LANE), F32)]),
        in_specs=tuple([_HBM] * (2 * n)), out_specs=tuple([_SEM] * (2 * n) + [_HBM] * (2 * n) + [pl.BlockSpec(memory_space=pltpu.VMEM)]),
        input_output_aliases={k: 2 * n + k for k in range(2 * n)},
        compiler_params=pltpu.CompilerParams(has_side_effects=_EFFECT),
    )(*[pltpu.with_memory_space_constraint(a, pltpu.HBM) for a in list(srcs) + lands])
    return [(outs[k], outs[n + k], outs[2 * n + k], outs[3 * n + k], outs[-1]) for k in range(n)]


def _exchange_wait(name, started, gather, after):
    send_sems, recv_sems, src_thru, land_thru, _ = started

    def body(src_ref, land_ref, send_sems, recv_sems, after_ref, src_out, land_out):
        copies = _peer_copies(src_ref, land_ref, send_sems, recv_sems, gather)
        for cp in copies:
            cp.wait_send()
        for cp in copies:
            cp.wait_recv()

    return pl.pallas_call(
        body, name=name,
        out_shape=(pltpu.HBM(src_thru.shape, src_thru.dtype), pltpu.HBM(land_thru.shape, land_thru.dtype)),
        in_specs=(_HBM, _HBM, _SEM, _SEM, pl.BlockSpec(memory_space=pl.ANY)), out_specs=(_HBM, _HBM),
        input_output_aliases={0: 0, 1: 1},
        compiler_params=pltpu.CompilerParams(has_side_effects=_EFFECT),
    )(src_thru, land_thru, send_sems, recv_sems, after)


def _adam_math(w, gsum, m, v):
    m2 = ADAM_B1 * m + (1.0 - ADAM_B1) * gsum
    v2 = ADAM_B2 * v + (1.0 - ADAM_B2) * (gsum * gsum)
    m_hat = m2 / (1.0 - ADAM_B1 ** ADAM_STEP)
    v_hat = v2 / (1.0 - ADAM_B2 ** ADAM_STEP)
    delta = -ADAM_LR * (m_hat / (jnp.sqrt(v_hat) + ADAM_EPS) + ADAM_WD * w)
    return delta, m2, v2


def _reduce_adam(name, parts, w, m, v):
    nl, r, c = w.shape
    tr = _tile(r, 256, 16)
    nr = r // tr

    def body(*refs):
        p_refs = refs[:nl]
        w_ref, m_ref, v_ref, g_ref, d_ref, m2_ref, v2_ref = refs[nl:]
        for li in range(nl):
            @pl.when(pl.program_id(0) == li)
            def _(p_ref=p_refs[li]):
                gsum = p_ref[0].astype(F32)
                for s in range(1, N_DEV):
                    gsum = gsum + p_ref[s].astype(F32)
                g_ref[...] = gsum
                delta, m2, v2 = _adam_math(w_ref[...], gsum, m_ref[...], v_ref[...])
                d_ref[...] = delta
                m2_ref[...] = m2
                v2_ref[...] = v2

    def part_spec(li):
        return pl.BlockSpec((N_DEV, tr, c), lambda lay, i: (0, jnp.where(lay == li, i, jnp.where(lay < li, 0, nr - 1)), 0))

    row = pl.BlockSpec((None, tr, c), lambda lay, i: (lay, i, 0))
    return pl.pallas_call(
        body, name=name, grid=(nl, nr),
        in_specs=[part_spec(li) for li in range(nl)] + [row, row, row],
        out_specs=[row, row, row, row], out_shape=[_sds((nl, r, c), F32)] * 4,
        compiler_params=_params(("arbitrary", "arbitrary")),
    )(*parts, w, m, v)


def _reduce8(name, parts):
    _, r, c = parts.shape

    def body(p_ref, g_ref):
        gsum = p_ref[0]
        for s in range(1, N_DEV):
            gsum = gsum + p_ref[s]
        g_ref[...] = gsum

    return pl.pallas_call(
        body, name=name, grid=(1,),
        in_specs=[pl.BlockSpec((N_DEV, r, c), lambda i: (0, 0, 0))],
        out_specs=pl.BlockSpec((r, c), lambda i: (0, 0)), out_shape=_sds((r, c), F32),
        compiler_params=_params(("arbitrary",)),
    )(parts)


def _adam_packed(name, w, g, m, v):
    r, c = w.shape

    def body(w_ref, g_ref, m_ref, v_ref, d_ref, m2_ref, v2_ref):
        delta, m2, v2 = _adam_math(w_ref[...], g_ref[...], m_ref[...], v_ref[...])
        d_ref[...] = delta
        m2_ref[...] = m2
        v2_ref[...] = v2

    spec = pl.BlockSpec((r, c), lambda i: (0, 0))
    return pl.pallas_call(
        body, name=name, grid=(1,), in_specs=[spec] * 4, out_specs=[spec] * 3, out_shape=[_sds((r, c), F32)] * 3,
        compiler_params=_params(("arbitrary",)),
    )(w, g, m, v)


def _pack(arrs):
    flat = jnp.concatenate([a.reshape(-1).astype(F32) for a in arrs])
    n = flat.shape[0]
    pad = (-n) % (SUBLANE * LANE)
    return jnp.pad(flat, (0, pad)).reshape(-1, LANE)


def _unpack(packed, like):
    flat = packed.reshape(-1)
    out, off = [], 0
    for a in like:
        n = math.prod(a.shape)
        out.append(flat[off:off + n].reshape(a.shape))
        off += n
    return out


def _row_blocks(full):
    return full.reshape(N_DEV, full.shape[0] // N_DEV, full.shape[1])


def _col_blocks(full):
    r, c = full.shape
    return full.reshape(r, N_DEV, c // N_DEV).transpose(1, 0, 2)


def kernel(x, ffn1_norm, ffn1_w1, ffn1_w2, mix_norm, ffn2_norm, ffn2_w1, ffn2_w2, ab_w_in, s5_lambda_re, s5_lambda_im, s5_b_re, s5_b_im, s5_c_re, s5_c_im, s5_log_dt, s5_d, s5_w_glu, gla_w_gk, gla_b_gk, gla_norm, ab_w_out, ret_w_in, ret_norm, ret_w_out, final_norm, loss_target, m_ffn1_norm, m_ffn1_w1, m_ffn1_w2, m_mix_norm, m_ffn2_norm, m_ffn2_w1, m_ffn2_w2, m_ab_w_in, m_s5_lambda_re, m_s5_lambda_im, m_s5_b_re, m_s5_b_im, m_s5_c_re, m_s5_c_im, m_s5_log_dt, m_s5_d, m_s5_w_glu, m_gla_w_gk, m_gla_b_gk, m_gla_norm, m_ab_w_out, m_ret_w_in, m_ret_norm, m_ret_w_out, m_final_norm, v_ffn1_norm, v_ffn1_w1, v_ffn1_w2, v_mix_norm, v_ffn2_norm, v_ffn2_w1, v_ffn2_w2, v_ab_w_in, v_s5_lambda_re, v_s5_lambda_im, v_s5_b_re, v_s5_b_im, v_s5_c_re, v_s5_c_im, v_s5_log_dt, v_s5_d, v_s5_w_glu, v_gla_w_gk, v_gla_b_gk, v_gla_norm, v_ab_w_out, v_ret_w_in, v_ret_norm, v_ret_w_out, v_final_norm):
    names = ['ffn1_norm', 'ffn1_w1', 'ffn1_w2', 'mix_norm', 'ffn2_norm', 'ffn2_w1', 'ffn2_w2', 'ab_w_in', 's5_lambda_re', 's5_lambda_im', 's5_b_re', 's5_b_im', 's5_c_re', 's5_c_im', 's5_log_dt', 's5_d', 's5_w_glu', 'gla_w_gk', 'gla_b_gk', 'gla_norm', 'ab_w_out', 'ret_w_in', 'ret_norm', 'ret_w_out', 'final_norm']
    loc = locals()
    W = {n: loc[n] for n in names}
    M = {n: loc["m_" + n] for n in names}
    V = {n: loc["v_" + n] for n in names}

    me = 4 * lax.axis_index("x") + 2 * lax.axis_index("y") + lax.axis_index("c")
    xs = x[0]
    tgt = loss_target[0]
    l, d = xs.shape
    depth = ffn1_norm.shape[0]

    pending, to_start = {}, []

    def start_gather(tag, shard):
        to_start.append((tag, shard))

    def finish_gather(tag, after):
        started, shard = pending.pop(tag)
        _, got = _exchange_wait("agw_" + tag, started, True, after)
        return lax.dynamic_update_index_in_dim(got, shard, me, 0)

    def finish_cols(tag, after):
        g = finish_gather(tag, after)
        return g.transpose(1, 0, 2).reshape(g.shape[1], -1)

    def finish_rows(tag, after):
        g = finish_gather(tag, after)
        return g.reshape(-1, g.shape[2])

    small_sharded = [gla_w_gk, gla_b_gk, ret_norm]
    for i in range(depth):
        j = i // 2
        start_gather(f"ffn1_w1_{i}", ffn1_w1[i].astype(BF))
        start_gather(f"ffn1_w2_{i}", ffn1_w2[i].astype(BF))
        if i % 2 == 0:
            start_gather(f"ab_w_in_{j}", ab_w_in[j].astype(BF))
            if i == 0:
                start_gather("small", _pack(small_sharded))
            start_gather(f"s5_w_glu_{j}", s5_w_glu[j].astype(BF))
            start_gather(f"ab_w_out_{j}", ab_w_out[j].astype(BF))
        else:
            start_gather(f"ret_w_in_{j}", ret_w_in[j].astype(BF))
            start_gather(f"ret_w_out_{j}", ret_w_out[j].astype(BF))
        start_gather(f"ffn2_w1_{i}", ffn2_w1[i].astype(BF))
        start_gather(f"ffn2_w2_{i}", ffn2_w2[i].astype(BF))
    for (tag, shard), started in zip(to_start, _exchange_start("ags_weights", [s_ for _, s_ in to_start], True)):
        pending[tag] = (started, shard)
    started_all = started[4][0, 0]
    full = {}

    s5w = s5_d.shape[1]
    g_s5, n_s5 = s5_lambda_re.shape[2], s5_lambda_re.shape[3]
    hs = min(SUBLANE * LANE, g_s5 * n_s5)
    gla_hk = gla_w_gk.shape[-1] * N_DEV
    gla_dk = gla_hk // GLA_HEADS
    gla_hv = gla_norm.shape[1]
    gla_dv = gla_hv // GLA_HEADS
    ret_hv = ret_norm.shape[1] * N_DEV
    ret_dv = ret_hv // RET_HEADS
    ret_hk = (ret_w_in.shape[2] * N_DEV - 2 * ret_hv) // 2
    ret_dk = ret_hk // RET_HEADS
    assert s5w == gla_hv and 2 * gla_hk == s5w, "column blocks of the mixer projection assume these widths"
    assert ret_hv == 2 * ret_hk
    main_w = s5w + 2 * gla_hk + 2 * gla_hv
    gla_tb = _tile(l, 256, GLA_CHUNK)
    ret_chunk = min(RET_CHUNK, l)

    cos_t, sin_t = _rot_tables(l, ret_dk)
    lg_f = jnp.log1p(-jnp.exp2(-5.0 - jnp.arange(RET_HEADS, dtype=F32)))
    lgtab_f = jnp.broadcast_to(lg_f[:, None, None], (RET_HEADS, 1, LANE))
    lgtab_b = jnp.broadcast_to(lg_f[::-1][:, None, None], (RET_HEADS, 1, LANE))
    s5_pre = {}
    for j in range((depth + 1) // 2):
        s5_args = (s5_lambda_re[j], s5_lambda_im[j], s5_b_re[j], s5_b_im[j], s5_c_re[j], s5_c_im[j], s5_log_dt[j])
        (a_tab, bd, cd), s5_vjp = jax.vjp(lambda *a: _s5_chunk_tables(*a, hs), *s5_args)
        s5_pre[j] = (a_tab, bd.astype(BF), cd.astype(BF), s5_vjp)
    tables_done = jnp.stack([cos_t[0, 0], sin_t[0, 0], lgtab_f[0, 0, 0], lgtab_b[0, 0, 0]]
                            + [t[0].reshape(-1)[0] + t[1].reshape(-1)[0].astype(F32) + t[2].reshape(-1)[0].astype(F32) for t in s5_pre.values()])

    saved = []
    cur = xs
    hn = _rms_fwd("l0_ffn1_norm", cur, ffn1_norm[0:1] + started_all)
    for i in range(depth):
        j = i // 2
        s = {}
        s['x0'] = cur

        def first_w1(after):
            if i == 0:
                after = jnp.concatenate([after[0, 0:1].astype(F32), tables_done])
            return finish_cols(f"ffn1_w1_{i}", after)

        cur, h, s['ffn1'], s['f1w1'], s['f1w2'] = _ffn_fwd(f"l{i}_ffn1", cur, hn, first_w1, lambda after: finish_rows(f"ffn1_w2_{i}", after),
                                                            mix_norm[i:i + 1])
        s['x1'] = cur
        s['h'] = h
        if i % 2 == 0:
            w_in = finish_cols(f"ab_w_in_{j}", cur)
            if i == 0:
                got = finish_gather("small", cur)
                flat, off, joined = got.reshape(N_DEV, -1), 0, []
                for a in small_sharded:
                    n = math.prod(a.shape)
                    blk = jnp.moveaxis(flat[:, off:off + n].reshape((N_DEV,) + a.shape), 0, -2)
                    joined.append(blk.reshape(a.shape[:-1] + (N_DEV * a.shape[-1],)))
                    off += n
                full['gla_w_gk'], full['gla_b_gk'], ret_norm_full = joined[0].astype(BF), joined[1], joined[2]
            s['w_glu'], s['w_out'] = finish_rows(f"s5_w_glu_{j}", cur), finish_rows(f"ab_w_out_{j}", cur)
            w_main, w_glo = w_in[:, :main_w], w_in[:, main_w:]
            proj = _mm_plain(f"l{i}_proj", h, w_main, NN, BF, tm=1024, tn=1024, tk=d, b_outer=True)
            glo = _mm_plain(f"l{i}_glo", h, w_glo, NN, F32, tm=1024, tn=2 * GLA_RANK, tk=d)
            a_tab, bd16, cd16, s5_vjp = s5_pre[j]
            tm = _tile(l, 512, SUBLANE)
            x_f, y_f = _s5_fwd(f"l{i}_s5_fwd_f", proj, bd16[0], cd16[0], a_tab[0], False)
            x_b, y_b = _s5_fwd(f"l{i}_s5_fwd_b", proj, bd16[1], cd16[1], a_tab[1], True)
            d_row = s5_d[j:j + 1]
            y, s5_out = _glu_fwd(f"l{i}_s5_glu", y_f, y_b, proj, d_row, s['w_glu'])
            zeros_r = jnp.zeros((GLA_RANK, gla_hk), BF)
            w_gk = full['gla_w_gk'][j]
            wgk_f = jnp.concatenate([w_gk[0], zeros_r], axis=0)
            wgk_b = jnp.concatenate([zeros_r, w_gk[1]], axis=0)
            b_gk = full['gla_b_gk'][j]
            g_f, g_b = _gate_fwd(f"l{i}_gla_gate", glo, wgk_f, wgk_b, b_gk[0:1], b_gk[1:2])
            qcol, kcol, vcol, ogcol = s5w // gla_hk, s5w // gla_hk + 1, (s5w + 2 * gla_hk) // gla_hv, (s5w + 2 * gla_hk) // gla_hv + 1
            lin_kw = dict(heads=GLA_HEADS, dk=gla_dk, dv=gla_dv, chunk=GLA_CHUNK, tb=gla_tb, qcol=qcol, kcol=kcol, vcol=vcol,
                          qscale=gla_dk ** -0.5)
            o_f, sp_f = _gla_fwd(f"l{i}_gla_fwd_f", proj, g_f, reverse=False, **lin_kw)
            o_b, sp_b = _gla_fwd(f"l{i}_gla_fwd_b", proj, g_b, o_f, reverse=True, **lin_kw)
            gla_out = _headgate_fwd(f"l{i}_gla_out", o_b, proj, ogcol, gla_norm[j:j + 1], gla_dv)
            w_out = s['w_out']

            row = pl.BlockSpec((tm, d), lambda ii, jj, kk: (ii, 0))
            vec = pl.BlockSpec((1, d), lambda ii, jj, kk: (0, 0))
            cur, hn = _mm(f"l{i}_mix_out",
                          [(s5_out, pl.BlockSpec((tm, s5w), lambda ii, jj, kk: (ii, 0)), w_out, pl.BlockSpec((s5w, d), lambda ii, jj, kk: (0, 0))),
                           (gla_out, pl.BlockSpec((tm, gla_hv), lambda ii, jj, kk: (ii, 0)), w_out, pl.BlockSpec((gla_hv, d), lambda ii, jj, kk: (1, 0)))],
                          NN, (l // tm, 1, 1), [_sds((l, d), F32), _sds((l, d), BF)], [row, row], (tm, d), _residual_epi(1.0, True),
                          [cur, ffn2_norm[i:i + 1]], [row, vec])
            s.update(proj=proj, glo=glo, s5_vjp=s5_vjp, a_tab=a_tab, bd16=bd16, cd16=cd16, x_f=x_f, x_b=x_b, y=y, s5_out=s5_out,
                     wgk_f=wgk_f, wgk_b=wgk_b, b_gk=b_gk, g_f=g_f, g_b=g_b, o_f=o_f, o_b=o_b, sp_f=sp_f, sp_b=sp_b, gla_out=gla_out,
                     w_main=w_main, w_glo=w_glo, lin_kw=lin_kw, ogcol=ogcol)
        else:
            w_in = finish_cols(f"ret_w_in_{j}", cur)
            s['w_in'], s['w_out'] = w_in, finish_rows(f"ret_w_out_{j}", cur)
            proj = _mm_plain(f"l{i}_proj", h, w_in, NN, BF, tm=1024, tn=1024, tk=d, b_outer=True)
            qr, kr = _rot_apply(f"l{i}_rot", proj, 0, proj, 1, cos_t, sin_t, RET_HEADS, ret_dk, ret_dk ** -0.5, BF, False)
            ret_hb = 4
            lin_kw = dict(heads=RET_HEADS, hb=ret_hb, dk=ret_dk, dv=ret_dv, chunk=ret_chunk, tb=_tile(l, 4 * ret_chunk, ret_chunk), qcol=0, kcol=0,
                          vcol=(2 * ret_hk) // (ret_hb * ret_dv))
            o_f, sp_f = _lin_fwd(f"l{i}_ret_fwd_f", qr, kr, proj, lgtab_f, reverse=False, **lin_kw)
            o_b, sp_b = _lin_fwd(f"l{i}_ret_fwd_b", qr, kr, proj, lgtab_b, o_f, reverse=True, **lin_kw)
            ogcol = (2 * ret_hk + ret_hv) // ret_hv
            r_out = _headgate_fwd(f"l{i}_ret_out", o_b, proj, ogcol, ret_norm_full, ret_dv)

            tm = _tile(l, 512, SUBLANE)
            cur, hn = _mm_plain(f"l{i}_mix_out", r_out, s['w_out'], NN, F32, tm=512, tn=d, tk=ret_hv, epi=_residual_epi(1.0, True),
                                eins=[cur, ffn2_norm[i:i + 1]], especs=[None, pl.BlockSpec((1, d), lambda ii, jj, kk: (0, 0))],
                                extra_outs=[_sds((l, d), BF)], extra_specs=[pl.BlockSpec((tm, d), lambda ii, jj, kk: (ii, 0))])
            s.update(proj=proj, qr=qr, kr=kr, o_f=o_f, o_b=o_b, sp_f=sp_f, sp_b=sp_b, r_out=r_out, lin_kw=lin_kw, ogcol=ogcol)
        s['x2'] = cur
        cur, hn, s['ffn2'], s['f2w1'], s['f2w2'] = _ffn_fwd(f"l{i}_ffn2", cur, hn, lambda after: finish_cols(f"ffn2_w1_{i}", after),
                                                             lambda after: finish_rows(f"ffn2_w2_{i}", after),
                                                             ffn1_norm[i + 1:i + 2] if i + 1 < depth else None)
        saved.append(s)

    dx, d_final_norm, loss_row = _loss_head("loss_head", cur, final_norm.reshape(1, -1), tgt)
    loss = lax.psum(loss_row[0, 0], ("x", "y", "c"))

    G = {}
    big = {}
    G['final_norm'] = d_final_norm.reshape(-1)
    per_layer = {n: [None] * depth for n in ['ffn1_norm', 'mix_norm', 'ffn2_norm']}
    small_late = ['ffn1_norm', 'mix_norm']
    small_early = ['ffn2_norm', 's5_lambda_re', 's5_lambda_im', 's5_b_re', 's5_b_im', 's5_c_re', 's5_c_im',
                   's5_log_dt', 's5_d', 'gla_w_gk', 'gla_b_gk', 'gla_norm', 'ret_norm', 'final_norm']
    a2a, tok = {}, [jnp.zeros((), F32)]

    def start_a2a(*tagged):
        for (tag, _), started in zip(tagged, _exchange_start("a2as_" + tagged[0][0], [b for _, b in tagged], False)):
            a2a[tag] = started
        tok[0] = tok[0] + started[4][0, 0]

    def dep(vec):
        return vec + tok[0]

    def proj_backward(tag, pieces, s, dres, a2a_tag):
        tm = _tile(l, 512, SUBLANE)
        row = pl.BlockSpec((tm, d), lambda ii, jj, kk: (ii, 0))
        vec = pl.BlockSpec((1, d), lambda ii, jj, kk: (0, 0))
        dws = [_mm_plain(f"{tag}_dwin_{k}", s['h'], piece, TN, BF, tm=d, tn=2048, tk=2048) for k, (piece, _, _) in enumerate(pieces)]
        start_a2a((a2a_tag, _col_blocks(jnp.concatenate(dws, axis=1))))
        pairs = []
        for piece, w, col in pieces:
            wd = piece.shape[1]
            pairs.append((piece, pl.BlockSpec((tm, wd), lambda ii, jj, kk: (ii, 0)),
                          w, pl.BlockSpec((d, wd), lambda ii, jj, kk, col=col: (0, col), pipeline_mode=pl.Buffered(1))))
        return _mm(f"{tag}_dh", pairs, NT, (l // tm, 1, 1), [_sds((l, d), F32), _sds((1, d), F32)], [row, vec], (tm, d),
                   _rms_bwd_epi(0), [s['x1'], dep(mix_norm[i:i + 1]), dres], [row, vec, row])

    for i in reversed(range(depth)):
        j = i // 2
        s = saved[i]
        def ffn_grads(which):
            def on_grads(dw1, dw2, gnorm):
                start_a2a((f"{which}_w1_{i}", _col_blocks(dw1)), (f"{which}_w2_{i}", _row_blocks(dw2)))
                return dep(gnorm)
            return on_grads

        dx, dg = _ffn_bwd(f"l{i}_ffn2b", dx, s['x2'], ffn2_norm[i:i + 1], s['f2w1'], s['f2w2'], s['ffn2'], ffn_grads("ffn2"))
        per_layer['ffn2_norm'][i] = dg[0]
        tm = _tile(l, 512, SUBLANE)
        row = pl.BlockSpec((tm, d), lambda ii, jj, kk: (ii, 0))
        vec = pl.BlockSpec((1, d), lambda ii, jj, kk: (0, 0))
        if i % 2 == 0:
            proj, lin_kw = s['proj'], s['lin_kw']
            w_out = s['w_out']
            d_cat = _mm_plain(f"l{i}_dcat", dx, w_out, NT, BF, tm=512, tn=1024, tk=d)
            dwo_a = _mm_plain(f"l{i}_dwout_a", s['s5_out'], dx, TN, BF, tm=s5w, tn=d, tk=2048)
            dwo_b = _mm_plain(f"l{i}_dwout_b", s['gla_out'], dx, TN, BF, tm=gla_hv, tn=d, tk=2048)
            start_a2a((f"ab_w_out_{j}", _row_blocks(jnp.concatenate([dwo_a, dwo_b], axis=0))))
            do, dog, dgn = _headgate_bwd(f"l{i}_gla_outb", s['o_b'], proj, s['ogcol'], dep(gla_norm[j:j + 1]), d_cat, 1, gla_dv)
            G['gla_norm'] = dgn
            dq, dk_, dv_, dgf = _gla_bwd(f"l{i}_gla_bwd_f", proj, s['g_f'], s['sp_f'], do, None, reverse=False, **lin_kw)
            dq, dk_, dv_, dgb = _gla_bwd(f"l{i}_gla_bwd_b", proj, s['g_b'], s['sp_b'], do, (dq, dk_, dv_), reverse=True, **lin_kw)
            dglo, dwf, dwb, dbf, dbb = _gate_bwd(f"l{i}_gla_gateb", s['glo'], s['wgk_f'], s['wgk_b'], s['b_gk'][0:1], s['b_gk'][1:2], dgf, dgb)
            G['gla_w_gk'] = jnp.stack([dwf[:GLA_RANK], dwb[GLA_RANK:]], axis=0)[None]
            G['gla_b_gk'] = jnp.concatenate([dbf, dbb], axis=0)[None]
            dy, dwglu = _glu_bwd(f"l{i}_s5_glub", s['y'], s['w_glu'], d_cat, 0)
            start_a2a((f"s5_w_glu_{j}", _row_blocks(dwglu.astype(BF))))
            cd16, bd16, a_tab = s['cd16'], s['bd16'], s['a_tab']
            nt2 = a_tab.shape[2]
            a_conj = a_tab * jnp.where(jnp.arange(nt2) < nt2 // 2, 1.0, -1.0)[None, None, :, None]
            du_f, dbd_f, dcd_f, da_f = _s5_bwd(f"l{i}_s5_bwd_f", proj, dy, s['x_f'], bd16[0], cd16[0], a_conj[0], True)
            du_b, dbd_b, dcd_b, da_b = _s5_bwd(f"l{i}_s5_bwd_b", proj, dy, s['x_b'], bd16[1], cd16[1], a_conj[1], False)
            du, dd = _s5_du(f"l{i}_s5_du", du_f, du_b, dy, proj, s5_d[j:j + 1])
            G['s5_d'] = dd
            cot = (jnp.stack([da_f, da_b]), jnp.stack([dbd_f, dbd_b]), jnp.stack([dcd_f, dcd_b]))
            g_lre, g_lim, g_bre, g_bim, g_cre, g_cim, g_ldt = s['s5_vjp'](cot)
            G['s5_lambda_re'], G['s5_lambda_im'], G['s5_b_re'], G['s5_b_im'] = g_lre[None], g_lim[None], g_bre[None], g_bim[None]
            G['s5_c_re'], G['s5_c_im'], G['s5_log_dt'] = g_cre[None], g_cim[None], g_ldt[None]
            if i == 0:
                G['ffn2_norm'] = jnp.stack(per_layer['ffn2_norm'], axis=0)
                early_packed = _pack([G[n] for n in small_early])
                early_started = _exchange_start("ags_small_grads_early", [early_packed], True)[0]
                tok[0] = tok[0] + early_started[4][0, 0]
            w_main, w_glo = s['w_main'], s['w_glo']
            pieces = [(du, w_main, 0), (dq, w_main, s5w // gla_hk), (dk_, w_main, s5w // gla_hk + 1),
                      (dv_, w_main, (s5w + 2 * gla_hk) // gla_hv), (dog, w_main, (s5w + 2 * gla_hk) // gla_hv + 1), (dglo, w_glo, 0)]
            dx, dg = proj_backward(f"l{i}", pieces, s, dx, f"ab_w_in_{j}")
        else:
            proj, lin_kw = s['proj'], s['lin_kw']
            w_out = s['w_out']
            d_ro = _mm_plain(f"l{i}_dro", dx, w_out, NT, BF, tm=1024, tn=1024, tk=d, b_outer=True)
            dwo = _mm_plain(f"l{i}_dwout", s['r_out'], dx, TN, BF, tm=1024, tn=d, tk=2048)
            start_a2a((f"ret_w_out_{j}", _row_blocks(dwo)))
            do, dog, dgn = _headgate_bwd(f"l{i}_ret_outb", s['o_b'], proj, s['ogcol'], dep(ret_norm_full), d_ro, 0, ret_dv)
            G['ret_norm'] = dgn
            r1 = _lin_bwd(f"l{i}_ret_bwd_f", s['qr'], s['kr'], proj, lgtab_f, s['sp_f'], do, None, reverse=False, **lin_kw)
            r2 = _lin_bwd(f"l{i}_ret_bwd_b", s['qr'], s['kr'], proj, lgtab_b, s['sp_b'], do, r1, reverse=True, **lin_kw)
            dqr, dkr, dv_ = r2
            dq, dk_ = _rot_apply(f"l{i}_rotb", dqr, 0, dkr, 0, cos_t, sin_t, RET_HEADS, ret_dk, ret_dk ** -0.5, BF, True)
            w_in = s['w_in']
            pieces = [(dq, w_in, 0), (dk_, w_in, 1), (dv_, w_in, (2 * ret_hk) // ret_hv), (dog, w_in, (2 * ret_hk) // ret_hv + 1)]
            dx, dg = proj_backward(f"l{i}", pieces, s, dx, f"ret_w_in_{j}")
        per_layer['mix_norm'][i] = dg[0]
        dx, dg = _ffn_bwd(f"l{i}_ffn1b", dx, s['x0'], ffn1_norm[i:i + 1], s['f1w1'], s['f1w2'], s['ffn1'], ffn_grads("ffn1"))
        per_layer['ffn1_norm'][i] = dg[0]
    for n in small_late:
        G[n] = jnp.stack(per_layer[n], axis=0)
    grad_x = dx[None]

    out_g, out_d, out_m, out_v = {}, {}, {}, {}
    small = small_early + small_late
    packed = _pack([G[n] for n in small_late])
    small_started = _exchange_start("ags_small_grads_late", [packed], True)[0]

    def big_update(n, layers):
        parts = []
        for i in layers:
            blocks, got = _exchange_wait(f"a2aw_{n}_{i}", a2a.pop(f"{n}_{i}"), False, small_started[4])
            parts.append(lax.dynamic_update_index_in_dim(got, lax.dynamic_index_in_dim(blocks, me, 0, keepdims=False), me, 0))
        out_g[n], out_d[n], out_m[n], out_v[n] = _reduce_adam("upd_" + n, parts, W[n], M[n], V[n])

    for n in ['ffn2_w1', 'ffn2_w2', 'ffn1_w1', 'ffn1_w2']:
        big_update(n, range(depth))
    for n in ['ab_w_in', 's5_w_glu', 'ab_w_out', 'ret_w_in', 'ret_w_out']:
        big_update(n, [0])
    assert not a2a and not pending

    g_full = {}
    for tag, started, mine, group in (("early", early_started, early_packed, small_early), ("late", small_started, packed, small_late)):
        _, gathered = _exchange_wait("agw_small_grads_" + tag, started, True, out_v['ret_w_out'])
        gathered = lax.dynamic_update_index_in_dim(gathered, mine, me, 0)
        summed = _reduce8("sum_small_grads_" + tag, gathered)
        g_full.update(zip(group, _unpack(summed, [G[n] for n in group])))
    g_small = {}
    for n in small:
        gf = g_full[n]
        if n in ('gla_w_gk', 'gla_b_gk', 'ret_norm'):
            width = W[n].shape[-1]
            gf = lax.dynamic_slice_in_dim(gf, me * width, width, axis=gf.ndim - 1)
        g_small[n] = gf.reshape(W[n].shape)
    pw, pg, pm, pv = (_pack([src[n] for n in small]) for src in (W, g_small, M, V))
    pd, pm2, pv2 = _adam_packed("upd_small", pw, pg, pm, pv)
    like = [W[n] for n in small]
    for n, dd_, mm_, vv_ in zip(small, _unpack(pd, like), _unpack(pm2, like), _unpack(pv2, like)):
        out_g[n], out_d[n], out_m[n], out_v[n] = g_small[n], dd_, mm_, vv_

    return (loss, grad_x, *[out_g[n] for n in names], *[out_d[n] for n in names], *[out_m[n] for n in names], *[out_v[n] for n in names])
```

```python
import math

import jax
import jax.numpy as jnp
from jax import lax
from jax.experimental import pallas as pl
from jax.experimental.pallas import tpu as pltpu

F32 = jnp.float32
BF = jnp.bfloat16
N_DEV = 8
EPS = 1e-6
S5_GROUP = 16
GLA_HEADS = 4
GLA_RANK = 16
GLA_GATE_NORM = 16.0
RET_HEADS = 8
ROPE_BASE = 10000.0
GLA_CHUNK = 64
RET_CHUNK = 256
ADAM_LR, ADAM_B1, ADAM_B2, ADAM_EPS, ADAM_WD, ADAM_STEP = 0.001, 0.9, 0.999, 1e-08, 0.01, 10
VMEM_LIMIT_BYTES = 56 * 1024 * 1024
LANE = 128
SUBLANE = 8

NN = (((1,), (0,)), ((), ()))
NT = (((1,), (1,)), ((), ()))
TN = (((0,), (0,)), ((), ()))


def _tile(n, pref, align):
    if n <= pref:
        return n
    t = (pref // align) * align
    while t >= align:
        if n % t == 0:
            return t
        t -= align
    return n


def _params(sem):
    return pltpu.CompilerParams(dimension_semantics=sem, vmem_limit_bytes=VMEM_LIMIT_BYTES)


def _dot(a, b, dims=NN):
    return lax.dot_general(a.astype(BF), b.astype(BF), dims, preferred_element_type=F32)


def _dot3(m01, g, dims=NN):
    g1 = g.astype(BF)
    r1 = g - g1.astype(F32)
    g2 = r1.astype(BF)
    g3 = (r1 - g2.astype(F32)).astype(BF)
    m = m01.astype(BF)
    return (lax.dot_general(m, g1, dims, preferred_element_type=F32)
            + lax.dot_general(m, g2, dims, preferred_element_type=F32)
            + lax.dot_general(m, g3, dims, preferred_element_type=F32))


def _sigmoid(x):
    return 0.5 + 0.5 * jnp.tanh(0.5 * x)


def _mm(name, pairs, dims, grid, outs, out_specs, acc_shape, epi=None, eins=(), especs=()):
    n_p, n_e, n_o = len(pairs), len(eins), len(outs)
    nk = grid[2]

    def body(*refs):
        a_refs = refs[0:2 * n_p:2]
        b_refs = refs[1:2 * n_p:2]
        e_refs = refs[2 * n_p:2 * n_p + n_e]
        o_refs = refs[2 * n_p + n_e:2 * n_p + n_e + n_o]
        acc = refs[-1]
        ids = (pl.program_id(0), pl.program_id(1), pl.program_id(2))

        part = _dot(a_refs[0][...], b_refs[0][...], dims)
        for p in range(1, n_p):
            part = part + _dot(a_refs[p][...], b_refs[p][...], dims)

        def finish(total):
            if epi is None:
                o_refs[0][...] = total.astype(o_refs[0].dtype)
            else:
                epi(total, e_refs, o_refs, ids)

        if nk == 1:
            finish(part)
        else:
            @pl.when(ids[2] == 0)
            def _():
                acc[...] = part

            @pl.when(ids[2] > 0)
            def _():
                acc[...] += part

            @pl.when(ids[2] == nk - 1)
            def _():
                finish(acc[...])

    in_specs, args = [], []
    for a, a_spec, b, b_spec in pairs:
        in_specs += [a_spec, b_spec]
        args += [a, b]
    in_specs += list(especs)
    args += list(eins)
    res = pl.pallas_call(
        body, name=name, grid=grid, in_specs=in_specs, out_specs=list(out_specs), out_shape=list(outs),
        scratch_shapes=[pltpu.VMEM(acc_shape, F32)],
        compiler_params=_params(("arbitrary", "arbitrary", "arbitrary")),
    )(*args)
    return res


def _sds(shape, dtype):
    return jax.ShapeDtypeStruct(shape, dtype)


def _mm_plain(name, a, b, dims, out_dtype, tm=512, tn=1024, tk=1024, epi=None, eins=(), especs=None, extra_outs=(), extra_specs=(),
              b_outer=False):
    if dims == NN:
        (m, k), n = a.shape, b.shape[1]
    elif dims == NT:
        (m, k), n = a.shape, b.shape[0]
    else:
        (k, m), n = a.shape, b.shape[1]
    tm, tn = _tile(m, tm, LANE if dims == TN else SUBLANE), _tile(n, tn, LANE)
    tk = _tile(k, tk, SUBLANE if dims == TN else LANE)
    grid = (n // tn, m // tm, k // tk) if b_outer else (m // tm, n // tn, k // tk)

    def spec(block, index):
        if b_outer:
            return pl.BlockSpec(block, lambda j, i, kk: index(i, j, kk))
        return pl.BlockSpec(block, index)

    if dims == NN:
        a_spec = spec((tm, tk), lambda i, j, kk: (i, kk))
        b_spec = spec((tk, tn), lambda i, j, kk: (kk, j))
    elif dims == NT:
        a_spec = spec((tm, tk), lambda i, j, kk: (i, kk))
        b_spec = spec((tn, tk), lambda i, j, kk: (j, kk))
    else:
        a_spec = spec((tk, tm), lambda i, j, kk: (kk, i))
        b_spec = spec((tk, tn), lambda i, j, kk: (kk, j))
    o_spec = spec((tm, tn), lambda i, j, kk: (i, j))
    if especs is None:
        especs = [o_spec] * len(eins)
    else:
        especs = [o_spec if s is None else s for s in especs]
    res = _mm(name, [(a, a_spec, b, b_spec)], dims, grid, [_sds((m, n), out_dtype)] + list(extra_outs),
              [o_spec] + list(extra_specs), (tm, tn), epi, eins, especs)
    return res if extra_outs else res[0]


def _rms_fwd(name, x, g):
    l, d = x.shape
    tm = _tile(l, 1024, SUBLANE)

    def body(x_ref, g_ref, o_ref):
        xv = x_ref[...]
        r = lax.rsqrt(jnp.mean(xv * xv, axis=-1, keepdims=True) + EPS)
        o_ref[...] = (xv * r * g_ref[...]).astype(o_ref.dtype)

    return pl.pallas_call(
        body, name=name, grid=(l // tm,),
        in_specs=[pl.BlockSpec((tm, d), lambda i: (i, 0)), pl.BlockSpec((1, d), lambda i: (0, 0))],
        out_specs=pl.BlockSpec((tm, d), lambda i: (i, 0)), out_shape=_sds((l, d), BF),
        compiler_params=_params(("arbitrary",)),
    )(x, g)


def _rms_bwd_epi(first_axis):
    def epi(acc, e_refs, o_refs, ids):
        x_ref, g_ref, dr_ref = e_refs
        dx_ref, dg_ref = o_refs
        xv = x_ref[...]
        r = lax.rsqrt(jnp.mean(xv * xv, axis=-1, keepdims=True) + EPS)
        xh = xv * r
        dxh = acc * g_ref[...]
        dx_ref[...] = dr_ref[...] + r * (dxh - xh * jnp.mean(dxh * xh, axis=-1, keepdims=True))
        part = jnp.sum(acc * xh, axis=0, keepdims=True)

        @pl.when(ids[first_axis] == 0)
        def _():
            dg_ref[...] = part

        @pl.when(ids[first_axis] > 0)
        def _():
            dg_ref[...] += part

    return epi


def _loss_head(name, x, g, target):
    l, d = x.shape
    tm = _tile(l, 512, SUBLANE)
    n = l // tm

    def body(x_ref, g_ref, t_ref, dx_ref, dg_ref, loss_ref, lacc):
        i = pl.program_id(0)
        xv = x_ref[...]
        r = lax.rsqrt(jnp.mean(xv * xv, axis=-1, keepdims=True) + EPS)
        xh = xv * r
        e = xh * g_ref[...] - t_ref[...]
        dy = e * (1.0 / d)
        dxh = dy * g_ref[...]
        dx_ref[...] = r * (dxh - xh * jnp.mean(dxh * xh, axis=-1, keepdims=True))
        dg_part = jnp.sum(dy * xh, axis=0, keepdims=True)
        l_part = jnp.sum(e * e, axis=0, keepdims=True)

        @pl.when(i == 0)
        def _():
            dg_ref[...] = dg_part
            lacc[...] = l_part

        @pl.when(i > 0)
        def _():
            dg_ref[...] += dg_part
            lacc[...] += l_part

        @pl.when(i == n - 1)
        def _():
            loss_ref[...] = jnp.zeros_like(loss_ref) + jnp.sum(lacc[...]) * (0.5 / d)

    return pl.pallas_call(
        body, name=name, grid=(n,),
        in_specs=[pl.BlockSpec((tm, d), lambda i: (i, 0)), pl.BlockSpec((1, d), lambda i: (0, 0)),
                  pl.BlockSpec((tm, d), lambda i: (i, 0))],
        out_specs=[pl.BlockSpec((tm, d), lambda i: (i, 0)), pl.BlockSpec((1, d), lambda i: (0, 0)),
                   pl.BlockSpec((1, LANE), lambda i: (0, 0))],
        out_shape=[_sds((l, d), F32), _sds((1, d), F32), _sds((1, LANE), F32)],
        scratch_shapes=[pltpu.VMEM((1, d), F32)],
        compiler_params=_params(("arbitrary",)),
    )(x, g, target)


def _ffn_up(name, hn, w1):
    l, d = hn.shape
    f = w1.shape[1] // 2
    tm, tn = _tile(l, 512, SUBLANE), _tile(f, 1408, LANE)
    nj = f // tn

    def body(h_ref, wg_ref, wu_ref, gu_ref, a_ref):
        h = h_ref[...]
        g = jnp.dot(h, wg_ref[...], preferred_element_type=F32)
        u = jnp.dot(h, wu_ref[...], preferred_element_type=F32)
        s = _sigmoid(g)
        gs = g * s
        gu_ref[0] = (u * (s + gs * (1.0 - s))).astype(BF)
        gu_ref[1] = gs.astype(BF)
        a_ref[...] = (gs * u).astype(BF)

    return pl.pallas_call(
        body, name=name, grid=(nj, l // tm),
        in_specs=[pl.BlockSpec((tm, d), lambda j, i: (i, 0)), pl.BlockSpec((d, tn), lambda j, i: (0, j)),
                  pl.BlockSpec((d, tn), lambda j, i: (0, j + nj))],
        out_specs=[pl.BlockSpec((2, tm, tn), lambda j, i: (0, i, j)), pl.BlockSpec((tm, tn), lambda j, i: (i, j))],
        out_shape=[_sds((2, l, f), BF), _sds((l, f), BF)],
        compiler_params=_params(("arbitrary", "arbitrary")),
    )(hn, w1, w1)


def _residual_epi(scale, with_norm):
    def epi(acc, e_refs, o_refs, ids):
        xn = e_refs[0][...] + scale * acc
        o_refs[0][...] = xn
        if with_norm:
            r = lax.rsqrt(jnp.mean(xn * xn, axis=-1, keepdims=True) + EPS)
            o_refs[1][...] = (xn * r * e_refs[1][...]).astype(o_refs[1].dtype)

    return epi


def _ffn_fwd(tag, x, hn, get_w1, get_w2, next_gnorm):
    w1 = get_w1(hn)
    gu, a = _ffn_up(tag + "_up", hn, w1)
    w2 = get_w2(a)
    l, d = x.shape
    if next_gnorm is None:
        x_new = _mm_plain(tag + "_down", a, w2, NN, F32, tm=512, tn=d, tk=w2.shape[0], epi=_residual_epi(0.5, False), eins=[x])
        hn_next = None
    else:
        vec = pl.BlockSpec((1, d), lambda i, j, kk: (0, 0))
        tm = _tile(l, 512, SUBLANE)
        x_new, hn_next = _mm_plain(tag + "_down", a, w2, NN, F32, tm=512, tn=d, tk=w2.shape[0], epi=_residual_epi(0.5, True),
                                   eins=[x, next_gnorm], especs=[None, vec],
                                   extra_outs=[_sds((l, d), BF)], extra_specs=[pl.BlockSpec((tm, d), lambda i, j, kk: (i, 0))])
    return x_new, hn_next, (hn, gu, a), w1, w2


def _ffn_bwd(tag, dres, x, gnorm, w1, w2, saved, on_grads):
    hn, gu, a = saved
    l, d = x.shape
    f = w2.shape[0]
    tm, tn = _tile(l, 512, SUBLANE), _tile(f, 1408, LANE)
    nj = f // tn

    def epi_gu(acc, e_refs, o_refs, ids):
        da = 0.5 * acc
        o_refs[0][0] = (da * e_refs[0][0].astype(F32)).astype(BF)
        o_refs[0][1] = (da * e_refs[0][1].astype(F32)).astype(BF)

    tmg = _tile(l, 256, SUBLANE)
    gu_spec = pl.BlockSpec((2, tmg, f), lambda j, i, kk: (0, i, 0))
    dgu = _mm(tag + "_dgu",
              [(dres, pl.BlockSpec((tmg, d), lambda j, i, kk: (i, 0)), w2, pl.BlockSpec((f, d), lambda j, i, kk: (0, 0)))],
              NT, (1, l // tmg, 1), [_sds((2, l, f), BF)], [gu_spec], (tmg, f), epi_gu, [gu], [gu_spec])[0]

    def epi_half(acc, e_refs, o_refs, ids):
        o_refs[0][...] = (0.5 * acc).astype(BF)

    dw2 = _mm_plain(tag + "_dw2", a, dres, TN, BF, tm=1408, tn=d, tk=2048, epi=epi_half)

    tk = _tile(l, 2048, SUBLANE)
    dw1 = _mm(tag + "_dw1",
              [(hn, pl.BlockSpec((tk, d), lambda i, j, kk: (kk, 0)), dgu, pl.BlockSpec((None, tk, tn), lambda i, j, kk: (j // nj, kk, j % nj)))],
              TN, (1, 2 * nj, l // tk), [_sds((d, 2 * f), BF)], [pl.BlockSpec((d, tn), lambda i, j, kk: (0, j))], (d, tn))[0]

    gnorm = on_grads(dw1, dw2, gnorm)
    row = pl.BlockSpec((tm, d), lambda i, j, kk: (i, 0))
    vec = pl.BlockSpec((1, d), lambda i, j, kk: (0, 0))
    once = pl.Buffered(1)
    dx, dg = _mm(tag + "_dhn",
                 [(dgu, pl.BlockSpec((None, tm, f), lambda i, j, kk: (0, i, 0)), w1, pl.BlockSpec((d, f), lambda i, j, kk: (0, 0), pipeline_mode=once)),
                  (dgu, pl.BlockSpec((None, tm, f), lambda i, j, kk: (1, i, 0)), w1, pl.BlockSpec((d, f), lambda i, j, kk: (0, 1), pipeline_mode=once))],
                 NT, (l // tm, 1, 1), [_sds((l, d), F32), _sds((1, d), F32)], [row, vec], (tm, d),
                 _rms_bwd_epi(0), [x, gnorm, dres], [row, vec, row])
    return dx, dg


def _s5_chunk_tables(lam_re, lam_im, b_re, b_im, c_re, c_im, log_dt, hs):
    f32 = F32
    g, n = lam_re.shape[1], lam_re.shape[2]
    p = b_re.shape[-1]
    nch, gpc, nt = (g * n) // hs, hs // n, hs // LANE
    lr = jnp.minimum(lam_re.astype(f32), -1e-4)
    li = lam_im.astype(f32)
    dt = jnp.exp(log_dt.astype(f32))[..., None]
    mag = jnp.exp(lr * dt)
    ar = mag * jnp.cos(li * dt)
    ai = mag * jnp.sin(li * dt)
    den = lr * lr + li * li
    cr = ((ar - 1.0) * lr + ai * li) / den
    ci = (ai * lr - (ar - 1.0) * li) / den
    bbr = cr[..., None] * b_re - ci[..., None] * b_im
    bbi = cr[..., None] * b_im + ci[..., None] * b_re
    a_f = jnp.stack([ar, ai], axis=1).reshape(2, 2, nch, nt, LANE).transpose(0, 2, 1, 3, 4).reshape(2, nch, 2 * nt, LANE)
    rows_g = jnp.arange(gpc * p) // p
    cols_g = (jnp.arange(2 * hs) % hs) // n
    diag = (rows_g[:, None] == cols_g[None, :]).astype(f32)
    bb = jnp.stack([bbr, bbi], axis=1).reshape(2, 2, nch, hs, p)
    bd = jnp.tile(bb.transpose(0, 2, 4, 1, 3).reshape(2, nch, p, 2 * hs), (1, 1, gpc, 1)) * diag
    cc = jnp.stack([c_re, -c_im], axis=1).reshape(2, 2, nch, gpc, p, n)
    cd = jnp.tile(cc.transpose(0, 2, 4, 1, 3, 5).reshape(2, nch, p, 2 * hs), (1, 1, gpc, 1)) * diag
    return a_f, bd, cd


def _fold_store(ref, val, tb, ntiles):
    for s in range(ntiles):
        ref[:, s * SUBLANE:(s + 1) * SUBLANE, :] = val[:, s * LANE:(s + 1) * LANE].reshape(tb // SUBLANE, SUBLANE, LANE)


def _unfold(ref, tb, ntiles):
    return jnp.concatenate([ref[:, s * SUBLANE:(s + 1) * SUBLANE, :].reshape(tb, LANE) for s in range(ntiles)], axis=1)


def _s5_fwd(name, proj, bd, cd, a_f, reverse):
    l = proj.shape[0]
    nch, cu, hs2 = bd.shape
    nt = hs2 // (2 * LANE)
    frows = 2 * nt * SUBLANE
    tb = _tile(l, 512, SUBLANE)
    nb = l // tb

    def body(u_ref, bd_ref, cd_ref, a_ref, xf_ref, y_ref, st):
        r = pl.program_id(1)

        @pl.when(r == 0)
        def _():
            st[...] = jnp.zeros_like(st)

        _fold_store(xf_ref, _dot(u_ref[...], bd_ref[...]), tb, 2 * nt)
        ar, ai = a_ref[0:nt, :], a_ref[nt:2 * nt, :]

        def group(gi, carry):
            rr = (tb // SUBLANE - 1 - gi) if reverse else gi
            sr, si = carry
            for qq in range(SUBLANE):
                q = (SUBLANE - 1 - qq) if reverse else qq
                re_rows, im_rows = pl.ds(q, nt, stride=SUBLANE), pl.ds(nt * SUBLANE + q, nt, stride=SUBLANE)
                nr = ar * sr - ai * si + xf_ref[rr, re_rows, :]
                ni = ar * si + ai * sr + xf_ref[rr, im_rows, :]
                xf_ref[rr, re_rows, :] = nr
                xf_ref[rr, im_rows, :] = ni
                sr, si = nr, ni
            return sr, si

        fin = lax.fori_loop(0, tb // SUBLANE, group, (st[0:nt, :], st[nt:2 * nt, :]))
        st[0:nt, :] = fin[0]
        st[nt:2 * nt, :] = fin[1]
        y_ref[...] = _dot(_unfold(xf_ref, tb, 2 * nt), cd_ref[...], NT)

    def rows(r):
        return (nb - 1 - r) if reverse else r

    return pl.pallas_call(
        body, name=name, grid=(nch, nb),
        in_specs=[pl.BlockSpec((tb, cu), lambda c, r: (rows(r), c)), pl.BlockSpec((None, cu, hs2), lambda c, r: (c, 0, 0)),
                  pl.BlockSpec((None, cu, hs2), lambda c, r: (c, 0, 0)), pl.BlockSpec((None, 2 * nt, LANE), lambda c, r: (c, 0, 0))],
        out_specs=[pl.BlockSpec((tb // SUBLANE, frows, LANE), lambda c, r: (rows(r), c, 0)), pl.BlockSpec((tb, cu), lambda c, r: (rows(r), c))],
        out_shape=[_sds((l // SUBLANE, nch * frows, LANE), F32), _sds((l, nch * cu), F32)],
        scratch_shapes=[pltpu.VMEM((2 * nt, LANE), F32)],
        compiler_params=_params(("arbitrary", "arbitrary")),
    )(proj, bd, cd, a_f)


def _s5_bwd(name, proj, dy, xf, bd, cd, a_conj, reverse):
    l = proj.shape[0]
    nch, cu, hs2 = bd.shape
    nt = hs2 // (2 * LANE)
    frows = 2 * nt * SUBLANE
    tb = _tile(l, 512, SUBLANE)
    nb = l // tb

    def body(u_ref, dy_ref, xs_ref, bd_ref, cd_ref, a_ref, du_ref, dbd_ref, dcd_ref, da_ref, lam, st):
        r = pl.program_id(1)

        @pl.when(r == 0)
        def _():
            st[...] = jnp.zeros_like(st)
            dbd_ref[...] = jnp.zeros_like(dbd_ref)
            dcd_ref[...] = jnp.zeros_like(dcd_ref)
            da_ref[...] = jnp.zeros_like(da_ref)

        dyv = dy_ref[...]
        _fold_store(lam, _dot(dyv, cd_ref[...]), tb, 2 * nt)
        ar, ai = a_ref[0:nt, :], a_ref[nt:2 * nt, :]

        def group(gi, carry):
            rr = (tb // SUBLANE - 1 - gi) if reverse else gi
            sr, si, cr, ci = carry
            for qq in range(SUBLANE):
                q = (SUBLANE - 1 - qq) if reverse else qq
                re_rows, im_rows = pl.ds(q, nt, stride=SUBLANE), pl.ds(nt * SUBLANE + q, nt, stride=SUBLANE)
                xr, xi = xs_ref[rr, re_rows, :], xs_ref[rr, im_rows, :]
                cr = cr + sr * xr + si * xi
                ci = ci + si * xr - sr * xi
                nr = ar * sr - ai * si + lam[rr, re_rows, :]
                ni = ar * si + ai * sr + lam[rr, im_rows, :]
                lam[rr, re_rows, :] = nr
                lam[rr, im_rows, :] = ni
                sr, si = nr, ni
            return sr, si, cr, ci

        zero = jnp.zeros((nt, LANE), F32)
        fin = lax.fori_loop(0, tb // SUBLANE, group, (st[0:nt, :], st[nt:2 * nt, :], zero, zero))
        st[0:nt, :] = fin[0]
        st[nt:2 * nt, :] = fin[1]
        da_ref[0:nt, :] += fin[2]
        da_ref[nt:2 * nt, :] += fin[3]
        lam_u = _unfold(lam, tb, 2 * nt)
        du_ref[...] = _dot(lam_u, bd_ref[...], NT)
        dbd_ref[...] += _dot(u_ref[...], lam_u, TN)
        dcd_ref[...] += _dot(dyv, _unfold(xs_ref, tb, 2 * nt), TN)

    def rows(r):
        return (nb - 1 - r) if reverse else r

    chunk_rows = pl.BlockSpec((tb, cu), lambda c, r: (rows(r), c))
    bd_spec = pl.BlockSpec((None, cu, hs2), lambda c, r: (c, 0, 0))
    cd_spec = bd_spec
    a_spec = pl.BlockSpec((None, 2 * nt, LANE), lambda c, r: (c, 0, 0))
    return pl.pallas_call(
        body, name=name, grid=(nch, nb),
        in_specs=[chunk_rows, chunk_rows, pl.BlockSpec((tb // SUBLANE, frows, LANE), lambda c, r: (rows(r), c, 0)), bd_spec, cd_spec, a_spec],
        out_specs=[chunk_rows, bd_spec, cd_spec, a_spec],
        out_shape=[_sds((l, nch * cu), F32), _sds((nch, cu, hs2), F32), _sds((nch, cu, hs2), F32), _sds((nch, 2 * nt, LANE), F32)],
        scratch_shapes=[pltpu.VMEM((tb // SUBLANE, frows, LANE), F32), pltpu.VMEM((2 * nt, LANE), F32)],
        compiler_params=_params(("arbitrary", "arbitrary")),
    )(proj, dy, xf, bd, cd, a_conj)


def _s5_du(name, du_f, du_b, dy, proj, d_row):
    l, w = dy.shape
    tm = _tile(l, 1024, SUBLANE)

    def body(f_ref, b_ref, dy_ref, u_ref, d_ref, du_ref, dd_ref):
        i = pl.program_id(0)
        dyv = dy_ref[...]
        du_ref[...] = (f_ref[...] + b_ref[...] + dyv * d_ref[...]).astype(du_ref.dtype)
        part = jnp.sum(dyv * u_ref[...].astype(F32), axis=0, keepdims=True)

        @pl.when(i == 0)
        def _():
            dd_ref[...] = part

        @pl.when(i > 0)
        def _():
            dd_ref[...] += part

    row = pl.BlockSpec((tm, w), lambda i: (i, 0))
    vec = pl.BlockSpec((1, w), lambda i: (0, 0))
    return pl.pallas_call(
        body, name=name, grid=(l // tm,), in_specs=[row, row, row, row, vec], out_specs=[row, vec],
        out_shape=[_sds((l, w), BF), _sds((1, w), F32)],
        compiler_params=_params(("arbitrary",)),
    )(du_f, du_b, dy, proj, d_row)


def _gelu(y):
    c = math.sqrt(2.0 / math.pi)
    return 0.5 * y * (1.0 + jnp.tanh(c * (y + 0.044715 * y * y * y)))


def _gelu_grad(y):
    c = math.sqrt(2.0 / math.pi)
    th = jnp.tanh(c * (y + 0.044715 * y * y * y))
    return 0.5 * (1.0 + th) + 0.5 * y * (1.0 - th * th) * c * (1.0 + 3.0 * 0.044715 * y * y)


def _glu_fwd(name, y_f, y_b, proj, d_row, w):
    l, wd = y_f.shape
    tm = _tile(l, 1024, SUBLANE)

    def body(yf_ref, yb_ref, u_ref, d_ref, w_ref, y_ref, o_ref):
        y = yf_ref[...] + yb_ref[...] + u_ref[...].astype(F32) * d_ref[...]
        y_ref[...] = y
        gy = _gelu(y)
        z = _dot(gy, w_ref[...])
        o_ref[...] = (gy * _sigmoid(z)).astype(o_ref.dtype)

    row = pl.BlockSpec((tm, wd), lambda i: (i, 0))
    return pl.pallas_call(
        body, name=name, grid=(l // tm,),
        in_specs=[row, row, row, pl.BlockSpec((1, wd), lambda i: (0, 0)), pl.BlockSpec((wd, wd), lambda i: (0, 0))],
        out_specs=[row, row], out_shape=[_sds((l, wd), F32), _sds((l, wd), BF)],
        compiler_params=_params(("arbitrary",)),
    )(y_f, y_b, proj, d_row, w)


def _glu_bwd(name, y, w, dout, dcol):
    l, wd = y.shape
    tm = _tile(l, 1024, SUBLANE)

    def body(y_ref, w_ref, d_ref, dy_ref, dw_ref):
        i = pl.program_id(0)
        yv = y_ref[...]
        gy = _gelu(yv)
        s = _sigmoid(_dot(gy, w_ref[...]))
        d = d_ref[...].astype(F32)
        t = d * gy * s * (1.0 - s)
        dgy = d * s + _dot(t, w_ref[...], NT)
        dy_ref[...] = dgy * _gelu_grad(yv)
        part = _dot(gy, t, TN)

        @pl.when(i == 0)
        def _():
            dw_ref[...] = part

        @pl.when(i > 0)
        def _():
            dw_ref[...] += part

    return pl.pallas_call(
        body, name=name, grid=(l // tm,),
        in_specs=[pl.BlockSpec((tm, wd), lambda i: (i, 0)), pl.BlockSpec((wd, wd), lambda i: (0, 0)),
                  pl.BlockSpec((tm, wd), lambda i: (i, dcol))],
        out_specs=[pl.BlockSpec((tm, wd), lambda i: (i, 0)), pl.BlockSpec((wd, wd), lambda i: (0, 0))],
        out_shape=[_sds((l, wd), F32), _sds((wd, wd), F32)],
        compiler_params=_params(("arbitrary",)),
    )(y, w, dout)


def _log_sigmoid(x):
    return jnp.minimum(x, 0.0) - jnp.log(1.0 + jnp.exp(-jnp.abs(x)))


def _gate_fwd(name, glo, wf, wb, bf, bb):
    l, r2 = glo.shape
    hk = wf.shape[1]
    tm = _tile(l, 1024, SUBLANE)

    def body(x_ref, wf_ref, wb_ref, bf_ref, bb_ref, gf_ref, gb_ref):
        xv = x_ref[...]
        gf_ref[...] = _log_sigmoid(_dot(xv, wf_ref[...]) + bf_ref[...]) * (1.0 / GLA_GATE_NORM)
        gb_ref[...] = _log_sigmoid(_dot(xv, wb_ref[...]) + bb_ref[...]) * (1.0 / GLA_GATE_NORM)

    w_spec = pl.BlockSpec((r2, hk), lambda i: (0, 0))
    b_spec = pl.BlockSpec((1, hk), lambda i: (0, 0))
    o_spec = pl.BlockSpec((tm, hk), lambda i: (i, 0))
    return pl.pallas_call(
        body, name=name, grid=(l // tm,),
        in_specs=[pl.BlockSpec((tm, r2), lambda i: (i, 0)), w_spec, w_spec, b_spec, b_spec],
        out_specs=[o_spec, o_spec], out_shape=[_sds((l, hk), F32), _sds((l, hk), F32)],
        compiler_params=_params(("arbitrary",)),
    )(glo, wf, wb, bf, bb)


def _gate_bwd(name, glo, wf, wb, bf, bb, dgf, dgb):
    l, r2 = glo.shape
    hk = wf.shape[1]
    tm = _tile(l, 1024, SUBLANE)

    def body(x_ref, wf_ref, wb_ref, bf_ref, bb_ref, dgf_ref, dgb_ref, dx_ref, dwf_ref, dwb_ref, dbf_ref, dbb_ref):
        i = pl.program_id(0)
        xv = x_ref[...]
        kf = _dot(xv, wf_ref[...]) + bf_ref[...]
        kb = _dot(xv, wb_ref[...]) + bb_ref[...]
        dkf = dgf_ref[...] * (1.0 / GLA_GATE_NORM) * _sigmoid(-kf)
        dkb = dgb_ref[...] * (1.0 / GLA_GATE_NORM) * _sigmoid(-kb)
        dx_ref[...] = _dot(dkf, wf_ref[...], NT) + _dot(dkb, wb_ref[...], NT)
        parts = (_dot(xv, dkf, TN), _dot(xv, dkb, TN), jnp.sum(dkf, axis=0, keepdims=True), jnp.sum(dkb, axis=0, keepdims=True))
        accs = (dwf_ref, dwb_ref, dbf_ref, dbb_ref)

        @pl.when(i == 0)
        def _():
            for a_, p_ in zip(accs, parts):
                a_[...] = p_

        @pl.when(i > 0)
        def _():
            for a_, p_ in zip(accs, parts):
                a_[...] += p_

    w_spec = pl.BlockSpec((r2, hk), lambda i: (0, 0))
    b_spec = pl.BlockSpec((1, hk), lambda i: (0, 0))
    g_spec = pl.BlockSpec((tm, hk), lambda i: (i, 0))
    x_spec = pl.BlockSpec((tm, r2), lambda i: (i, 0))
    return pl.pallas_call(
        body, name=name, grid=(l // tm,),
        in_specs=[x_spec, w_spec, w_spec, b_spec, b_spec, g_spec, g_spec],
        out_specs=[x_spec, w_spec, w_spec, b_spec, b_spec],
        out_shape=[_sds((l, r2), F32), _sds((r2, hk), F32), _sds((r2, hk), F32), _sds((1, hk), F32), _sds((1, hk), F32)],
        compiler_params=_params(("arbitrary",)),
    )(glo, wf, wb, bf, bb, dgf, dgb)


def _chunk_terms(qc, kc, lg, chunk, reverse):
    ri = lax.broadcasted_iota(jnp.int32, (chunk, chunk), 0)
    ci = lax.broadcasted_iota(jnp.int32, (chunk, chunk), 1)
    mask = (ci > ri) if reverse else (ci <= ri)
    pos = lax.broadcasted_iota(jnp.int32, (chunk, 1), 0).astype(F32)
    cum = ((chunk - pos) if reverse else (pos + 1.0)) * lg
    last = chunk * lg
    e = jnp.exp(cum)
    einv = jnp.exp(-cum)
    dec = jnp.exp(last - cum)
    return e, einv, dec, qc * e, kc * einv, kc * dec, jnp.exp(last), mask


def _lin_specs(width, col, tb, nb, reverse):
    return pl.BlockSpec((tb, width), lambda h, r: ((nb - 1 - r) if reverse else r, col + h))


def _lin_fwd(name, q, k, v, lgtab, prev_o=None, *, heads, hb, dk, dv, chunk, tb, qcol, kcol, vcol, reverse):
    l = q.shape[0]
    nb, ncb, ng = l // tb, tb // chunk, heads // hb

    def body(q_ref, k_ref, v_ref, lg_ref, *rest):
        p_ref = rest[0] if prev_o is not None else None
        o_ref, sp_ref, st = rest[-3:]

        @pl.when(pl.program_id(1) == 0)
        def _():
            st[...] = jnp.zeros_like(st)

        for c in range(ncb):
            cc = (ncb - 1 - c) if reverse else c
            rows = pl.ds(cc * chunk, chunk)
            for h in range(hb):
                ks, vs = slice(h * dk, (h + 1) * dk), slice(h * dv, (h + 1) * dv)
                qc, kc, vc = q_ref[rows, ks].astype(F32), k_ref[rows, ks].astype(F32), v_ref[rows, vs]
                _, _, _, qd, ki, kdec, e_last, mask = _chunk_terms(qc, kc, lg_ref[h, :, 0:1], chunk, reverse)
                a = jnp.where(mask, _dot(qd, ki, NT), 0.0)
                s_t = st[h]
                oc = _dot(a, vc) + _dot(qd, s_t, NT)
                if p_ref is not None:
                    oc = oc + p_ref[rows, vs]
                o_ref[rows, vs] = oc.astype(o_ref.dtype)
                sp_ref[cc, h] = s_t
                st[h] = s_t * e_last + _dot(vc, kdec, TN)

    o_spec = _lin_specs(hb * dv, 0, tb, nb, reverse)
    extra = [] if prev_o is None else [prev_o]
    return pl.pallas_call(
        body, name=name, grid=(ng, nb),
        in_specs=[_lin_specs(hb * dk, qcol, tb, nb, reverse), _lin_specs(hb * dk, kcol, tb, nb, reverse),
                  _lin_specs(hb * dv, vcol, tb, nb, reverse), pl.BlockSpec((hb, 1, LANE), lambda h, r: (h, 0, 0))] + [o_spec] * len(extra),
        out_specs=[o_spec, pl.BlockSpec((ncb, hb, dv, dk), lambda h, r: ((nb - 1 - r) if reverse else r, h, 0, 0))],
        out_shape=[_sds((l, heads * dv), F32 if prev_o is None else BF), _sds((l // chunk, heads, dv, dk), F32)],
        scratch_shapes=[pltpu.VMEM((hb, dv, dk), F32)],
        compiler_params=_params(("arbitrary", "arbitrary")),
    )(q, k, v, lgtab, *extra)


def _lin_bwd(name, q, k, v, lgtab, sprev, do, prev, *, heads, hb, dk, dv, chunk, tb, qcol, kcol, vcol, reverse):
    l = q.shape[0]
    nb, ncb, ng = l // tb, tb // chunk, heads // hb
    brev = not reverse
    n_prev = 0 if prev is None else len(prev)

    def body(q_ref, k_ref, v_ref, lg_ref, sp_ref, do_ref, *rest):
        p_refs = rest[:n_prev]
        dq_ref, dk_ref, dv_ref, dst = rest[n_prev:]

        @pl.when(pl.program_id(1) == 0)
        def _():
            dst[...] = jnp.zeros_like(dst)

        for c in range(ncb):
            cc = (ncb - 1 - c) if brev else c
            rows = pl.ds(cc * chunk, chunk)
            for h in range(hb):
                ks, vs = slice(h * dk, (h + 1) * dk), slice(h * dv, (h + 1) * dv)
                qc, kc, vc = q_ref[rows, ks].astype(F32), k_ref[rows, ks].astype(F32), v_ref[rows, vs]
                e, einv, dec, qd, ki, kdec, e_last, mask = _chunk_terms(qc, kc, lg_ref[h, :, 0:1], chunk, reverse)
                a = jnp.where(mask, _dot(qd, ki, NT), 0.0)
                s_t, ds_t, doc = sp_ref[cc, h], dst[h], do_ref[rows, vs]
                dvc = _dot(a, doc, TN) + _dot(kdec, ds_t, NT)
                da = jnp.where(mask, _dot(doc, vc, NT), 0.0)
                dqc = (_dot(da, ki) + _dot(doc, s_t)) * e
                dkc = _dot(da, qd, TN) * einv + _dot(vc, ds_t) * dec
                dst[h] = ds_t * e_last + _dot(doc, qd, TN)
                if n_prev:
                    dqc = dqc + p_refs[0][rows, ks]
                    dkc = dkc + p_refs[1][rows, ks]
                    dvc = dvc + p_refs[2][rows, vs]
                dq_ref[rows, ks] = dqc
                dk_ref[rows, ks] = dkc
                dv_ref[rows, vs] = dvc.astype(dv_ref.dtype)

    k_spec, v_spec = _lin_specs(hb * dk, 0, tb, nb, brev), _lin_specs(hb * dv, 0, tb, nb, brev)
    in_specs = [_lin_specs(hb * dk, qcol, tb, nb, brev), _lin_specs(hb * dk, kcol, tb, nb, brev), _lin_specs(hb * dv, vcol, tb, nb, brev),
                pl.BlockSpec((hb, 1, LANE), lambda h, r: (h, 0, 0)),
                pl.BlockSpec((ncb, hb, dv, dk), lambda h, r: ((nb - 1 - r) if brev else r, h, 0, 0)), v_spec]
    args = [q, k, v, lgtab, sprev, do]
    if n_prev:
        in_specs += [k_spec, k_spec, v_spec]
        args += list(prev)
    return pl.pallas_call(
        body, name=name, grid=(ng, nb), in_specs=in_specs, out_specs=[k_spec, k_spec, v_spec],
        out_shape=[_sds((l, heads * dk), F32), _sds((l, heads * dk), F32), _sds((l, heads * dv), BF if n_prev else F32)],
        scratch_shapes=[pltpu.VMEM((hb, dv, dk), F32)],
        compiler_params=_params(("arbitrary", "arbitrary")),
    )(*args)


def _log2(n):
    assert n & (n - 1) == 0, "a power of two"
    return n.bit_length() - 1


def _gla_block_terms(q, k, g, qscale, chunk, tb, reverse):
    ri = lax.broadcasted_iota(jnp.int32, (tb, tb), 0)
    ci = lax.broadcasted_iota(jnp.int32, (tb, tb), 1)
    same = jnp.right_shift(ri, _log2(chunk)) == jnp.right_shift(ci, _log2(chunk))
    t_in = jnp.logical_and(same, (ci >= ri) if reverse else (ci <= ri)).astype(F32)
    cum = _dot3(t_in, g)
    tot = _dot3(same.astype(F32), g)
    e = jnp.exp(cum)
    einv = jnp.exp(-cum)
    dec = jnp.exp(tot - cum)
    return e, einv, dec, jnp.exp(tot), q * (qscale * e), k * einv, k * dec, t_in


def _gla_masks(hk, dk, heads, chunk, reverse):
    lane = lax.broadcasted_iota(jnp.int32, (1, hk), 1)
    head_of = jnp.right_shift(lane, _log2(dk))
    ri = lax.broadcasted_iota(jnp.int32, (chunk, chunk), 0)
    ci = lax.broadcasted_iota(jnp.int32, (chunk, chunk), 1)
    return [head_of == h for h in range(heads)], ((ci > ri) if reverse else (ci <= ri))


def _gla_fwd(name, proj, g, prev_o=None, *, heads, dk, dv, chunk, tb, qcol, kcol, vcol, qscale, reverse):
    l = proj.shape[0]
    nb, ncb, hk, hv = l // tb, tb // chunk, heads * dk, heads * dv

    def body(q_ref, k_ref, v_ref, g_ref, *rest):
        p_ref = rest[0] if prev_o is not None else None
        o_ref, sp_ref, st = rest[-3:]

        @pl.when(pl.program_id(0) == 0)
        def _():
            st[...] = jnp.zeros_like(st)

        _, _, _, etot, qd, ki, kdec, _ = _gla_block_terms(q_ref[...].astype(F32), k_ref[...].astype(F32), g_ref[...], qscale, chunk, tb, reverse)
        heads_m, causal = _gla_masks(hk, dk, heads, chunk, reverse)
        order = [(ncb - 1 - c) if reverse else c for c in range(ncb)]
        pre = []
        for cc in order:
            rc = slice(cc * chunk, (cc + 1) * chunk)
            qd_c, ki_c, kdec_c = qd[rc], ki[rc], kdec[rc]
            kv = jnp.zeros((dv, hk), F32)
            per_head = []
            for h in range(heads):
                qm = jnp.where(heads_m[h], qd_c, 0.0)
                a = jnp.where(causal, _dot(qm, ki_c, NT), 0.0)
                vc = v_ref[rc, h * dv:(h + 1) * dv]
                per_head.append((qm, _dot(a, vc)))
                kv = kv + jnp.where(heads_m[h], _dot(vc, kdec_c, TN), 0.0)
            pre.append((per_head, kv))
        s_all = st[...]
        s_in = []
        for cc, (_, kv) in zip(order, pre):
            s_in.append(s_all)
            s_all = s_all * etot[cc * chunk:cc * chunk + 1, :] + kv
        st[...] = s_all
        for cc, (per_head, _), s_all in zip(order, pre, s_in):
            rc = slice(cc * chunk, (cc + 1) * chunk)
            sp_ref[cc] = s_all
            for h, (qm, o_intra) in enumerate(per_head):
                vs = slice(h * dv, (h + 1) * dv)
                oc = o_intra + _dot(qm, s_all, NT)
                if p_ref is not None:
                    oc = oc + p_ref[rc, vs]
                o_ref[rc, vs] = oc.astype(o_ref.dtype)

    def rows(r):
        return (nb - 1 - r) if reverse else r

    o_spec = pl.BlockSpec((tb, hv), lambda r: (rows(r), 0))
    extra = [] if prev_o is None else [prev_o]
    return pl.pallas_call(
        body, name=name, grid=(nb,),
        in_specs=[pl.BlockSpec((tb, hk), lambda r: (rows(r), qcol)), pl.BlockSpec((tb, hk), lambda r: (rows(r), kcol)),
                  pl.BlockSpec((tb, hv), lambda r: (rows(r), vcol)), pl.BlockSpec((tb, hk), lambda r: (rows(r), 0))] + [o_spec] * len(extra),
        out_specs=[o_spec, pl.BlockSpec((ncb, dv, hk), lambda r: (rows(r), 0, 0))],
        out_shape=[_sds((l, hv), F32 if prev_o is None else BF), _sds((l // chunk, dv, hk), F32)],
        scratch_shapes=[pltpu.VMEM((dv, hk), F32)],
        compiler_params=_params(("arbitrary",)),
    )(proj, proj, proj, g, *extra)


def _gla_bwd(name, proj, g, sprev, do, prev, *, heads, dk, dv, chunk, tb, qcol, kcol, vcol, qscale, reverse):
    l = proj.shape[0]
    nb, ncb, hk, hv = l // tb, tb // chunk, heads * dk, heads * dv
    brev = not reverse
    n_prev = 0 if prev is None else len(prev)

    def body(q_ref, k_ref, v_ref, g_ref, sp_ref, do_ref, *rest):
        p_refs = rest[:n_prev]
        dq_ref, dk_ref, dv_ref, dg_ref, dst, dcs = rest[n_prev:]

        @pl.when(pl.program_id(0) == 0)
        def _():
            dst[...] = jnp.zeros_like(dst)

        e, einv, dec, etot, qd, ki, kdec, t_in = _gla_block_terms(q_ref[...].astype(F32), k_ref[...].astype(F32), g_ref[...], qscale, chunk, tb, reverse)
        heads_m, causal = _gla_masks(hk, dk, heads, chunk, reverse)
        last_row = lax.broadcasted_iota(jnp.int32, (chunk, 1), 0) == (0 if reverse else chunk - 1)
        order = [(ncb - 1 - c) if brev else c for c in range(ncb)]
        pre = []
        for cc in order:
            rc = slice(cc * chunk, (cc + 1) * chunk)
            qd_c, ki_c = qd[rc], ki[rc]
            s_all = sp_ref[cc]
            dqd = jnp.zeros((chunk, hk), F32)
            dki = jnp.zeros((chunk, hk), F32)
            ds_add = jnp.zeros((dv, hk), F32)
            per_head = []
            for h in range(heads):
                vs = slice(h * dv, (h + 1) * dv)
                m = heads_m[h]
                qm = jnp.where(m, qd_c, 0.0)
                a = jnp.where(causal, _dot(qm, ki_c, NT), 0.0)
                doc, vc = do_ref[rc, vs], v_ref[rc, vs]
                da = jnp.where(causal, _dot(doc, vc, NT), 0.0)
                dqd = dqd + jnp.where(m, _dot(da, ki_c) + _dot(doc, s_all), 0.0)
                dki = dki + _dot(da, qm, TN)
                ds_add = ds_add + _dot(doc, qm, TN)
                per_head.append((_dot(a, doc, TN), vc))
            pre.append((dqd, dki, ds_add, per_head, s_all))
        ds_all = dst[...]
        ds_in = []
        for cc, item in zip(order, pre):
            ds_in.append(ds_all)
            ds_all = ds_all * etot[cc * chunk:cc * chunk + 1, :] + item[2]
        dst[...] = ds_all
        for cc, (dqd, dki, _, per_head, s_all), ds_all in zip(order, pre, ds_in):
            rc = slice(cc * chunk, (cc + 1) * chunk)
            qd_c, ki_c, kdec_c = qd[rc], ki[rc], kdec[rc]
            et = etot[cc * chunk:cc * chunk + 1, :]
            dkdec = jnp.zeros((chunk, hk), F32)
            for h, (dv_part, vc) in enumerate(per_head):
                vs = slice(h * dv, (h + 1) * dv)
                m = heads_m[h]
                dvc = dv_part + _dot(jnp.where(m, kdec_c, 0.0), ds_all, NT)
                if n_prev:
                    dvc = dvc + p_refs[2][rc, vs]
                dv_ref[rc, vs] = dvc.astype(dv_ref.dtype)
                dkdec = dkdec + jnp.where(m, _dot(vc, ds_all), 0.0)
            dqc = dqd * e[rc] * qscale
            dkc = dki * einv[rc] + dkdec * dec[rc]
            if n_prev:
                dqc = dqc + p_refs[0][rc, :]
                dkc = dkc + p_refs[1][rc, :]
            dq_ref[rc, :] = dqc
            dk_ref[rc, :] = dkc
            dlast = jnp.sum(dkdec * kdec_c, axis=0, keepdims=True) + et * jnp.sum(s_all * ds_all, axis=0, keepdims=True)
            dcs[rc, :] = dqd * qd_c - dki * ki_c - dkdec * kdec_c + jnp.where(last_row, dlast, 0.0)
        dg_ref[...] = _dot3(t_in, dcs[...], TN)

    def rows(r):
        return (nb - 1 - r) if brev else r

    k_spec = pl.BlockSpec((tb, hk), lambda r: (rows(r), 0))
    v_spec = pl.BlockSpec((tb, hv), lambda r: (rows(r), 0))
    in_specs = [pl.BlockSpec((tb, hk), lambda r: (rows(r), qcol)), pl.BlockSpec((tb, hk), lambda r: (rows(r), kcol)),
                pl.BlockSpec((tb, hv), lambda r: (rows(r), vcol)), k_spec,
                pl.BlockSpec((ncb, dv, hk), lambda r: (rows(r), 0, 0)), v_spec]
    args = [proj, proj, proj, g, sprev, do]
    if n_prev:
        in_specs += [k_spec, k_spec, v_spec]
        args += list(prev)
    return pl.pallas_call(
        body, name=name, grid=(nb,), in_specs=in_specs, out_specs=[k_spec, k_spec, v_spec, k_spec],
        out_shape=[_sds((l, hk), F32), _sds((l, hk), F32), _sds((l, hv), BF if n_prev else F32), _sds((l, hk), F32)],
        scratch_shapes=[pltpu.VMEM((dv, hk), F32), pltpu.VMEM((tb, hk), F32)],
        compiler_params=_params(("arbitrary",)),
    )(*args)


def _headgate_fwd(name, o_sum, og_arr, og_col, gn, dv):
    l, w = o_sum.shape
    tm = _tile(l, 1024, SUBLANE)
    nh = w // dv

    def body(o_ref, og_ref, gn_ref, out_ref):
        for h in range(nh):
            cs = slice(h * dv, (h + 1) * dv)
            o = o_ref[:, cs].astype(F32)
            r = lax.rsqrt(jnp.mean(o * o, axis=-1, keepdims=True) + EPS)
            og = og_ref[:, cs].astype(F32)
            out_ref[:, cs] = (o * r * gn_ref[:, cs] * (og * _sigmoid(og))).astype(out_ref.dtype)

    row = pl.BlockSpec((tm, w), lambda i: (i, 0))
    return pl.pallas_call(
        body, name=name, grid=(l // tm,),
        in_specs=[row, pl.BlockSpec((tm, w), lambda i: (i, og_col)), pl.BlockSpec((1, w), lambda i: (0, 0))],
        out_specs=row, out_shape=_sds((l, w), BF),
        compiler_params=_params(("arbitrary",)),
    )(o_sum, og_arr, gn)


def _headgate_bwd(name, o_sum, og_arr, og_col, gn, dout, dcol, dv):
    l, w = o_sum.shape
    tm = _tile(l, 1024, SUBLANE)
    nh = w // dv

    def body(o_ref, og_ref, gn_ref, d_ref, do_ref, dog_ref, dgn_ref):
        i = pl.program_id(0)
        for h in range(nh):
            cs = slice(h * dv, (h + 1) * dv)
            o = o_ref[:, cs].astype(F32)
            r = lax.rsqrt(jnp.mean(o * o, axis=-1, keepdims=True) + EPS)
            oh = o * r
            og = og_ref[:, cs].astype(F32)
            s = _sigmoid(og)
            d = d_ref[:, cs].astype(F32)
            gnv = gn_ref[:, cs]
            d_on = d * (og * s)
            dog_ref[:, cs] = (d * (oh * gnv) * s * (1.0 + og * (1.0 - s))).astype(dog_ref.dtype)
            doh = d_on * gnv
            do_ref[:, cs] = (r * (doh - oh * jnp.mean(doh * oh, axis=-1, keepdims=True))).astype(do_ref.dtype)
            part = jnp.sum(d_on * oh, axis=0, keepdims=True)

            @pl.when(i == 0)
            def _():
                dgn_ref[:, cs] = part

            @pl.when(i > 0)
            def _():
                dgn_ref[:, cs] += part

    row = pl.BlockSpec((tm, w), lambda i: (i, 0))
    vec = pl.BlockSpec((1, w), lambda i: (0, 0))
    return pl.pallas_call(
        body, name=name, grid=(l // tm,),
        in_specs=[row, pl.BlockSpec((tm, w), lambda i: (i, og_col)), vec, pl.BlockSpec((tm, w), lambda i: (i, dcol))],
        out_specs=[row, row, vec], out_shape=[_sds((l, w), BF), _sds((l, w), BF), _sds((1, w), F32)],
        compiler_params=_params(("arbitrary",)),
    )(o_sum, og_arr, gn, dout)


def _rot_tables(l, dk):
    half = dk // 2
    pos = jnp.arange(l, dtype=F32)
    inv = jnp.exp(-math.log(ROPE_BASE) * jnp.arange(half, dtype=F32) / half)
    ang = pos[:, None] * inv[None, :]
    cos, sin = jnp.cos(ang), jnp.sin(ang)
    return jnp.concatenate([cos, cos], axis=-1), jnp.concatenate([-sin, sin], axis=-1)


def _rot_apply(name, src_q, qcol, src_k, kcol, cos_t, sin_t, heads, dk, kscale, out_dtype, transpose):
    l = src_q.shape[0]
    w = heads * dk
    tm = _tile(l, 1024, SUBLANE)

    def rot(t, cos_v, sin_v):
        if transpose:
            return t * cos_v + pltpu.roll(t * sin_v, dk // 2, 1)
        return t * cos_v + pltpu.roll(t, dk // 2, 1) * sin_v

    def body(q_ref, k_ref, c_ref, s_ref, qo_ref, ko_ref):
        cos_v, sin_v = c_ref[...], s_ref[...]
        for h in range(heads):
            cs = slice(h * dk, (h + 1) * dk)
            qo_ref[:, cs] = rot(q_ref[:, cs].astype(F32), cos_v, sin_v).astype(out_dtype)
            ko_ref[:, cs] = (rot(k_ref[:, cs].astype(F32), cos_v, sin_v) * kscale).astype(out_dtype)

    tab = pl.BlockSpec((tm, dk), lambda i: (i, 0))
    row = pl.BlockSpec((tm, w), lambda i: (i, 0))
    return pl.pallas_call(
        body, name=name, grid=(l // tm,),
        in_specs=[pl.BlockSpec((tm, w), lambda i: (i, qcol)), pl.BlockSpec((tm, w), lambda i: (i, kcol)), tab, tab],
        out_specs=[row, row], out_shape=[_sds((l, w), out_dtype), _sds((l, w), out_dtype)],
        compiler_params=_params(("arbitrary",)),
    )(src_q, src_k, cos_t, sin_t)


def _peer_copies(src_ref, out_ref, send_sems, recv_sems, gather):
    x, y, c = lax.axis_index("x"), lax.axis_index("y"), lax.axis_index("c")
    me = 4 * x + 2 * y + c
    copies = []
    for kk in range(1, N_DEV):
        px = (1 - x) if kk & 4 else x
        py = (1 - y) if kk & 2 else y
        pc = (1 - c) if kk & 1 else c
        peer = 4 * px + 2 * py + pc
        copies.append(pltpu.make_async_remote_copy(
            src_ref=src_ref if gather else src_ref.at[peer], dst_ref=out_ref.at[me],
            send_sem=send_sems.at[kk - 1], recv_sem=recv_sems.at[kk - 1],
            device_id=(px, py, pc), device_id_type=pl.DeviceIdType.MESH))
    return copies


_HBM = pl.BlockSpec(memory_space=pltpu.HBM)
_SEM = pl.BlockSpec(memory_space=pltpu.SEMAPHORE)
_EFFECT = pltpu.SideEffectType.DATAFLOW_SIDE_EFFECTING


def _exchange_start(name, srcs, gather):
    n = len(srcs)
    lands = [lax.empty((N_DEV,) + tuple(s.shape if gather else s.shape[1:]), s.dtype) for s in srcs]

    def body(*refs):
        src_refs, land_refs = refs[:n], refs[n:2 * n]
        send, recv = refs[2 * n:3 * n], refs[3 * n:4 * n]
        token = refs[-1]
        for k in range(n):
            for cp in _peer_copies(src_refs[k], land_refs[k], send[k], recv[k], gather):
                cp.start()
        token[...] = jnp.zeros_like(token)

    sem = pltpu.SemaphoreType.DMA((N_DEV - 1,))
    outs = pl.pallas_call(
        body, name=name,
        out_shape=tuple([sem] * (2 * n) + [pltpu.HBM(s.shape, s.dtype) for s in srcs] + [pltpu.HBM(a.shape, a.dtype) for a in lands]
                        + [_sds((SUBLANE, LANE), F32)]),
        in_specs=tuple([_HBM] * (2 * n)), out_specs=tuple([_SEM] * (2 * n) + [_HBM] * (2 * n) + [pl.BlockSpec(memory_space=pltpu.VMEM)]),
        input_output_aliases={k: 2 * n + k for k in range(2 * n)},
        compiler_params=pltpu.CompilerParams(has_side_effects=_EFFECT),
    )(*[pltpu.with_memory_space_constraint(a, pltpu.HBM) for a in list(srcs) + lands])
    return [(outs[k], outs[n + k], outs[2 * n + k], outs[3 * n + k], outs[-1]) for k in range(n)]


def _exchange_wait(name, started, gather, after):
    send_sems, recv_sems, src_thru, land_thru, _ = started

    def body(src_ref, land_ref, send_sems, recv_sems, after_ref, src_out, land_out):
        copies = _peer_copies(src_ref, land_ref, send_sems, recv_sems, gather)
        for cp in copies:
            cp.wait_send()
        for cp in copies:
            cp.wait_recv()

    return pl.pallas_call(
        body, name=name,
        out_shape=(pltpu.HBM(src_thru.shape, src_thru.dtype), pltpu.HBM(land_thru.shape, land_thru.dtype)),
        in_specs=(_HBM, _HBM, _SEM, _SEM, pl.BlockSpec(memory_space=pl.ANY)), out_specs=(_HBM, _HBM),
        input_output_aliases={0: 0, 1: 1},
        compiler_params=pltpu.CompilerParams(has_side_effects=_EFFECT),
    )(src_thru, land_thru, send_sems, recv_sems, after)


def _adam_math(w, gsum, m, v):
    m2 = ADAM_B1 * m + (1.0 - ADAM_B1) * gsum
    v2 = ADAM_B2 * v + (1.0 - ADAM_B2) * (gsum * gsum)
    m_hat = m2 / (1.0 - ADAM_B1 ** ADAM_STEP)
    v_hat = v2 / (1.0 - ADAM_B2 ** ADAM_STEP)
    delta = -ADAM_LR * (m_hat / (jnp.sqrt(v_hat) + ADAM_EPS) + ADAM_WD * w)
    return delta, m2, v2


def _reduce_adam(name, parts, w, m, v):
    nl, r, c = w.shape
    tr = _tile(r, 256, 16)
    nr = r // tr

    def body(*refs):
        p_refs = refs[:nl]
        w_ref, m_ref, v_ref, g_ref, d_ref, m2_ref, v2_ref = refs[nl:]
        for li in range(nl):
            @pl.when(pl.program_id(0) == li)
            def _(p_ref=p_refs[li]):
                gsum = p_ref[0].astype(F32)
                for s in range(1, N_DEV):
                    gsum = gsum + p_ref[s].astype(F32)
                g_ref[...] = gsum
                delta, m2, v2 = _adam_math(w_ref[...], gsum, m_ref[...], v_ref[...])
                d_ref[...] = delta
                m2_ref[...] = m2
                v2_ref[...] = v2

    def part_spec(li):
        return pl.BlockSpec((N_DEV, tr, c), lambda lay, i: (0, jnp.where(lay == li, i, jnp.where(lay < li, 0, nr - 1)), 0))

    row = pl.BlockSpec((None, tr, c), lambda lay, i: (lay, i, 0))
    return pl.pallas_call(
        body, name=name, grid=(nl, nr),
        in_specs=[part_spec(li) for li in range(nl)] + [row, row, row],
        out_specs=[row, row, row, row], out_shape=[_sds((nl, r, c), F32)] * 4,
        compiler_params=_params(("arbitrary", "arbitrary")),
    )(*parts, w, m, v)


def _reduce8(name, parts):
    _, r, c = parts.shape

    def body(p_ref, g_ref):
        gsum = p_ref[0]
        for s in range(1, N_DEV):
            gsum = gsum + p_ref[s]
        g_ref[...] = gsum

    return pl.pallas_call(
        body, name=name, grid=(1,),
        in_specs=[pl.BlockSpec((N_DEV, r, c), lambda i: (0, 0, 0))],
        out_specs=pl.BlockSpec((r, c), lambda i: (0, 0)), out_shape=_sds((r, c), F32),
        compiler_params=_params(("arbitrary",)),
    )(parts)


def _adam_packed(name, w, g, m, v):
    r, c = w.shape

    def body(w_ref, g_ref, m_ref, v_ref, d_ref, m2_ref, v2_ref):
        delta, m2, v2 = _adam_math(w_ref[...], g_ref[...], m_ref[...], v_ref[...])
        d_ref[...] = delta
        m2_ref[...] = m2
        v2_ref[...] = v2

    spec = pl.BlockSpec((r, c), lambda i: (0, 0))
    return pl.pallas_call(
        body, name=name, grid=(1,), in_specs=[spec] * 4, out_specs=[spec] * 3, out_shape=[_sds((r, c), F32)] * 3,
        compiler_params=_params(("arbitrary",)),
    )(w, g, m, v)


def _pack(arrs):
    flat = jnp.concatenate([a.reshape(-1).astype(F32) for a in arrs])
    n = flat.shape[0]
    pad = (-n) % (SUBLANE * LANE)
    return jnp.pad(flat, (0, pad)).reshape(-1, LANE)


def _unpack(packed, like):
    flat = packed.reshape(-1)
    out, off = [], 0
    for a in like:
        n = math.prod(a.shape)
        out.append(flat[off:off + n].reshape(a.shape))
        off += n
    return out


def _row_blocks(full):
    return full.reshape(N_DEV, full.shape[0] // N_DEV, full.shape[1])


def _col_blocks(full):
    r, c = full.shape
    return full.reshape(r, N_DEV, c // N_DEV).transpose(1, 0, 2)


def kernel(x, ffn1_norm, ffn1_w1, ffn1_w2, mix_norm, ffn2_norm, ffn2_w1, ffn2_w2, ab_w_in, s5_lambda_re, s5_lambda_im, s5_b_re, s5_b_im, s5_c_re, s5_c_im, s5_log_dt, s5_d, s5_w_glu, gla_w_gk, gla_b_gk, gla_norm, ab_w_out, ret_w_in, ret_norm, ret_w_out, final_norm, loss_target, m_ffn1_norm, m_ffn1_w1, m_ffn1_w2, m_mix_norm, m_ffn2_norm, m_ffn2_w1, m_ffn2_w2, m_ab_w_in, m_s5_lambda_re, m_s5_lambda_im, m_s5_b_re, m_s5_b_im, m_s5_c_re, m_s5_c_im, m_s5_log_dt, m_s5_d, m_s5_w_glu, m_gla_w_gk, m_gla_b_gk, m_gla_norm, m_ab_w_out, m_ret_w_in, m_ret_norm, m_ret_w_out, m_final_norm, v_ffn1_norm, v_ffn1_w1, v_ffn1_w2, v_mix_norm, v_ffn2_norm, v_ffn2_w1, v_ffn2_w2, v_ab_w_in, v_s5_lambda_re, v_s5_lambda_im, v_s5_b_re, v_s5_b_im, v_s5_c_re, v_s5_c_im, v_s5_log_dt, v_s5_d, v_s5_w_glu, v_gla_w_gk, v_gla_b_gk, v_gla_norm, v_ab_w_out, v_ret_w_in, v_ret_norm, v_ret_w_out, v_final_norm):
    names = ['ffn1_norm', 'ffn1_w1', 'ffn1_w2', 'mix_norm', 'ffn2_norm', 'ffn2_w1', 'ffn2_w2', 'ab_w_in', 's5_lambda_re', 's5_lambda_im', 's5_b_re', 's5_b_im', 's5_c_re', 's5_c_im', 's5_log_dt', 's5_d', 's5_w_glu', 'gla_w_gk', 'gla_b_gk', 'gla_norm', 'ab_w_out', 'ret_w_in', 'ret_norm', 'ret_w_out', 'final_norm']
    loc = locals()
    W = {n: loc[n] for n in names}
    M = {n: loc["m_" + n] for n in names}
    V = {n: loc["v_" + n] for n in names}

    me = 4 * lax.axis_index("x") + 2 * lax.axis_index("y") + lax.axis_index("c")
    xs = x[0]
    tgt = loss_target[0]
    l, d = xs.shape
    depth = ffn1_norm.shape[0]

    pending, to_start = {}, []

    def start_gather(tag, shard):
        to_start.append((tag, shard))

    def finish_gather(tag, after):
        started, shard = pending.pop(tag)
        _, got = _exchange_wait("agw_" + tag, started, True, after)
        return lax.dynamic_update_index_in_dim(got, shard, me, 0)

    def finish_cols(tag, after):
        g = finish_gather(tag, after)
        return g.transpose(1, 0, 2).reshape(g.shape[1], -1)

    def finish_rows(tag, after):
        g = finish_gather(tag, after)
        return g.reshape(-1, g.shape[2])

    small_sharded = [gla_w_gk, gla_b_gk, ret_norm]
    for i in range(depth):
        j = i // 2
        start_gather(f"ffn1_w1_{i}", ffn1_w1[i].astype(BF))
        start_gather(f"ffn1_w2_{i}", ffn1_w2[i].astype(BF))
        if i % 2 == 0:
            start_gather(f"ab_w_in_{j}", ab_w_in[j].astype(BF))
            if i == 0:
                start_gather("small", _pack(small_sharded))
            start_gather(f"s5_w_glu_{j}", s5_w_glu[j].astype(BF))
            start_gather(f"ab_w_out_{j}", ab_w_out[j].astype(BF))
        else:
            start_gather(f"ret_w_in_{j}", ret_w_in[j].astype(BF))
            start_gather(f"ret_w_out_{j}", ret_w_out[j].astype(BF))
        start_gather(f"ffn2_w1_{i}", ffn2_w1[i].astype(BF))
        start_gather(f"ffn2_w2_{i}", ffn2_w2[i].astype(BF))
    for (tag, shard), started in zip(to_start, _exchange_start("ags_weights", [s_ for _, s_ in to_start], True)):
        pending[tag] = (started, shard)
    started_all = started[4][0, 0]
    full = {}

    s5w = s5_d.shape[1]
    g_s5, n_s5 = s5_lambda_re.shape[2], s5_lambda_re.shape[3]
    hs = min(SUBLANE * LANE, g_s5 * n_s5)
    gla_hk = gla_w_gk.shape[-1] * N_DEV
    gla_dk = gla_hk // GLA_HEADS
    gla_hv = gla_norm.shape[1]
    gla_dv = gla_hv // GLA_HEADS
    ret_hv = ret_norm.shape[1] * N_DEV
    ret_dv = ret_hv // RET_HEADS
    ret_hk = (ret_w_in.shape[2] * N_DEV - 2 * ret_hv) // 2
    ret_dk = ret_hk // RET_HEADS
    assert s5w == gla_hv and 2 * gla_hk == s5w, "column blocks of the mixer projection assume these widths"
    assert ret_hv == 2 * ret_hk
    main_w = s5w + 2 * gla_hk + 2 * gla_hv
    gla_tb = _tile(l, 256, GLA_CHUNK)
    ret_chunk = min(RET_CHUNK, l)

    cos_t, sin_t = _rot_tables(l, ret_dk)
    lg_f = jnp.log1p(-jnp.exp2(-5.0 - jnp.arange(RET_HEADS, dtype=F32)))
    lgtab_f = jnp.broadcast_to(lg_f[:, None, None], (RET_HEADS, 1, LANE))
    lgtab_b = jnp.broadcast_to(lg_f[::-1][:, None, None], (RET_HEADS, 1, LANE))
    s5_pre = {}
    for j in range((depth + 1) // 2):
        s5_args = (s5_lambda_re[j], s5_lambda_im[j], s5_b_re[j], s5_b_im[j], s5_c_re[j], s5_c_im[j], s5_log_dt[j])
        (a_tab, bd, cd), s5_vjp = jax.vjp(lambda *a: _s5_chunk_tables(*a, hs), *s5_args)
        s5_pre[j] = (a_tab, bd.astype(BF), cd.astype(BF), s5_vjp)
    tables_done = jnp.stack([cos_t[0, 0], sin_t[0, 0], lgtab_f[0, 0, 0], lgtab_b[0, 0, 0]]
                            + [t[0].reshape(-1)[0] + t[1].reshape(-1)[0].astype(F32) + t[2].reshape(-1)[0].astype(F32) for t in s5_pre.values()])

    saved = []
    cur = xs
    hn = _rms_fwd("l0_ffn1_norm", cur, ffn1_norm[0:1] + started_all)
    for i in range(depth):
        j = i // 2
        s = {}
        s['x0'] = cur

        def first_w1(after):
            if i == 0:
                after = jnp.concatenate([after[0, 0:1].astype(F32), tables_done])
            return finish_cols(f"ffn1_w1_{i}", after)

        cur, h, s['ffn1'], s['f1w1'], s['f1w2'] = _ffn_fwd(f"l{i}_ffn1", cur, hn, first_w1, lambda after: finish_rows(f"ffn1_w2_{i}", after),
                                                            mix_norm[i:i + 1])
        s['x1'] = cur
        s['h'] = h
        if i % 2 == 0:
            w_in = finish_cols(f"ab_w_in_{j}", cur)
            if i == 0:
                got = finish_gather("small", cur)
                flat, off, joined = got.reshape(N_DEV, -1), 0, []
                for a in small_sharded:
                    n = math.prod(a.shape)
                    blk = jnp.moveaxis(flat[:, off:off + n].reshape((N_DEV,) + a.shape), 0, -2)
                    joined.append(blk.reshape(a.shape[:-1] + (N_DEV * a.shape[-1],)))
                    off += n
                full['gla_w_gk'], full['gla_b_gk'], ret_norm_full = joined[0].astype(BF), joined[1], joined[2]
            s['w_glu'], s['w_out'] = finish_rows(f"s5_w_glu_{j}", cur), finish_rows(f"ab_w_out_{j}", cur)
            w_main, w_glo = w_in[:, :main_w], w_in[:, main_w:]
            proj = _mm_plain(f"l{i}_proj", h, w_main, NN, BF, tm=1024, tn=1024, tk=d, b_outer=True)
            glo = _mm_plain(f"l{i}_glo", h, w_glo, NN, F32, tm=1024, tn=2 * GLA_RANK, tk=d)
            a_tab, bd16, cd16, s5_vjp = s5_pre[j]
            tm = _tile(l, 512, SUBLANE)
            x_f, y_f = _s5_fwd(f"l{i}_s5_fwd_f", proj, bd16[0], cd16[0], a_tab[0], False)
            x_b, y_b = _s5_fwd(f"l{i}_s5_fwd_b", proj, bd16[1], cd16[1], a_tab[1], True)
            d_row = s5_d[j:j + 1]
            y, s5_out = _glu_fwd(f"l{i}_s5_glu", y_f, y_b, proj, d_row, s['w_glu'])
            zeros_r = jnp.zeros((GLA_RANK, gla_hk), BF)
            w_gk = full['gla_w_gk'][j]
            wgk_f = jnp.concatenate([w_gk[0], zeros_r], axis=0)
            wgk_b = jnp.concatenate([zeros_r, w_gk[1]], axis=0)
            b_gk = full['gla_b_gk'][j]
            g_f, g_b = _gate_fwd(f"l{i}_gla_gate", glo, wgk_f, wgk_b, b_gk[0:1], b_gk[1:2])
            qcol, kcol, vcol, ogcol = s5w // gla_hk, s5w // gla_hk + 1, (s5w + 2 * gla_hk) // gla_hv, (s5w + 2 * gla_hk) // gla_hv + 1
            lin_kw = dict(heads=GLA_HEADS, dk=gla_dk, dv=gla_dv, chunk=GLA_CHUNK, tb=gla_tb, qcol=qcol, kcol=kcol, vcol=vcol,
                          qscale=gla_dk ** -0.5)
            o_f, sp_f = _gla_fwd(f"l{i}_gla_fwd_f", proj, g_f, reverse=False, **lin_kw)
            o_b, sp_b = _gla_fwd(f"l{i}_gla_fwd_b", proj, g_b, o_f, reverse=True, **lin_kw)
            gla_out = _headgate_fwd(f"l{i}_gla_out", o_b, proj, ogcol, gla_norm[j:j + 1], gla_dv)
            w_out = s['w_out']

            row = pl.BlockSpec((tm, d), lambda ii, jj, kk: (ii, 0))
            vec = pl.BlockSpec((1, d), lambda ii, jj, kk: (0, 0))
            cur, hn = _mm(f"l{i}_mix_out",
                          [(s5_out, pl.BlockSpec((tm, s5w), lambda ii, jj, kk: (ii, 0)), w_out, pl.BlockSpec((s5w, d), lambda ii, jj, kk: (0, 0))),
                           (gla_out, pl.BlockSpec((tm, gla_hv), lambda ii, jj, kk: (ii, 0)), w_out, pl.BlockSpec((gla_hv, d), lambda ii, jj, kk: (1, 0)))],
                          NN, (l // tm, 1, 1), [_sds((l, d), F32), _sds((l, d), BF)], [row, row], (tm, d), _residual_epi(1.0, True),
                          [cur, ffn2_norm[i:i + 1]], [row, vec])
            s.update(proj=proj, glo=glo, s5_vjp=s5_vjp, a_tab=a_tab, bd16=bd16, cd16=cd16, x_f=x_f, x_b=x_b, y=y, s5_out=s5_out,
                     wgk_f=wgk_f, wgk_b=wgk_b, b_gk=b_gk, g_f=g_f, g_b=g_b, o_f=o_f, o_b=o_b, sp_f=sp_f, sp_b=sp_b, gla_out=gla_out,
                     w_main=w_main, w_glo=w_glo, lin_kw=lin_kw, ogcol=ogcol)
        else:
            w_in = finish_cols(f"ret_w_in_{j}", cur)
            s['w_in'], s['w_out'] = w_in, finish_rows(f"ret_w_out_{j}", cur)
            proj = _mm_plain(f"l{i}_proj", h, w_in, NN, BF, tm=1024, tn=1024, tk=d, b_outer=True)
            qr, kr = _rot_apply(f"l{i}_rot", proj, 0, proj, 1, cos_t, sin_t, RET_HEADS, ret_dk, ret_dk ** -0.5, BF, False)
            ret_hb = 4
            lin_kw = dict(heads=RET_HEADS, hb=ret_hb, dk=ret_dk, dv=ret_dv, chunk=ret_chunk, tb=_tile(l, 4 * ret_chunk, ret_chunk), qcol=0, kcol=0,
                          vcol=(2 * ret_hk) // (ret_hb * ret_dv))
            o_f, sp_f = _lin_fwd(f"l{i}_ret_fwd_f", qr, kr, proj, lgtab_f, reverse=False, **lin_kw)
            o_b, sp_b = _lin_fwd(f"l{i}_ret_fwd_b", qr, kr, proj, lgtab_b, o_f, reverse=True, **lin_kw)
            ogcol = (2 * ret_hk + ret_hv) // ret_hv
            r_out = _headgate_fwd(f"l{i}_ret_out", o_b, proj, ogcol, ret_norm_full, ret_dv)

            tm = _tile(l, 512, SUBLANE)
            cur, hn = _mm_plain(f"l{i}_mix_out", r_out, s['w_out'], NN, F32, tm=512, tn=d, tk=ret_hv, epi=_residual_epi(1.0, True),
                                eins=[cur, ffn2_norm[i:i + 1]], especs=[None, pl.BlockSpec((1, d), lambda ii, jj, kk: (0, 0))],
                                extra_outs=[_sds((l, d), BF)], extra_specs=[pl.BlockSpec((tm, d), lambda ii, jj, kk: (ii, 0))])
            s.update(proj=proj, qr=qr, kr=kr, o_f=o_f, o_b=o_b, sp_f=sp_f, sp_b=sp_b, r_out=r_out, lin_kw=lin_kw, ogcol=ogcol)
        s['x2'] = cur
        cur, hn, s['ffn2'], s['f2w1'], s['f2w2'] = _ffn_fwd(f"l{i}_ffn2", cur, hn, lambda after: finish_cols(f"ffn2_w1_{i}", after),
                                                             lambda after: finish_rows(f"ffn2_w2_{i}", after),
                                                             ffn1_norm[i + 1:i + 2] if i + 1 < depth else None)
        saved.append(s)

    dx, d_final_norm, loss_row = _loss_head("loss_head", cur, final_norm.reshape(1, -1), tgt)
    loss = lax.psum(loss_row[0, 0], ("x", "y", "c"))

    G = {}
    big = {}
    G['final_norm'] = d_final_norm.reshape(-1)
    per_layer = {n: [None] * depth for n in ['ffn1_norm', 'mix_norm', 'ffn2_norm']}
    small_late = ['ffn1_norm', 'mix_norm']
    small_early = ['ffn2_norm', 's5_lambda_re', 's5_lambda_im', 's5_b_re', 's5_b_im', 's5_c_re', 's5_c_im',
                   's5_log_dt', 's5_d', 'gla_w_gk', 'gla_b_gk', 'gla_norm', 'ret_norm', 'final_norm']
    a2a, tok = {}, [jnp.zeros((), F32)]

    def start_a2a(*tagged):
        for (tag, _), started in zip(tagged, _exchange_start("a2as_" + tagged[0][0], [b for _, b in tagged], False)):
            a2a[tag] = started
        tok[0] = tok[0] + started[4][0, 0]

    def dep(vec):
        return vec + tok[0]

    def proj_backward(tag, pieces, s, dres, a2a_tag):
        tm = _tile(l, 512, SUBLANE)
        row = pl.BlockSpec((tm, d), lambda ii, jj, kk: (ii, 0))
        vec = pl.BlockSpec((1, d), lambda ii, jj, kk: (0, 0))
        dws = [_mm_plain(f"{tag}_dwin_{k}", s['h'], piece, TN, BF, tm=d, tn=2048, tk=2048) for k, (piece, _, _) in enumerate(pieces)]
        start_a2a((a2a_tag, _col_blocks(jnp.concatenate(dws, axis=1))))
        pairs = []
        for piece, w, col in pieces:
            wd = piece.shape[1]
            pairs.append((piece, pl.BlockSpec((tm, wd), lambda ii, jj, kk: (ii, 0)),
                          w, pl.BlockSpec((d, wd), lambda ii, jj, kk, col=col: (0, col), pipeline_mode=pl.Buffered(1))))
        return _mm(f"{tag}_dh", pairs, NT, (l // tm, 1, 1), [_sds((l, d), F32), _sds((1, d), F32)], [row, vec], (tm, d),
                   _rms_bwd_epi(0), [s['x1'], dep(mix_norm[i:i + 1]), dres], [row, vec, row])

    for i in reversed(range(depth)):
        j = i // 2
        s = saved[i]
        def ffn_grads(which):
            def on_grads(dw1, dw2, gnorm):
                start_a2a((f"{which}_w1_{i}", _col_blocks(dw1)), (f"{which}_w2_{i}", _row_blocks(dw2)))
                return dep(gnorm)
            return on_grads

        dx, dg = _ffn_bwd(f"l{i}_ffn2b", dx, s['x2'], ffn2_norm[i:i + 1], s['f2w1'], s['f2w2'], s['ffn2'], ffn_grads("ffn2"))
        per_layer['ffn2_norm'][i] = dg[0]
        tm = _tile(l, 512, SUBLANE)
        row = pl.BlockSpec((tm, d), lambda ii, jj, kk: (ii, 0))
        vec = pl.BlockSpec((1, d), lambda ii, jj, kk: (0, 0))
        if i % 2 == 0:
            proj, lin_kw = s['proj'], s['lin_kw']
            w_out = s['w_out']
            d_cat = _mm_plain(f"l{i}_dcat", dx, w_out, NT, BF, tm=512, tn=1024, tk=d)
            dwo_a = _mm_plain(f"l{i}_dwout_a", s['s5_out'], dx, TN, BF, tm=s5w, tn=d, tk=2048)
            dwo_b = _mm_plain(f"l{i}_dwout_b", s['gla_out'], dx, TN, BF, tm=gla_hv, tn=d, tk=2048)
            start_a2a((f"ab_w_out_{j}", _row_blocks(jnp.concatenate([dwo_a, dwo_b], axis=0))))
            do, dog, dgn = _headgate_bwd(f"l{i}_gla_outb", s['o_b'], proj, s['ogcol'], dep(gla_norm[j:j + 1]), d_cat, 1, gla_dv)
            G['gla_norm'] = dgn
            dq, dk_, dv_, dgf = _gla_bwd(f"l{i}_gla_bwd_f", proj, s['g_f'], s['sp_f'], do, None, reverse=False, **lin_kw)
            dq, dk_, dv_, dgb = _gla_bwd(f"l{i}_gla_bwd_b", proj, s['g_b'], s['sp_b'], do, (dq, dk_, dv_), reverse=True, **lin_kw)
            dglo, dwf, dwb, dbf, dbb = _gate_bwd(f"l{i}_gla_gateb", s['glo'], s['wgk_f'], s['wgk_b'], s['b_gk'][0:1], s['b_gk'][1:2], dgf, dgb)
            G['gla_w_gk'] = jnp.stack([dwf[:GLA_RANK], dwb[GLA_RANK:]], axis=0)[None]
            G['gla_b_gk'] = jnp.concatenate([dbf, dbb], axis=0)[None]
            dy, dwglu = _glu_bwd(f"l{i}_s5_glub", s['y'], s['w_glu'], d_cat, 0)
            start_a2a((f"s5_w_glu_{j}", _row_blocks(dwglu.astype(BF))))
            cd16, bd16, a_tab = s['cd16'], s['bd16'], s['a_tab']
            nt2 = a_tab.shape[2]
            a_conj = a_tab * jnp.where(jnp.arange(nt2) < nt2 // 2, 1.0, -1.0)[None, None, :, None]
            du_f, dbd_f, dcd_f, da_f = _s5_bwd(f"l{i}_s5_bwd_f", proj, dy, s['x_f'], bd16[0], cd16[0], a_conj[0], True)
            du_b, dbd_b, dcd_b, da_b = _s5_bwd(f"l{i}_s5_bwd_b", proj, dy, s['x_b'], bd16[1], cd16[1], a_conj[1], False)
            du, dd = _s5_du(f"l{i}_s5_du", du_f, du_b, dy, proj, s5_d[j:j + 1])
            G['s5_d'] = dd
            cot = (jnp.stack([da_f, da_b]), jnp.stack([dbd_f, dbd_b]), jnp.stack([dcd_f, dcd_b]))
            g_lre, g_lim, g_bre, g_bim, g_cre, g_cim, g_ldt = s['s5_vjp'](cot)
            G['s5_lambda_re'], G['s5_lambda_im'], G['s5_b_re'], G['s5_b_im'] = g_lre[None], g_lim[None], g_bre[None], g_bim[None]
            G['s5_c_re'], G['s5_c_im'], G['s5_log_dt'] = g_cre[None], g_cim[None], g_ldt[None]
            if i == 0:
                G['ffn2_norm'] = jnp.stack(per_layer['ffn2_norm'], axis=0)
                early_packed = _pack([G[n] for n in small_early])
                early_started = _exchange_start("ags_small_grads_early", [early_packed], True)[0]
                tok[0] = tok[0] + early_started[4][0, 0]
            w_main, w_glo = s['w_main'], s['w_glo']
            pieces = [(du, w_main, 0), (dq, w_main, s5w // gla_hk), (dk_, w_main, s5w // gla_hk + 1),
                      (dv_, w_main, (s5w + 2 * gla_hk) // gla_hv), (dog, w_main, (s5w + 2 * gla_hk) // gla_hv + 1), (dglo, w_glo, 0)]
            dx, dg = proj_backward(f"l{i}", pieces, s, dx, f"ab_w_in_{j}")
        else:
            proj, lin_kw = s['proj'], s['lin_kw']
            w_out = s['w_out']
            d_ro = _mm_plain(f"l{i}_dro", dx, w_out, NT, BF, tm=1024, tn=1024, tk=d, b_outer=True)
            dwo = _mm_plain(f"l{i}_dwout", s['r_out'], dx, TN, BF, tm=1024, tn=d, tk=2048)
            start_a2a((f"ret_w_out_{j}", _row_blocks(dwo)))
            do, dog, dgn = _headgate_bwd(f"l{i}_ret_outb", s['o_b'], proj, s['ogcol'], dep(ret_norm_full), d_ro, 0, ret_dv)
            G['ret_norm'] = dgn
            r1 = _lin_bwd(f"l{i}_ret_bwd_f", s['qr'], s['kr'], proj, lgtab_f, s['sp_f'], do, None, reverse=False, **lin_kw)
            r2 = _lin_bwd(f"l{i}_ret_bwd_b", s['qr'], s['kr'], proj, lgtab_b, s['sp_b'], do, r1, reverse=True, **lin_kw)
            dqr, dkr, dv_ = r2
            dq, dk_ = _rot_apply(f"l{i}_rotb", dqr, 0, dkr, 0, cos_t, sin_t, RET_HEADS, ret_dk, ret_dk ** -0.5, BF, True)
            w_in = s['w_in']
            pieces = [(dq, w_in, 0), (dk_, w_in, 1), (dv_, w_in, (2 * ret_hk) // ret_hv), (dog, w_in, (2 * ret_hk) // ret_hv + 1)]
            dx, dg = proj_backward(f"l{i}", pieces, s, dx, f"ret_w_in_{j}")
        per_layer['mix_norm'][i] = dg[0]
        dx, dg = _ffn_bwd(f"l{i}_ffn1b", dx, s['x0'], ffn1_norm[i:i + 1], s['f1w1'], s['f1w2'], s['ffn1'], ffn_grads("ffn1"))
        per_layer['ffn1_norm'][i] = dg[0]
    for n in small_late:
        G[n] = jnp.stack(per_layer[n], axis=0)
    grad_x = dx[None]

    out_g, out_d, out_m, out_v = {}, {}, {}, {}
    small = small_early + small_late
    packed = _pack([G[n] for n in small_late])
    small_started = _exchange_start("ags_small_grads_late", [packed], True)[0]

    def big_update(n, layers):
        parts = []
        for i in layers:
            blocks, got = _exchange_wait(f"a2aw_{n}_{i}", a2a.pop(f"{n}_{i}"), False, small_started[4])
            parts.append(lax.dynamic_update_index_in_dim(got, lax.dynamic_index_in_dim(blocks, me, 0, keepdims=False), me, 0))
        out_g[n], out_d[n], out_m[n], out_v[n] = _reduce_adam("upd_" + n, parts, W[n], M[n], V[n])

    for n in ['ffn2_w1', 'ffn2_w2', 'ffn1_w1', 'ffn1_w2']:
        big_update(n, range(depth))
    for n in ['ab_w_in', 's5_w_glu', 'ab_w_out', 'ret_w_in', 'ret_w_out']:
        big_update(n, [0])
    assert not a2a and not pending

    g_full = {}
    for tag, started, mine, group in (("early", early_started, early_packed, small_early), ("late", small_started, packed, small_late)):
        _, gathered = _exchange_wait("agw_small_grads_" + tag, started, True, out_v['ret_w_out'])
        gathered = lax.dynamic_update_index_in_dim(gathered, mine, me, 0)
        summed = _reduce8("sum_small_grads_" + tag, gathered)
        g_full.update(zip(group, _unpack(summed, [G[n] for n in group])))
    g_small = {}
    for n in small:
        gf = g_full[n]
        if n in ('gla_w_gk', 'gla_b_gk', 'ret_norm'):
            width = W[n].shape[-1]
            gf = lax.dynamic_slice_in_dim(gf, me * width, width, axis=gf.ndim - 1)
        g_small[n] = gf.reshape(W[n].shape)
    pw, pg, pm, pv = (_pack([src[n] for n in small]) for src in (W, g_small, M, V))
    pd, pm2, pv2 = _adam_packed("upd_small", pw, pg, pm, pv)
    like = [W[n] for n in small]
    for n, dd_, mm_, vv_ in zip(small, _unpack(pd, like), _unpack(pm2, like), _unpack(pv2, like)):
        out_g[n], out_d[n], out_m[n], out_v[n] = g_small[n], dd_, mm_, vv_

    return (loss, grad_x, *[out_g[n] for n in names], *[out_d[n] for n in names], *[out_m[n] for n in names], *[out_v[n] for n in names])
```

```python
import math

import jax
import jax.numpy as jnp
from jax import lax
from jax.experimental import pallas as pl
from jax.experimental.pallas import tpu as pltpu

F32 = jnp.float32
BF = jnp.bfloat16
N_DEV = 8
EPS = 1e-6
S5_GROUP = 16
GLA_HEADS = 4
GLA_RANK = 16
GLA_GATE_NORM = 16.0
RET_HEADS = 8
ROPE_BASE = 10000.0
GLA_CHUNK = 64
RET_CHUNK = 256
ADAM_LR, ADAM_B1, ADAM_B2, ADAM_EPS, ADAM_WD, ADAM_STEP = 0.001, 0.9, 0.999, 1e-08, 0.01, 10
VMEM_LIMIT_BYTES = 56 * 1024 * 1024
LANE = 128
SUBLANE = 8

NN = (((1,), (0,)), ((), ()))
NT = (((1,), (1,)), ((), ()))
TN = (((0,), (0,)), ((), ()))


def _tile(n, pref, align):
    if n <= pref:
        return n
    t = (pref // align) * align
    while t >= align:
        if n % t == 0:
            return t
        t -= align
    return n


def _params(sem):
    return pltpu.CompilerParams(dimension_semantics=sem, vmem_limit_bytes=VMEM_LIMIT_BYTES)


def _dot(a, b, dims=NN):
    return lax.dot_general(a.astype(BF), b.astype(BF), dims, preferred_element_type=F32)


def _dot3(m01, g, dims=NN):
    g1 = g.astype(BF)
    r1 = g - g1.astype(F32)
    g2 = r1.astype(BF)
    g3 = (r1 - g2.astype(F32)).astype(BF)
    m = m01.astype(BF)
    return (lax.dot_general(m, g1, dims, preferred_element_type=F32)
            + lax.dot_general(m, g2, dims, preferred_element_type=F32)
            + lax.dot_general(m, g3, dims, preferred_element_type=F32))


def _sigmoid(x):
    return 0.5 + 0.5 * jnp.tanh(0.5 * x)


def _mm(name, pairs, dims, grid, outs, out_specs, acc_shape, epi=None, eins=(), especs=()):
    n_p, n_e, n_o = len(pairs), len(eins), len(outs)
    nk = grid[2]

    def body(*refs):
        a_refs = refs[0:2 * n_p:2]
        b_refs = refs[1:2 * n_p:2]
        e_refs = refs[2 * n_p:2 * n_p + n_e]
        o_refs = refs[2 * n_p + n_e:2 * n_p + n_e + n_o]
        acc = refs[-1]
        ids = (pl.program_id(0), pl.program_id(1), pl.program_id(2))

        part = _dot(a_refs[0][...], b_refs[0][...], dims)
        for p in range(1, n_p):
            part = part + _dot(a_refs[p][...], b_refs[p][...], dims)

        def finish(total):
            if epi is None:
                o_refs[0][...] = total.astype(o_refs[0].dtype)
            else:
                epi(total, e_refs, o_refs, ids)

        if nk == 1:
            finish(part)
        else:
            @pl.when(ids[2] == 0)
            def _():
                acc[...] = part

            @pl.when(ids[2] > 0)
            def _():
                acc[...] += part

            @pl.when(ids[2] == nk - 1)
            def _():
                finish(acc[...])

    in_specs, args = [], []
    for a, a_spec, b, b_spec in pairs:
        in_specs += [a_spec, b_spec]
        args += [a, b]
    in_specs += list(especs)
    args += list(eins)
    res = pl.pallas_call(
        body, name=name, grid=grid, in_specs=in_specs, out_specs=list(out_specs), out_shape=list(outs),
        scratch_shapes=[pltpu.VMEM(acc_shape, F32)],
        compiler_params=_params(("arbitrary", "arbitrary", "arbitrary")),
    )(*args)
    return res


def _sds(shape, dtype):
    return jax.ShapeDtypeStruct(shape, dtype)


def _mm_plain(name, a, b, dims, out_dtype, tm=512, tn=1024, tk=1024, epi=None, eins=(), especs=None, extra_outs=(), extra_specs=(),
              b_outer=False):
    if dims == NN:
        (m, k), n = a.shape, b.shape[1]
    elif dims == NT:
        (m, k), n = a.shape, b.shape[0]
    else:
        (k, m), n = a.shape, b.shape[1]
    tm, tn = _tile(m, tm, LANE if dims == TN else SUBLANE), _tile(n, tn, LANE)
    tk = _tile(k, tk, SUBLANE if dims == TN else LANE)
    grid = (n // tn, m // tm, k // tk) if b_outer else (m // tm, n // tn, k // tk)

    def spec(block, index):
        if b_outer:
            return pl.BlockSpec(block, lambda j, i, kk: index(i, j, kk))
        return pl.BlockSpec(block, index)

    if dims == NN:
        a_spec = spec((tm, tk), lambda i, j, kk: (i, kk))
        b_spec = spec((tk, tn), lambda i, j, kk: (kk, j))
    elif dims == NT:
        a_spec = spec((tm, tk), lambda i, j, kk: (i, kk))
        b_spec = spec((tn, tk), lambda i, j, kk: (j, kk))
    else:
        a_spec = spec((tk, tm), lambda i, j, kk: (kk, i))
        b_spec = spec((tk, tn), lambda i, j, kk: (kk, j))
    o_spec = spec((tm, tn), lambda i, j, kk: (i, j))
    if especs is None:
        especs = [o_spec] * len(eins)
    else:
        especs = [o_spec if s is None else s for s in especs]
    res = _mm(name, [(a, a_spec, b, b_spec)], dims, grid, [_sds((m, n), out_dtype)] + list(extra_outs),
              [o_spec] + list(extra_specs), (tm, tn), epi, eins, especs)
    return res if extra_outs else res[0]


def _rms_fwd(name, x, g):
    l, d = x.shape
    tm = _tile(l, 1024, SUBLANE)

    def body(x_ref, g_ref, o_ref):
        xv = x_ref[...]
        r = lax.rsqrt(jnp.mean(xv * xv, axis=-1, keepdims=True) + EPS)
        o_ref[...] = (xv * r * g_ref[...]).astype(o_ref.dtype)

    return pl.pallas_call(
        body, name=name, grid=(l // tm,),
        in_specs=[pl.BlockSpec((tm, d), lambda i: (i, 0)), pl.BlockSpec((1, d), lambda i: (0, 0))],
        out_specs=pl.BlockSpec((tm, d), lambda i: (i, 0)), out_shape=_sds((l, d), BF),
        compiler_params=_params(("arbitrary",)),
    )(x, g)


def _rms_bwd_epi(first_axis):
    def epi(acc, e_refs, o_refs, ids):
        x_ref, g_ref, dr_ref = e_refs
        dx_ref, dg_ref = o_refs
        xv = x_ref[...]
        r = lax.rsqrt(jnp.mean(xv * xv, axis=-1, keepdims=True) + EPS)
        xh = xv * r
        dxh = acc * g_ref[...]
        dx_ref[...] = dr_ref[...] + r * (dxh - xh * jnp.mean(dxh * xh, axis=-1, keepdims=True))
        part = jnp.sum(acc * xh, axis=0, keepdims=True)

        @pl.when(ids[first_axis] == 0)
        def _():
            dg_ref[...] = part

        @pl.when(ids[first_axis] > 0)
        def _():
            dg_ref[...] += part

    return epi


def _loss_head(name, x, g, target):
    l, d = x.shape
    tm = _tile(l, 512, SUBLANE)
    n = l // tm

    def body(x_ref, g_ref, t_ref, dx_ref, dg_ref, loss_ref, lacc):
        i = pl.program_id(0)
        xv = x_ref[...]
        r = lax.rsqrt(jnp.mean(xv * xv, axis=-1, keepdims=True) + EPS)
        xh = xv * r
        e = xh * g_ref[...] - t_ref[...]
        dy = e * (1.0 / d)
        dxh = dy * g_ref[...]
        dx_ref[...] = r * (dxh - xh * jnp.mean(dxh * xh, axis=-1, keepdims=True))
        dg_part = jnp.sum(dy * xh, axis=0, keepdims=True)
        l_part = jnp.sum(e * e, axis=0, keepdims=True)

        @pl.when(i == 0)
        def _():
            dg_ref[...] = dg_part
            lacc[...] = l_part

        @pl.when(i > 0)
        def _():
            dg_ref[...] += dg_part
            lacc[...] += l_part

        @pl.when(i == n - 1)
        def _():
            loss_ref[...] = jnp.zeros_like(loss_ref) + jnp.sum(lacc[...]) * (0.5 / d)

    return pl.pallas_call(
        body, name=name, grid=(n,),
        in_specs=[pl.BlockSpec((tm, d), lambda i: (i, 0)), pl.BlockSpec((1, d), lambda i: (0, 0)),
                  pl.BlockSpec((tm, d), lambda i: (i, 0))],
        out_specs=[pl.BlockSpec((tm, d), lambda i: (i, 0)), pl.BlockSpec((1, d), lambda i: (0, 0)),
                   pl.BlockSpec((1, LANE), lambda i: (0, 0))],
        out_shape=[_sds((l, d), F32), _sds((1, d), F32), _sds((1, LANE), F32)],
        scratch_shapes=[pltpu.VMEM((1, d), F32)],
        compiler_params=_params(("arbitrary",)),
    )(x, g, target)


def _ffn_up(name, hn, w1):
    l, d = hn.shape
    f = w1.shape[1] // 2
    tm, tn = _tile(l, 512, SUBLANE), _tile(f, 1408, LANE)
    nj = f // tn

    def body(h_ref, wg_ref, wu_ref, gu_ref, a_ref):
        h = h_ref[...]
        g = jnp.dot(h, wg_ref[...], preferred_element_type=F32)
        u = jnp.dot(h, wu_ref[...], preferred_element_type=F32)
        s = _sigmoid(g)
        gs = g * s
        gu_ref[0] = (u * (s + gs * (1.0 - s))).astype(BF)
        gu_ref[1] = gs.astype(BF)
        a_ref[...] = (gs * u).astype(BF)

    return pl.pallas_call(
        body, name=name, grid=(nj, l // tm),
        in_specs=[pl.BlockSpec((tm, d), lambda j, i: (i, 0)), pl.BlockSpec((d, tn), lambda j, i: (0, j)),
                  pl.BlockSpec((d, tn), lambda j, i: (0, j + nj))],
        out_specs=[pl.BlockSpec((2, tm, tn), lambda j, i: (0, i, j)), pl.BlockSpec((tm, tn), lambda j, i: (i, j))],
        out_shape=[_sds((2, l, f), BF), _sds((l, f), BF)],
        compiler_params=_params(("arbitrary", "arbitrary")),
    )(hn, w1, w1)


def _residual_epi(scale, with_norm):
    def epi(acc, e_refs, o_refs, ids):
        xn = e_refs[0][...] + scale * acc
        o_refs[0][...] = xn
        if with_norm:
            r = lax.rsqrt(jnp.mean(xn * xn, axis=-1, keepdims=True) + EPS)
            o_refs[1][...] = (xn * r * e_refs[1][...]).astype(o_refs[1].dtype)

    return epi


def _ffn_fwd(tag, x, hn, get_w1, get_w2, next_gnorm):
    w1 = get_w1(hn)
    gu, a = _ffn_up(tag + "_up", hn, w1)
    w2 = get_w2(a)
    l, d = x.shape
    if next_gnorm is None:
        x_new = _mm_plain(tag + "_down", a, w2, NN, F32, tm=512, tn=d, tk=w2.shape[0], epi=_residual_epi(0.5, False), eins=[x])
        hn_next = None
    else:
        vec = pl.BlockSpec((1, d), lambda i, j, kk: (0, 0))
        tm = _tile(l, 512, SUBLANE)
        x_new, hn_next = _mm_plain(tag + "_down", a, w2, NN, F32, tm=512, tn=d, tk=w2.shape[0], epi=_residual_epi(0.5, True),
                                   eins=[x, next_gnorm], especs=[None, vec],
                                   extra_outs=[_sds((l, d), BF)], extra_specs=[pl.BlockSpec((tm, d), lambda i, j, kk: (i, 0))])
    return x_new, hn_next, (hn, gu, a), w1, w2


def _ffn_bwd(tag, dres, x, gnorm, w1, w2, saved, on_grads):
    hn, gu, a = saved
    l, d = x.shape
    f = w2.shape[0]
    tm, tn = _tile(l, 512, SUBLANE), _tile(f, 1408, LANE)
    nj = f // tn

    def epi_gu(acc, e_refs, o_refs, ids):
        da = 0.5 * acc
        o_refs[0][0] = (da * e_refs[0][0].astype(F32)).astype(BF)
        o_refs[0][1] = (da * e_refs[0][1].astype(F32)).astype(BF)

    tmg = _tile(l, 256, SUBLANE)
    gu_spec = pl.BlockSpec((2, tmg, f), lambda j, i, kk: (0, i, 0))
    dgu = _mm(tag + "_dgu",
              [(dres, pl.BlockSpec((tmg, d), lambda j, i, kk: (i, 0)), w2, pl.BlockSpec((f, d), lambda j, i, kk: (0, 0)))],
              NT, (1, l // tmg, 1), [_sds((2, l, f), BF)], [gu_spec], (tmg, f), epi_gu, [gu], [gu_spec])[0]

    def epi_half(acc, e_refs, o_refs, ids):
        o_refs[0][...] = (0.5 * acc).astype(BF)

    dw2 = _mm_plain(tag + "_dw2", a, dres, TN, BF, tm=1408, tn=d, tk=2048, epi=epi_half)

    tk = _tile(l, 2048, SUBLANE)
    dw1 = _mm(tag + "_dw1",
              [(hn, pl.BlockSpec((tk, d), lambda i, j, kk: (kk, 0)), dgu, pl.BlockSpec((None, tk, tn), lambda i, j, kk: (j // nj, kk, j % nj)))],
              TN, (1, 2 * nj, l // tk), [_sds((d, 2 * f), BF)], [pl.BlockSpec((d, tn), lambda i, j, kk: (0, j))], (d, tn))[0]

    gnorm = on_grads(dw1, dw2, gnorm)
    row = pl.BlockSpec((tm, d), lambda i, j, kk: (i, 0))
    vec = pl.BlockSpec((1, d), lambda i, j, kk: (0, 0))
    once = pl.Buffered(1)
    dx, dg = _mm(tag + "_dhn",
                 [(dgu, pl.BlockSpec((None, tm, f), lambda i, j, kk: (0, i, 0)), w1, pl.BlockSpec((d, f), lambda i, j, kk: (0, 0), pipeline_mode=once)),
                  (dgu, pl.BlockSpec((None, tm, f), lambda i, j, kk: (1, i, 0)), w1, pl.BlockSpec((d, f), lambda i, j, kk: (0, 1), pipeline_mode=once))],
                 NT, (l // tm, 1, 1), [_sds((l, d), F32), _sds((1, d), F32)], [row, vec], (tm, d),
                 _rms_bwd_epi(0), [x, gnorm, dres], [row, vec, row])
    return dx, dg


def _s5_chunk_tables(lam_re, lam_im, b_re, b_im, c_re, c_im, log_dt, hs):
    f32 = F32
    g, n = lam_re.shape[1], lam_re.shape[2]
    p = b_re.shape[-1]
    nch, gpc, nt = (g * n) // hs, hs // n, hs // LANE
    lr = jnp.minimum(lam_re.astype(f32), -1e-4)
    li = lam_im.astype(f32)
    dt = jnp.exp(log_dt.astype(f32))[..., None]
    mag = jnp.exp(lr * dt)
    ar = mag * jnp.cos(li * dt)
    ai = mag * jnp.sin(li * dt)
    den = lr * lr + li * li
    cr = ((ar - 1.0) * lr + ai * li) / den
    ci = (ai * lr - (ar - 1.0) * li) / den
    bbr = cr[..., None] * b_re - ci[..., None] * b_im
    bbi = cr[..., None] * b_im + ci[..., None] * b_re
    a_f = jnp.stack([ar, ai], axis=1).reshape(2, 2, nch, nt, LANE).transpose(0, 2, 1, 3, 4).reshape(2, nch, 2 * nt, LANE)
    rows_g = jnp.arange(gpc * p) // p
    cols_g = (jnp.arange(2 * hs) % hs) // n
    diag = (rows_g[:, None] == cols_g[None, :]).astype(f32)
    bb = jnp.stack([bbr, bbi], axis=1).reshape(2, 2, nch, hs, p)
    bd = jnp.tile(bb.transpose(0, 2, 4, 1, 3).reshape(2, nch, p, 2 * hs), (1, 1, gpc, 1)) * diag
    cc = jnp.stack([c_re, -c_im], axis=1).reshape(2, 2, nch, gpc, p, n)
    cd = jnp.tile(cc.transpose(0, 2, 4, 1, 3, 5).reshape(2, nch, p, 2 * hs), (1, 1, gpc, 1)) * diag
    return a_f, bd, cd


def _fold_store(ref, val, tb, ntiles):
    for s in range(ntiles):
        ref[:, s * SUBLANE:(s + 1) * SUBLANE, :] = val[:, s * LANE:(s + 1) * LANE].reshape(tb // SUBLANE, SUBLANE, LANE)


def _unfold(ref, tb, ntiles):
    return jnp.concatenate([ref[:, s * SUBLANE:(s + 1) * SUBLANE, :].reshape(tb, LANE) for s in range(ntiles)], axis=1)


def _s5_fwd(name, proj, bd, cd, a_f, reverse):
    l = proj.shape[0]
    nch, cu, hs2 = bd.shape
    nt = hs2 // (2 * LANE)
    frows = 2 * nt * SUBLANE
    tb = _tile(l, 512, SUBLANE)
    nb = l // tb

    def body(u_ref, bd_ref, cd_ref, a_ref, xf_ref, y_ref, st):
        r = pl.program_id(1)

        @pl.when(r == 0)
        def _():
            st[...] = jnp.zeros_like(st)

        _fold_store(xf_ref, _dot(u_ref[...], bd_ref[...]), tb, 2 * nt)
        ar, ai = a_ref[0:nt, :], a_ref[nt:2 * nt, :]

        def group(gi, carry):
            rr = (tb // SUBLANE - 1 - gi) if reverse else gi
            sr, si = carry
            for qq in range(SUBLANE):
                q = (SUBLANE - 1 - qq) if reverse else qq
                re_rows, im_rows = pl.ds(q, nt, stride=SUBLANE), pl.ds(nt * SUBLANE + q, nt, stride=SUBLANE)
                nr = ar * sr - ai * si + xf_ref[rr, re_rows, :]
                ni = ar * si + ai * sr + xf_ref[rr, im_rows, :]
                xf_ref[rr, re_rows, :] = nr
                xf_ref[rr, im_rows, :] = ni
                sr, si = nr, ni
            return sr, si

        fin = lax.fori_loop(0, tb // SUBLANE, group, (st[0:nt, :], st[nt:2 * nt, :]))
        st[0:nt, :] = fin[0]
        st[nt:2 * nt, :] = fin[1]
        y_ref[...] = _dot(_unfold(xf_ref, tb, 2 * nt), cd_ref[...], NT)

    def rows(r):
        return (nb - 1 - r) if reverse else r

    return pl.pallas_call(
        body, name=name, grid=(nch, nb),
        in_specs=[pl.BlockSpec((tb, cu), lambda c, r: (rows(r), c)), pl.BlockSpec((None, cu, hs2), lambda c, r: (c, 0, 0)),
                  pl.BlockSpec((None, cu, hs2), lambda c, r: (c, 0, 0)), pl.BlockSpec((None, 2 * nt, LANE), lambda c, r: (c, 0, 0))],
        out_specs=[pl.BlockSpec((tb // SUBLANE, frows, LANE), lambda c, r: (rows(r), c, 0)), pl.BlockSpec((tb, cu), lambda c, r: (rows(r), c))],
        out_shape=[_sds((l // SUBLANE, nch * frows, LANE), F32), _sds((l, nch * cu), F32)],
        scratch_shapes=[pltpu.VMEM((2 * nt, LANE), F32)],
        compiler_params=_params(("arbitrary", "arbitrary")),
    )(proj, bd, cd, a_f)


def _s5_bwd(name, proj, dy, xf, bd, cd, a_conj, reverse):
    l = proj.shape[0]
    nch, cu, hs2 = bd.shape
    nt = hs2 // (2 * LANE)
    frows = 2 * nt * SUBLANE
    tb = _tile(l, 512, SUBLANE)
    nb = l // tb

    def body(u_ref, dy_ref, xs_ref, bd_ref, cd_ref, a_ref, du_ref, dbd_ref, dcd_ref, da_ref, lam, st):
        r = pl.program_id(1)

        @pl.when(r == 0)
        def _():
            st[...] = jnp.zeros_like(st)
            dbd_ref[...] = jnp.zeros_like(dbd_ref)
            dcd_ref[...] = jnp.zeros_like(dcd_ref)
            da_ref[...] = jnp.zeros_like(da_ref)

        dyv = dy_ref[...]
        _fold_store(lam, _dot(dyv, cd_ref[...]), tb, 2 * nt)
        ar, ai = a_ref[0:nt, :], a_ref[nt:2 * nt, :]

        def group(gi, carry):
            rr = (tb // SUBLANE - 1 - gi) if reverse else gi
            sr, si, cr, ci = carry
            for qq in range(SUBLANE):
                q = (SUBLANE - 1 - qq) if reverse else qq
                re_rows, im_rows = pl.ds(q, nt, stride=SUBLANE), pl.ds(nt * SUBLANE + q, nt, stride=SUBLANE)
                xr, xi = xs_ref[rr, re_rows, :], xs_ref[rr, im_rows, :]
                cr = cr + sr * xr + si * xi
                ci = ci + si * xr - sr * xi
                nr = ar * sr - ai * si + lam[rr, re_rows, :]
                ni = ar * si + ai * sr + lam[rr, im_rows, :]
                lam[rr, re_rows, :] = nr
                lam[rr, im_rows, :] = ni
                sr, si = nr, ni
            return sr, si, cr, ci

        zero = jnp.zeros((nt, LANE), F32)
        fin = lax.fori_loop(0, tb // SUBLANE, group, (st[0:nt, :], st[nt:2 * nt, :], zero, zero))
        st[0:nt, :] = fin[0]
        st[nt:2 * nt, :] = fin[1]
        da_ref[0:nt, :] += fin[2]
        da_ref[nt:2 * nt, :] += fin[3]
        lam_u = _unfold(lam, tb, 2 * nt)
        du_ref[...] = _dot(lam_u, bd_ref[...], NT)
        dbd_ref[...] += _dot(u_ref[...], lam_u, TN)
        dcd_ref[...] += _dot(dyv, _unfold(xs_ref, tb, 2 * nt), TN)

    def rows(r):
        return (nb - 1 - r) if reverse else r

    chunk_rows = pl.BlockSpec((tb, cu), lambda c, r: (rows(r), c))
    bd_spec = pl.BlockSpec((None, cu, hs2), lambda c, r: (c, 0, 0))
    cd_spec = bd_spec
    a_spec = pl.BlockSpec((None, 2 * nt, LANE), lambda c, r: (c, 0, 0))
    return pl.pallas_call(
        body, name=name, grid=(nch, nb),
        in_specs=[chunk_rows, chunk_rows, pl.BlockSpec((tb // SUBLANE, frows, LANE), lambda c, r: (rows(r), c, 0)), bd_spec, cd_spec, a_spec],
        out_specs=[chunk_rows, bd_spec, cd_spec, a_spec],
        out_shape=[_sds((l, nch * cu), F32), _sds((nch, cu, hs2), F32), _sds((nch, cu, hs2), F32), _sds((nch, 2 * nt, LANE), F32)],
        scratch_shapes=[pltpu.VMEM((tb // SUBLANE, frows, LANE), F32), pltpu.VMEM((2 * nt, LANE), F32)],
        compiler_params=_params(("arbitrary", "arbitrary")),
    )(proj, dy, xf, bd, cd, a_conj)


def _s5_du(name, du_f, du_b, dy, proj, d_row):
    l, w = dy.shape
    tm = _tile(l, 1024, SUBLANE)

    def body(f_ref, b_ref, dy_ref, u_ref, d_ref, du_ref, dd_ref):
        i = pl.program_id(0)
        dyv = dy_ref[...]
        du_ref[...] = (f_ref[...] + b_ref[...] + dyv * d_ref[...]).astype(du_ref.dtype)
        part = jnp.sum(dyv * u_ref[...].astype(F32), axis=0, keepdims=True)

        @pl.when(i == 0)
        def _():
            dd_ref[...] = part

        @pl.when(i > 0)
        def _():
            dd_ref[...] += part

    row = pl.BlockSpec((tm, w), lambda i: (i, 0))
    vec = pl.BlockSpec((1, w), lambda i: (0, 0))
    return pl.pallas_call(
        body, name=name, grid=(l // tm,), in_specs=[row, row, row, row, vec], out_specs=[row, vec],
        out_shape=[_sds((l, w), BF), _sds((1, w), F32)],
        compiler_params=_params(("arbitrary",)),
    )(du_f, du_b, dy, proj, d_row)


def _gelu(y):
    c = math.sqrt(2.0 / math.pi)
    return 0.5 * y * (1.0 + jnp.tanh(c * (y + 0.044715 * y * y * y)))


def _gelu_grad(y):
    c = math.sqrt(2.0 / math.pi)
    th = jnp.tanh(c * (y + 0.044715 * y * y * y))
    return 0.5 * (1.0 + th) + 0.5 * y * (1.0 - th * th) * c * (1.0 + 3.0 * 0.044715 * y * y)


def _glu_fwd(name, y_f, y_b, proj, d_row, w):
    l, wd = y_f.shape
    tm = _tile(l, 1024, SUBLANE)

    def body(yf_ref, yb_ref, u_ref, d_ref, w_ref, y_ref, o_ref):
        y = yf_ref[...] + yb_ref[...] + u_ref[...].astype(F32) * d_ref[...]
        y_ref[...] = y
        gy = _gelu(y)
        z = _dot(gy, w_ref[...])
        o_ref[...] = (gy * _sigmoid(z)).astype(o_ref.dtype)

    row = pl.BlockSpec((tm, wd), lambda i: (i, 0))
    return pl.pallas_call(
        body, name=name, grid=(l // tm,),
        in_specs=[row, row, row, pl.BlockSpec((1, wd), lambda i: (0, 0)), pl.BlockSpec((wd, wd), lambda i: (0, 0))],
        out_specs=[row, row], out_shape=[_sds((l, wd), F32), _sds((l, wd), BF)],
        compiler_params=_params(("arbitrary",)),
    )(y_f, y_b, proj, d_row, w)


def _glu_bwd(name, y, w, dout, dcol):
    l, wd = y.shape
    tm = _tile(l, 1024, SUBLANE)

    def body(y_ref, w_ref, d_ref, dy_ref, dw_ref):
        i = pl.program_id(0)
        yv = y_ref[...]
        gy = _gelu(yv)
        s = _sigmoid(_dot(gy, w_ref[...]))
        d = d_ref[...].astype(F32)
        t = d * gy * s * (1.0 - s)
        dgy = d * s + _dot(t, w_ref[...], NT)
        dy_ref[...] = dgy * _gelu_grad(yv)
        part = _dot(gy, t, TN)

        @pl.when(i == 0)
        def _():
            dw_ref[...] = part

        @pl.when(i > 0)
        def _():
            dw_ref[...] += part

    return pl.pallas_call(
        body, name=name, grid=(l // tm,),
        in_specs=[pl.BlockSpec((tm, wd), lambda i: (i, 0)), pl.BlockSpec((wd, wd), lambda i: (0, 0)),
                  pl.BlockSpec((tm, wd), lambda i: (i, dcol))],
        out_specs=[pl.BlockSpec((tm, wd), lambda i: (i, 0)), pl.BlockSpec((wd, wd), lambda i: (0, 0))],
        out_shape=[_sds((l, wd), F32), _sds((wd, wd), F32)],
        compiler_params=_params(("arbitrary",)),
    )(y, w, dout)


def _log_sigmoid(x):
    return jnp.minimum(x, 0.0) - jnp.log(1.0 + jnp.exp(-jnp.abs(x)))


def _gate_fwd(name, glo, wf, wb, bf, bb):
    l, r2 = glo.shape
    hk = wf.shape[1]
    tm = _tile(l, 1024, SUBLANE)

    def body(x_ref, wf_ref, wb_ref, bf_ref, bb_ref, gf_ref, gb_ref):
        xv = x_ref[...]
        gf_ref[...] = _log_sigmoid(_dot(xv, wf_ref[...]) + bf_ref[...]) * (1.0 / GLA_GATE_NORM)
        gb_ref[...] = _log_sigmoid(_dot(xv, wb_ref[...]) + bb_ref[...]) * (1.0 / GLA_GATE_NORM)

    w_spec = pl.BlockSpec((r2, hk), lambda i: (0, 0))
    b_spec = pl.BlockSpec((1, hk), lambda i: (0, 0))
    o_spec = pl.BlockSpec((tm, hk), lambda i: (i, 0))
    return pl.pallas_call(
        body, name=name, grid=(l // tm,),
        in_specs=[pl.BlockSpec((tm, r2), lambda i: (i, 0)), w_spec, w_spec, b_spec, b_spec],
        out_specs=[o_spec, o_spec], out_shape=[_sds((l, hk), F32), _sds((l, hk), F32)],
        compiler_params=_params(("arbitrary",)),
    )(glo, wf, wb, bf, bb)


def _gate_bwd(name, glo, wf, wb, bf, bb, dgf, dgb):
    l, r2 = glo.shape
    hk = wf.shape[1]
    tm = _tile(l, 1024, SUBLANE)

    def body(x_ref, wf_ref, wb_ref, bf_ref, bb_ref, dgf_ref, dgb_ref, dx_ref, dwf_ref, dwb_ref, dbf_ref, dbb_ref):
        i = pl.program_id(0)
        xv = x_ref[...]
        kf = _dot(xv, wf_ref[...]) + bf_ref[...]
        kb = _dot(xv, wb_ref[...]) + bb_ref[...]
        dkf = dgf_ref[...] * (1.0 / GLA_GATE_NORM) * _sigmoid(-kf)
        dkb = dgb_ref[...] * (1.0 / GLA_GATE_NORM) * _sigmoid(-kb)
        dx_ref[...] = _dot(dkf, wf_ref[...], NT) + _dot(dkb, wb_ref[...], NT)
        parts = (_dot(xv, dkf, TN), _dot(xv, dkb, TN), jnp.sum(dkf, axis=0, keepdims=True), jnp.sum(dkb, axis=0, keepdims=True))
        accs = (dwf_ref, dwb_ref, dbf_ref, dbb_ref)

        @pl.when(i == 0)
        def _():
            for a_, p_ in zip(accs, parts):
                a_[...] = p_

        @pl.when(i > 0)
        def _():
            for a_, p_ in zip(accs, parts):
                a_[...] += p_

    w_spec = pl.BlockSpec((r2, hk), lambda i: (0, 0))
    b_spec = pl.BlockSpec((1, hk), lambda i: (0, 0))
    g_spec = pl.BlockSpec((tm, hk), lambda i: (i, 0))
    x_spec = pl.BlockSpec((tm, r2), lambda i: (i, 0))
    return pl.pallas_call(
        body, name=name, grid=(l // tm,),
        in_specs=[x_spec, w_spec, w_spec, b_spec, b_spec, g_spec, g_spec],
        out_specs=[x_spec, w_spec, w_spec, b_spec, b_spec],
        out_shape=[_sds((l, r2), F32), _sds((r2, hk), F32), _sds((r2, hk), F32), _sds((1, hk), F32), _sds((1, hk), F32)],
        compiler_params=_params(("arbitrary",)),
    )(glo, wf, wb, bf, bb, dgf, dgb)


def _chunk_terms(qc, kc, lg, chunk, reverse):
    ri = lax.broadcasted_iota(jnp.int32, (chunk, chunk), 0)
    ci = lax.broadcasted_iota(jnp.int32, (chunk, chunk), 1)
    mask = (ci > ri) if reverse else (ci <= ri)
    pos = lax.broadcasted_iota(jnp.int32, (chunk, 1), 0).astype(F32)
    cum = ((chunk - pos) if reverse else (pos + 1.0)) * lg
    last = chunk * lg
    e = jnp.exp(cum)
    einv = jnp.exp(-cum)
    dec = jnp.exp(last - cum)
    return e, einv, dec, qc * e, kc * einv, kc * dec, jnp.exp(last), mask


def _lin_specs(width, col, tb, nb, reverse):
    return pl.BlockSpec((tb, width), lambda h, r: ((nb - 1 - r) if reverse else r, col + h))


def _lin_fwd(name, q, k, v, lgtab, prev_o=None, *, heads, hb, dk, dv, chunk, tb, qcol, kcol, vcol, reverse):
    l = q.shape[0]
    nb, ncb, ng = l // tb, tb // chunk, heads // hb

    def body(q_ref, k_ref, v_ref, lg_ref, *rest):
        p_ref = rest[0] if prev_o is not None else None
        o_ref, sp_ref, st = rest[-3:]

        @pl.when(pl.program_id(1) == 0)
        def _():
            st[...] = jnp.zeros_like(st)

        order = [(ncb - 1 - c) if reverse else c for c in range(ncb)]
        for h in range(hb):
            ks, vs = slice(h * dk, (h + 1) * dk), slice(h * dv, (h + 1) * dv)
            pre = []
            for cc in order:
                rows = pl.ds(cc * chunk, chunk)
                qc, kc, vc = q_ref[rows, ks].astype(F32), k_ref[rows, ks].astype(F32), v_ref[rows, vs]
                _, _, _, qd, ki, kdec, e_last, mask = _chunk_terms(qc, kc, lg_ref[h, :, 0:1], chunk, reverse)
                a = jnp.where(mask, _dot(qd, ki, NT), 0.0)
                pre.append((qd, _dot(a, vc), _dot(vc, kdec, TN), e_last))
            s_t = st[h]
            s_in = []
            for _, _, kv, e_last in pre:
                s_in.append(s_t)
                s_t = s_t * e_last + kv
            st[h] = s_t
            for cc, (qd, o_intra, _, _), s_c in zip(order, pre, s_in):
                rows = pl.ds(cc * chunk, chunk)
                oc = o_intra + _dot(qd, s_c, NT)
                if p_ref is not None:
                    oc = oc + p_ref[rows, vs]
                o_ref[rows, vs] = oc.astype(o_ref.dtype)
                sp_ref[cc, h] = s_c

    o_spec = _lin_specs(hb * dv, 0, tb, nb, reverse)
    extra = [] if prev_o is None else [prev_o]
    return pl.pallas_call(
        body, name=name, grid=(ng, nb),
        in_specs=[_lin_specs(hb * dk, qcol, tb, nb, reverse), _lin_specs(hb * dk, kcol, tb, nb, reverse),
                  _lin_specs(hb * dv, vcol, tb, nb, reverse), pl.BlockSpec((hb, 1, LANE), lambda h, r: (h, 0, 0))] + [o_spec] * len(extra),
        out_specs=[o_spec, pl.BlockSpec((ncb, hb, dv, dk), lambda h, r: ((nb - 1 - r) if reverse else r, h, 0, 0))],
        out_shape=[_sds((l, heads * dv), F32 if prev_o is None else BF), _sds((l // chunk, heads, dv, dk), F32)],
        scratch_shapes=[pltpu.VMEM((hb, dv, dk), F32)],
        compiler_params=_params(("arbitrary", "arbitrary")),
    )(q, k, v, lgtab, *extra)


def _lin_bwd(name, q, k, v, lgtab, sprev, do, prev, *, heads, hb, dk, dv, chunk, tb, qcol, kcol, vcol, reverse):
    l = q.shape[0]
    nb, ncb, ng = l // tb, tb // chunk, heads // hb
    brev = not reverse
    n_prev = 0 if prev is None else len(prev)

    def body(q_ref, k_ref, v_ref, lg_ref, sp_ref, do_ref, *rest):
        p_refs = rest[:n_prev]
        dq_ref, dk_ref, dv_ref, dst = rest[n_prev:]

        @pl.when(pl.program_id(1) == 0)
        def _():
            dst[...] = jnp.zeros_like(dst)

        order = [(ncb - 1 - c) if brev else c for c in range(ncb)]
        for h in range(hb):
            ks, vs = slice(h * dk, (h + 1) * dk), slice(h * dv, (h + 1) * dv)
            pre = []
            for cc in order:
                rows = pl.ds(cc * chunk, chunk)
                qc, kc, vc = q_ref[rows, ks].astype(F32), k_ref[rows, ks].astype(F32), v_ref[rows, vs]
                e, einv, dec, qd, ki, kdec, e_last, mask = _chunk_terms(qc, kc, lg_ref[h, :, 0:1], chunk, reverse)
                a = jnp.where(mask, _dot(qd, ki, NT), 0.0)
                s_t, doc = sp_ref[cc, h], do_ref[rows, vs]
                da = jnp.where(mask, _dot(doc, vc, NT), 0.0)
                dqc = (_dot(da, ki) + _dot(doc, s_t)) * e
                if n_prev:
                    dqc = dqc + p_refs[0][rows, ks]
                dq_ref[rows, ks] = dqc
                pre.append((_dot(a, doc, TN), _dot(da, qd, TN) * einv, _dot(doc, qd, TN), e_last, kdec, vc, dec))
            ds_t = dst[h]
            ds_in = []
            for _, _, ds_add, e_last, _, _, _ in pre:
                ds_in.append(ds_t)
                ds_t = ds_t * e_last + ds_add
            dst[h] = ds_t
            for cc, (dv_part, dk_part, _, _, kdec, vc, dec), ds_c in zip(order, pre, ds_in):
                rows = pl.ds(cc * chunk, chunk)
                dvc = dv_part + _dot(kdec, ds_c, NT)
                dkc = dk_part + _dot(vc, ds_c) * dec
                if n_prev:
                    dkc = dkc + p_refs[1][rows, ks]
                    dvc = dvc + p_refs[2][rows, vs]
                dk_ref[rows, ks] = dkc
                dv_ref[rows, vs] = dvc.astype(dv_ref.dtype)

    k_spec, v_spec = _lin_specs(hb * dk, 0, tb, nb, brev), _lin_specs(hb * dv, 0, tb, nb, brev)
    in_specs = [_lin_specs(hb * dk, qcol, tb, nb, brev), _lin_specs(hb * dk, kcol, tb, nb, brev), _lin_specs(hb * dv, vcol, tb, nb, brev),
                pl.BlockSpec((hb, 1, LANE), lambda h, r: (h, 0, 0)),
                pl.BlockSpec((ncb, hb, dv, dk), lambda h, r: ((nb - 1 - r) if brev else r, h, 0, 0)), v_spec]
    args = [q, k, v, lgtab, sprev, do]
    if n_prev:
        in_specs += [k_spec, k_spec, v_spec]
        args += list(prev)
    return pl.pallas_call(
        body, name=name, grid=(ng, nb), in_specs=in_specs, out_specs=[k_spec, k_spec, v_spec],
        out_shape=[_sds((l, heads * dk), F32), _sds((l, heads * dk), F32), _sds((l, heads * dv), BF if n_prev else F32)],
        scratch_shapes=[pltpu.VMEM((hb, dv, dk), F32)],
        compiler_params=_params(("arbitrary", "arbitrary")),
    )(*args)


def _log2(n):
    assert n & (n - 1) == 0, "a power of two"
    return n.bit_length() - 1


def _gla_block_terms(q, k, g, qscale, chunk, tb, reverse):
    ri = lax.broadcasted_iota(jnp.int32, (tb, tb), 0)
    ci = lax.broadcasted_iota(jnp.int32, (tb, tb), 1)
    same = jnp.right_shift(ri, _log2(chunk)) == jnp.right_shift(ci, _log2(chunk))
    t_in = jnp.logical_and(same, (ci >= ri) if reverse else (ci <= ri)).astype(F32)
    cum = _dot3(t_in, g)
    tot = _dot3(same.astype(F32), g)
    e = jnp.exp(cum)
    einv = jnp.exp(-cum)
    dec = jnp.exp(tot - cum)
    return e, einv, dec, jnp.exp(tot), q * (qscale * e), k * einv, k * dec, t_in


def _gla_masks(hk, dk, heads, chunk, reverse):
    lane = lax.broadcasted_iota(jnp.int32, (1, hk), 1)
    head_of = jnp.right_shift(lane, _log2(dk))
    ri = lax.broadcasted_iota(jnp.int32, (chunk, chunk), 0)
    ci = lax.broadcasted_iota(jnp.int32, (chunk, chunk), 1)
    return [head_of == h for h in range(heads)], ((ci > ri) if reverse else (ci <= ri))


def _gla_fwd(name, proj, g, prev_o=None, *, heads, dk, dv, chunk, tb, qcol, kcol, vcol, qscale, reverse):
    l = proj.shape[0]
    nb, ncb, hk, hv = l // tb, tb // chunk, heads * dk, heads * dv

    def body(q_ref, k_ref, v_ref, g_ref, *rest):
        p_ref = rest[0] if prev_o is not None else None
        o_ref, sp_ref, st = rest[-3:]

        @pl.when(pl.program_id(0) == 0)
        def _():
            st[...] = jnp.zeros_like(st)

        _, _, _, etot, qd, ki, kdec, _ = _gla_block_terms(q_ref[...].astype(F32), k_ref[...].astype(F32), g_ref[...], qscale, chunk, tb, reverse)
        heads_m, causal = _gla_masks(hk, dk, heads, chunk, reverse)
        order = [(ncb - 1 - c) if reverse else c for c in range(ncb)]
        pre = []
        for cc in order:
            rc = slice(cc * chunk, (cc + 1) * chunk)
            qd_c, ki_c, kdec_c = qd[rc], ki[rc], kdec[rc]
            kv = jnp.zeros((dv, hk), F32)
            per_head = []
            for h in range(heads):
                qm = jnp.where(heads_m[h], qd_c, 0.0)
                a = jnp.where(causal, _dot(qm, ki_c, NT), 0.0)
                vc = v_ref[rc, h * dv:(h + 1) * dv]
                per_head.append((qm, _dot(a, vc)))
                kv = kv + jnp.where(heads_m[h], _dot(vc, kdec_c, TN), 0.0)
            pre.append((per_head, kv))
        s_all = st[...]
        s_in = []
        for cc, (_, kv) in zip(order, pre):
            s_in.append(s_all)
            s_all = s_all * etot[cc * chunk:cc * chunk + 1, :] + kv
        st[...] = s_all
        for cc, (per_head, _), s_all in zip(order, pre, s_in):
            rc = slice(cc * chunk, (cc + 1) * chunk)
            sp_ref[cc] = s_all
            for h, (qm, o_intra) in enumerate(per_head):
                vs = slice(h * dv, (h + 1) * dv)
                oc = o_intra + _dot(qm, s_all, NT)
                if p_ref is not None:
                    oc = oc + p_ref[rc, vs]
                o_ref[rc, vs] = oc.astype(o_ref.dtype)

    def rows(r):
        return (nb - 1 - r) if reverse else r

    o_spec = pl.BlockSpec((tb, hv), lambda r: (rows(r), 0))
    extra = [] if prev_o is None else [prev_o]
    return pl.pallas_call(
        body, name=name, grid=(nb,),
        in_specs=[pl.BlockSpec((tb, hk), lambda r: (rows(r), qcol)), pl.BlockSpec((tb, hk), lambda r: (rows(r), kcol)),
                  pl.BlockSpec((tb, hv), lambda r: (rows(r), vcol)), pl.BlockSpec((tb, hk), lambda r: (rows(r), 0))] + [o_spec] * len(extra),
        out_specs=[o_spec, pl.BlockSpec((ncb, dv, hk), lambda r: (rows(r), 0, 0))],
        out_shape=[_sds((l, hv), F32 if prev_o is None else BF), _sds((l // chunk, dv, hk), F32)],
        scratch_shapes=[pltpu.VMEM((dv, hk), F32)],
        compiler_params=_params(("arbitrary",)),
    )(proj, proj, proj, g, *extra)


def _gla_bwd(name, proj, g, sprev, do, prev, *, heads, dk, dv, chunk, tb, qcol, kcol, vcol, qscale, reverse):
    l = proj.shape[0]
    nb, ncb, hk, hv = l // tb, tb // chunk, heads * dk, heads * dv
    brev = not reverse
    n_prev = 0 if prev is None else len(prev)

    def body(q_ref, k_ref, v_ref, g_ref, sp_ref, do_ref, *rest):
        p_refs = rest[:n_prev]
        dq_ref, dk_ref, dv_ref, dg_ref, dst, dcs = rest[n_prev:]

        @pl.when(pl.program_id(0) == 0)
        def _():
            dst[...] = jnp.zeros_like(dst)

        e, einv, dec, etot, qd, ki, kdec, t_in = _gla_block_terms(q_ref[...].astype(F32), k_ref[...].astype(F32), g_ref[...], qscale, chunk, tb, reverse)
        heads_m, causal = _gla_masks(hk, dk, heads, chunk, reverse)
        last_row = lax.broadcasted_iota(jnp.int32, (chunk, 1), 0) == (0 if reverse else chunk - 1)
        order = [(ncb - 1 - c) if brev else c for c in range(ncb)]
        pre = []
        for cc in order:
            rc = slice(cc * chunk, (cc + 1) * chunk)
            qd_c, ki_c = qd[rc], ki[rc]
            s_all = sp_ref[cc]
            dqd = jnp.zeros((chunk, hk), F32)
            dki = jnp.zeros((chunk, hk), F32)
            ds_add = jnp.zeros((dv, hk), F32)
            per_head = []
            for h in range(heads):
                vs = slice(h * dv, (h + 1) * dv)
                m = heads_m[h]
                qm = jnp.where(m, qd_c, 0.0)
                a = jnp.where(causal, _dot(qm, ki_c, NT), 0.0)
                doc, vc = do_ref[rc, vs], v_ref[rc, vs]
                da = jnp.where(causal, _dot(doc, vc, NT), 0.0)
                dqd = dqd + jnp.where(m, _dot(da, ki_c) + _dot(doc, s_all), 0.0)
                dki = dki + _dot(da, qm, TN)
                ds_add = ds_add + _dot(doc, qm, TN)
                per_head.append((_dot(a, doc, TN), vc))
            pre.append((dqd, dki, ds_add, per_head, s_all))
        ds_all = dst[...]
        ds_in = []
        for cc, item in zip(order, pre):
            ds_in.append(ds_all)
            ds_all = ds_all * etot[cc * chunk:cc * chunk + 1, :] + item[2]
        dst[...] = ds_all
        for cc, (dqd, dki, _, per_head, s_all), ds_all in zip(order, pre, ds_in):
            rc = slice(cc * chunk, (cc + 1) * chunk)
            qd_c, ki_c, kdec_c = qd[rc], ki[rc], kdec[rc]
            et = etot[cc * chunk:cc * chunk + 1, :]
            dkdec = jnp.zeros((chunk, hk), F32)
            for h, (dv_part, vc) in enumerate(per_head):
                vs = slice(h * dv, (h + 1) * dv)
                m = heads_m[h]
                dvc = dv_part + _dot(jnp.where(m, kdec_c, 0.0), ds_all, NT)
                if n_prev:
                    dvc = dvc + p_refs[2][rc, vs]
                dv_ref[rc, vs] = dvc.astype(dv_ref.dtype)
                dkdec = dkdec + jnp.where(m, _dot(vc, ds_all), 0.0)
            dqc = dqd * e[rc] * qscale
            dkc = dki * einv[rc] + dkdec * dec[rc]
            if n_prev:
                dqc = dqc + p_refs[0][rc, :]
                dkc = dkc + p_refs[1][rc, :]
            dq_ref[rc, :] = dqc
            dk_ref[rc, :] = dkc
            dlast = jnp.sum(dkdec * kdec_c, axis=0, keepdims=True) + et * jnp.sum(s_all * ds_all, axis=0, keepdims=True)
            dcs[rc, :] = dqd * qd_c - dki * ki_c - dkdec * kdec_c + jnp.where(last_row, dlast, 0.0)
        dg_ref[...] = _dot3(t_in, dcs[...], TN)

    def rows(r):
        return (nb - 1 - r) if brev else r

    k_spec = pl.BlockSpec((tb, hk), lambda r: (rows(r), 0))
    v_spec = pl.BlockSpec((tb, hv), lambda r: (rows(r), 0))
    in_specs = [pl.BlockSpec((tb, hk), lambda r: (rows(r), qcol)), pl.BlockSpec((tb, hk), lambda r: (rows(r), kcol)),
                pl.BlockSpec((tb, hv), lambda r: (rows(r), vcol)), k_spec,
                pl.BlockSpec((ncb, dv, hk), lambda r: (rows(r), 0, 0)), v_spec]
    args = [proj, proj, proj, g, sprev, do]
    if n_prev:
        in_specs += [k_spec, k_spec, v_spec]
        args += list(prev)
    return pl.pallas_call(
        body, name=name, grid=(nb,), in_specs=in_specs, out_specs=[k_spec, k_spec, v_spec, k_spec],
        out_shape=[_sds((l, hk), F32), _sds((l, hk), F32), _sds((l, hv), BF if n_prev else F32), _sds((l, hk), F32)],
        scratch_shapes=[pltpu.VMEM((dv, hk), F32), pltpu.VMEM((tb, hk), F32)],
        compiler_params=_params(("arbitrary",)),
    )(*args)


def _headgate_fwd(name, o_sum, og_arr, og_col, gn, dv):
    l, w = o_sum.shape
    tm = _tile(l, 1024, SUBLANE)
    nh = w // dv

    def body(o_ref, og_ref, gn_ref, out_ref):
        for h in range(nh):
            cs = slice(h * dv, (h + 1) * dv)
            o = o_ref[:, cs].astype(F32)
            r = lax.rsqrt(jnp.mean(o * o, axis=-1, keepdims=True) + EPS)
            og = og_ref[:, cs].astype(F32)
            out_ref[:, cs] = (o * r * gn_ref[:, cs] * (og * _sigmoid(og))).astype(out_ref.dtype)

    row = pl.BlockSpec((tm, w), lambda i: (i, 0))
    return pl.pallas_call(
        body, name=name, grid=(l // tm,),
        in_specs=[row, pl.BlockSpec((tm, w), lambda i: (i, og_col)), pl.BlockSpec((1, w), lambda i: (0, 0))],
        out_specs=row, out_shape=_sds((l, w), BF),
        compiler_params=_params(("arbitrary",)),
    )(o_sum, og_arr, gn)


def _headgate_bwd(name, o_sum, og_arr, og_col, gn, dout, dcol, dv):
    l, w = o_sum.shape
    tm = _tile(l, 1024, SUBLANE)
    nh = w // dv

    def body(o_ref, og_ref, gn_ref, d_ref, do_ref, dog_ref, dgn_ref):
        i = pl.program_id(0)
        for h in range(nh):
            cs = slice(h * dv, (h + 1) * dv)
            o = o_ref[:, cs].astype(F32)
            r = lax.rsqrt(jnp.mean(o * o, axis=-1, keepdims=True) + EPS)
            oh = o * r
            og = og_ref[:, cs].astype(F32)
            s = _sigmoid(og)
            d = d_ref[:, cs].astype(F32)
            gnv = gn_ref[:, cs]
            d_on = d * (og * s)
            dog_ref[:, cs] = (d * (oh * gnv) * s * (1.0 + og * (1.0 - s))).astype(dog_ref.dtype)
            doh = d_on * gnv
            do_ref[:, cs] = (r * (doh - oh * jnp.mean(doh * oh, axis=-1, keepdims=True))).astype(do_ref.dtype)
            part = jnp.sum(d_on * oh, axis=0, keepdims=True)

            @pl.when(i == 0)
            def _():
                dgn_ref[:, cs] = part

            @pl.when(i > 0)
            def _():
                dgn_ref[:, cs] += part

    row = pl.BlockSpec((tm, w), lambda i: (i, 0))
    vec = pl.BlockSpec((1, w), lambda i: (0, 0))
    return pl.pallas_call(
        body, name=name, grid=(l // tm,),
        in_specs=[row, pl.BlockSpec((tm, w), lambda i: (i, og_col)), vec, pl.BlockSpec((tm, w), lambda i: (i, dcol))],
        out_specs=[row, row, vec], out_shape=[_sds((l, w), BF), _sds((l, w), BF), _sds((1, w), F32)],
        compiler_params=_params(("arbitrary",)),
    )(o_sum, og_arr, gn, dout)


def _rot_tables(l, dk):
    half = dk // 2
    pos = jnp.arange(l, dtype=F32)
    inv = jnp.exp(-math.log(ROPE_BASE) * jnp.arange(half, dtype=F32) / half)
    ang = pos[:, None] * inv[None, :]
    cos, sin = jnp.cos(ang), jnp.sin(ang)
    return jnp.concatenate([cos, cos], axis=-1), jnp.concatenate([-sin, sin], axis=-1)


def _rot_apply(name, src_q, qcol, src_k, kcol, cos_t, sin_t, heads, dk, kscale, out_dtype, transpose):
    l = src_q.shape[0]
    w = heads * dk
    tm = _tile(l, 1024, SUBLANE)

    def rot(t, cos_v, sin_v):
        if transpose:
            return t * cos_v + pltpu.roll(t * sin_v, dk // 2, 1)
        return t * cos_v + pltpu.roll(t, dk // 2, 1) * sin_v

    def body(q_ref, k_ref, c_ref, s_ref, qo_ref, ko_ref):
        cos_v, sin_v = c_ref[...], s_ref[...]
        for h in range(heads):
            cs = slice(h * dk, (h + 1) * dk)
            qo_ref[:, cs] = rot(q_ref[:, cs].astype(F32), cos_v, sin_v).astype(out_dtype)
            ko_ref[:, cs] = (rot(k_ref[:, cs].astype(F32), cos_v, sin_v) * kscale).astype(out_dtype)

    tab = pl.BlockSpec((tm, dk), lambda i: (i, 0))
    row = pl.BlockSpec((tm, w), lambda i: (i, 0))
    return pl.pallas_call(
        body, name=name, grid=(l // tm,),
        in_specs=[pl.BlockSpec((tm, w), lambda i: (i, qcol)), pl.BlockSpec((tm, w), lambda i: (i, kcol)), tab, tab],
        out_specs=[row, row], out_shape=[_sds((l, w), out_dtype), _sds((l, w), out_dtype)],
        compiler_params=_params(("arbitrary",)),
    )(src_q, src_k, cos_t, sin_t)


def _peer_copies(src_ref, out_ref, send_sems, recv_sems, gather):
    x, y, c = lax.axis_index("x"), lax.axis_index("y"), lax.axis_index("c")
    me = 4 * x + 2 * y + c
    copies = []
    for kk in range(1, N_DEV):
        px = (1 - x) if kk & 4 else x
        py = (1 - y) if kk & 2 else y
        pc = (1 - c) if kk & 1 else c
        peer = 4 * px + 2 * py + pc
        copies.append(pltpu.make_async_remote_copy(
            src_ref=src_ref if gather else src_ref.at[peer], dst_ref=out_ref.at[me],
            send_sem=send_sems.at[kk - 1], recv_sem=recv_sems.at[kk - 1],
            device_id=(px, py, pc), device_id_type=pl.DeviceIdType.MESH))
    return copies


_HBM = pl.BlockSpec(memory_space=pltpu.HBM)
_SEM = pl.BlockSpec(memory_space=pltpu.SEMAPHORE)
_EFFECT = pltpu.SideEffectType.DATAFLOW_SIDE_EFFECTING


def _exchange_start(name, srcs, gather):
    n = len(srcs)
    lands = [lax.empty((N_DEV,) + tuple(s.shape if gather else s.shape[1:]), s.dtype) for s in srcs]

    def body(*refs):
        src_refs, land_refs = refs[:n], refs[n:2 * n]
        send, recv = refs[2 * n:3 * n], refs[3 * n:4 * n]
        token = refs[-1]
        for k in range(n):
            for cp in _peer_copies(src_refs[k], land_refs[k], send[k], recv[k], gather):
                cp.start()
        token[...] = jnp.zeros_like(token)

    sem = pltpu.SemaphoreType.DMA((N_DEV - 1,))
    outs = pl.pallas_call(
        body, name=name,
        out_shape=tuple([sem] * (2 * n) + [pltpu.HBM(s.shape, s.dtype) for s in srcs] + [pltpu.HBM(a.shape, a.dtype) for a in lands]
                        + [_sds((SUBLANE, LANE), F32)]),
        in_specs=tuple([_HBM] * (2 * n)), out_specs=tuple([_SEM] * (2 * n) + [_HBM] * (2 * n) + [pl.BlockSpec(memory_space=pltpu.VMEM)]),
        input_output_aliases={k: 2 * n + k for k in range(2 * n)},
        compiler_params=pltpu.CompilerParams(has_side_effects=_EFFECT),
    )(*[pltpu.with_memory_space_constraint(a, pltpu.HBM) for a in list(srcs) + lands])
    return [(outs[k], outs[n + k], outs[2 * n + k], outs[3 * n + k], outs[-1]) for k in range(n)]


def _exchange_wait(name, started, gather, after):
    send_sems, recv_sems, src_thru, land_thru, _ = started

    def body(src_ref, land_ref, send_sems, recv_sems, after_ref, src_out, land_out):
        copies = _peer_copies(src_ref, land_ref, send_sems, recv_sems, gather)
        for cp in copies:
            cp.wait_send()
        for cp in copies:
            cp.wait_recv()

    return pl.pallas_call(
        body, name=name,
        out_shape=(pltpu.HBM(src_thru.shape, src_thru.dtype), pltpu.HBM(land_thru.shape, land_thru.dtype)),
        in_specs=(_HBM, _HBM, _SEM, _SEM, pl.BlockSpec(memory_space=pl.ANY)), out_specs=(_HBM, _HBM),
        input_output_aliases={0: 0, 1: 1},
        compiler_params=pltpu.CompilerParams(has_side_effects=_EFFECT),
    )(src_thru, land_thru, send_sems, recv_sems, after)


def _adam_math(w, gsum, m, v):
    m2 = ADAM_B1 * m + (1.0 - ADAM_B1) * gsum
    v2 = ADAM_B2 * v + (1.0 - ADAM_B2) * (gsum * gsum)
    m_hat = m2 / (1.0 - ADAM_B1 ** ADAM_STEP)
    v_hat = v2 / (1.0 - ADAM_B2 ** ADAM_STEP)
    delta = -ADAM_LR * (m_hat / (jnp.sqrt(v_hat) + ADAM_EPS) + ADAM_WD * w)
    return delta, m2, v2


def _reduce_adam(name, parts, w, m, v):
    nl, r, c = w.shape
    tr = _tile(r, 256, 16)
    nr = r // tr

    def body(*refs):
        p_refs = refs[:nl]
        w_ref, m_ref, v_ref, g_ref, d_ref, m2_ref, v2_ref = refs[nl:]
        for li in range(nl):
            @pl.when(pl.program_id(0) == li)
            def _(p_ref=p_refs[li]):
                gsum = p_ref[0].astype(F32)
                for s in range(1, N_DEV):
                    gsum = gsum + p_ref[s].astype(F32)
                g_ref[...] = gsum
                delta, m2, v2 = _adam_math(w_ref[...], gsum, m_ref[...], v_ref[...])
                d_ref[...] = delta
                m2_ref[...] = m2
                v2_ref[...] = v2

    def part_spec(li):
        return pl.BlockSpec((N_DEV, tr, c), lambda lay, i: (0, jnp.where(lay == li, i, jnp.where(lay < li, 0, nr - 1)), 0))

    row = pl.BlockSpec((None, tr, c), lambda lay, i: (lay, i, 0))
    return pl.pallas_call(
        body, name=name, grid=(nl, nr),
        in_specs=[part_spec(li) for li in range(nl)] + [row, row, row],
        out_specs=[row, row, row, row], out_shape=[_sds((nl, r, c), F32)] * 4,
        compiler_params=_params(("arbitrary", "arbitrary")),
    )(*parts, w, m, v)


def _reduce8(name, parts):
    _, r, c = parts.shape

    def body(p_ref, g_ref):
        gsum = p_ref[0]
        for s in range(1, N_DEV):
            gsum = gsum + p_ref[s]
        g_ref[...] = gsum

    return pl.pallas_call(
        body, name=name, grid=(1,),
        in_specs=[pl.BlockSpec((N_DEV, r, c), lambda i: (0, 0, 0))],
        out_specs=pl.BlockSpec((r, c), lambda i: (0, 0)), out_shape=_sds((r, c), F32),
        compiler_params=_params(("arbitrary",)),
    )(parts)


def _adam_packed(name, w, g, m, v):
    r, c = w.shape

    def body(w_ref, g_ref, m_ref, v_ref, d_ref, m2_ref, v2_ref):
        delta, m2, v2 = _adam_math(w_ref[...], g_ref[...], m_ref[...], v_ref[...])
        d_ref[...] = delta
        m2_ref[...] = m2
        v2_ref[...] = v2

    spec = pl.BlockSpec((r, c), lambda i: (0, 0))
    return pl.pallas_call(
        body, name=name, grid=(1,), in_specs=[spec] * 4, out_specs=[spec] * 3, out_shape=[_sds((r, c), F32)] * 3,
        compiler_params=_params(("arbitrary",)),
    )(w, g, m, v)


def _pack(arrs):
    flat = jnp.concatenate([a.reshape(-1).astype(F32) for a in arrs])
    n = flat.shape[0]
    pad = (-n) % (SUBLANE * LANE)
    return jnp.pad(flat, (0, pad)).reshape(-1, LANE)


def _unpack(packed, like):
    flat = packed.reshape(-1)
    out, off = [], 0
    for a in like:
        n = math.prod(a.shape)
        out.append(flat[off:off + n].reshape(a.shape))
        off += n
    return out


def _row_blocks(full):
    return full.reshape(N_DEV, full.shape[0] // N_DEV, full.shape[1])


def _col_blocks(full):
    r, c = full.shape
    return full.reshape(r, N_DEV, c // N_DEV).transpose(1, 0, 2)


def kernel(x, ffn1_norm, ffn1_w1, ffn1_w2, mix_norm, ffn2_norm, ffn2_w1, ffn2_w2, ab_w_in, s5_lambda_re, s5_lambda_im, s5_b_re, s5_b_im, s5_c_re, s5_c_im, s5_log_dt, s5_d, s5_w_glu, gla_w_gk, gla_b_gk, gla_norm, ab_w_out, ret_w_in, ret_norm, ret_w_out, final_norm, loss_target, m_ffn1_norm, m_ffn1_w1, m_ffn1_w2, m_mix_norm, m_ffn2_norm, m_ffn2_w1, m_ffn2_w2, m_ab_w_in, m_s5_lambda_re, m_s5_lambda_im, m_s5_b_re, m_s5_b_im, m_s5_c_re, m_s5_c_im, m_s5_log_dt, m_s5_d, m_s5_w_glu, m_gla_w_gk, m_gla_b_gk, m_gla_norm, m_ab_w_out, m_ret_w_in, m_ret_norm, m_ret_w_out, m_final_norm, v_ffn1_norm, v_ffn1_w1, v_ffn1_w2, v_mix_norm, v_ffn2_norm, v_ffn2_w1, v_ffn2_w2, v_ab_w_in, v_s5_lambda_re, v_s5_lambda_im, v_s5_b_re, v_s5_b_im, v_s5_c_re, v_s5_c_im, v_s5_log_dt, v_s5_d, v_s5_w_glu, v_gla_w_gk, v_gla_b_gk, v_gla_norm, v_ab_w_out, v_ret_w_in, v_ret_norm, v_ret_w_out, v_final_norm):
    names = ['ffn1_norm', 'ffn1_w1', 'ffn1_w2', 'mix_norm', 'ffn2_norm', 'ffn2_w1', 'ffn2_w2', 'ab_w_in', 's5_lambda_re', 's5_lambda_im', 's5_b_re', 's5_b_im', 's5_c_re', 's5_c_im', 's5_log_dt', 's5_d', 's5_w_glu', 'gla_w_gk', 'gla_b_gk', 'gla_norm', 'ab_w_out', 'ret_w_in', 'ret_norm', 'ret_w_out', 'final_norm']
    loc = locals()
    W = {n: loc[n] for n in names}
    M = {n: loc["m_" + n] for n in names}
    V = {n: loc["v_" + n] for n in names}

    me = 4 * lax.axis_index("x") + 2 * lax.axis_index("y") + lax.axis_index("c")
    xs = x[0]
    tgt = loss_target[0]
    l, d = xs.shape
    depth = ffn1_norm.shape[0]

    pending, to_start = {}, []

    def start_gather(tag, shard):
        to_start.append((tag, shard))

    def finish_gather(tag, after):
        started, shard = pending.pop(tag)
        _, got = _exchange_wait("agw_" + tag, started, True, after)
        return lax.dynamic_update_index_in_dim(got, shard, me, 0)

    def finish_cols(tag, after):
        g = finish_gather(tag, after)
        return g.transpose(1, 0, 2).reshape(g.shape[1], -1)

    def finish_rows(tag, after):
        g = finish_gather(tag, after)
        return g.reshape(-1, g.shape[2])

    small_sharded = [gla_w_gk, gla_b_gk, ret_norm]
    for i in range(depth):
        j = i // 2
        start_gather(f"ffn1_w1_{i}", ffn1_w1[i].astype(BF))
        start_gather(f"ffn1_w2_{i}", ffn1_w2[i].astype(BF))
        if i % 2 == 0:
            start_gather(f"ab_w_in_{j}", ab_w_in[j].astype(BF))
            if i == 0:
                start_gather("small", _pack(small_sharded))
            start_gather(f"s5_w_glu_{j}", s5_w_glu[j].astype(BF))
            start_gather(f"ab_w_out_{j}", ab_w_out[j].astype(BF))
        else:
            start_gather(f"ret_w_in_{j}", ret_w_in[j].astype(BF))
            start_gather(f"ret_w_out_{j}", ret_w_out[j].astype(BF))
        start_gather(f"ffn2_w1_{i}", ffn2_w1[i].astype(BF))
        start_gather(f"ffn2_w2_{i}", ffn2_w2[i].astype(BF))
    for (tag, shard), started in zip(to_start, _exchange_start("ags_weights", [s_ for _, s_ in to_start], True)):
        pending[tag] = (started, shard)
    started_all = started[4][0, 0]
    full = {}

    s5w = s5_d.shape[1]
    g_s5, n_s5 = s5_lambda_re.shape[2], s5_lambda_re.shape[3]
    hs = min(SUBLANE * LANE, g_s5 * n_s5)
    gla_hk = gla_w_gk.shape[-1] * N_DEV
    gla_dk = gla_hk // GLA_HEADS
    gla_hv = gla_norm.shape[1]
    gla_dv = gla_hv // GLA_HEADS
    ret_hv = ret_norm.shape[1] * N_DEV
    ret_dv = ret_hv // RET_HEADS
    ret_hk = (ret_w_in.shape[2] * N_DEV - 2 * ret_hv) // 2
    ret_dk = ret_hk // RET_HEADS
    assert s5w == gla_hv and 2 * gla_hk == s5w, "column blocks of the mixer projection assume these widths"
    assert ret_hv == 2 * ret_hk
    main_w = s5w + 2 * gla_hk + 2 * gla_hv
    gla_tb = _tile(l, 256, GLA_CHUNK)
    ret_chunk = min(RET_CHUNK, l)

    cos_t, sin_t = _rot_tables(l, ret_dk)
    lg_f = jnp.log1p(-jnp.exp2(-5.0 - jnp.arange(RET_HEADS, dtype=F32)))
    lgtab_f = jnp.broadcast_to(lg_f[:, None, None], (RET_HEADS, 1, LANE))
    lgtab_b = jnp.broadcast_to(lg_f[::-1][:, None, None], (RET_HEADS, 1, LANE))
    s5_pre = {}
    for j in range((depth + 1) // 2):
        s5_args = (s5_lambda_re[j], s5_lambda_im[j], s5_b_re[j], s5_b_im[j], s5_c_re[j], s5_c_im[j], s5_log_dt[j])
        (a_tab, bd, cd), s5_vjp = jax.vjp(lambda *a: _s5_chunk_tables(*a, hs), *s5_args)
        s5_pre[j] = (a_tab, bd.astype(BF), cd.astype(BF), s5_vjp)
    tables_done = jnp.stack([cos_t[0, 0], sin_t[0, 0], lgtab_f[0, 0, 0], lgtab_b[0, 0, 0]]
                            + [t[0].reshape(-1)[0] + t[1].reshape(-1)[0].astype(F32) + t[2].reshape(-1)[0].astype(F32) for t in s5_pre.values()])

    saved = []
    cur = xs
    hn = _rms_fwd("l0_ffn1_norm", cur, ffn1_norm[0:1] + started_all)
    for i in range(depth):
        j = i // 2
        s = {}
        s['x0'] = cur

        def first_w1(after):
            if i == 0:
                after = jnp.concatenate([after[0, 0:1].astype(F32), tables_done])
            return finish_cols(f"ffn1_w1_{i}", after)

        cur, h, s['ffn1'], s['f1w1'], s['f1w2'] = _ffn_fwd(f"l{i}_ffn1", cur, hn, first_w1, lambda after: finish_rows(f"ffn1_w2_{i}", after),
                                                            mix_norm[i:i + 1])
        s['x1'] = cur
        s['h'] = h
        if i % 2 == 0:
            w_in = finish_cols(f"ab_w_in_{j}", cur)
            if i == 0:
                got = finish_gather("small", cur)
                flat, off, joined = got.reshape(N_DEV, -1), 0, []
                for a in small_sharded:
                    n = math.prod(a.shape)
                    blk = jnp.moveaxis(flat[:, off:off + n].reshape((N_DEV,) + a.shape), 0, -2)
                    joined.append(blk.reshape(a.shape[:-1] + (N_DEV * a.shape[-1],)))
                    off += n
                full['gla_w_gk'], full['gla_b_gk'], ret_norm_full = joined[0].astype(BF), joined[1], joined[2]
            s['w_glu'], s['w_out'] = finish_rows(f"s5_w_glu_{j}", cur), finish_rows(f"ab_w_out_{j}", cur)
            w_main, w_glo = w_in[:, :main_w], w_in[:, main_w:]
            proj = _mm_plain(f"l{i}_proj", h, w_main, NN, BF, tm=1024, tn=1024, tk=d, b_outer=True)
            glo = _mm_plain(f"l{i}_glo", h, w_glo, NN, F32, tm=1024, tn=2 * GLA_RANK, tk=d)
            a_tab, bd16, cd16, s5_vjp = s5_pre[j]
            tm = _tile(l, 512, SUBLANE)
            x_f, y_f = _s5_fwd(f"l{i}_s5_fwd_f", proj, bd16[0], cd16[0], a_tab[0], False)
            x_b, y_b = _s5_fwd(f"l{i}_s5_fwd_b", proj, bd16[1], cd16[1], a_tab[1], True)
            d_row = s5_d[j:j + 1]
            y, s5_out = _glu_fwd(f"l{i}_s5_glu", y_f, y_b, proj, d_row, s['w_glu'])
            zeros_r = jnp.zeros((GLA_RANK, gla_hk), BF)
            w_gk = full['gla_w_gk'][j]
            wgk_f = jnp.concatenate([w_gk[0], zeros_r], axis=0)
            wgk_b = jnp.concatenate([zeros_r, w_gk[1]], axis=0)
            b_gk = full['gla_b_gk'][j]
            g_f, g_b = _gate_fwd(f"l{i}_gla_gate", glo, wgk_f, wgk_b, b_gk[0:1], b_gk[1:2])
            qcol, kcol, vcol, ogcol = s5w // gla_hk, s5w // gla_hk + 1, (s5w + 2 * gla_hk) // gla_hv, (s5w + 2 * gla_hk) // gla_hv + 1
            lin_kw = dict(heads=GLA_HEADS, dk=gla_dk, dv=gla_dv, chunk=GLA_CHUNK, tb=gla_tb, qcol=qcol, kcol=kcol, vcol=vcol,
                          qscale=gla_dk ** -0.5)
            o_f, sp_f = _gla_fwd(f"l{i}_gla_fwd_f", proj, g_f, reverse=False, **lin_kw)
            o_b, sp_b = _gla_fwd(f"l{i}_gla_fwd_b", proj, g_b, o_f, reverse=True, **lin_kw)
            gla_out = _headgate_fwd(f"l{i}_gla_out", o_b, proj, ogcol, gla_norm[j:j + 1], gla_dv)
            w_out = s['w_out']

            row = pl.BlockSpec((tm, d), lambda ii, jj, kk: (ii, 0))
            vec = pl.BlockSpec((1, d), lambda ii, jj, kk: (0, 0))
            cur, hn = _mm(f"l{i}_mix_out",
                          [(s5_out, pl.BlockSpec((tm, s5w), lambda ii, jj, kk: (ii, 0)), w_out, pl.BlockSpec((s5w, d), lambda ii, jj, kk: (0, 0))),
                           (gla_out, pl.BlockSpec((tm, gla_hv), lambda ii, jj, kk: (ii, 0)), w_out, pl.BlockSpec((gla_hv, d), lambda ii, jj, kk: (1, 0)))],
                          NN, (l // tm, 1, 1), [_sds((l, d), F32), _sds((l, d), BF)], [row, row], (tm, d), _residual_epi(1.0, True),
                          [cur, ffn2_norm[i:i + 1]], [row, vec])
            s.update(proj=proj, glo=glo, s5_vjp=s5_vjp, a_tab=a_tab, bd16=bd16, cd16=cd16, x_f=x_f, x_b=x_b, y=y, s5_out=s5_out,
                     wgk_f=wgk_f, wgk_b=wgk_b, b_gk=b_gk, g_f=g_f, g_b=g_b, o_f=o_f, o_b=o_b, sp_f=sp_f, sp_b=sp_b, gla_out=gla_out,
                     w_main=w_main, w_glo=w_glo, lin_kw=lin_kw, ogcol=ogcol)
        else:
            w_in = finish_cols(f"ret_w_in_{j}", cur)
            s['w_in'], s['w_out'] = w_in, finish_rows(f"ret_w_out_{j}", cur)
            proj = _mm_plain(f"l{i}_proj", h, w_in, NN, BF, tm=1024, tn=1024, tk=d, b_outer=True)
            qr, kr = _rot_apply(f"l{i}_rot", proj, 0, proj, 1, cos_t, sin_t, RET_HEADS, ret_dk, ret_dk ** -0.5, BF, False)
            ret_hb = 4
            lin_kw = dict(heads=RET_HEADS, hb=ret_hb, dk=ret_dk, dv=ret_dv, chunk=ret_chunk, tb=_tile(l, 4 * ret_chunk, ret_chunk), qcol=0, kcol=0,
                          vcol=(2 * ret_hk) // (ret_hb * ret_dv))
            o_f, sp_f = _lin_fwd(f"l{i}_ret_fwd_f", qr, kr, proj, lgtab_f, reverse=False, **lin_kw)
            o_b, sp_b = _lin_fwd(f"l{i}_ret_fwd_b", qr, kr, proj, lgtab_b, o_f, reverse=True, **lin_kw)
            ogcol = (2 * ret_hk + ret_hv) // ret_hv
            r_out = _headgate_fwd(f"l{i}_ret_out", o_b, proj, ogcol, ret_norm_full, ret_dv)

            tm = _tile(l, 512, SUBLANE)
            cur, hn = _mm_plain(f"l{i}_mix_out", r_out, s['w_out'], NN, F32, tm=512, tn=d, tk=ret_hv, epi=_residual_epi(1.0, True),
                                eins=[cur, ffn2_norm[i:i + 1]], especs=[None, pl.BlockSpec((1, d), lambda ii, jj, kk: (0, 0))],
                                extra_outs=[_sds((l, d), BF)], extra_specs=[pl.BlockSpec((tm, d), lambda ii, jj, kk: (ii, 0))])
            s.update(proj=proj, qr=qr, kr=kr, o_f=o_f, o_b=o_b, sp_f=sp_f, sp_b=sp_b, r_out=r_out, lin_kw=lin_kw, ogcol=ogcol)
        s['x2'] = cur
        cur, hn, s['ffn2'], s['f2w1'], s['f2w2'] = _ffn_fwd(f"l{i}_ffn2", cur, hn, lambda after: finish_cols(f"ffn2_w1_{i}", after),
                                                             lambda after: finish_rows(f"ffn2_w2_{i}", after),
                                                             ffn1_norm[i + 1:i + 2] if i + 1 < depth else None)
        saved.append(s)

    dx, d_final_norm, loss_row = _loss_head("loss_head", cur, final_norm.reshape(1, -1), tgt)
    loss = lax.psum(loss_row[0, 0], ("x", "y", "c"))

    G = {}
    big = {}
    G['final_norm'] = d_final_norm.reshape(-1)
    per_layer = {n: [None] * depth for n in ['ffn1_norm', 'mix_norm', 'ffn2_norm']}
    small_late = ['ffn1_norm', 'mix_norm']
    small_early = ['ffn2_norm', 's5_lambda_re', 's5_lambda_im', 's5_b_re', 's5_b_im', 's5_c_re', 's5_c_im',
                   's5_log_dt', 's5_d', 'gla_w_gk', 'gla_b_gk', 'gla_norm', 'ret_norm', 'final_norm']
    a2a, tok = {}, [jnp.zeros((), F32)]

    def start_a2a(*tagged):
        for (tag, _), started in zip(tagged, _exchange_start("a2as_" + tagged[0][0], [b for _, b in tagged], False)):
            a2a[tag] = started
        tok[0] = tok[0] + started[4][0, 0]

    def dep(vec):
        return vec + tok[0]

    def proj_backward(tag, pieces, s, dres, a2a_tag):
        tm = _tile(l, 512, SUBLANE)
        row = pl.BlockSpec((tm, d), lambda ii, jj, kk: (ii, 0))
        vec = pl.BlockSpec((1, d), lambda ii, jj, kk: (0, 0))
        dws = [_mm_plain(f"{tag}_dwin_{k}", s['h'], piece, TN, BF, tm=d, tn=2048, tk=2048) for k, (piece, _, _) in enumerate(pieces)]
        start_a2a((a2a_tag, _col_blocks(jnp.concatenate(dws, axis=1))))
        pairs = []
        for piece, w, col in pieces:
            wd = piece.shape[1]
            pairs.append((piece, pl.BlockSpec((tm, wd), lambda ii, jj, kk: (ii, 0)),
                          w, pl.BlockSpec((d, wd), lambda ii, jj, kk, col=col: (0, col), pipeline_mode=pl.Buffered(1))))
        return _mm(f"{tag}_dh", pairs, NT, (l // tm, 1, 1), [_sds((l, d), F32), _sds((1, d), F32)], [row, vec], (tm, d),
                   _rms_bwd_epi(0), [s['x1'], dep(mix_norm[i:i + 1]), dres], [row, vec, row])

    for i in reversed(range(depth)):
        j = i // 2
        s = saved[i]
        def ffn_grads(which):
            def on_grads(dw1, dw2, gnorm):
                start_a2a((f"{which}_w1_{i}", _col_blocks(dw1)), (f"{which}_w2_{i}", _row_blocks(dw2)))
                return dep(gnorm)
            return on_grads

        dx, dg = _ffn_bwd(f"l{i}_ffn2b", dx, s['x2'], ffn2_norm[i:i + 1], s['f2w1'], s['f2w2'], s['ffn2'], ffn_grads("ffn2"))
        per_layer['ffn2_norm'][i] = dg[0]
        tm = _tile(l, 512, SUBLANE)
        row = pl.BlockSpec((tm, d), lambda ii, jj, kk: (ii, 0))
        vec = pl.BlockSpec((1, d), lambda ii, jj, kk: (0, 0))
        if i % 2 == 0:
            proj, lin_kw = s['proj'], s['lin_kw']
            w_out = s['w_out']
            d_cat = _mm_plain(f"l{i}_dcat", dx, w_out, NT, BF, tm=512, tn=1024, tk=d)
            dwo_a = _mm_plain(f"l{i}_dwout_a", s['s5_out'], dx, TN, BF, tm=s5w, tn=d, tk=2048)
            dwo_b = _mm_plain(f"l{i}_dwout_b", s['gla_out'], dx, TN, BF, tm=gla_hv, tn=d, tk=2048)
            start_a2a((f"ab_w_out_{j}", _row_blocks(jnp.concatenate([dwo_a, dwo_b], axis=0))))
            do, dog, dgn = _headgate_bwd(f"l{i}_gla_outb", s['o_b'], proj, s['ogcol'], dep(gla_norm[j:j + 1]), d_cat, 1, gla_dv)
            G['gla_norm'] = dgn
            dq, dk_, dv_, dgf = _gla_bwd(f"l{i}_gla_bwd_f", proj, s['g_f'], s['sp_f'], do, None, reverse=False, **lin_kw)
            dq, dk_, dv_, dgb = _gla_bwd(f"l{i}_gla_bwd_b", proj, s['g_b'], s['sp_b'], do, (dq, dk_, dv_), reverse=True, **lin_kw)
            dglo, dwf, dwb, dbf, dbb = _gate_bwd(f"l{i}_gla_gateb", s['glo'], s['wgk_f'], s['wgk_b'], s['b_gk'][0:1], s['b_gk'][1:2], dgf, dgb)
            G['gla_w_gk'] = jnp.stack([dwf[:GLA_RANK], dwb[GLA_RANK:]], axis=0)[None]
            G['gla_b_gk'] = jnp.concatenate([dbf, dbb], axis=0)[None]
            dy, dwglu = _glu_bwd(f"l{i}_s5_glub", s['y'], s['w_glu'], d_cat, 0)
            start_a2a((f"s5_w_glu_{j}", _row_blocks(dwglu.astype(BF))))
            cd16, bd16, a_tab = s['cd16'], s['bd16'], s['a_tab']
            nt2 = a_tab.shape[2]
            a_conj = a_tab * jnp.where(jnp.arange(nt2) < nt2 // 2, 1.0, -1.0)[None, None, :, None]
            du_f, dbd_f, dcd_f, da_f = _s5_bwd(f"l{i}_s5_bwd_f", proj, dy, s['x_f'], bd16[0], cd16[0], a_conj[0], True)
            du_b, dbd_b, dcd_b, da_b = _s5_bwd(f"l{i}_s5_bwd_b", proj, dy, s['x_b'], bd16[1], cd16[1], a_conj[1], False)
            du, dd = _s5_du(f"l{i}_s5_du", du_f, du_b, dy, proj, s5_d[j:j + 1])
            G['s5_d'] = dd
            cot = (jnp.stack([da_f, da_b]), jnp.stack([dbd_f, dbd_b]), jnp.stack([dcd_f, dcd_b]))
            g_lre, g_lim, g_bre, g_bim, g_cre, g_cim, g_ldt = s['s5_vjp'](cot)
            G['s5_lambda_re'], G['s5_lambda_im'], G['s5_b_re'], G['s5_b_im'] = g_lre[None], g_lim[None], g_bre[None], g_bim[None]
            G['s5_c_re'], G['s5_c_im'], G['s5_log_dt'] = g_cre[None], g_cim[None], g_ldt[None]
            if i == 0:
                G['ffn2_norm'] = jnp.stack(per_layer['ffn2_norm'], axis=0)
                early_packed = _pack([G[n] for n in small_early])
                early_started = _exchange_start("ags_small_grads_early", [early_packed], True)[0]
                tok[0] = tok[0] + early_started[4][0, 0]
            w_main, w_glo = s['w_main'], s['w_glo']
            pieces = [(du, w_main, 0), (dq, w_main, s5w // gla_hk), (dk_, w_main, s5w // gla_hk + 1),
                      (dv_, w_main, (s5w + 2 * gla_hk) // gla_hv), (dog, w_main, (s5w + 2 * gla_hk) // gla_hv + 1), (dglo, w_glo, 0)]
            dx, dg = proj_backward(f"l{i}", pieces, s, dx, f"ab_w_in_{j}")
        else:
            proj, lin_kw = s['proj'], s['lin_kw']
            w_out = s['w_out']
            d_ro = _mm_plain(f"l{i}_dro", dx, w_out, NT, BF, tm=1024, tn=1024, tk=d, b_outer=True)
            dwo = _mm_plain(f"l{i}_dwout", s['r_out'], dx, TN, BF, tm=1024, tn=d, tk=2048)
            start_a2a((f"ret_w_out_{j}", _row_blocks(dwo)))
            do, dog, dgn = _headgate_bwd(f"l{i}_ret_outb", s['o_b'], proj, s['ogcol'], dep(ret_norm_full), d_ro, 0, ret_dv)
            G['ret_norm'] = dgn
            r1 = _lin_bwd(f"l{i}_ret_bwd_f", s['qr'], s['kr'], proj, lgtab_f, s['sp_f'], do, None, reverse=False, **lin_kw)
            r2 = _lin_bwd(f"l{i}_ret_bwd_b", s['qr'], s['kr'], proj, lgtab_b, s['sp_b'], do, r1, reverse=True, **lin_kw)
            dqr, dkr, dv_ = r2
            dq, dk_ = _rot_apply(f"l{i}_rotb", dqr, 0, dkr, 0, cos_t, sin_t, RET_HEADS, ret_dk, ret_dk ** -0.5, BF, True)
            w_in = s['w_in']
            pieces = [(dq, w_in, 0), (dk_, w_in, 1), (dv_, w_in, (2 * ret_hk) // ret_hv), (dog, w_in, (2 * ret_hk) // ret_hv + 1)]
            dx, dg = proj_backward(f"l{i}", pieces, s, dx, f"ret_w_in_{j}")
        per_layer['mix_norm'][i] = dg[0]
        dx, dg = _ffn_bwd(f"l{i}_ffn1b", dx, s['x0'], ffn1_norm[i:i + 1], s['f1w1'], s['f1w2'], s['ffn1'], ffn_grads("ffn1"))
        per_layer['ffn1_norm'][i] = dg[0]
    for n in small_late:
        G[n] = jnp.stack(per_layer[n], axis=0)
    grad_x = dx[None]

    out_g, out_d, out_m, out_v = {}, {}, {}, {}
    small = small_early + small_late
    packed = _pack([G[n] for n in small_late])
    small_started = _exchange_start("ags_small_grads_late", [packed], True)[0]

    def big_update(n, layers):
        parts = []
        for i in layers:
            blocks, got = _exchange_wait(f"a2aw_{n}_{i}", a2a.pop(f"{n}_{i}"), False, small_started[4])
            parts.append(lax.dynamic_update_index_in_dim(got, lax.dynamic_index_in_dim(blocks, me, 0, keepdims=False), me, 0))
        out_g[n], out_d[n], out_m[n], out_v[n] = _reduce_adam("upd_" + n, parts, W[n], M[n], V[n])

    for n in ['ffn2_w1', 'ffn2_w2', 'ffn1_w1', 'ffn1_w2']:
        big_update(n, range(depth))
    for n in ['ab_w_in', 's5_w_glu', 'ab_w_out', 'ret_w_in', 'ret_w_out']:
        big_update(n, [0])
    assert not a2a and not pending

    g_full = {}
    for tag, started, mine, group in (("early", early_started, early_packed, small_early), ("late", small_started, packed, small_late)):
        _, gathered = _exchange_wait("agw_small_grads_" + tag, started, True, out_v['ret_w_out'])
        gathered = lax.dynamic_update_index_in_dim(gathered, mine, me, 0)
        summed = _reduce8("sum_small_grads_" + tag, gathered)
        g_full.update(zip(group, _unpack(summed, [G[n] for n in group])))
    g_small = {}
    for n in small:
        gf = g_full[n]
        if n in ('gla_w_gk', 'gla_b_gk', 'ret_norm'):
            width = W[n].shape[-1]
            gf = lax.dynamic_slice_in_dim(gf, me * width, width, axis=gf.ndim - 1)
        g_small[n] = gf.reshape(W[n].shape)
    pw, pg, pm, pv = (_pack([src[n] for n in small]) for src in (W, g_small, M, V))
    pd, pm2, pv2 = _adam_packed("upd_small", pw, pg, pm, pv)
    like = [W[n] for n in small]
    for n, dd_, mm_, vv_ in zip(small, _unpack(pd, like), _unpack(pm2, like), _unpack(pv2, like)):
        out_g[n], out_d[n], out_m[n], out_v[n] = g_small[n], dd_, mm_, vv_

    return (loss, grad_x, *[out_g[n] for n in names], *[out_d[n] for n in names], *[out_m[n] for n in names], *[out_v[n] for n in names])
```
